```python
import jax, jax.numpy as jnp
from jax import lax
import numpy as np

D_MODEL = 1024
BATCH = 2
SEQ = 8192
DEPTH = 1
DEC_BATCH = 128
DEC_SEQ = 8
PAST_LEN = 2048
PAGE_SIZE = 128

HEAD_DIM = 64
NSA_HEADS = 8
NSA_KV_HEADS = 2
NSA_GROUP = NSA_HEADS // NSA_KV_HEADS
CMP_BLOCK = 64
SEL_BLOCK = CMP_BLOCK
SEL_TOPK = 16
WINDOW = 512
FOX_HEADS = 8
Q_BLOCK = 128
PEER_HEADS = 8
PEER_N_KEYS = 128
PEER_N_EXPERTS = PEER_N_KEYS * PEER_N_KEYS
PEER_DK = 256
PEER_DK_HALF = PEER_DK // 2
PEER_TOPK = 16
PEER_CHUNK = 256

NSA_Q_W = NSA_HEADS * HEAD_DIM
NSA_KV_W = 6 * NSA_KV_HEADS * HEAD_DIM
NSA_GATE_W = 3 * NSA_HEADS
FOX_W = FOX_HEADS * HEAD_DIM
FOX_QKV_W = 3 * FOX_W
FOX_F_W = FOX_HEADS
MERGE_W = 2 * D_MODEL
SPLIT_Q_A = NSA_Q_W
SPLIT_KV_A = SPLIT_Q_A + NSA_KV_W
SPLIT_G_A = SPLIT_KV_A + NSA_GATE_W
SPLIT_QKV_B = SPLIT_G_A + FOX_QKV_W
SPLIT_F_B = SPLIT_QKV_B + FOX_F_W
IN_WIDTH = SPLIT_F_B + MERGE_W

SCALE = HEAD_DIM ** -0.5
FORCE_SCORE = float(NSA_GROUP + 1)
NEG_INF = -1e30
EPS = 1e-6

kernel_name = 'nsa_fox_peer_hybrid_step'


def rms_norm(x, g):
    xf = x.astype(jnp.float32)
    y = xf * lax.rsqrt(jnp.mean(xf * xf, axis=-1, keepdims=True) + EPS)
    return (y * g.astype(jnp.float32)).astype(x.dtype)


def masked_softmax(logits, mask):
    lf = jnp.where(mask, logits.astype(jnp.float32), NEG_INF)
    m = jnp.max(lf, axis=-1, keepdims=True)
    p = jnp.where(mask, jnp.exp(lf - m), 0.0)
    return p / jnp.maximum(jnp.sum(p, axis=-1, keepdims=True), 1e-30)


def alibi_slopes(n):
    return jnp.exp2(-8.0 * jnp.arange(1, n + 1, dtype=jnp.float32) / n)


def compress_blocks(rows, pe, w1, w2):
    b, l = rows.shape[:2]
    blk = rows.reshape(b, l // CMP_BLOCK, CMP_BLOCK, NSA_KV_HEADS, HEAD_DIM) + pe[:, None, :]
    hid = jax.nn.gelu(jnp.einsum('bnlgd,lde->bnge', blk, w1))
    return jnp.einsum('bnge,ef->bngf', hid, w2)


def _front(x, norm_attn, w_in, fox_f_bias, nsa_q_norm, nsa_k_norm, fox_q_norm, fox_k_norm):
    b, t, _ = x.shape
    h = rms_norm(x, norm_attn)
    z = h @ w_in
    q_a, kv_a, g_a, qkv_b, f_b, mg = jnp.split(z, [SPLIT_Q_A, SPLIT_KV_A, SPLIT_G_A, SPLIT_QKV_B, SPLIT_F_B], axis=-1)
    q_a = rms_norm(q_a.reshape(b, t, NSA_HEADS, HEAD_DIM), nsa_q_norm)
    kv_a = kv_a.reshape(b, t, 3, 2, NSA_KV_HEADS, HEAD_DIM)
    nsa_rows = jnp.stack([kv_a[:, :, 0, 0], kv_a[:, :, 0, 1],
                          rms_norm(kv_a[:, :, 1, 0], nsa_k_norm[1]), kv_a[:, :, 1, 1]], axis=2)
    win_rows = jnp.stack([rms_norm(kv_a[:, :, 2, 0], nsa_k_norm[2]), kv_a[:, :, 2, 1]], axis=2)
    gates_a = jax.nn.sigmoid(g_a.reshape(b, t, NSA_HEADS, 3))
    qkv_b = qkv_b.reshape(b, t, 3, FOX_HEADS, HEAD_DIM)
    q_b = rms_norm(qkv_b[:, :, 0], fox_q_norm)
    fox_rows = jnp.stack([rms_norm(qkv_b[:, :, 1], fox_k_norm), qkv_b[:, :, 2]], axis=2)
    logf = jax.nn.log_sigmoid(f_b.astype(jnp.float32) + fox_f_bias.astype(jnp.float32))
    merge_gates = jax.nn.sigmoid(mg.reshape(b, t, 2, D_MODEL))
    return q_a, gates_a, nsa_rows, win_rows, q_b, fox_rows, logf, merge_gates


def nsa_block(q, gates, qpos, kc, vc, cmp_end, ks, vs, kw, vw, wpos, slopes):
    b, nq = q.shape[:2]
    nb = kc.shape[1]
    qg = q.reshape(b, nq, NSA_KV_HEADS, NSA_GROUP, HEAD_DIM)
    m = slopes.reshape(NSA_KV_HEADS, NSA_GROUP)[:, :, None, None]
    dist_c = qpos[:, None] - cmp_end[None, :]
    lc = jnp.einsum('bqgrd,bngd->bgrqn', qg, kc) * SCALE - m * dist_c.astype(jnp.float32)
    pc = masked_softmax(lc, dist_c >= 0)
    o_c = jnp.einsum('bgrqn,bngd->bqgrd', pc, vc)
    blk = jnp.arange(nb)
    cur = (qpos // SEL_BLOCK)[:, None]
    forced = (blk[None, :] == cur) | (blk[None, :] == 0)
    allowed = blk[None, :] <= cur
    imp = jnp.where(forced, FORCE_SCORE, jnp.where(allowed, pc.sum(axis=2), -1.0))
    _, idx = lax.top_k(imp, min(SEL_TOPK, nb))
    kb = ks.reshape(b, nb, SEL_BLOCK, NSA_KV_HEADS, HEAD_DIM).transpose(0, 3, 1, 2, 4)
    vb = vs.reshape(b, nb, SEL_BLOCK, NSA_KV_HEADS, HEAD_DIM).transpose(0, 3, 1, 2, 4)
    take = jax.vmap(jax.vmap(lambda blocks, i: blocks[i]))
    k_sel = take(kb, idx).reshape(b, NSA_KV_HEADS, nq, -1, HEAD_DIM)
    v_sel = take(vb, idx).reshape(b, NSA_KV_HEADS, nq, -1, HEAD_DIM)
    spos = (idx[..., None] * SEL_BLOCK + jnp.arange(SEL_BLOCK)).reshape(b, NSA_KV_HEADS, nq, -1)
    dist_s = (qpos[:, None] - spos)[:, :, None]
    ls = jnp.einsum('bqgrd,bgqsd->bgrqs', qg, k_sel) * SCALE - m * dist_s.astype(jnp.float32)
    ps = masked_softmax(ls, dist_s >= 0)
    o_s = jnp.einsum('bgrqs,bgqsd->bqgrd', ps, v_sel)
    dist_w = qpos[:, None] - wpos[None, :]
    mask_w = (dist_w >= 0) & (dist_w < WINDOW) & (wpos[None, :] >= 0)
    lw = jnp.einsum('bqgrd,bsgd->bgrqs', qg, kw) * SCALE - m * dist_w.astype(jnp.float32)
    pw = masked_softmax(lw, mask_w)
    o_w = jnp.einsum('bgrqs,bsgd->bqgrd', pw, vw)
    g = gates.reshape(b, nq, NSA_KV_HEADS, NSA_GROUP, 3)
    o = g[..., 0:1] * o_c + g[..., 1:2] * o_s + g[..., 2:3] * o_w
    return o.reshape(b, nq, NSA_Q_W)


def _compressed_kv(rows, nsa_k_norm, cmp_pe, cmp_w1, cmp_w2):
    kc = rms_norm(compress_blocks(rows[:, :, 0], cmp_pe[0], cmp_w1[0], cmp_w2[0]), nsa_k_norm[0])
    vc = compress_blocks(rows[:, :, 1], cmp_pe[1], cmp_w1[1], cmp_w2[1])
    nb = rows.shape[1] // CMP_BLOCK
    cmp_end = jnp.arange(nb) * CMP_BLOCK + (CMP_BLOCK - 1)
    return kc, vc, cmp_end


def nsa_prompt(q, gates, nsa_rows, win_rows, slopes, nsa_k_norm, cmp_pe, cmp_w1, cmp_w2):
    b, t = q.shape[:2]
    kc, vc, cmp_end = _compressed_kv(nsa_rows, nsa_k_norm, cmp_pe, cmp_w1, cmp_w2)
    ks, vs = nsa_rows[:, :, 2], nsa_rows[:, :, 3]
    win_pad = jnp.pad(win_rows, ((0, 0), (WINDOW, 0), (0, 0), (0, 0), (0, 0)))

    def one(i):
        q0 = i * Q_BLOCK
        qb = lax.dynamic_slice_in_dim(q, q0, Q_BLOCK, axis=1)
        gb = lax.dynamic_slice_in_dim(gates, q0, Q_BLOCK, axis=1)
        wb = lax.dynamic_slice_in_dim(win_pad, q0, WINDOW + Q_BLOCK, axis=1)
        wpos = q0 - WINDOW + jnp.arange(WINDOW + Q_BLOCK)
        return nsa_block(qb, gb, q0 + jnp.arange(Q_BLOCK), kc, vc, cmp_end, ks, vs,
                         wb[:, :, 0], wb[:, :, 1], wpos, slopes)

    o = lax.map(one, jnp.arange(t // Q_BLOCK))
    return o.transpose(1, 0, 2, 3).reshape(b, t, NSA_Q_W)


def nsa_sample(q, gates, new_rows, new_win, cache_nsa, state_nsa_win, page_table, slopes,
               nsa_k_norm, cmp_pe, cmp_w1, cmp_w2):
    db, ns = q.shape[:2]
    past_len = page_table.shape[1] * cache_nsa.shape[1]
    past = cache_nsa[page_table].reshape(db, past_len, 4, NSA_KV_HEADS, HEAD_DIM)
    rows = jnp.concatenate([past, new_rows], axis=1)
    pad = (-(past_len + ns)) % SEL_BLOCK
    rows = jnp.pad(rows, ((0, 0), (0, pad), (0, 0), (0, 0), (0, 0)))
    kc, vc, cmp_end = _compressed_kv(rows, nsa_k_norm, cmp_pe, cmp_w1, cmp_w2)
    wbuf = state_nsa_win.shape[1]
    wrows = jnp.concatenate([state_nsa_win, new_win], axis=1)
    wpos = past_len - wbuf + jnp.arange(wbuf + ns)
    o = nsa_block(q, gates, past_len + jnp.arange(ns), kc, vc, cmp_end, rows[:, :, 2], rows[:, :, 3],
                  wrows[:, :, 0], wrows[:, :, 1], wpos, slopes)
    return o, wrows[:, ns:]


def fox_block(q, cq, qpos, k, v, ck, kpos):
    logits = jnp.einsum('bqhd,bkhd->bhqk', q, k) * SCALE + (cq[..., :, None] - ck[..., None, :])
    p = masked_softmax(logits, kpos[None, :] <= qpos[:, None])
    o = jnp.einsum('bhqk,bkhd->bqhd', p, v)
    return o.reshape(q.shape[0], q.shape[1], FOX_W)


def fox_prompt(q, fox_rows, logf):
    b, t = q.shape[:2]
    c = jnp.cumsum(logf.astype(jnp.float32), axis=1).transpose(0, 2, 1)
    k, v = fox_rows[:, :, 0], fox_rows[:, :, 1]
    kpos = jnp.arange(t)

    def one(i):
        q0 = i * Q_BLOCK
        qb = lax.dynamic_slice_in_dim(q, q0, Q_BLOCK, axis=1)
        cq = lax.dynamic_slice_in_dim(c, q0, Q_BLOCK, axis=2)
        return fox_block(qb, cq, q0 + jnp.arange(Q_BLOCK), k, v, c, kpos)

    o = lax.map(one, jnp.arange(t // Q_BLOCK))
    return o.transpose(1, 0, 2, 3).reshape(b, t, FOX_W)


def fox_sample(q, fox_rows, logf, cache_fox_kv, cache_fox_logf, page_table):
    db, ns = q.shape[:2]
    past_len = page_table.shape[1] * cache_fox_kv.shape[1]
    kv = jnp.concatenate([cache_fox_kv[page_table].reshape(db, past_len, 2, FOX_HEADS, HEAD_DIM), fox_rows], axis=1)
    lf = jnp.concatenate([cache_fox_logf[page_table].reshape(db, past_len, FOX_HEADS).astype(jnp.float32), logf], axis=1)
    c = jnp.cumsum(lf, axis=1).transpose(0, 2, 1)
    kpos = jnp.arange(past_len + ns)
    return fox_block(q, c[:, :, past_len:], past_len + jnp.arange(ns), kv[:, :, 0], kv[:, :, 1], c, kpos)


def peer_chunk(h, w_query, sub_keys, u_tab, v_tab):
    c = h.shape[0]
    q = (h @ w_query).reshape(c, PEER_HEADS, 2, PEER_DK_HALF)
    s = jnp.einsum('chpd,hpkd->chpk', q, sub_keys).astype(jnp.float32)
    s1, i1 = lax.top_k(s[:, :, 0], PEER_TOPK)
    s2, i2 = lax.top_k(s[:, :, 1], PEER_TOPK)
    cand = (s1[..., :, None] + s2[..., None, :]).reshape(c, PEER_HEADS, -1)
    cidx = (i1[..., :, None] * PEER_N_KEYS + i2[..., None, :]).reshape(c, PEER_HEADS, -1)
    top, pos = lax.top_k(cand, PEER_TOPK)
    eidx = jnp.take_along_axis(cidx, pos, axis=-1)
    g = jax.nn.softmax(top, axis=-1)
    act = jax.nn.gelu(jnp.einsum('cd,chkd->chk', h, u_tab[eidx]))
    return jnp.einsum('chk,chkd->cd', g * act, v_tab[eidx]).astype(h.dtype)


def peer_ffn(h, w_query, sub_keys, u_tab, v_tab):
    n = h.shape[0]
    pad = (-n) % PEER_CHUNK
    hp = jnp.pad(h, ((0, pad), (0, 0))).reshape(-1, PEER_CHUNK, D_MODEL)
    out = lax.map(lambda hc: peer_chunk(hc, w_query, sub_keys, u_tab, v_tab), hp)
    return out.reshape(-1, D_MODEL)[:n]


def _merge_and_ffn(x, o_nsa, o_fox, merge_gates, w_up_nsa, w_up_fox, w_out, norm_ffn,
                   peer_w_query, peer_sub_keys, peer_u, peer_v):
    mixed = merge_gates[:, :, 0] * (o_nsa @ w_up_nsa) + merge_gates[:, :, 1] * (o_fox @ w_up_fox)
    x1 = x + (mixed @ w_out).astype(x.dtype)
    h2 = rms_norm(x1, norm_ffn).reshape(-1, D_MODEL)
    return x1 + peer_ffn(h2, peer_w_query, peer_sub_keys, peer_u, peer_v).reshape(x.shape)


def setup_inputs(seed: int = 0) -> dict:
    key = jax.random.key(seed)
    ks = jax.random.split(key, 26)
    nrm = jax.random.normal
    n_pages = PAST_LEN // PAGE_SIZE
    n_used = DEC_BATCH * n_pages
    n_pool = n_used + n_used // 4
    wbuf = min(WINDOW, PAST_LEN)
    page_table = jax.random.permutation(ks[6], n_pool)[:n_used].reshape(DEC_BATCH, n_pages).astype(jnp.int32)
    return {
        'x_prompt': nrm(ks[0], (BATCH, SEQ, D_MODEL), jnp.float32),
        'x_sample': nrm(ks[1], (DEC_BATCH, DEC_SEQ, D_MODEL), jnp.float32),
        'cache_nsa': nrm(ks[2], (n_pool, PAGE_SIZE, 4, NSA_KV_HEADS, HEAD_DIM), jnp.float32),
        'cache_fox_kv': nrm(ks[3], (n_pool, PAGE_SIZE, 2, FOX_HEADS, HEAD_DIM), jnp.float32),
        'cache_fox_logf': jax.nn.log_sigmoid(3.0 + nrm(ks[4], (n_pool, PAGE_SIZE, FOX_HEADS), jnp.float32)),
        'state_nsa_win': nrm(ks[5], (DEC_BATCH, wbuf, 2, NSA_KV_HEADS, HEAD_DIM), jnp.float32),
        'page_table': page_table,
        'norm_attn': 1.0 + 0.05 * nrm(ks[7], (D_MODEL,), jnp.float32),
        'w_in': nrm(ks[8], (D_MODEL, IN_WIDTH), jnp.float32) * D_MODEL ** -0.5,
        'fox_f_bias': 3.0 + 0.1 * nrm(ks[9], (FOX_HEADS,), jnp.float32),
        'nsa_q_norm': 1.0 + 0.05 * nrm(ks[10], (HEAD_DIM,), jnp.float32),
        'nsa_k_norm': 1.0 + 0.05 * nrm(ks[11], (3, HEAD_DIM), jnp.float32),
        'fox_q_norm': 1.0 + 0.05 * nrm(ks[12], (HEAD_DIM,), jnp.float32),
        'fox_k_norm': 1.0 + 0.05 * nrm(ks[13], (HEAD_DIM,), jnp.float32),
        'cmp_pe': 0.1 * nrm(ks[14], (2, CMP_BLOCK, HEAD_DIM), jnp.float32),
        'cmp_w1': nrm(ks[15], (2, CMP_BLOCK, HEAD_DIM, HEAD_DIM), jnp.float32) * (CMP_BLOCK * HEAD_DIM) ** -0.5,
        'cmp_w2': nrm(ks[16], (2, HEAD_DIM, HEAD_DIM), jnp.float32) * HEAD_DIM ** -0.5,
        'w_up_nsa': nrm(ks[17], (NSA_Q_W, D_MODEL), jnp.float32) * NSA_Q_W ** -0.5,
        'w_up_fox': nrm(ks[18], (FOX_W, D_MODEL), jnp.float32) * FOX_W ** -0.5,
        'w_out': nrm(ks[19], (D_MODEL, D_MODEL), jnp.float32) * D_MODEL ** -0.5,
        'norm_ffn': 1.0 + 0.05 * nrm(ks[20], (D_MODEL,), jnp.float32),
        'peer_w_query': nrm(ks[21], (D_MODEL, PEER_HEADS * PEER_DK), jnp.float32) * D_MODEL ** -0.5,
        'peer_sub_keys': nrm(ks[22], (PEER_HEADS, 2, PEER_N_KEYS, PEER_DK_HALF), jnp.float32) * PEER_DK_HALF ** -0.5,
        'peer_u': nrm(ks[23], (PEER_N_EXPERTS, D_MODEL), jnp.float32) * D_MODEL ** -0.5,
        'peer_v': 0.5 * nrm(ks[24], (PEER_N_EXPERTS, D_MODEL), jnp.float32),
    }


def reference(x_prompt, x_sample, cache_nsa, cache_fox_kv, cache_fox_logf, state_nsa_win, page_table,
              norm_attn, w_in, fox_f_bias, nsa_q_norm, nsa_k_norm, fox_q_norm, fox_k_norm,
              cmp_pe, cmp_w1, cmp_w2, w_up_nsa, w_up_fox, w_out, norm_ffn,
              peer_w_query, peer_sub_keys, peer_u, peer_v):
    slopes = alibi_slopes(NSA_HEADS)
    x_p, x_s = x_prompt, x_sample
    for _ in range(DEPTH):
        q_a, gates_a, nsa_p, win_rows_p, q_b, fox_p, logf_p, mg_p = _front(
            x_p, norm_attn, w_in, fox_f_bias, nsa_q_norm, nsa_k_norm, fox_q_norm, fox_k_norm)
        seq = x_p.shape[1]
        o_nsa = nsa_prompt(q_a, gates_a, nsa_p, win_rows_p, slopes, nsa_k_norm, cmp_pe, cmp_w1, cmp_w2)
        o_fox = fox_prompt(q_b, fox_p, logf_p)
        win_p = win_rows_p[:, seq - min(WINDOW, seq):]
        x_p = _merge_and_ffn(x_p, o_nsa, o_fox, mg_p, w_up_nsa, w_up_fox, w_out, norm_ffn,
                             peer_w_query, peer_sub_keys, peer_u, peer_v)
        q_a, gates_a, nsa_s, win_rows_s, q_b, fox_s, logf_s, mg_s = _front(
            x_s, norm_attn, w_in, fox_f_bias, nsa_q_norm, nsa_k_norm, fox_q_norm, fox_k_norm)
        o_nsa, win_s = nsa_sample(q_a, gates_a, nsa_s, win_rows_s, cache_nsa, state_nsa_win, page_table,
                                  slopes, nsa_k_norm, cmp_pe, cmp_w1, cmp_w2)
        o_fox = fox_sample(q_b, fox_s, logf_s, cache_fox_kv, cache_fox_logf, page_table)
        x_s = _merge_and_ffn(x_s, o_nsa, o_fox, mg_s, w_up_nsa, w_up_fox, w_out, norm_ffn,
                             peer_w_query, peer_sub_keys, peer_u, peer_v)
    return (x_p, x_s, nsa_p, fox_p, logf_p, win_p, nsa_s, fox_s, logf_s, win_s)
```

```python
import functools

import jax
import jax.numpy as jnp
from jax import lax
from jax.experimental import pallas as pl
from jax.experimental.pallas import tpu as pltpu

D_MODEL = 1024
HEAD_DIM = 64
NSA_HEADS = 8
NSA_KV_HEADS = 2
NSA_GROUP = NSA_HEADS // NSA_KV_HEADS
CMP_BLOCK = 64
SEL_BLOCK = CMP_BLOCK
SEL_TOPK = 16
WINDOW = 512
FOX_HEADS = 8
Q_BLOCK = 128
PEER_HEADS = 8
PEER_N_KEYS = 128
PEER_DK = 256
PEER_DK_HALF = PEER_DK // 2
PEER_TOPK = 16
PEER_CHUNK = 256

NSA_Q_W = NSA_HEADS * HEAD_DIM
NSA_KV_W = 6 * NSA_KV_HEADS * HEAD_DIM
NSA_GATE_W = 3 * NSA_HEADS
FOX_W = FOX_HEADS * HEAD_DIM
FOX_QKV_W = 3 * FOX_W
FOX_F_W = FOX_HEADS
MERGE_W = 2 * D_MODEL
SPLIT_Q_A = NSA_Q_W
SPLIT_KV_A = SPLIT_Q_A + NSA_KV_W
SPLIT_G_A = SPLIT_KV_A + NSA_GATE_W
SPLIT_QKV_B = SPLIT_G_A + FOX_QKV_W
SPLIT_F_B = SPLIT_QKV_B + FOX_F_W
IN_WIDTH = SPLIT_F_B + MERGE_W

SCALE = HEAD_DIM ** -0.5
FORCE_SCORE = float(NSA_GROUP + 1)
NEG_INF = -1e30
EPS = 1e-6

LANE = 128
VMEM_LIMIT = 48 * 1024 * 1024


def _group_mean_sq(x, bd):
    sq = x * x
    hi = sq.astype(jnp.bfloat16)
    lo = (sq - hi.astype(jnp.float32)).astype(jnp.bfloat16)
    return (jnp.dot(hi, bd, preferred_element_type=jnp.float32)
            + jnp.dot(lo, bd, preferred_element_type=jnp.float32))


def _head_rms(x, g, bd):
    outs = []
    for c in range(x.shape[1] // LANE):
        xc = x[:, c * LANE:(c + 1) * LANE]
        outs.append(xc * lax.rsqrt(_group_mean_sq(xc, bd) + EPS) * g)
    return outs[0] if len(outs) == 1 else jnp.concatenate(outs, axis=1)


def _front_kernel(x_ref, na_ref, w_ref, bd_ref, fb_ref, gq_a_ref, gk_sel_ref, gk_win_ref,
                  gq_b_ref, gk_b_ref,
                  qa_ref, rows_ref, win_ref, ga_ref, qb_ref, fox_ref, logf_ref, mg_ref):
    x = x_ref[...]
    h = x * lax.rsqrt(jnp.mean(x * x, axis=-1, keepdims=True) + EPS) * na_ref[...]
    hb = h.astype(jnp.bfloat16)
    bd = bd_ref[...]

    def proj(c0, width):
        return jnp.dot(hb, w_ref[:, c0:c0 + width], preferred_element_type=jnp.float32)

    c = 0
    qa_ref[...] = _head_rms(proj(c, NSA_Q_W), gq_a_ref[...], bd)
    c += NSA_Q_W
    rows_ref[:, 0:256] = proj(c, 256)
    rows_ref[:, 256:384] = _head_rms(proj(c + 256, 128), gk_sel_ref[...], bd)
    rows_ref[:, 384:512] = proj(c + 384, 128)
    win_ref[:, 0:128] = _head_rms(proj(c + 512, 128), gk_win_ref[...], bd)
    win_ref[:, 128:256] = proj(c + 640, 128)
    c += NSA_KV_W
    qb_ref[...] = _head_rms(proj(c, FOX_W), gq_b_ref[...], bd)
    fox_ref[:, 0:FOX_W] = _head_rms(proj(c + FOX_W, FOX_W), gk_b_ref[...], bd)
    fox_ref[:, FOX_W:2 * FOX_W] = proj(c + 2 * FOX_W, FOX_W)
    c += FOX_QKV_W
    for j in range(MERGE_W // 512):
        mg_ref[:, j * 512:(j + 1) * 512] = jax.nn.sigmoid(proj(c + j * 512, 512))
    c += MERGE_W
    ga_ref[...] = jax.nn.sigmoid(proj(c, LANE))
    f = proj(c + LANE, LANE)[:, 0:FOX_F_W] + fb_ref[...]
    logf_ref[...] = jnp.minimum(f, 0.0) - jnp.log1p(jnp.exp(-jnp.abs(f)))


def _front(x2d, norm_attn, w_front, bd, fox_f_bias, nsa_q_norm, nsa_k_norm, fox_q_norm, fox_k_norm,
           tm=256):
    n = x2d.shape[0]
    wf = w_front.shape[1]
    two = lambda g: jnp.concatenate([g, g]).reshape(1, LANE)
    row = lambda w: pl.BlockSpec((tm, w), lambda i: (i, 0))
    full = lambda a: pl.BlockSpec(a.shape, lambda i: (0,) * a.ndim)
    args = (x2d, norm_attn.reshape(1, D_MODEL), w_front, bd, fox_f_bias.reshape(1, FOX_F_W),
            two(nsa_q_norm), two(nsa_k_norm[1]), two(nsa_k_norm[2]), two(fox_q_norm), two(fox_k_norm))
    widths = (NSA_Q_W, 512, 256, LANE, FOX_W, 2 * FOX_W, FOX_F_W, MERGE_W)
    return pl.pallas_call(
        _front_kernel,
        grid=(n // tm,),
        in_specs=[row(D_MODEL)] + [full(a) for a in args[1:]],
        out_specs=[row(w) for w in widths],
        out_shape=[jax.ShapeDtypeStruct((n, w), jnp.float32) for w in widths],
        compiler_params=pltpu.CompilerParams(dimension_semantics=("arbitrary",),
                                             vmem_limit_bytes=VMEM_LIMIT),
        name="front",
    )(*args)


def _front_weights(w_in):
    pad = lambda w: jnp.pad(w, ((0, 0), (0, LANE - w.shape[1])))
    parts = [w_in[:, :SPLIT_KV_A], w_in[:, SPLIT_G_A:SPLIT_QKV_B], w_in[:, SPLIT_F_B:],
             pad(w_in[:, SPLIT_KV_A:SPLIT_G_A]), pad(w_in[:, SPLIT_QKV_B:SPLIT_F_B])]
    return jnp.concatenate(parts, axis=1).astype(jnp.bfloat16)


def _block_diag_mean():
    r = lax.broadcasted_iota(jnp.int32, (LANE, LANE), 0) // HEAD_DIM
    c = lax.broadcasted_iota(jnp.int32, (LANE, LANE), 1) // HEAD_DIM
    return jnp.where(r == c, 1.0 / HEAD_DIM, 0.0).astype(jnp.bfloat16)


def rms_norm(x, g):
    xf = x.astype(jnp.float32)
    y = xf * lax.rsqrt(jnp.mean(xf * xf, axis=-1, keepdims=True) + EPS)
    return (y * g.astype(jnp.float32)).astype(x.dtype)


def masked_softmax(logits, mask):
    lf = jnp.where(mask, logits.astype(jnp.float32), NEG_INF)
    m = jnp.max(lf, axis=-1, keepdims=True)
    p = jnp.where(mask, jnp.exp(lf - m), 0.0)
    return p / jnp.maximum(jnp.sum(p, axis=-1, keepdims=True), 1e-30)


def alibi_slopes(n):
    return jnp.exp2(-8.0 * jnp.arange(1, n + 1, dtype=jnp.float32) / n)


def compress_blocks(rows, pe, w1, w2):
    b, l = rows.shape[:2]
    blk = rows.reshape(b, l // CMP_BLOCK, CMP_BLOCK, NSA_KV_HEADS, HEAD_DIM) + pe[:, None, :]
    hid = jax.nn.gelu(jnp.einsum('bnlgd,lde->bnge', blk, w1))
    return jnp.einsum('bnge,ef->bngf', hid, w2)


def nsa_block(q, gates, qpos, kc, vc, cmp_end, ks, vs, kw, vw, wpos, slopes):
    b, nq = q.shape[:2]
    nb = kc.shape[1]
    qg = q.reshape(b, nq, NSA_KV_HEADS, NSA_GROUP, HEAD_DIM)
    m = slopes.reshape(NSA_KV_HEADS, NSA_GROUP)[:, :, None, None]
    dist_c = qpos[:, None] - cmp_end[None, :]
    lc = jnp.einsum('bqgrd,bngd->bgrqn', qg, kc) * SCALE - m * dist_c.astype(jnp.float32)
    pc = masked_softmax(lc, dist_c >= 0)
    o_c = jnp.einsum('bgrqn,bngd->bqgrd', pc, vc)
    blk = jnp.arange(nb)
    cur = (qpos // SEL_BLOCK)[:, None]
    forced = (blk[None, :] == cur) | (blk[None, :] == 0)
    allowed = blk[None, :] <= cur
    imp = jnp.where(forced, FORCE_SCORE, jnp.where(allowed, pc.sum(axis=2), -1.0))
    _, idx = lax.top_k(imp, min(SEL_TOPK, nb))
    kb = ks.reshape(b, nb, SEL_BLOCK, NSA_KV_HEADS, HEAD_DIM).transpose(0, 3, 1, 2, 4)
    vb = vs.reshape(b, nb, SEL_BLOCK, NSA_KV_HEADS, HEAD_DIM).transpose(0, 3, 1, 2, 4)
    take = jax.vmap(jax.vmap(lambda blocks, i: blocks[i]))
    k_sel = take(kb, idx).reshape(b, NSA_KV_HEADS, nq, -1, HEAD_DIM)
    v_sel = take(vb, idx).reshape(b, NSA_KV_HEADS, nq, -1, HEAD_DIM)
    spos = (idx[..., None] * SEL_BLOCK + jnp.arange(SEL_BLOCK)).reshape(b, NSA_KV_HEADS, nq, -1)
    dist_s = (qpos[:, None] - spos)[:, :, None]
    ls = jnp.einsum('bqgrd,bgqsd->bgrqs', qg, k_sel) * SCALE - m * dist_s.astype(jnp.float32)
    ps = masked_softmax(ls, dist_s >= 0)
    o_s = jnp.einsum('bgrqs,bgqsd->bqgrd', ps, v_sel)
    dist_w = qpos[:, None] - wpos[None, :]
    mask_w = (dist_w >= 0) & (dist_w < WINDOW) & (wpos[None, :] >= 0)
    lw = jnp.einsum('bqgrd,bsgd->bgrqs', qg, kw) * SCALE - m * dist_w.astype(jnp.float32)
    pw = masked_softmax(lw, mask_w)
    o_w = jnp.einsum('bgrqs,bsgd->bqgrd', pw, vw)
    g = gates.reshape(b, nq, NSA_KV_HEADS, NSA_GROUP, 3)
    o = g[..., 0:1] * o_c + g[..., 1:2] * o_s + g[..., 2:3] * o_w
    return o.reshape(b, nq, NSA_Q_W)


def _compressed_kv(rows, nsa_k_norm, cmp_pe, cmp_w1, cmp_w2):
    kc = rms_norm(compress_blocks(rows[:, :, 0], cmp_pe[0], cmp_w1[0], cmp_w2[0]), nsa_k_norm[0])
    vc = compress_blocks(rows[:, :, 1], cmp_pe[1], cmp_w1[1], cmp_w2[1])
    nb = rows.shape[1] // CMP_BLOCK
    cmp_end = jnp.arange(nb) * CMP_BLOCK + (CMP_BLOCK - 1)
    return kc, vc, cmp_end


def nsa_prompt(q, gates, nsa_rows, win_rows, slopes, nsa_k_norm, cmp_pe, cmp_w1, cmp_w2):
    b, t = q.shape[:2]
    kc, vc, cmp_end = _compressed_kv(nsa_rows, nsa_k_norm, cmp_pe, cmp_w1, cmp_w2)
    ks, vs = nsa_rows[:, :, 2], nsa_rows[:, :, 3]
    win_pad = jnp.pad(win_rows, ((0, 0), (WINDOW, 0), (0, 0), (0, 0), (0, 0)))

    def one(i):
        q0 = i * Q_BLOCK
        qb = lax.dynamic_slice_in_dim(q, q0, Q_BLOCK, axis=1)
        gb = lax.dynamic_slice_in_dim(gates, q0, Q_BLOCK, axis=1)
        wb = lax.dynamic_slice_in_dim(win_pad, q0, WINDOW + Q_BLOCK, axis=1)
        wpos = q0 - WINDOW + jnp.arange(WINDOW + Q_BLOCK)
        return nsa_block(qb, gb, q0 + jnp.arange(Q_BLOCK), kc, vc, cmp_end, ks, vs,
                         wb[:, :, 0], wb[:, :, 1], wpos, slopes)

    o = lax.map(one, jnp.arange(t // Q_BLOCK))
    return o.transpose(1, 0, 2, 3).reshape(b, t, NSA_Q_W)


def nsa_sample(q, gates, new_rows, new_win, cache_nsa, state_nsa_win, page_table, slopes,
               nsa_k_norm, cmp_pe, cmp_w1, cmp_w2):
    db, ns = q.shape[:2]
    past_len = page_table.shape[1] * cache_nsa.shape[1]
    past = cache_nsa[page_table].reshape(db, past_len, 4, NSA_KV_HEADS, HEAD_DIM)
    rows = jnp.concatenate([past, new_rows], axis=1)
    pad = (-(past_len + ns)) % SEL_BLOCK
    rows = jnp.pad(rows, ((0, 0), (0, pad), (0, 0), (0, 0), (0, 0)))
    kc, vc, cmp_end = _compressed_kv(rows, nsa_k_norm, cmp_pe, cmp_w1, cmp_w2)
    wbuf = state_nsa_win.shape[1]
    wrows = jnp.concatenate([state_nsa_win, new_win], axis=1)
    wpos = past_len - wbuf + jnp.arange(wbuf + ns)
    o = nsa_block(q, gates, past_len + jnp.arange(ns), kc, vc, cmp_end, rows[:, :, 2], rows[:, :, 3],
                  wrows[:, :, 0], wrows[:, :, 1], wpos, slopes)
    return o, wrows[:, ns:]


def fox_block(q, cq, qpos, k, v, ck, kpos):
    logits = jnp.einsum('bqhd,bkhd->bhqk', q, k) * SCALE + (cq[..., :, None] - ck[..., None, :])
    p = masked_softmax(logits, kpos[None, :] <= qpos[:, None])
    o = jnp.einsum('bhqk,bkhd->bqhd', p, v)
    return o.reshape(q.shape[0], q.shape[1], FOX_W)


def fox_prompt(q, fox_rows, logf):
    b, t = q.shape[:2]
    c = jnp.cumsum(logf.astype(jnp.float32), axis=1).transpose(0, 2, 1)
    k, v = fox_rows[:, :, 0], fox_rows[:, :, 1]
    kpos = jnp.arange(t)

    def one(i):
        q0 = i * Q_BLOCK
        qb = lax.dynamic_slice_in_dim(q, q0, Q_BLOCK, axis=1)
        cq = lax.dynamic_slice_in_dim(c, q0, Q_BLOCK, axis=2)
        return fox_block(qb, cq, q0 + jnp.arange(Q_BLOCK), k, v, c, kpos)

    o = lax.map(one, jnp.arange(t // Q_BLOCK))
    return o.transpose(1, 0, 2, 3).reshape(b, t, FOX_W)


def fox_sample(q, fox_rows, logf, cache_fox_kv, cache_fox_logf, page_table):
    db, ns = q.shape[:2]
    past_len = page_table.shape[1] * cache_fox_kv.shape[1]
    kv = jnp.concatenate([cache_fox_kv[page_table].reshape(db, past_len, 2, FOX_HEADS, HEAD_DIM), fox_rows], axis=1)
    lf = jnp.concatenate([cache_fox_logf[page_table].reshape(db, past_len, FOX_HEADS).astype(jnp.float32), logf], axis=1)
    c = jnp.cumsum(lf, axis=1).transpose(0, 2, 1)
    kpos = jnp.arange(past_len + ns)
    return fox_block(q, c[:, :, past_len:], past_len + jnp.arange(ns), kv[:, :, 0], kv[:, :, 1], c, kpos)


def peer_chunk(h, w_query, sub_keys, u_tab, v_tab):
    c = h.shape[0]
    q = (h @ w_query).reshape(c, PEER_HEADS, 2, PEER_DK_HALF)
    s = jnp.einsum('chpd,hpkd->chpk', q, sub_keys).astype(jnp.float32)
    s1, i1 = lax.top_k(s[:, :, 0], PEER_TOPK)
    s2, i2 = lax.top_k(s[:, :, 1], PEER_TOPK)
    cand = (s1[..., :, None] + s2[..., None, :]).reshape(c, PEER_HEADS, -1)
    cidx = (i1[..., :, None] * PEER_N_KEYS + i2[..., None, :]).reshape(c, PEER_HEADS, -1)
    top, pos = lax.top_k(cand, PEER_TOPK)
    eidx = jnp.take_along_axis(cidx, pos, axis=-1)
    g = jax.nn.softmax(top, axis=-1)
    act = jax.nn.gelu(jnp.einsum('cd,chkd->chk', h, u_tab[eidx]))
    return jnp.einsum('chk,chkd->cd', g * act, v_tab[eidx]).astype(h.dtype)


def peer_ffn(h, w_query, sub_keys, u_tab, v_tab):
    n = h.shape[0]
    pad = (-n) % PEER_CHUNK
    hp = jnp.pad(h, ((0, pad), (0, 0))).reshape(-1, PEER_CHUNK, D_MODEL)
    out = lax.map(lambda hc: peer_chunk(hc, w_query, sub_keys, u_tab, v_tab), hp)
    return out.reshape(-1, D_MODEL)[:n]


def _merge_and_ffn(x, o_nsa, o_fox, merge_gates, w_up_nsa, w_up_fox, w_out, norm_ffn,
                   peer_w_query, peer_sub_keys, peer_u, peer_v):
    mixed = merge_gates[:, :, 0] * (o_nsa @ w_up_nsa) + merge_gates[:, :, 1] * (o_fox @ w_up_fox)
    x1 = x + (mixed @ w_out).astype(x.dtype)
    h2 = rms_norm(x1, norm_ffn).reshape(-1, D_MODEL)
    return x1 + peer_ffn(h2, peer_w_query, peer_sub_keys, peer_u, peer_v).reshape(x.shape)


def kernel(x_prompt, x_sample, cache_nsa, cache_fox_kv, cache_fox_logf, state_nsa_win, page_table,
           norm_attn, w_in, fox_f_bias, nsa_q_norm, nsa_k_norm, fox_q_norm, fox_k_norm,
           cmp_pe, cmp_w1, cmp_w2, w_up_nsa, w_up_fox, w_out, norm_ffn,
           peer_w_query, peer_sub_keys, peer_u, peer_v):
    slopes = alibi_slopes(NSA_HEADS)
    w_front = _front_weights(w_in)
    bd = _block_diag_mean()

    def front(x):
        b, t, _ = x.shape
        qa, rows, win, ga, qb, fox, logf, mg = _front(
            x.reshape(b * t, D_MODEL), norm_attn, w_front, bd, fox_f_bias,
            nsa_q_norm, nsa_k_norm, fox_q_norm, fox_k_norm)
        return (qa.reshape(b, t, NSA_HEADS, HEAD_DIM),
                ga[:, :NSA_GATE_W].reshape(b, t, NSA_HEADS, 3),
                rows.reshape(b, t, 4, NSA_KV_HEADS, HEAD_DIM),
                win.reshape(b, t, 2, NSA_KV_HEADS, HEAD_DIM),
                qb.reshape(b, t, FOX_HEADS, HEAD_DIM),
                fox.reshape(b, t, 2, FOX_HEADS, HEAD_DIM),
                logf.reshape(b, t, FOX_HEADS),
                mg.reshape(b, t, 2, D_MODEL))

    q_a, gates_a, nsa_p, win_rows_p, q_b, fox_p, logf_p, mg_p = front(x_prompt)
    seq = x_prompt.shape[1]
    o_nsa = nsa_prompt(q_a, gates_a, nsa_p, win_rows_p, slopes, nsa_k_norm, cmp_pe, cmp_w1, cmp_w2)
    o_fox = fox_prompt(q_b, fox_p, logf_p)
    win_p = win_rows_p[:, seq - min(WINDOW, seq):]
    y_p = _merge_and_ffn(x_prompt, o_nsa, o_fox, mg_p, w_up_nsa, w_up_fox, w_out, norm_ffn,
                         peer_w_query, peer_sub_keys, peer_u, peer_v)

    q_a, gates_a, nsa_s, win_rows_s, q_b, fox_s, logf_s, mg_s = front(x_sample)
    o_nsa, win_s = nsa_sample(q_a, gates_a, nsa_s, win_rows_s, cache_nsa, state_nsa_win, page_table,
                              slopes, nsa_k_norm, cmp_pe, cmp_w1, cmp_w2)
    o_fox = fox_sample(q_b, fox_s, logf_s, cache_fox_kv, cache_fox_logf, page_table)
    y_s = _merge_and_ffn(x_sample, o_nsa, o_fox, mg_s, w_up_nsa, w_up_fox, w_out, norm_ffn,
                         peer_w_query, peer_sub_keys, peer_u, peer_v)
    return (y_p, y_s, nsa_p, fox_p, logf_p, win_p, nsa_s, fox_s, logf_s, win_s)
```

```python
import functools

import jax
import jax.numpy as jnp
from jax import lax
from jax.experimental import pallas as pl
from jax.experimental.pallas import tpu as pltpu

D_MODEL = 1024
HEAD_DIM = 64
NSA_HEADS = 8
NSA_KV_HEADS = 2
NSA_GROUP = NSA_HEADS // NSA_KV_HEADS
CMP_BLOCK = 64
SEL_BLOCK = CMP_BLOCK
SEL_TOPK = 16
WINDOW = 512
FOX_HEADS = 8
Q_BLOCK = 128
PEER_HEADS = 8
PEER_N_KEYS = 128
PEER_DK = 256
PEER_DK_HALF = PEER_DK // 2
PEER_TOPK = 16
PEER_CHUNK = 256

NSA_Q_W = NSA_HEADS * HEAD_DIM
NSA_KV_W = 6 * NSA_KV_HEADS * HEAD_DIM
NSA_GATE_W = 3 * NSA_HEADS
FOX_W = FOX_HEADS * HEAD_DIM
FOX_QKV_W = 3 * FOX_W
FOX_F_W = FOX_HEADS
MERGE_W = 2 * D_MODEL
SPLIT_Q_A = NSA_Q_W
SPLIT_KV_A = SPLIT_Q_A + NSA_KV_W
SPLIT_G_A = SPLIT_KV_A + NSA_GATE_W
SPLIT_QKV_B = SPLIT_G_A + FOX_QKV_W
SPLIT_F_B = SPLIT_QKV_B + FOX_F_W
IN_WIDTH = SPLIT_F_B + MERGE_W

SCALE = HEAD_DIM ** -0.5
FORCE_SCORE = float(NSA_GROUP + 1)
NEG_INF = -1e30
EPS = 1e-6

LANE = 128
VMEM_LIMIT = 48 * 1024 * 1024


def _group_mean_sq(x, bd):
    sq = x * x
    hi = sq.astype(jnp.bfloat16)
    lo = (sq - hi.astype(jnp.float32)).astype(jnp.bfloat16)
    return (jnp.dot(hi, bd, preferred_element_type=jnp.float32)
            + jnp.dot(lo, bd, preferred_element_type=jnp.float32))


def _head_rms(x, g, bd):
    outs = []
    for c in range(x.shape[1] // LANE):
        xc = x[:, c * LANE:(c + 1) * LANE]
        outs.append(xc * lax.rsqrt(_group_mean_sq(xc, bd) + EPS) * g)
    return outs[0] if len(outs) == 1 else jnp.concatenate(outs, axis=1)


def _front_kernel(x_ref, na_ref, w_ref, bd_ref, fb_ref, gq_a_ref, gk_sel_ref, gk_win_ref,
                  gq_b_ref, gk_b_ref,
                  qa_ref, rows_ref, win_ref, ga_ref, qb_ref, fox_ref, logf_ref, mg_ref):
    x = x_ref[...]
    h = x * lax.rsqrt(jnp.mean(x * x, axis=-1, keepdims=True) + EPS) * na_ref[...]
    hb = h.astype(jnp.bfloat16)
    bd = bd_ref[...]

    def proj(c0, width):
        return jnp.dot(hb, w_ref[:, c0:c0 + width], preferred_element_type=jnp.float32)

    c = 0
    qa_ref[...] = _head_rms(proj(c, NSA_Q_W), gq_a_ref[...], bd)
    c += NSA_Q_W
    rows_ref[:, 0:256] = proj(c, 256)
    rows_ref[:, 256:384] = _head_rms(proj(c + 256, 128), gk_sel_ref[...], bd)
    rows_ref[:, 384:512] = proj(c + 384, 128)
    win_ref[:, 0:128] = _head_rms(proj(c + 512, 128), gk_win_ref[...], bd)
    win_ref[:, 128:256] = proj(c + 640, 128)
    c += NSA_KV_W
    qb_ref[...] = _head_rms(proj(c, FOX_W), gq_b_ref[...], bd)
    fox_ref[:, 0:FOX_W] = _head_rms(proj(c + FOX_W, FOX_W), gk_b_ref[...], bd)
    fox_ref[:, FOX_W:2 * FOX_W] = proj(c + 2 * FOX_W, FOX_W)
    c += FOX_QKV_W
    for j in range(MERGE_W // 512):
        mg_ref[:, j * 512:(j + 1) * 512] = jax.nn.sigmoid(proj(c + j * 512, 512))
    c += MERGE_W
    ga_ref[...] = jax.nn.sigmoid(proj(c, LANE))
    f = proj(c + LANE, LANE)[:, 0:FOX_F_W] + fb_ref[...]
    logf_ref[...] = jnp.minimum(f, 0.0) - jnp.log1p(jnp.exp(-jnp.abs(f)))


def _front(x2d, norm_attn, w_front, bd, fox_f_bias, nsa_q_norm, nsa_k_norm, fox_q_norm, fox_k_norm,
           tm=256):
    n = x2d.shape[0]
    wf = w_front.shape[1]
    two = lambda g: jnp.concatenate([g, g]).reshape(1, LANE)
    row = lambda w: pl.BlockSpec((tm, w), lambda i: (i, 0))
    full = lambda a: pl.BlockSpec(a.shape, lambda i: (0,) * a.ndim)
    args = (x2d, norm_attn.reshape(1, D_MODEL), w_front, bd, fox_f_bias.reshape(1, FOX_F_W),
            two(nsa_q_norm), two(nsa_k_norm[1]), two(nsa_k_norm[2]), two(fox_q_norm), two(fox_k_norm))
    widths = (NSA_Q_W, 512, 256, LANE, FOX_W, 2 * FOX_W, FOX_F_W, MERGE_W)
    return pl.pallas_call(
        _front_kernel,
        grid=(n // tm,),
        in_specs=[row(D_MODEL)] + [full(a) for a in args[1:]],
        out_specs=[row(w) for w in widths],
        out_shape=[jax.ShapeDtypeStruct((n, w), jnp.float32) for w in widths],
        compiler_params=pltpu.CompilerParams(dimension_semantics=("arbitrary",),
                                             vmem_limit_bytes=VMEM_LIMIT),
        name="front",
    )(*args)


def _front_weights(w_in):
    pad = lambda w: jnp.pad(w, ((0, 0), (0, LANE - w.shape[1])))
    parts = [w_in[:, :SPLIT_KV_A], w_in[:, SPLIT_G_A:SPLIT_QKV_B], w_in[:, SPLIT_F_B:],
             pad(w_in[:, SPLIT_KV_A:SPLIT_G_A]), pad(w_in[:, SPLIT_QKV_B:SPLIT_F_B])]
    return jnp.concatenate(parts, axis=1).astype(jnp.bfloat16)


def _block_diag_mean():
    r = lax.broadcasted_iota(jnp.int32, (LANE, LANE), 0) // HEAD_DIM
    c = lax.broadcasted_iota(jnp.int32, (LANE, LANE), 1) // HEAD_DIM
    return jnp.where(r == c, 1.0 / HEAD_DIM, 0.0).astype(jnp.bfloat16)


BIG_ID = 1 << 20


def _topk_rows(s, k, ids):
    vals, idxs = [], []
    for _ in range(k):
        m = jnp.max(s, axis=0, keepdims=True)
        idx = jnp.min(jnp.where(s == m, ids, BIG_ID), axis=0, keepdims=True)
        vals.append(m)
        idxs.append(idx)
        s = jnp.where(ids == idx, -jnp.inf, s)
    return jnp.concatenate(vals, axis=0), jnp.concatenate(idxs, axis=0)


def _pick_rows(sel, table):
    out = jnp.zeros(sel.shape, table.dtype)
    for r in range(table.shape[0]):
        out = jnp.where(sel == r, table[r:r + 1, :], out)
    return out


def _merge_route_kernel(x_ref, on_ref, of_ref, mg_ref, wun_ref, wuf_ref, wo_ref, nf_ref, wqt_ref, sk_ref,
                        x1_ref, h2_ref, i1_ref, i2_ref, g_ref):
    tm = x_ref.shape[0]
    f32, bf16 = jnp.float32, jnp.bfloat16
    a = jnp.dot(on_ref[...].astype(bf16), wun_ref[...], preferred_element_type=f32)
    b = jnp.dot(of_ref[...].astype(bf16), wuf_ref[...], preferred_element_type=f32)
    mixed = mg_ref[:, 0:D_MODEL] * a + mg_ref[:, D_MODEL:2 * D_MODEL] * b
    x1 = x_ref[...] + jnp.dot(mixed.astype(bf16), wo_ref[...], preferred_element_type=f32)
    x1_ref[...] = x1
    h2 = x1 * lax.rsqrt(jnp.mean(x1 * x1, axis=-1, keepdims=True) + EPS) * nf_ref[...]
    h2b = h2.astype(bf16)
    h2_ref[...] = h2b

    nk = PEER_N_KEYS
    key_ids = lax.broadcasted_iota(jnp.int32, (nk, tm), 0)
    io16 = lax.broadcasted_iota(jnp.int32, (PEER_TOPK, tm), 0)
    io8 = lax.broadcasted_iota(jnp.int32, (8, tm), 0)
    cand_ids = jnp.concatenate([io16] + [a_ * PEER_TOPK + io8 for a_ in range(1, 8)]
                               + [(io8 + 8) * PEER_TOPK], axis=0)
    for h in range(PEER_HEADS):
        sv, si = [], []
        for p in range(2):
            hp = 2 * h + p
            qt = lax.dot_general(wqt_ref[hp * PEER_DK_HALF:(hp + 1) * PEER_DK_HALF, :], h2b,
                                 (((1,), (1,)), ((), ())), preferred_element_type=f32)
            st = jnp.dot(sk_ref[hp], qt.astype(bf16), preferred_element_type=f32)
            v, i = _topk_rows(st, PEER_TOPK, key_ids)
            sv.append(v)
            si.append(i)
        s1, s2 = sv
        cand = jnp.concatenate([s1[0:1] + s2] + [s1[a_:a_ + 1] + s2[0:8] for a_ in range(1, 8)]
                               + [s1[8:16] + s2[0:1]], axis=0)
        top, fid = _topk_rows(cand, PEER_TOPK, cand_ids)
        e = jnp.exp(top - jnp.max(top, axis=0, keepdims=True))
        g = e / jnp.sum(e, axis=0, keepdims=True)
        sl = slice(h * PEER_TOPK, (h + 1) * PEER_TOPK)
        i1_ref[:, sl] = _pick_rows(fid >> 4, si[0]).T
        i2_ref[:, sl] = _pick_rows(fid & (PEER_TOPK - 1), si[1]).T
        g_ref[:, sl] = g.T


def _merge_route(x2d, o_nsa, o_fox, mg, w_up_nsa, w_up_fox, w_out, norm_ffn, wq_t, sub_keys, tm=256):
    n = x2d.shape[0]
    row = lambda w: pl.BlockSpec((tm, w), lambda i: (i, 0))
    full = lambda a: pl.BlockSpec(a.shape, lambda i: (0,) * a.ndim)
    args = (x2d, o_nsa, o_fox, mg, w_up_nsa, w_up_fox, w_out, norm_ffn.reshape(1, D_MODEL), wq_t, sub_keys)
    hk = PEER_HEADS * PEER_TOPK
    return pl.pallas_call(
        _merge_route_kernel,
        grid=(n // tm,),
        in_specs=[row(D_MODEL), row(NSA_Q_W), row(FOX_W), row(MERGE_W)] + [full(a) for a in args[4:]],
        out_specs=[row(D_MODEL), row(D_MODEL), row(hk), row(hk), row(hk)],
        out_shape=[jax.ShapeDtypeStruct((n, D_MODEL), jnp.float32),
                   jax.ShapeDtypeStruct((n, D_MODEL), jnp.bfloat16),
                   jax.ShapeDtypeStruct((n, hk), jnp.int32),
                   jax.ShapeDtypeStruct((n, hk), jnp.int32),
                   jax.ShapeDtypeStruct((n, hk), jnp.float32)],
        compiler_params=pltpu.CompilerParams(dimension_semantics=("arbitrary",),
                                             vmem_limit_bytes=VMEM_LIMIT),
        name="merge_route",
    )(*args)


def _peer_act_kernel(h2_ref, u_ref, i1_ref, i2_ref, act_ref):
    c = pl.program_id(1)
    ec = u_ref.shape[0]

    @pl.when(c == 0)
    def _():
        act_ref[...] = jnp.zeros_like(act_ref)

    a = lax.dot_general(h2_ref[...], u_ref[...], (((1,), (1,)), ((), ())),
                        preferred_element_type=jnp.float32)
    i1 = i1_ref[...]
    i2 = i2_ref[...]
    act = act_ref[...]
    for ii in range(ec // PEER_N_KEYS):
        got = jnp.take_along_axis(a[:, ii * PEER_N_KEYS:(ii + 1) * PEER_N_KEYS], i2, axis=1)
        act = jnp.where(i1 == c * (ec // PEER_N_KEYS) + ii, got, act)
    act_ref[...] = act


def _peer_act(h2b, u_b, i1, i2, tm=512, ec=2048):
    n = h2b.shape[0]
    hk = i1.shape[1]
    return pl.pallas_call(
        _peer_act_kernel,
        grid=(n // tm, u_b.shape[0] // ec),
        in_specs=[pl.BlockSpec((tm, D_MODEL), lambda t, c: (t, 0)),
                  pl.BlockSpec((ec, D_MODEL), lambda t, c: (c, 0)),
                  pl.BlockSpec((tm, hk), lambda t, c: (t, 0)),
                  pl.BlockSpec((tm, hk), lambda t, c: (t, 0))],
        out_specs=pl.BlockSpec((tm, hk), lambda t, c: (t, 0)),
        out_shape=jax.ShapeDtypeStruct((n, hk), jnp.float32),
        compiler_params=pltpu.CompilerParams(dimension_semantics=("arbitrary", "arbitrary"),
                                             vmem_limit_bytes=VMEM_LIMIT),
        name="peer_act",
    )(h2b, u_b, i1, i2)


def _peer_coef_kernel(act_ref, g_ref, i1_ref, i2_ref, c_ref, coef_ref):
    tm = act_ref.shape[0]
    nk = PEER_N_KEYS
    coef_ref[...] = g_ref[...] * jax.nn.gelu(act_ref[...])
    sub = lax.broadcasted_iota(jnp.int32, (nk, i1_ref.shape[1]), 0)

    def body(t, carry):
        r1 = i1_ref[pl.ds(t, 1), :]
        r2 = i2_ref[pl.ds(t, 1), :]
        cf = coef_ref[pl.ds(t, 1), :]
        m1 = jnp.where(r1 == sub, cf, 0.0).astype(jnp.bfloat16)
        m2t = jnp.where(r2 == sub, 1.0, 0.0).astype(jnp.bfloat16)
        ct = lax.dot_general(m1, m2t, (((1,), (1,)), ((), ())), preferred_element_type=jnp.float32)
        c_ref[t] = ct.astype(c_ref.dtype)
        return carry

    lax.fori_loop(0, tm, body, 0, unroll=4)


def _peer_coef(act, g, i1, i2, tm=128):
    n, hk = act.shape
    nk = PEER_N_KEYS
    row = pl.BlockSpec((tm, hk), lambda t: (t, 0))
    return pl.pallas_call(
        _peer_coef_kernel,
        grid=(n // tm,),
        in_specs=[row, row, row, row],
        out_specs=pl.BlockSpec((tm, nk, nk), lambda t: (t, 0, 0)),
        out_shape=jax.ShapeDtypeStruct((n, nk, nk), jnp.bfloat16),
        scratch_shapes=[pltpu.VMEM((tm, hk), jnp.float32)],
        compiler_params=pltpu.CompilerParams(dimension_semantics=("arbitrary",),
                                             vmem_limit_bytes=VMEM_LIMIT),
        name="peer_coef",
    )(act, g, i1, i2)


def _peer_out_kernel(c_ref, v_ref, x1_ref, y_ref, acc_ref):
    k = pl.program_id(1)

    @pl.when(k == 0)
    def _():
        acc_ref[...] = x1_ref[...]

    acc_ref[...] += jnp.dot(c_ref[...], v_ref[...], preferred_element_type=jnp.float32)

    @pl.when(k == pl.num_programs(1) - 1)
    def _():
        y_ref[...] = acc_ref[...]


def _peer_out(c2d, v_b, x1, tm=1024, tk=2048):
    n, ne = c2d.shape
    return pl.pallas_call(
        _peer_out_kernel,
        grid=(n // tm, ne // tk),
        in_specs=[pl.BlockSpec((tm, tk), lambda t, k: (t, k)),
                  pl.BlockSpec((tk, D_MODEL), lambda t, k: (k, 0)),
                  pl.BlockSpec((tm, D_MODEL), lambda t, k: (t, 0))],
        out_specs=pl.BlockSpec((tm, D_MODEL), lambda t, k: (t, 0)),
        out_shape=jax.ShapeDtypeStruct((n, D_MODEL), jnp.float32),
        scratch_shapes=[pltpu.VMEM((tm, D_MODEL), jnp.float32)],
        compiler_params=pltpu.CompilerParams(dimension_semantics=("arbitrary", "arbitrary"),
                                             vmem_limit_bytes=VMEM_LIMIT),
        name="peer_out",
    )(c2d, v_b, x1)


def _peer_weights(w_up_nsa, w_up_fox, w_out, norm_ffn, peer_w_query, peer_sub_keys, peer_u, peer_v):
    bf16 = jnp.bfloat16
    return dict(w_up_nsa=w_up_nsa.astype(bf16), w_up_fox=w_up_fox.astype(bf16), w_out=w_out.astype(bf16),
                norm_ffn=norm_ffn, wq_t=peer_w_query.T.astype(bf16),
                sub_keys=peer_sub_keys.reshape(2 * PEER_HEADS, PEER_N_KEYS, PEER_DK_HALF).astype(bf16),
                u=peer_u.astype(bf16), v=peer_v.astype(bf16))


def _merge_peer(x2d, o_nsa, o_fox, mg, wts):
    x1, h2b, i1, i2, g = _merge_route(x2d, o_nsa, o_fox, mg, wts['w_up_nsa'], wts['w_up_fox'], wts['w_out'],
                                      wts['norm_ffn'], wts['wq_t'], wts['sub_keys'])
    act = _peer_act(h2b, wts['u'], i1, i2)
    c3 = _peer_coef(act, g, i1, i2)
    return _peer_out(c3.reshape(c3.shape[0], -1), wts['v'], x1)


def rms_norm(x, g):
    xf = x.astype(jnp.float32)
    y = xf * lax.rsqrt(jnp.mean(xf * xf, axis=-1, keepdims=True) + EPS)
    return (y * g.astype(jnp.float32)).astype(x.dtype)


def masked_softmax(logits, mask):
    lf = jnp.where(mask, logits.astype(jnp.float32), NEG_INF)
    m = jnp.max(lf, axis=-1, keepdims=True)
    p = jnp.where(mask, jnp.exp(lf - m), 0.0)
    return p / jnp.maximum(jnp.sum(p, axis=-1, keepdims=True), 1e-30)


def alibi_slopes(n):
    return jnp.exp2(-8.0 * jnp.arange(1, n + 1, dtype=jnp.float32) / n)


def compress_blocks(rows, pe, w1, w2):
    b, l = rows.shape[:2]
    blk = rows.reshape(b, l // CMP_BLOCK, CMP_BLOCK, NSA_KV_HEADS, HEAD_DIM) + pe[:, None, :]
    hid = jax.nn.gelu(jnp.einsum('bnlgd,lde->bnge', blk, w1))
    return jnp.einsum('bnge,ef->bngf', hid, w2)


def nsa_block(q, gates, qpos, kc, vc, cmp_end, ks, vs, kw, vw, wpos, slopes):
    b, nq = q.shape[:2]
    nb = kc.shape[1]
    qg = q.reshape(b, nq, NSA_KV_HEADS, NSA_GROUP, HEAD_DIM)
    m = slopes.reshape(NSA_KV_HEADS, NSA_GROUP)[:, :, None, None]
    dist_c = qpos[:, None] - cmp_end[None, :]
    lc = jnp.einsum('bqgrd,bngd->bgrqn', qg, kc) * SCALE - m * dist_c.astype(jnp.float32)
    pc = masked_softmax(lc, dist_c >= 0)
    o_c = jnp.einsum('bgrqn,bngd->bqgrd', pc, vc)
    blk = jnp.arange(nb)
    cur = (qpos // SEL_BLOCK)[:, None]
    forced = (blk[None, :] == cur) | (blk[None, :] == 0)
    allowed = blk[None, :] <= cur
    imp = jnp.where(forced, FORCE_SCORE, jnp.where(allowed, pc.sum(axis=2), -1.0))
    _, idx = lax.top_k(imp, min(SEL_TOPK, nb))
    kb = ks.reshape(b, nb, SEL_BLOCK, NSA_KV_HEADS, HEAD_DIM).transpose(0, 3, 1, 2, 4)
    vb = vs.reshape(b, nb, SEL_BLOCK, NSA_KV_HEADS, HEAD_DIM).transpose(0, 3, 1, 2, 4)
    take = jax.vmap(jax.vmap(lambda blocks, i: blocks[i]))
    k_sel = take(kb, idx).reshape(b, NSA_KV_HEADS, nq, -1, HEAD_DIM)
    v_sel = take(vb, idx).reshape(b, NSA_KV_HEADS, nq, -1, HEAD_DIM)
    spos = (idx[..., None] * SEL_BLOCK + jnp.arange(SEL_BLOCK)).reshape(b, NSA_KV_HEADS, nq, -1)
    dist_s = (qpos[:, None] - spos)[:, :, None]
    ls = jnp.einsum('bqgrd,bgqsd->bgrqs', qg, k_sel) * SCALE - m * dist_s.astype(jnp.float32)
    ps = masked_softmax(ls, dist_s >= 0)
    o_s = jnp.einsum('bgrqs,bgqsd->bqgrd', ps, v_sel)
    dist_w = qpos[:, None] - wpos[None, :]
    mask_w = (dist_w >= 0) & (dist_w < WINDOW) & (wpos[None, :] >= 0)
    lw = jnp.einsum('bqgrd,bsgd->bgrqs', qg, kw) * SCALE - m * dist_w.astype(jnp.float32)
    pw = masked_softmax(lw, mask_w)
    o_w = jnp.einsum('bgrqs,bsgd->bqgrd', pw, vw)
    g = gates.reshape(b, nq, NSA_KV_HEADS, NSA_GROUP, 3)
    o = g[..., 0:1] * o_c + g[..., 1:2] * o_s + g[..., 2:3] * o_w
    return o.reshape(b, nq, NSA_Q_W)


def _compressed_kv(rows, nsa_k_norm, cmp_pe, cmp_w1, cmp_w2):
    kc = rms_norm(compress_blocks(rows[:, :, 0], cmp_pe[0], cmp_w1[0], cmp_w2[0]), nsa_k_norm[0])
    vc = compress_blocks(rows[:, :, 1], cmp_pe[1], cmp_w1[1], cmp_w2[1])
    nb = rows.shape[1] // CMP_BLOCK
    cmp_end = jnp.arange(nb) * CMP_BLOCK + (CMP_BLOCK - 1)
    return kc, vc, cmp_end


def nsa_prompt(q, gates, nsa_rows, win_rows, slopes, nsa_k_norm, cmp_pe, cmp_w1, cmp_w2):
    b, t = q.shape[:2]
    kc, vc, cmp_end = _compressed_kv(nsa_rows, nsa_k_norm, cmp_pe, cmp_w1, cmp_w2)
    ks, vs = nsa_rows[:, :, 2], nsa_rows[:, :, 3]
    win_pad = jnp.pad(win_rows, ((0, 0), (WINDOW, 0), (0, 0), (0, 0), (0, 0)))

    def one(i):
        q0 = i * Q_BLOCK
        qb = lax.dynamic_slice_in_dim(q, q0, Q_BLOCK, axis=1)
        gb = lax.dynamic_slice_in_dim(gates, q0, Q_BLOCK, axis=1)
        wb = lax.dynamic_slice_in_dim(win_pad, q0, WINDOW + Q_BLOCK, axis=1)
        wpos = q0 - WINDOW + jnp.arange(WINDOW + Q_BLOCK)
        return nsa_block(qb, gb, q0 + jnp.arange(Q_BLOCK), kc, vc, cmp_end, ks, vs,
                         wb[:, :, 0], wb[:, :, 1], wpos, slopes)

    o = lax.map(one, jnp.arange(t // Q_BLOCK))
    return o.transpose(1, 0, 2, 3).reshape(b, t, NSA_Q_W)


def nsa_sample(q, gates, new_rows, new_win, cache_nsa, state_nsa_win, page_table, slopes,
               nsa_k_norm, cmp_pe, cmp_w1, cmp_w2):
    db, ns = q.shape[:2]
    past_len = page_table.shape[1] * cache_nsa.shape[1]
    past = cache_nsa[page_table].reshape(db, past_len, 4, NSA_KV_HEADS, HEAD_DIM)
    rows = jnp.concatenate([past, new_rows], axis=1)
    pad = (-(past_len + ns)) % SEL_BLOCK
    rows = jnp.pad(rows, ((0, 0), (0, pad), (0, 0), (0, 0), (0, 0)))
    kc, vc, cmp_end = _compressed_kv(rows, nsa_k_norm, cmp_pe, cmp_w1, cmp_w2)
    wbuf = state_nsa_win.shape[1]
    wrows = jnp.concatenate([state_nsa_win, new_win], axis=1)
    wpos = past_len - wbuf + jnp.arange(wbuf + ns)
    o = nsa_block(q, gates, past_len + jnp.arange(ns), kc, vc, cmp_end, rows[:, :, 2], rows[:, :, 3],
                  wrows[:, :, 0], wrows[:, :, 1], wpos, slopes)
    return o, wrows[:, ns:]


def fox_block(q, cq, qpos, k, v, ck, kpos):
    logits = jnp.einsum('bqhd,bkhd->bhqk', q, k) * SCALE + (cq[..., :, None] - ck[..., None, :])
    p = masked_softmax(logits, kpos[None, :] <= qpos[:, None])
    o = jnp.einsum('bhqk,bkhd->bqhd', p, v)
    return o.reshape(q.shape[0], q.shape[1], FOX_W)


def fox_prompt(q, fox_rows, logf):
    b, t = q.shape[:2]
    c = jnp.cumsum(logf.astype(jnp.float32), axis=1).transpose(0, 2, 1)
    k, v = fox_rows[:, :, 0], fox_rows[:, :, 1]
    kpos = jnp.arange(t)

    def one(i):
        q0 = i * Q_BLOCK
        qb = lax.dynamic_slice_in_dim(q, q0, Q_BLOCK, axis=1)
        cq = lax.dynamic_slice_in_dim(c, q0, Q_BLOCK, axis=2)
        return fox_block(qb, cq, q0 + jnp.arange(Q_BLOCK), k, v, c, kpos)

    o = lax.map(one, jnp.arange(t // Q_BLOCK))
    return o.transpose(1, 0, 2, 3).reshape(b, t, FOX_W)


def fox_sample(q, fox_rows, logf, cache_fox_kv, cache_fox_logf, page_table):
    db, ns = q.shape[:2]
    past_len = page_table.shape[1] * cache_fox_kv.shape[1]
    kv = jnp.concatenate([cache_fox_kv[page_table].reshape(db, past_len, 2, FOX_HEADS, HEAD_DIM), fox_rows], axis=1)
    lf = jnp.concatenate([cache_fox_logf[page_table].reshape(db, past_len, FOX_HEADS).astype(jnp.float32), logf], axis=1)
    c = jnp.cumsum(lf, axis=1).transpose(0, 2, 1)
    kpos = jnp.arange(past_len + ns)
    return fox_block(q, c[:, :, past_len:], past_len + jnp.arange(ns), kv[:, :, 0], kv[:, :, 1], c, kpos)


def peer_chunk(h, w_query, sub_keys, u_tab, v_tab):
    c = h.shape[0]
    q = (h @ w_query).reshape(c, PEER_HEADS, 2, PEER_DK_HALF)
    s = jnp.einsum('chpd,hpkd->chpk', q, sub_keys).astype(jnp.float32)
    s1, i1 = lax.top_k(s[:, :, 0], PEER_TOPK)
    s2, i2 = lax.top_k(s[:, :, 1], PEER_TOPK)
    cand = (s1[..., :, None] + s2[..., None, :]).reshape(c, PEER_HEADS, -1)
    cidx = (i1[..., :, None] * PEER_N_KEYS + i2[..., None, :]).reshape(c, PEER_HEADS, -1)
    top, pos = lax.top_k(cand, PEER_TOPK)
    eidx = jnp.take_along_axis(cidx, pos, axis=-1)
    g = jax.nn.softmax(top, axis=-1)
    act = jax.nn.gelu(jnp.einsum('cd,chkd->chk', h, u_tab[eidx]))
    return jnp.einsum('chk,chkd->cd', g * act, v_tab[eidx]).astype(h.dtype)


def peer_ffn(h, w_query, sub_keys, u_tab, v_tab):
    n = h.shape[0]
    pad = (-n) % PEER_CHUNK
    hp = jnp.pad(h, ((0, pad), (0, 0))).reshape(-1, PEER_CHUNK, D_MODEL)
    out = lax.map(lambda hc: peer_chunk(hc, w_query, sub_keys, u_tab, v_tab), hp)
    return out.reshape(-1, D_MODEL)[:n]


def _merge_and_ffn(x, o_nsa, o_fox, merge_gates, w_up_nsa, w_up_fox, w_out, norm_ffn,
                   peer_w_query, peer_sub_keys, peer_u, peer_v):
    mixed = merge_gates[:, :, 0] * (o_nsa @ w_up_nsa) + merge_gates[:, :, 1] * (o_fox @ w_up_fox)
    x1 = x + (mixed @ w_out).astype(x.dtype)
    h2 = rms_norm(x1, norm_ffn).reshape(-1, D_MODEL)
    return x1 + peer_ffn(h2, peer_w_query, peer_sub_keys, peer_u, peer_v).reshape(x.shape)


def kernel(x_prompt, x_sample, cache_nsa, cache_fox_kv, cache_fox_logf, state_nsa_win, page_table,
           norm_attn, w_in, fox_f_bias, nsa_q_norm, nsa_k_norm, fox_q_norm, fox_k_norm,
           cmp_pe, cmp_w1, cmp_w2, w_up_nsa, w_up_fox, w_out, norm_ffn,
           peer_w_query, peer_sub_keys, peer_u, peer_v):
    slopes = alibi_slopes(NSA_HEADS)
    w_front = _front_weights(w_in)
    bd = _block_diag_mean()

    def front(x):
        b, t, _ = x.shape
        qa, rows, win, ga, qb, fox, logf, mg = _front(
            x.reshape(b * t, D_MODEL), norm_attn, w_front, bd, fox_f_bias,
            nsa_q_norm, nsa_k_norm, fox_q_norm, fox_k_norm)
        return (qa.reshape(b, t, NSA_HEADS, HEAD_DIM),
                ga[:, :NSA_GATE_W].reshape(b, t, NSA_HEADS, 3),
                rows.reshape(b, t, 4, NSA_KV_HEADS, HEAD_DIM),
                win.reshape(b, t, 2, NSA_KV_HEADS, HEAD_DIM),
                qb.reshape(b, t, FOX_HEADS, HEAD_DIM),
                fox.reshape(b, t, 2, FOX_HEADS, HEAD_DIM),
                logf.reshape(b, t, FOX_HEADS),
                mg.reshape(b, t, 2, D_MODEL))

    q_a, gates_a, nsa_p, win_rows_p, q_b, fox_p, logf_p, mg_p = front(x_prompt)
    seq = x_prompt.shape[1]
    o_nsa = nsa_prompt(q_a, gates_a, nsa_p, win_rows_p, slopes, nsa_k_norm, cmp_pe, cmp_w1, cmp_w2)
    o_fox = fox_prompt(q_b, fox_p, logf_p)
    win_p = win_rows_p[:, seq - min(WINDOW, seq):]
    wts = _peer_weights(w_up_nsa, w_up_fox, w_out, norm_ffn, peer_w_query, peer_sub_keys, peer_u, peer_v)

    def merge_peer(x, o_nsa, o_fox, mg):
        n = x.shape[0] * x.shape[1]
        return _merge_peer(x.reshape(n, D_MODEL), o_nsa.reshape(n, NSA_Q_W), o_fox.reshape(n, FOX_W),
                           mg.reshape(n, MERGE_W), wts).reshape(x.shape)

    y_p = merge_peer(x_prompt, o_nsa, o_fox, mg_p)

    q_a, gates_a, nsa_s, win_rows_s, q_b, fox_s, logf_s, mg_s = front(x_sample)
    o_nsa, win_s = nsa_sample(q_a, gates_a, nsa_s, win_rows_s, cache_nsa, state_nsa_win, page_table,
                              slopes, nsa_k_norm, cmp_pe, cmp_w1, cmp_w2)
    o_fox = fox_sample(q_b, fox_s, logf_s, cache_fox_kv, cache_fox_logf, page_table)
    y_s = merge_peer(x_sample, o_nsa, o_fox, mg_s)
    return (y_p, y_s, nsa_p, fox_p, logf_p, win_p, nsa_s, fox_s, logf_s, win_s)
```

```python
import functools

import jax
import jax.numpy as jnp
from jax import lax
from jax.experimental import pallas as pl
from jax.experimental.pallas import tpu as pltpu

D_MODEL = 1024
HEAD_DIM = 64
NSA_HEADS = 8
NSA_KV_HEADS = 2
NSA_GROUP = NSA_HEADS // NSA_KV_HEADS
CMP_BLOCK = 64
SEL_BLOCK = CMP_BLOCK
SEL_TOPK = 16
WINDOW = 512
FOX_HEADS = 8
Q_BLOCK = 128
PEER_HEADS = 8
PEER_N_KEYS = 128
PEER_DK = 256
PEER_DK_HALF = PEER_DK // 2
PEER_TOPK = 16
PEER_CHUNK = 256

NSA_Q_W = NSA_HEADS * HEAD_DIM
NSA_KV_W = 6 * NSA_KV_HEADS * HEAD_DIM
NSA_GATE_W = 3 * NSA_HEADS
FOX_W = FOX_HEADS * HEAD_DIM
FOX_QKV_W = 3 * FOX_W
FOX_F_W = FOX_HEADS
MERGE_W = 2 * D_MODEL
SPLIT_Q_A = NSA_Q_W
SPLIT_KV_A = SPLIT_Q_A + NSA_KV_W
SPLIT_G_A = SPLIT_KV_A + NSA_GATE_W
SPLIT_QKV_B = SPLIT_G_A + FOX_QKV_W
SPLIT_F_B = SPLIT_QKV_B + FOX_F_W
IN_WIDTH = SPLIT_F_B + MERGE_W

SCALE = HEAD_DIM ** -0.5
FORCE_SCORE = float(NSA_GROUP + 1)
NEG_INF = -1e30
EPS = 1e-6

LANE = 128
VMEM_LIMIT = 48 * 1024 * 1024


def _group_mean_sq(x, bd):
    sq = x * x
    hi = sq.astype(jnp.bfloat16)
    lo = (sq - hi.astype(jnp.float32)).astype(jnp.bfloat16)
    return (jnp.dot(hi, bd, preferred_element_type=jnp.float32)
            + jnp.dot(lo, bd, preferred_element_type=jnp.float32))


def _head_rms(x, g, bd):
    outs = []
    for c in range(x.shape[1] // LANE):
        xc = x[:, c * LANE:(c + 1) * LANE]
        outs.append(xc * lax.rsqrt(_group_mean_sq(xc, bd) + EPS) * g)
    return outs[0] if len(outs) == 1 else jnp.concatenate(outs, axis=1)


def _front_kernel(x_ref, na_ref, w_ref, bd_ref, fb_ref, gq_a_ref, gk_sel_ref, gk_win_ref,
                  gq_b_ref, gk_b_ref,
                  qa_ref, rows_ref, win_ref, ga_ref, qb_ref, fox_ref, logf_ref, mg_ref):
    x = x_ref[...]
    h = x * lax.rsqrt(jnp.mean(x * x, axis=-1, keepdims=True) + EPS) * na_ref[...]
    hb = h.astype(jnp.bfloat16)
    bd = bd_ref[...]

    def proj(c0, width):
        return jnp.dot(hb, w_ref[:, c0:c0 + width], preferred_element_type=jnp.float32)

    c = 0
    qa_ref[...] = _head_rms(proj(c, NSA_Q_W), gq_a_ref[...], bd)
    c += NSA_Q_W
    rows_ref[:, 0:256] = proj(c, 256)
    rows_ref[:, 256:384] = _head_rms(proj(c + 256, 128), gk_sel_ref[...], bd)
    rows_ref[:, 384:512] = proj(c + 384, 128)
    win_ref[:, 0:128] = _head_rms(proj(c + 512, 128), gk_win_ref[...], bd)
    win_ref[:, 128:256] = proj(c + 640, 128)
    c += NSA_KV_W
    qb_ref[...] = _head_rms(proj(c, FOX_W), gq_b_ref[...], bd)
    fox_ref[:, 0:FOX_W] = _head_rms(proj(c + FOX_W, FOX_W), gk_b_ref[...], bd)
    fox_ref[:, FOX_W:2 * FOX_W] = proj(c + 2 * FOX_W, FOX_W)
    c += FOX_QKV_W
    for j in range(MERGE_W // 512):
        mg_ref[:, j * 512:(j + 1) * 512] = jax.nn.sigmoid(proj(c + j * 512, 512))
    c += MERGE_W
    ga_ref[...] = jax.nn.sigmoid(proj(c, LANE))
    f = proj(c + LANE, LANE)[:, 0:FOX_F_W] + fb_ref[...]
    logf_ref[...] = jnp.minimum(f, 0.0) - jnp.log1p(jnp.exp(-jnp.abs(f)))


def _front(x2d, norm_attn, w_front, bd, fox_f_bias, nsa_q_norm, nsa_k_norm, fox_q_norm, fox_k_norm,
           tm=256):
    n = x2d.shape[0]
    wf = w_front.shape[1]
    two = lambda g: jnp.concatenate([g, g]).reshape(1, LANE)
    row = lambda w: pl.BlockSpec((tm, w), lambda i: (i, 0))
    full = lambda a: pl.BlockSpec(a.shape, lambda i: (0,) * a.ndim)
    args = (x2d, norm_attn.reshape(1, D_MODEL), w_front, bd, fox_f_bias.reshape(1, FOX_F_W),
            two(nsa_q_norm), two(nsa_k_norm[1]), two(nsa_k_norm[2]), two(fox_q_norm), two(fox_k_norm))
    widths = (NSA_Q_W, 512, 256, LANE, FOX_W, 2 * FOX_W, FOX_F_W, MERGE_W)
    return pl.pallas_call(
        _front_kernel,
        grid=(n // tm,),
        in_specs=[row(D_MODEL)] + [full(a) for a in args[1:]],
        out_specs=[row(w) for w in widths],
        out_shape=[jax.ShapeDtypeStruct((n, w), jnp.float32) for w in widths],
        compiler_params=pltpu.CompilerParams(dimension_semantics=("arbitrary",),
                                             vmem_limit_bytes=VMEM_LIMIT),
        name="front",
    )(*args)


FEAT = HEAD_DIM


def _lane(tm):
    return lax.broadcasted_iota(jnp.int32, (tm, LANE), 1)


def _expand_halves(x):
    lo = _lane(x.shape[0]) < HEAD_DIM
    return jnp.where(lo, x, 0.0), jnp.where(lo, pltpu.roll(x, HEAD_DIM, axis=1), 0.0)


def _split3(x):
    hi = x.astype(jnp.bfloat16)
    r = x - hi.astype(jnp.float32)
    mid = r.astype(jnp.bfloat16)
    lo = (r - mid.astype(jnp.float32)).astype(jnp.bfloat16)
    return hi, mid, lo


def _front_attn_kernel(seq_len, x_ref, na_ref, w_ref, bd_ref, fb_ref, gq_a_ref, gk_sel_ref, gk_win_ref,
                       gq_b_ref, gk_b_ref, place_ref,
                       rows_ref, win_ref, fox_ref, logf_ref, mg_ref,
                       qat_ref, gat_ref, ksel_ref, kwin_ref, vselt_ref, vwint_ref, qbt_ref, kb_ref, vbt_ref,
                       carry_ref):
    f32, bf16 = jnp.float32, jnp.bfloat16
    tm = x_ref.shape[0]
    i = pl.program_id(0)
    x = x_ref[...]
    h = x * lax.rsqrt(jnp.mean(x * x, axis=-1, keepdims=True) + EPS) * na_ref[...]
    hb = h.astype(bf16)
    bd = bd_ref[...]
    lane = _lane(tm)
    pos = (i * tm + lax.broadcasted_iota(jnp.int32, (tm, LANE), 0)) % seq_len
    kfeat = jnp.where(lane == FEAT, (pos // SEL_BLOCK).astype(f32),
                      jnp.where(lane == FEAT + 1, (pos % SEL_BLOCK).astype(f32), 0.0))

    def proj(c0, width):
        return jnp.dot(hb, w_ref[:, c0:c0 + width], preferred_element_type=f32)

    c = 0
    qa = _head_rms(proj(c, NSA_Q_W), gq_a_ref[...], bd)
    for j in range(NSA_HEADS // 2):
        for s, half in enumerate(_expand_halves(qa[:, j * LANE:(j + 1) * LANE])):
            hd = 2 * j + s
            slope = 2.0 ** -(hd + 1)
            qfeat = jnp.where(lane == FEAT, slope * SEL_BLOCK, jnp.where(lane == FEAT + 1, slope, 0.0))
            qat_ref[hd * LANE:(hd + 1) * LANE, :] = (half * SCALE + qfeat).T.astype(bf16)
    c += NSA_Q_W
    rows_ref[:, 0:256] = proj(c, 256)
    ksel = _head_rms(proj(c + 256, 128), gk_sel_ref[...], bd)
    rows_ref[:, 256:384] = ksel
    vsel = proj(c + 384, 128)
    rows_ref[:, 384:512] = vsel
    kwin = _head_rms(proj(c + 512, 128), gk_win_ref[...], bd)
    win_ref[:, 0:128] = kwin
    vwin = proj(c + 640, 128)
    win_ref[:, 128:256] = vwin
    for g, (ks_g, kw_g) in enumerate(zip(_expand_halves(ksel), _expand_halves(kwin))):
        ksel_ref[:, g * LANE:(g + 1) * LANE] = (ks_g + kfeat).astype(bf16)
        kwin_ref[:, g * LANE:(g + 1) * LANE] = (kw_g + kfeat).astype(bf16)
    vselt_ref[...] = vsel.T.astype(bf16)
    vwint_ref[...] = vwin.T.astype(bf16)
    c += NSA_KV_W

    qb = _head_rms(proj(c, FOX_W), gq_b_ref[...], bd)
    ones3 = jnp.where((lane >= FEAT) & (lane < FEAT + 3), 1.0, 0.0)
    for j in range(FOX_HEADS // 2):
        for s, half in enumerate(_expand_halves(qb[:, j * LANE:(j + 1) * LANE])):
            hd = 2 * j + s
            qbt_ref[hd * LANE:(hd + 1) * LANE, :] = (half * SCALE + ones3).T.astype(bf16)
    kbn = _head_rms(proj(c + FOX_W, FOX_W), gk_b_ref[...], bd)
    fox_ref[:, 0:FOX_W] = kbn
    for j in range(FOX_HEADS // 2):
        vb = proj(c + 2 * FOX_W + j * LANE, LANE)
        fox_ref[:, FOX_W + j * LANE:FOX_W + (j + 1) * LANE] = vb
        vbt_ref[j * LANE:(j + 1) * LANE, :] = vb.T.astype(bf16)
    c += FOX_QKV_W
    for j in range(MERGE_W // 512):
        mg_ref[:, j * 512:(j + 1) * 512] = jax.nn.sigmoid(proj(c + j * 512, 512))
    c += MERGE_W
    gat_ref[...] = jax.nn.sigmoid(proj(c, LANE)).T
    f = proj(c + LANE, LANE) + fb_ref[...]
    lf = jnp.minimum(f, 0.0) - jnp.log1p(jnp.exp(-jnp.abs(f)))
    logf_ref[...] = lf[:, 0:FOX_F_W]

    @pl.when((i * tm) % seq_len == 0)
    def _():
        carry_ref[...] = jnp.zeros_like(carry_ref)

    r_io = lax.broadcasted_iota(jnp.int32, (tm, tm), 0)
    c_io = lax.broadcasted_iota(jnp.int32, (tm, tm), 1)
    tri = jnp.where(c_io <= r_io, 1.0, 0.0).astype(bf16)
    csum = carry_ref[...] + sum(jnp.dot(tri, p, preferred_element_type=f32) for p in _split3(lf))
    carry_ref[...] = csum[tm - 1:tm, :]
    pieces = jnp.concatenate(_split3(-csum), axis=1)
    cfeat = jnp.dot(pieces, place_ref[...], preferred_element_type=f32)
    for j in range(FOX_HEADS // 2):
        for s, half in enumerate(_expand_halves(kbn[:, j * LANE:(j + 1) * LANE])):
            hd = 2 * j + s
            kb_ref[:, hd * LANE:(hd + 1) * LANE] = (half + cfeat[:, hd * LANE:(hd + 1) * LANE]).astype(bf16)


def _fox_feature_placement():
    r = lax.broadcasted_iota(jnp.int32, (3 * LANE, FOX_HEADS * LANE), 0)
    c = lax.broadcasted_iota(jnp.int32, (3 * LANE, FOX_HEADS * LANE), 1)
    s, hd = r // LANE, r % LANE
    return jnp.where((hd < FOX_HEADS) & (c == hd * LANE + FEAT + s), 1.0, 0.0).astype(jnp.bfloat16)


def _front_attn(x2d, seq_len, norm_attn, w_front, bd, fox_f_bias, nsa_q_norm, nsa_k_norm, fox_q_norm,
                fox_k_norm, tm=256):
    n = x2d.shape[0]
    f32, bf16 = jnp.float32, jnp.bfloat16
    two = lambda g: jnp.concatenate([g, g]).reshape(1, LANE)
    row = lambda w: pl.BlockSpec((tm, w), lambda i: (i, 0))
    col = lambda h: pl.BlockSpec((h, tm), lambda i: (0, i))
    full = lambda a: pl.BlockSpec(a.shape, lambda i: (0,) * a.ndim)
    fb = jnp.pad(fox_f_bias, (0, LANE - FOX_F_W)).reshape(1, LANE)
    args = (x2d, norm_attn.reshape(1, D_MODEL), w_front, bd, fb,
            two(nsa_q_norm), two(nsa_k_norm[1]), two(nsa_k_norm[2]), two(fox_q_norm), two(fox_k_norm),
            _fox_feature_placement())
    outs = [(row(512), (n, 512), f32), (row(256), (n, 256), f32), (row(2 * FOX_W), (n, 2 * FOX_W), f32),
            (row(FOX_F_W), (n, FOX_F_W), f32), (row(MERGE_W), (n, MERGE_W), f32),
            (col(NSA_HEADS * LANE), (NSA_HEADS * LANE, n), bf16), (col(LANE), (LANE, n), f32),
            (row(2 * LANE), (n, 2 * LANE), bf16), (row(2 * LANE), (n, 2 * LANE), bf16),
            (col(LANE), (LANE, n), bf16), (col(LANE), (LANE, n), bf16),
            (col(FOX_HEADS * LANE), (FOX_HEADS * LANE, n), bf16),
            (row(FOX_HEADS * LANE), (n, FOX_HEADS * LANE), bf16), (col(FOX_W), (FOX_W, n), bf16)]
    return pl.pallas_call(
        functools.partial(_front_attn_kernel, seq_len),
        grid=(n // tm,),
        in_specs=[row(D_MODEL)] + [full(a) for a in args[1:]],
        out_specs=[o[0] for o in outs],
        out_shape=[jax.ShapeDtypeStruct(o[1], o[2]) for o in outs],
        scratch_shapes=[pltpu.VMEM((1, LANE), f32)],
        compiler_params=pltpu.CompilerParams(dimension_semantics=("arbitrary",),
                                             vmem_limit_bytes=VMEM_LIMIT),
        name="front_attn",
    )(*args)


def _front_weights(w_in):
    pad = lambda w: jnp.pad(w, ((0, 0), (0, LANE - w.shape[1])))
    parts = [w_in[:, :SPLIT_KV_A], w_in[:, SPLIT_G_A:SPLIT_QKV_B], w_in[:, SPLIT_F_B:],
             pad(w_in[:, SPLIT_KV_A:SPLIT_G_A]), pad(w_in[:, SPLIT_QKV_B:SPLIT_F_B])]
    return jnp.concatenate(parts, axis=1).astype(jnp.bfloat16)


def _block_diag_mean():
    r = lax.broadcasted_iota(jnp.int32, (LANE, LANE), 0) // HEAD_DIM
    c = lax.broadcasted_iota(jnp.int32, (LANE, LANE), 1) // HEAD_DIM
    return jnp.where(r == c, 1.0 / HEAD_DIM, 0.0).astype(jnp.bfloat16)


def _softmax_step(s, vt, carry):
    m, l, acc = carry
    m_new = jnp.maximum(m, jnp.max(s, axis=0, keepdims=True))
    alpha = jnp.exp(m - m_new)
    p = jnp.exp(s - m_new)
    l = alpha * l + jnp.sum(p, axis=0, keepdims=True)
    acc = alpha * acc + jnp.dot(vt, p.astype(jnp.bfloat16), preferred_element_type=jnp.float32)
    return m_new, l, acc


def _softmax_init(w):
    return (jnp.full((1, w), NEG_INF, jnp.float32), jnp.zeros((1, w), jnp.float32),
            jnp.zeros((HEAD_DIM, w), jnp.float32))


def _compress_kernel(xk_ref, xv_ref, pe_ref, w1_ref, w2_ref, gk_ref, bd_ref, kc_ref, vct_ref):
    f32, bf16 = jnp.float32, jnp.bfloat16
    nb = kc_ref.shape[0]

    def summarize(s, x_ref):
        def body(l, acc):
            xl = x_ref[pl.ds(l, nb, stride=CMP_BLOCK), :] + pe_ref[s, pl.ds(l, 1), :]
            return acc + jnp.dot(xl.astype(bf16), w1_ref[s, l], preferred_element_type=f32)

        hid = jax.nn.gelu(lax.fori_loop(0, CMP_BLOCK, body, jnp.zeros((nb, LANE), f32)))
        return jnp.dot(hid.astype(bf16), w2_ref[s], preferred_element_type=f32)

    kc = summarize(0, xk_ref)
    kc = kc * lax.rsqrt(_group_mean_sq(kc, bd_ref[...]) + EPS) * gk_ref[...]
    lane = _lane(nb)
    blk = lax.broadcasted_iota(jnp.int32, (nb, LANE), 0).astype(f32)
    feat = jnp.where(lane == FEAT, blk, jnp.where(lane == FEAT + 1, float(CMP_BLOCK - 1), 0.0))
    for g, half in enumerate(_expand_halves(kc)):
        kc_ref[:, g * LANE:(g + 1) * LANE] = (half + feat).astype(bf16)
    vct_ref[...] = summarize(1, xv_ref).T.astype(bf16)


def _compress_weights(cmp_pe, cmp_w1, cmp_w2):
    def bdiag(w):
        z = jnp.zeros_like(w)
        return jnp.concatenate([jnp.concatenate([w, z], axis=-1), jnp.concatenate([z, w], axis=-1)], axis=-2)
    pe = jnp.concatenate([cmp_pe, cmp_pe], axis=-1)
    return pe, bdiag(cmp_w1).astype(jnp.bfloat16), bdiag(cmp_w2).astype(jnp.bfloat16)


def _compress(rows2d, nb, pe, w1, w2, gk, bd):
    nseq = rows2d.shape[0] // (nb * CMP_BLOCK)
    full = lambda a: pl.BlockSpec(a.shape, lambda b: (0,) * a.ndim)
    args = (rows2d, rows2d, pe, w1, w2, jnp.concatenate([gk, gk]).reshape(1, LANE), bd)
    return pl.pallas_call(
        _compress_kernel,
        grid=(nseq,),
        in_specs=[pl.BlockSpec((nb * CMP_BLOCK, LANE), lambda b: (b, 0)),
                  pl.BlockSpec((nb * CMP_BLOCK, LANE), lambda b: (b, 1))] + [full(a) for a in args[2:]],
        out_specs=[pl.BlockSpec((nb, 2 * LANE), lambda b: (b, 0)), pl.BlockSpec((LANE, nb), lambda b: (0, b))],
        out_shape=[jax.ShapeDtypeStruct((nseq * nb, 2 * LANE), jnp.bfloat16),
                   jax.ShapeDtypeStruct((LANE, nseq * nb), jnp.bfloat16)],
        compiler_params=pltpu.CompilerParams(dimension_semantics=("arbitrary",),
                                             vmem_limit_bytes=VMEM_LIMIT),
        name="nsa_compress",
    )(*args)


def _nsa_prompt_kernel(qt_ref, gat_ref, ksel_ref, vselt_ref, kwin_ref, vwint_ref, kc_ref, vct_ref,
                       o_ref, selb_ref):
    f32, bf16 = jnp.float32, jnp.bfloat16
    tq = Q_BLOCK
    w = NSA_GROUP * tq
    g = pl.program_id(1)
    qi = pl.program_id(2)
    nb = kc_ref.shape[0]
    qt = jnp.concatenate([qt_ref[r * LANE:(r + 1) * LANE, :] for r in range(NSA_GROUP)], axis=1)
    qloc = lax.broadcasted_iota(jnp.int32, (1, w), 1) % tq
    qpos = qi * tq + qloc

    sc = jnp.dot(kc_ref[...], qt, preferred_element_type=f32)
    blk = lax.broadcasted_iota(jnp.int32, (nb, w), 0)
    vis = blk * CMP_BLOCK + (CMP_BLOCK - 1) <= qpos
    sc = jnp.where(vis, sc, NEG_INF)
    pc = jnp.where(vis, jnp.exp(sc - jnp.max(sc, axis=0, keepdims=True)), 0.0)
    pc = pc / jnp.maximum(jnp.sum(pc, axis=0, keepdims=True), 1e-30)
    o_c = jnp.dot(vct_ref[...], pc.astype(bf16), preferred_element_type=f32)

    imp = sum(pc[:, r * tq:(r + 1) * tq] for r in range(NSA_GROUP))
    blk1 = blk[:, 0:tq]
    cur = qpos[:, 0:tq] // SEL_BLOCK
    imp = jnp.where((blk1 == cur) | (blk1 == 0), FORCE_SCORE, jnp.where(blk1 <= cur, imp, -1.0))
    for _ in range(min(SEL_TOPK, nb)):
        mx = jnp.max(imp, axis=0, keepdims=True)
        idx = jnp.min(jnp.where(imp == mx, blk1, BIG_ID), axis=0, keepdims=True)
        imp = jnp.where(blk1 == idx, -jnp.inf, imp)
    selb = jnp.where(imp == -jnp.inf, 0.0, NEG_INF)
    selb_ref[...] = jnp.concatenate([selb] * NSA_GROUP, axis=1)

    krow = lax.broadcasted_iota(jnp.int32, (tq, w), 0)
    causal = krow <= qloc

    def sel_bias(kt):
        per = tq // SEL_BLOCK
        return jnp.concatenate([jnp.broadcast_to(selb_ref[pl.ds(kt * per + j, 1), :], (SEL_BLOCK, w))
                                for j in range(per)], axis=0)

    def sel_tile(kt):
        k0 = pl.multiple_of(kt * tq, tq)
        s = jnp.dot(ksel_ref[pl.ds(k0, tq), :], qt, preferred_element_type=f32) + sel_bias(kt)
        return s, vselt_ref[:, pl.ds(k0, tq)]

    def sel_body(kt, carry):
        s, vt = sel_tile(kt)
        return _softmax_step(s, vt, carry)

    carry = lax.fori_loop(0, qi, sel_body, _softmax_init(w))
    s, vt = sel_tile(qi)
    m_s, l_s, acc_s = _softmax_step(jnp.where(causal, s, NEG_INF), vt, carry)
    o_s = acc_s / l_s

    carry = _softmax_init(w)
    nwin = WINDOW // tq
    for j in range(nwin + 1):
        kt = qi - nwin + j
        k0 = pl.multiple_of(jnp.maximum(kt, 0) * tq, tq)
        s = jnp.dot(kwin_ref[pl.ds(k0, tq), :], qt, preferred_element_type=f32)
        ok = kt >= 0
        if j == 0:
            ok = ok & (krow > qloc)
        elif j == nwin:
            ok = ok & causal
        carry = _softmax_step(jnp.where(ok, s, NEG_INF), vwint_ref[:, pl.ds(k0, tq)], carry)
    m_w, l_w, acc_w = carry
    o_w = acc_w / l_w

    for r in range(NSA_GROUP):
        gate = lambda br: gat_ref[pl.ds((g * NSA_GROUP + r) * 3 + br, 1), :]
        sl = slice(r * tq, (r + 1) * tq)
        o_ref[r * HEAD_DIM:(r + 1) * HEAD_DIM, :] = (
            gate(0) * o_c[:, sl] + gate(1) * o_s[:, sl] + gate(2) * o_w[:, sl])


def _nsa_prompt(qat, gat, ksel, vselt, kwin, vwint, kc, vct, nseq, seq_len):
    n = qat.shape[1]
    nq = seq_len // Q_BLOCK
    nb = seq_len // CMP_BLOCK
    gw = NSA_GROUP * LANE
    return pl.pallas_call(
        _nsa_prompt_kernel,
        grid=(nseq, NSA_KV_HEADS, nq),
        in_specs=[pl.BlockSpec((gw, Q_BLOCK), lambda b, g, q: (g, b * nq + q)),
                  pl.BlockSpec((LANE, Q_BLOCK), lambda b, g, q: (0, b * nq + q)),
                  pl.BlockSpec((seq_len, LANE), lambda b, g, q: (b, g)),
                  pl.BlockSpec((HEAD_DIM, seq_len), lambda b, g, q: (g, b)),
                  pl.BlockSpec((seq_len, LANE), lambda b, g, q: (b, g)),
                  pl.BlockSpec((HEAD_DIM, seq_len), lambda b, g, q: (g, b)),
                  pl.BlockSpec((nb, LANE), lambda b, g, q: (b, g)),
                  pl.BlockSpec((HEAD_DIM, nb), lambda b, g, q: (g, b))],
        out_specs=pl.BlockSpec((NSA_GROUP * HEAD_DIM, Q_BLOCK), lambda b, g, q: (g, b * nq + q)),
        out_shape=jax.ShapeDtypeStruct((NSA_Q_W, n), jnp.float32),
        scratch_shapes=[pltpu.VMEM((nb, NSA_GROUP * Q_BLOCK), jnp.float32)],
        compiler_params=pltpu.CompilerParams(dimension_semantics=("arbitrary", "arbitrary", "arbitrary"),
                                             vmem_limit_bytes=VMEM_LIMIT),
        name="nsa_prompt",
    )(qat, gat, ksel, vselt, kwin, vwint, kc, vct)


FOX_TQ = 512
FOX_TK = 128


def _fox_prompt_kernel(qt_ref, kb_ref, vbt_ref, o_ref):
    f32 = jnp.float32
    tq, tk = FOX_TQ, FOX_TK
    qi = pl.program_id(2)
    qt = qt_ref[...]

    def tile(kt):
        k0 = pl.multiple_of(kt * tk, tk)
        return jnp.dot(kb_ref[pl.ds(k0, tk), :], qt, preferred_element_type=f32), vbt_ref[:, pl.ds(k0, tk)]

    def body(kt, carry):
        s, vt = tile(kt)
        return _softmax_step(s, vt, carry)

    ndiag = tq // tk
    carry = lax.fori_loop(0, qi * ndiag, body, _softmax_init(tq))
    krow = lax.broadcasted_iota(jnp.int32, (tk, tq), 0)
    qloc = lax.broadcasted_iota(jnp.int32, (tk, tq), 1)
    for j in range(ndiag):
        s, vt = tile(qi * ndiag + j)
        carry = _softmax_step(jnp.where(krow + j * tk <= qloc, s, NEG_INF), vt, carry)
    m, l, acc = carry
    o_ref[...] = acc / l


def _fox_prompt(qbt, kb, vbt, nseq, seq_len):
    n = qbt.shape[1]
    nq = seq_len // FOX_TQ
    return pl.pallas_call(
        _fox_prompt_kernel,
        grid=(nseq, FOX_HEADS, nq),
        in_specs=[pl.BlockSpec((LANE, FOX_TQ), lambda b, h, q: (h, b * nq + q)),
                  pl.BlockSpec((seq_len, LANE), lambda b, h, q: (b, h)),
                  pl.BlockSpec((HEAD_DIM, seq_len), lambda b, h, q: (h, b))],
        out_specs=pl.BlockSpec((HEAD_DIM, FOX_TQ), lambda b, h, q: (h, b * nq + q)),
        out_shape=jax.ShapeDtypeStruct((FOX_W, n), jnp.float32),
        compiler_params=pltpu.CompilerParams(dimension_semantics=("arbitrary", "arbitrary", "arbitrary"),
                                             vmem_limit_bytes=VMEM_LIMIT),
        name="fox_prompt",
    )(qbt, kb, vbt)


BIG_ID = 1 << 20


def _topk_rows(s, k, ids):
    vals, idxs = [], []
    for _ in range(k):
        m = jnp.max(s, axis=0, keepdims=True)
        idx = jnp.min(jnp.where(s == m, ids, BIG_ID), axis=0, keepdims=True)
        vals.append(m)
        idxs.append(idx)
        s = jnp.where(ids == idx, -jnp.inf, s)
    return jnp.concatenate(vals, axis=0), jnp.concatenate(idxs, axis=0)


def _pick_rows(sel, table):
    out = jnp.zeros(sel.shape, table.dtype)
    for r in range(table.shape[0]):
        out = jnp.where(sel == r, table[r:r + 1, :], out)
    return out


def _merge_route_kernel(x_ref, on_ref, of_ref, mg_ref, wun_ref, wuf_ref, wo_ref, nf_ref, wqt_ref, sk_ref,
                        x1_ref, h2_ref, i1_ref, i2_ref, g_ref):
    tm = x_ref.shape[0]
    f32, bf16 = jnp.float32, jnp.bfloat16
    tdot = lambda ot, wgt: lax.dot_general(ot.astype(bf16), wgt, (((0,), (0,)), ((), ())),
                                           preferred_element_type=f32)
    a = tdot(on_ref[...], wun_ref[...])
    b = tdot(of_ref[...], wuf_ref[...])
    mixed = mg_ref[:, 0:D_MODEL] * a + mg_ref[:, D_MODEL:2 * D_MODEL] * b
    x1 = x_ref[...] + jnp.dot(mixed.astype(bf16), wo_ref[...], preferred_element_type=f32)
    x1_ref[...] = x1
    h2 = x1 * lax.rsqrt(jnp.mean(x1 * x1, axis=-1, keepdims=True) + EPS) * nf_ref[...]
    h2b = h2.astype(bf16)
    h2_ref[...] = h2b

    nk = PEER_N_KEYS
    key_ids = lax.broadcasted_iota(jnp.int32, (nk, tm), 0)
    io16 = lax.broadcasted_iota(jnp.int32, (PEER_TOPK, tm), 0)
    io8 = lax.broadcasted_iota(jnp.int32, (8, tm), 0)
    cand_ids = jnp.concatenate([io16] + [a_ * PEER_TOPK + io8 for a_ in range(1, 8)]
                               + [(io8 + 8) * PEER_TOPK], axis=0)
    for h in range(PEER_HEADS):
        sv, si = [], []
        for p in range(2):
            hp = 2 * h + p
            qt = lax.dot_general(wqt_ref[hp * PEER_DK_HALF:(hp + 1) * PEER_DK_HALF, :], h2b,
                                 (((1,), (1,)), ((), ())), preferred_element_type=f32)
            st = jnp.dot(sk_ref[hp], qt.astype(bf16), preferred_element_type=f32)
            v, i = _topk_rows(st, PEER_TOPK, key_ids)
            sv.append(v)
            si.append(i)
        s1, s2 = sv
        cand = jnp.concatenate([s1[0:1] + s2] + [s1[a_:a_ + 1] + s2[0:8] for a_ in range(1, 8)]
                               + [s1[8:16] + s2[0:1]], axis=0)
        top, fid = _topk_rows(cand, PEER_TOPK, cand_ids)
        e = jnp.exp(top - jnp.max(top, axis=0, keepdims=True))
        g = e / jnp.sum(e, axis=0, keepdims=True)
        sl = slice(h * PEER_TOPK, (h + 1) * PEER_TOPK)
        i1_ref[:, sl] = _pick_rows(fid >> 4, si[0]).T
        i2_ref[:, sl] = _pick_rows(fid & (PEER_TOPK - 1), si[1]).T
        g_ref[:, sl] = g.T


def _merge_route(x2d, o_nsa, o_fox, mg, w_up_nsa, w_up_fox, w_out, norm_ffn, wq_t, sub_keys, tm=256):
    n = x2d.shape[0]
    row = lambda w: pl.BlockSpec((tm, w), lambda i: (i, 0))
    col = lambda h: pl.BlockSpec((h, tm), lambda i: (0, i))
    full = lambda a: pl.BlockSpec(a.shape, lambda i: (0,) * a.ndim)
    args = (x2d, o_nsa, o_fox, mg, w_up_nsa, w_up_fox, w_out, norm_ffn.reshape(1, D_MODEL), wq_t, sub_keys)
    hk = PEER_HEADS * PEER_TOPK
    return pl.pallas_call(
        _merge_route_kernel,
        grid=(n // tm,),
        in_specs=[row(D_MODEL), col(NSA_Q_W), col(FOX_W), row(MERGE_W)] + [full(a) for a in args[4:]],
        out_specs=[row(D_MODEL), row(D_MODEL), row(hk), row(hk), row(hk)],
        out_shape=[jax.ShapeDtypeStruct((n, D_MODEL), jnp.float32),
                   jax.ShapeDtypeStruct((n, D_MODEL), jnp.bfloat16),
                   jax.ShapeDtypeStruct((n, hk), jnp.int32),
                   jax.ShapeDtypeStruct((n, hk), jnp.int32),
                   jax.ShapeDtypeStruct((n, hk), jnp.float32)],
        compiler_params=pltpu.CompilerParams(dimension_semantics=("arbitrary",),
                                             vmem_limit_bytes=VMEM_LIMIT),
        name="merge_route",
    )(*args)


def _peer_act_kernel(h2_ref, u_ref, i1_ref, i2_ref, act_ref):
    c = pl.program_id(1)
    ec = u_ref.shape[0]

    @pl.when(c == 0)
    def _():
        act_ref[...] = jnp.zeros_like(act_ref)

    a = lax.dot_general(h2_ref[...], u_ref[...], (((1,), (1,)), ((), ())),
                        preferred_element_type=jnp.float32)
    i1 = i1_ref[...]
    i2 = i2_ref[...]
    act = act_ref[...]
    for ii in range(ec // PEER_N_KEYS):
        got = jnp.take_along_axis(a[:, ii * PEER_N_KEYS:(ii + 1) * PEER_N_KEYS], i2, axis=1)
        act = jnp.where(i1 == c * (ec // PEER_N_KEYS) + ii, got, act)
    act_ref[...] = act


def _peer_act(h2b, u_b, i1, i2, tm=512, ec=2048):
    n = h2b.shape[0]
    hk = i1.shape[1]
    return pl.pallas_call(
        _peer_act_kernel,
        grid=(n // tm, u_b.shape[0] // ec),
        in_specs=[pl.BlockSpec((tm, D_MODEL), lambda t, c: (t, 0)),
                  pl.BlockSpec((ec, D_MODEL), lambda t, c: (c, 0)),
                  pl.BlockSpec((tm, hk), lambda t, c: (t, 0)),
                  pl.BlockSpec((tm, hk), lambda t, c: (t, 0))],
        out_specs=pl.BlockSpec((tm, hk), lambda t, c: (t, 0)),
        out_shape=jax.ShapeDtypeStruct((n, hk), jnp.float32),
        compiler_params=pltpu.CompilerParams(dimension_semantics=("arbitrary", "arbitrary"),
                                             vmem_limit_bytes=VMEM_LIMIT),
        name="peer_act",
    )(h2b, u_b, i1, i2)


def _peer_coef_kernel(act_ref, g_ref, i1_ref, i2_ref, c_ref, coef_ref):
    tm = act_ref.shape[0]
    nk = PEER_N_KEYS
    coef_ref[...] = g_ref[...] * jax.nn.gelu(act_ref[...])
    sub = lax.broadcasted_iota(jnp.int32, (nk, i1_ref.shape[1]), 0)

    def body(t, carry):
        r1 = i1_ref[pl.ds(t, 1), :]
        r2 = i2_ref[pl.ds(t, 1), :]
        cf = coef_ref[pl.ds(t, 1), :]
        m1 = jnp.where(r1 == sub, cf, 0.0).astype(jnp.bfloat16)
        m2t = jnp.where(r2 == sub, 1.0, 0.0).astype(jnp.bfloat16)
        ct = lax.dot_general(m1, m2t, (((1,), (1,)), ((), ())), preferred_element_type=jnp.float32)
        c_ref[t] = ct.astype(c_ref.dtype)
        return carry

    lax.fori_loop(0, tm, body, 0, unroll=4)


def _peer_coef(act, g, i1, i2, tm=128):
    n, hk = act.shape
    nk = PEER_N_KEYS
    row = pl.BlockSpec((tm, hk), lambda t: (t, 0))
    return pl.pallas_call(
        _peer_coef_kernel,
        grid=(n // tm,),
        in_specs=[row, row, row, row],
        out_specs=pl.BlockSpec((tm, nk, nk), lambda t: (t, 0, 0)),
        out_shape=jax.ShapeDtypeStruct((n, nk, nk), jnp.bfloat16),
        scratch_shapes=[pltpu.VMEM((tm, hk), jnp.float32)],
        compiler_params=pltpu.CompilerParams(dimension_semantics=("arbitrary",),
                                             vmem_limit_bytes=VMEM_LIMIT),
        name="peer_coef",
    )(act, g, i1, i2)


def _peer_out_kernel(c_ref, v_ref, x1_ref, y_ref, acc_ref):
    k = pl.program_id(1)

    @pl.when(k == 0)
    def _():
        acc_ref[...] = x1_ref[...]

    acc_ref[...] += jnp.dot(c_ref[...], v_ref[...], preferred_element_type=jnp.float32)

    @pl.when(k == pl.num_programs(1) - 1)
    def _():
        y_ref[...] = acc_ref[...]


def _peer_out(c2d, v_b, x1, tm=1024, tk=2048):
    n, ne = c2d.shape
    return pl.pallas_call(
        _peer_out_kernel,
        grid=(n // tm, ne // tk),
        in_specs=[pl.BlockSpec((tm, tk), lambda t, k: (t, k)),
                  pl.BlockSpec((tk, D_MODEL), lambda t, k: (k, 0)),
                  pl.BlockSpec((tm, D_MODEL), lambda t, k: (t, 0))],
        out_specs=pl.BlockSpec((tm, D_MODEL), lambda t, k: (t, 0)),
        out_shape=jax.ShapeDtypeStruct((n, D_MODEL), jnp.float32),
        scratch_shapes=[pltpu.VMEM((tm, D_MODEL), jnp.float32)],
        compiler_params=pltpu.CompilerParams(dimension_semantics=("arbitrary", "arbitrary"),
                                             vmem_limit_bytes=VMEM_LIMIT),
        name="peer_out",
    )(c2d, v_b, x1)


def _peer_weights(w_up_nsa, w_up_fox, w_out, norm_ffn, peer_w_query, peer_sub_keys, peer_u, peer_v):
    bf16 = jnp.bfloat16
    return dict(w_up_nsa=w_up_nsa.astype(bf16), w_up_fox=w_up_fox.astype(bf16), w_out=w_out.astype(bf16),
                norm_ffn=norm_ffn, wq_t=peer_w_query.T.astype(bf16),
                sub_keys=peer_sub_keys.reshape(2 * PEER_HEADS, PEER_N_KEYS, PEER_DK_HALF).astype(bf16),
                u=peer_u.astype(bf16), v=peer_v.astype(bf16))


def _merge_peer(x2d, o_nsa, o_fox, mg, wts):
    x1, h2b, i1, i2, g = _merge_route(x2d, o_nsa, o_fox, mg, wts['w_up_nsa'], wts['w_up_fox'], wts['w_out'],
                                      wts['norm_ffn'], wts['wq_t'], wts['sub_keys'])
    act = _peer_act(h2b, wts['u'], i1, i2)
    c3 = _peer_coef(act, g, i1, i2)
    return _peer_out(c3.reshape(c3.shape[0], -1), wts['v'], x1)


def rms_norm(x, g):
    xf = x.astype(jnp.float32)
    y = xf * lax.rsqrt(jnp.mean(xf * xf, axis=-1, keepdims=True) + EPS)
    return (y * g.astype(jnp.float32)).astype(x.dtype)


def masked_softmax(logits, mask):
    lf = jnp.where(mask, logits.astype(jnp.float32), NEG_INF)
    m = jnp.max(lf, axis=-1, keepdims=True)
    p = jnp.where(mask, jnp.exp(lf - m), 0.0)
    return p / jnp.maximum(jnp.sum(p, axis=-1, keepdims=True), 1e-30)


def alibi_slopes(n):
    return jnp.exp2(-8.0 * jnp.arange(1, n + 1, dtype=jnp.float32) / n)


def compress_blocks(rows, pe, w1, w2):
    b, l = rows.shape[:2]
    blk = rows.reshape(b, l // CMP_BLOCK, CMP_BLOCK, NSA_KV_HEADS, HEAD_DIM) + pe[:, None, :]
    hid = jax.nn.gelu(jnp.einsum('bnlgd,lde->bnge', blk, w1))
    return jnp.einsum('bnge,ef->bngf', hid, w2)


def nsa_block(q, gates, qpos, kc, vc, cmp_end, ks, vs, kw, vw, wpos, slopes):
    b, nq = q.shape[:2]
    nb = kc.shape[1]
    qg = q.reshape(b, nq, NSA_KV_HEADS, NSA_GROUP, HEAD_DIM)
    m = slopes.reshape(NSA_KV_HEADS, NSA_GROUP)[:, :, None, None]
    dist_c = qpos[:, None] - cmp_end[None, :]
    lc = jnp.einsum('bqgrd,bngd->bgrqn', qg, kc) * SCALE - m * dist_c.astype(jnp.float32)
    pc = masked_softmax(lc, dist_c >= 0)
    o_c = jnp.einsum('bgrqn,bngd->bqgrd', pc, vc)
    blk = jnp.arange(nb)
    cur = (qpos // SEL_BLOCK)[:, None]
    forced = (blk[None, :] == cur) | (blk[None, :] == 0)
    allowed = blk[None, :] <= cur
    imp = jnp.where(forced, FORCE_SCORE, jnp.where(allowed, pc.sum(axis=2), -1.0))
    _, idx = lax.top_k(imp, min(SEL_TOPK, nb))
    kb = ks.reshape(b, nb, SEL_BLOCK, NSA_KV_HEADS, HEAD_DIM).transpose(0, 3, 1, 2, 4)
    vb = vs.reshape(b, nb, SEL_BLOCK, NSA_KV_HEADS, HEAD_DIM).transpose(0, 3, 1, 2, 4)
    take = jax.vmap(jax.vmap(lambda blocks, i: blocks[i]))
    k_sel = take(kb, idx).reshape(b, NSA_KV_HEADS, nq, -1, HEAD_DIM)
    v_sel = take(vb, idx).reshape(b, NSA_KV_HEADS, nq, -1, HEAD_DIM)
    spos = (idx[..., None] * SEL_BLOCK + jnp.arange(SEL_BLOCK)).reshape(b, NSA_KV_HEADS, nq, -1)
    dist_s = (qpos[:, None] - spos)[:, :, None]
    ls = jnp.einsum('bqgrd,bgqsd->bgrqs', qg, k_sel) * SCALE - m * dist_s.astype(jnp.float32)
    ps = masked_softmax(ls, dist_s >= 0)
    o_s = jnp.einsum('bgrqs,bgqsd->bqgrd', ps, v_sel)
    dist_w = qpos[:, None] - wpos[None, :]
    mask_w = (dist_w >= 0) & (dist_w < WINDOW) & (wpos[None, :] >= 0)
    lw = jnp.einsum('bqgrd,bsgd->bgrqs', qg, kw) * SCALE - m * dist_w.astype(jnp.float32)
    pw = masked_softmax(lw, mask_w)
    o_w = jnp.einsum('bgrqs,bsgd->bqgrd', pw, vw)
    g = gates.reshape(b, nq, NSA_KV_HEADS, NSA_GROUP, 3)
    o = g[..., 0:1] * o_c + g[..., 1:2] * o_s + g[..., 2:3] * o_w
    return o.reshape(b, nq, NSA_Q_W)


def _compressed_kv(rows, nsa_k_norm, cmp_pe, cmp_w1, cmp_w2):
    kc = rms_norm(compress_blocks(rows[:, :, 0], cmp_pe[0], cmp_w1[0], cmp_w2[0]), nsa_k_norm[0])
    vc = compress_blocks(rows[:, :, 1], cmp_pe[1], cmp_w1[1], cmp_w2[1])
    nb = rows.shape[1] // CMP_BLOCK
    cmp_end = jnp.arange(nb) * CMP_BLOCK + (CMP_BLOCK - 1)
    return kc, vc, cmp_end


def nsa_prompt(q, gates, nsa_rows, win_rows, slopes, nsa_k_norm, cmp_pe, cmp_w1, cmp_w2):
    b, t = q.shape[:2]
    kc, vc, cmp_end = _compressed_kv(nsa_rows, nsa_k_norm, cmp_pe, cmp_w1, cmp_w2)
    ks, vs = nsa_rows[:, :, 2], nsa_rows[:, :, 3]
    win_pad = jnp.pad(win_rows, ((0, 0), (WINDOW, 0), (0, 0), (0, 0), (0, 0)))

    def one(i):
        q0 = i * Q_BLOCK
        qb = lax.dynamic_slice_in_dim(q, q0, Q_BLOCK, axis=1)
        gb = lax.dynamic_slice_in_dim(gates, q0, Q_BLOCK, axis=1)
        wb = lax.dynamic_slice_in_dim(win_pad, q0, WINDOW + Q_BLOCK, axis=1)
        wpos = q0 - WINDOW + jnp.arange(WINDOW + Q_BLOCK)
        return nsa_block(qb, gb, q0 + jnp.arange(Q_BLOCK), kc, vc, cmp_end, ks, vs,
                         wb[:, :, 0], wb[:, :, 1], wpos, slopes)

    o = lax.map(one, jnp.arange(t // Q_BLOCK))
    return o.transpose(1, 0, 2, 3).reshape(b, t, NSA_Q_W)


def nsa_sample(q, gates, new_rows, new_win, cache_nsa, state_nsa_win, page_table, slopes,
               nsa_k_norm, cmp_pe, cmp_w1, cmp_w2):
    db, ns = q.shape[:2]
    past_len = page_table.shape[1] * cache_nsa.shape[1]
    past = cache_nsa[page_table].reshape(db, past_len, 4, NSA_KV_HEADS, HEAD_DIM)
    rows = jnp.concatenate([past, new_rows], axis=1)
    pad = (-(past_len + ns)) % SEL_BLOCK
    rows = jnp.pad(rows, ((0, 0), (0, pad), (0, 0), (0, 0), (0, 0)))
    kc, vc, cmp_end = _compressed_kv(rows, nsa_k_norm, cmp_pe, cmp_w1, cmp_w2)
    wbuf = state_nsa_win.shape[1]
    wrows = jnp.concatenate([state_nsa_win, new_win], axis=1)
    wpos = past_len - wbuf + jnp.arange(wbuf + ns)
    o = nsa_block(q, gates, past_len + jnp.arange(ns), kc, vc, cmp_end, rows[:, :, 2], rows[:, :, 3],
                  wrows[:, :, 0], wrows[:, :, 1], wpos, slopes)
    return o, wrows[:, ns:]


def fox_block(q, cq, qpos, k, v, ck, kpos):
    logits = jnp.einsum('bqhd,bkhd->bhqk', q, k) * SCALE + (cq[..., :, None] - ck[..., None, :])
    p = masked_softmax(logits, kpos[None, :] <= qpos[:, None])
    o = jnp.einsum('bhqk,bkhd->bqhd', p, v)
    return o.reshape(q.shape[0], q.shape[1], FOX_W)


def fox_prompt(q, fox_rows, logf):
    b, t = q.shape[:2]
    c = jnp.cumsum(logf.astype(jnp.float32), axis=1).transpose(0, 2, 1)
    k, v = fox_rows[:, :, 0], fox_rows[:, :, 1]
    kpos = jnp.arange(t)

    def one(i):
        q0 = i * Q_BLOCK
        qb = lax.dynamic_slice_in_dim(q, q0, Q_BLOCK, axis=1)
        cq = lax.dynamic_slice_in_dim(c, q0, Q_BLOCK, axis=2)
        return fox_block(qb, cq, q0 + jnp.arange(Q_BLOCK), k, v, c, kpos)

    o = lax.map(one, jnp.arange(t // Q_BLOCK))
    return o.transpose(1, 0, 2, 3).reshape(b, t, FOX_W)


def fox_sample(q, fox_rows, logf, cache_fox_kv, cache_fox_logf, page_table):
    db, ns = q.shape[:2]
    past_len = page_table.shape[1] * cache_fox_kv.shape[1]
    kv = jnp.concatenate([cache_fox_kv[page_table].reshape(db, past_len, 2, FOX_HEADS, HEAD_DIM), fox_rows], axis=1)
    lf = jnp.concatenate([cache_fox_logf[page_table].reshape(db, past_len, FOX_HEADS).astype(jnp.float32), logf], axis=1)
    c = jnp.cumsum(lf, axis=1).transpose(0, 2, 1)
    kpos = jnp.arange(past_len + ns)
    return fox_block(q, c[:, :, past_len:], past_len + jnp.arange(ns), kv[:, :, 0], kv[:, :, 1], c, kpos)


def peer_chunk(h, w_query, sub_keys, u_tab, v_tab):
    c = h.shape[0]
    q = (h @ w_query).reshape(c, PEER_HEADS, 2, PEER_DK_HALF)
    s = jnp.einsum('chpd,hpkd->chpk', q, sub_keys).astype(jnp.float32)
    s1, i1 = lax.top_k(s[:, :, 0], PEER_TOPK)
    s2, i2 = lax.top_k(s[:, :, 1], PEER_TOPK)
    cand = (s1[..., :, None] + s2[..., None, :]).reshape(c, PEER_HEADS, -1)
    cidx = (i1[..., :, None] * PEER_N_KEYS + i2[..., None, :]).reshape(c, PEER_HEADS, -1)
    top, pos = lax.top_k(cand, PEER_TOPK)
    eidx = jnp.take_along_axis(cidx, pos, axis=-1)
    g = jax.nn.softmax(top, axis=-1)
    act = jax.nn.gelu(jnp.einsum('cd,chkd->chk', h, u_tab[eidx]))
    return jnp.einsum('chk,chkd->cd', g * act, v_tab[eidx]).astype(h.dtype)


def peer_ffn(h, w_query, sub_keys, u_tab, v_tab):
    n = h.shape[0]
    pad = (-n) % PEER_CHUNK
    hp = jnp.pad(h, ((0, pad), (0, 0))).reshape(-1, PEER_CHUNK, D_MODEL)
    out = lax.map(lambda hc: peer_chunk(hc, w_query, sub_keys, u_tab, v_tab), hp)
    return out.reshape(-1, D_MODEL)[:n]


def _merge_and_ffn(x, o_nsa, o_fox, merge_gates, w_up_nsa, w_up_fox, w_out, norm_ffn,
                   peer_w_query, peer_sub_keys, peer_u, peer_v):
    mixed = merge_gates[:, :, 0] * (o_nsa @ w_up_nsa) + merge_gates[:, :, 1] * (o_fox @ w_up_fox)
    x1 = x + (mixed @ w_out).astype(x.dtype)
    h2 = rms_norm(x1, norm_ffn).reshape(-1, D_MODEL)
    return x1 + peer_ffn(h2, peer_w_query, peer_sub_keys, peer_u, peer_v).reshape(x.shape)


def kernel(x_prompt, x_sample, cache_nsa, cache_fox_kv, cache_fox_logf, state_nsa_win, page_table,
           norm_attn, w_in, fox_f_bias, nsa_q_norm, nsa_k_norm, fox_q_norm, fox_k_norm,
           cmp_pe, cmp_w1, cmp_w2, w_up_nsa, w_up_fox, w_out, norm_ffn,
           peer_w_query, peer_sub_keys, peer_u, peer_v):
    slopes = alibi_slopes(NSA_HEADS)
    w_front = _front_weights(w_in)
    bd = _block_diag_mean()

    def front(x):
        b, t, _ = x.shape
        qa, rows, win, ga, qb, fox, logf, mg = _front(
            x.reshape(b * t, D_MODEL), norm_attn, w_front, bd, fox_f_bias,
            nsa_q_norm, nsa_k_norm, fox_q_norm, fox_k_norm)
        return (qa.reshape(b, t, NSA_HEADS, HEAD_DIM),
                ga[:, :NSA_GATE_W].reshape(b, t, NSA_HEADS, 3),
                rows.reshape(b, t, 4, NSA_KV_HEADS, HEAD_DIM),
                win.reshape(b, t, 2, NSA_KV_HEADS, HEAD_DIM),
                qb.reshape(b, t, FOX_HEADS, HEAD_DIM),
                fox.reshape(b, t, 2, FOX_HEADS, HEAD_DIM),
                logf.reshape(b, t, FOX_HEADS),
                mg.reshape(b, t, 2, D_MODEL))

    wts = _peer_weights(w_up_nsa, w_up_fox, w_out, norm_ffn, peer_w_query, peer_sub_keys, peer_u, peer_v)
    cmp_wts = _compress_weights(cmp_pe, cmp_w1, cmp_w2)

    bp, seq, _ = x_prompt.shape
    n_p = bp * seq
    (rows_p, win_p2d, fox_p2d, logf_p2d, mg_p, qat, gat, ksel, kwin, vselt, vwint, qbt, kb, vbt) = _front_attn(
        x_prompt.reshape(n_p, D_MODEL), seq, norm_attn, w_front, bd, fox_f_bias,
        nsa_q_norm, nsa_k_norm, fox_q_norm, fox_k_norm)
    kc, vct = _compress(rows_p, seq // CMP_BLOCK, *cmp_wts, nsa_k_norm[0], bd)
    o_nsa_t = _nsa_prompt(qat, gat, ksel, vselt, kwin, vwint, kc, vct, bp, seq)
    o_fox_t = _fox_prompt(qbt, kb, vbt, bp, seq)
    y_p = _merge_peer(x_prompt.reshape(n_p, D_MODEL), o_nsa_t, o_fox_t, mg_p, wts).reshape(x_prompt.shape)
    nsa_p = rows_p.reshape(bp, seq, 4, NSA_KV_HEADS, HEAD_DIM)
    fox_p = fox_p2d.reshape(bp, seq, 2, FOX_HEADS, HEAD_DIM)
    logf_p = logf_p2d.reshape(bp, seq, FOX_HEADS)
    win_p = win_p2d.reshape(bp, seq, 2, NSA_KV_HEADS, HEAD_DIM)[:, seq - min(WINDOW, seq):]

    def merge_peer(x, o_nsa, o_fox, mg):
        n = x.shape[0] * x.shape[1]
        return _merge_peer(x.reshape(n, D_MODEL), o_nsa.reshape(n, NSA_Q_W).T, o_fox.reshape(n, FOX_W).T,
                           mg.reshape(n, MERGE_W), wts).reshape(x.shape)

    q_a, gates_a, nsa_s, win_rows_s, q_b, fox_s, logf_s, mg_s = front(x_sample)
    o_nsa, win_s = nsa_sample(q_a, gates_a, nsa_s, win_rows_s, cache_nsa, state_nsa_win, page_table,
                              slopes, nsa_k_norm, cmp_pe, cmp_w1, cmp_w2)
    o_fox = fox_sample(q_b, fox_s, logf_s, cache_fox_kv, cache_fox_logf, page_table)
    y_s = merge_peer(x_sample, o_nsa, o_fox, mg_s)
    return (y_p, y_s, nsa_p, fox_p, logf_p, win_p, nsa_s, fox_s, logf_s, win_s)
```

```python
import functools

import jax
import jax.numpy as jnp
from jax import lax
from jax.experimental import pallas as pl
from jax.experimental.pallas import tpu as pltpu

D_MODEL = 1024
HEAD_DIM = 64
NSA_HEADS = 8
NSA_KV_HEADS = 2
NSA_GROUP = NSA_HEADS // NSA_KV_HEADS
CMP_BLOCK = 64
SEL_BLOCK = CMP_BLOCK
SEL_TOPK = 16
WINDOW = 512
FOX_HEADS = 8
Q_BLOCK = 128
PEER_HEADS = 8
PEER_N_KEYS = 128
PEER_DK = 256
PEER_DK_HALF = PEER_DK // 2
PEER_TOPK = 16
PEER_CHUNK = 256

NSA_Q_W = NSA_HEADS * HEAD_DIM
NSA_KV_W = 6 * NSA_KV_HEADS * HEAD_DIM
NSA_GATE_W = 3 * NSA_HEADS
FOX_W = FOX_HEADS * HEAD_DIM
FOX_QKV_W = 3 * FOX_W
FOX_F_W = FOX_HEADS
MERGE_W = 2 * D_MODEL
SPLIT_Q_A = NSA_Q_W
SPLIT_KV_A = SPLIT_Q_A + NSA_KV_W
SPLIT_G_A = SPLIT_KV_A + NSA_GATE_W
SPLIT_QKV_B = SPLIT_G_A + FOX_QKV_W
SPLIT_F_B = SPLIT_QKV_B + FOX_F_W
IN_WIDTH = SPLIT_F_B + MERGE_W

SCALE = HEAD_DIM ** -0.5
FORCE_SCORE = float(NSA_GROUP + 1)
NEG_INF = -1e30
EPS = 1e-6

LANE = 128
VMEM_LIMIT = 48 * 1024 * 1024


def _group_mean_sq(x, bd):
    sq = x * x
    hi = sq.astype(jnp.bfloat16)
    lo = (sq - hi.astype(jnp.float32)).astype(jnp.bfloat16)
    return (jnp.dot(hi, bd, preferred_element_type=jnp.float32)
            + jnp.dot(lo, bd, preferred_element_type=jnp.float32))


def _head_rms(x, g, bd):
    outs = []
    for c in range(x.shape[1] // LANE):
        xc = x[:, c * LANE:(c + 1) * LANE]
        outs.append(xc * lax.rsqrt(_group_mean_sq(xc, bd) + EPS) * g)
    return outs[0] if len(outs) == 1 else jnp.concatenate(outs, axis=1)


def _front_kernel(x_ref, na_ref, w_ref, bd_ref, fb_ref, gq_a_ref, gk_sel_ref, gk_win_ref,
                  gq_b_ref, gk_b_ref,
                  qa_ref, rows_ref, win_ref, ga_ref, qb_ref, fox_ref, logf_ref, mg_ref):
    x = x_ref[...]
    h = x * lax.rsqrt(jnp.mean(x * x, axis=-1, keepdims=True) + EPS) * na_ref[...]
    hb = h.astype(jnp.bfloat16)
    bd = bd_ref[...]

    def proj(c0, width):
        return jnp.dot(hb, w_ref[:, c0:c0 + width], preferred_element_type=jnp.float32)

    c = 0
    qa_ref[...] = _head_rms(proj(c, NSA_Q_W), gq_a_ref[...], bd)
    c += NSA_Q_W
    rows_ref[:, 0:256] = proj(c, 256)
    rows_ref[:, 256:384] = _head_rms(proj(c + 256, 128), gk_sel_ref[...], bd)
    rows_ref[:, 384:512] = proj(c + 384, 128)
    win_ref[:, 0:128] = _head_rms(proj(c + 512, 128), gk_win_ref[...], bd)
    win_ref[:, 128:256] = proj(c + 640, 128)
    c += NSA_KV_W
    qb_ref[...] = _head_rms(proj(c, FOX_W), gq_b_ref[...], bd)
    fox_ref[:, 0:FOX_W] = _head_rms(proj(c + FOX_W, FOX_W), gk_b_ref[...], bd)
    fox_ref[:, FOX_W:2 * FOX_W] = proj(c + 2 * FOX_W, FOX_W)
    c += FOX_QKV_W
    for j in range(MERGE_W // 512):
        mg_ref[:, j * 512:(j + 1) * 512] = jax.nn.sigmoid(proj(c + j * 512, 512))
    c += MERGE_W
    ga_ref[...] = jax.nn.sigmoid(proj(c, LANE))
    f = proj(c + LANE, LANE)[:, 0:FOX_F_W] + fb_ref[...]
    logf_ref[...] = jnp.minimum(f, 0.0) - jnp.log1p(jnp.exp(-jnp.abs(f)))


def _front(x2d, norm_attn, w_front, bd, fox_f_bias, nsa_q_norm, nsa_k_norm, fox_q_norm, fox_k_norm,
           tm=256):
    n = x2d.shape[0]
    wf = w_front.shape[1]
    two = lambda g: jnp.concatenate([g, g]).reshape(1, LANE)
    row = lambda w: pl.BlockSpec((tm, w), lambda i: (i, 0))
    full = lambda a: pl.BlockSpec(a.shape, lambda i: (0,) * a.ndim)
    args = (x2d, norm_attn.reshape(1, D_MODEL), w_front, bd, fox_f_bias.reshape(1, FOX_F_W),
            two(nsa_q_norm), two(nsa_k_norm[1]), two(nsa_k_norm[2]), two(fox_q_norm), two(fox_k_norm))
    widths = (NSA_Q_W, 512, 256, LANE, FOX_W, 2 * FOX_W, FOX_F_W, MERGE_W)
    return pl.pallas_call(
        _front_kernel,
        grid=(n // tm,),
        in_specs=[row(D_MODEL)] + [full(a) for a in args[1:]],
        out_specs=[row(w) for w in widths],
        out_shape=[jax.ShapeDtypeStruct((n, w), jnp.float32) for w in widths],
        compiler_params=pltpu.CompilerParams(dimension_semantics=("arbitrary",),
                                             vmem_limit_bytes=VMEM_LIMIT),
        name="front",
    )(*args)


FEAT = HEAD_DIM


def _lane(tm):
    return lax.broadcasted_iota(jnp.int32, (tm, LANE), 1)


def _expand_halves(x):
    lo = _lane(x.shape[0]) < HEAD_DIM
    return jnp.where(lo, x, 0.0), jnp.where(lo, pltpu.roll(x, HEAD_DIM, axis=1), 0.0)


def _split3(x):
    hi = x.astype(jnp.bfloat16)
    r = x - hi.astype(jnp.float32)
    mid = r.astype(jnp.bfloat16)
    lo = (r - mid.astype(jnp.float32)).astype(jnp.bfloat16)
    return hi, mid, lo


def _front_attn_kernel(seq_len, x_ref, na_ref, w_ref, bd_ref, fb_ref, gq_a_ref, gk_sel_ref, gk_win_ref,
                       gq_b_ref, gk_b_ref, place_ref,
                       rows_ref, win_ref, fox_ref, logf_ref, mg_ref,
                       qat_ref, gat_ref, ksel_ref, kwin_ref, vselt_ref, vwint_ref, qbt_ref, kb_ref, vbt_ref,
                       carry_ref):
    f32, bf16 = jnp.float32, jnp.bfloat16
    tm = x_ref.shape[0]
    i = pl.program_id(0)
    x = x_ref[...]
    h = x * lax.rsqrt(jnp.mean(x * x, axis=-1, keepdims=True) + EPS) * na_ref[...]
    hb = h.astype(bf16)
    bd = bd_ref[...]
    lane = _lane(tm)
    pos = (i * tm + lax.broadcasted_iota(jnp.int32, (tm, LANE), 0)) % seq_len
    kfeat = jnp.where(lane == FEAT, (pos // SEL_BLOCK).astype(f32),
                      jnp.where(lane == FEAT + 1, (pos % SEL_BLOCK).astype(f32), 0.0))

    def proj(c0, width):
        return jnp.dot(hb, w_ref[:, c0:c0 + width], preferred_element_type=f32)

    c = 0
    qa = _head_rms(proj(c, NSA_Q_W), gq_a_ref[...], bd)
    for j in range(NSA_HEADS // 2):
        for s, half in enumerate(_expand_halves(qa[:, j * LANE:(j + 1) * LANE])):
            hd = 2 * j + s
            slope = 2.0 ** -(hd + 1)
            qfeat = jnp.where(lane == FEAT, slope * SEL_BLOCK, jnp.where(lane == FEAT + 1, slope, 0.0))
            qat_ref[hd * LANE:(hd + 1) * LANE, :] = (half * SCALE + qfeat).T.astype(bf16)
    c += NSA_Q_W
    rows_ref[:, 0:256] = proj(c, 256)
    ksel = _head_rms(proj(c + 256, 128), gk_sel_ref[...], bd)
    rows_ref[:, 256:384] = ksel
    vsel = proj(c + 384, 128)
    rows_ref[:, 384:512] = vsel
    kwin = _head_rms(proj(c + 512, 128), gk_win_ref[...], bd)
    win_ref[:, 0:128] = kwin
    vwin = proj(c + 640, 128)
    win_ref[:, 128:256] = vwin
    for g, (ks_g, kw_g) in enumerate(zip(_expand_halves(ksel), _expand_halves(kwin))):
        ksel_ref[:, g * LANE:(g + 1) * LANE] = (ks_g + kfeat).astype(bf16)
        kwin_ref[:, g * LANE:(g + 1) * LANE] = (kw_g + kfeat).astype(bf16)
    vselt_ref[...] = vsel.T.astype(bf16)
    vwint_ref[...] = vwin.T.astype(bf16)
    c += NSA_KV_W

    qb = _head_rms(proj(c, FOX_W), gq_b_ref[...], bd)
    ones3 = jnp.where((lane >= FEAT) & (lane < FEAT + 3), 1.0, 0.0)
    for j in range(FOX_HEADS // 2):
        for s, half in enumerate(_expand_halves(qb[:, j * LANE:(j + 1) * LANE])):
            hd = 2 * j + s
            qbt_ref[hd * LANE:(hd + 1) * LANE, :] = (half * SCALE + ones3).T.astype(bf16)
    kbn = _head_rms(proj(c + FOX_W, FOX_W), gk_b_ref[...], bd)
    fox_ref[:, 0:FOX_W] = kbn
    for j in range(FOX_HEADS // 2):
        vb = proj(c + 2 * FOX_W + j * LANE, LANE)
        fox_ref[:, FOX_W + j * LANE:FOX_W + (j + 1) * LANE] = vb
        vbt_ref[j * LANE:(j + 1) * LANE, :] = vb.T.astype(bf16)
    c += FOX_QKV_W
    for j in range(MERGE_W // 512):
        mg_ref[:, j * 512:(j + 1) * 512] = jax.nn.sigmoid(proj(c + j * 512, 512))
    c += MERGE_W
    gat_ref[...] = jax.nn.sigmoid(proj(c, LANE)).T
    f = proj(c + LANE, LANE) + fb_ref[...]
    lf = jnp.minimum(f, 0.0) - jnp.log1p(jnp.exp(-jnp.abs(f)))
    logf_ref[...] = lf[:, 0:FOX_F_W]

    @pl.when((i * tm) % seq_len == 0)
    def _():
        carry_ref[...] = jnp.zeros_like(carry_ref)

    r_io = lax.broadcasted_iota(jnp.int32, (tm, tm), 0)
    c_io = lax.broadcasted_iota(jnp.int32, (tm, tm), 1)
    tri = jnp.where(c_io <= r_io, 1.0, 0.0).astype(bf16)
    csum = carry_ref[...] + sum(jnp.dot(tri, p, preferred_element_type=f32) for p in _split3(lf))
    carry_ref[...] = csum[tm - 1:tm, :]
    pieces = jnp.concatenate(_split3(-csum), axis=1)
    cfeat = jnp.dot(pieces, place_ref[...], preferred_element_type=f32)
    for j in range(FOX_HEADS // 2):
        for s, half in enumerate(_expand_halves(kbn[:, j * LANE:(j + 1) * LANE])):
            hd = 2 * j + s
            kb_ref[:, hd * LANE:(hd + 1) * LANE] = (half + cfeat[:, hd * LANE:(hd + 1) * LANE]).astype(bf16)


def _fox_feature_placement():
    r = lax.broadcasted_iota(jnp.int32, (3 * LANE, FOX_HEADS * LANE), 0)
    c = lax.broadcasted_iota(jnp.int32, (3 * LANE, FOX_HEADS * LANE), 1)
    s, hd = r // LANE, r % LANE
    return jnp.where((hd < FOX_HEADS) & (c == hd * LANE + FEAT + s), 1.0, 0.0).astype(jnp.bfloat16)


def _front_attn(x2d, seq_len, norm_attn, w_front, bd, fox_f_bias, nsa_q_norm, nsa_k_norm, fox_q_norm,
                fox_k_norm, tm=256):
    n = x2d.shape[0]
    f32, bf16 = jnp.float32, jnp.bfloat16
    two = lambda g: jnp.concatenate([g, g]).reshape(1, LANE)
    row = lambda w: pl.BlockSpec((tm, w), lambda i: (i, 0))
    col = lambda h: pl.BlockSpec((h, tm), lambda i: (0, i))
    full = lambda a: pl.BlockSpec(a.shape, lambda i: (0,) * a.ndim)
    fb = jnp.pad(fox_f_bias, (0, LANE - FOX_F_W)).reshape(1, LANE)
    args = (x2d, norm_attn.reshape(1, D_MODEL), w_front, bd, fb,
            two(nsa_q_norm), two(nsa_k_norm[1]), two(nsa_k_norm[2]), two(fox_q_norm), two(fox_k_norm),
            _fox_feature_placement())
    outs = [(row(512), (n, 512), f32), (row(256), (n, 256), f32), (row(2 * FOX_W), (n, 2 * FOX_W), f32),
            (row(FOX_F_W), (n, FOX_F_W), f32), (row(MERGE_W), (n, MERGE_W), f32),
            (col(NSA_HEADS * LANE), (NSA_HEADS * LANE, n), bf16), (col(LANE), (LANE, n), f32),
            (row(2 * LANE), (n, 2 * LANE), bf16), (row(2 * LANE), (n, 2 * LANE), bf16),
            (col(LANE), (LANE, n), bf16), (col(LANE), (LANE, n), bf16),
            (col(FOX_HEADS * LANE), (FOX_HEADS * LANE, n), bf16),
            (row(FOX_HEADS * LANE), (n, FOX_HEADS * LANE), bf16), (col(FOX_W), (FOX_W, n), bf16)]
    return pl.pallas_call(
        functools.partial(_front_attn_kernel, seq_len),
        grid=(n // tm,),
        in_specs=[row(D_MODEL)] + [full(a) for a in args[1:]],
        out_specs=[o[0] for o in outs],
        out_shape=[jax.ShapeDtypeStruct(o[1], o[2]) for o in outs],
        scratch_shapes=[pltpu.VMEM((1, LANE), f32)],
        compiler_params=pltpu.CompilerParams(dimension_semantics=("arbitrary",),
                                             vmem_limit_bytes=VMEM_LIMIT),
        name="front_attn",
    )(*args)


def _front_weights(w_in):
    pad = lambda w: jnp.pad(w, ((0, 0), (0, LANE - w.shape[1])))
    parts = [w_in[:, :SPLIT_KV_A], w_in[:, SPLIT_G_A:SPLIT_QKV_B], w_in[:, SPLIT_F_B:],
             pad(w_in[:, SPLIT_KV_A:SPLIT_G_A]), pad(w_in[:, SPLIT_QKV_B:SPLIT_F_B])]
    return jnp.concatenate(parts, axis=1).astype(jnp.bfloat16)


def _block_diag_mean():
    r = lax.broadcasted_iota(jnp.int32, (LANE, LANE), 0) // HEAD_DIM
    c = lax.broadcasted_iota(jnp.int32, (LANE, LANE), 1) // HEAD_DIM
    return jnp.where(r == c, 1.0 / HEAD_DIM, 0.0).astype(jnp.bfloat16)


def _softmax_step(tiles, carry):
    m, l, acc = carry
    m_new = functools.reduce(jnp.maximum, [jnp.max(s, axis=0, keepdims=True) for s, _ in tiles], m)
    alpha = jnp.exp(m - m_new)
    ps = [jnp.exp(s - m_new) for s, _ in tiles]
    l = alpha * l + sum(jnp.sum(p, axis=0, keepdims=True) for p in ps)
    acc = alpha * acc + sum(jnp.dot(vt, p.astype(jnp.bfloat16), preferred_element_type=jnp.float32)
                            for p, (_, vt) in zip(ps, tiles))
    return m_new, l, acc


TILE_UNROLL = 8


def _tile_loop(n, tile, carry):
    def body_u(i, c):
        return _softmax_step([tile(i * TILE_UNROLL + u) for u in range(TILE_UNROLL)], c)

    nu = n // TILE_UNROLL
    carry = lax.fori_loop(0, nu, body_u, carry)
    base = nu * TILE_UNROLL
    size = TILE_UNROLL // 2
    while size >= 1:
        has = (n & size) != 0
        carry = lax.cond(has, functools.partial(
            lambda b, sz, c: _softmax_step([tile(b + u) for u in range(sz)], c), base, size),
            lambda c: c, carry)
        base = base + jnp.where(has, size, 0)
        size //= 2
    return carry


def _softmax_init(w):
    return (jnp.full((1, w), NEG_INF, jnp.float32), jnp.zeros((1, w), jnp.float32),
            jnp.zeros((HEAD_DIM, w), jnp.float32))


def _summarize(x_ref, s, pe_ref, w1_ref, w2_ref, nb):
    f32, bf16 = jnp.float32, jnp.bfloat16

    def body(l, acc):
        xl = x_ref[pl.ds(l, nb, stride=CMP_BLOCK), :] + pe_ref[s, pl.ds(l, 1), :]
        return acc + jnp.dot(xl.astype(bf16), w1_ref[s, l], preferred_element_type=f32)

    hid = jax.nn.gelu(lax.fori_loop(0, CMP_BLOCK, body, jnp.zeros((nb, LANE), f32), unroll=8))
    return jnp.dot(hid.astype(bf16), w2_ref[s], preferred_element_type=f32)


def _compress_kernel(xk_ref, xv_ref, pe_ref, w1_ref, w2_ref, gk_ref, bd_ref, kc_ref, vct_ref):
    f32, bf16 = jnp.float32, jnp.bfloat16
    nb = kc_ref.shape[0]
    kc = _summarize(xk_ref, 0, pe_ref, w1_ref, w2_ref, nb)
    kc = kc * lax.rsqrt(_group_mean_sq(kc, bd_ref[...]) + EPS) * gk_ref[...]
    lane = _lane(nb)
    blk = lax.broadcasted_iota(jnp.int32, (nb, LANE), 0).astype(f32)
    feat = jnp.where(lane == FEAT, blk, jnp.where(lane == FEAT + 1, float(CMP_BLOCK - 1), 0.0))
    for g, half in enumerate(_expand_halves(kc)):
        kc_ref[:, g * LANE:(g + 1) * LANE] = (half + feat).astype(bf16)
    vct_ref[...] = _summarize(xv_ref, 1, pe_ref, w1_ref, w2_ref, nb).T.astype(bf16)


def _compress_weights(cmp_pe, cmp_w1, cmp_w2):
    def bdiag(w):
        z = jnp.zeros_like(w)
        return jnp.concatenate([jnp.concatenate([w, z], axis=-1), jnp.concatenate([z, w], axis=-1)], axis=-2)
    pe = jnp.concatenate([cmp_pe, cmp_pe], axis=-1)
    return pe, bdiag(cmp_w1).astype(jnp.bfloat16), bdiag(cmp_w2).astype(jnp.bfloat16)


def _compress(rows2d, nb, pe, w1, w2, gk, bd):
    nseq = rows2d.shape[0] // (nb * CMP_BLOCK)
    full = lambda a: pl.BlockSpec(a.shape, lambda b: (0,) * a.ndim)
    args = (rows2d, rows2d, pe, w1, w2, jnp.concatenate([gk, gk]).reshape(1, LANE), bd)
    return pl.pallas_call(
        _compress_kernel,
        grid=(nseq,),
        in_specs=[pl.BlockSpec((nb * CMP_BLOCK, LANE), lambda b: (b, 0)),
                  pl.BlockSpec((nb * CMP_BLOCK, LANE), lambda b: (b, 1))] + [full(a) for a in args[2:]],
        out_specs=[pl.BlockSpec((nb, 2 * LANE), lambda b: (b, 0)), pl.BlockSpec((LANE, nb), lambda b: (0, b))],
        out_shape=[jax.ShapeDtypeStruct((nseq * nb, 2 * LANE), jnp.bfloat16),
                   jax.ShapeDtypeStruct((LANE, nseq * nb), jnp.bfloat16)],
        compiler_params=pltpu.CompilerParams(dimension_semantics=("arbitrary",),
                                             vmem_limit_bytes=VMEM_LIMIT),
        name="nsa_compress",
    )(*args)


def _nsa_prompt_kernel(qt_ref, gat_ref, ksel_ref, vselt_ref, kwin_ref, vwint_ref, kc_ref, vct_ref,
                       o_ref, selb_ref):
    f32, bf16 = jnp.float32, jnp.bfloat16
    tq = Q_BLOCK
    w = NSA_GROUP * tq
    g = pl.program_id(1)
    qi = pl.program_id(2)
    nb = kc_ref.shape[0]
    qt = jnp.concatenate([qt_ref[r * LANE:(r + 1) * LANE, :] for r in range(NSA_GROUP)], axis=1)
    qloc = lax.broadcasted_iota(jnp.int32, (1, w), 1) % tq
    qpos = qi * tq + qloc

    sc = jnp.dot(kc_ref[...], qt, preferred_element_type=f32)
    blk = lax.broadcasted_iota(jnp.int32, (nb, w), 0)
    vis = blk * CMP_BLOCK + (CMP_BLOCK - 1) <= qpos
    sc = jnp.where(vis, sc, NEG_INF)
    pc = jnp.where(vis, jnp.exp(sc - jnp.max(sc, axis=0, keepdims=True)), 0.0)
    pc = pc / jnp.maximum(jnp.sum(pc, axis=0, keepdims=True), 1e-30)
    o_c = jnp.dot(vct_ref[...], pc.astype(bf16), preferred_element_type=f32)

    imp = sum(pc[:, r * tq:(r + 1) * tq] for r in range(NSA_GROUP))
    blk1 = blk[:, 0:tq]
    cur = qpos[:, 0:tq] // SEL_BLOCK
    imp = jnp.where((blk1 == cur) | (blk1 == 0), FORCE_SCORE, jnp.where(blk1 <= cur, imp, -1.0))
    for _ in range(min(SEL_TOPK, nb)):
        mx = jnp.max(imp, axis=0, keepdims=True)
        idx = jnp.min(jnp.where(imp == mx, blk1, BIG_ID), axis=0, keepdims=True)
        imp = jnp.where(blk1 == idx, -jnp.inf, imp)
    selb = jnp.where(imp == -jnp.inf, 0.0, NEG_INF)
    selb_ref[...] = jnp.concatenate([selb] * NSA_GROUP, axis=1)

    krow = lax.broadcasted_iota(jnp.int32, (tq, w), 0)
    causal = krow <= qloc

    def sel_bias(kt):
        per = tq // SEL_BLOCK
        return jnp.concatenate([jnp.broadcast_to(selb_ref[pl.ds(kt * per + j, 1), :], (SEL_BLOCK, w))
                                for j in range(per)], axis=0)

    def sel_tile(kt):
        k0 = pl.multiple_of(kt * tq, tq)
        s = jnp.dot(ksel_ref[pl.ds(k0, tq), :], qt, preferred_element_type=f32) + sel_bias(kt)
        return s, vselt_ref[:, pl.ds(k0, tq)]

    carry = _tile_loop(qi, sel_tile, _softmax_init(w))
    s, vt = sel_tile(qi)
    m_s, l_s, acc_s = _softmax_step([(jnp.where(causal, s, NEG_INF), vt)], carry)
    o_s = acc_s / l_s

    nwin = WINDOW // tq
    tiles = []
    for j in range(nwin + 1):
        kt = qi - nwin + j
        k0 = pl.multiple_of(jnp.maximum(kt, 0) * tq, tq)
        s = jnp.dot(kwin_ref[pl.ds(k0, tq), :], qt, preferred_element_type=f32)
        ok = kt >= 0
        if j == 0:
            ok = ok & (krow > qloc)
        elif j == nwin:
            ok = ok & causal
        tiles.append((jnp.where(ok, s, NEG_INF), vwint_ref[:, pl.ds(k0, tq)]))
    m_w, l_w, acc_w = _softmax_step(tiles, _softmax_init(w))
    o_w = acc_w / l_w

    for r in range(NSA_GROUP):
        gate = lambda br: gat_ref[pl.ds((g * NSA_GROUP + r) * 3 + br, 1), :]
        sl = slice(r * tq, (r + 1) * tq)
        o_ref[r * HEAD_DIM:(r + 1) * HEAD_DIM, :] = (
            gate(0) * o_c[:, sl] + gate(1) * o_s[:, sl] + gate(2) * o_w[:, sl])


def _nsa_prompt(qat, gat, ksel, vselt, kwin, vwint, kc, vct, nseq, seq_len):
    n = qat.shape[1]
    nq = seq_len // Q_BLOCK
    nb = seq_len // CMP_BLOCK
    gw = NSA_GROUP * LANE
    return pl.pallas_call(
        _nsa_prompt_kernel,
        grid=(nseq, NSA_KV_HEADS, nq),
        in_specs=[pl.BlockSpec((gw, Q_BLOCK), lambda b, g, q: (g, b * nq + q)),
                  pl.BlockSpec((LANE, Q_BLOCK), lambda b, g, q: (0, b * nq + q)),
                  pl.BlockSpec((seq_len, LANE), lambda b, g, q: (b, g)),
                  pl.BlockSpec((HEAD_DIM, seq_len), lambda b, g, q: (g, b)),
                  pl.BlockSpec((seq_len, LANE), lambda b, g, q: (b, g)),
                  pl.BlockSpec((HEAD_DIM, seq_len), lambda b, g, q: (g, b)),
                  pl.BlockSpec((nb, LANE), lambda b, g, q: (b, g)),
                  pl.BlockSpec((HEAD_DIM, nb), lambda b, g, q: (g, b))],
        out_specs=pl.BlockSpec((NSA_GROUP * HEAD_DIM, Q_BLOCK), lambda b, g, q: (g, b * nq + q)),
        out_shape=jax.ShapeDtypeStruct((NSA_Q_W, n), jnp.float32),
        scratch_shapes=[pltpu.VMEM((nb, NSA_GROUP * Q_BLOCK), jnp.float32)],
        compiler_params=pltpu.CompilerParams(dimension_semantics=("arbitrary", "arbitrary", "arbitrary"),
                                             vmem_limit_bytes=VMEM_LIMIT),
        name="nsa_prompt",
    )(qat, gat, ksel, vselt, kwin, vwint, kc, vct)


FOX_TQ = 512
FOX_TK = 128


def _fox_prompt_kernel(qt_ref, kb_ref, vbt_ref, o_ref):
    f32 = jnp.float32
    tq, tk = FOX_TQ, FOX_TK
    qi = pl.program_id(2)
    qt = qt_ref[...]

    def tile(kt):
        k0 = pl.multiple_of(kt * tk, tk)
        return jnp.dot(kb_ref[pl.ds(k0, tk), :], qt, preferred_element_type=f32), vbt_ref[:, pl.ds(k0, tk)]

    ndiag = tq // tk
    carry = _tile_loop(qi * ndiag, tile, _softmax_init(tq))
    krow = lax.broadcasted_iota(jnp.int32, (tk, tq), 0)
    qloc = lax.broadcasted_iota(jnp.int32, (tk, tq), 1)
    diag = []
    for j in range(ndiag):
        s, vt = tile(qi * ndiag + j)
        diag.append((jnp.where(krow + j * tk <= qloc, s, NEG_INF), vt))
    m, l, acc = _softmax_step(diag, carry)
    o_ref[...] = acc / l


def _fox_prompt(qbt, kb, vbt, nseq, seq_len):
    n = qbt.shape[1]
    nq = seq_len // FOX_TQ
    return pl.pallas_call(
        _fox_prompt_kernel,
        grid=(nseq, FOX_HEADS, nq),
        in_specs=[pl.BlockSpec((LANE, FOX_TQ), lambda b, h, q: (h, b * nq + q)),
                  pl.BlockSpec((seq_len, LANE), lambda b, h, q: (b, h)),
                  pl.BlockSpec((HEAD_DIM, seq_len), lambda b, h, q: (h, b))],
        out_specs=pl.BlockSpec((HEAD_DIM, FOX_TQ), lambda b, h, q: (h, b * nq + q)),
        out_shape=jax.ShapeDtypeStruct((FOX_W, n), jnp.float32),
        compiler_params=pltpu.CompilerParams(dimension_semantics=("arbitrary", "arbitrary", "arbitrary"),
                                             vmem_limit_bytes=VMEM_LIMIT),
        name="fox_prompt",
    )(qbt, kb, vbt)


def _slope_rows(shape, rows_per_head):
    hd = lax.broadcasted_iota(jnp.int32, shape, 0) // rows_per_head
    return pltpu.bitcast((126 - hd) << 23, jnp.float32)


def _pad_rows(x, rows):
    return jnp.concatenate([x, jnp.zeros((rows - x.shape[0], x.shape[1]), x.dtype)], axis=0)


def _nt_dot(a, b):
    return lax.dot_general(a, b, (((1,), (1,)), ((), ())), preferred_element_type=jnp.float32)


def _joint_softmax(parts):
    m = functools.reduce(jnp.maximum, [jnp.max(p, axis=1, keepdims=True) for p in parts])
    es = [jnp.exp(p - m) for p in parts]
    return es, sum(jnp.sum(e, axis=1, keepdims=True) for e in es)


def _nsa_sample_kernel(npages, ns, pt_ref, *refs):
    f32, bf16 = jnp.float32, jnp.bfloat16
    pages = refs[:npages]
    (e_ref, qa_ref, ga_ref, rnew_ref, wnew_ref, state_ref, pe_ref, w1_ref, w2_ref, gk_ref, bd_ref,
     o_ref, bufk_ref, bufv_ref) = refs[npages:]
    page = pages[0].shape[1]
    past = npages * page
    nb = past // CMP_BLOCK
    nrow = NSA_HEADS * ns

    for p in range(npages):
        bufk_ref[p * page:(p + 1) * page, :] = pages[p][0, :, 0:LANE]
        bufv_ref[p * page:(p + 1) * page, :] = pages[p][0, :, LANE:2 * LANE]
    kc = _summarize(bufk_ref, 0, pe_ref, w1_ref, w2_ref, nb)
    kc = (kc * lax.rsqrt(_group_mean_sq(kc, bd_ref[...]) + EPS) * gk_ref[...]).astype(bf16)
    vc = _summarize(bufv_ref, 1, pe_ref, w1_ref, w2_ref, nb).astype(bf16)

    lane8 = _lane(ns)
    ql = qa_ref[...]
    qrows = []
    for hd in range(NSA_HEADS):
        g = hd // NSA_GROUP
        t = ql[:, (hd // 2) * LANE:(hd // 2 + 1) * LANE]
        if hd % 2 != g:
            t = pltpu.roll(t, HEAD_DIM, axis=1)
        qrows.append(jnp.where(lane8 // HEAD_DIM == g, t, 0.0))
    qb = (jnp.concatenate(qrows, axis=0) * SCALE).astype(bf16)

    def geom(width):
        tok = lax.broadcasted_iota(jnp.int32, (nrow, width), 0) % ns
        col = lax.broadcasted_iota(jnp.int32, (nrow, width), 1)
        return tok, col, _slope_rows((nrow, width), ns)

    tok, col, slope = geom(nb)
    lc = _nt_dot(qb, kc) - slope * (past + tok - (col * CMP_BLOCK + CMP_BLOCK - 1)).astype(f32)
    (ec,), lsum = _joint_softmax([lc])
    pc = ec / lsum
    o_c = jnp.dot(pc.astype(bf16), vc, preferred_element_type=f32)

    imp = jnp.concatenate([sum(pc[(g * NSA_GROUP + r) * ns:(g * NSA_GROUP + r + 1) * ns] for r in range(NSA_GROUP))
                           for g in range(NSA_KV_HEADS)], axis=0)
    bcol = lax.broadcasted_iota(jnp.int32, imp.shape, 1)
    imp = jnp.where(bcol == 0, FORCE_SCORE, imp)
    for _ in range(min(SEL_TOPK - 1, nb)):
        mx = jnp.max(imp, axis=1, keepdims=True)
        idx = jnp.min(jnp.where(imp == mx, bcol, BIG_ID), axis=1, keepdims=True)
        imp = jnp.where(bcol == idx, -jnp.inf, imp)
    sel = jnp.where(imp == -jnp.inf, 1.0, 0.0)
    sel = jnp.concatenate([sel[g * ns:(g + 1) * ns] for g in range(NSA_KV_HEADS) for _ in range(NSA_GROUP)], axis=0)
    selexp = jnp.dot(sel.astype(bf16), e_ref[...], preferred_element_type=f32)

    def new_tile(k_new):
        tok, col, slope = geom(LANE)
        s = _nt_dot(qb, _pad_rows(k_new, LANE).astype(bf16))
        return jnp.where(col <= tok, s - slope * (tok - col).astype(f32), NEG_INF)

    def weighted(es, vs, lsum):
        acc = sum(jnp.dot(e.astype(bf16), v, preferred_element_type=f32) for e, v in zip(es, vs))
        return acc / lsum

    tok, col, slope = geom(past)
    ls = jnp.concatenate([_nt_dot(qb, pages[p][0, :, 2 * LANE:3 * LANE].astype(bf16)) for p in range(npages)],
                         axis=1)
    ls = jnp.where(selexp > 0.5, ls - slope * (past + tok - col).astype(f32), NEG_INF)
    (es, en), lsum = _joint_softmax([ls, new_tile(rnew_ref[:, 2 * LANE:3 * LANE])])
    o_s = weighted([es[:, p * page:(p + 1) * page] for p in range(npages)] + [en],
                   [pages[p][0, :, 3 * LANE:4 * LANE].astype(bf16) for p in range(npages)]
                   + [_pad_rows(rnew_ref[:, 3 * LANE:4 * LANE], LANE).astype(bf16)], lsum)

    wbuf = state_ref.shape[1]
    tok, col, slope = geom(wbuf)
    lw = _nt_dot(qb, state_ref[0, :, 0:LANE].astype(bf16))
    lw = jnp.where(col > tok + (wbuf - WINDOW), lw - slope * (wbuf + tok - col).astype(f32), NEG_INF)
    (ew, en), lsum = _joint_softmax([lw, new_tile(wnew_ref[:, 0:LANE])])
    o_w = weighted([ew, en], [state_ref[0, :, LANE:2 * LANE].astype(bf16),
                              _pad_rows(wnew_ref[:, LANE:2 * LANE], LANE).astype(bf16)], lsum)

    ga = ga_ref[...]
    gate = lambda br: jnp.concatenate(
        [jnp.broadcast_to(ga[:, hd * 3 + br:hd * 3 + br + 1], (ns, LANE)) for hd in range(NSA_HEADS)], axis=0)
    o = gate(0) * o_c + gate(1) * o_s + gate(2) * o_w
    for j in range(NSA_HEADS // 2):
        g = (2 * j) // NSA_GROUP
        a = o[2 * j * ns:(2 * j + 1) * ns]
        b = o[(2 * j + 1) * ns:(2 * j + 2) * ns]
        if g == 0:
            b = pltpu.roll(b, HEAD_DIM, axis=1)
        else:
            a = pltpu.roll(a, HEAD_DIM, axis=1)
        o_ref[:, j * LANE:(j + 1) * LANE] = jnp.where(lane8 < HEAD_DIM, a, b)


def _nsa_sample(page_table, cache2d, qa, ga, rows_new, win_new, state2d, pe, w1, w2, gk, bd):
    db, npages = page_table.shape
    page = cache2d.shape[1]
    ns = qa.shape[0] // db
    past = npages * page
    nb = past // CMP_BLOCK
    r = lax.broadcasted_iota(jnp.int32, (nb, past), 0)
    c = lax.broadcasted_iota(jnp.int32, (nb, past), 1)
    expand = jnp.where(c // SEL_BLOCK == r, 1.0, 0.0).astype(jnp.bfloat16)
    gk2 = jnp.concatenate([gk, gk]).reshape(1, LANE)
    full = lambda a: pl.BlockSpec(a.shape, lambda b, pt: (0,) * a.ndim)
    tok = lambda w: pl.BlockSpec((ns, w), lambda b, pt: (b, 0))
    page_specs = [pl.BlockSpec((1, page, cache2d.shape[2]), functools.partial(lambda p, b, pt: (pt[b, p], 0, 0), p))
                  for p in range(npages)]
    return pl.pallas_call(
        functools.partial(_nsa_sample_kernel, npages, ns),
        grid_spec=pltpu.PrefetchScalarGridSpec(
            num_scalar_prefetch=1, grid=(db,),
            in_specs=page_specs + [full(expand), tok(NSA_Q_W), tok(LANE), tok(512), tok(256),
                                   pl.BlockSpec((1,) + state2d.shape[1:], lambda b, pt: (b, 0, 0)),
                                   full(pe), full(w1), full(w2), full(gk2), full(bd)],
            out_specs=tok(NSA_Q_W),
            scratch_shapes=[pltpu.VMEM((past, LANE), jnp.float32), pltpu.VMEM((past, LANE), jnp.float32)]),
        out_shape=jax.ShapeDtypeStruct((db * ns, NSA_Q_W), jnp.float32),
        compiler_params=pltpu.CompilerParams(dimension_semantics=("arbitrary",),
                                             vmem_limit_bytes=VMEM_LIMIT),
        name="nsa_sample",
    )(page_table, *([cache2d] * npages), expand, qa, ga, rows_new, win_new, state2d, pe, w1, w2, gk2, bd)


def _fox_sample_kernel(npages, ns, pt_ref, *refs):
    f32, bf16 = jnp.float32, jnp.bfloat16
    kv = refs[:npages]
    lft = refs[npages:2 * npages]
    qb_ref, knew_ref, lfnew_ref, o_ref = refs[2 * npages:]
    page = kv[0].shape[1]
    nrow = FOX_HEADS * ns

    r_io = lax.broadcasted_iota(jnp.int32, (page, page), 0)
    c_io = lax.broadcasted_iota(jnp.int32, (page, page), 1)
    triu = jnp.where(r_io <= c_io, 1.0, 0.0).astype(bf16)
    carry = jnp.zeros((FOX_HEADS, 1), f32)
    negc = []
    for t in range(npages + 1):
        lf = lft[t][0] if t < npages else lfnew_ref[0]
        ct = carry + sum(jnp.dot(pc_, triu, preferred_element_type=f32) for pc_ in _split3(lf))
        carry = ct[:, page - 1:page]
        negc.append(jnp.concatenate([jnp.broadcast_to(-ct[hd:hd + 1], (ns, page)) for hd in range(FOX_HEADS)],
                                    axis=0))

    lane8 = _lane(ns)
    ql = qb_ref[...] * SCALE
    tok = lax.broadcasted_iota(jnp.int32, (2 * ns, page), 0) % ns
    col = lax.broadcasted_iota(jnp.int32, (2 * ns, page), 1)
    for j in range(FOX_HEADS // 2):
        t = ql[:, j * LANE:(j + 1) * LANE]
        qj = jnp.concatenate([jnp.where(lane8 < HEAD_DIM, t, 0.0), jnp.where(lane8 >= HEAD_DIM, t, 0.0)],
                             axis=0).astype(bf16)
        ksl = slice(j * LANE, (j + 1) * LANE)
        vsl = slice(FOX_W + j * LANE, FOX_W + (j + 1) * LANE)
        rows = slice(2 * j * ns, (2 * j + 2) * ns)
        parts = [_nt_dot(qj, kv[p][0, :, ksl].astype(bf16)) + negc[p][rows] for p in range(npages)]
        s_new = _nt_dot(qj, _pad_rows(knew_ref[:, ksl], page).astype(bf16)) + negc[npages][rows]
        parts.append(jnp.where(col <= tok, s_new, NEG_INF))
        es, lsum = _joint_softmax(parts)
        vs = [kv[p][0, :, vsl].astype(bf16) for p in range(npages)] + [_pad_rows(knew_ref[:, vsl], page).astype(bf16)]
        acc = sum(jnp.dot(e.astype(bf16), v, preferred_element_type=f32) for e, v in zip(es, vs)) / lsum
        o_ref[:, j * LANE:(j + 1) * LANE] = jnp.where(lane8 < HEAD_DIM, acc[0:ns], acc[ns:2 * ns])


def _fox_sample(page_table, kv2d, lft, qb, fox_new, lft_new):
    db, npages = page_table.shape
    page = kv2d.shape[1]
    ns = qb.shape[0] // db
    tok = lambda w: pl.BlockSpec((ns, w), lambda b, pt: (b, 0))
    pg = lambda a: [pl.BlockSpec((1,) + a.shape[1:], functools.partial(lambda p, b, pt: (pt[b, p], 0, 0), p))
                    for p in range(npages)]
    return pl.pallas_call(
        functools.partial(_fox_sample_kernel, npages, ns),
        grid_spec=pltpu.PrefetchScalarGridSpec(
            num_scalar_prefetch=1, grid=(db,),
            in_specs=pg(kv2d) + pg(lft) + [tok(FOX_W), tok(2 * FOX_W),
                                            pl.BlockSpec((1, FOX_HEADS, page), lambda b, pt: (b, 0, 0))],
            out_specs=tok(FOX_W)),
        out_shape=jax.ShapeDtypeStruct((db * ns, FOX_W), jnp.float32),
        compiler_params=pltpu.CompilerParams(dimension_semantics=("arbitrary",),
                                             vmem_limit_bytes=VMEM_LIMIT),
        name="fox_sample",
    )(page_table, *([kv2d] * npages), *([lft] * npages), qb, fox_new, lft_new)


BIG_ID = 1 << 20


def _topk_rows(s, k, ids):
    vals, idxs = [], []
    for _ in range(k):
        m = jnp.max(s, axis=0, keepdims=True)
        idx = jnp.min(jnp.where(s == m, ids, BIG_ID), axis=0, keepdims=True)
        vals.append(m)
        idxs.append(idx)
        s = jnp.where(ids == idx, -jnp.inf, s)
    return jnp.concatenate(vals, axis=0), jnp.concatenate(idxs, axis=0)


def _pick_rows(sel, table):
    out = jnp.zeros(sel.shape, table.dtype)
    for r in range(table.shape[0]):
        out = jnp.where(sel == r, table[r:r + 1, :], out)
    return out


def _merge_route_kernel(x_ref, on_ref, of_ref, mg_ref, wun_ref, wuf_ref, wo_ref, nf_ref, wqt_ref, sk_ref,
                        x1_ref, h2_ref, i1_ref, i2_ref, g_ref):
    tm = x_ref.shape[0]
    f32, bf16 = jnp.float32, jnp.bfloat16
    tdot = lambda ot, wgt: lax.dot_general(ot.astype(bf16), wgt, (((0,), (0,)), ((), ())),
                                           preferred_element_type=f32)
    a = tdot(on_ref[...], wun_ref[...])
    b = tdot(of_ref[...], wuf_ref[...])
    mixed = mg_ref[:, 0:D_MODEL] * a + mg_ref[:, D_MODEL:2 * D_MODEL] * b
    x1 = x_ref[...] + jnp.dot(mixed.astype(bf16), wo_ref[...], preferred_element_type=f32)
    x1_ref[...] = x1
    h2 = x1 * lax.rsqrt(jnp.mean(x1 * x1, axis=-1, keepdims=True) + EPS) * nf_ref[...]
    h2b = h2.astype(bf16)
    h2_ref[...] = h2b

    nk = PEER_N_KEYS
    key_ids = lax.broadcasted_iota(jnp.int32, (nk, tm), 0)
    io16 = lax.broadcasted_iota(jnp.int32, (PEER_TOPK, tm), 0)
    io8 = lax.broadcasted_iota(jnp.int32, (8, tm), 0)
    cand_ids = jnp.concatenate([io16] + [a_ * PEER_TOPK + io8 for a_ in range(1, 8)]
                               + [(io8 + 8) * PEER_TOPK], axis=0)
    for h in range(PEER_HEADS):
        sv, si = [], []
        for p in range(2):
            hp = 2 * h + p
            qt = lax.dot_general(wqt_ref[hp * PEER_DK_HALF:(hp + 1) * PEER_DK_HALF, :], h2b,
                                 (((1,), (1,)), ((), ())), preferred_element_type=f32)
            st = jnp.dot(sk_ref[hp], qt.astype(bf16), preferred_element_type=f32)
            v, i = _topk_rows(st, PEER_TOPK, key_ids)
            sv.append(v)
            si.append(i)
        s1, s2 = sv
        cand = jnp.concatenate([s1[0:1] + s2] + [s1[a_:a_ + 1] + s2[0:8] for a_ in range(1, 8)]
                               + [s1[8:16] + s2[0:1]], axis=0)
        top, fid = _topk_rows(cand, PEER_TOPK, cand_ids)
        e = jnp.exp(top - jnp.max(top, axis=0, keepdims=True))
        g = e / jnp.sum(e, axis=0, keepdims=True)
        sl = slice(h * PEER_TOPK, (h + 1) * PEER_TOPK)
        i1_ref[:, sl] = _pick_rows(fid >> 4, si[0]).T
        i2_ref[:, sl] = _pick_rows(fid & (PEER_TOPK - 1), si[1]).T
        g_ref[:, sl] = g.T


def _merge_route(x2d, o_nsa, o_fox, mg, w_up_nsa, w_up_fox, w_out, norm_ffn, wq_t, sub_keys, tm=256):
    n = x2d.shape[0]
    row = lambda w: pl.BlockSpec((tm, w), lambda i: (i, 0))
    col = lambda h: pl.BlockSpec((h, tm), lambda i: (0, i))
    full = lambda a: pl.BlockSpec(a.shape, lambda i: (0,) * a.ndim)
    args = (x2d, o_nsa, o_fox, mg, w_up_nsa, w_up_fox, w_out, norm_ffn.reshape(1, D_MODEL), wq_t, sub_keys)
    hk = PEER_HEADS * PEER_TOPK
    return pl.pallas_call(
        _merge_route_kernel,
        grid=(n // tm,),
        in_specs=[row(D_MODEL), col(NSA_Q_W), col(FOX_W), row(MERGE_W)] + [full(a) for a in args[4:]],
        out_specs=[row(D_MODEL), row(D_MODEL), row(hk), row(hk), row(hk)],
        out_shape=[jax.ShapeDtypeStruct((n, D_MODEL), jnp.float32),
                   jax.ShapeDtypeStruct((n, D_MODEL), jnp.bfloat16),
                   jax.ShapeDtypeStruct((n, hk), jnp.int32),
                   jax.ShapeDtypeStruct((n, hk), jnp.int32),
                   jax.ShapeDtypeStruct((n, hk), jnp.float32)],
        compiler_params=pltpu.CompilerParams(dimension_semantics=("arbitrary",),
                                             vmem_limit_bytes=VMEM_LIMIT),
        name="merge_route",
    )(*args)


def _peer_act_kernel(h2_ref, u_ref, i1_ref, i2_ref, act_ref):
    c = pl.program_id(1)
    ec = u_ref.shape[0]

    @pl.when(c == 0)
    def _():
        act_ref[...] = jnp.zeros_like(act_ref)

    a = lax.dot_general(h2_ref[...], u_ref[...], (((1,), (1,)), ((), ())),
                        preferred_element_type=jnp.float32)
    i1 = i1_ref[...]
    i2 = i2_ref[...]
    act = act_ref[...]
    for ii in range(ec // PEER_N_KEYS):
        got = jnp.take_along_axis(a[:, ii * PEER_N_KEYS:(ii + 1) * PEER_N_KEYS], i2, axis=1)
        act = jnp.where(i1 == c * (ec // PEER_N_KEYS) + ii, got, act)
    act_ref[...] = act


def _peer_act(h2b, u_b, i1, i2, tm=512, ec=2048):
    n = h2b.shape[0]
    hk = i1.shape[1]
    return pl.pallas_call(
        _peer_act_kernel,
        grid=(n // tm, u_b.shape[0] // ec),
        in_specs=[pl.BlockSpec((tm, D_MODEL), lambda t, c: (t, 0)),
                  pl.BlockSpec((ec, D_MODEL), lambda t, c: (c, 0)),
                  pl.BlockSpec((tm, hk), lambda t, c: (t, 0)),
                  pl.BlockSpec((tm, hk), lambda t, c: (t, 0))],
        out_specs=pl.BlockSpec((tm, hk), lambda t, c: (t, 0)),
        out_shape=jax.ShapeDtypeStruct((n, hk), jnp.float32),
        compiler_params=pltpu.CompilerParams(dimension_semantics=("arbitrary", "arbitrary"),
                                             vmem_limit_bytes=VMEM_LIMIT),
        name="peer_act",
    )(h2b, u_b, i1, i2)


def _peer_coef_kernel(act_ref, g_ref, i1_ref, i2_ref, c_ref, coef_ref):
    tm = act_ref.shape[0]
    nk = PEER_N_KEYS
    coef_ref[...] = g_ref[...] * jax.nn.gelu(act_ref[...])
    sub = lax.broadcasted_iota(jnp.int32, (nk, i1_ref.shape[1]), 0)

    def body(t, carry):
        r1 = i1_ref[pl.ds(t, 1), :]
        r2 = i2_ref[pl.ds(t, 1), :]
        cf = coef_ref[pl.ds(t, 1), :]
        m1 = jnp.where(r1 == sub, cf, 0.0).astype(jnp.bfloat16)
        m2t = jnp.where(r2 == sub, 1.0, 0.0).astype(jnp.bfloat16)
        ct = lax.dot_general(m1, m2t, (((1,), (1,)), ((), ())), preferred_element_type=jnp.float32)
        c_ref[t] = ct.astype(c_ref.dtype)
        return carry

    lax.fori_loop(0, tm, body, 0, unroll=4)


def _peer_coef(act, g, i1, i2, tm=128):
    n, hk = act.shape
    nk = PEER_N_KEYS
    row = pl.BlockSpec((tm, hk), lambda t: (t, 0))
    return pl.pallas_call(
        _peer_coef_kernel,
        grid=(n // tm,),
        in_specs=[row, row, row, row],
        out_specs=pl.BlockSpec((tm, nk, nk), lambda t: (t, 0, 0)),
        out_shape=jax.ShapeDtypeStruct((n, nk, nk), jnp.bfloat16),
        scratch_shapes=[pltpu.VMEM((tm, hk), jnp.float32)],
        compiler_params=pltpu.CompilerParams(dimension_semantics=("arbitrary",),
                                             vmem_limit_bytes=VMEM_LIMIT),
        name="peer_coef",
    )(act, g, i1, i2)


def _peer_out_kernel(c_ref, v_ref, x1_ref, y_ref, acc_ref):
    k = pl.program_id(1)

    @pl.when(k == 0)
    def _():
        acc_ref[...] = x1_ref[...]

    acc_ref[...] += jnp.dot(c_ref[...], v_ref[...], preferred_element_type=jnp.float32)

    @pl.when(k == pl.num_programs(1) - 1)
    def _():
        y_ref[...] = acc_ref[...]


def _peer_out(c2d, v_b, x1, tm=1024, tk=2048):
    n, ne = c2d.shape
    return pl.pallas_call(
        _peer_out_kernel,
        grid=(n // tm, ne // tk),
        in_specs=[pl.BlockSpec((tm, tk), lambda t, k: (t, k)),
                  pl.BlockSpec((tk, D_MODEL), lambda t, k: (k, 0)),
                  pl.BlockSpec((tm, D_MODEL), lambda t, k: (t, 0))],
        out_specs=pl.BlockSpec((tm, D_MODEL), lambda t, k: (t, 0)),
        out_shape=jax.ShapeDtypeStruct((n, D_MODEL), jnp.float32),
        scratch_shapes=[pltpu.VMEM((tm, D_MODEL), jnp.float32)],
        compiler_params=pltpu.CompilerParams(dimension_semantics=("arbitrary", "arbitrary"),
                                             vmem_limit_bytes=VMEM_LIMIT),
        name="peer_out",
    )(c2d, v_b, x1)


def _peer_weights(w_up_nsa, w_up_fox, w_out, norm_ffn, peer_w_query, peer_sub_keys, peer_u, peer_v):
    bf16 = jnp.bfloat16
    return dict(w_up_nsa=w_up_nsa.astype(bf16), w_up_fox=w_up_fox.astype(bf16), w_out=w_out.astype(bf16),
                norm_ffn=norm_ffn, wq_t=peer_w_query.T.astype(bf16),
                sub_keys=peer_sub_keys.reshape(2 * PEER_HEADS, PEER_N_KEYS, PEER_DK_HALF).astype(bf16),
                u=peer_u.astype(bf16), v=peer_v.astype(bf16))


def _merge_peer(x2d, o_nsa, o_fox, mg, wts):
    x1, h2b, i1, i2, g = _merge_route(x2d, o_nsa, o_fox, mg, wts['w_up_nsa'], wts['w_up_fox'], wts['w_out'],
                                      wts['norm_ffn'], wts['wq_t'], wts['sub_keys'])
    act = _peer_act(h2b, wts['u'], i1, i2)
    c3 = _peer_coef(act, g, i1, i2)
    return _peer_out(c3.reshape(c3.shape[0], -1), wts['v'], x1)


def kernel(x_prompt, x_sample, cache_nsa, cache_fox_kv, cache_fox_logf, state_nsa_win, page_table,
           norm_attn, w_in, fox_f_bias, nsa_q_norm, nsa_k_norm, fox_q_norm, fox_k_norm,
           cmp_pe, cmp_w1, cmp_w2, w_up_nsa, w_up_fox, w_out, norm_ffn,
           peer_w_query, peer_sub_keys, peer_u, peer_v):
    w_front = _front_weights(w_in)
    bd = _block_diag_mean()
    wts = _peer_weights(w_up_nsa, w_up_fox, w_out, norm_ffn, peer_w_query, peer_sub_keys, peer_u, peer_v)
    cmp_wts = _compress_weights(cmp_pe, cmp_w1, cmp_w2)

    bp, seq, _ = x_prompt.shape
    n_p = bp * seq
    (rows_p, win_p2d, fox_p2d, logf_p2d, mg_p, qat, gat, ksel, kwin, vselt, vwint, qbt, kb, vbt) = _front_attn(
        x_prompt.reshape(n_p, D_MODEL), seq, norm_attn, w_front, bd, fox_f_bias,
        nsa_q_norm, nsa_k_norm, fox_q_norm, fox_k_norm)
    kc, vct = _compress(rows_p, seq // CMP_BLOCK, *cmp_wts, nsa_k_norm[0], bd)
    o_nsa_t = _nsa_prompt(qat, gat, ksel, vselt, kwin, vwint, kc, vct, bp, seq)
    o_fox_t = _fox_prompt(qbt, kb, vbt, bp, seq)
    y_p = _merge_peer(x_prompt.reshape(n_p, D_MODEL), o_nsa_t, o_fox_t, mg_p, wts).reshape(x_prompt.shape)
    nsa_p = rows_p.reshape(bp, seq, 4, NSA_KV_HEADS, HEAD_DIM)
    fox_p = fox_p2d.reshape(bp, seq, 2, FOX_HEADS, HEAD_DIM)
    logf_p = logf_p2d.reshape(bp, seq, FOX_HEADS)
    win_p = win_p2d.reshape(bp, seq, 2, NSA_KV_HEADS, HEAD_DIM)[:, seq - min(WINDOW, seq):]

    db, ns, _ = x_sample.shape
    n_s = db * ns
    n_pool, page = cache_nsa.shape[:2]
    wbuf = state_nsa_win.shape[1]
    qa_s, rows_s, win_s2d, ga_s, qb_s, fox_s2d, logf_s2d, mg_s = _front(
        x_sample.reshape(n_s, D_MODEL), norm_attn, w_front, bd, fox_f_bias,
        nsa_q_norm, nsa_k_norm, fox_q_norm, fox_k_norm)
    o_nsa_s = _nsa_sample(page_table, cache_nsa.reshape(n_pool, page, 4 * LANE), qa_s, ga_s, rows_s, win_s2d,
                          state_nsa_win.reshape(db, wbuf, 2 * LANE), *cmp_wts, nsa_k_norm[0], bd)
    lft_new = jnp.pad(logf_s2d.reshape(db, ns, FOX_HEADS).transpose(0, 2, 1), ((0, 0), (0, 0), (0, page - ns)))
    o_fox_s = _fox_sample(page_table, cache_fox_kv.reshape(n_pool, page, 2 * FOX_W),
                          cache_fox_logf.transpose(0, 2, 1), qb_s, fox_s2d, lft_new)
    y_s = _merge_peer(x_sample.reshape(n_s, D_MODEL), o_nsa_s.T, o_fox_s.T, mg_s, wts).reshape(x_sample.shape)
    nsa_s = rows_s.reshape(db, ns, 4, NSA_KV_HEADS, HEAD_DIM)
    fox_s = fox_s2d.reshape(db, ns, 2, FOX_HEADS, HEAD_DIM)
    logf_s = logf_s2d.reshape(db, ns, FOX_HEADS)
    win_s = jnp.concatenate([state_nsa_win[:, ns:], win_s2d.reshape(db, ns, 2, NSA_KV_HEADS, HEAD_DIM)], axis=1)
    return (y_p, y_s, nsa_p, fox_p, logf_p, win_p, nsa_s, fox_s, logf_s, win_s)
```

```python
import functools

import jax
import jax.numpy as jnp
from jax import lax
from jax.experimental import pallas as pl
from jax.experimental.pallas import tpu as pltpu

D_MODEL = 1024
HEAD_DIM = 64
NSA_HEADS = 8
NSA_KV_HEADS = 2
NSA_GROUP = NSA_HEADS // NSA_KV_HEADS
CMP_BLOCK = 64
SEL_BLOCK = CMP_BLOCK
SEL_TOPK = 16
WINDOW = 512
FOX_HEADS = 8
Q_BLOCK = 128
PEER_HEADS = 8
PEER_N_KEYS = 128
PEER_DK = 256
PEER_DK_HALF = PEER_DK // 2
PEER_TOPK = 16
PEER_CHUNK = 256

NSA_Q_W = NSA_HEADS * HEAD_DIM
NSA_KV_W = 6 * NSA_KV_HEADS * HEAD_DIM
NSA_GATE_W = 3 * NSA_HEADS
FOX_W = FOX_HEADS * HEAD_DIM
FOX_QKV_W = 3 * FOX_W
FOX_F_W = FOX_HEADS
MERGE_W = 2 * D_MODEL
SPLIT_Q_A = NSA_Q_W
SPLIT_KV_A = SPLIT_Q_A + NSA_KV_W
SPLIT_G_A = SPLIT_KV_A + NSA_GATE_W
SPLIT_QKV_B = SPLIT_G_A + FOX_QKV_W
SPLIT_F_B = SPLIT_QKV_B + FOX_F_W
IN_WIDTH = SPLIT_F_B + MERGE_W

SCALE = HEAD_DIM ** -0.5
FORCE_SCORE = float(NSA_GROUP + 1)
NEG_INF = -1e30
EPS = 1e-6

LANE = 128
VMEM_LIMIT = 48 * 1024 * 1024


def _group_mean_sq(x, bd):
    sq = x * x
    hi = sq.astype(jnp.bfloat16)
    lo = (sq - hi.astype(jnp.float32)).astype(jnp.bfloat16)
    return (jnp.dot(hi, bd, preferred_element_type=jnp.float32)
            + jnp.dot(lo, bd, preferred_element_type=jnp.float32))


def _head_rms(x, g, bd):
    outs = []
    for c in range(x.shape[1] // LANE):
        xc = x[:, c * LANE:(c + 1) * LANE]
        outs.append(xc * lax.rsqrt(_group_mean_sq(xc, bd) + EPS) * g)
    return outs[0] if len(outs) == 1 else jnp.concatenate(outs, axis=1)


def _front_kernel(x_ref, na_ref, w_ref, bd_ref, fb_ref, gq_a_ref, gk_sel_ref, gk_win_ref,
                  gq_b_ref, gk_b_ref,
                  qa_ref, rows_ref, win_ref, ga_ref, qb_ref, fox_ref, logf_ref, mg_ref):
    x = x_ref[...]
    h = x * lax.rsqrt(jnp.mean(x * x, axis=-1, keepdims=True) + EPS) * na_ref[...]
    hb = h.astype(jnp.bfloat16)
    bd = bd_ref[...]

    def proj(c0, width):
        return jnp.dot(hb, w_ref[:, c0:c0 + width], preferred_element_type=jnp.float32)

    c = 0
    qa_ref[...] = _head_rms(proj(c, NSA_Q_W), gq_a_ref[...], bd)
    c += NSA_Q_W
    rows_ref[:, 0:256] = proj(c, 256)
    rows_ref[:, 256:384] = _head_rms(proj(c + 256, 128), gk_sel_ref[...], bd)
    rows_ref[:, 384:512] = proj(c + 384, 128)
    win_ref[:, 0:128] = _head_rms(proj(c + 512, 128), gk_win_ref[...], bd)
    win_ref[:, 128:256] = proj(c + 640, 128)
    c += NSA_KV_W
    qb_ref[...] = _head_rms(proj(c, FOX_W), gq_b_ref[...], bd)
    fox_ref[:, 0:FOX_W] = _head_rms(proj(c + FOX_W, FOX_W), gk_b_ref[...], bd)
    fox_ref[:, FOX_W:2 * FOX_W] = proj(c + 2 * FOX_W, FOX_W)
    c += FOX_QKV_W
    for j in range(MERGE_W // 512):
        mg_ref[:, j * 512:(j + 1) * 512] = jax.nn.sigmoid(proj(c + j * 512, 512))
    c += MERGE_W
    ga_ref[...] = jax.nn.sigmoid(proj(c, LANE))
    f = proj(c + LANE, LANE)[:, 0:FOX_F_W] + fb_ref[...]
    logf_ref[...] = jnp.minimum(f, 0.0) - jnp.log1p(jnp.exp(-jnp.abs(f)))


def _front(x2d, norm_attn, w_front, bd, fox_f_bias, nsa_q_norm, nsa_k_norm, fox_q_norm, fox_k_norm,
           tm=256):
    n = x2d.shape[0]
    wf = w_front.shape[1]
    two = lambda g: jnp.concatenate([g, g]).reshape(1, LANE)
    row = lambda w: pl.BlockSpec((tm, w), lambda i: (i, 0))
    full = lambda a: pl.BlockSpec(a.shape, lambda i: (0,) * a.ndim)
    args = (x2d, norm_attn.reshape(1, D_MODEL), w_front, bd, fox_f_bias.reshape(1, FOX_F_W),
            two(nsa_q_norm), two(nsa_k_norm[1]), two(nsa_k_norm[2]), two(fox_q_norm), two(fox_k_norm))
    widths = (NSA_Q_W, 512, 256, LANE, FOX_W, 2 * FOX_W, FOX_F_W, MERGE_W)
    return pl.pallas_call(
        _front_kernel,
        grid=(n // tm,),
        in_specs=[row(D_MODEL)] + [full(a) for a in args[1:]],
        out_specs=[row(w) for w in widths],
        out_shape=[jax.ShapeDtypeStruct((n, w), jnp.float32) for w in widths],
        compiler_params=pltpu.CompilerParams(dimension_semantics=("arbitrary",),
                                             vmem_limit_bytes=VMEM_LIMIT),
        name="front",
    )(*args)


FEAT = HEAD_DIM


def _lane(tm):
    return lax.broadcasted_iota(jnp.int32, (tm, LANE), 1)


def _expand_halves(x):
    lo = _lane(x.shape[0]) < HEAD_DIM
    return jnp.where(lo, x, 0.0), jnp.where(lo, pltpu.roll(x, HEAD_DIM, axis=1), 0.0)


def _split3(x):
    hi = x.astype(jnp.bfloat16)
    r = x - hi.astype(jnp.float32)
    mid = r.astype(jnp.bfloat16)
    lo = (r - mid.astype(jnp.float32)).astype(jnp.bfloat16)
    return hi, mid, lo


def _front_attn_kernel(seq_len, x_ref, na_ref, w_ref, bd_ref, fb_ref, gq_a_ref, gk_sel_ref, gk_win_ref,
                       gq_b_ref, gk_b_ref, place_ref,
                       rows_ref, win_ref, fox_ref, logf_ref, mg_ref,
                       qat_ref, gat_ref, ksel_ref, kwin_ref, vselt_ref, vwint_ref, qbt_ref, kb_ref, vbt_ref,
                       carry_ref):
    f32, bf16 = jnp.float32, jnp.bfloat16
    tm = x_ref.shape[0]
    i = pl.program_id(0)
    x = x_ref[...]
    h = x * lax.rsqrt(jnp.mean(x * x, axis=-1, keepdims=True) + EPS) * na_ref[...]
    hb = h.astype(bf16)
    bd = bd_ref[...]
    lane = _lane(tm)
    pos = (i * tm + lax.broadcasted_iota(jnp.int32, (tm, LANE), 0)) % seq_len
    kfeat = jnp.where(lane == FEAT, (pos // SEL_BLOCK).astype(f32),
                      jnp.where(lane == FEAT + 1, (pos % SEL_BLOCK).astype(f32), 0.0))

    def proj(c0, width):
        return jnp.dot(hb, w_ref[:, c0:c0 + width], preferred_element_type=f32)

    c = 0
    qa = _head_rms(proj(c, NSA_Q_W), gq_a_ref[...], bd)
    for j in range(NSA_HEADS // 2):
        for s, half in enumerate(_expand_halves(qa[:, j * LANE:(j + 1) * LANE])):
            hd = 2 * j + s
            slope = 2.0 ** -(hd + 1)
            qfeat = jnp.where(lane == FEAT, slope * SEL_BLOCK, jnp.where(lane == FEAT + 1, slope, 0.0))
            qat_ref[hd * LANE:(hd + 1) * LANE, :] = (half * SCALE + qfeat).T.astype(bf16)
    c += NSA_Q_W
    rows_ref[:, 0:256] = proj(c, 256)
    ksel = _head_rms(proj(c + 256, 128), gk_sel_ref[...], bd)
    rows_ref[:, 256:384] = ksel
    vsel = proj(c + 384, 128)
    rows_ref[:, 384:512] = vsel
    kwin = _head_rms(proj(c + 512, 128), gk_win_ref[...], bd)
    win_ref[:, 0:128] = kwin
    vwin = proj(c + 640, 128)
    win_ref[:, 128:256] = vwin
    for g, (ks_g, kw_g) in enumerate(zip(_expand_halves(ksel), _expand_halves(kwin))):
        ksel_ref[:, g * LANE:(g + 1) * LANE] = (ks_g + kfeat).astype(bf16)
        kwin_ref[:, g * LANE:(g + 1) * LANE] = (kw_g + kfeat).astype(bf16)
    vselt_ref[...] = vsel.T.astype(bf16)
    vwint_ref[...] = vwin.T.astype(bf16)
    c += NSA_KV_W

    qb = _head_rms(proj(c, FOX_W), gq_b_ref[...], bd)
    ones3 = jnp.where((lane >= FEAT) & (lane < FEAT + 3), 1.0, 0.0)
    for j in range(FOX_HEADS // 2):
        for s, half in enumerate(_expand_halves(qb[:, j * LANE:(j + 1) * LANE])):
            hd = 2 * j + s
            qbt_ref[hd * LANE:(hd + 1) * LANE, :] = (half * SCALE + ones3).T.astype(bf16)
    kbn = _head_rms(proj(c + FOX_W, FOX_W), gk_b_ref[...], bd)
    fox_ref[:, 0:FOX_W] = kbn
    for j in range(FOX_HEADS // 2):
        vb = proj(c + 2 * FOX_W + j * LANE, LANE)
        fox_ref[:, FOX_W + j * LANE:FOX_W + (j + 1) * LANE] = vb
        vbt_ref[j * LANE:(j + 1) * LANE, :] = vb.T.astype(bf16)
    c += FOX_QKV_W
    for j in range(MERGE_W // 512):
        mg_ref[:, j * 512:(j + 1) * 512] = jax.nn.sigmoid(proj(c + j * 512, 512))
    c += MERGE_W
    gat_ref[...] = jax.nn.sigmoid(proj(c, LANE)).T
    f = proj(c + LANE, LANE) + fb_ref[...]
    lf = jnp.minimum(f, 0.0) - jnp.log1p(jnp.exp(-jnp.abs(f)))
    logf_ref[...] = lf[:, 0:FOX_F_W]

    @pl.when((i * tm) % seq_len == 0)
    def _():
        carry_ref[...] = jnp.zeros_like(carry_ref)

    r_io = lax.broadcasted_iota(jnp.int32, (tm, tm), 0)
    c_io = lax.broadcasted_iota(jnp.int32, (tm, tm), 1)
    tri = jnp.where(c_io <= r_io, 1.0, 0.0).astype(bf16)
    csum = carry_ref[...] + sum(jnp.dot(tri, p, preferred_element_type=f32) for p in _split3(lf))
    carry_ref[...] = csum[tm - 1:tm, :]
    pieces = jnp.concatenate(_split3(-csum), axis=1)
    cfeat = jnp.dot(pieces, place_ref[...], preferred_element_type=f32)
    for j in range(FOX_HEADS // 2):
        for s, half in enumerate(_expand_halves(kbn[:, j * LANE:(j + 1) * LANE])):
            hd = 2 * j + s
            kb_ref[:, hd * LANE:(hd + 1) * LANE] = (half + cfeat[:, hd * LANE:(hd + 1) * LANE]).astype(bf16)


def _fox_feature_placement():
    r = lax.broadcasted_iota(jnp.int32, (3 * LANE, FOX_HEADS * LANE), 0)
    c = lax.broadcasted_iota(jnp.int32, (3 * LANE, FOX_HEADS * LANE), 1)
    s, hd = r // LANE, r % LANE
    return jnp.where((hd < FOX_HEADS) & (c == hd * LANE + FEAT + s), 1.0, 0.0).astype(jnp.bfloat16)


def _front_attn(x2d, seq_len, norm_attn, w_front, bd, fox_f_bias, nsa_q_norm, nsa_k_norm, fox_q_norm,
                fox_k_norm, tm=256):
    n = x2d.shape[0]
    f32, bf16 = jnp.float32, jnp.bfloat16
    two = lambda g: jnp.concatenate([g, g]).reshape(1, LANE)
    row = lambda w: pl.BlockSpec((tm, w), lambda i: (i, 0))
    col = lambda h: pl.BlockSpec((h, tm), lambda i: (0, i))
    full = lambda a: pl.BlockSpec(a.shape, lambda i: (0,) * a.ndim)
    fb = jnp.pad(fox_f_bias, (0, LANE - FOX_F_W)).reshape(1, LANE)
    args = (x2d, norm_attn.reshape(1, D_MODEL), w_front, bd, fb,
            two(nsa_q_norm), two(nsa_k_norm[1]), two(nsa_k_norm[2]), two(fox_q_norm), two(fox_k_norm),
            _fox_feature_placement())
    outs = [(row(512), (n, 512), f32), (row(256), (n, 256), f32), (row(2 * FOX_W), (n, 2 * FOX_W), f32),
            (row(FOX_F_W), (n, FOX_F_W), f32), (row(MERGE_W), (n, MERGE_W), f32),
            (col(NSA_HEADS * LANE), (NSA_HEADS * LANE, n), bf16), (col(LANE), (LANE, n), f32),
            (row(2 * LANE), (n, 2 * LANE), bf16), (row(2 * LANE), (n, 2 * LANE), bf16),
            (col(LANE), (LANE, n), bf16), (col(LANE), (LANE, n), bf16),
            (col(FOX_HEADS * LANE), (FOX_HEADS * LANE, n), bf16),
            (row(FOX_HEADS * LANE), (n, FOX_HEADS * LANE), bf16), (col(FOX_W), (FOX_W, n), bf16)]
    return pl.pallas_call(
        functools.partial(_front_attn_kernel, seq_len),
        grid=(n // tm,),
        in_specs=[row(D_MODEL)] + [full(a) for a in args[1:]],
        out_specs=[o[0] for o in outs],
        out_shape=[jax.ShapeDtypeStruct(o[1], o[2]) for o in outs],
        scratch_shapes=[pltpu.VMEM((1, LANE), f32)],
        compiler_params=pltpu.CompilerParams(dimension_semantics=("arbitrary",),
                                             vmem_limit_bytes=VMEM_LIMIT),
        name="front_attn",
    )(*args)


def _front_weights(w_in):
    pad = lambda w: jnp.pad(w, ((0, 0), (0, LANE - w.shape[1])))
    parts = [w_in[:, :SPLIT_KV_A], w_in[:, SPLIT_G_A:SPLIT_QKV_B], w_in[:, SPLIT_F_B:],
             pad(w_in[:, SPLIT_KV_A:SPLIT_G_A]), pad(w_in[:, SPLIT_QKV_B:SPLIT_F_B])]
    return jnp.concatenate(parts, axis=1).astype(jnp.bfloat16)


def _block_diag_mean():
    r = lax.broadcasted_iota(jnp.int32, (LANE, LANE), 0) // HEAD_DIM
    c = lax.broadcasted_iota(jnp.int32, (LANE, LANE), 1) // HEAD_DIM
    return jnp.where(r == c, 1.0 / HEAD_DIM, 0.0).astype(jnp.bfloat16)


def _softmax_step(tiles, carry):
    m, l, acc = carry
    m_new = functools.reduce(jnp.maximum, [jnp.max(s, axis=0, keepdims=True) for s, _ in tiles], m)
    alpha = jnp.exp(m - m_new)
    ps = [jnp.exp(s - m_new) for s, _ in tiles]
    l = alpha * l + sum(jnp.sum(p, axis=0, keepdims=True) for p in ps)
    acc = alpha * acc + sum(jnp.dot(vt, p.astype(jnp.bfloat16), preferred_element_type=jnp.float32)
                            for p, (_, vt) in zip(ps, tiles))
    return m_new, l, acc


TILE_UNROLL = 8


def _tile_loop(n, tile, carry):
    def body_u(i, c):
        return _softmax_step([tile(i * TILE_UNROLL + u) for u in range(TILE_UNROLL)], c)

    nu = n // TILE_UNROLL
    carry = lax.fori_loop(0, nu, body_u, carry)
    base = nu * TILE_UNROLL
    size = TILE_UNROLL // 2
    while size >= 1:
        has = (n & size) != 0
        carry = lax.cond(has, functools.partial(
            lambda b, sz, c: _softmax_step([tile(b + u) for u in range(sz)], c), base, size),
            lambda c: c, carry)
        base = base + jnp.where(has, size, 0)
        size //= 2
    return carry


def _softmax_init(w):
    return (jnp.full((1, w), NEG_INF, jnp.float32), jnp.zeros((1, w), jnp.float32),
            jnp.zeros((HEAD_DIM, w), jnp.float32))


def _summarize(x_ref, s, pe_ref, w1_ref, w2_ref, nb):
    f32, bf16 = jnp.float32, jnp.bfloat16

    def body(l, acc):
        xl = x_ref[pl.ds(l, nb, stride=CMP_BLOCK), :] + pe_ref[s, pl.ds(l, 1), :]
        return acc + jnp.dot(xl.astype(bf16), w1_ref[s, l], preferred_element_type=f32)

    hid = jax.nn.gelu(lax.fori_loop(0, CMP_BLOCK, body, jnp.zeros((nb, LANE), f32), unroll=8))
    return jnp.dot(hid.astype(bf16), w2_ref[s], preferred_element_type=f32)


def _compress_kernel(xk_ref, xv_ref, pe_ref, w1_ref, w2_ref, gk_ref, bd_ref, kc_ref, vct_ref):
    f32, bf16 = jnp.float32, jnp.bfloat16
    nb = kc_ref.shape[0]
    kc = _summarize(xk_ref, 0, pe_ref, w1_ref, w2_ref, nb)
    kc = kc * lax.rsqrt(_group_mean_sq(kc, bd_ref[...]) + EPS) * gk_ref[...]
    lane = _lane(nb)
    blk = lax.broadcasted_iota(jnp.int32, (nb, LANE), 0).astype(f32)
    feat = jnp.where(lane == FEAT, blk, jnp.where(lane == FEAT + 1, float(CMP_BLOCK - 1), 0.0))
    for g, half in enumerate(_expand_halves(kc)):
        kc_ref[:, g * LANE:(g + 1) * LANE] = (half + feat).astype(bf16)
    vct_ref[...] = _summarize(xv_ref, 1, pe_ref, w1_ref, w2_ref, nb).T.astype(bf16)


def _compress_weights(cmp_pe, cmp_w1, cmp_w2):
    def bdiag(w):
        z = jnp.zeros_like(w)
        return jnp.concatenate([jnp.concatenate([w, z], axis=-1), jnp.concatenate([z, w], axis=-1)], axis=-2)
    pe = jnp.concatenate([cmp_pe, cmp_pe], axis=-1)
    return pe, bdiag(cmp_w1).astype(jnp.bfloat16), bdiag(cmp_w2).astype(jnp.bfloat16)


def _compress(rows2d, nb, pe, w1, w2, gk, bd):
    nseq = rows2d.shape[0] // (nb * CMP_BLOCK)
    full = lambda a: pl.BlockSpec(a.shape, lambda b: (0,) * a.ndim)
    args = (rows2d, rows2d, pe, w1, w2, jnp.concatenate([gk, gk]).reshape(1, LANE), bd)
    return pl.pallas_call(
        _compress_kernel,
        grid=(nseq,),
        in_specs=[pl.BlockSpec((nb * CMP_BLOCK, LANE), lambda b: (b, 0)),
                  pl.BlockSpec((nb * CMP_BLOCK, LANE), lambda b: (b, 1))] + [full(a) for a in args[2:]],
        out_specs=[pl.BlockSpec((nb, 2 * LANE), lambda b: (b, 0)), pl.BlockSpec((LANE, nb), lambda b: (0, b))],
        out_shape=[jax.ShapeDtypeStruct((nseq * nb, 2 * LANE), jnp.bfloat16),
                   jax.ShapeDtypeStruct((LANE, nseq * nb), jnp.bfloat16)],
        compiler_params=pltpu.CompilerParams(dimension_semantics=("arbitrary",),
                                             vmem_limit_bytes=VMEM_LIMIT),
        name="nsa_compress",
    )(*args)


def _nsa_prompt_kernel(qt_ref, gat_ref, ksel_ref, vselt_ref, kwin_ref, vwint_ref, kc_ref, vct_ref,
                       o_ref, selb_ref):
    f32, bf16 = jnp.float32, jnp.bfloat16
    tq = Q_BLOCK
    w = NSA_GROUP * tq
    g = pl.program_id(1)
    qi = pl.program_id(2)
    nb = kc_ref.shape[0]
    qt = jnp.concatenate([qt_ref[r * LANE:(r + 1) * LANE, :] for r in range(NSA_GROUP)], axis=1)
    qloc = lax.broadcasted_iota(jnp.int32, (1, w), 1) % tq
    qpos = qi * tq + qloc

    sc = jnp.dot(kc_ref[...], qt, preferred_element_type=f32)
    blk = lax.broadcasted_iota(jnp.int32, (nb, w), 0)
    vis = blk * CMP_BLOCK + (CMP_BLOCK - 1) <= qpos
    sc = jnp.where(vis, sc, NEG_INF)
    pc = jnp.where(vis, jnp.exp(sc - jnp.max(sc, axis=0, keepdims=True)), 0.0)
    pc = pc / jnp.maximum(jnp.sum(pc, axis=0, keepdims=True), 1e-30)
    o_c = jnp.dot(vct_ref[...], pc.astype(bf16), preferred_element_type=f32)

    imp = sum(pc[:, r * tq:(r + 1) * tq] for r in range(NSA_GROUP))
    blk1 = blk[:, 0:tq]
    cur = qpos[:, 0:tq] // SEL_BLOCK
    imp = jnp.where((blk1 == cur) | (blk1 == 0), FORCE_SCORE, jnp.where(blk1 <= cur, imp, -1.0))
    for _ in range(min(SEL_TOPK, nb)):
        mx = jnp.max(imp, axis=0, keepdims=True)
        idx = jnp.min(jnp.where(imp == mx, blk1, BIG_ID), axis=0, keepdims=True)
        imp = jnp.where(blk1 == idx, -jnp.inf, imp)
    selb = jnp.where(imp == -jnp.inf, 0.0, NEG_INF)
    selb_ref[...] = jnp.concatenate([selb] * NSA_GROUP, axis=1)

    krow = lax.broadcasted_iota(jnp.int32, (tq, w), 0)
    causal = krow <= qloc

    def sel_bias(kt):
        per = tq // SEL_BLOCK
        return jnp.concatenate([jnp.broadcast_to(selb_ref[pl.ds(kt * per + j, 1), :], (SEL_BLOCK, w))
                                for j in range(per)], axis=0)

    def sel_tile(kt):
        k0 = pl.multiple_of(kt * tq, tq)
        s = jnp.dot(ksel_ref[pl.ds(k0, tq), :], qt, preferred_element_type=f32) + sel_bias(kt)
        return s, vselt_ref[:, pl.ds(k0, tq)]

    carry = _tile_loop(qi, sel_tile, _softmax_init(w))
    s, vt = sel_tile(qi)
    m_s, l_s, acc_s = _softmax_step([(jnp.where(causal, s, NEG_INF), vt)], carry)
    o_s = acc_s / l_s

    nwin = WINDOW // tq
    tiles = []
    for j in range(nwin + 1):
        kt = qi - nwin + j
        k0 = pl.multiple_of(jnp.maximum(kt, 0) * tq, tq)
        s = jnp.dot(kwin_ref[pl.ds(k0, tq), :], qt, preferred_element_type=f32)
        ok = kt >= 0
        if j == 0:
            ok = ok & (krow > qloc)
        elif j == nwin:
            ok = ok & causal
        tiles.append((jnp.where(ok, s, NEG_INF), vwint_ref[:, pl.ds(k0, tq)]))
    m_w, l_w, acc_w = _softmax_step(tiles, _softmax_init(w))
    o_w = acc_w / l_w

    for r in range(NSA_GROUP):
        gate = lambda br: gat_ref[pl.ds((g * NSA_GROUP + r) * 3 + br, 1), :]
        sl = slice(r * tq, (r + 1) * tq)
        o_ref[r * HEAD_DIM:(r + 1) * HEAD_DIM, :] = (
            gate(0) * o_c[:, sl] + gate(1) * o_s[:, sl] + gate(2) * o_w[:, sl])


def _nsa_prompt(qat, gat, ksel, vselt, kwin, vwint, kc, vct, nseq, seq_len):
    n = qat.shape[1]
    nq = seq_len // Q_BLOCK
    nb = seq_len // CMP_BLOCK
    gw = NSA_GROUP * LANE
    return pl.pallas_call(
        _nsa_prompt_kernel,
        grid=(nseq, NSA_KV_HEADS, nq),
        in_specs=[pl.BlockSpec((gw, Q_BLOCK), lambda b, g, q: (g, b * nq + q)),
                  pl.BlockSpec((LANE, Q_BLOCK), lambda b, g, q: (0, b * nq + q)),
                  pl.BlockSpec((seq_len, LANE), lambda b, g, q: (b, g)),
                  pl.BlockSpec((HEAD_DIM, seq_len), lambda b, g, q: (g, b)),
                  pl.BlockSpec((seq_len, LANE), lambda b, g, q: (b, g)),
                  pl.BlockSpec((HEAD_DIM, seq_len), lambda b, g, q: (g, b)),
                  pl.BlockSpec((nb, LANE), lambda b, g, q: (b, g)),
                  pl.BlockSpec((HEAD_DIM, nb), lambda b, g, q: (g, b))],
        out_specs=pl.BlockSpec((NSA_GROUP * HEAD_DIM, Q_BLOCK), lambda b, g, q: (g, b * nq + q)),
        out_shape=jax.ShapeDtypeStruct((NSA_Q_W, n), jnp.float32),
        scratch_shapes=[pltpu.VMEM((nb, NSA_GROUP * Q_BLOCK), jnp.float32)],
        compiler_params=pltpu.CompilerParams(dimension_semantics=("arbitrary", "arbitrary", "arbitrary"),
                                             vmem_limit_bytes=VMEM_LIMIT),
        name="nsa_prompt",
    )(qat, gat, ksel, vselt, kwin, vwint, kc, vct)


FOX_TQ = 512
FOX_TK = 128


def _fox_prompt_kernel(qt_ref, kb_ref, vbt_ref, o_ref):
    f32 = jnp.float32
    tq, tk = FOX_TQ, FOX_TK
    qi = pl.program_id(2)
    qt = qt_ref[...]

    def tile(kt):
        k0 = pl.multiple_of(kt * tk, tk)
        return jnp.dot(kb_ref[pl.ds(k0, tk), :], qt, preferred_element_type=f32), vbt_ref[:, pl.ds(k0, tk)]

    ndiag = tq // tk
    carry = _tile_loop(qi * ndiag, tile, _softmax_init(tq))
    krow = lax.broadcasted_iota(jnp.int32, (tk, tq), 0)
    qloc = lax.broadcasted_iota(jnp.int32, (tk, tq), 1)
    diag = []
    for j in range(ndiag):
        s, vt = tile(qi * ndiag + j)
        diag.append((jnp.where(krow + j * tk <= qloc, s, NEG_INF), vt))
    m, l, acc = _softmax_step(diag, carry)
    o_ref[...] = acc / l


def _fox_prompt(qbt, kb, vbt, nseq, seq_len):
    n = qbt.shape[1]
    nq = seq_len // FOX_TQ
    return pl.pallas_call(
        _fox_prompt_kernel,
        grid=(nseq, FOX_HEADS, nq),
        in_specs=[pl.BlockSpec((LANE, FOX_TQ), lambda b, h, q: (h, b * nq + q)),
                  pl.BlockSpec((seq_len, LANE), lambda b, h, q: (b, h)),
                  pl.BlockSpec((HEAD_DIM, seq_len), lambda b, h, q: (h, b))],
        out_specs=pl.BlockSpec((HEAD_DIM, FOX_TQ), lambda b, h, q: (h, b * nq + q)),
        out_shape=jax.ShapeDtypeStruct((FOX_W, n), jnp.float32),
        compiler_params=pltpu.CompilerParams(dimension_semantics=("arbitrary", "arbitrary", "arbitrary"),
                                             vmem_limit_bytes=VMEM_LIMIT),
        name="fox_prompt",
    )(qbt, kb, vbt)


def _slope_rows(shape, rows_per_head):
    hd = lax.broadcasted_iota(jnp.int32, shape, 0) // rows_per_head
    return pltpu.bitcast((126 - hd) << 23, jnp.float32)


def _pad_rows(x, rows):
    return jnp.concatenate([x, jnp.zeros((rows - x.shape[0], x.shape[1]), x.dtype)], axis=0)


def _nt_dot(a, b):
    return lax.dot_general(a, b, (((1,), (1,)), ((), ())), preferred_element_type=jnp.float32)


def _joint_softmax(parts):
    m = functools.reduce(jnp.maximum, [jnp.max(p, axis=1, keepdims=True) for p in parts])
    es = [jnp.exp(p - m) for p in parts]
    return es, sum(jnp.sum(e, axis=1, keepdims=True) for e in es)


def _nsa_sample_kernel(npages, ns, pt_ref, *refs):
    f32, bf16 = jnp.float32, jnp.bfloat16
    pages = refs[:npages]
    (e_ref, qa_ref, ga_ref, rnew_ref, wnew_ref, state_ref, pe_ref, w1_ref, w2_ref, gk_ref, bd_ref,
     o_ref, bufk_ref, bufv_ref) = refs[npages:]
    page = pages[0].shape[1]
    past = npages * page
    nb = past // CMP_BLOCK
    nrow = NSA_HEADS * ns

    for p in range(npages):
        bufk_ref[p * page:(p + 1) * page, :] = pages[p][0, :, 0:LANE]
        bufv_ref[p * page:(p + 1) * page, :] = pages[p][0, :, LANE:2 * LANE]
    kc = _summarize(bufk_ref, 0, pe_ref, w1_ref, w2_ref, nb)
    kc = (kc * lax.rsqrt(_group_mean_sq(kc, bd_ref[...]) + EPS) * gk_ref[...]).astype(bf16)
    vc = _summarize(bufv_ref, 1, pe_ref, w1_ref, w2_ref, nb).astype(bf16)

    lane8 = _lane(ns)
    ql = qa_ref[...]
    qrows = []
    for hd in range(NSA_HEADS):
        g = hd // NSA_GROUP
        t = ql[:, (hd // 2) * LANE:(hd // 2 + 1) * LANE]
        if hd % 2 != g:
            t = pltpu.roll(t, HEAD_DIM, axis=1)
        qrows.append(jnp.where(lane8 // HEAD_DIM == g, t, 0.0))
    qb = (jnp.concatenate(qrows, axis=0) * SCALE).astype(bf16)

    def geom(width):
        tok = lax.broadcasted_iota(jnp.int32, (nrow, width), 0) % ns
        col = lax.broadcasted_iota(jnp.int32, (nrow, width), 1)
        return tok, col, _slope_rows((nrow, width), ns)

    tok, col, slope = geom(nb)
    lc = _nt_dot(qb, kc) - slope * (past + tok - (col * CMP_BLOCK + CMP_BLOCK - 1)).astype(f32)
    (ec,), lsum = _joint_softmax([lc])
    pc = ec / lsum
    o_c = jnp.dot(pc.astype(bf16), vc, preferred_element_type=f32)

    imp = jnp.concatenate([sum(pc[(g * NSA_GROUP + r) * ns:(g * NSA_GROUP + r + 1) * ns] for r in range(NSA_GROUP))
                           for g in range(NSA_KV_HEADS)], axis=0)
    bcol = lax.broadcasted_iota(jnp.int32, imp.shape, 1)
    imp = jnp.where(bcol == 0, FORCE_SCORE, imp)
    rank = jnp.zeros(imp.shape, jnp.int32)
    for c in range(nb):
        other = jnp.broadcast_to(imp[:, c:c + 1], imp.shape)
        rank = rank + jnp.where((other > imp) | ((other == imp) & (bcol > c)), 1, 0)
    sel = jnp.where(rank < SEL_TOPK - 1, 1.0, 0.0)
    sel = jnp.concatenate([sel[g * ns:(g + 1) * ns] for g in range(NSA_KV_HEADS) for _ in range(NSA_GROUP)], axis=0)
    selexp = jnp.dot(sel.astype(bf16), e_ref[...], preferred_element_type=f32)

    def new_tile(k_new):
        tok, col, slope = geom(LANE)
        s = _nt_dot(qb, _pad_rows(k_new, LANE).astype(bf16))
        return jnp.where(col <= tok, s - slope * (tok - col).astype(f32), NEG_INF)

    def weighted(es, vs, lsum):
        acc = sum(jnp.dot(e.astype(bf16), v, preferred_element_type=f32) for e, v in zip(es, vs))
        return acc / lsum

    tok, col, slope = geom(past)
    ls = jnp.concatenate([_nt_dot(qb, pages[p][0, :, 2 * LANE:3 * LANE].astype(bf16)) for p in range(npages)],
                         axis=1)
    ls = jnp.where(selexp > 0.5, ls - slope * (past + tok - col).astype(f32), NEG_INF)
    (es, en), lsum = _joint_softmax([ls, new_tile(rnew_ref[:, 2 * LANE:3 * LANE])])
    o_s = weighted([es[:, p * page:(p + 1) * page] for p in range(npages)] + [en],
                   [pages[p][0, :, 3 * LANE:4 * LANE].astype(bf16) for p in range(npages)]
                   + [_pad_rows(rnew_ref[:, 3 * LANE:4 * LANE], LANE).astype(bf16)], lsum)

    wbuf = state_ref.shape[1]
    tok, col, slope = geom(wbuf)
    lw = _nt_dot(qb, state_ref[0, :, 0:LANE].astype(bf16))
    lw = jnp.where(col > tok + (wbuf - WINDOW), lw - slope * (wbuf + tok - col).astype(f32), NEG_INF)
    (ew, en), lsum = _joint_softmax([lw, new_tile(wnew_ref[:, 0:LANE])])
    o_w = weighted([ew, en], [state_ref[0, :, LANE:2 * LANE].astype(bf16),
                              _pad_rows(wnew_ref[:, LANE:2 * LANE], LANE).astype(bf16)], lsum)

    ga = ga_ref[...]
    gate = lambda br: jnp.concatenate(
        [jnp.broadcast_to(ga[:, hd * 3 + br:hd * 3 + br + 1], (ns, LANE)) for hd in range(NSA_HEADS)], axis=0)
    o = gate(0) * o_c + gate(1) * o_s + gate(2) * o_w
    for j in range(NSA_HEADS // 2):
        g = (2 * j) // NSA_GROUP
        a = o[2 * j * ns:(2 * j + 1) * ns]
        b = o[(2 * j + 1) * ns:(2 * j + 2) * ns]
        if g == 0:
            b = pltpu.roll(b, HEAD_DIM, axis=1)
        else:
            a = pltpu.roll(a, HEAD_DIM, axis=1)
        o_ref[:, j * LANE:(j + 1) * LANE] = jnp.where(lane8 < HEAD_DIM, a, b)


def _nsa_sample(page_table, cache2d, qa, ga, rows_new, win_new, state2d, pe, w1, w2, gk, bd):
    db, npages = page_table.shape
    page = cache2d.shape[1]
    ns = qa.shape[0] // db
    past = npages * page
    nb = past // CMP_BLOCK
    r = lax.broadcasted_iota(jnp.int32, (nb, past), 0)
    c = lax.broadcasted_iota(jnp.int32, (nb, past), 1)
    expand = jnp.where(c // SEL_BLOCK == r, 1.0, 0.0).astype(jnp.bfloat16)
    gk2 = jnp.concatenate([gk, gk]).reshape(1, LANE)
    full = lambda a: pl.BlockSpec(a.shape, lambda b, pt: (0,) * a.ndim)
    tok = lambda w: pl.BlockSpec((ns, w), lambda b, pt: (b, 0))
    page_specs = [pl.BlockSpec((1, page, cache2d.shape[2]), functools.partial(lambda p, b, pt: (pt[b, p], 0, 0), p))
                  for p in range(npages)]
    return pl.pallas_call(
        functools.partial(_nsa_sample_kernel, npages, ns),
        grid_spec=pltpu.PrefetchScalarGridSpec(
            num_scalar_prefetch=1, grid=(db,),
            in_specs=page_specs + [full(expand), tok(NSA_Q_W), tok(LANE), tok(512), tok(256),
                                   pl.BlockSpec((1,) + state2d.shape[1:], lambda b, pt: (b, 0, 0)),
                                   full(pe), full(w1), full(w2), full(gk2), full(bd)],
            out_specs=tok(NSA_Q_W),
            scratch_shapes=[pltpu.VMEM((past, LANE), jnp.float32), pltpu.VMEM((past, LANE), jnp.float32)]),
        out_shape=jax.ShapeDtypeStruct((db * ns, NSA_Q_W), jnp.float32),
        compiler_params=pltpu.CompilerParams(dimension_semantics=("arbitrary",),
                                             vmem_limit_bytes=VMEM_LIMIT),
        name="nsa_sample",
    )(page_table, *([cache2d] * npages), expand, qa, ga, rows_new, win_new, state2d, pe, w1, w2, gk2, bd)


def _fox_sample_kernel(npages, ns, pt_ref, *refs):
    f32, bf16 = jnp.float32, jnp.bfloat16
    kv = refs[:npages]
    lft = refs[npages:2 * npages]
    qb_ref, knew_ref, lfnew_ref, o_ref = refs[2 * npages:]
    page = kv[0].shape[1]
    nrow = FOX_HEADS * ns

    r_io = lax.broadcasted_iota(jnp.int32, (page, page), 0)
    c_io = lax.broadcasted_iota(jnp.int32, (page, page), 1)
    triu = jnp.where(r_io <= c_io, 1.0, 0.0).astype(bf16)
    carry = jnp.zeros((FOX_HEADS, 1), f32)
    negc = []
    for t in range(npages + 1):
        lf = lft[t][0] if t < npages else lfnew_ref[0]
        ct = carry + sum(jnp.dot(pc_, triu, preferred_element_type=f32) for pc_ in _split3(lf))
        carry = ct[:, page - 1:page]
        negc.append(jnp.concatenate([jnp.broadcast_to(-ct[hd:hd + 1], (ns, page)) for hd in range(FOX_HEADS)],
                                    axis=0))

    lane8 = _lane(ns)
    ql = qb_ref[...] * SCALE
    tok = lax.broadcasted_iota(jnp.int32, (2 * ns, page), 0) % ns
    col = lax.broadcasted_iota(jnp.int32, (2 * ns, page), 1)
    for j in range(FOX_HEADS // 2):
        t = ql[:, j * LANE:(j + 1) * LANE]
        qj = jnp.concatenate([jnp.where(lane8 < HEAD_DIM, t, 0.0), jnp.where(lane8 >= HEAD_DIM, t, 0.0)],
                             axis=0).astype(bf16)
        ksl = slice(j * LANE, (j + 1) * LANE)
        vsl = slice(FOX_W + j * LANE, FOX_W + (j + 1) * LANE)
        rows = slice(2 * j * ns, (2 * j + 2) * ns)
        parts = [_nt_dot(qj, kv[p][0, :, ksl].astype(bf16)) + negc[p][rows] for p in range(npages)]
        s_new = _nt_dot(qj, _pad_rows(knew_ref[:, ksl], page).astype(bf16)) + negc[npages][rows]
        parts.append(jnp.where(col <= tok, s_new, NEG_INF))
        es, lsum = _joint_softmax(parts)
        vs = [kv[p][0, :, vsl].astype(bf16) for p in range(npages)] + [_pad_rows(knew_ref[:, vsl], page).astype(bf16)]
        acc = sum(jnp.dot(e.astype(bf16), v, preferred_element_type=f32) for e, v in zip(es, vs)) / lsum
        o_ref[:, j * LANE:(j + 1) * LANE] = jnp.where(lane8 < HEAD_DIM, acc[0:ns], acc[ns:2 * ns])


def _fox_sample(page_table, kv2d, lft, qb, fox_new, lft_new):
    db, npages = page_table.shape
    page = kv2d.shape[1]
    ns = qb.shape[0] // db
    tok = lambda w: pl.BlockSpec((ns, w), lambda b, pt: (b, 0))
    pg = lambda a: [pl.BlockSpec((1,) + a.shape[1:], functools.partial(lambda p, b, pt: (pt[b, p], 0, 0), p))
                    for p in range(npages)]
    return pl.pallas_call(
        functools.partial(_fox_sample_kernel, npages, ns),
        grid_spec=pltpu.PrefetchScalarGridSpec(
            num_scalar_prefetch=1, grid=(db,),
            in_specs=pg(kv2d) + pg(lft) + [tok(FOX_W), tok(2 * FOX_W),
                                            pl.BlockSpec((1, FOX_HEADS, page), lambda b, pt: (b, 0, 0))],
            out_specs=tok(FOX_W)),
        out_shape=jax.ShapeDtypeStruct((db * ns, FOX_W), jnp.float32),
        compiler_params=pltpu.CompilerParams(dimension_semantics=("arbitrary",),
                                             vmem_limit_bytes=VMEM_LIMIT),
        name="fox_sample",
    )(page_table, *([kv2d] * npages), *([lft] * npages), qb, fox_new, lft_new)


BIG_ID = 1 << 20


def _topk_rows(s, k, ids):
    ids = ids.astype(jnp.float32)
    vals, idxs = [], []
    for _ in range(k):
        m = jnp.max(s, axis=0, keepdims=True)
        idx = jnp.min(jnp.where(s == m, ids, float(BIG_ID)), axis=0, keepdims=True)
        vals.append(m)
        idxs.append(idx)
        s = jnp.where(ids == idx, -jnp.inf, s)
    return jnp.concatenate(vals, axis=0), jnp.concatenate(idxs, axis=0).astype(jnp.int32)


def _pick_rows(sel, table):
    out = jnp.zeros(sel.shape, table.dtype)
    for r in range(table.shape[0]):
        out = jnp.where(sel == r, table[r:r + 1, :], out)
    return out


def _merge_route_kernel(x_ref, on_ref, of_ref, mg_ref, wun_ref, wuf_ref, wo_ref, nf_ref, wqt_ref, sk_ref,
                        x1_ref, h2_ref, i1_ref, i2_ref, g_ref):
    tm = x_ref.shape[0]
    f32, bf16 = jnp.float32, jnp.bfloat16
    tdot = lambda ot, wgt: lax.dot_general(ot.astype(bf16), wgt, (((0,), (0,)), ((), ())),
                                           preferred_element_type=f32)
    a = tdot(on_ref[...], wun_ref[...])
    b = tdot(of_ref[...], wuf_ref[...])
    mixed = mg_ref[:, 0:D_MODEL] * a + mg_ref[:, D_MODEL:2 * D_MODEL] * b
    x1 = x_ref[...] + jnp.dot(mixed.astype(bf16), wo_ref[...], preferred_element_type=f32)
    x1_ref[...] = x1
    h2 = x1 * lax.rsqrt(jnp.mean(x1 * x1, axis=-1, keepdims=True) + EPS) * nf_ref[...]
    h2b = h2.astype(bf16)
    h2_ref[...] = h2b

    nk = PEER_N_KEYS
    key_ids = lax.broadcasted_iota(jnp.int32, (nk, tm), 0)
    io16 = lax.broadcasted_iota(jnp.int32, (PEER_TOPK, tm), 0)
    io8 = lax.broadcasted_iota(jnp.int32, (8, tm), 0)
    cand_ids = jnp.concatenate([io16] + [a_ * PEER_TOPK + io8 for a_ in range(1, 8)]
                               + [(io8 + 8) * PEER_TOPK], axis=0)
    for h in range(PEER_HEADS):
        sv, si = [], []
        for p in range(2):
            hp = 2 * h + p
            qt = lax.dot_general(wqt_ref[hp * PEER_DK_HALF:(hp + 1) * PEER_DK_HALF, :], h2b,
                                 (((1,), (1,)), ((), ())), preferred_element_type=f32)
            st = jnp.dot(sk_ref[hp], qt.astype(bf16), preferred_element_type=f32)
            v, i = _topk_rows(st, PEER_TOPK, key_ids)
            sv.append(v)
            si.append(i)
        s1, s2 = sv
        cand = jnp.concatenate([s1[0:1] + s2] + [s1[a_:a_ + 1] + s2[0:8] for a_ in range(1, 8)]
                               + [s1[8:16] + s2[0:1]], axis=0)
        top, fid = _topk_rows(cand, PEER_TOPK, cand_ids)
        e = jnp.exp(top - jnp.max(top, axis=0, keepdims=True))
        g = e / jnp.sum(e, axis=0, keepdims=True)
        sl = slice(h * PEER_TOPK, (h + 1) * PEER_TOPK)
        i1_ref[:, sl] = _pick_rows(fid >> 4, si[0]).T
        i2_ref[:, sl] = _pick_rows(fid & (PEER_TOPK - 1), si[1]).T
        g_ref[:, sl] = g.T


def _merge_route(x2d, o_nsa, o_fox, mg, w_up_nsa, w_up_fox, w_out, norm_ffn, wq_t, sub_keys, tm=256):
    n = x2d.shape[0]
    row = lambda w: pl.BlockSpec((tm, w), lambda i: (i, 0))
    col = lambda h: pl.BlockSpec((h, tm), lambda i: (0, i))
    full = lambda a: pl.BlockSpec(a.shape, lambda i: (0,) * a.ndim)
    args = (x2d, o_nsa, o_fox, mg, w_up_nsa, w_up_fox, w_out, norm_ffn.reshape(1, D_MODEL), wq_t, sub_keys)
    hk = PEER_HEADS * PEER_TOPK
    return pl.pallas_call(
        _merge_route_kernel,
        grid=(n // tm,),
        in_specs=[row(D_MODEL), col(NSA_Q_W), col(FOX_W), row(MERGE_W)] + [full(a) for a in args[4:]],
        out_specs=[row(D_MODEL), row(D_MODEL), row(hk), row(hk), row(hk)],
        out_shape=[jax.ShapeDtypeStruct((n, D_MODEL), jnp.float32),
                   jax.ShapeDtypeStruct((n, D_MODEL), jnp.bfloat16),
                   jax.ShapeDtypeStruct((n, hk), jnp.int32),
                   jax.ShapeDtypeStruct((n, hk), jnp.int32),
                   jax.ShapeDtypeStruct((n, hk), jnp.float32)],
        compiler_params=pltpu.CompilerParams(dimension_semantics=("arbitrary",),
                                             vmem_limit_bytes=VMEM_LIMIT),
        name="merge_route",
    )(*args)


def _peer_act_kernel(h2_ref, u_ref, i1_ref, i2_ref, act_ref):
    c = pl.program_id(1)
    ec = u_ref.shape[0]

    @pl.when(c == 0)
    def _():
        act_ref[...] = jnp.zeros_like(act_ref)

    a = lax.dot_general(h2_ref[...], u_ref[...], (((1,), (1,)), ((), ())),
                        preferred_element_type=jnp.float32)
    i1 = i1_ref[...]
    i2 = i2_ref[...]
    act = act_ref[...]
    for ii in range(ec // PEER_N_KEYS):
        got = jnp.take_along_axis(a[:, ii * PEER_N_KEYS:(ii + 1) * PEER_N_KEYS], i2, axis=1)
        act = jnp.where(i1 == c * (ec // PEER_N_KEYS) + ii, got, act)
    act_ref[...] = act


def _peer_act(h2b, u_b, i1, i2, tm=512, ec=2048):
    n = h2b.shape[0]
    hk = i1.shape[1]
    return pl.pallas_call(
        _peer_act_kernel,
        grid=(n // tm, u_b.shape[0] // ec),
        in_specs=[pl.BlockSpec((tm, D_MODEL), lambda t, c: (t, 0)),
                  pl.BlockSpec((ec, D_MODEL), lambda t, c: (c, 0)),
                  pl.BlockSpec((tm, hk), lambda t, c: (t, 0)),
                  pl.BlockSpec((tm, hk), lambda t, c: (t, 0))],
        out_specs=pl.BlockSpec((tm, hk), lambda t, c: (t, 0)),
        out_shape=jax.ShapeDtypeStruct((n, hk), jnp.float32),
        compiler_params=pltpu.CompilerParams(dimension_semantics=("arbitrary", "arbitrary"),
                                             vmem_limit_bytes=VMEM_LIMIT),
        name="peer_act",
    )(h2b, u_b, i1, i2)


def _peer_coef_kernel(act_ref, g_ref, i1_ref, i2_ref, c_ref, coef_ref):
    tm = act_ref.shape[0]
    nk = PEER_N_KEYS
    coef_ref[...] = g_ref[...] * jax.nn.gelu(act_ref[...])
    sub = lax.broadcasted_iota(jnp.int32, (nk, i1_ref.shape[1]), 0)

    def token(t):
        r1 = i1_ref[pl.ds(t, 1), :]
        r2 = i2_ref[pl.ds(t, 1), :]
        cf = coef_ref[pl.ds(t, 1), :]
        m1 = jnp.where(r1 == sub, cf, 0.0).astype(jnp.bfloat16)
        m2t = jnp.where(r2 == sub, 1.0, 0.0).astype(jnp.bfloat16)
        return lax.dot_general(m1, m2t, (((1,), (1,)), ((), ())), preferred_element_type=jnp.float32)

    def body(tg, carry):
        t0 = pl.multiple_of(tg * COEF_GROUP, COEF_GROUP)
        ct = jnp.stack([token(t0 + u) for u in range(COEF_GROUP)], axis=0)
        c_ref[:, pl.ds(t0, COEF_GROUP), :] = pltpu.einshape("tij->itj", ct).astype(c_ref.dtype)
        return carry

    lax.fori_loop(0, tm // COEF_GROUP, body, 0)


COEF_GROUP = 16


def _peer_coef(act, g, i1, i2, tm=128):
    n, hk = act.shape
    nk = PEER_N_KEYS
    row = pl.BlockSpec((tm, hk), lambda t: (t, 0))
    return pl.pallas_call(
        _peer_coef_kernel,
        grid=(n // tm,),
        in_specs=[row, row, row, row],
        out_specs=pl.BlockSpec((nk, tm, nk), lambda t: (0, t, 0)),
        out_shape=jax.ShapeDtypeStruct((nk, n, nk), jnp.bfloat16),
        scratch_shapes=[pltpu.VMEM((tm, hk), jnp.float32)],
        compiler_params=pltpu.CompilerParams(dimension_semantics=("arbitrary",),
                                             vmem_limit_bytes=VMEM_LIMIT),
        name="peer_coef",
    )(act, g, i1, i2)


def _peer_out_kernel(c_ref, v_ref, x1_ref, y_ref, acc_ref):
    k = pl.program_id(1)

    @pl.when(k == 0)
    def _():
        acc_ref[...] = x1_ref[...]

    acc = acc_ref[...]
    nk = PEER_N_KEYS
    for p in range(c_ref.shape[0] // 2):
        lhs = jnp.concatenate([c_ref[2 * p], c_ref[2 * p + 1]], axis=1)
        acc = acc + jnp.dot(lhs, v_ref[2 * p * nk:(2 * p + 2) * nk, :], preferred_element_type=jnp.float32)
    acc_ref[...] = acc

    @pl.when(k == pl.num_programs(1) - 1)
    def _():
        y_ref[...] = acc_ref[...]


def _peer_out(c3, v_b, x1, tm=1024, tk=2048):
    nk, n, _ = c3.shape
    ne = nk * nk
    return pl.pallas_call(
        _peer_out_kernel,
        grid=(n // tm, ne // tk),
        in_specs=[pl.BlockSpec((tk // nk, tm, nk), lambda t, k: (k, t, 0)),
                  pl.BlockSpec((tk, D_MODEL), lambda t, k: (k, 0)),
                  pl.BlockSpec((tm, D_MODEL), lambda t, k: (t, 0))],
        out_specs=pl.BlockSpec((tm, D_MODEL), lambda t, k: (t, 0)),
        out_shape=jax.ShapeDtypeStruct((n, D_MODEL), jnp.float32),
        scratch_shapes=[pltpu.VMEM((tm, D_MODEL), jnp.float32)],
        compiler_params=pltpu.CompilerParams(dimension_semantics=("arbitrary", "arbitrary"),
                                             vmem_limit_bytes=VMEM_LIMIT),
        name="peer_out",
    )(c3, v_b, x1)


def _peer_weights(w_up_nsa, w_up_fox, w_out, norm_ffn, peer_w_query, peer_sub_keys, peer_u, peer_v):
    bf16 = jnp.bfloat16
    return dict(w_up_nsa=w_up_nsa.astype(bf16), w_up_fox=w_up_fox.astype(bf16), w_out=w_out.astype(bf16),
                norm_ffn=norm_ffn, wq_t=peer_w_query.T.astype(bf16),
                sub_keys=peer_sub_keys.reshape(2 * PEER_HEADS, PEER_N_KEYS, PEER_DK_HALF).astype(bf16),
                u=peer_u.astype(bf16), v=peer_v.astype(bf16))


def _merge_peer(x2d, o_nsa, o_fox, mg, wts):
    x1, h2b, i1, i2, g = _merge_route(x2d, o_nsa, o_fox, mg, wts['w_up_nsa'], wts['w_up_fox'], wts['w_out'],
                                      wts['norm_ffn'], wts['wq_t'], wts['sub_keys'])
    act = _peer_act(h2b, wts['u'], i1, i2)
    c3 = _peer_coef(act, g, i1, i2)
    return _peer_out(c3, wts['v'], x1)


def kernel(x_prompt, x_sample, cache_nsa, cache_fox_kv, cache_fox_logf, state_nsa_win, page_table,
           norm_attn, w_in, fox_f_bias, nsa_q_norm, nsa_k_norm, fox_q_norm, fox_k_norm,
           cmp_pe, cmp_w1, cmp_w2, w_up_nsa, w_up_fox, w_out, norm_ffn,
           peer_w_query, peer_sub_keys, peer_u, peer_v):
    w_front = _front_weights(w_in)
    bd = _block_diag_mean()
    wts = _peer_weights(w_up_nsa, w_up_fox, w_out, norm_ffn, peer_w_query, peer_sub_keys, peer_u, peer_v)
    cmp_wts = _compress_weights(cmp_pe, cmp_w1, cmp_w2)

    bp, seq, _ = x_prompt.shape
    n_p = bp * seq
    (rows_p, win_p2d, fox_p2d, logf_p2d, mg_p, qat, gat, ksel, kwin, vselt, vwint, qbt, kb, vbt) = _front_attn(
        x_prompt.reshape(n_p, D_MODEL), seq, norm_attn, w_front, bd, fox_f_bias,
        nsa_q_norm, nsa_k_norm, fox_q_norm, fox_k_norm)
    kc, vct = _compress(rows_p, seq // CMP_BLOCK, *cmp_wts, nsa_k_norm[0], bd)
    o_nsa_t = _nsa_prompt(qat, gat, ksel, vselt, kwin, vwint, kc, vct, bp, seq)
    o_fox_t = _fox_prompt(qbt, kb, vbt, bp, seq)
    y_p = _merge_peer(x_prompt.reshape(n_p, D_MODEL), o_nsa_t, o_fox_t, mg_p, wts).reshape(x_prompt.shape)
    nsa_p = rows_p.reshape(bp, seq, 4, NSA_KV_HEADS, HEAD_DIM)
    fox_p = fox_p2d.reshape(bp, seq, 2, FOX_HEADS, HEAD_DIM)
    logf_p = logf_p2d.reshape(bp, seq, FOX_HEADS)
    win_p = win_p2d.reshape(bp, seq, 2, NSA_KV_HEADS, HEAD_DIM)[:, seq - min(WINDOW, seq):]

    db, ns, _ = x_sample.shape
    n_s = db * ns
    n_pool, page = cache_nsa.shape[:2]
    wbuf = state_nsa_win.shape[1]
    qa_s, rows_s, win_s2d, ga_s, qb_s, fox_s2d, logf_s2d, mg_s = _front(
        x_sample.reshape(n_s, D_MODEL), norm_attn, w_front, bd, fox_f_bias,
        nsa_q_norm, nsa_k_norm, fox_q_norm, fox_k_norm)
    o_nsa_s = _nsa_sample(page_table, cache_nsa.reshape(n_pool, page, 4 * LANE), qa_s, ga_s, rows_s, win_s2d,
                          state_nsa_win.reshape(db, wbuf, 2 * LANE), *cmp_wts, nsa_k_norm[0], bd)
    lft_new = jnp.pad(logf_s2d.reshape(db, ns, FOX_HEADS).transpose(0, 2, 1), ((0, 0), (0, 0), (0, page - ns)))
    o_fox_s = _fox_sample(page_table, cache_fox_kv.reshape(n_pool, page, 2 * FOX_W),
                          cache_fox_logf.transpose(0, 2, 1), qb_s, fox_s2d, lft_new)
    y_s = _merge_peer(x_sample.reshape(n_s, D_MODEL), o_nsa_s.T, o_fox_s.T, mg_s, wts).reshape(x_sample.shape)
    nsa_s = rows_s.reshape(db, ns, 4, NSA_KV_HEADS, HEAD_DIM)
    fox_s = fox_s2d.reshape(db, ns, 2, FOX_HEADS, HEAD_DIM)
    logf_s = logf_s2d.reshape(db, ns, FOX_HEADS)
    win_s = jnp.concatenate([state_nsa_win[:, ns:], win_s2d.reshape(db, ns, 2, NSA_KV_HEADS, HEAD_DIM)], axis=1)
    return (y_p, y_s, nsa_p, fox_p, logf_p, win_p, nsa_s, fox_s, logf_s, win_s)
```

```python
import functools

import jax
import jax.numpy as jnp
from jax import lax
from jax.experimental import pallas as pl
from jax.experimental.pallas import tpu as pltpu

D_MODEL = 1024
HEAD_DIM = 64
NSA_HEADS = 8
NSA_KV_HEADS = 2
NSA_GROUP = NSA_HEADS // NSA_KV_HEADS
CMP_BLOCK = 64
SEL_BLOCK = CMP_BLOCK
SEL_TOPK = 16
WINDOW = 512
FOX_HEADS = 8
Q_BLOCK = 128
PEER_HEADS = 8
PEER_N_KEYS = 128
PEER_DK = 256
PEER_DK_HALF = PEER_DK // 2
PEER_TOPK = 16
PEER_CHUNK = 256

NSA_Q_W = NSA_HEADS * HEAD_DIM
NSA_KV_W = 6 * NSA_KV_HEADS * HEAD_DIM
NSA_GATE_W = 3 * NSA_HEADS
FOX_W = FOX_HEADS * HEAD_DIM
FOX_QKV_W = 3 * FOX_W
FOX_F_W = FOX_HEADS
MERGE_W = 2 * D_MODEL
SPLIT_Q_A = NSA_Q_W
SPLIT_KV_A = SPLIT_Q_A + NSA_KV_W
SPLIT_G_A = SPLIT_KV_A + NSA_GATE_W
SPLIT_QKV_B = SPLIT_G_A + FOX_QKV_W
SPLIT_F_B = SPLIT_QKV_B + FOX_F_W
IN_WIDTH = SPLIT_F_B + MERGE_W

SCALE = HEAD_DIM ** -0.5
FORCE_SCORE = float(NSA_GROUP + 1)
NEG_INF = -1e30
EPS = 1e-6

LANE = 128
VMEM_LIMIT = 48 * 1024 * 1024


def _group_mean_sq(x, bd):
    sq = x * x
    hi = sq.astype(jnp.bfloat16)
    lo = (sq - hi.astype(jnp.float32)).astype(jnp.bfloat16)
    return (jnp.dot(hi, bd, preferred_element_type=jnp.float32)
            + jnp.dot(lo, bd, preferred_element_type=jnp.float32))


def _head_rms(x, g, bd):
    outs = []
    for c in range(x.shape[1] // LANE):
        xc = x[:, c * LANE:(c + 1) * LANE]
        outs.append(xc * lax.rsqrt(_group_mean_sq(xc, bd) + EPS) * g)
    return outs[0] if len(outs) == 1 else jnp.concatenate(outs, axis=1)


def _front_kernel(x_ref, na_ref, w_ref, bd_ref, fb_ref, gq_a_ref, gk_sel_ref, gk_win_ref,
                  gq_b_ref, gk_b_ref,
                  qa_ref, rows_ref, win_ref, ga_ref, qb_ref, fox_ref, logf_ref, mg_ref):
    x = x_ref[...]
    h = x * lax.rsqrt(jnp.mean(x * x, axis=-1, keepdims=True) + EPS) * na_ref[...]
    hb = h.astype(jnp.bfloat16)
    bd = bd_ref[...]

    def proj(c0, width):
        return jnp.dot(hb, w_ref[:, c0:c0 + width], preferred_element_type=jnp.float32)

    c = 0
    qa_ref[...] = _head_rms(proj(c, NSA_Q_W), gq_a_ref[...], bd)
    c += NSA_Q_W
    rows_ref[:, 0:256] = proj(c, 256)
    rows_ref[:, 256:384] = _head_rms(proj(c + 256, 128), gk_sel_ref[...], bd)
    rows_ref[:, 384:512] = proj(c + 384, 128)
    win_ref[:, 0:128] = _head_rms(proj(c + 512, 128), gk_win_ref[...], bd)
    win_ref[:, 128:256] = proj(c + 640, 128)
    c += NSA_KV_W
    qb_ref[...] = _head_rms(proj(c, FOX_W), gq_b_ref[...], bd)
    fox_ref[:, 0:FOX_W] = _head_rms(proj(c + FOX_W, FOX_W), gk_b_ref[...], bd)
    fox_ref[:, FOX_W:2 * FOX_W] = proj(c + 2 * FOX_W, FOX_W)
    c += FOX_QKV_W
    for j in range(MERGE_W // 512):
        mg_ref[:, j * 512:(j + 1) * 512] = jax.nn.sigmoid(proj(c + j * 512, 512))
    c += MERGE_W
    ga_ref[...] = jax.nn.sigmoid(proj(c, LANE))
    f = proj(c + LANE, LANE)[:, 0:FOX_F_W] + fb_ref[...]
    logf_ref[...] = jnp.minimum(f, 0.0) - jnp.log1p(jnp.exp(-jnp.abs(f)))


def _front(x2d, norm_attn, w_front, bd, fox_f_bias, nsa_q_norm, nsa_k_norm, fox_q_norm, fox_k_norm,
           tm=256):
    n = x2d.shape[0]
    wf = w_front.shape[1]
    two = lambda g: jnp.concatenate([g, g]).reshape(1, LANE)
    row = lambda w: pl.BlockSpec((tm, w), lambda i: (i, 0))
    full = lambda a: pl.BlockSpec(a.shape, lambda i: (0,) * a.ndim)
    args = (x2d, norm_attn.reshape(1, D_MODEL), w_front, bd, fox_f_bias.reshape(1, FOX_F_W),
            two(nsa_q_norm), two(nsa_k_norm[1]), two(nsa_k_norm[2]), two(fox_q_norm), two(fox_k_norm))
    widths = (NSA_Q_W, 512, 256, LANE, FOX_W, 2 * FOX_W, FOX_F_W, MERGE_W)
    return pl.pallas_call(
        _front_kernel,
        grid=(n // tm,),
        in_specs=[row(D_MODEL)] + [full(a) for a in args[1:]],
        out_specs=[row(w) for w in widths],
        out_shape=[jax.ShapeDtypeStruct((n, w), jnp.float32) for w in widths],
        compiler_params=pltpu.CompilerParams(dimension_semantics=("arbitrary",),
                                             vmem_limit_bytes=VMEM_LIMIT),
        name="front",
    )(*args)


FEAT = HEAD_DIM


def _lane(tm):
    return lax.broadcasted_iota(jnp.int32, (tm, LANE), 1)


def _expand_halves(x):
    lo = _lane(x.shape[0]) < HEAD_DIM
    return jnp.where(lo, x, 0.0), jnp.where(lo, pltpu.roll(x, HEAD_DIM, axis=1), 0.0)


def _split3(x):
    hi = x.astype(jnp.bfloat16)
    r = x - hi.astype(jnp.float32)
    mid = r.astype(jnp.bfloat16)
    lo = (r - mid.astype(jnp.float32)).astype(jnp.bfloat16)
    return hi, mid, lo


def _front_attn_kernel(seq_len, x_ref, na_ref, w_ref, bd_ref, fb_ref, gq_a_ref, gk_sel_ref, gk_win_ref,
                       gq_b_ref, gk_b_ref, place_ref,
                       rows_ref, win_ref, fox_ref, logf_ref, mg_ref,
                       qat_ref, gat_ref, ksel_ref, kwin_ref, vselt_ref, vwint_ref, qbt_ref, kb_ref, vbt_ref,
                       carry_ref):
    f32, bf16 = jnp.float32, jnp.bfloat16
    tm = x_ref.shape[0]
    i = pl.program_id(0)
    x = x_ref[...]
    h = x * lax.rsqrt(jnp.mean(x * x, axis=-1, keepdims=True) + EPS) * na_ref[...]
    hb = h.astype(bf16)
    bd = bd_ref[...]
    lane = _lane(tm)
    pos = (i * tm + lax.broadcasted_iota(jnp.int32, (tm, LANE), 0)) % seq_len
    kfeat = jnp.where(lane == FEAT, (pos // SEL_BLOCK).astype(f32),
                      jnp.where(lane == FEAT + 1, (pos % SEL_BLOCK).astype(f32), 0.0))

    def proj(c0, width):
        return jnp.dot(hb, w_ref[:, c0:c0 + width], preferred_element_type=f32)

    c = 0
    qa = _head_rms(proj(c, NSA_Q_W), gq_a_ref[...], bd)
    for j in range(NSA_HEADS // 2):
        for s, half in enumerate(_expand_halves(qa[:, j * LANE:(j + 1) * LANE])):
            hd = 2 * j + s
            slope = 2.0 ** -(hd + 1)
            qfeat = jnp.where(lane == FEAT, slope * SEL_BLOCK, jnp.where(lane == FEAT + 1, slope, 0.0))
            qat_ref[hd * LANE:(hd + 1) * LANE, :] = (half * SCALE + qfeat).T.astype(bf16)
    c += NSA_Q_W
    rows_ref[:, 0:256] = proj(c, 256)
    ksel = _head_rms(proj(c + 256, 128), gk_sel_ref[...], bd)
    rows_ref[:, 256:384] = ksel
    vsel = proj(c + 384, 128)
    rows_ref[:, 384:512] = vsel
    kwin = _head_rms(proj(c + 512, 128), gk_win_ref[...], bd)
    win_ref[:, 0:128] = kwin
    vwin = proj(c + 640, 128)
    win_ref[:, 128:256] = vwin
    for g, (ks_g, kw_g) in enumerate(zip(_expand_halves(ksel), _expand_halves(kwin))):
        ksel_ref[:, g * LANE:(g + 1) * LANE] = (ks_g + kfeat).astype(bf16)
        kwin_ref[:, g * LANE:(g + 1) * LANE] = (kw_g + kfeat).astype(bf16)
    vselt_ref[...] = vsel.T.astype(bf16)
    vwint_ref[...] = vwin.T.astype(bf16)
    c += NSA_KV_W

    qb = _head_rms(proj(c, FOX_W), gq_b_ref[...], bd)
    ones3 = jnp.where((lane >= FEAT) & (lane < FEAT + 3), 1.0, 0.0)
    for j in range(FOX_HEADS // 2):
        for s, half in enumerate(_expand_halves(qb[:, j * LANE:(j + 1) * LANE])):
            hd = 2 * j + s
            qbt_ref[hd * LANE:(hd + 1) * LANE, :] = (half * SCALE + ones3).T.astype(bf16)
    kbn = _head_rms(proj(c + FOX_W, FOX_W), gk_b_ref[...], bd)
    fox_ref[:, 0:FOX_W] = kbn
    for j in range(FOX_HEADS // 2):
        vb = proj(c + 2 * FOX_W + j * LANE, LANE)
        fox_ref[:, FOX_W + j * LANE:FOX_W + (j + 1) * LANE] = vb
        vbt_ref[j * LANE:(j + 1) * LANE, :] = vb.T.astype(bf16)
    c += FOX_QKV_W
    for j in range(MERGE_W // 512):
        mg_ref[:, j * 512:(j + 1) * 512] = jax.nn.sigmoid(proj(c + j * 512, 512))
    c += MERGE_W
    gat_ref[...] = jax.nn.sigmoid(proj(c, LANE)).T
    f = proj(c + LANE, LANE) + fb_ref[...]
    lf = jnp.minimum(f, 0.0) - jnp.log1p(jnp.exp(-jnp.abs(f)))
    logf_ref[...] = lf[:, 0:FOX_F_W]

    @pl.when((i * tm) % seq_len == 0)
    def _():
        carry_ref[...] = jnp.zeros_like(carry_ref)

    r_io = lax.broadcasted_iota(jnp.int32, (tm, tm), 0)
    c_io = lax.broadcasted_iota(jnp.int32, (tm, tm), 1)
    tri = jnp.where(c_io <= r_io, 1.0, 0.0).astype(bf16)
    csum = carry_ref[...] + sum(jnp.dot(tri, p, preferred_element_type=f32) for p in _split3(lf))
    carry_ref[...] = csum[tm - 1:tm, :]
    pieces = jnp.concatenate(_split3(-csum), axis=1)
    cfeat = jnp.dot(pieces, place_ref[...], preferred_element_type=f32)
    for j in range(FOX_HEADS // 2):
        for s, half in enumerate(_expand_halves(kbn[:, j * LANE:(j + 1) * LANE])):
            hd = 2 * j + s
            kb_ref[:, hd * LANE:(hd + 1) * LANE] = (half + cfeat[:, hd * LANE:(hd + 1) * LANE]).astype(bf16)


def _fox_feature_placement():
    r = lax.broadcasted_iota(jnp.int32, (3 * LANE, FOX_HEADS * LANE), 0)
    c = lax.broadcasted_iota(jnp.int32, (3 * LANE, FOX_HEADS * LANE), 1)
    s, hd = r // LANE, r % LANE
    return jnp.where((hd < FOX_HEADS) & (c == hd * LANE + FEAT + s), 1.0, 0.0).astype(jnp.bfloat16)


def _front_attn(x2d, seq_len, norm_attn, w_front, bd, fox_f_bias, nsa_q_norm, nsa_k_norm, fox_q_norm,
                fox_k_norm, tm=256):
    n = x2d.shape[0]
    f32, bf16 = jnp.float32, jnp.bfloat16
    two = lambda g: jnp.concatenate([g, g]).reshape(1, LANE)
    row = lambda w: pl.BlockSpec((tm, w), lambda i: (i, 0))
    col = lambda h: pl.BlockSpec((h, tm), lambda i: (0, i))
    full = lambda a: pl.BlockSpec(a.shape, lambda i: (0,) * a.ndim)
    fb = jnp.pad(fox_f_bias, (0, LANE - FOX_F_W)).reshape(1, LANE)
    args = (x2d, norm_attn.reshape(1, D_MODEL), w_front, bd, fb,
            two(nsa_q_norm), two(nsa_k_norm[1]), two(nsa_k_norm[2]), two(fox_q_norm), two(fox_k_norm),
            _fox_feature_placement())
    outs = [(row(512), (n, 512), f32), (row(256), (n, 256), f32), (row(2 * FOX_W), (n, 2 * FOX_W), f32),
            (row(FOX_F_W), (n, FOX_F_W), f32), (row(MERGE_W), (n, MERGE_W), f32),
            (col(NSA_HEADS * LANE), (NSA_HEADS * LANE, n), bf16), (col(LANE), (LANE, n), f32),
            (row(2 * LANE), (n, 2 * LANE), bf16), (row(2 * LANE), (n, 2 * LANE), bf16),
            (col(LANE), (LANE, n), bf16), (col(LANE), (LANE, n), bf16),
            (col(FOX_HEADS * LANE), (FOX_HEADS * LANE, n), bf16),
            (row(FOX_HEADS * LANE), (n, FOX_HEADS * LANE), bf16), (col(FOX_W), (FOX_W, n), bf16)]
    return pl.pallas_call(
        functools.partial(_front_attn_kernel, seq_len),
        grid=(n // tm,),
        in_specs=[row(D_MODEL)] + [full(a) for a in args[1:]],
        out_specs=[o[0] for o in outs],
        out_shape=[jax.ShapeDtypeStruct(o[1], o[2]) for o in outs],
        scratch_shapes=[pltpu.VMEM((1, LANE), f32)],
        compiler_params=pltpu.CompilerParams(dimension_semantics=("arbitrary",),
                                             vmem_limit_bytes=VMEM_LIMIT),
        name="front_attn",
    )(*args)


def _front_weights(w_in):
    pad = lambda w: jnp.pad(w, ((0, 0), (0, LANE - w.shape[1])))
    parts = [w_in[:, :SPLIT_KV_A], w_in[:, SPLIT_G_A:SPLIT_QKV_B], w_in[:, SPLIT_F_B:],
             pad(w_in[:, SPLIT_KV_A:SPLIT_G_A]), pad(w_in[:, SPLIT_QKV_B:SPLIT_F_B])]
    return jnp.concatenate(parts, axis=1).astype(jnp.bfloat16)


def _block_diag_mean():
    r = lax.broadcasted_iota(jnp.int32, (LANE, LANE), 0) // HEAD_DIM
    c = lax.broadcasted_iota(jnp.int32, (LANE, LANE), 1) // HEAD_DIM
    return jnp.where(r == c, 1.0 / HEAD_DIM, 0.0).astype(jnp.bfloat16)


def _softmax_step(tiles, carry):
    m, l, acc = carry
    m_new = functools.reduce(jnp.maximum, [jnp.max(s, axis=0, keepdims=True) for s, _ in tiles], m)
    alpha = jnp.exp(m - m_new)
    ps = [jnp.exp(s - m_new) for s, _ in tiles]
    l = alpha * l + sum(jnp.sum(p, axis=0, keepdims=True) for p in ps)
    acc = alpha * acc + sum(jnp.dot(vt, p.astype(jnp.bfloat16), preferred_element_type=jnp.float32)
                            for p, (_, vt) in zip(ps, tiles))
    return m_new, l, acc


TILE_UNROLL = 8


def _tile_loop(n, tile, carry):
    def body_u(i, c):
        return _softmax_step([tile(i * TILE_UNROLL + u) for u in range(TILE_UNROLL)], c)

    nu = n // TILE_UNROLL
    carry = lax.fori_loop(0, nu, body_u, carry)
    base = nu * TILE_UNROLL
    size = TILE_UNROLL // 2
    while size >= 1:
        has = (n & size) != 0
        carry = lax.cond(has, functools.partial(
            lambda b, sz, c: _softmax_step([tile(b + u) for u in range(sz)], c), base, size),
            lambda c: c, carry)
        base = base + jnp.where(has, size, 0)
        size //= 2
    return carry


def _softmax_init(w):
    return (jnp.full((1, w), NEG_INF, jnp.float32), jnp.zeros((1, w), jnp.float32),
            jnp.zeros((HEAD_DIM, w), jnp.float32))


def _summarize(x_ref, s, pe_ref, w1_ref, w2_ref, nb):
    f32, bf16 = jnp.float32, jnp.bfloat16

    def body(l, acc):
        xl = x_ref[pl.ds(l, nb, stride=CMP_BLOCK), :] + pe_ref[s, pl.ds(l, 1), :]
        return acc + jnp.dot(xl.astype(bf16), w1_ref[s, l], preferred_element_type=f32)

    hid = jax.nn.gelu(lax.fori_loop(0, CMP_BLOCK, body, jnp.zeros((nb, LANE), f32), unroll=8))
    return jnp.dot(hid.astype(bf16), w2_ref[s], preferred_element_type=f32)


def _compress_kernel(xk_ref, xv_ref, pe_ref, w1_ref, w2_ref, gk_ref, bd_ref, kc_ref, vct_ref):
    f32, bf16 = jnp.float32, jnp.bfloat16
    nb = kc_ref.shape[0]
    kc = _summarize(xk_ref, 0, pe_ref, w1_ref, w2_ref, nb)
    kc = kc * lax.rsqrt(_group_mean_sq(kc, bd_ref[...]) + EPS) * gk_ref[...]
    lane = _lane(nb)
    blk = lax.broadcasted_iota(jnp.int32, (nb, LANE), 0).astype(f32)
    feat = jnp.where(lane == FEAT, blk, jnp.where(lane == FEAT + 1, float(CMP_BLOCK - 1), 0.0))
    for g, half in enumerate(_expand_halves(kc)):
        kc_ref[:, g * LANE:(g + 1) * LANE] = (half + feat).astype(bf16)
    vct_ref[...] = _summarize(xv_ref, 1, pe_ref, w1_ref, w2_ref, nb).T.astype(bf16)


def _compress_weights(cmp_pe, cmp_w1, cmp_w2):
    def bdiag(w):
        z = jnp.zeros_like(w)
        return jnp.concatenate([jnp.concatenate([w, z], axis=-1), jnp.concatenate([z, w], axis=-1)], axis=-2)
    pe = jnp.concatenate([cmp_pe, cmp_pe], axis=-1)
    return pe, bdiag(cmp_w1).astype(jnp.bfloat16), bdiag(cmp_w2).astype(jnp.bfloat16)


def _compress(rows2d, nb, pe, w1, w2, gk, bd):
    nseq = rows2d.shape[0] // (nb * CMP_BLOCK)
    full = lambda a: pl.BlockSpec(a.shape, lambda b: (0,) * a.ndim)
    args = (rows2d, rows2d, pe, w1, w2, jnp.concatenate([gk, gk]).reshape(1, LANE), bd)
    return pl.pallas_call(
        _compress_kernel,
        grid=(nseq,),
        in_specs=[pl.BlockSpec((nb * CMP_BLOCK, LANE), lambda b: (b, 0)),
                  pl.BlockSpec((nb * CMP_BLOCK, LANE), lambda b: (b, 1))] + [full(a) for a in args[2:]],
        out_specs=[pl.BlockSpec((nb, 2 * LANE), lambda b: (b, 0)), pl.BlockSpec((LANE, nb), lambda b: (0, b))],
        out_shape=[jax.ShapeDtypeStruct((nseq * nb, 2 * LANE), jnp.bfloat16),
                   jax.ShapeDtypeStruct((LANE, nseq * nb), jnp.bfloat16)],
        compiler_params=pltpu.CompilerParams(dimension_semantics=("arbitrary",),
                                             vmem_limit_bytes=VMEM_LIMIT),
        name="nsa_compress",
    )(*args)


def _nsa_prompt_kernel(qt_ref, gat_ref, ksel_ref, vselt_ref, kwin_ref, vwint_ref, kc_ref, vct_ref,
                       o_ref, selb_ref):
    f32, bf16 = jnp.float32, jnp.bfloat16
    tq = Q_BLOCK
    w = NSA_GROUP * tq
    g = pl.program_id(1)
    qi = pl.program_id(2)
    nb = kc_ref.shape[0]
    qt = jnp.concatenate([qt_ref[r * LANE:(r + 1) * LANE, :] for r in range(NSA_GROUP)], axis=1)
    qloc = lax.broadcasted_iota(jnp.int32, (1, w), 1) % tq
    qpos = qi * tq + qloc

    sc = jnp.dot(kc_ref[...], qt, preferred_element_type=f32)
    blk = lax.broadcasted_iota(jnp.int32, (nb, w), 0)
    vis = blk * CMP_BLOCK + (CMP_BLOCK - 1) <= qpos
    sc = jnp.where(vis, sc, NEG_INF)
    pc = jnp.where(vis, jnp.exp(sc - jnp.max(sc, axis=0, keepdims=True)), 0.0)
    pc = pc / jnp.maximum(jnp.sum(pc, axis=0, keepdims=True), 1e-30)
    o_c = jnp.dot(vct_ref[...], pc.astype(bf16), preferred_element_type=f32)

    imp = sum(pc[:, r * tq:(r + 1) * tq] for r in range(NSA_GROUP))
    blk1 = blk[:, 0:tq]
    cur = qpos[:, 0:tq] // SEL_BLOCK
    imp = jnp.where((blk1 == cur) | (blk1 == 0), FORCE_SCORE, jnp.where(blk1 <= cur, imp, -1.0))
    for _ in range(min(SEL_TOPK, nb)):
        mx = jnp.max(imp, axis=0, keepdims=True)
        idx = jnp.min(jnp.where(imp == mx, blk1, BIG_ID), axis=0, keepdims=True)
        imp = jnp.where(blk1 == idx, -jnp.inf, imp)
    selb = jnp.where(imp == -jnp.inf, 0.0, NEG_INF)
    selb_ref[...] = jnp.concatenate([selb] * NSA_GROUP, axis=1)

    krow = lax.broadcasted_iota(jnp.int32, (tq, w), 0)
    causal = krow <= qloc

    def sel_bias(kt):
        per = tq // SEL_BLOCK
        return jnp.concatenate([jnp.broadcast_to(selb_ref[pl.ds(kt * per + j, 1), :], (SEL_BLOCK, w))
                                for j in range(per)], axis=0)

    def sel_tile(kt):
        k0 = pl.multiple_of(kt * tq, tq)
        s = jnp.dot(ksel_ref[pl.ds(k0, tq), :], qt, preferred_element_type=f32) + sel_bias(kt)
        return s, vselt_ref[:, pl.ds(k0, tq)]

    carry = _tile_loop(qi, sel_tile, _softmax_init(w))
    s, vt = sel_tile(qi)
    m_s, l_s, acc_s = _softmax_step([(jnp.where(causal, s, NEG_INF), vt)], carry)
    o_s = acc_s / l_s

    nwin = WINDOW // tq
    tiles = []
    for j in range(nwin + 1):
        kt = qi - nwin + j
        k0 = pl.multiple_of(jnp.maximum(kt, 0) * tq, tq)
        s = jnp.dot(kwin_ref[pl.ds(k0, tq), :], qt, preferred_element_type=f32)
        ok = kt >= 0
        if j == 0:
            ok = ok & (krow > qloc)
        elif j == nwin:
            ok = ok & causal
        tiles.append((jnp.where(ok, s, NEG_INF), vwint_ref[:, pl.ds(k0, tq)]))
    m_w, l_w, acc_w = _softmax_step(tiles, _softmax_init(w))
    o_w = acc_w / l_w

    for r in range(NSA_GROUP):
        gate = lambda br: gat_ref[pl.ds((g * NSA_GROUP + r) * 3 + br, 1), :]
        sl = slice(r * tq, (r + 1) * tq)
        o_ref[r * HEAD_DIM:(r + 1) * HEAD_DIM, :] = (
            gate(0) * o_c[:, sl] + gate(1) * o_s[:, sl] + gate(2) * o_w[:, sl])


def _nsa_prompt(qat, gat, ksel, vselt, kwin, vwint, kc, vct, nseq, seq_len):
    n = qat.shape[1]
    nq = seq_len // Q_BLOCK
    nb = seq_len // CMP_BLOCK
    gw = NSA_GROUP * LANE
    return pl.pallas_call(
        _nsa_prompt_kernel,
        grid=(nseq, NSA_KV_HEADS, nq),
        in_specs=[pl.BlockSpec((gw, Q_BLOCK), lambda b, g, q: (g, b * nq + q)),
                  pl.BlockSpec((LANE, Q_BLOCK), lambda b, g, q: (0, b * nq + q)),
                  pl.BlockSpec((seq_len, LANE), lambda b, g, q: (b, g)),
                  pl.BlockSpec((HEAD_DIM, seq_len), lambda b, g, q: (g, b)),
                  pl.BlockSpec((seq_len, LANE), lambda b, g, q: (b, g)),
                  pl.BlockSpec((HEAD_DIM, seq_len), lambda b, g, q: (g, b)),
                  pl.BlockSpec((nb, LANE), lambda b, g, q: (b, g)),
                  pl.BlockSpec((HEAD_DIM, nb), lambda b, g, q: (g, b))],
        out_specs=pl.BlockSpec((NSA_GROUP * HEAD_DIM, Q_BLOCK), lambda b, g, q: (g, b * nq + q)),
        out_shape=jax.ShapeDtypeStruct((NSA_Q_W, n), jnp.float32),
        scratch_shapes=[pltpu.VMEM((nb, NSA_GROUP * Q_BLOCK), jnp.float32)],
        compiler_params=pltpu.CompilerParams(dimension_semantics=("arbitrary", "arbitrary", "arbitrary"),
                                             vmem_limit_bytes=VMEM_LIMIT),
        name="nsa_prompt",
    )(qat, gat, ksel, vselt, kwin, vwint, kc, vct)


FOX_TQ = 512
FOX_TK = 128


def _fox_prompt_kernel(qt_ref, kb_ref, vbt_ref, o_ref):
    f32 = jnp.float32
    tq, tk = FOX_TQ, FOX_TK
    qi = pl.program_id(2)
    qt = qt_ref[...]

    def tile(kt):
        k0 = pl.multiple_of(kt * tk, tk)
        return jnp.dot(kb_ref[pl.ds(k0, tk), :], qt, preferred_element_type=f32), vbt_ref[:, pl.ds(k0, tk)]

    ndiag = tq // tk
    carry = _tile_loop(qi * ndiag, tile, _softmax_init(tq))
    krow = lax.broadcasted_iota(jnp.int32, (tk, tq), 0)
    qloc = lax.broadcasted_iota(jnp.int32, (tk, tq), 1)
    diag = []
    for j in range(ndiag):
        s, vt = tile(qi * ndiag + j)
        diag.append((jnp.where(krow + j * tk <= qloc, s, NEG_INF), vt))
    m, l, acc = _softmax_step(diag, carry)
    o_ref[...] = acc / l


def _fox_prompt(qbt, kb, vbt, nseq, seq_len):
    n = qbt.shape[1]
    nq = seq_len // FOX_TQ
    return pl.pallas_call(
        _fox_prompt_kernel,
        grid=(nseq, FOX_HEADS, nq),
        in_specs=[pl.BlockSpec((LANE, FOX_TQ), lambda b, h, q: (h, b * nq + q)),
                  pl.BlockSpec((seq_len, LANE), lambda b, h, q: (b, h)),
                  pl.BlockSpec((HEAD_DIM, seq_len), lambda b, h, q: (h, b))],
        out_specs=pl.BlockSpec((HEAD_DIM, FOX_TQ), lambda b, h, q: (h, b * nq + q)),
        out_shape=jax.ShapeDtypeStruct((FOX_W, n), jnp.float32),
        compiler_params=pltpu.CompilerParams(dimension_semantics=("arbitrary", "arbitrary", "arbitrary"),
                                             vmem_limit_bytes=VMEM_LIMIT),
        name="fox_prompt",
    )(qbt, kb, vbt)


def _slope_rows(shape, rows_per_head):
    hd = lax.broadcasted_iota(jnp.int32, shape, 0) // rows_per_head
    return pltpu.bitcast((126 - hd) << 23, jnp.float32)


def _pad_rows(x, rows):
    return jnp.concatenate([x, jnp.zeros((rows - x.shape[0], x.shape[1]), x.dtype)], axis=0)


def _nt_dot(a, b):
    return lax.dot_general(a, b, (((1,), (1,)), ((), ())), preferred_element_type=jnp.float32)


def _joint_softmax(parts):
    m = functools.reduce(jnp.maximum, [jnp.max(p, axis=1, keepdims=True) for p in parts])
    es = [jnp.exp(p - m) for p in parts]
    return es, sum(jnp.sum(e, axis=1, keepdims=True) for e in es)


def _nsa_sample_kernel(npages, ns, pt_ref, *refs):
    f32, bf16 = jnp.float32, jnp.bfloat16
    pages = refs[:npages]
    (e_ref, qa_ref, ga_ref, rnew_ref, wnew_ref, state_ref, pe_ref, w1_ref, w2_ref, gk_ref, bd_ref,
     o_ref, bufk_ref, bufv_ref) = refs[npages:]
    page = pages[0].shape[1]
    past = npages * page
    nb = past // CMP_BLOCK
    nrow = NSA_HEADS * ns

    for p in range(npages):
        bufk_ref[p * page:(p + 1) * page, :] = pages[p][0, :, 0:LANE]
        bufv_ref[p * page:(p + 1) * page, :] = pages[p][0, :, LANE:2 * LANE]
    kc = _summarize(bufk_ref, 0, pe_ref, w1_ref, w2_ref, nb)
    kc = (kc * lax.rsqrt(_group_mean_sq(kc, bd_ref[...]) + EPS) * gk_ref[...]).astype(bf16)
    vc = _summarize(bufv_ref, 1, pe_ref, w1_ref, w2_ref, nb).astype(bf16)

    lane8 = _lane(ns)
    ql = qa_ref[...]
    qrows = []
    for hd in range(NSA_HEADS):
        g = hd // NSA_GROUP
        t = ql[:, (hd // 2) * LANE:(hd // 2 + 1) * LANE]
        if hd % 2 != g:
            t = pltpu.roll(t, HEAD_DIM, axis=1)
        qrows.append(jnp.where(lane8 // HEAD_DIM == g, t, 0.0))
    qb = (jnp.concatenate(qrows, axis=0) * SCALE).astype(bf16)

    def geom(width):
        tok = lax.broadcasted_iota(jnp.int32, (nrow, width), 0) % ns
        col = lax.broadcasted_iota(jnp.int32, (nrow, width), 1)
        return tok, col, _slope_rows((nrow, width), ns)

    tok, col, slope = geom(nb)
    lc = _nt_dot(qb, kc) - slope * (past + tok - (col * CMP_BLOCK + CMP_BLOCK - 1)).astype(f32)
    (ec,), lsum = _joint_softmax([lc])
    pc = ec / lsum
    o_c = jnp.dot(pc.astype(bf16), vc, preferred_element_type=f32)

    imp = jnp.concatenate([sum(pc[(g * NSA_GROUP + r) * ns:(g * NSA_GROUP + r + 1) * ns] for r in range(NSA_GROUP))
                           for g in range(NSA_KV_HEADS)], axis=0)
    bcol = lax.broadcasted_iota(jnp.int32, imp.shape, 1)
    imp = jnp.where(bcol == 0, FORCE_SCORE, imp)
    rank = jnp.zeros(imp.shape, jnp.int32)
    for c in range(nb):
        other = jnp.broadcast_to(imp[:, c:c + 1], imp.shape)
        rank = rank + jnp.where((other > imp) | ((other == imp) & (bcol > c)), 1, 0)
    sel = jnp.where(rank < SEL_TOPK - 1, 1.0, 0.0)
    sel = jnp.concatenate([sel[g * ns:(g + 1) * ns] for g in range(NSA_KV_HEADS) for _ in range(NSA_GROUP)], axis=0)
    selexp = jnp.dot(sel.astype(bf16), e_ref[...], preferred_element_type=f32)

    def new_tile(k_new):
        tok, col, slope = geom(LANE)
        s = _nt_dot(qb, _pad_rows(k_new, LANE).astype(bf16))
        return jnp.where(col <= tok, s - slope * (tok - col).astype(f32), NEG_INF)

    def weighted(es, vs, lsum):
        acc = sum(jnp.dot(e.astype(bf16), v, preferred_element_type=f32) for e, v in zip(es, vs))
        return acc / lsum

    tok, col, slope = geom(past)
    ls = jnp.concatenate([_nt_dot(qb, pages[p][0, :, 2 * LANE:3 * LANE].astype(bf16)) for p in range(npages)],
                         axis=1)
    ls = jnp.where(selexp > 0.5, ls - slope * (past + tok - col).astype(f32), NEG_INF)
    (es, en), lsum = _joint_softmax([ls, new_tile(rnew_ref[:, 2 * LANE:3 * LANE])])
    o_s = weighted([es[:, p * page:(p + 1) * page] for p in range(npages)] + [en],
                   [pages[p][0, :, 3 * LANE:4 * LANE].astype(bf16) for p in range(npages)]
                   + [_pad_rows(rnew_ref[:, 3 * LANE:4 * LANE], LANE).astype(bf16)], lsum)

    wbuf = state_ref.shape[1]
    tok, col, slope = geom(wbuf)
    lw = _nt_dot(qb, state_ref[0, :, 0:LANE].astype(bf16))
    lw = jnp.where(col > tok + (wbuf - WINDOW), lw - slope * (wbuf + tok - col).astype(f32), NEG_INF)
    (ew, en), lsum = _joint_softmax([lw, new_tile(wnew_ref[:, 0:LANE])])
    o_w = weighted([ew, en], [state_ref[0, :, LANE:2 * LANE].astype(bf16),
                              _pad_rows(wnew_ref[:, LANE:2 * LANE], LANE).astype(bf16)], lsum)

    ga = ga_ref[...]
    gate = lambda br: jnp.concatenate(
        [jnp.broadcast_to(ga[:, hd * 3 + br:hd * 3 + br + 1], (ns, LANE)) for hd in range(NSA_HEADS)], axis=0)
    o = gate(0) * o_c + gate(1) * o_s + gate(2) * o_w
    for j in range(NSA_HEADS // 2):
        g = (2 * j) // NSA_GROUP
        a = o[2 * j * ns:(2 * j + 1) * ns]
        b = o[(2 * j + 1) * ns:(2 * j + 2) * ns]
        if g == 0:
            b = pltpu.roll(b, HEAD_DIM, axis=1)
        else:
            a = pltpu.roll(a, HEAD_DIM, axis=1)
        o_ref[:, j * LANE:(j + 1) * LANE] = jnp.where(lane8 < HEAD_DIM, a, b)


def _nsa_sample(page_table, cache2d, qa, ga, rows_new, win_new, state2d, pe, w1, w2, gk, bd):
    db, npages = page_table.shape
    page = cache2d.shape[1]
    ns = qa.shape[0] // db
    past = npages * page
    nb = past // CMP_BLOCK
    r = lax.broadcasted_iota(jnp.int32, (nb, past), 0)
    c = lax.broadcasted_iota(jnp.int32, (nb, past), 1)
    expand = jnp.where(c // SEL_BLOCK == r, 1.0, 0.0).astype(jnp.bfloat16)
    gk2 = jnp.concatenate([gk, gk]).reshape(1, LANE)
    full = lambda a: pl.BlockSpec(a.shape, lambda b, pt: (0,) * a.ndim)
    tok = lambda w: pl.BlockSpec((ns, w), lambda b, pt: (b, 0))
    page_specs = [pl.BlockSpec((1, page, cache2d.shape[2]), functools.partial(lambda p, b, pt: (pt[b, p], 0, 0), p))
                  for p in range(npages)]
    return pl.pallas_call(
        functools.partial(_nsa_sample_kernel, npages, ns),
        grid_spec=pltpu.PrefetchScalarGridSpec(
            num_scalar_prefetch=1, grid=(db,),
            in_specs=page_specs + [full(expand), tok(NSA_Q_W), tok(LANE), tok(512), tok(256),
                                   pl.BlockSpec((1,) + state2d.shape[1:], lambda b, pt: (b, 0, 0)),
                                   full(pe), full(w1), full(w2), full(gk2), full(bd)],
            out_specs=tok(NSA_Q_W),
            scratch_shapes=[pltpu.VMEM((past, LANE), jnp.float32), pltpu.VMEM((past, LANE), jnp.float32)]),
        out_shape=jax.ShapeDtypeStruct((db * ns, NSA_Q_W), jnp.float32),
        compiler_params=pltpu.CompilerParams(dimension_semantics=("arbitrary",),
                                             vmem_limit_bytes=VMEM_LIMIT),
        name="nsa_sample",
    )(page_table, *([cache2d] * npages), expand, qa, ga, rows_new, win_new, state2d, pe, w1, w2, gk2, bd)


def _fox_sample_kernel(npages, ns, pt_ref, *refs):
    f32, bf16 = jnp.float32, jnp.bfloat16
    kv = refs[:npages]
    lft = refs[npages:2 * npages]
    qb_ref, knew_ref, lfnew_ref, o_ref = refs[2 * npages:]
    page = kv[0].shape[1]
    nrow = FOX_HEADS * ns

    r_io = lax.broadcasted_iota(jnp.int32, (page, page), 0)
    c_io = lax.broadcasted_iota(jnp.int32, (page, page), 1)
    triu = jnp.where(r_io <= c_io, 1.0, 0.0).astype(bf16)
    carry = jnp.zeros((FOX_HEADS, 1), f32)
    negc = []
    for t in range(npages + 1):
        lf = lft[t][0] if t < npages else lfnew_ref[0]
        ct = carry + sum(jnp.dot(pc_, triu, preferred_element_type=f32) for pc_ in _split3(lf))
        carry = ct[:, page - 1:page]
        negc.append(jnp.concatenate([jnp.broadcast_to(-ct[hd:hd + 1], (ns, page)) for hd in range(FOX_HEADS)],
                                    axis=0))

    lane8 = _lane(ns)
    ql = qb_ref[...] * SCALE
    tok = lax.broadcasted_iota(jnp.int32, (2 * ns, page), 0) % ns
    col = lax.broadcasted_iota(jnp.int32, (2 * ns, page), 1)
    for j in range(FOX_HEADS // 2):
        t = ql[:, j * LANE:(j + 1) * LANE]
        qj = jnp.concatenate([jnp.where(lane8 < HEAD_DIM, t, 0.0), jnp.where(lane8 >= HEAD_DIM, t, 0.0)],
                             axis=0).astype(bf16)
        ksl = slice(j * LANE, (j + 1) * LANE)
        vsl = slice(FOX_W + j * LANE, FOX_W + (j + 1) * LANE)
        rows = slice(2 * j * ns, (2 * j + 2) * ns)
        parts = [_nt_dot(qj, kv[p][0, :, ksl].astype(bf16)) + negc[p][rows] for p in range(npages)]
        s_new = _nt_dot(qj, _pad_rows(knew_ref[:, ksl], page).astype(bf16)) + negc[npages][rows]
        parts.append(jnp.where(col <= tok, s_new, NEG_INF))
        es, lsum = _joint_softmax(parts)
        vs = [kv[p][0, :, vsl].astype(bf16) for p in range(npages)] + [_pad_rows(knew_ref[:, vsl], page).astype(bf16)]
        acc = sum(jnp.dot(e.astype(bf16), v, preferred_element_type=f32) for e, v in zip(es, vs)) / lsum
        o_ref[:, j * LANE:(j + 1) * LANE] = jnp.where(lane8 < HEAD_DIM, acc[0:ns], acc[ns:2 * ns])


def _fox_sample(page_table, kv2d, lft, qb, fox_new, lft_new):
    db, npages = page_table.shape
    page = kv2d.shape[1]
    ns = qb.shape[0] // db
    tok = lambda w: pl.BlockSpec((ns, w), lambda b, pt: (b, 0))
    pg = lambda a: [pl.BlockSpec((1,) + a.shape[1:], functools.partial(lambda p, b, pt: (pt[b, p], 0, 0), p))
                    for p in range(npages)]
    return pl.pallas_call(
        functools.partial(_fox_sample_kernel, npages, ns),
        grid_spec=pltpu.PrefetchScalarGridSpec(
            num_scalar_prefetch=1, grid=(db,),
            in_specs=pg(kv2d) + pg(lft) + [tok(FOX_W), tok(2 * FOX_W),
                                            pl.BlockSpec((1, FOX_HEADS, page), lambda b, pt: (b, 0, 0))],
            out_specs=tok(FOX_W)),
        out_shape=jax.ShapeDtypeStruct((db * ns, FOX_W), jnp.float32),
        compiler_params=pltpu.CompilerParams(dimension_semantics=("arbitrary",),
                                             vmem_limit_bytes=VMEM_LIMIT),
        name="fox_sample",
    )(page_table, *([kv2d] * npages), *([lft] * npages), qb, fox_new, lft_new)


def _fox_decode_kernel(npages, ns, pt_ref, *refs):
    f32, bf16 = jnp.float32, jnp.bfloat16
    kv = refs[:npages]
    lfe = refs[npages:2 * npages]
    qm_ref, knew_ref, lfnew_ref, o_ref = refs[2 * npages:]
    page = kv[0].shape[1]
    nh = FOX_HEADS
    rows, cols = page * nh, nh * ns
    qm = (qm_ref[0] * SCALE).astype(bf16)

    r_io = lax.broadcasted_iota(jnp.int32, (rows, cols), 0)
    c_io = lax.broadcasted_iota(jnp.int32, (rows, cols), 1)
    head_mask = jnp.where(r_io % nh == c_io // ns, 0.0, NEG_INF)
    tri = jnp.where(lax.broadcasted_iota(jnp.int32, (page, page), 1)
                    <= lax.broadcasted_iota(jnp.int32, (page, page), 0), 1.0, 0.0).astype(bf16)

    carry_c = jnp.zeros((1, cols), f32)
    state = _softmax_init(cols)
    for p in range(npages):
        k2d = kv[p][0, :, 0].reshape(rows, HEAD_DIM).astype(bf16)
        v2d = kv[p][0, :, 1].reshape(rows, HEAD_DIM).astype(bf16)
        negc = carry_c + sum(jnp.dot(tri, piece, preferred_element_type=f32) for piece in _split3(-lfe[p][0]))
        carry_c = negc[page - 1:page, :]
        bias = jnp.broadcast_to(negc[:, None, :], (page, nh, cols)).reshape(rows, cols) + head_mask
        state = _softmax_rows_step(_nt_dot(k2d, qm) + bias, v2d, state)

    nrow = ns * nh
    lfn = -lfnew_ref[0]
    run, negs = carry_c, []
    for t in range(ns):
        run = run + lfn[t:t + 1]
        negs.append(jnp.broadcast_to(run, (nh, cols)))
    negc = jnp.concatenate(negs, axis=0)
    r_io = lax.broadcasted_iota(jnp.int32, (nrow, cols), 0)
    c_io = lax.broadcasted_iota(jnp.int32, (nrow, cols), 1)
    ok = (r_io % nh == c_io // ns) & (r_io // nh <= c_io % ns)
    k2d = knew_ref[0, :, 0].reshape(nrow, HEAD_DIM).astype(bf16)
    v2d = knew_ref[0, :, 1].reshape(nrow, HEAD_DIM).astype(bf16)
    s = jnp.where(ok, _nt_dot(k2d, qm) + negc, NEG_INF)
    m, l, acc = _softmax_rows_step(s, v2d, state)
    o_ref[0] = acc / l


def _softmax_rows_step(s, v2d, carry):
    m, l, acc = carry
    m_new = jnp.maximum(m, jnp.max(s, axis=0, keepdims=True))
    alpha = jnp.exp(m - m_new)
    p = jnp.exp(s - m_new)
    l = alpha * l + jnp.sum(p, axis=0, keepdims=True)
    acc = alpha * acc + lax.dot_general(v2d, p.astype(jnp.bfloat16), (((0,), (0,)), ((), ())),
                                        preferred_element_type=jnp.float32)
    return m_new, l, acc


def _fox_decode(page_table, cache_kv, lfe, qm, kv_new, lfe_new):
    db, npages = page_table.shape
    ns = kv_new.shape[1]
    blk = lambda a: pl.BlockSpec((1,) + a.shape[1:], lambda b, pt: (b,) + (0,) * (a.ndim - 1))
    pg = lambda a: [pl.BlockSpec((1,) + a.shape[1:],
                                 functools.partial(lambda p, nd, b, pt: (pt[b, p],) + (0,) * nd, p, a.ndim - 1))
                    for p in range(npages)]
    return pl.pallas_call(
        functools.partial(_fox_decode_kernel, npages, ns),
        grid_spec=pltpu.PrefetchScalarGridSpec(
            num_scalar_prefetch=1, grid=(db,),
            in_specs=pg(cache_kv) + pg(lfe) + [blk(qm), blk(kv_new), blk(lfe_new)],
            out_specs=blk(qm)),
        out_shape=jax.ShapeDtypeStruct(qm.shape, jnp.float32),
        compiler_params=pltpu.CompilerParams(dimension_semantics=("arbitrary",),
                                             vmem_limit_bytes=VMEM_LIMIT),
        name="fox_decode",
    )(page_table, *([cache_kv] * npages), *([lfe] * npages), qm, kv_new, lfe_new)


BIG_ID = 1 << 20


def _topk_rows(s, k, ids):
    ids = ids.astype(jnp.float32)
    vals, idxs = [], []
    for _ in range(k):
        m = jnp.max(s, axis=0, keepdims=True)
        idx = jnp.min(jnp.where(s == m, ids, float(BIG_ID)), axis=0, keepdims=True)
        vals.append(m)
        idxs.append(idx)
        s = jnp.where(ids == idx, -jnp.inf, s)
    return jnp.concatenate(vals, axis=0), jnp.concatenate(idxs, axis=0).astype(jnp.int32)


def _pick_rows(sel, table):
    out = jnp.zeros(sel.shape, table.dtype)
    for r in range(table.shape[0]):
        out = jnp.where(sel == r, table[r:r + 1, :], out)
    return out


def _merge_route_kernel(x_ref, on_ref, of_ref, mg_ref, wun_ref, wuf_ref, wo_ref, nf_ref, wqt_ref, sk_ref,
                        x1_ref, h2_ref, i1_ref, i2_ref, g_ref):
    tm = x_ref.shape[0]
    f32, bf16 = jnp.float32, jnp.bfloat16
    tdot = lambda ot, wgt: lax.dot_general(ot.astype(bf16), wgt, (((0,), (0,)), ((), ())),
                                           preferred_element_type=f32)
    a = tdot(on_ref[...], wun_ref[...])
    b = tdot(of_ref[...], wuf_ref[...])
    mixed = mg_ref[:, 0:D_MODEL] * a + mg_ref[:, D_MODEL:2 * D_MODEL] * b
    x1 = x_ref[...] + jnp.dot(mixed.astype(bf16), wo_ref[...], preferred_element_type=f32)
    x1_ref[...] = x1
    h2 = x1 * lax.rsqrt(jnp.mean(x1 * x1, axis=-1, keepdims=True) + EPS) * nf_ref[...]
    h2b = h2.astype(bf16)
    h2_ref[...] = h2b

    nk = PEER_N_KEYS
    key_ids = lax.broadcasted_iota(jnp.int32, (nk, tm), 0)
    io16 = lax.broadcasted_iota(jnp.int32, (PEER_TOPK, tm), 0)
    io8 = lax.broadcasted_iota(jnp.int32, (8, tm), 0)
    cand_ids = jnp.concatenate([io16] + [a_ * PEER_TOPK + io8 for a_ in range(1, 8)]
                               + [(io8 + 8) * PEER_TOPK], axis=0)
    for h in range(PEER_HEADS):
        sv, si = [], []
        for p in range(2):
            hp = 2 * h + p
            qt = lax.dot_general(wqt_ref[hp * PEER_DK_HALF:(hp + 1) * PEER_DK_HALF, :], h2b,
                                 (((1,), (1,)), ((), ())), preferred_element_type=f32)
            st = jnp.dot(sk_ref[hp], qt.astype(bf16), preferred_element_type=f32)
            v, i = _topk_rows(st, PEER_TOPK, key_ids)
            sv.append(v)
            si.append(i)
        s1, s2 = sv
        cand = jnp.concatenate([s1[0:1] + s2] + [s1[a_:a_ + 1] + s2[0:8] for a_ in range(1, 8)]
                               + [s1[8:16] + s2[0:1]], axis=0)
        top, fid = _topk_rows(cand, PEER_TOPK, cand_ids)
        e = jnp.exp(top - jnp.max(top, axis=0, keepdims=True))
        g = e / jnp.sum(e, axis=0, keepdims=True)
        sl = slice(h * PEER_TOPK, (h + 1) * PEER_TOPK)
        i1_ref[:, sl] = _pick_rows(fid >> 4, si[0]).T
        i2_ref[:, sl] = _pick_rows(fid & (PEER_TOPK - 1), si[1]).T
        g_ref[:, sl] = g.T


def _merge_route(x2d, o_nsa, o_fox, mg, w_up_nsa, w_up_fox, w_out, norm_ffn, wq_t, sub_keys, tm=256):
    n = x2d.shape[0]
    row = lambda w: pl.BlockSpec((tm, w), lambda i: (i, 0))
    col = lambda h: pl.BlockSpec((h, tm), lambda i: (0, i))
    full = lambda a: pl.BlockSpec(a.shape, lambda i: (0,) * a.ndim)
    args = (x2d, o_nsa, o_fox, mg, w_up_nsa, w_up_fox, w_out, norm_ffn.reshape(1, D_MODEL), wq_t, sub_keys)
    hk = PEER_HEADS * PEER_TOPK
    return pl.pallas_call(
        _merge_route_kernel,
        grid=(n // tm,),
        in_specs=[row(D_MODEL), col(NSA_Q_W), col(FOX_W), row(MERGE_W)] + [full(a) for a in args[4:]],
        out_specs=[row(D_MODEL), row(D_MODEL), row(hk), row(hk), row(hk)],
        out_shape=[jax.ShapeDtypeStruct((n, D_MODEL), jnp.float32),
                   jax.ShapeDtypeStruct((n, D_MODEL), jnp.bfloat16),
                   jax.ShapeDtypeStruct((n, hk), jnp.int32),
                   jax.ShapeDtypeStruct((n, hk), jnp.int32),
                   jax.ShapeDtypeStruct((n, hk), jnp.float32)],
        compiler_params=pltpu.CompilerParams(dimension_semantics=("arbitrary",),
                                             vmem_limit_bytes=VMEM_LIMIT),
        name="merge_route",
    )(*args)


def _peer_act_kernel(h2_ref, u_ref, i1_ref, i2_ref, act_ref):
    c = pl.program_id(1)
    ec = u_ref.shape[0]

    @pl.when(c == 0)
    def _():
        act_ref[...] = jnp.zeros_like(act_ref)

    a = lax.dot_general(h2_ref[...], u_ref[...], (((1,), (1,)), ((), ())),
                        preferred_element_type=jnp.float32)
    i1 = i1_ref[...]
    i2 = i2_ref[...]
    act = act_ref[...]
    for ii in range(ec // PEER_N_KEYS):
        got = jnp.take_along_axis(a[:, ii * PEER_N_KEYS:(ii + 1) * PEER_N_KEYS], i2, axis=1)
        act = jnp.where(i1 == c * (ec // PEER_N_KEYS) + ii, got, act)
    act_ref[...] = act


def _peer_act(h2b, u_b, i1, i2, tm=512, ec=2048):
    n = h2b.shape[0]
    hk = i1.shape[1]
    return pl.pallas_call(
        _peer_act_kernel,
        grid=(n // tm, u_b.shape[0] // ec),
        in_specs=[pl.BlockSpec((tm, D_MODEL), lambda t, c: (t, 0)),
                  pl.BlockSpec((ec, D_MODEL), lambda t, c: (c, 0)),
                  pl.BlockSpec((tm, hk), lambda t, c: (t, 0)),
                  pl.BlockSpec((tm, hk), lambda t, c: (t, 0))],
        out_specs=pl.BlockSpec((tm, hk), lambda t, c: (t, 0)),
        out_shape=jax.ShapeDtypeStruct((n, hk), jnp.float32),
        compiler_params=pltpu.CompilerParams(dimension_semantics=("arbitrary", "arbitrary"),
                                             vmem_limit_bytes=VMEM_LIMIT),
        name="peer_act",
    )(h2b, u_b, i1, i2)


def _peer_coef_kernel(act_ref, g_ref, i1_ref, i2_ref, c_ref, coef_ref):
    tm = act_ref.shape[0]
    nk = PEER_N_KEYS
    coef_ref[...] = g_ref[...] * jax.nn.gelu(act_ref[...])
    sub = lax.broadcasted_iota(jnp.int32, (nk, i1_ref.shape[1]), 0)

    def token(t):
        r1 = i1_ref[pl.ds(t, 1), :]
        r2 = i2_ref[pl.ds(t, 1), :]
        cf = coef_ref[pl.ds(t, 1), :]
        m1 = jnp.where(r1 == sub, cf, 0.0).astype(jnp.bfloat16)
        m2t = jnp.where(r2 == sub, 1.0, 0.0).astype(jnp.bfloat16)
        return lax.dot_general(m1, m2t, (((1,), (1,)), ((), ())), preferred_element_type=jnp.float32)

    def body(tg, carry):
        t0 = pl.multiple_of(tg * COEF_GROUP, COEF_GROUP)
        ct = jnp.stack([token(t0 + u) for u in range(COEF_GROUP)], axis=0)
        c_ref[:, pl.ds(t0, COEF_GROUP), :] = pltpu.einshape("tij->itj", ct).astype(c_ref.dtype)
        return carry

    lax.fori_loop(0, tm // COEF_GROUP, body, 0)


COEF_GROUP = 16


def _peer_coef(act, g, i1, i2, tm=128):
    n, hk = act.shape
    nk = PEER_N_KEYS
    row = pl.BlockSpec((tm, hk), lambda t: (t, 0))
    return pl.pallas_call(
        _peer_coef_kernel,
        grid=(n // tm,),
        in_specs=[row, row, row, row],
        out_specs=pl.BlockSpec((nk, tm, nk), lambda t: (0, t, 0)),
        out_shape=jax.ShapeDtypeStruct((nk, n, nk), jnp.bfloat16),
        scratch_shapes=[pltpu.VMEM((tm, hk), jnp.float32)],
        compiler_params=pltpu.CompilerParams(dimension_semantics=("arbitrary",),
                                             vmem_limit_bytes=VMEM_LIMIT),
        name="peer_coef",
    )(act, g, i1, i2)


def _peer_out_kernel(c_ref, v_ref, x1_ref, y_ref, acc_ref):
    k = pl.program_id(1)

    @pl.when(k == 0)
    def _():
        acc_ref[...] = x1_ref[...]

    acc = acc_ref[...]
    nk = PEER_N_KEYS
    for p in range(c_ref.shape[0] // 2):
        lhs = jnp.concatenate([c_ref[2 * p], c_ref[2 * p + 1]], axis=1)
        acc = acc + jnp.dot(lhs, v_ref[2 * p * nk:(2 * p + 2) * nk, :], preferred_element_type=jnp.float32)
    acc_ref[...] = acc

    @pl.when(k == pl.num_programs(1) - 1)
    def _():
        y_ref[...] = acc_ref[...]


def _peer_out(c3, v_b, x1, tm=1024, tk=2048):
    nk, n, _ = c3.shape
    ne = nk * nk
    return pl.pallas_call(
        _peer_out_kernel,
        grid=(n // tm, ne // tk),
        in_specs=[pl.BlockSpec((tk // nk, tm, nk), lambda t, k: (k, t, 0)),
                  pl.BlockSpec((tk, D_MODEL), lambda t, k: (k, 0)),
                  pl.BlockSpec((tm, D_MODEL), lambda t, k: (t, 0))],
        out_specs=pl.BlockSpec((tm, D_MODEL), lambda t, k: (t, 0)),
        out_shape=jax.ShapeDtypeStruct((n, D_MODEL), jnp.float32),
        scratch_shapes=[pltpu.VMEM((tm, D_MODEL), jnp.float32)],
        compiler_params=pltpu.CompilerParams(dimension_semantics=("arbitrary", "arbitrary"),
                                             vmem_limit_bytes=VMEM_LIMIT),
        name="peer_out",
    )(c3, v_b, x1)


def _peer_weights(w_up_nsa, w_up_fox, w_out, norm_ffn, peer_w_query, peer_sub_keys, peer_u, peer_v):
    bf16 = jnp.bfloat16
    return dict(w_up_nsa=w_up_nsa.astype(bf16), w_up_fox=w_up_fox.astype(bf16), w_out=w_out.astype(bf16),
                norm_ffn=norm_ffn, wq_t=peer_w_query.T.astype(bf16),
                sub_keys=peer_sub_keys.reshape(2 * PEER_HEADS, PEER_N_KEYS, PEER_DK_HALF).astype(bf16),
                u=peer_u.astype(bf16), v=peer_v.astype(bf16))


def _merge_peer(x2d, o_nsa, o_fox, mg, wts):
    x1, h2b, i1, i2, g = _merge_route(x2d, o_nsa, o_fox, mg, wts['w_up_nsa'], wts['w_up_fox'], wts['w_out'],
                                      wts['norm_ffn'], wts['wq_t'], wts['sub_keys'])
    act = _peer_act(h2b, wts['u'], i1, i2)
    c3 = _peer_coef(act, g, i1, i2)
    return _peer_out(c3, wts['v'], x1)


def kernel(x_prompt, x_sample, cache_nsa, cache_fox_kv, cache_fox_logf, state_nsa_win, page_table,
           norm_attn, w_in, fox_f_bias, nsa_q_norm, nsa_k_norm, fox_q_norm, fox_k_norm,
           cmp_pe, cmp_w1, cmp_w2, w_up_nsa, w_up_fox, w_out, norm_ffn,
           peer_w_query, peer_sub_keys, peer_u, peer_v):
    w_front = _front_weights(w_in)
    bd = _block_diag_mean()
    wts = _peer_weights(w_up_nsa, w_up_fox, w_out, norm_ffn, peer_w_query, peer_sub_keys, peer_u, peer_v)
    cmp_wts = _compress_weights(cmp_pe, cmp_w1, cmp_w2)

    bp, seq, _ = x_prompt.shape
    n_p = bp * seq
    (rows_p, win_p2d, fox_p2d, logf_p2d, mg_p, qat, gat, ksel, kwin, vselt, vwint, qbt, kb, vbt) = _front_attn(
        x_prompt.reshape(n_p, D_MODEL), seq, norm_attn, w_front, bd, fox_f_bias,
        nsa_q_norm, nsa_k_norm, fox_q_norm, fox_k_norm)
    kc, vct = _compress(rows_p, seq // CMP_BLOCK, *cmp_wts, nsa_k_norm[0], bd)
    o_nsa_t = _nsa_prompt(qat, gat, ksel, vselt, kwin, vwint, kc, vct, bp, seq)
    o_fox_t = _fox_prompt(qbt, kb, vbt, bp, seq)
    y_p = _merge_peer(x_prompt.reshape(n_p, D_MODEL), o_nsa_t, o_fox_t, mg_p, wts).reshape(x_prompt.shape)
    nsa_p = rows_p.reshape(bp, seq, 4, NSA_KV_HEADS, HEAD_DIM)
    fox_p = fox_p2d.reshape(bp, seq, 2, FOX_HEADS, HEAD_DIM)
    logf_p = logf_p2d.reshape(bp, seq, FOX_HEADS)
    win_p = win_p2d.reshape(bp, seq, 2, NSA_KV_HEADS, HEAD_DIM)[:, seq - min(WINDOW, seq):]

    db, ns, _ = x_sample.shape
    n_s = db * ns
    n_pool, page = cache_nsa.shape[:2]
    wbuf = state_nsa_win.shape[1]
    qa_s, rows_s, win_s2d, ga_s, qb_s, fox_s2d, logf_s2d, mg_s = _front(
        x_sample.reshape(n_s, D_MODEL), norm_attn, w_front, bd, fox_f_bias,
        nsa_q_norm, nsa_k_norm, fox_q_norm, fox_k_norm)
    o_nsa_s = _nsa_sample(page_table, cache_nsa.reshape(n_pool, page, 4 * LANE), qa_s, ga_s, rows_s, win_s2d,
                          state_nsa_win.reshape(db, wbuf, 2 * LANE), *cmp_wts, nsa_k_norm[0], bd)
    qm = qb_s.reshape(db, ns, FOX_HEADS, HEAD_DIM).transpose(0, 2, 1, 3).reshape(db, FOX_HEADS * ns, HEAD_DIM)
    o_fox_d = _fox_decode(page_table, cache_fox_kv, jnp.repeat(cache_fox_logf, ns, axis=2), qm,
                          fox_s2d.reshape(db, ns, 2, FOX_HEADS, HEAD_DIM),
                          jnp.repeat(logf_s2d.reshape(db, ns, FOX_HEADS), ns, axis=2))
    o_fox_st = o_fox_d.reshape(db, HEAD_DIM, FOX_HEADS, ns).transpose(2, 1, 0, 3).reshape(FOX_W, n_s)
    y_s = _merge_peer(x_sample.reshape(n_s, D_MODEL), o_nsa_s.T, o_fox_st, mg_s, wts).reshape(x_sample.shape)
    nsa_s = rows_s.reshape(db, ns, 4, NSA_KV_HEADS, HEAD_DIM)
    fox_s = fox_s2d.reshape(db, ns, 2, FOX_HEADS, HEAD_DIM)
    logf_s = logf_s2d.reshape(db, ns, FOX_HEADS)
    win_s = jnp.concatenate([state_nsa_win[:, ns:], win_s2d.reshape(db, ns, 2, NSA_KV_HEADS, HEAD_DIM)], axis=1)
    return (y_p, y_s, nsa_p, fox_p, logf_p, win_p, nsa_s, fox_s, logf_s, win_s)
```

```python
import functools

import jax
import jax.numpy as jnp
from jax import lax
from jax.experimental import pallas as pl
from jax.experimental.pallas import tpu as pltpu

D_MODEL = 1024
HEAD_DIM = 64
NSA_HEADS = 8
NSA_KV_HEADS = 2
NSA_GROUP = NSA_HEADS // NSA_KV_HEADS
CMP_BLOCK = 64
SEL_BLOCK = CMP_BLOCK
SEL_TOPK = 16
WINDOW = 512
FOX_HEADS = 8
Q_BLOCK = 128
PEER_HEADS = 8
PEER_N_KEYS = 128
PEER_DK = 256
PEER_DK_HALF = PEER_DK // 2
PEER_TOPK = 16
PEER_CHUNK = 256

NSA_Q_W = NSA_HEADS * HEAD_DIM
NSA_KV_W = 6 * NSA_KV_HEADS * HEAD_DIM
NSA_GATE_W = 3 * NSA_HEADS
FOX_W = FOX_HEADS * HEAD_DIM
FOX_QKV_W = 3 * FOX_W
FOX_F_W = FOX_HEADS
MERGE_W = 2 * D_MODEL
SPLIT_Q_A = NSA_Q_W
SPLIT_KV_A = SPLIT_Q_A + NSA_KV_W
SPLIT_G_A = SPLIT_KV_A + NSA_GATE_W
SPLIT_QKV_B = SPLIT_G_A + FOX_QKV_W
SPLIT_F_B = SPLIT_QKV_B + FOX_F_W
IN_WIDTH = SPLIT_F_B + MERGE_W

SCALE = HEAD_DIM ** -0.5
FORCE_SCORE = float(NSA_GROUP + 1)
NEG_INF = -1e30
EPS = 1e-6

LANE = 128
VMEM_LIMIT = 48 * 1024 * 1024


def _group_mean_sq(x, bd):
    sq = x * x
    hi = sq.astype(jnp.bfloat16)
    lo = (sq - hi.astype(jnp.float32)).astype(jnp.bfloat16)
    return (jnp.dot(hi, bd, preferred_element_type=jnp.float32)
            + jnp.dot(lo, bd, preferred_element_type=jnp.float32))


def _head_rms(x, g, bd):
    outs = []
    for c in range(x.shape[1] // LANE):
        xc = x[:, c * LANE:(c + 1) * LANE]
        outs.append(xc * lax.rsqrt(_group_mean_sq(xc, bd) + EPS) * g)
    return outs[0] if len(outs) == 1 else jnp.concatenate(outs, axis=1)


def _front_kernel(x_ref, na_ref, w_ref, bd_ref, fb_ref, gq_a_ref, gk_sel_ref, gk_win_ref,
                  gq_b_ref, gk_b_ref,
                  qa_ref, rows_ref, win_ref, ga_ref, qb_ref, fox_ref, logf_ref, mg_ref):
    x = x_ref[...]
    h = x * lax.rsqrt(jnp.mean(x * x, axis=-1, keepdims=True) + EPS) * na_ref[...]
    hb = h.astype(jnp.bfloat16)
    bd = bd_ref[...]

    def proj(c0, width):
        return jnp.dot(hb, w_ref[:, c0:c0 + width], preferred_element_type=jnp.float32)

    c = 0
    qa_ref[...] = _head_rms(proj(c, NSA_Q_W), gq_a_ref[...], bd)
    c += NSA_Q_W
    rows_ref[:, 0:256] = proj(c, 256)
    rows_ref[:, 256:384] = _head_rms(proj(c + 256, 128), gk_sel_ref[...], bd)
    rows_ref[:, 384:512] = proj(c + 384, 128)
    win_ref[:, 0:128] = _head_rms(proj(c + 512, 128), gk_win_ref[...], bd)
    win_ref[:, 128:256] = proj(c + 640, 128)
    c += NSA_KV_W
    qb_ref[...] = _head_rms(proj(c, FOX_W), gq_b_ref[...], bd)
    fox_ref[:, 0:FOX_W] = _head_rms(proj(c + FOX_W, FOX_W), gk_b_ref[...], bd)
    fox_ref[:, FOX_W:2 * FOX_W] = proj(c + 2 * FOX_W, FOX_W)
    c += FOX_QKV_W
    for j in range(MERGE_W // 512):
        mg_ref[:, j * 512:(j + 1) * 512] = jax.nn.sigmoid(proj(c + j * 512, 512))
    c += MERGE_W
    ga_ref[...] = jax.nn.sigmoid(proj(c, LANE))
    f = proj(c + LANE, LANE)[:, 0:FOX_F_W] + fb_ref[...]
    logf_ref[...] = jnp.minimum(f, 0.0) - jnp.log1p(jnp.exp(-jnp.abs(f)))


def _front(x2d, norm_attn, w_front, bd, fox_f_bias, nsa_q_norm, nsa_k_norm, fox_q_norm, fox_k_norm,
           tm=256):
    n = x2d.shape[0]
    wf = w_front.shape[1]
    two = lambda g: jnp.concatenate([g, g]).reshape(1, LANE)
    row = lambda w: pl.BlockSpec((tm, w), lambda i: (i, 0))
    full = lambda a: pl.BlockSpec(a.shape, lambda i: (0,) * a.ndim)
    args = (x2d, norm_attn.reshape(1, D_MODEL), w_front, bd, fox_f_bias.reshape(1, FOX_F_W),
            two(nsa_q_norm), two(nsa_k_norm[1]), two(nsa_k_norm[2]), two(fox_q_norm), two(fox_k_norm))
    widths = (NSA_Q_W, 512, 256, LANE, FOX_W, 2 * FOX_W, FOX_F_W, MERGE_W)
    return pl.pallas_call(
        _front_kernel,
        grid=(n // tm,),
        in_specs=[row(D_MODEL)] + [full(a) for a in args[1:]],
        out_specs=[row(w) for w in widths],
        out_shape=[jax.ShapeDtypeStruct((n, w), jnp.float32) for w in widths],
        compiler_params=pltpu.CompilerParams(dimension_semantics=("arbitrary",),
                                             vmem_limit_bytes=VMEM_LIMIT),
        name="front",
    )(*args)


FEAT = HEAD_DIM


def _lane(tm):
    return lax.broadcasted_iota(jnp.int32, (tm, LANE), 1)


def _expand_halves(x):
    lo = _lane(x.shape[0]) < HEAD_DIM
    return jnp.where(lo, x, 0.0), jnp.where(lo, pltpu.roll(x, HEAD_DIM, axis=1), 0.0)


def _split3(x):
    hi = x.astype(jnp.bfloat16)
    r = x - hi.astype(jnp.float32)
    mid = r.astype(jnp.bfloat16)
    lo = (r - mid.astype(jnp.float32)).astype(jnp.bfloat16)
    return hi, mid, lo


def _front_attn_kernel(seq_len, x_ref, na_ref, w_ref, bd_ref, fb_ref, gq_a_ref, gk_sel_ref, gk_win_ref,
                       gq_b_ref, gk_b_ref, place_ref,
                       rows_ref, win_ref, fox_ref, logf_ref, mg_ref,
                       qat_ref, gat_ref, ksel_ref, kwin_ref, vselt_ref, vwint_ref, qbt_ref, kb_ref, vbt_ref,
                       carry_ref):
    f32, bf16 = jnp.float32, jnp.bfloat16
    tm = x_ref.shape[0]
    i = pl.program_id(0)
    x = x_ref[...]
    h = x * lax.rsqrt(jnp.mean(x * x, axis=-1, keepdims=True) + EPS) * na_ref[...]
    hb = h.astype(bf16)
    bd = bd_ref[...]
    lane = _lane(tm)
    pos = (i * tm + lax.broadcasted_iota(jnp.int32, (tm, LANE), 0)) % seq_len
    kfeat = jnp.where(lane == FEAT, (pos // SEL_BLOCK).astype(f32),
                      jnp.where(lane == FEAT + 1, (pos % SEL_BLOCK).astype(f32), 0.0))

    def proj(c0, width):
        return jnp.dot(hb, w_ref[:, c0:c0 + width], preferred_element_type=f32)

    c = 0
    qa = _head_rms(proj(c, NSA_Q_W), gq_a_ref[...], bd)
    for j in range(NSA_HEADS // 2):
        for s, half in enumerate(_expand_halves(qa[:, j * LANE:(j + 1) * LANE])):
            hd = 2 * j + s
            slope = 2.0 ** -(hd + 1)
            qfeat = jnp.where(lane == FEAT, slope * SEL_BLOCK, jnp.where(lane == FEAT + 1, slope, 0.0))
            qat_ref[hd * LANE:(hd + 1) * LANE, :] = (half * SCALE + qfeat).T.astype(bf16)
    c += NSA_Q_W
    rows_ref[:, 0:256] = proj(c, 256)
    ksel = _head_rms(proj(c + 256, 128), gk_sel_ref[...], bd)
    rows_ref[:, 256:384] = ksel
    vsel = proj(c + 384, 128)
    rows_ref[:, 384:512] = vsel
    kwin = _head_rms(proj(c + 512, 128), gk_win_ref[...], bd)
    win_ref[:, 0:128] = kwin
    vwin = proj(c + 640, 128)
    win_ref[:, 128:256] = vwin
    for g, (ks_g, kw_g) in enumerate(zip(_expand_halves(ksel), _expand_halves(kwin))):
        ksel_ref[:, g * LANE:(g + 1) * LANE] = (ks_g + kfeat).astype(bf16)
        kwin_ref[:, g * LANE:(g + 1) * LANE] = (kw_g + kfeat).astype(bf16)
    vselt_ref[...] = vsel.T.astype(bf16)
    vwint_ref[...] = vwin.T.astype(bf16)
    c += NSA_KV_W

    qb = _head_rms(proj(c, FOX_W), gq_b_ref[...], bd)
    ones3 = jnp.where((lane >= FEAT) & (lane < FEAT + 3), 1.0, 0.0)
    for j in range(FOX_HEADS // 2):
        for s, half in enumerate(_expand_halves(qb[:, j * LANE:(j + 1) * LANE])):
            hd = 2 * j + s
            qbt_ref[hd * LANE:(hd + 1) * LANE, :] = (half * SCALE + ones3).T.astype(bf16)
    kbn = _head_rms(proj(c + FOX_W, FOX_W), gk_b_ref[...], bd)
    fox_ref[:, 0:FOX_W] = kbn
    for j in range(FOX_HEADS // 2):
        vb = proj(c + 2 * FOX_W + j * LANE, LANE)
        fox_ref[:, FOX_W + j * LANE:FOX_W + (j + 1) * LANE] = vb
        vbt_ref[j * LANE:(j + 1) * LANE, :] = vb.T.astype(bf16)
    c += FOX_QKV_W
    for j in range(MERGE_W // 512):
        mg_ref[:, j * 512:(j + 1) * 512] = jax.nn.sigmoid(proj(c + j * 512, 512))
    c += MERGE_W
    gat_ref[...] = jax.nn.sigmoid(proj(c, LANE)).T
    f = proj(c + LANE, LANE) + fb_ref[...]
    lf = jnp.minimum(f, 0.0) - jnp.log1p(jnp.exp(-jnp.abs(f)))
    logf_ref[...] = lf[:, 0:FOX_F_W]

    @pl.when((i * tm) % seq_len == 0)
    def _():
        carry_ref[...] = jnp.zeros_like(carry_ref)

    r_io = lax.broadcasted_iota(jnp.int32, (tm, tm), 0)
    c_io = lax.broadcasted_iota(jnp.int32, (tm, tm), 1)
    tri = jnp.where(c_io <= r_io, 1.0, 0.0).astype(bf16)
    csum = carry_ref[...] + sum(jnp.dot(tri, p, preferred_element_type=f32) for p in _split3(lf))
    carry_ref[...] = csum[tm - 1:tm, :]
    pieces = jnp.concatenate(_split3(-csum), axis=1)
    cfeat = jnp.dot(pieces, place_ref[...], preferred_element_type=f32)
    for j in range(FOX_HEADS // 2):
        for s, half in enumerate(_expand_halves(kbn[:, j * LANE:(j + 1) * LANE])):
            hd = 2 * j + s
            kb_ref[:, hd * LANE:(hd + 1) * LANE] = (half + cfeat[:, hd * LANE:(hd + 1) * LANE]).astype(bf16)


def _fox_feature_placement():
    r = lax.broadcasted_iota(jnp.int32, (3 * LANE, FOX_HEADS * LANE), 0)
    c = lax.broadcasted_iota(jnp.int32, (3 * LANE, FOX_HEADS * LANE), 1)
    s, hd = r // LANE, r % LANE
    return jnp.where((hd < FOX_HEADS) & (c == hd * LANE + FEAT + s), 1.0, 0.0).astype(jnp.bfloat16)


def _front_attn(x2d, seq_len, norm_attn, w_front, bd, fox_f_bias, nsa_q_norm, nsa_k_norm, fox_q_norm,
                fox_k_norm, tm=256):
    n = x2d.shape[0]
    f32, bf16 = jnp.float32, jnp.bfloat16
    two = lambda g: jnp.concatenate([g, g]).reshape(1, LANE)
    row = lambda w: pl.BlockSpec((tm, w), lambda i: (i, 0))
    col = lambda h: pl.BlockSpec((h, tm), lambda i: (0, i))
    full = lambda a: pl.BlockSpec(a.shape, lambda i: (0,) * a.ndim)
    fb = jnp.pad(fox_f_bias, (0, LANE - FOX_F_W)).reshape(1, LANE)
    args = (x2d, norm_attn.reshape(1, D_MODEL), w_front, bd, fb,
            two(nsa_q_norm), two(nsa_k_norm[1]), two(nsa_k_norm[2]), two(fox_q_norm), two(fox_k_norm),
            _fox_feature_placement())
    outs = [(row(512), (n, 512), f32), (row(256), (n, 256), f32), (row(2 * FOX_W), (n, 2 * FOX_W), f32),
            (row(FOX_F_W), (n, FOX_F_W), f32), (row(MERGE_W), (n, MERGE_W), f32),
            (col(NSA_HEADS * LANE), (NSA_HEADS * LANE, n), bf16), (col(LANE), (LANE, n), f32),
            (row(2 * LANE), (n, 2 * LANE), bf16), (row(2 * LANE), (n, 2 * LANE), bf16),
            (col(LANE), (LANE, n), bf16), (col(LANE), (LANE, n), bf16),
            (col(FOX_HEADS * LANE), (FOX_HEADS * LANE, n), bf16),
            (row(FOX_HEADS * LANE), (n, FOX_HEADS * LANE), bf16), (col(FOX_W), (FOX_W, n), bf16)]
    return pl.pallas_call(
        functools.partial(_front_attn_kernel, seq_len),
        grid=(n // tm,),
        in_specs=[row(D_MODEL)] + [full(a) for a in args[1:]],
        out_specs=[o[0] for o in outs],
        out_shape=[jax.ShapeDtypeStruct(o[1], o[2]) for o in outs],
        scratch_shapes=[pltpu.VMEM((1, LANE), f32)],
        compiler_params=pltpu.CompilerParams(dimension_semantics=("arbitrary",),
                                             vmem_limit_bytes=VMEM_LIMIT),
        name="front_attn",
    )(*args)


def _front_weights(w_in):
    pad = lambda w: jnp.pad(w, ((0, 0), (0, LANE - w.shape[1])))
    parts = [w_in[:, :SPLIT_KV_A], w_in[:, SPLIT_G_A:SPLIT_QKV_B], w_in[:, SPLIT_F_B:],
             pad(w_in[:, SPLIT_KV_A:SPLIT_G_A]), pad(w_in[:, SPLIT_QKV_B:SPLIT_F_B])]
    return jnp.concatenate(parts, axis=1).astype(jnp.bfloat16)


def _block_diag_mean():
    r = lax.broadcasted_iota(jnp.int32, (LANE, LANE), 0) // HEAD_DIM
    c = lax.broadcasted_iota(jnp.int32, (LANE, LANE), 1) // HEAD_DIM
    return jnp.where(r == c, 1.0 / HEAD_DIM, 0.0).astype(jnp.bfloat16)


def _softmax_step(tiles, carry):
    m, l, acc = carry
    m_new = functools.reduce(jnp.maximum, [jnp.max(s, axis=0, keepdims=True) for s, _ in tiles], m)
    alpha = jnp.exp(m - m_new)
    ps = [jnp.exp(s - m_new) for s, _ in tiles]
    l = alpha * l + sum(jnp.sum(p, axis=0, keepdims=True) for p in ps)
    acc = alpha * acc + sum(jnp.dot(vt, p.astype(jnp.bfloat16), preferred_element_type=jnp.float32)
                            for p, (_, vt) in zip(ps, tiles))
    return m_new, l, acc


TILE_UNROLL = 8


def _tile_loop(n, tile, carry):
    def body_u(i, c):
        return _softmax_step([tile(i * TILE_UNROLL + u) for u in range(TILE_UNROLL)], c)

    nu = n // TILE_UNROLL
    carry = lax.fori_loop(0, nu, body_u, carry)
    base = nu * TILE_UNROLL
    size = TILE_UNROLL // 2
    while size >= 1:
        has = (n & size) != 0
        carry = lax.cond(has, functools.partial(
            lambda b, sz, c: _softmax_step([tile(b + u) for u in range(sz)], c), base, size),
            lambda c: c, carry)
        base = base + jnp.where(has, size, 0)
        size //= 2
    return carry


def _softmax_init(w):
    return (jnp.full((1, w), NEG_INF, jnp.float32), jnp.zeros((1, w), jnp.float32),
            jnp.zeros((HEAD_DIM, w), jnp.float32))


def _summarize(x_ref, s, pe_ref, w1_ref, w2_ref, nb):
    f32, bf16 = jnp.float32, jnp.bfloat16

    def body(l, acc):
        xl = x_ref[pl.ds(l, nb, stride=CMP_BLOCK), :] + pe_ref[s, pl.ds(l, 1), :]
        return acc + jnp.dot(xl.astype(bf16), w1_ref[s, l], preferred_element_type=f32)

    hid = jax.nn.gelu(lax.fori_loop(0, CMP_BLOCK, body, jnp.zeros((nb, LANE), f32), unroll=8))
    return jnp.dot(hid.astype(bf16), w2_ref[s], preferred_element_type=f32)


def _compress_kernel(xk_ref, xv_ref, pe_ref, w1_ref, w2_ref, gk_ref, bd_ref, kc_ref, vct_ref):
    f32, bf16 = jnp.float32, jnp.bfloat16
    nb = kc_ref.shape[0]
    kc = _summarize(xk_ref, 0, pe_ref, w1_ref, w2_ref, nb)
    kc = kc * lax.rsqrt(_group_mean_sq(kc, bd_ref[...]) + EPS) * gk_ref[...]
    lane = _lane(nb)
    blk = lax.broadcasted_iota(jnp.int32, (nb, LANE), 0).astype(f32)
    feat = jnp.where(lane == FEAT, blk, jnp.where(lane == FEAT + 1, float(CMP_BLOCK - 1), 0.0))
    for g, half in enumerate(_expand_halves(kc)):
        kc_ref[:, g * LANE:(g + 1) * LANE] = (half + feat).astype(bf16)
    vct_ref[...] = _summarize(xv_ref, 1, pe_ref, w1_ref, w2_ref, nb).T.astype(bf16)


def _compress_weights(cmp_pe, cmp_w1, cmp_w2):
    def bdiag(w):
        z = jnp.zeros_like(w)
        return jnp.concatenate([jnp.concatenate([w, z], axis=-1), jnp.concatenate([z, w], axis=-1)], axis=-2)
    pe = jnp.concatenate([cmp_pe, cmp_pe], axis=-1)
    return pe, bdiag(cmp_w1).astype(jnp.bfloat16), bdiag(cmp_w2).astype(jnp.bfloat16)


def _compress(rows2d, nb, pe, w1, w2, gk, bd):
    nseq = rows2d.shape[0] // (nb * CMP_BLOCK)
    full = lambda a: pl.BlockSpec(a.shape, lambda b: (0,) * a.ndim)
    args = (rows2d, rows2d, pe, w1, w2, jnp.concatenate([gk, gk]).reshape(1, LANE), bd)
    return pl.pallas_call(
        _compress_kernel,
        grid=(nseq,),
        in_specs=[pl.BlockSpec((nb * CMP_BLOCK, LANE), lambda b: (b, 0)),
                  pl.BlockSpec((nb * CMP_BLOCK, LANE), lambda b: (b, 1))] + [full(a) for a in args[2:]],
        out_specs=[pl.BlockSpec((nb, 2 * LANE), lambda b: (b, 0)), pl.BlockSpec((LANE, nb), lambda b: (0, b))],
        out_shape=[jax.ShapeDtypeStruct((nseq * nb, 2 * LANE), jnp.bfloat16),
                   jax.ShapeDtypeStruct((LANE, nseq * nb), jnp.bfloat16)],
        compiler_params=pltpu.CompilerParams(dimension_semantics=("arbitrary",),
                                             vmem_limit_bytes=VMEM_LIMIT),
        name="nsa_compress",
    )(*args)


def _nsa_prompt_kernel(qt_ref, gat_ref, ksel_ref, vselt_ref, kwin_ref, vwint_ref, kc_ref, vct_ref,
                       o_ref, selb_ref):
    f32, bf16 = jnp.float32, jnp.bfloat16
    tq = Q_BLOCK
    w = NSA_GROUP * tq
    g = pl.program_id(1)
    qi = pl.program_id(2)
    nb = kc_ref.shape[0]
    qt = jnp.concatenate([qt_ref[r * LANE:(r + 1) * LANE, :] for r in range(NSA_GROUP)], axis=1)
    qloc = lax.broadcasted_iota(jnp.int32, (1, w), 1) % tq
    qpos = qi * tq + qloc

    sc = jnp.dot(kc_ref[...], qt, preferred_element_type=f32)
    blk = lax.broadcasted_iota(jnp.int32, (nb, w), 0)
    vis = blk * CMP_BLOCK + (CMP_BLOCK - 1) <= qpos
    sc = jnp.where(vis, sc, NEG_INF)
    pc = jnp.where(vis, jnp.exp(sc - jnp.max(sc, axis=0, keepdims=True)), 0.0)
    pc = pc / jnp.maximum(jnp.sum(pc, axis=0, keepdims=True), 1e-30)
    o_c = jnp.dot(vct_ref[...], pc.astype(bf16), preferred_element_type=f32)

    imp = sum(pc[:, r * tq:(r + 1) * tq] for r in range(NSA_GROUP))
    blk1 = blk[:, 0:tq]
    cur = qpos[:, 0:tq] // SEL_BLOCK
    imp = jnp.where((blk1 == cur) | (blk1 == 0), FORCE_SCORE, jnp.where(blk1 <= cur, imp, -1.0))
    for _ in range(min(SEL_TOPK, nb)):
        mx = jnp.max(imp, axis=0, keepdims=True)
        idx = jnp.min(jnp.where(imp == mx, blk1, BIG_ID), axis=0, keepdims=True)
        imp = jnp.where(blk1 == idx, -jnp.inf, imp)
    selb = jnp.where(imp == -jnp.inf, 0.0, NEG_INF)
    selb_ref[...] = jnp.concatenate([selb] * NSA_GROUP, axis=1)

    krow = lax.broadcasted_iota(jnp.int32, (tq, w), 0)
    causal = krow <= qloc

    def sel_bias(kt):
        per = tq // SEL_BLOCK
        return jnp.concatenate([jnp.broadcast_to(selb_ref[pl.ds(kt * per + j, 1), :], (SEL_BLOCK, w))
                                for j in range(per)], axis=0)

    def sel_tile(kt):
        k0 = pl.multiple_of(kt * tq, tq)
        s = jnp.dot(ksel_ref[pl.ds(k0, tq), :], qt, preferred_element_type=f32) + sel_bias(kt)
        return s, vselt_ref[:, pl.ds(k0, tq)]

    carry = _tile_loop(qi, sel_tile, _softmax_init(w))
    s, vt = sel_tile(qi)
    m_s, l_s, acc_s = _softmax_step([(jnp.where(causal, s, NEG_INF), vt)], carry)
    o_s = acc_s / l_s

    nwin = WINDOW // tq
    tiles = []
    for j in range(nwin + 1):
        kt = qi - nwin + j
        k0 = pl.multiple_of(jnp.maximum(kt, 0) * tq, tq)
        s = jnp.dot(kwin_ref[pl.ds(k0, tq), :], qt, preferred_element_type=f32)
        ok = kt >= 0
        if j == 0:
            ok = ok & (krow > qloc)
        elif j == nwin:
            ok = ok & causal
        tiles.append((jnp.where(ok, s, NEG_INF), vwint_ref[:, pl.ds(k0, tq)]))
    m_w, l_w, acc_w = _softmax_step(tiles, _softmax_init(w))
    o_w = acc_w / l_w

    for r in range(NSA_GROUP):
        gate = lambda br: gat_ref[pl.ds((g * NSA_GROUP + r) * 3 + br, 1), :]
        sl = slice(r * tq, (r + 1) * tq)
        o_ref[r * HEAD_DIM:(r + 1) * HEAD_DIM, :] = (
            gate(0) * o_c[:, sl] + gate(1) * o_s[:, sl] + gate(2) * o_w[:, sl])


def _nsa_prompt(qat, gat, ksel, vselt, kwin, vwint, kc, vct, nseq, seq_len):
    n = qat.shape[1]
    nq = seq_len // Q_BLOCK
    nb = seq_len // CMP_BLOCK
    gw = NSA_GROUP * LANE
    return pl.pallas_call(
        _nsa_prompt_kernel,
        grid=(nseq, NSA_KV_HEADS, nq),
        in_specs=[pl.BlockSpec((gw, Q_BLOCK), lambda b, g, q: (g, b * nq + q)),
                  pl.BlockSpec((LANE, Q_BLOCK), lambda b, g, q: (0, b * nq + q)),
                  pl.BlockSpec((seq_len, LANE), lambda b, g, q: (b, g)),
                  pl.BlockSpec((HEAD_DIM, seq_len), lambda b, g, q: (g, b)),
                  pl.BlockSpec((seq_len, LANE), lambda b, g, q: (b, g)),
                  pl.BlockSpec((HEAD_DIM, seq_len), lambda b, g, q: (g, b)),
                  pl.BlockSpec((nb, LANE), lambda b, g, q: (b, g)),
                  pl.BlockSpec((HEAD_DIM, nb), lambda b, g, q: (g, b))],
        out_specs=pl.BlockSpec((NSA_GROUP * HEAD_DIM, Q_BLOCK), lambda b, g, q: (g, b * nq + q)),
        out_shape=jax.ShapeDtypeStruct((NSA_Q_W, n), jnp.float32),
        scratch_shapes=[pltpu.VMEM((nb, NSA_GROUP * Q_BLOCK), jnp.float32)],
        compiler_params=pltpu.CompilerParams(dimension_semantics=("arbitrary", "arbitrary", "arbitrary"),
                                             vmem_limit_bytes=VMEM_LIMIT),
        name="nsa_prompt",
    )(qat, gat, ksel, vselt, kwin, vwint, kc, vct)


FOX_TQ = 512
FOX_TK = 128


def _fox_prompt_kernel(qt_ref, kb_ref, vbt_ref, o_ref):
    f32 = jnp.float32
    tq, tk = FOX_TQ, FOX_TK
    qi = pl.program_id(2)
    qt = qt_ref[...]

    def tile(kt):
        k0 = pl.multiple_of(kt * tk, tk)
        return jnp.dot(kb_ref[pl.ds(k0, tk), :], qt, preferred_element_type=f32), vbt_ref[:, pl.ds(k0, tk)]

    ndiag = tq // tk
    carry = _tile_loop(qi * ndiag, tile, _softmax_init(tq))
    krow = lax.broadcasted_iota(jnp.int32, (tk, tq), 0)
    qloc = lax.broadcasted_iota(jnp.int32, (tk, tq), 1)
    diag = []
    for j in range(ndiag):
        s, vt = tile(qi * ndiag + j)
        diag.append((jnp.where(krow + j * tk <= qloc, s, NEG_INF), vt))
    m, l, acc = _softmax_step(diag, carry)
    o_ref[...] = acc / l


def _fox_prompt(qbt, kb, vbt, nseq, seq_len):
    n = qbt.shape[1]
    nq = seq_len // FOX_TQ
    return pl.pallas_call(
        _fox_prompt_kernel,
        grid=(nseq, FOX_HEADS, nq),
        in_specs=[pl.BlockSpec((LANE, FOX_TQ), lambda b, h, q: (h, b * nq + q)),
                  pl.BlockSpec((seq_len, LANE), lambda b, h, q: (b, h)),
                  pl.BlockSpec((HEAD_DIM, seq_len), lambda b, h, q: (h, b))],
        out_specs=pl.BlockSpec((HEAD_DIM, FOX_TQ), lambda b, h, q: (h, b * nq + q)),
        out_shape=jax.ShapeDtypeStruct((FOX_W, n), jnp.float32),
        compiler_params=pltpu.CompilerParams(dimension_semantics=("arbitrary", "arbitrary", "arbitrary"),
                                             vmem_limit_bytes=VMEM_LIMIT),
        name="fox_prompt",
    )(qbt, kb, vbt)


def _slope_rows(shape, rows_per_head):
    hd = lax.broadcasted_iota(jnp.int32, shape, 0) // rows_per_head
    return pltpu.bitcast((126 - hd) << 23, jnp.float32)


def _pad_rows(x, rows):
    return jnp.concatenate([x, jnp.zeros((rows - x.shape[0], x.shape[1]), x.dtype)], axis=0)


def _nt_dot(a, b):
    return lax.dot_general(a, b, (((1,), (1,)), ((), ())), preferred_element_type=jnp.float32)


def _joint_softmax(parts):
    m = functools.reduce(jnp.maximum, [jnp.max(p, axis=1, keepdims=True) for p in parts])
    es = [jnp.exp(p - m) for p in parts]
    return es, sum(jnp.sum(e, axis=1, keepdims=True) for e in es)


def _nsa_sample_kernel(npages, ns, pt_ref, *refs):
    f32, bf16 = jnp.float32, jnp.bfloat16
    pages = refs[:npages]
    (e_ref, qa_ref, ga_ref, rnew_ref, wnew_ref, state_ref, pe_ref, w1_ref, w2_ref, gk_ref, bd_ref,
     o_ref, bufk_ref, bufv_ref) = refs[npages:]
    page = pages[0].shape[1]
    past = npages * page
    nb = past // CMP_BLOCK
    nrow = NSA_HEADS * ns

    for p in range(npages):
        bufk_ref[p * page:(p + 1) * page, :] = pages[p][0, :, 0:LANE]
        bufv_ref[p * page:(p + 1) * page, :] = pages[p][0, :, LANE:2 * LANE]
    kc = _summarize(bufk_ref, 0, pe_ref, w1_ref, w2_ref, nb)
    kc = (kc * lax.rsqrt(_group_mean_sq(kc, bd_ref[...]) + EPS) * gk_ref[...]).astype(bf16)
    vc = _summarize(bufv_ref, 1, pe_ref, w1_ref, w2_ref, nb).astype(bf16)

    lane8 = _lane(ns)
    ql = qa_ref[...]
    qrows = []
    for hd in range(NSA_HEADS):
        g = hd // NSA_GROUP
        t = ql[:, (hd // 2) * LANE:(hd // 2 + 1) * LANE]
        if hd % 2 != g:
            t = pltpu.roll(t, HEAD_DIM, axis=1)
        qrows.append(jnp.where(lane8 // HEAD_DIM == g, t, 0.0))
    qb = (jnp.concatenate(qrows, axis=0) * SCALE).astype(bf16)

    def geom(width):
        tok = lax.broadcasted_iota(jnp.int32, (nrow, width), 0) % ns
        col = lax.broadcasted_iota(jnp.int32, (nrow, width), 1)
        return tok, col, _slope_rows((nrow, width), ns)

    tok, col, slope = geom(nb)
    lc = _nt_dot(qb, kc) - slope * (past + tok - (col * CMP_BLOCK + CMP_BLOCK - 1)).astype(f32)
    (ec,), lsum = _joint_softmax([lc])
    pc = ec / lsum
    o_c = jnp.dot(pc.astype(bf16), vc, preferred_element_type=f32)

    imp = jnp.concatenate([sum(pc[(g * NSA_GROUP + r) * ns:(g * NSA_GROUP + r + 1) * ns] for r in range(NSA_GROUP))
                           for g in range(NSA_KV_HEADS)], axis=0)
    bcol = lax.broadcasted_iota(jnp.int32, imp.shape, 1)
    imp = jnp.where(bcol == 0, FORCE_SCORE, imp)
    rank = jnp.zeros(imp.shape, jnp.int32)
    for c in range(nb):
        other = jnp.broadcast_to(imp[:, c:c + 1], imp.shape)
        rank = rank + jnp.where((other > imp) | ((other == imp) & (bcol > c)), 1, 0)
    sel = jnp.where(rank < SEL_TOPK - 1, 1.0, 0.0)
    sel = jnp.concatenate([sel[g * ns:(g + 1) * ns] for g in range(NSA_KV_HEADS) for _ in range(NSA_GROUP)], axis=0)
    selexp = jnp.dot(sel.astype(bf16), e_ref[...], preferred_element_type=f32)

    def new_tile(k_new):
        tok, col, slope = geom(LANE)
        s = _nt_dot(qb, _pad_rows(k_new, LANE).astype(bf16))
        return jnp.where(col <= tok, s - slope * (tok - col).astype(f32), NEG_INF)

    def weighted(es, vs, lsum):
        acc = sum(jnp.dot(e.astype(bf16), v, preferred_element_type=f32) for e, v in zip(es, vs))
        return acc / lsum

    tok, col, slope = geom(past)
    ls = jnp.concatenate([_nt_dot(qb, pages[p][0, :, 2 * LANE:3 * LANE].astype(bf16)) for p in range(npages)],
                         axis=1)
    ls = jnp.where(selexp > 0.5, ls - slope * (past + tok - col).astype(f32), NEG_INF)
    (es, en), lsum = _joint_softmax([ls, new_tile(rnew_ref[:, 2 * LANE:3 * LANE])])
    o_s = weighted([es[:, p * page:(p + 1) * page] for p in range(npages)] + [en],
                   [pages[p][0, :, 3 * LANE:4 * LANE].astype(bf16) for p in range(npages)]
                   + [_pad_rows(rnew_ref[:, 3 * LANE:4 * LANE], LANE).astype(bf16)], lsum)

    wbuf = state_ref.shape[1]
    tok, col, slope = geom(wbuf)
    lw = _nt_dot(qb, state_ref[0, :, 0:LANE].astype(bf16))
    lw = jnp.where(col > tok + (wbuf - WINDOW), lw - slope * (wbuf + tok - col).astype(f32), NEG_INF)
    (ew, en), lsum = _joint_softmax([lw, new_tile(wnew_ref[:, 0:LANE])])
    o_w = weighted([ew, en], [state_ref[0, :, LANE:2 * LANE].astype(bf16),
                              _pad_rows(wnew_ref[:, LANE:2 * LANE], LANE).astype(bf16)], lsum)

    ga = ga_ref[...]
    gate = lambda br: jnp.concatenate(
        [jnp.broadcast_to(ga[:, hd * 3 + br:hd * 3 + br + 1], (ns, LANE)) for hd in range(NSA_HEADS)], axis=0)
    o = gate(0) * o_c + gate(1) * o_s + gate(2) * o_w
    for j in range(NSA_HEADS // 2):
        g = (2 * j) // NSA_GROUP
        a = o[2 * j * ns:(2 * j + 1) * ns]
        b = o[(2 * j + 1) * ns:(2 * j + 2) * ns]
        if g == 0:
            b = pltpu.roll(b, HEAD_DIM, axis=1)
        else:
            a = pltpu.roll(a, HEAD_DIM, axis=1)
        o_ref[:, j * LANE:(j + 1) * LANE] = jnp.where(lane8 < HEAD_DIM, a, b)


def _nsa_sample(page_table, cache2d, qa, ga, rows_new, win_new, state2d, pe, w1, w2, gk, bd):
    db, npages = page_table.shape
    page = cache2d.shape[1]
    ns = qa.shape[0] // db
    past = npages * page
    nb = past // CMP_BLOCK
    r = lax.broadcasted_iota(jnp.int32, (nb, past), 0)
    c = lax.broadcasted_iota(jnp.int32, (nb, past), 1)
    expand = jnp.where(c // SEL_BLOCK == r, 1.0, 0.0).astype(jnp.bfloat16)
    gk2 = jnp.concatenate([gk, gk]).reshape(1, LANE)
    full = lambda a: pl.BlockSpec(a.shape, lambda b, pt: (0,) * a.ndim)
    tok = lambda w: pl.BlockSpec((ns, w), lambda b, pt: (b, 0))
    page_specs = [pl.BlockSpec((1, page, cache2d.shape[2]), functools.partial(lambda p, b, pt: (pt[b, p], 0, 0), p))
                  for p in range(npages)]
    return pl.pallas_call(
        functools.partial(_nsa_sample_kernel, npages, ns),
        grid_spec=pltpu.PrefetchScalarGridSpec(
            num_scalar_prefetch=1, grid=(db,),
            in_specs=page_specs + [full(expand), tok(NSA_Q_W), tok(LANE), tok(512), tok(256),
                                   pl.BlockSpec((1,) + state2d.shape[1:], lambda b, pt: (b, 0, 0)),
                                   full(pe), full(w1), full(w2), full(gk2), full(bd)],
            out_specs=tok(NSA_Q_W),
            scratch_shapes=[pltpu.VMEM((past, LANE), jnp.float32), pltpu.VMEM((past, LANE), jnp.float32)]),
        out_shape=jax.ShapeDtypeStruct((db * ns, NSA_Q_W), jnp.float32),
        compiler_params=pltpu.CompilerParams(dimension_semantics=("arbitrary",),
                                             vmem_limit_bytes=VMEM_LIMIT),
        name="nsa_sample",
    )(page_table, *([cache2d] * npages), expand, qa, ga, rows_new, win_new, state2d, pe, w1, w2, gk2, bd)


def _fox_decode_kernel(npages, ns, pt_ref, *refs):
    f32, bf16 = jnp.float32, jnp.bfloat16
    kvt = refs[:npages]
    lft = refs[npages:2 * npages]
    qb_ref, knew_ref, lfnew_ref, o_ref = refs[2 * npages:]
    page = kvt[0].shape[-1]
    nh = FOX_HEADS
    nrow = nh * ns

    r_io = lax.broadcasted_iota(jnp.int32, (page, page), 0)
    c_io = lax.broadcasted_iota(jnp.int32, (page, page), 1)
    triu = jnp.where(r_io <= c_io, 1.0, 0.0).astype(bf16)
    carry = jnp.zeros((nh, 1), f32)
    negc = []
    for t in range(npages + 1):
        lf = lft[t][0] if t < npages else lfnew_ref[0]
        ct = carry + sum(jnp.dot(pc_, triu, preferred_element_type=f32) for pc_ in _split3(lf))
        carry = ct[:, page - 1:page]
        negc.append(jnp.concatenate([jnp.broadcast_to(-ct[hd:hd + 1], (ns, page)) for hd in range(nh)], axis=0))

    head_of_lane = lax.broadcasted_iota(jnp.int32, (ns, FOX_W), 1) // HEAD_DIM
    q = qb_ref[...] * SCALE
    q_bd = jnp.concatenate([jnp.where(head_of_lane == hd, q, 0.0) for hd in range(nh)], axis=0).astype(bf16)

    parts = [jnp.dot(q_bd, kvt[p][0, 0].reshape(FOX_W, page).astype(bf16), preferred_element_type=f32) + negc[p]
             for p in range(npages)]
    tok = lax.broadcasted_iota(jnp.int32, (nrow, page), 0) % ns
    col = lax.broadcasted_iota(jnp.int32, (nrow, page), 1)
    s_new = _nt_dot(q_bd, _pad_rows(knew_ref[:, 0:FOX_W], page).astype(bf16)) + negc[npages]
    parts.append(jnp.where(col <= tok, s_new, NEG_INF))
    es, lsum = _joint_softmax(parts)
    acc = sum(_nt_dot(es[p].astype(bf16), kvt[p][0, 1].reshape(FOX_W, page).astype(bf16)) for p in range(npages))
    acc = acc + jnp.dot(es[npages].astype(bf16), _pad_rows(knew_ref[:, FOX_W:2 * FOX_W], page).astype(bf16),
                        preferred_element_type=f32)
    acc = acc / lsum
    o_ref[...] = sum(jnp.where(head_of_lane == hd, acc[hd * ns:(hd + 1) * ns], 0.0) for hd in range(nh))


def _fox_decode(page_table, cache_kvt, lft, qb, fox_new, lft_new):
    db, npages = page_table.shape
    ns = qb.shape[0] // db
    tok = lambda w: pl.BlockSpec((ns, w), lambda b, pt: (b, 0))
    pg = lambda a: [pl.BlockSpec((1,) + a.shape[1:],
                                 functools.partial(lambda p, nd, b, pt: (pt[b, p],) + (0,) * nd, p, a.ndim - 1))
                    for p in range(npages)]
    return pl.pallas_call(
        functools.partial(_fox_decode_kernel, npages, ns),
        grid_spec=pltpu.PrefetchScalarGridSpec(
            num_scalar_prefetch=1, grid=(db,),
            in_specs=pg(cache_kvt) + pg(lft) + [tok(FOX_W), tok(2 * FOX_W),
                                                pl.BlockSpec((1,) + lft_new.shape[1:], lambda b, pt: (b, 0, 0))],
            out_specs=tok(FOX_W)),
        out_shape=jax.ShapeDtypeStruct((db * ns, FOX_W), jnp.float32),
        compiler_params=pltpu.CompilerParams(dimension_semantics=("arbitrary",),
                                             vmem_limit_bytes=VMEM_LIMIT),
        name="fox_decode",
    )(page_table, *([cache_kvt] * npages), *([lft] * npages), qb, fox_new, lft_new)


BIG_ID = 1 << 20


def _topk_rows(s, k, ids):
    ids = ids.astype(jnp.float32)
    vals, idxs = [], []
    for _ in range(k):
        m = jnp.max(s, axis=0, keepdims=True)
        idx = jnp.min(jnp.where(s == m, ids, float(BIG_ID)), axis=0, keepdims=True)
        vals.append(m)
        idxs.append(idx)
        s = jnp.where(ids == idx, -jnp.inf, s)
    return jnp.concatenate(vals, axis=0), jnp.concatenate(idxs, axis=0).astype(jnp.int32)


def _pick_rows(sel, table):
    out = jnp.zeros(sel.shape, table.dtype)
    for r in range(table.shape[0]):
        out = jnp.where(sel == r, table[r:r + 1, :], out)
    return out


def _merge_route_kernel(x_ref, on_ref, of_ref, mg_ref, wun_ref, wuf_ref, wo_ref, nf_ref, wqt_ref, sk_ref,
                        x1_ref, h2_ref, i1_ref, i2_ref, g_ref):
    tm = x_ref.shape[0]
    f32, bf16 = jnp.float32, jnp.bfloat16
    tdot = lambda ot, wgt: lax.dot_general(ot.astype(bf16), wgt, (((0,), (0,)), ((), ())),
                                           preferred_element_type=f32)
    a = tdot(on_ref[...], wun_ref[...])
    b = tdot(of_ref[...], wuf_ref[...])
    mixed = mg_ref[:, 0:D_MODEL] * a + mg_ref[:, D_MODEL:2 * D_MODEL] * b
    x1 = x_ref[...] + jnp.dot(mixed.astype(bf16), wo_ref[...], preferred_element_type=f32)
    x1_ref[...] = x1
    h2 = x1 * lax.rsqrt(jnp.mean(x1 * x1, axis=-1, keepdims=True) + EPS) * nf_ref[...]
    h2b = h2.astype(bf16)
    h2_ref[...] = h2b

    nk = PEER_N_KEYS
    key_ids = lax.broadcasted_iota(jnp.int32, (nk, tm), 0)
    io16 = lax.broadcasted_iota(jnp.int32, (PEER_TOPK, tm), 0)
    io8 = lax.broadcasted_iota(jnp.int32, (8, tm), 0)
    cand_ids = jnp.concatenate([io16] + [a_ * PEER_TOPK + io8 for a_ in range(1, 8)]
                               + [(io8 + 8) * PEER_TOPK], axis=0)
    for h in range(PEER_HEADS):
        sv, si = [], []
        for p in range(2):
            hp = 2 * h + p
            qt = lax.dot_general(wqt_ref[hp * PEER_DK_HALF:(hp + 1) * PEER_DK_HALF, :], h2b,
                                 (((1,), (1,)), ((), ())), preferred_element_type=f32)
            st = jnp.dot(sk_ref[hp], qt.astype(bf16), preferred_element_type=f32)
            v, i = _topk_rows(st, PEER_TOPK, key_ids)
            sv.append(v)
            si.append(i)
        s1, s2 = sv
        cand = jnp.concatenate([s1[0:1] + s2] + [s1[a_:a_ + 1] + s2[0:8] for a_ in range(1, 8)]
                               + [s1[8:16] + s2[0:1]], axis=0)
        top, fid = _topk_rows(cand, PEER_TOPK, cand_ids)
        e = jnp.exp(top - jnp.max(top, axis=0, keepdims=True))
        g = e / jnp.sum(e, axis=0, keepdims=True)
        sl = slice(h * PEER_TOPK, (h + 1) * PEER_TOPK)
        i1_ref[:, sl] = _pick_rows(fid >> 4, si[0]).T
        i2_ref[:, sl] = _pick_rows(fid & (PEER_TOPK - 1), si[1]).T
        g_ref[:, sl] = g.T


def _merge_route(x2d, o_nsa, o_fox, mg, w_up_nsa, w_up_fox, w_out, norm_ffn, wq_t, sub_keys, tm=256):
    n = x2d.shape[0]
    row = lambda w: pl.BlockSpec((tm, w), lambda i: (i, 0))
    col = lambda h: pl.BlockSpec((h, tm), lambda i: (0, i))
    full = lambda a: pl.BlockSpec(a.shape, lambda i: (0,) * a.ndim)
    args = (x2d, o_nsa, o_fox, mg, w_up_nsa, w_up_fox, w_out, norm_ffn.reshape(1, D_MODEL), wq_t, sub_keys)
    hk = PEER_HEADS * PEER_TOPK
    return pl.pallas_call(
        _merge_route_kernel,
        grid=(n // tm,),
        in_specs=[row(D_MODEL), col(NSA_Q_W), col(FOX_W), row(MERGE_W)] + [full(a) for a in args[4:]],
        out_specs=[row(D_MODEL), row(D_MODEL), row(hk), row(hk), row(hk)],
        out_shape=[jax.ShapeDtypeStruct((n, D_MODEL), jnp.float32),
                   jax.ShapeDtypeStruct((n, D_MODEL), jnp.bfloat16),
                   jax.ShapeDtypeStruct((n, hk), jnp.int32),
                   jax.ShapeDtypeStruct((n, hk), jnp.int32),
                   jax.ShapeDtypeStruct((n, hk), jnp.float32)],
        compiler_params=pltpu.CompilerParams(dimension_semantics=("arbitrary",),
                                             vmem_limit_bytes=VMEM_LIMIT),
        name="merge_route",
    )(*args)


def _peer_act_kernel(h2_ref, u_ref, i1_ref, i2_ref, act_ref):
    c = pl.program_id(1)
    ec = u_ref.shape[0]

    @pl.when(c == 0)
    def _():
        act_ref[...] = jnp.zeros_like(act_ref)

    a = lax.dot_general(h2_ref[...], u_ref[...], (((1,), (1,)), ((), ())),
                        preferred_element_type=jnp.float32)
    i1 = i1_ref[...]
    i2 = i2_ref[...]
    act = act_ref[...]
    for ii in range(ec // PEER_N_KEYS):
        got = jnp.take_along_axis(a[:, ii * PEER_N_KEYS:(ii + 1) * PEER_N_KEYS], i2, axis=1)
        act = jnp.where(i1 == c * (ec // PEER_N_KEYS) + ii, got, act)
    act_ref[...] = act


def _peer_act(h2b, u_b, i1, i2, tm=512, ec=2048):
    n = h2b.shape[0]
    hk = i1.shape[1]
    return pl.pallas_call(
        _peer_act_kernel,
        grid=(n // tm, u_b.shape[0] // ec),
        in_specs=[pl.BlockSpec((tm, D_MODEL), lambda t, c: (t, 0)),
                  pl.BlockSpec((ec, D_MODEL), lambda t, c: (c, 0)),
                  pl.BlockSpec((tm, hk), lambda t, c: (t, 0)),
                  pl.BlockSpec((tm, hk), lambda t, c: (t, 0))],
        out_specs=pl.BlockSpec((tm, hk), lambda t, c: (t, 0)),
        out_shape=jax.ShapeDtypeStruct((n, hk), jnp.float32),
        compiler_params=pltpu.CompilerParams(dimension_semantics=("arbitrary", "arbitrary"),
                                             vmem_limit_bytes=VMEM_LIMIT),
        name="peer_act",
    )(h2b, u_b, i1, i2)


def _peer_coef_kernel(act_ref, g_ref, i1_ref, i2_ref, c_ref, coef_ref):
    tm = act_ref.shape[0]
    nk = PEER_N_KEYS
    coef_ref[...] = g_ref[...] * jax.nn.gelu(act_ref[...])
    sub = lax.broadcasted_iota(jnp.int32, (nk, i1_ref.shape[1]), 0)

    def token(t):
        r1 = i1_ref[pl.ds(t, 1), :]
        r2 = i2_ref[pl.ds(t, 1), :]
        cf = coef_ref[pl.ds(t, 1), :]
        m1 = jnp.where(r1 == sub, cf, 0.0).astype(jnp.bfloat16)
        m2t = jnp.where(r2 == sub, 1.0, 0.0).astype(jnp.bfloat16)
        return lax.dot_general(m1, m2t, (((1,), (1,)), ((), ())), preferred_element_type=jnp.float32)

    def body(tg, carry):
        t0 = pl.multiple_of(tg * COEF_GROUP, COEF_GROUP)
        ct = jnp.stack([token(t0 + u) for u in range(COEF_GROUP)], axis=0)
        c_ref[:, pl.ds(t0, COEF_GROUP), :] = pltpu.einshape("tij->itj", ct).astype(c_ref.dtype)
        return carry

    lax.fori_loop(0, tm // COEF_GROUP, body, 0)


COEF_GROUP = 16


def _peer_coef(act, g, i1, i2, tm=128):
    n, hk = act.shape
    nk = PEER_N_KEYS
    row = pl.BlockSpec((tm, hk), lambda t: (t, 0))
    return pl.pallas_call(
        _peer_coef_kernel,
        grid=(n // tm,),
        in_specs=[row, row, row, row],
        out_specs=pl.BlockSpec((nk, tm, nk), lambda t: (0, t, 0)),
        out_shape=jax.ShapeDtypeStruct((nk, n, nk), jnp.bfloat16),
        scratch_shapes=[pltpu.VMEM((tm, hk), jnp.float32)],
        compiler_params=pltpu.CompilerParams(dimension_semantics=("arbitrary",),
                                             vmem_limit_bytes=VMEM_LIMIT),
        name="peer_coef",
    )(act, g, i1, i2)


def _peer_out_kernel(c_ref, v_ref, x1_ref, y_ref, acc_ref):
    k = pl.program_id(1)

    @pl.when(k == 0)
    def _():
        acc_ref[...] = x1_ref[...]

    acc = acc_ref[...]
    nk = PEER_N_KEYS
    for p in range(c_ref.shape[0] // 2):
        lhs = jnp.concatenate([c_ref[2 * p], c_ref[2 * p + 1]], axis=1)
        acc = acc + jnp.dot(lhs, v_ref[2 * p * nk:(2 * p + 2) * nk, :], preferred_element_type=jnp.float32)
    acc_ref[...] = acc

    @pl.when(k == pl.num_programs(1) - 1)
    def _():
        y_ref[...] = acc_ref[...]


def _peer_out(c3, v_b, x1, tm=1024, tk=2048):
    nk, n, _ = c3.shape
    ne = nk * nk
    return pl.pallas_call(
        _peer_out_kernel,
        grid=(n // tm, ne // tk),
        in_specs=[pl.BlockSpec((tk // nk, tm, nk), lambda t, k: (k, t, 0)),
                  pl.BlockSpec((tk, D_MODEL), lambda t, k: (k, 0)),
                  pl.BlockSpec((tm, D_MODEL), lambda t, k: (t, 0))],
        out_specs=pl.BlockSpec((tm, D_MODEL), lambda t, k: (t, 0)),
        out_shape=jax.ShapeDtypeStruct((n, D_MODEL), jnp.float32),
        scratch_shapes=[pltpu.VMEM((tm, D_MODEL), jnp.float32)],
        compiler_params=pltpu.CompilerParams(dimension_semantics=("arbitrary", "arbitrary"),
                                             vmem_limit_bytes=VMEM_LIMIT),
        name="peer_out",
    )(c3, v_b, x1)


def _peer_weights(w_up_nsa, w_up_fox, w_out, norm_ffn, peer_w_query, peer_sub_keys, peer_u, peer_v):
    bf16 = jnp.bfloat16
    return dict(w_up_nsa=w_up_nsa.astype(bf16), w_up_fox=w_up_fox.astype(bf16), w_out=w_out.astype(bf16),
                norm_ffn=norm_ffn, wq_t=peer_w_query.T.astype(bf16),
                sub_keys=peer_sub_keys.reshape(2 * PEER_HEADS, PEER_N_KEYS, PEER_DK_HALF).astype(bf16),
                u=peer_u.astype(bf16), v=peer_v.astype(bf16))


def _merge_peer(x2d, o_nsa, o_fox, mg, wts):
    x1, h2b, i1, i2, g = _merge_route(x2d, o_nsa, o_fox, mg, wts['w_up_nsa'], wts['w_up_fox'], wts['w_out'],
                                      wts['norm_ffn'], wts['wq_t'], wts['sub_keys'])
    act = _peer_act(h2b, wts['u'], i1, i2)
    c3 = _peer_coef(act, g, i1, i2)
    return _peer_out(c3, wts['v'], x1)


def kernel(x_prompt, x_sample, cache_nsa, cache_fox_kv, cache_fox_logf, state_nsa_win, page_table,
           norm_attn, w_in, fox_f_bias, nsa_q_norm, nsa_k_norm, fox_q_norm, fox_k_norm,
           cmp_pe, cmp_w1, cmp_w2, w_up_nsa, w_up_fox, w_out, norm_ffn,
           peer_w_query, peer_sub_keys, peer_u, peer_v):
    w_front = _front_weights(w_in)
    bd = _block_diag_mean()
    wts = _peer_weights(w_up_nsa, w_up_fox, w_out, norm_ffn, peer_w_query, peer_sub_keys, peer_u, peer_v)
    cmp_wts = _compress_weights(cmp_pe, cmp_w1, cmp_w2)

    bp, seq, _ = x_prompt.shape
    n_p = bp * seq
    (rows_p, win_p2d, fox_p2d, logf_p2d, mg_p, qat, gat, ksel, kwin, vselt, vwint, qbt, kb, vbt) = _front_attn(
        x_prompt.reshape(n_p, D_MODEL), seq, norm_attn, w_front, bd, fox_f_bias,
        nsa_q_norm, nsa_k_norm, fox_q_norm, fox_k_norm)
    kc, vct = _compress(rows_p, seq // CMP_BLOCK, *cmp_wts, nsa_k_norm[0], bd)
    o_nsa_t = _nsa_prompt(qat, gat, ksel, vselt, kwin, vwint, kc, vct, bp, seq)
    o_fox_t = _fox_prompt(qbt, kb, vbt, bp, seq)
    y_p = _merge_peer(x_prompt.reshape(n_p, D_MODEL), o_nsa_t, o_fox_t, mg_p, wts).reshape(x_prompt.shape)
    nsa_p = rows_p.reshape(bp, seq, 4, NSA_KV_HEADS, HEAD_DIM)
    fox_p = fox_p2d.reshape(bp, seq, 2, FOX_HEADS, HEAD_DIM)
    logf_p = logf_p2d.reshape(bp, seq, FOX_HEADS)
    win_p = win_p2d.reshape(bp, seq, 2, NSA_KV_HEADS, HEAD_DIM)[:, seq - min(WINDOW, seq):]

    db, ns, _ = x_sample.shape
    n_s = db * ns
    n_pool, page = cache_nsa.shape[:2]
    wbuf = state_nsa_win.shape[1]
    qa_s, rows_s, win_s2d, ga_s, qb_s, fox_s2d, logf_s2d, mg_s = _front(
        x_sample.reshape(n_s, D_MODEL), norm_attn, w_front, bd, fox_f_bias,
        nsa_q_norm, nsa_k_norm, fox_q_norm, fox_k_norm)
    o_nsa_s = _nsa_sample(page_table, cache_nsa.reshape(n_pool, page, 4 * LANE), qa_s, ga_s, rows_s, win_s2d,
                          state_nsa_win.reshape(db, wbuf, 2 * LANE), *cmp_wts, nsa_k_norm[0], bd)
    lft_new = jnp.pad(logf_s2d.reshape(db, ns, FOX_HEADS).transpose(0, 2, 1), ((0, 0), (0, 0), (0, page - ns)))
    o_fox_s = _fox_decode(page_table, cache_fox_kv.transpose(0, 2, 3, 4, 1), cache_fox_logf.transpose(0, 2, 1),
                          qb_s, fox_s2d, lft_new)
    y_s = _merge_peer(x_sample.reshape(n_s, D_MODEL), o_nsa_s.T, o_fox_s.T, mg_s, wts).reshape(x_sample.shape)
    nsa_s = rows_s.reshape(db, ns, 4, NSA_KV_HEADS, HEAD_DIM)
    fox_s = fox_s2d.reshape(db, ns, 2, FOX_HEADS, HEAD_DIM)
    logf_s = logf_s2d.reshape(db, ns, FOX_HEADS)
    win_s = jnp.concatenate([state_nsa_win[:, ns:], win_s2d.reshape(db, ns, 2, NSA_KV_HEADS, HEAD_DIM)], axis=1)
    return (y_p, y_s, nsa_p, fox_p, logf_p, win_p, nsa_s, fox_s, logf_s, win_s)
```

```python
import functools

import jax
import jax.numpy as jnp
from jax import lax
from jax.experimental import pallas as pl
from jax.experimental.pallas import tpu as pltpu

D_MODEL = 1024
HEAD_DIM = 64
NSA_HEADS = 8
NSA_KV_HEADS = 2
NSA_GROUP = NSA_HEADS // NSA_KV_HEADS
CMP_BLOCK = 64
SEL_BLOCK = CMP_BLOCK
SEL_TOPK = 16
WINDOW = 512
FOX_HEADS = 8
Q_BLOCK = 128
PEER_HEADS = 8
PEER_N_KEYS = 128
PEER_DK = 256
PEER_DK_HALF = PEER_DK // 2
PEER_TOPK = 16
PEER_CHUNK = 256

NSA_Q_W = NSA_HEADS * HEAD_DIM
NSA_KV_W = 6 * NSA_KV_HEADS * HEAD_DIM
NSA_GATE_W = 3 * NSA_HEADS
FOX_W = FOX_HEADS * HEAD_DIM
FOX_QKV_W = 3 * FOX_W
FOX_F_W = FOX_HEADS
MERGE_W = 2 * D_MODEL
SPLIT_Q_A = NSA_Q_W
SPLIT_KV_A = SPLIT_Q_A + NSA_KV_W
SPLIT_G_A = SPLIT_KV_A + NSA_GATE_W
SPLIT_QKV_B = SPLIT_G_A + FOX_QKV_W
SPLIT_F_B = SPLIT_QKV_B + FOX_F_W
IN_WIDTH = SPLIT_F_B + MERGE_W

SCALE = HEAD_DIM ** -0.5
FORCE_SCORE = float(NSA_GROUP + 1)
NEG_INF = -1e30
EPS = 1e-6

LANE = 128
VMEM_LIMIT = 48 * 1024 * 1024


def _group_mean_sq(x, bd):
    sq = x * x
    hi = sq.astype(jnp.bfloat16)
    lo = (sq - hi.astype(jnp.float32)).astype(jnp.bfloat16)
    return (jnp.dot(hi, bd, preferred_element_type=jnp.float32)
            + jnp.dot(lo, bd, preferred_element_type=jnp.float32))


def _head_rms(x, g, bd):
    outs = []
    for c in range(x.shape[1] // LANE):
        xc = x[:, c * LANE:(c + 1) * LANE]
        outs.append(xc * lax.rsqrt(_group_mean_sq(xc, bd) + EPS) * g)
    return outs[0] if len(outs) == 1 else jnp.concatenate(outs, axis=1)


def _front_kernel(x_ref, na_ref, w_ref, bd_ref, fb_ref, gq_a_ref, gk_sel_ref, gk_win_ref,
                  gq_b_ref, gk_b_ref,
                  qa_ref, rows_ref, win_ref, ga_ref, qb_ref, fox_ref, logf_ref, mg_ref):
    x = x_ref[...]
    h = x * lax.rsqrt(jnp.mean(x * x, axis=-1, keepdims=True) + EPS) * na_ref[...]
    hb = h.astype(jnp.bfloat16)
    bd = bd_ref[...]

    def proj(c0, width):
        return jnp.dot(hb, w_ref[:, c0:c0 + width], preferred_element_type=jnp.float32)

    c = 0
    qa_ref[...] = _head_rms(proj(c, NSA_Q_W), gq_a_ref[...], bd)
    c += NSA_Q_W
    rows_ref[:, 0:256] = proj(c, 256)
    rows_ref[:, 256:384] = _head_rms(proj(c + 256, 128), gk_sel_ref[...], bd)
    rows_ref[:, 384:512] = proj(c + 384, 128)
    win_ref[:, 0:128] = _head_rms(proj(c + 512, 128), gk_win_ref[...], bd)
    win_ref[:, 128:256] = proj(c + 640, 128)
    c += NSA_KV_W
    qb_ref[...] = _head_rms(proj(c, FOX_W), gq_b_ref[...], bd)
    fox_ref[:, 0:FOX_W] = _head_rms(proj(c + FOX_W, FOX_W), gk_b_ref[...], bd)
    fox_ref[:, FOX_W:2 * FOX_W] = proj(c + 2 * FOX_W, FOX_W)
    c += FOX_QKV_W
    for j in range(MERGE_W // 512):
        mg_ref[:, j * 512:(j + 1) * 512] = jax.nn.sigmoid(proj(c + j * 512, 512))
    c += MERGE_W
    ga_ref[...] = jax.nn.sigmoid(proj(c, LANE))
    f = proj(c + LANE, LANE)[:, 0:FOX_F_W] + fb_ref[...]
    logf_ref[...] = jnp.minimum(f, 0.0) - jnp.log1p(jnp.exp(-jnp.abs(f)))


def _front(x2d, norm_attn, w_front, bd, fox_f_bias, nsa_q_norm, nsa_k_norm, fox_q_norm, fox_k_norm,
           tm=256):
    n = x2d.shape[0]
    wf = w_front.shape[1]
    two = lambda g: jnp.concatenate([g, g]).reshape(1, LANE)
    row = lambda w: pl.BlockSpec((tm, w), lambda i: (i, 0))
    full = lambda a: pl.BlockSpec(a.shape, lambda i: (0,) * a.ndim)
    args = (x2d, norm_attn.reshape(1, D_MODEL), w_front, bd, fox_f_bias.reshape(1, FOX_F_W),
            two(nsa_q_norm), two(nsa_k_norm[1]), two(nsa_k_norm[2]), two(fox_q_norm), two(fox_k_norm))
    widths = (NSA_Q_W, 512, 256, LANE, FOX_W, 2 * FOX_W, FOX_F_W, MERGE_W)
    return pl.pallas_call(
        _front_kernel,
        grid=(n // tm,),
        in_specs=[row(D_MODEL)] + [full(a) for a in args[1:]],
        out_specs=[row(w) for w in widths],
        out_shape=[jax.ShapeDtypeStruct((n, w), jnp.float32) for w in widths],
        compiler_params=pltpu.CompilerParams(dimension_semantics=("arbitrary",),
                                             vmem_limit_bytes=VMEM_LIMIT),
        name="front",
    )(*args)


FEAT = HEAD_DIM


def _lane(tm):
    return lax.broadcasted_iota(jnp.int32, (tm, LANE), 1)


def _expand_halves(x):
    lo = _lane(x.shape[0]) < HEAD_DIM
    return jnp.where(lo, x, 0.0), jnp.where(lo, pltpu.roll(x, HEAD_DIM, axis=1), 0.0)


def _split3(x):
    hi = x.astype(jnp.bfloat16)
    r = x - hi.astype(jnp.float32)
    mid = r.astype(jnp.bfloat16)
    lo = (r - mid.astype(jnp.float32)).astype(jnp.bfloat16)
    return hi, mid, lo


def _front_attn_kernel(seq_len, x_ref, na_ref, w_ref, bd_ref, fb_ref, gq_a_ref, gk_sel_ref, gk_win_ref,
                       gq_b_ref, gk_b_ref, place_ref,
                       rows_ref, win_ref, fox_ref, logf_ref, mg_ref,
                       qat_ref, gat_ref, ksel_ref, kwin_ref, vselt_ref, vwint_ref, qbt_ref, kb_ref, vbt_ref,
                       carry_ref):
    f32, bf16 = jnp.float32, jnp.bfloat16
    tm = x_ref.shape[0]
    i = pl.program_id(0)
    x = x_ref[...]
    h = x * lax.rsqrt(jnp.mean(x * x, axis=-1, keepdims=True) + EPS) * na_ref[...]
    hb = h.astype(bf16)
    bd = bd_ref[...]
    lane = _lane(tm)
    pos = (i * tm + lax.broadcasted_iota(jnp.int32, (tm, LANE), 0)) % seq_len
    kfeat = jnp.where(lane == FEAT, (pos // SEL_BLOCK).astype(f32),
                      jnp.where(lane == FEAT + 1, (pos % SEL_BLOCK).astype(f32), 0.0))

    def proj(c0, width):
        return jnp.dot(hb, w_ref[:, c0:c0 + width], preferred_element_type=f32)

    c = 0
    qa = _head_rms(proj(c, NSA_Q_W), gq_a_ref[...], bd)
    for j in range(NSA_HEADS // 2):
        for s, half in enumerate(_expand_halves(qa[:, j * LANE:(j + 1) * LANE])):
            hd = 2 * j + s
            slope = 2.0 ** -(hd + 1)
            qfeat = jnp.where(lane == FEAT, slope * SEL_BLOCK, jnp.where(lane == FEAT + 1, slope, 0.0))
            qat_ref[hd * LANE:(hd + 1) * LANE, :] = (half * SCALE + qfeat).T.astype(bf16)
    c += NSA_Q_W
    rows_ref[:, 0:256] = proj(c, 256)
    ksel = _head_rms(proj(c + 256, 128), gk_sel_ref[...], bd)
    rows_ref[:, 256:384] = ksel
    vsel = proj(c + 384, 128)
    rows_ref[:, 384:512] = vsel
    kwin = _head_rms(proj(c + 512, 128), gk_win_ref[...], bd)
    win_ref[:, 0:128] = kwin
    vwin = proj(c + 640, 128)
    win_ref[:, 128:256] = vwin
    for g, (ks_g, kw_g) in enumerate(zip(_expand_halves(ksel), _expand_halves(kwin))):
        ksel_ref[:, g * LANE:(g + 1) * LANE] = (ks_g + kfeat).astype(bf16)
        kwin_ref[:, g * LANE:(g + 1) * LANE] = (kw_g + kfeat).astype(bf16)
    vselt_ref[...] = vsel.T.astype(bf16)
    vwint_ref[...] = vwin.T.astype(bf16)
    c += NSA_KV_W

    qb = _head_rms(proj(c, FOX_W), gq_b_ref[...], bd)
    ones3 = jnp.where((lane >= FEAT) & (lane < FEAT + 3), 1.0, 0.0)
    for j in range(FOX_HEADS // 2):
        for s, half in enumerate(_expand_halves(qb[:, j * LANE:(j + 1) * LANE])):
            hd = 2 * j + s
            qbt_ref[hd * LANE:(hd + 1) * LANE, :] = (half * SCALE + ones3).T.astype(bf16)
    kbn = _head_rms(proj(c + FOX_W, FOX_W), gk_b_ref[...], bd)
    fox_ref[:, 0:FOX_W] = kbn
    for j in range(FOX_HEADS // 2):
        vb = proj(c + 2 * FOX_W + j * LANE, LANE)
        fox_ref[:, FOX_W + j * LANE:FOX_W + (j + 1) * LANE] = vb
        vbt_ref[j * LANE:(j + 1) * LANE, :] = vb.T.astype(bf16)
    c += FOX_QKV_W
    for j in range(MERGE_W // 512):
        mg_ref[:, j * 512:(j + 1) * 512] = jax.nn.sigmoid(proj(c + j * 512, 512))
    c += MERGE_W
    gat_ref[...] = jax.nn.sigmoid(proj(c, LANE)).T
    f = proj(c + LANE, LANE) + fb_ref[...]
    lf = jnp.minimum(f, 0.0) - jnp.log1p(jnp.exp(-jnp.abs(f)))
    logf_ref[...] = lf[:, 0:FOX_F_W]

    @pl.when((i * tm) % seq_len == 0)
    def _():
        carry_ref[...] = jnp.zeros_like(carry_ref)

    r_io = lax.broadcasted_iota(jnp.int32, (tm, tm), 0)
    c_io = lax.broadcasted_iota(jnp.int32, (tm, tm), 1)
    tri = jnp.where(c_io <= r_io, 1.0, 0.0).astype(bf16)
    csum = carry_ref[...] + sum(jnp.dot(tri, p, preferred_element_type=f32) for p in _split3(lf))
    carry_ref[...] = csum[tm - 1:tm, :]
    pieces = jnp.concatenate(_split3(-csum), axis=1)
    cfeat = jnp.dot(pieces, place_ref[...], preferred_element_type=f32)
    for j in range(FOX_HEADS // 2):
        for s, half in enumerate(_expand_halves(kbn[:, j * LANE:(j + 1) * LANE])):
            hd = 2 * j + s
            kb_ref[:, hd * LANE:(hd + 1) * LANE] = (half + cfeat[:, hd * LANE:(hd + 1) * LANE]).astype(bf16)


def _fox_feature_placement():
    r = lax.broadcasted_iota(jnp.int32, (3 * LANE, FOX_HEADS * LANE), 0)
    c = lax.broadcasted_iota(jnp.int32, (3 * LANE, FOX_HEADS * LANE), 1)
    s, hd = r // LANE, r % LANE
    return jnp.where((hd < FOX_HEADS) & (c == hd * LANE + FEAT + s), 1.0, 0.0).astype(jnp.bfloat16)


def _front_attn(x2d, seq_len, norm_attn, w_front, bd, fox_f_bias, nsa_q_norm, nsa_k_norm, fox_q_norm,
                fox_k_norm, tm=256):
    n = x2d.shape[0]
    f32, bf16 = jnp.float32, jnp.bfloat16
    two = lambda g: jnp.concatenate([g, g]).reshape(1, LANE)
    row = lambda w: pl.BlockSpec((tm, w), lambda i: (i, 0))
    col = lambda h: pl.BlockSpec((h, tm), lambda i: (0, i))
    full = lambda a: pl.BlockSpec(a.shape, lambda i: (0,) * a.ndim)
    fb = jnp.pad(fox_f_bias, (0, LANE - FOX_F_W)).reshape(1, LANE)
    args = (x2d, norm_attn.reshape(1, D_MODEL), w_front, bd, fb,
            two(nsa_q_norm), two(nsa_k_norm[1]), two(nsa_k_norm[2]), two(fox_q_norm), two(fox_k_norm),
            _fox_feature_placement())
    outs = [(row(512), (n, 512), f32), (row(256), (n, 256), f32), (row(2 * FOX_W), (n, 2 * FOX_W), f32),
            (row(FOX_F_W), (n, FOX_F_W), f32), (row(MERGE_W), (n, MERGE_W), f32),
            (col(NSA_HEADS * LANE), (NSA_HEADS * LANE, n), bf16), (col(LANE), (LANE, n), f32),
            (row(2 * LANE), (n, 2 * LANE), bf16), (row(2 * LANE), (n, 2 * LANE), bf16),
            (col(LANE), (LANE, n), bf16), (col(LANE), (LANE, n), bf16),
            (col(FOX_HEADS * LANE), (FOX_HEADS * LANE, n), bf16),
            (row(FOX_HEADS * LANE), (n, FOX_HEADS * LANE), bf16), (col(FOX_W), (FOX_W, n), bf16)]
    return pl.pallas_call(
        functools.partial(_front_attn_kernel, seq_len),
        grid=(n // tm,),
        in_specs=[row(D_MODEL)] + [full(a) for a in args[1:]],
        out_specs=[o[0] for o in outs],
        out_shape=[jax.ShapeDtypeStruct(o[1], o[2]) for o in outs],
        scratch_shapes=[pltpu.VMEM((1, LANE), f32)],
        compiler_params=pltpu.CompilerParams(dimension_semantics=("arbitrary",),
                                             vmem_limit_bytes=VMEM_LIMIT),
        name="front_attn",
    )(*args)


def _front_weights(w_in):
    pad = lambda w: jnp.pad(w, ((0, 0), (0, LANE - w.shape[1])))
    parts = [w_in[:, :SPLIT_KV_A], w_in[:, SPLIT_G_A:SPLIT_QKV_B], w_in[:, SPLIT_F_B:],
             pad(w_in[:, SPLIT_KV_A:SPLIT_G_A]), pad(w_in[:, SPLIT_QKV_B:SPLIT_F_B])]
    return jnp.concatenate(parts, axis=1).astype(jnp.bfloat16)


def _block_diag_mean():
    r = lax.broadcasted_iota(jnp.int32, (LANE, LANE), 0) // HEAD_DIM
    c = lax.broadcasted_iota(jnp.int32, (LANE, LANE), 1) // HEAD_DIM
    return jnp.where(r == c, 1.0 / HEAD_DIM, 0.0).astype(jnp.bfloat16)


def _softmax_step(tiles, carry):
    m, l, acc = carry
    m_new = functools.reduce(jnp.maximum, [jnp.max(s, axis=0, keepdims=True) for s, _ in tiles], m)
    alpha = jnp.exp(m - m_new)
    ps = [jnp.exp(s - m_new) for s, _ in tiles]
    l = alpha * l + sum(jnp.sum(p, axis=0, keepdims=True) for p in ps)
    acc = alpha * acc + sum(jnp.dot(vt, p.astype(jnp.bfloat16), preferred_element_type=jnp.float32)
                            for p, (_, vt) in zip(ps, tiles))
    return m_new, l, acc


TILE_UNROLL = 8


def _tile_loop(n, tile, carry):
    def body_u(i, c):
        return _softmax_step([tile(i * TILE_UNROLL + u) for u in range(TILE_UNROLL)], c)

    nu = n // TILE_UNROLL
    carry = lax.fori_loop(0, nu, body_u, carry)
    base = nu * TILE_UNROLL
    size = TILE_UNROLL // 2
    while size >= 1:
        has = (n & size) != 0
        carry = lax.cond(has, functools.partial(
            lambda b, sz, c: _softmax_step([tile(b + u) for u in range(sz)], c), base, size),
            lambda c: c, carry)
        base = base + jnp.where(has, size, 0)
        size //= 2
    return carry


def _softmax_init(w):
    return (jnp.full((1, w), NEG_INF, jnp.float32), jnp.zeros((1, w), jnp.float32),
            jnp.zeros((HEAD_DIM, w), jnp.float32))


def _summarize(x_ref, s, pe_ref, w1_ref, w2_ref, nb):
    f32, bf16 = jnp.float32, jnp.bfloat16

    def body(l, acc):
        xl = x_ref[pl.ds(l, nb, stride=CMP_BLOCK), :] + pe_ref[s, pl.ds(l, 1), :]
        return acc + jnp.dot(xl.astype(bf16), w1_ref[s, l], preferred_element_type=f32)

    hid = jax.nn.gelu(lax.fori_loop(0, CMP_BLOCK, body, jnp.zeros((nb, LANE), f32), unroll=8))
    return jnp.dot(hid.astype(bf16), w2_ref[s], preferred_element_type=f32)


def _compress_kernel(xk_ref, xv_ref, pe_ref, w1_ref, w2_ref, gk_ref, bd_ref, kc_ref, vct_ref):
    f32, bf16 = jnp.float32, jnp.bfloat16
    nb = kc_ref.shape[0]
    kc = _summarize(xk_ref, 0, pe_ref, w1_ref, w2_ref, nb)
    kc = kc * lax.rsqrt(_group_mean_sq(kc, bd_ref[...]) + EPS) * gk_ref[...]
    lane = _lane(nb)
    blk = lax.broadcasted_iota(jnp.int32, (nb, LANE), 0).astype(f32)
    feat = jnp.where(lane == FEAT, blk, jnp.where(lane == FEAT + 1, float(CMP_BLOCK - 1), 0.0))
    for g, half in enumerate(_expand_halves(kc)):
        kc_ref[:, g * LANE:(g + 1) * LANE] = (half + feat).astype(bf16)
    vct_ref[...] = _summarize(xv_ref, 1, pe_ref, w1_ref, w2_ref, nb).T.astype(bf16)


def _compress_weights(cmp_pe, cmp_w1, cmp_w2):
    def bdiag(w):
        z = jnp.zeros_like(w)
        return jnp.concatenate([jnp.concatenate([w, z], axis=-1), jnp.concatenate([z, w], axis=-1)], axis=-2)
    pe = jnp.concatenate([cmp_pe, cmp_pe], axis=-1)
    return pe, bdiag(cmp_w1).astype(jnp.bfloat16), bdiag(cmp_w2).astype(jnp.bfloat16)


def _compress(rows2d, nb, pe, w1, w2, gk, bd):
    nseq = rows2d.shape[0] // (nb * CMP_BLOCK)
    full = lambda a: pl.BlockSpec(a.shape, lambda b: (0,) * a.ndim)
    args = (rows2d, rows2d, pe, w1, w2, jnp.concatenate([gk, gk]).reshape(1, LANE), bd)
    return pl.pallas_call(
        _compress_kernel,
        grid=(nseq,),
        in_specs=[pl.BlockSpec((nb * CMP_BLOCK, LANE), lambda b: (b, 0)),
                  pl.BlockSpec((nb * CMP_BLOCK, LANE), lambda b: (b, 1))] + [full(a) for a in args[2:]],
        out_specs=[pl.BlockSpec((nb, 2 * LANE), lambda b: (b, 0)), pl.BlockSpec((LANE, nb), lambda b: (0, b))],
        out_shape=[jax.ShapeDtypeStruct((nseq * nb, 2 * LANE), jnp.bfloat16),
                   jax.ShapeDtypeStruct((LANE, nseq * nb), jnp.bfloat16)],
        compiler_params=pltpu.CompilerParams(dimension_semantics=("arbitrary",),
                                             vmem_limit_bytes=VMEM_LIMIT),
        name="nsa_compress",
    )(*args)


def _nsa_prompt_kernel(qt_ref, gat_ref, ksel_ref, vselt_ref, kwin_ref, vwint_ref, kc_ref, vct_ref,
                       o_ref, selb_ref):
    f32, bf16 = jnp.float32, jnp.bfloat16
    tq = Q_BLOCK
    w = NSA_GROUP * tq
    g = pl.program_id(1)
    qi = pl.program_id(2)
    nb = kc_ref.shape[0]
    qt = jnp.concatenate([qt_ref[r * LANE:(r + 1) * LANE, :] for r in range(NSA_GROUP)], axis=1)
    qloc = lax.broadcasted_iota(jnp.int32, (1, w), 1) % tq
    qpos = qi * tq + qloc

    sc = jnp.dot(kc_ref[...], qt, preferred_element_type=f32)
    blk = lax.broadcasted_iota(jnp.int32, (nb, w), 0)
    vis = blk * CMP_BLOCK + (CMP_BLOCK - 1) <= qpos
    sc = jnp.where(vis, sc, NEG_INF)
    pc = jnp.where(vis, jnp.exp(sc - jnp.max(sc, axis=0, keepdims=True)), 0.0)
    pc = pc / jnp.maximum(jnp.sum(pc, axis=0, keepdims=True), 1e-30)
    o_c = jnp.dot(vct_ref[...], pc.astype(bf16), preferred_element_type=f32)

    imp = sum(pc[:, r * tq:(r + 1) * tq] for r in range(NSA_GROUP))
    blk1 = blk[:, 0:tq]
    cur = qpos[:, 0:tq] // SEL_BLOCK
    imp = jnp.where((blk1 == cur) | (blk1 == 0), FORCE_SCORE, jnp.where(blk1 <= cur, imp, -1.0))
    for _ in range(min(SEL_TOPK, nb)):
        mx = jnp.max(imp, axis=0, keepdims=True)
        idx = jnp.min(jnp.where(imp == mx, blk1, BIG_ID), axis=0, keepdims=True)
        imp = jnp.where(blk1 == idx, -jnp.inf, imp)
    selb = jnp.where(imp == -jnp.inf, 0.0, NEG_INF)
    selb_ref[...] = jnp.concatenate([selb] * NSA_GROUP, axis=1)

    krow = lax.broadcasted_iota(jnp.int32, (tq, w), 0)
    causal = krow <= qloc

    def sel_bias(kt):
        per = tq // SEL_BLOCK
        return jnp.concatenate([jnp.broadcast_to(selb_ref[pl.ds(kt * per + j, 1), :], (SEL_BLOCK, w))
                                for j in range(per)], axis=0)

    def sel_tile(kt):
        k0 = pl.multiple_of(kt * tq, tq)
        s = jnp.dot(ksel_ref[pl.ds(k0, tq), :], qt, preferred_element_type=f32) + sel_bias(kt)
        return s, vselt_ref[:, pl.ds(k0, tq)]

    carry = _tile_loop(qi, sel_tile, _softmax_init(w))
    s, vt = sel_tile(qi)
    m_s, l_s, acc_s = _softmax_step([(jnp.where(causal, s, NEG_INF), vt)], carry)
    o_s = acc_s / l_s

    nwin = WINDOW // tq
    tiles = []
    for j in range(nwin + 1):
        kt = qi - nwin + j
        k0 = pl.multiple_of(jnp.maximum(kt, 0) * tq, tq)
        s = jnp.dot(kwin_ref[pl.ds(k0, tq), :], qt, preferred_element_type=f32)
        ok = kt >= 0
        if j == 0:
            ok = ok & (krow > qloc)
        elif j == nwin:
            ok = ok & causal
        tiles.append((jnp.where(ok, s, NEG_INF), vwint_ref[:, pl.ds(k0, tq)]))
    m_w, l_w, acc_w = _softmax_step(tiles, _softmax_init(w))
    o_w = acc_w / l_w

    for r in range(NSA_GROUP):
        gate = lambda br: gat_ref[pl.ds((g * NSA_GROUP + r) * 3 + br, 1), :]
        sl = slice(r * tq, (r + 1) * tq)
        o_ref[r * HEAD_DIM:(r + 1) * HEAD_DIM, :] = (
            gate(0) * o_c[:, sl] + gate(1) * o_s[:, sl] + gate(2) * o_w[:, sl])


def _nsa_prompt(qat, gat, ksel, vselt, kwin, vwint, kc, vct, nseq, seq_len):
    n = qat.shape[1]
    nq = seq_len // Q_BLOCK
    nb = seq_len // CMP_BLOCK
    gw = NSA_GROUP * LANE
    return pl.pallas_call(
        _nsa_prompt_kernel,
        grid=(nseq, NSA_KV_HEADS, nq),
        in_specs=[pl.BlockSpec((gw, Q_BLOCK), lambda b, g, q: (g, b * nq + q)),
                  pl.BlockSpec((LANE, Q_BLOCK), lambda b, g, q: (0, b * nq + q)),
                  pl.BlockSpec((seq_len, LANE), lambda b, g, q: (b, g)),
                  pl.BlockSpec((HEAD_DIM, seq_len), lambda b, g, q: (g, b)),
                  pl.BlockSpec((seq_len, LANE), lambda b, g, q: (b, g)),
                  pl.BlockSpec((HEAD_DIM, seq_len), lambda b, g, q: (g, b)),
                  pl.BlockSpec((nb, LANE), lambda b, g, q: (b, g)),
                  pl.BlockSpec((HEAD_DIM, nb), lambda b, g, q: (g, b))],
        out_specs=pl.BlockSpec((NSA_GROUP * HEAD_DIM, Q_BLOCK), lambda b, g, q: (g, b * nq + q)),
        out_shape=jax.ShapeDtypeStruct((NSA_Q_W, n), jnp.float32),
        scratch_shapes=[pltpu.VMEM((nb, NSA_GROUP * Q_BLOCK), jnp.float32)],
        compiler_params=pltpu.CompilerParams(dimension_semantics=("arbitrary", "arbitrary", "arbitrary"),
                                             vmem_limit_bytes=VMEM_LIMIT),
        name="nsa_prompt",
    )(qat, gat, ksel, vselt, kwin, vwint, kc, vct)


FOX_TQ = 512
FOX_TK = 128


def _fox_prompt_kernel(qt_ref, kb_ref, vbt_ref, o_ref):
    f32 = jnp.float32
    tq, tk = FOX_TQ, FOX_TK
    qi = pl.program_id(2)
    qt = qt_ref[...]

    def tile(kt):
        k0 = pl.multiple_of(kt * tk, tk)
        return jnp.dot(kb_ref[pl.ds(k0, tk), :], qt, preferred_element_type=f32), vbt_ref[:, pl.ds(k0, tk)]

    ndiag = tq // tk
    carry = _tile_loop(qi * ndiag, tile, _softmax_init(tq))
    krow = lax.broadcasted_iota(jnp.int32, (tk, tq), 0)
    qloc = lax.broadcasted_iota(jnp.int32, (tk, tq), 1)
    diag = []
    for j in range(ndiag):
        s, vt = tile(qi * ndiag + j)
        diag.append((jnp.where(krow + j * tk <= qloc, s, NEG_INF), vt))
    m, l, acc = _softmax_step(diag, carry)
    o_ref[...] = acc / l


def _fox_prompt(qbt, kb, vbt, nseq, seq_len):
    n = qbt.shape[1]
    nq = seq_len // FOX_TQ
    return pl.pallas_call(
        _fox_prompt_kernel,
        grid=(nseq, FOX_HEADS, nq),
        in_specs=[pl.BlockSpec((LANE, FOX_TQ), lambda b, h, q: (h, b * nq + q)),
                  pl.BlockSpec((seq_len, LANE), lambda b, h, q: (b, h)),
                  pl.BlockSpec((HEAD_DIM, seq_len), lambda b, h, q: (h, b))],
        out_specs=pl.BlockSpec((HEAD_DIM, FOX_TQ), lambda b, h, q: (h, b * nq + q)),
        out_shape=jax.ShapeDtypeStruct((FOX_W, n), jnp.float32),
        compiler_params=pltpu.CompilerParams(dimension_semantics=("arbitrary", "arbitrary", "arbitrary"),
                                             vmem_limit_bytes=VMEM_LIMIT),
        name="fox_prompt",
    )(qbt, kb, vbt)


def _slope_rows(shape, rows_per_head):
    hd = lax.broadcasted_iota(jnp.int32, shape, 0) // rows_per_head
    return pltpu.bitcast((126 - hd) << 23, jnp.float32)


def _pad_rows(x, rows):
    return jnp.concatenate([x, jnp.zeros((rows - x.shape[0], x.shape[1]), x.dtype)], axis=0)


def _nt_dot(a, b):
    return lax.dot_general(a, b, (((1,), (1,)), ((), ())), preferred_element_type=jnp.float32)


def _joint_softmax(parts):
    m = functools.reduce(jnp.maximum, [jnp.max(p, axis=1, keepdims=True) for p in parts])
    es = [jnp.exp(p - m) for p in parts]
    return es, sum(jnp.sum(e, axis=1, keepdims=True) for e in es)


def _nsa_sample_kernel(npages, ns, pt_ref, *refs):
    f32, bf16 = jnp.float32, jnp.bfloat16
    pages = refs[:npages]
    (e_ref, qa_ref, ga_ref, rnew_ref, wnew_ref, state_ref, pe_ref, w1_ref, w2_ref, gk_ref, bd_ref,
     o_ref, bufk_ref, bufv_ref) = refs[npages:]
    page = pages[0].shape[-1]
    past = npages * page
    ncol = NSA_KV_HEADS * npages
    nrow = NSA_HEADS * ns

    for p in range(npages):
        bufk_ref[p * LANE:(p + 1) * LANE, :] = pages[p][0, 0].reshape(LANE, page)
        bufv_ref[p * LANE:(p + 1) * LANE, :] = pages[p][0, 1].reshape(LANE, page)
    kc = _summarize(bufk_ref, 0, pe_ref, w1_ref, w2_ref, ncol)
    kc = (kc * lax.rsqrt(_group_mean_sq(kc, bd_ref[...]) + EPS) * gk_ref[...]).astype(bf16)
    vc = _summarize(bufv_ref, 1, pe_ref, w1_ref, w2_ref, ncol).astype(bf16)

    lane8 = _lane(ns)
    ql = qa_ref[...]
    qrows = []
    for hd in range(NSA_HEADS):
        g = hd // NSA_GROUP
        t = ql[:, (hd // 2) * LANE:(hd // 2 + 1) * LANE]
        if hd % 2 != g:
            t = pltpu.roll(t, HEAD_DIM, axis=1)
        qrows.append(jnp.where(lane8 // HEAD_DIM == g, t, 0.0))
    qb = (jnp.concatenate(qrows, axis=0) * SCALE).astype(bf16)

    def geom(width):
        tok = lax.broadcasted_iota(jnp.int32, (nrow, width), 0) % ns
        col = lax.broadcasted_iota(jnp.int32, (nrow, width), 1)
        return tok, col, _slope_rows((nrow, width), ns)

    q_both = qb + pltpu.roll(qb.astype(f32), HEAD_DIM, axis=1).astype(bf16)
    lane64 = lax.broadcasted_iota(jnp.int32, (nrow, LANE), 1) // HEAD_DIM
    tok, col, slope = geom(ncol)
    own = (lax.broadcasted_iota(jnp.int32, (nrow, ncol), 0) // (NSA_GROUP * ns)) == col % NSA_KV_HEADS
    lcs, blks = [], []
    for half in range(2):
        blk = (col // NSA_KV_HEADS) * 2 + half
        lc = _nt_dot(jnp.where(lane64 == half, q_both, 0.0), kc)
        lc = lc - slope * (past + tok - (blk * CMP_BLOCK + CMP_BLOCK - 1)).astype(f32)
        lcs.append(jnp.where(own, lc, NEG_INF))
        blks.append(blk)
    ecs, lsum = _joint_softmax(lcs)
    pcs = [e / lsum for e in ecs]
    res = [jnp.dot(pc.astype(bf16), vc, preferred_element_type=f32) for pc in pcs]
    o_c = jnp.where(lane64 == 0, res[0], res[1])
    o_c = o_c + pltpu.roll(o_c, HEAD_DIM, axis=1)

    nsel = NSA_KV_HEADS * ns
    col_s = lax.broadcasted_iota(jnp.int32, (nsel, ncol), 1)
    own_s = (lax.broadcasted_iota(jnp.int32, (nsel, ncol), 0) // ns) == col_s % NSA_KV_HEADS
    blks_s = [(col_s // NSA_KV_HEADS) * 2 + half for half in range(2)]
    imps = []
    for half in range(2):
        imp = jnp.concatenate(
            [sum(pcs[half][(g * NSA_GROUP + r) * ns:(g * NSA_GROUP + r + 1) * ns] for r in range(NSA_GROUP))
             for g in range(NSA_KV_HEADS)], axis=0)
        imp = jnp.where(blks_s[half] == 0, FORCE_SCORE, imp)
        imps.append(jnp.where(own_s, imp, -1.0))
    ranks = [jnp.zeros((nsel, ncol), jnp.int32) for _ in range(2)]
    for h2 in range(2):
        for c in range(ncol):
            other = jnp.broadcast_to(imps[h2][:, c:c + 1], (nsel, ncol))
            blk_c = (c // NSA_KV_HEADS) * 2 + h2
            for half in range(2):
                ahead = (other > imps[half]) | ((other == imps[half]) & (blks_s[half] > blk_c))
                ranks[half] = ranks[half] + jnp.where(ahead, 1, 0)
    selexp = 0.0
    for half in range(2):
        sel = jnp.where((ranks[half] < SEL_TOPK - 1) & own_s, 1.0, 0.0)
        sel = jnp.concatenate([sel[g * ns:(g + 1) * ns] for g in range(NSA_KV_HEADS) for _ in range(NSA_GROUP)],
                              axis=0)
        selexp = selexp + jnp.dot(sel.astype(bf16), e_ref[half], preferred_element_type=f32)

    def new_tile(k_new):
        tok, col, slope = geom(LANE)
        s = _nt_dot(qb, _pad_rows(k_new, LANE).astype(bf16))
        return jnp.where(col <= tok, s - slope * (tok - col).astype(f32), NEG_INF)

    def weighted(es, vts, e_new, v_new, lsum):
        acc = sum(_nt_dot(e.astype(bf16), vt.astype(bf16)) for e, vt in zip(es, vts))
        acc = acc + jnp.dot(e_new.astype(bf16), _pad_rows(v_new, LANE).astype(bf16), preferred_element_type=f32)
        return acc / lsum

    tok, col, slope = geom(past)
    ls = jnp.concatenate([jnp.dot(qb, pages[p][0, 2].reshape(LANE, page).astype(bf16), preferred_element_type=f32)
                          for p in range(npages)], axis=1)
    ls = jnp.where(selexp > 0.5, ls - slope * (past + tok - col).astype(f32), NEG_INF)
    (es, en), lsum = _joint_softmax([ls, new_tile(rnew_ref[:, 2 * LANE:3 * LANE])])
    o_s = weighted([es[:, p * page:(p + 1) * page] for p in range(npages)],
                   [pages[p][0, 3].reshape(LANE, page) for p in range(npages)],
                   en, rnew_ref[:, 3 * LANE:4 * LANE], lsum)

    wbuf = state_ref.shape[-1]
    tok, col, slope = geom(wbuf)
    lw = jnp.dot(qb, state_ref[0, 0].reshape(LANE, wbuf).astype(bf16), preferred_element_type=f32)
    lw = jnp.where(col > tok + (wbuf - WINDOW), lw - slope * (wbuf + tok - col).astype(f32), NEG_INF)
    (ew, en), lsum = _joint_softmax([lw, new_tile(wnew_ref[:, 0:LANE])])
    o_w = weighted([ew], [state_ref[0, 1].reshape(LANE, wbuf)], en, wnew_ref[:, LANE:2 * LANE], lsum)

    ga = ga_ref[...]
    gate = lambda br: jnp.concatenate(
        [jnp.broadcast_to(ga[:, hd * 3 + br:hd * 3 + br + 1], (ns, LANE)) for hd in range(NSA_HEADS)], axis=0)
    o = gate(0) * o_c + gate(1) * o_s + gate(2) * o_w
    for j in range(NSA_HEADS // 2):
        g = (2 * j) // NSA_GROUP
        a = o[2 * j * ns:(2 * j + 1) * ns]
        b = o[(2 * j + 1) * ns:(2 * j + 2) * ns]
        if g == 0:
            b = pltpu.roll(b, HEAD_DIM, axis=1)
        else:
            a = pltpu.roll(a, HEAD_DIM, axis=1)
        o_ref[:, j * LANE:(j + 1) * LANE] = jnp.where(lane8 < HEAD_DIM, a, b)


def _compress_weights_t(cmp_pe, cmp_w1, cmp_w2):
    pe, w1, w2 = _compress_weights(cmp_pe.transpose(0, 2, 1), cmp_w1.transpose(0, 2, 1, 3), cmp_w2)
    return pe, w1, w2


def _nsa_sample(page_table, cache_t, qa, ga, rows_new, win_new, state_t, pe, w1, w2, gk, bd):
    db, npages = page_table.shape
    page = cache_t.shape[-1]
    assert page == 2 * CMP_BLOCK == LANE
    ns = qa.shape[0] // db
    past = npages * page
    ncol = NSA_KV_HEADS * npages
    half = lax.broadcasted_iota(jnp.int32, (2, ncol, past), 0)
    col = lax.broadcasted_iota(jnp.int32, (2, ncol, past), 1)
    key = lax.broadcasted_iota(jnp.int32, (2, ncol, past), 2)
    expand = jnp.where(key // SEL_BLOCK == (col // NSA_KV_HEADS) * 2 + half, 1.0, 0.0).astype(jnp.bfloat16)
    gk2 = jnp.concatenate([gk, gk]).reshape(1, LANE)
    full = lambda a: pl.BlockSpec(a.shape, lambda b, pt: (0,) * a.ndim)
    tok = lambda w: pl.BlockSpec((ns, w), lambda b, pt: (b, 0))
    page_specs = [pl.BlockSpec((1,) + cache_t.shape[1:], functools.partial(lambda p, b, pt: (pt[b, p], 0, 0, 0, 0), p))
                  for p in range(npages)]
    return pl.pallas_call(
        functools.partial(_nsa_sample_kernel, npages, ns),
        grid_spec=pltpu.PrefetchScalarGridSpec(
            num_scalar_prefetch=1, grid=(db,),
            in_specs=page_specs + [full(expand), tok(NSA_Q_W), tok(LANE), tok(512), tok(256),
                                   pl.BlockSpec((1,) + state_t.shape[1:], lambda b, pt: (b, 0, 0, 0, 0)),
                                   full(pe), full(w1), full(w2), full(gk2), full(bd)],
            out_specs=tok(NSA_Q_W),
            scratch_shapes=[pltpu.VMEM((npages * LANE, page), jnp.float32),
                            pltpu.VMEM((npages * LANE, page), jnp.float32)]),
        out_shape=jax.ShapeDtypeStruct((db * ns, NSA_Q_W), jnp.float32),
        compiler_params=pltpu.CompilerParams(dimension_semantics=("arbitrary",),
                                             vmem_limit_bytes=VMEM_LIMIT),
        name="nsa_sample",
    )(page_table, *([cache_t] * npages), expand, qa, ga, rows_new, win_new, state_t, pe, w1, w2, gk2, bd)


def _fox_decode_kernel(npages, ns, pt_ref, *refs):
    f32, bf16 = jnp.float32, jnp.bfloat16
    kvt = refs[:npages]
    lft = refs[npages:2 * npages]
    qb_ref, knew_ref, lfnew_ref, o_ref = refs[2 * npages:]
    page = kvt[0].shape[-1]
    nh = FOX_HEADS
    nrow = nh * ns

    r_io = lax.broadcasted_iota(jnp.int32, (page, page), 0)
    c_io = lax.broadcasted_iota(jnp.int32, (page, page), 1)
    triu = jnp.where(r_io <= c_io, 1.0, 0.0).astype(bf16)
    carry = jnp.zeros((nh, 1), f32)
    negc = []
    for t in range(npages + 1):
        lf = lft[t][0] if t < npages else lfnew_ref[0]
        ct = carry + sum(jnp.dot(pc_, triu, preferred_element_type=f32) for pc_ in _split3(lf))
        carry = ct[:, page - 1:page]
        negc.append(jnp.concatenate([jnp.broadcast_to(-ct[hd:hd + 1], (ns, page)) for hd in range(nh)], axis=0))

    head_of_lane = lax.broadcasted_iota(jnp.int32, (ns, FOX_W), 1) // HEAD_DIM
    q = qb_ref[...] * SCALE
    q_bd = jnp.concatenate([jnp.where(head_of_lane == hd, q, 0.0) for hd in range(nh)], axis=0).astype(bf16)

    parts = [jnp.dot(q_bd, kvt[p][0, 0].reshape(FOX_W, page).astype(bf16), preferred_element_type=f32) + negc[p]
             for p in range(npages)]
    tok = lax.broadcasted_iota(jnp.int32, (nrow, page), 0) % ns
    col = lax.broadcasted_iota(jnp.int32, (nrow, page), 1)
    s_new = _nt_dot(q_bd, _pad_rows(knew_ref[:, 0:FOX_W], page).astype(bf16)) + negc[npages]
    parts.append(jnp.where(col <= tok, s_new, NEG_INF))
    es, lsum = _joint_softmax(parts)
    acc = sum(_nt_dot(es[p].astype(bf16), kvt[p][0, 1].reshape(FOX_W, page).astype(bf16)) for p in range(npages))
    acc = acc + jnp.dot(es[npages].astype(bf16), _pad_rows(knew_ref[:, FOX_W:2 * FOX_W], page).astype(bf16),
                        preferred_element_type=f32)
    acc = acc / lsum
    o_ref[...] = sum(jnp.where(head_of_lane == hd, acc[hd * ns:(hd + 1) * ns], 0.0) for hd in range(nh))


def _fox_decode(page_table, cache_kvt, lft, qb, fox_new, lft_new):
    db, npages = page_table.shape
    ns = qb.shape[0] // db
    tok = lambda w: pl.BlockSpec((ns, w), lambda b, pt: (b, 0))
    pg = lambda a: [pl.BlockSpec((1,) + a.shape[1:],
                                 functools.partial(lambda p, nd, b, pt: (pt[b, p],) + (0,) * nd, p, a.ndim - 1))
                    for p in range(npages)]
    return pl.pallas_call(
        functools.partial(_fox_decode_kernel, npages, ns),
        grid_spec=pltpu.PrefetchScalarGridSpec(
            num_scalar_prefetch=1, grid=(db,),
            in_specs=pg(cache_kvt) + pg(lft) + [tok(FOX_W), tok(2 * FOX_W),
                                                pl.BlockSpec((1,) + lft_new.shape[1:], lambda b, pt: (b, 0, 0))],
            out_specs=tok(FOX_W)),
        out_shape=jax.ShapeDtypeStruct((db * ns, FOX_W), jnp.float32),
        compiler_params=pltpu.CompilerParams(dimension_semantics=("arbitrary",),
                                             vmem_limit_bytes=VMEM_LIMIT),
        name="fox_decode",
    )(page_table, *([cache_kvt] * npages), *([lft] * npages), qb, fox_new, lft_new)


BIG_ID = 1 << 20


def _topk_rows(s, k, ids):
    ids = ids.astype(jnp.float32)
    vals, idxs = [], []
    for _ in range(k):
        m = jnp.max(s, axis=0, keepdims=True)
        idx = jnp.min(jnp.where(s == m, ids, float(BIG_ID)), axis=0, keepdims=True)
        vals.append(m)
        idxs.append(idx)
        s = jnp.where(ids == idx, -jnp.inf, s)
    return jnp.concatenate(vals, axis=0), jnp.concatenate(idxs, axis=0).astype(jnp.int32)


def _pick_rows(sel, table):
    out = jnp.zeros(sel.shape, table.dtype)
    for r in range(table.shape[0]):
        out = jnp.where(sel == r, table[r:r + 1, :], out)
    return out


def _merge_route_kernel(x_ref, on_ref, of_ref, mg_ref, wun_ref, wuf_ref, wo_ref, nf_ref, wqt_ref, sk_ref,
                        x1_ref, h2_ref, i1_ref, i2_ref, g_ref):
    tm = x_ref.shape[0]
    f32, bf16 = jnp.float32, jnp.bfloat16
    tdot = lambda ot, wgt: lax.dot_general(ot.astype(bf16), wgt, (((0,), (0,)), ((), ())),
                                           preferred_element_type=f32)
    a = tdot(on_ref[...], wun_ref[...])
    b = tdot(of_ref[...], wuf_ref[...])
    mixed = mg_ref[:, 0:D_MODEL] * a + mg_ref[:, D_MODEL:2 * D_MODEL] * b
    x1 = x_ref[...] + jnp.dot(mixed.astype(bf16), wo_ref[...], preferred_element_type=f32)
    x1_ref[...] = x1
    h2 = x1 * lax.rsqrt(jnp.mean(x1 * x1, axis=-1, keepdims=True) + EPS) * nf_ref[...]
    h2b = h2.astype(bf16)
    h2_ref[...] = h2b

    nk = PEER_N_KEYS
    key_ids = lax.broadcasted_iota(jnp.int32, (nk, tm), 0)
    io16 = lax.broadcasted_iota(jnp.int32, (PEER_TOPK, tm), 0)
    io8 = lax.broadcasted_iota(jnp.int32, (8, tm), 0)
    cand_ids = jnp.concatenate([io16] + [a_ * PEER_TOPK + io8 for a_ in range(1, 8)]
                               + [(io8 + 8) * PEER_TOPK], axis=0)
    for h in range(PEER_HEADS):
        sv, si = [], []
        for p in range(2):
            hp = 2 * h + p
            qt = lax.dot_general(wqt_ref[hp * PEER_DK_HALF:(hp + 1) * PEER_DK_HALF, :], h2b,
                                 (((1,), (1,)), ((), ())), preferred_element_type=f32)
            st = jnp.dot(sk_ref[hp], qt.astype(bf16), preferred_element_type=f32)
            v, i = _topk_rows(st, PEER_TOPK, key_ids)
            sv.append(v)
            si.append(i)
        s1, s2 = sv
        cand = jnp.concatenate([s1[0:1] + s2] + [s1[a_:a_ + 1] + s2[0:8] for a_ in range(1, 8)]
                               + [s1[8:16] + s2[0:1]], axis=0)
        top, fid = _topk_rows(cand, PEER_TOPK, cand_ids)
        e = jnp.exp(top - jnp.max(top, axis=0, keepdims=True))
        g = e / jnp.sum(e, axis=0, keepdims=True)
        sl = slice(h * PEER_TOPK, (h + 1) * PEER_TOPK)
        i1_ref[:, sl] = _pick_rows(fid >> 4, si[0]).T
        i2_ref[:, sl] = _pick_rows(fid & (PEER_TOPK - 1), si[1]).T
        g_ref[:, sl] = g.T


def _merge_route(x2d, o_nsa, o_fox, mg, w_up_nsa, w_up_fox, w_out, norm_ffn, wq_t, sub_keys, tm=256):
    n = x2d.shape[0]
    row = lambda w: pl.BlockSpec((tm, w), lambda i: (i, 0))
    col = lambda h: pl.BlockSpec((h, tm), lambda i: (0, i))
    full = lambda a: pl.BlockSpec(a.shape, lambda i: (0,) * a.ndim)
    args = (x2d, o_nsa, o_fox, mg, w_up_nsa, w_up_fox, w_out, norm_ffn.reshape(1, D_MODEL), wq_t, sub_keys)
    hk = PEER_HEADS * PEER_TOPK
    return pl.pallas_call(
        _merge_route_kernel,
        grid=(n // tm,),
        in_specs=[row(D_MODEL), col(NSA_Q_W), col(FOX_W), row(MERGE_W)] + [full(a) for a in args[4:]],
        out_specs=[row(D_MODEL), row(D_MODEL), row(hk), row(hk), row(hk)],
        out_shape=[jax.ShapeDtypeStruct((n, D_MODEL), jnp.float32),
                   jax.ShapeDtypeStruct((n, D_MODEL), jnp.bfloat16),
                   jax.ShapeDtypeStruct((n, hk), jnp.int32),
                   jax.ShapeDtypeStruct((n, hk), jnp.int32),
                   jax.ShapeDtypeStruct((n, hk), jnp.float32)],
        compiler_params=pltpu.CompilerParams(dimension_semantics=("arbitrary",),
                                             vmem_limit_bytes=VMEM_LIMIT),
        name="merge_route",
    )(*args)


def _peer_act_kernel(h2_ref, u_ref, i1_ref, i2_ref, act_ref):
    c = pl.program_id(1)
    ec = u_ref.shape[0]

    @pl.when(c == 0)
    def _():
        act_ref[...] = jnp.zeros_like(act_ref)

    a = lax.dot_general(h2_ref[...], u_ref[...], (((1,), (1,)), ((), ())),
                        preferred_element_type=jnp.float32)
    i1 = i1_ref[...]
    i2 = i2_ref[...]
    act = act_ref[...]
    for ii in range(ec // PEER_N_KEYS):
        got = jnp.take_along_axis(a[:, ii * PEER_N_KEYS:(ii + 1) * PEER_N_KEYS], i2, axis=1)
        act = jnp.where(i1 == c * (ec // PEER_N_KEYS) + ii, got, act)
    act_ref[...] = act


def _peer_act(h2b, u_b, i1, i2, tm=512, ec=2048):
    n = h2b.shape[0]
    hk = i1.shape[1]
    return pl.pallas_call(
        _peer_act_kernel,
        grid=(n // tm, u_b.shape[0] // ec),
        in_specs=[pl.BlockSpec((tm, D_MODEL), lambda t, c: (t, 0)),
                  pl.BlockSpec((ec, D_MODEL), lambda t, c: (c, 0)),
                  pl.BlockSpec((tm, hk), lambda t, c: (t, 0)),
                  pl.BlockSpec((tm, hk), lambda t, c: (t, 0))],
        out_specs=pl.BlockSpec((tm, hk), lambda t, c: (t, 0)),
        out_shape=jax.ShapeDtypeStruct((n, hk), jnp.float32),
        compiler_params=pltpu.CompilerParams(dimension_semantics=("arbitrary", "arbitrary"),
                                             vmem_limit_bytes=VMEM_LIMIT),
        name="peer_act",
    )(h2b, u_b, i1, i2)


def _peer_coef_kernel(act_ref, g_ref, i1_ref, i2_ref, c_ref, coef_ref):
    tm = act_ref.shape[0]
    nk = PEER_N_KEYS
    coef_ref[...] = g_ref[...] * jax.nn.gelu(act_ref[...])
    sub = lax.broadcasted_iota(jnp.int32, (nk, i1_ref.shape[1]), 0)

    def token(t):
        r1 = i1_ref[pl.ds(t, 1), :]
        r2 = i2_ref[pl.ds(t, 1), :]
        cf = coef_ref[pl.ds(t, 1), :]
        m1 = jnp.where(r1 == sub, cf, 0.0).astype(jnp.bfloat16)
        m2t = jnp.where(r2 == sub, 1.0, 0.0).astype(jnp.bfloat16)
        return lax.dot_general(m1, m2t, (((1,), (1,)), ((), ())), preferred_element_type=jnp.float32)

    def body(tg, carry):
        t0 = pl.multiple_of(tg * COEF_GROUP, COEF_GROUP)
        ct = jnp.stack([token(t0 + u) for u in range(COEF_GROUP)], axis=0)
        c_ref[:, pl.ds(t0, COEF_GROUP), :] = pltpu.einshape("tij->itj", ct).astype(c_ref.dtype)
        return carry

    lax.fori_loop(0, tm // COEF_GROUP, body, 0)


COEF_GROUP = 16


def _peer_coef(act, g, i1, i2, tm=128):
    n, hk = act.shape
    nk = PEER_N_KEYS
    row = pl.BlockSpec((tm, hk), lambda t: (t, 0))
    return pl.pallas_call(
        _peer_coef_kernel,
        grid=(n // tm,),
        in_specs=[row, row, row, row],
        out_specs=pl.BlockSpec((nk, tm, nk), lambda t: (0, t, 0)),
        out_shape=jax.ShapeDtypeStruct((nk, n, nk), jnp.bfloat16),
        scratch_shapes=[pltpu.VMEM((tm, hk), jnp.float32)],
        compiler_params=pltpu.CompilerParams(dimension_semantics=("arbitrary",),
                                             vmem_limit_bytes=VMEM_LIMIT),
        name="peer_coef",
    )(act, g, i1, i2)


def _peer_out_kernel(c_ref, v_ref, x1_ref, y_ref, acc_ref):
    k = pl.program_id(1)

    @pl.when(k == 0)
    def _():
        acc_ref[...] = x1_ref[...]

    acc = acc_ref[...]
    nk = PEER_N_KEYS
    for p in range(c_ref.shape[0] // 2):
        lhs = jnp.concatenate([c_ref[2 * p], c_ref[2 * p + 1]], axis=1)
        acc = acc + jnp.dot(lhs, v_ref[2 * p * nk:(2 * p + 2) * nk, :], preferred_element_type=jnp.float32)
    acc_ref[...] = acc

    @pl.when(k == pl.num_programs(1) - 1)
    def _():
        y_ref[...] = acc_ref[...]


def _peer_out(c3, v_b, x1, tm=1024, tk=2048):
    nk, n, _ = c3.shape
    ne = nk * nk
    return pl.pallas_call(
        _peer_out_kernel,
        grid=(n // tm, ne // tk),
        in_specs=[pl.BlockSpec((tk // nk, tm, nk), lambda t, k: (k, t, 0)),
                  pl.BlockSpec((tk, D_MODEL), lambda t, k: (k, 0)),
                  pl.BlockSpec((tm, D_MODEL), lambda t, k: (t, 0))],
        out_specs=pl.BlockSpec((tm, D_MODEL), lambda t, k: (t, 0)),
        out_shape=jax.ShapeDtypeStruct((n, D_MODEL), jnp.float32),
        scratch_shapes=[pltpu.VMEM((tm, D_MODEL), jnp.float32)],
        compiler_params=pltpu.CompilerParams(dimension_semantics=("arbitrary", "arbitrary"),
                                             vmem_limit_bytes=VMEM_LIMIT),
        name="peer_out",
    )(c3, v_b, x1)


def _peer_weights(w_up_nsa, w_up_fox, w_out, norm_ffn, peer_w_query, peer_sub_keys, peer_u, peer_v):
    bf16 = jnp.bfloat16
    return dict(w_up_nsa=w_up_nsa.astype(bf16), w_up_fox=w_up_fox.astype(bf16), w_out=w_out.astype(bf16),
                norm_ffn=norm_ffn, wq_t=peer_w_query.T.astype(bf16),
                sub_keys=peer_sub_keys.reshape(2 * PEER_HEADS, PEER_N_KEYS, PEER_DK_HALF).astype(bf16),
                u=peer_u.astype(bf16), v=peer_v.astype(bf16))


def _merge_peer(x2d, o_nsa, o_fox, mg, wts):
    x1, h2b, i1, i2, g = _merge_route(x2d, o_nsa, o_fox, mg, wts['w_up_nsa'], wts['w_up_fox'], wts['w_out'],
                                      wts['norm_ffn'], wts['wq_t'], wts['sub_keys'])
    act = _peer_act(h2b, wts['u'], i1, i2)
    c3 = _peer_coef(act, g, i1, i2)
    return _peer_out(c3, wts['v'], x1)


def kernel(x_prompt, x_sample, cache_nsa, cache_fox_kv, cache_fox_logf, state_nsa_win, page_table,
           norm_attn, w_in, fox_f_bias, nsa_q_norm, nsa_k_norm, fox_q_norm, fox_k_norm,
           cmp_pe, cmp_w1, cmp_w2, w_up_nsa, w_up_fox, w_out, norm_ffn,
           peer_w_query, peer_sub_keys, peer_u, peer_v):
    w_front = _front_weights(w_in)
    bd = _block_diag_mean()
    wts = _peer_weights(w_up_nsa, w_up_fox, w_out, norm_ffn, peer_w_query, peer_sub_keys, peer_u, peer_v)
    cmp_wts = _compress_weights(cmp_pe, cmp_w1, cmp_w2)

    bp, seq, _ = x_prompt.shape
    n_p = bp * seq
    (rows_p, win_p2d, fox_p2d, logf_p2d, mg_p, qat, gat, ksel, kwin, vselt, vwint, qbt, kb, vbt) = _front_attn(
        x_prompt.reshape(n_p, D_MODEL), seq, norm_attn, w_front, bd, fox_f_bias,
        nsa_q_norm, nsa_k_norm, fox_q_norm, fox_k_norm)
    kc, vct = _compress(rows_p, seq // CMP_BLOCK, *cmp_wts, nsa_k_norm[0], bd)
    o_nsa_t = _nsa_prompt(qat, gat, ksel, vselt, kwin, vwint, kc, vct, bp, seq)
    o_fox_t = _fox_prompt(qbt, kb, vbt, bp, seq)
    y_p = _merge_peer(x_prompt.reshape(n_p, D_MODEL), o_nsa_t, o_fox_t, mg_p, wts).reshape(x_prompt.shape)
    nsa_p = rows_p.reshape(bp, seq, 4, NSA_KV_HEADS, HEAD_DIM)
    fox_p = fox_p2d.reshape(bp, seq, 2, FOX_HEADS, HEAD_DIM)
    logf_p = logf_p2d.reshape(bp, seq, FOX_HEADS)
    win_p = win_p2d.reshape(bp, seq, 2, NSA_KV_HEADS, HEAD_DIM)[:, seq - min(WINDOW, seq):]

    db, ns, _ = x_sample.shape
    n_s = db * ns
    n_pool, page = cache_nsa.shape[:2]
    wbuf = state_nsa_win.shape[1]
    qa_s, rows_s, win_s2d, ga_s, qb_s, fox_s2d, logf_s2d, mg_s = _front(
        x_sample.reshape(n_s, D_MODEL), norm_attn, w_front, bd, fox_f_bias,
        nsa_q_norm, nsa_k_norm, fox_q_norm, fox_k_norm)
    o_nsa_s = _nsa_sample(page_table, cache_nsa.transpose(0, 2, 3, 4, 1), qa_s, ga_s, rows_s, win_s2d,
                          state_nsa_win.transpose(0, 2, 3, 4, 1),
                          *_compress_weights_t(cmp_pe, cmp_w1, cmp_w2), nsa_k_norm[0], bd)
    lft_new = jnp.pad(logf_s2d.reshape(db, ns, FOX_HEADS).transpose(0, 2, 1), ((0, 0), (0, 0), (0, page - ns)))
    o_fox_s = _fox_decode(page_table, cache_fox_kv.transpose(0, 2, 3, 4, 1), cache_fox_logf.transpose(0, 2, 1),
                          qb_s, fox_s2d, lft_new)
    y_s = _merge_peer(x_sample.reshape(n_s, D_MODEL), o_nsa_s.T, o_fox_s.T, mg_s, wts).reshape(x_sample.shape)
    nsa_s = rows_s.reshape(db, ns, 4, NSA_KV_HEADS, HEAD_DIM)
    fox_s = fox_s2d.reshape(db, ns, 2, FOX_HEADS, HEAD_DIM)
    logf_s = logf_s2d.reshape(db, ns, FOX_HEADS)
    win_s = jnp.concatenate([state_nsa_win[:, ns:], win_s2d.reshape(db, ns, 2, NSA_KV_HEADS, HEAD_DIM)], axis=1)
    return (y_p, y_s, nsa_p, fox_p, logf_p, win_p, nsa_s, fox_s, logf_s, win_s)
```

```python
import functools

import jax
import jax.numpy as jnp
from jax import lax
from jax.experimental import pallas as pl
from jax.experimental.pallas import tpu as pltpu

D_MODEL = 1024
HEAD_DIM = 64
NSA_HEADS = 8
NSA_KV_HEADS = 2
NSA_GROUP = NSA_HEADS // NSA_KV_HEADS
CMP_BLOCK = 64
SEL_BLOCK = CMP_BLOCK
SEL_TOPK = 16
WINDOW = 512
FOX_HEADS = 8
Q_BLOCK = 128
PEER_HEADS = 8
PEER_N_KEYS = 128
PEER_DK = 256
PEER_DK_HALF = PEER_DK // 2
PEER_TOPK = 16
PEER_CHUNK = 256

NSA_Q_W = NSA_HEADS * HEAD_DIM
NSA_KV_W = 6 * NSA_KV_HEADS * HEAD_DIM
NSA_GATE_W = 3 * NSA_HEADS
FOX_W = FOX_HEADS * HEAD_DIM
FOX_QKV_W = 3 * FOX_W
FOX_F_W = FOX_HEADS
MERGE_W = 2 * D_MODEL
SPLIT_Q_A = NSA_Q_W
SPLIT_KV_A = SPLIT_Q_A + NSA_KV_W
SPLIT_G_A = SPLIT_KV_A + NSA_GATE_W
SPLIT_QKV_B = SPLIT_G_A + FOX_QKV_W
SPLIT_F_B = SPLIT_QKV_B + FOX_F_W
IN_WIDTH = SPLIT_F_B + MERGE_W

SCALE = HEAD_DIM ** -0.5
FORCE_SCORE = float(NSA_GROUP + 1)
NEG_INF = -1e30
EPS = 1e-6

LANE = 128
VMEM_LIMIT = 48 * 1024 * 1024


def _group_mean_sq(x, bd):
    sq = x * x
    hi = sq.astype(jnp.bfloat16)
    lo = (sq - hi.astype(jnp.float32)).astype(jnp.bfloat16)
    return (jnp.dot(hi, bd, preferred_element_type=jnp.float32)
            + jnp.dot(lo, bd, preferred_element_type=jnp.float32))


def _head_rms(x, g, bd):
    outs = []
    for c in range(x.shape[1] // LANE):
        xc = x[:, c * LANE:(c + 1) * LANE]
        outs.append(xc * lax.rsqrt(_group_mean_sq(xc, bd) + EPS) * g)
    return outs[0] if len(outs) == 1 else jnp.concatenate(outs, axis=1)


def _front_kernel(x_ref, na_ref, w_ref, bd_ref, fb_ref, gq_a_ref, gk_sel_ref, gk_win_ref,
                  gq_b_ref, gk_b_ref,
                  qa_ref, rows_ref, win_ref, ga_ref, qb_ref, fox_ref, logf_ref, mg_ref):
    x = x_ref[...]
    h = x * lax.rsqrt(jnp.mean(x * x, axis=-1, keepdims=True) + EPS) * na_ref[...]
    hb = h.astype(jnp.bfloat16)
    bd = bd_ref[...]

    def proj(c0, width):
        return jnp.dot(hb, w_ref[:, c0:c0 + width], preferred_element_type=jnp.float32)

    c = 0
    qa_ref[...] = _head_rms(proj(c, NSA_Q_W), gq_a_ref[...], bd)
    c += NSA_Q_W
    rows_ref[:, 0:256] = proj(c, 256)
    rows_ref[:, 256:384] = _head_rms(proj(c + 256, 128), gk_sel_ref[...], bd)
    rows_ref[:, 384:512] = proj(c + 384, 128)
    win_ref[:, 0:128] = _head_rms(proj(c + 512, 128), gk_win_ref[...], bd)
    win_ref[:, 128:256] = proj(c + 640, 128)
    c += NSA_KV_W
    qb_ref[...] = _head_rms(proj(c, FOX_W), gq_b_ref[...], bd)
    fox_ref[:, 0:FOX_W] = _head_rms(proj(c + FOX_W, FOX_W), gk_b_ref[...], bd)
    fox_ref[:, FOX_W:2 * FOX_W] = proj(c + 2 * FOX_W, FOX_W)
    c += FOX_QKV_W
    for j in range(MERGE_W // 512):
        mg_ref[:, j * 512:(j + 1) * 512] = jax.nn.sigmoid(proj(c + j * 512, 512))
    c += MERGE_W
    ga_ref[...] = jax.nn.sigmoid(proj(c, LANE))
    f = proj(c + LANE, LANE)[:, 0:FOX_F_W] + fb_ref[...]
    logf_ref[...] = jnp.minimum(f, 0.0) - jnp.log1p(jnp.exp(-jnp.abs(f)))


def _front(x2d, norm_attn, w_front, bd, fox_f_bias, nsa_q_norm, nsa_k_norm, fox_q_norm, fox_k_norm,
           tm=256):
    n = x2d.shape[0]
    wf = w_front.shape[1]
    two = lambda g: jnp.concatenate([g, g]).reshape(1, LANE)
    row = lambda w: pl.BlockSpec((tm, w), lambda i: (i, 0))
    full = lambda a: pl.BlockSpec(a.shape, lambda i: (0,) * a.ndim)
    args = (x2d, norm_attn.reshape(1, D_MODEL), w_front, bd, fox_f_bias.reshape(1, FOX_F_W),
            two(nsa_q_norm), two(nsa_k_norm[1]), two(nsa_k_norm[2]), two(fox_q_norm), two(fox_k_norm))
    widths = (NSA_Q_W, 512, 256, LANE, FOX_W, 2 * FOX_W, FOX_F_W, MERGE_W)
    return pl.pallas_call(
        _front_kernel,
        grid=(n // tm,),
        in_specs=[row(D_MODEL)] + [full(a) for a in args[1:]],
        out_specs=[row(w) for w in widths],
        out_shape=[jax.ShapeDtypeStruct((n, w), jnp.float32) for w in widths],
        compiler_params=pltpu.CompilerParams(dimension_semantics=("arbitrary",),
                                             vmem_limit_bytes=VMEM_LIMIT),
        name="front",
    )(*args)


FEAT = HEAD_DIM


def _lane(tm):
    return lax.broadcasted_iota(jnp.int32, (tm, LANE), 1)


def _expand_halves(x):
    lo = _lane(x.shape[0]) < HEAD_DIM
    return jnp.where(lo, x, 0.0), jnp.where(lo, pltpu.roll(x, HEAD_DIM, axis=1), 0.0)


def _split3(x):
    hi = x.astype(jnp.bfloat16)
    r = x - hi.astype(jnp.float32)
    mid = r.astype(jnp.bfloat16)
    lo = (r - mid.astype(jnp.float32)).astype(jnp.bfloat16)
    return hi, mid, lo


def _front_attn_kernel(seq_len, x_ref, na_ref, w_ref, bd_ref, fb_ref, gq_a_ref, gk_sel_ref, gk_win_ref,
                       gq_b_ref, gk_b_ref, place_ref,
                       rows_ref, win_ref, fox_ref, logf_ref, mg_ref,
                       qat_ref, gat_ref, ksel_ref, kwin_ref, vselt_ref, vwint_ref, qbt_ref, kb_ref, vbt_ref,
                       carry_ref):
    f32, bf16 = jnp.float32, jnp.bfloat16
    tm = x_ref.shape[0]
    i = pl.program_id(0)
    x = x_ref[...]
    h = x * lax.rsqrt(jnp.mean(x * x, axis=-1, keepdims=True) + EPS) * na_ref[...]
    hb = h.astype(bf16)
    bd = bd_ref[...]
    lane = _lane(tm)
    pos = (i * tm + lax.broadcasted_iota(jnp.int32, (tm, LANE), 0)) % seq_len
    kfeat = jnp.where(lane == FEAT, (pos // SEL_BLOCK).astype(f32),
                      jnp.where(lane == FEAT + 1, (pos % SEL_BLOCK).astype(f32), 0.0))

    def proj(c0, width):
        return jnp.dot(hb, w_ref[:, c0:c0 + width], preferred_element_type=f32)

    c = 0
    qa = _head_rms(proj(c, NSA_Q_W), gq_a_ref[...], bd)
    for j in range(NSA_HEADS // 2):
        for s, half in enumerate(_expand_halves(qa[:, j * LANE:(j + 1) * LANE])):
            hd = 2 * j + s
            slope = 2.0 ** -(hd + 1)
            qfeat = jnp.where(lane == FEAT, slope * SEL_BLOCK, jnp.where(lane == FEAT + 1, slope, 0.0))
            qat_ref[hd * LANE:(hd + 1) * LANE, :] = (half * SCALE + qfeat).T.astype(bf16)
    c += NSA_Q_W
    rows_ref[:, 0:256] = proj(c, 256)
    ksel = _head_rms(proj(c + 256, 128), gk_sel_ref[...], bd)
    rows_ref[:, 256:384] = ksel
    vsel = proj(c + 384, 128)
    rows_ref[:, 384:512] = vsel
    kwin = _head_rms(proj(c + 512, 128), gk_win_ref[...], bd)
    win_ref[:, 0:128] = kwin
    vwin = proj(c + 640, 128)
    win_ref[:, 128:256] = vwin
    for g, (ks_g, kw_g) in enumerate(zip(_expand_halves(ksel), _expand_halves(kwin))):
        ksel_ref[:, g * LANE:(g + 1) * LANE] = (ks_g + kfeat).astype(bf16)
        kwin_ref[:, g * LANE:(g + 1) * LANE] = (kw_g + kfeat).astype(bf16)
    vselt_ref[...] = vsel.T.astype(bf16)
    vwint_ref[...] = vwin.T.astype(bf16)
    c += NSA_KV_W

    qb = _head_rms(proj(c, FOX_W), gq_b_ref[...], bd)
    ones3 = jnp.where((lane >= FEAT) & (lane < FEAT + 3), 1.0, 0.0)
    for j in range(FOX_HEADS // 2):
        for s, half in enumerate(_expand_halves(qb[:, j * LANE:(j + 1) * LANE])):
            hd = 2 * j + s
            qbt_ref[hd * LANE:(hd + 1) * LANE, :] = (half * SCALE + ones3).T.astype(bf16)
    kbn = _head_rms(proj(c + FOX_W, FOX_W), gk_b_ref[...], bd)
    fox_ref[:, 0:FOX_W] = kbn
    for j in range(FOX_HEADS // 2):
        vb = proj(c + 2 * FOX_W + j * LANE, LANE)
        fox_ref[:, FOX_W + j * LANE:FOX_W + (j + 1) * LANE] = vb
        vbt_ref[j * LANE:(j + 1) * LANE, :] = vb.T.astype(bf16)
    c += FOX_QKV_W
    for j in range(MERGE_W // 512):
        mg_ref[:, j * 512:(j + 1) * 512] = jax.nn.sigmoid(proj(c + j * 512, 512))
    c += MERGE_W
    gat_ref[...] = jax.nn.sigmoid(proj(c, LANE)).T
    f = proj(c + LANE, LANE) + fb_ref[...]
    lf = jnp.minimum(f, 0.0) - jnp.log1p(jnp.exp(-jnp.abs(f)))
    logf_ref[...] = lf[:, 0:FOX_F_W]

    @pl.when((i * tm) % seq_len == 0)
    def _():
        carry_ref[...] = jnp.zeros_like(carry_ref)

    r_io = lax.broadcasted_iota(jnp.int32, (tm, tm), 0)
    c_io = lax.broadcasted_iota(jnp.int32, (tm, tm), 1)
    tri = jnp.where(c_io <= r_io, 1.0, 0.0).astype(bf16)
    csum = carry_ref[...] + sum(jnp.dot(tri, p, preferred_element_type=f32) for p in _split3(lf))
    carry_ref[...] = csum[tm - 1:tm, :]
    pieces = jnp.concatenate(_split3(-csum), axis=1)
    cfeat = jnp.dot(pieces, place_ref[...], preferred_element_type=f32)
    for j in range(FOX_HEADS // 2):
        for s, half in enumerate(_expand_halves(kbn[:, j * LANE:(j + 1) * LANE])):
            hd = 2 * j + s
            kb_ref[:, hd * LANE:(hd + 1) * LANE] = (half + cfeat[:, hd * LANE:(hd + 1) * LANE]).astype(bf16)


def _fox_feature_placement():
    r = lax.broadcasted_iota(jnp.int32, (3 * LANE, FOX_HEADS * LANE), 0)
    c = lax.broadcasted_iota(jnp.int32, (3 * LANE, FOX_HEADS * LANE), 1)
    s, hd = r // LANE, r % LANE
    return jnp.where((hd < FOX_HEADS) & (c == hd * LANE + FEAT + s), 1.0, 0.0).astype(jnp.bfloat16)


def _front_attn(x2d, seq_len, norm_attn, w_front, bd, fox_f_bias, nsa_q_norm, nsa_k_norm, fox_q_norm,
                fox_k_norm, tm=256):
    n = x2d.shape[0]
    f32, bf16 = jnp.float32, jnp.bfloat16
    two = lambda g: jnp.concatenate([g, g]).reshape(1, LANE)
    row = lambda w: pl.BlockSpec((tm, w), lambda i: (i, 0))
    col = lambda h: pl.BlockSpec((h, tm), lambda i: (0, i))
    full = lambda a: pl.BlockSpec(a.shape, lambda i: (0,) * a.ndim)
    fb = jnp.pad(fox_f_bias, (0, LANE - FOX_F_W)).reshape(1, LANE)
    args = (x2d, norm_attn.reshape(1, D_MODEL), w_front, bd, fb,
            two(nsa_q_norm), two(nsa_k_norm[1]), two(nsa_k_norm[2]), two(fox_q_norm), two(fox_k_norm),
            _fox_feature_placement())
    outs = [(row(512), (n, 512), f32), (row(256), (n, 256), f32), (row(2 * FOX_W), (n, 2 * FOX_W), f32),
            (row(FOX_F_W), (n, FOX_F_W), f32), (row(MERGE_W), (n, MERGE_W), f32),
            (col(NSA_HEADS * LANE), (NSA_HEADS * LANE, n), bf16), (col(LANE), (LANE, n), f32),
            (row(2 * LANE), (n, 2 * LANE), bf16), (row(2 * LANE), (n, 2 * LANE), bf16),
            (col(LANE), (LANE, n), bf16), (col(LANE), (LANE, n), bf16),
            (col(FOX_HEADS * LANE), (FOX_HEADS * LANE, n), bf16),
            (row(FOX_HEADS * LANE), (n, FOX_HEADS * LANE), bf16), (col(FOX_W), (FOX_W, n), bf16)]
    return pl.pallas_call(
        functools.partial(_front_attn_kernel, seq_len),
        grid=(n // tm,),
        in_specs=[row(D_MODEL)] + [full(a) for a in args[1:]],
        out_specs=[o[0] for o in outs],
        out_shape=[jax.ShapeDtypeStruct(o[1], o[2]) for o in outs],
        scratch_shapes=[pltpu.VMEM((1, LANE), f32)],
        compiler_params=pltpu.CompilerParams(dimension_semantics=("arbitrary",),
                                             vmem_limit_bytes=VMEM_LIMIT),
        name="front_attn",
    )(*args)


def _front_weights(w_in):
    pad = lambda w: jnp.pad(w, ((0, 0), (0, LANE - w.shape[1])))
    parts = [w_in[:, :SPLIT_KV_A], w_in[:, SPLIT_G_A:SPLIT_QKV_B], w_in[:, SPLIT_F_B:],
             pad(w_in[:, SPLIT_KV_A:SPLIT_G_A]), pad(w_in[:, SPLIT_QKV_B:SPLIT_F_B])]
    return jnp.concatenate(parts, axis=1).astype(jnp.bfloat16)


def _block_diag_mean():
    r = lax.broadcasted_iota(jnp.int32, (LANE, LANE), 0) // HEAD_DIM
    c = lax.broadcasted_iota(jnp.int32, (LANE, LANE), 1) // HEAD_DIM
    return jnp.where(r == c, 1.0 / HEAD_DIM, 0.0).astype(jnp.bfloat16)


def _softmax_step(tiles, carry):
    m, l, acc = carry
    m_new = functools.reduce(jnp.maximum, [jnp.max(s, axis=0, keepdims=True) for s, _ in tiles], m)
    alpha = jnp.exp(m - m_new)
    ps = [jnp.exp(s - m_new) for s, _ in tiles]
    l = alpha * l + sum(jnp.sum(p, axis=0, keepdims=True) for p in ps)
    acc = alpha * acc + sum(jnp.dot(vt, p.astype(jnp.bfloat16), preferred_element_type=jnp.float32)
                            for p, (_, vt) in zip(ps, tiles))
    return m_new, l, acc


TILE_UNROLL = 4


def _tile_loop(n, logits, values, carry, bufs, tk, last_group, unroll=TILE_UNROLL):
    tile = lambda kt: (logits(kt), values(kt))

    def fill(buf, grp):
        grp = jnp.minimum(grp, last_group)
        for j in range(unroll):
            buf[j * tk:(j + 1) * tk, :] = logits(grp * unroll + j)

    def drain(buf, grp, c):
        return _softmax_step([(buf[j * tk:(j + 1) * tk, :], values(grp * unroll + j)) for j in range(unroll)], c)

    def body(i, c):
        fill(bufs[1], 2 * i + 1)
        c = drain(bufs[0], 2 * i, c)
        fill(bufs[0], 2 * i + 2)
        return drain(bufs[1], 2 * i + 1, c)

    nu = n // unroll
    fill(bufs[0], 0)
    carry = lax.fori_loop(0, nu // 2, body, carry)
    carry = lax.cond(nu % 2 == 1, lambda c: drain(bufs[0], nu - 1, c), lambda c: c, carry)
    base = nu * unroll
    size = unroll // 2
    while size >= 1:
        has = (n & size) != 0
        carry = lax.cond(has, functools.partial(
            lambda b, sz, c: _softmax_step([tile(b + u) for u in range(sz)], c), base, size),
            lambda c: c, carry)
        base = base + jnp.where(has, size, 0)
        size //= 2
    return carry


def _softmax_init(w):
    return (jnp.full((1, w), NEG_INF, jnp.float32), jnp.zeros((1, w), jnp.float32),
            jnp.zeros((HEAD_DIM, w), jnp.float32))


def _summarize(x_ref, s, pe_ref, w1_ref, w2_ref, nb):
    f32, bf16 = jnp.float32, jnp.bfloat16

    def body(l, acc):
        xl = x_ref[pl.ds(l, nb, stride=CMP_BLOCK), :] + pe_ref[s, pl.ds(l, 1), :]
        return acc + jnp.dot(xl.astype(bf16), w1_ref[s, l], preferred_element_type=f32)

    hid = jax.nn.gelu(lax.fori_loop(0, CMP_BLOCK, body, jnp.zeros((nb, LANE), f32), unroll=8))
    return jnp.dot(hid.astype(bf16), w2_ref[s], preferred_element_type=f32)


def _compress_kernel(xk_ref, xv_ref, pe_ref, w1_ref, w2_ref, gk_ref, bd_ref, kc_ref, vct_ref):
    f32, bf16 = jnp.float32, jnp.bfloat16
    nb = kc_ref.shape[0]
    kc = _summarize(xk_ref, 0, pe_ref, w1_ref, w2_ref, nb)
    kc = kc * lax.rsqrt(_group_mean_sq(kc, bd_ref[...]) + EPS) * gk_ref[...]
    lane = _lane(nb)
    blk = lax.broadcasted_iota(jnp.int32, (nb, LANE), 0).astype(f32)
    feat = jnp.where(lane == FEAT, blk, jnp.where(lane == FEAT + 1, float(CMP_BLOCK - 1), 0.0))
    for g, half in enumerate(_expand_halves(kc)):
        kc_ref[:, g * LANE:(g + 1) * LANE] = (half + feat).astype(bf16)
    vct_ref[...] = _summarize(xv_ref, 1, pe_ref, w1_ref, w2_ref, nb).T.astype(bf16)


def _compress_weights(cmp_pe, cmp_w1, cmp_w2):
    def bdiag(w):
        z = jnp.zeros_like(w)
        return jnp.concatenate([jnp.concatenate([w, z], axis=-1), jnp.concatenate([z, w], axis=-1)], axis=-2)
    pe = jnp.concatenate([cmp_pe, cmp_pe], axis=-1)
    return pe, bdiag(cmp_w1).astype(jnp.bfloat16), bdiag(cmp_w2).astype(jnp.bfloat16)


def _compress(rows2d, nb, pe, w1, w2, gk, bd):
    nseq = rows2d.shape[0] // (nb * CMP_BLOCK)
    full = lambda a: pl.BlockSpec(a.shape, lambda b: (0,) * a.ndim)
    args = (rows2d, rows2d, pe, w1, w2, jnp.concatenate([gk, gk]).reshape(1, LANE), bd)
    return pl.pallas_call(
        _compress_kernel,
        grid=(nseq,),
        in_specs=[pl.BlockSpec((nb * CMP_BLOCK, LANE), lambda b: (b, 0)),
                  pl.BlockSpec((nb * CMP_BLOCK, LANE), lambda b: (b, 1))] + [full(a) for a in args[2:]],
        out_specs=[pl.BlockSpec((nb, 2 * LANE), lambda b: (b, 0)), pl.BlockSpec((LANE, nb), lambda b: (0, b))],
        out_shape=[jax.ShapeDtypeStruct((nseq * nb, 2 * LANE), jnp.bfloat16),
                   jax.ShapeDtypeStruct((LANE, nseq * nb), jnp.bfloat16)],
        compiler_params=pltpu.CompilerParams(dimension_semantics=("arbitrary",),
                                             vmem_limit_bytes=VMEM_LIMIT),
        name="nsa_compress",
    )(*args)


def _nsa_prompt_kernel(qt_ref, gat_ref, ksel_ref, vselt_ref, kwin_ref, vwint_ref, kc_ref, vct_ref,
                       o_ref, selb_ref, buf0_ref, buf1_ref):
    f32, bf16 = jnp.float32, jnp.bfloat16
    tq = Q_BLOCK
    w = NSA_GROUP * tq
    g = pl.program_id(1)
    qi = pl.program_id(2)
    nb = kc_ref.shape[0]
    qt = jnp.concatenate([qt_ref[r * LANE:(r + 1) * LANE, :] for r in range(NSA_GROUP)], axis=1)
    qloc = lax.broadcasted_iota(jnp.int32, (1, w), 1) % tq
    qpos = qi * tq + qloc

    sc = jnp.dot(kc_ref[...], qt, preferred_element_type=f32)
    blk = lax.broadcasted_iota(jnp.int32, (nb, w), 0)
    vis = blk * CMP_BLOCK + (CMP_BLOCK - 1) <= qpos
    sc = jnp.where(vis, sc, NEG_INF)
    pc = jnp.where(vis, jnp.exp(sc - jnp.max(sc, axis=0, keepdims=True)), 0.0)
    pc = pc / jnp.maximum(jnp.sum(pc, axis=0, keepdims=True), 1e-30)
    o_c = jnp.dot(vct_ref[...], pc.astype(bf16), preferred_element_type=f32)

    imp = sum(pc[:, r * tq:(r + 1) * tq] for r in range(NSA_GROUP))
    blk1 = blk[:, 0:tq]
    cur = qpos[:, 0:tq] // SEL_BLOCK
    imp = jnp.where((blk1 == cur) | (blk1 == 0), FORCE_SCORE, jnp.where(blk1 <= cur, imp, -1.0))
    for _ in range(min(SEL_TOPK, nb)):
        mx = jnp.max(imp, axis=0, keepdims=True)
        idx = jnp.min(jnp.where(imp == mx, blk1, BIG_ID), axis=0, keepdims=True)
        imp = jnp.where(blk1 == idx, -jnp.inf, imp)
    selb = jnp.where(imp == -jnp.inf, 0.0, NEG_INF)
    selb_ref[...] = jnp.concatenate([selb] * NSA_GROUP, axis=1)

    krow = lax.broadcasted_iota(jnp.int32, (tq, w), 0)
    causal = krow <= qloc

    def sel_bias(kt):
        per = tq // SEL_BLOCK
        return jnp.concatenate([jnp.broadcast_to(selb_ref[pl.ds(kt * per + j, 1), :], (SEL_BLOCK, w))
                                for j in range(per)], axis=0)

    def sel_logits(kt):
        k0 = pl.multiple_of(kt * tq, tq)
        return jnp.dot(ksel_ref[pl.ds(k0, tq), :], qt, preferred_element_type=f32) + sel_bias(kt)

    sel_values = lambda kt: vselt_ref[:, pl.ds(pl.multiple_of(kt * tq, tq), tq)]
    last_group = ksel_ref.shape[0] // (tq * TILE_UNROLL) - 1
    carry = _tile_loop(qi, sel_logits, sel_values, _softmax_init(w), (buf0_ref, buf1_ref), tq, last_group)
    m_s, l_s, acc_s = _softmax_step([(jnp.where(causal, sel_logits(qi), NEG_INF), sel_values(qi))], carry)
    o_s = acc_s / l_s

    nwin = WINDOW // tq
    tiles = []
    for j in range(nwin + 1):
        kt = qi - nwin + j
        k0 = pl.multiple_of(jnp.maximum(kt, 0) * tq, tq)
        s = jnp.dot(kwin_ref[pl.ds(k0, tq), :], qt, preferred_element_type=f32)
        ok = kt >= 0
        if j == 0:
            ok = ok & (krow > qloc)
        elif j == nwin:
            ok = ok & causal
        tiles.append((jnp.where(ok, s, NEG_INF), vwint_ref[:, pl.ds(k0, tq)]))
    m_w, l_w, acc_w = _softmax_step(tiles, _softmax_init(w))
    o_w = acc_w / l_w

    for r in range(NSA_GROUP):
        gate = lambda br: gat_ref[pl.ds((g * NSA_GROUP + r) * 3 + br, 1), :]
        sl = slice(r * tq, (r + 1) * tq)
        o_ref[r * HEAD_DIM:(r + 1) * HEAD_DIM, :] = (
            gate(0) * o_c[:, sl] + gate(1) * o_s[:, sl] + gate(2) * o_w[:, sl])


def _nsa_prompt(qat, gat, ksel, vselt, kwin, vwint, kc, vct, nseq, seq_len):
    n = qat.shape[1]
    nq = seq_len // Q_BLOCK
    nb = seq_len // CMP_BLOCK
    gw = NSA_GROUP * LANE
    return pl.pallas_call(
        _nsa_prompt_kernel,
        grid=(nseq, NSA_KV_HEADS, nq),
        in_specs=[pl.BlockSpec((gw, Q_BLOCK), lambda b, g, q: (g, b * nq + q)),
                  pl.BlockSpec((LANE, Q_BLOCK), lambda b, g, q: (0, b * nq + q)),
                  pl.BlockSpec((seq_len, LANE), lambda b, g, q: (b, g)),
                  pl.BlockSpec((HEAD_DIM, seq_len), lambda b, g, q: (g, b)),
                  pl.BlockSpec((seq_len, LANE), lambda b, g, q: (b, g)),
                  pl.BlockSpec((HEAD_DIM, seq_len), lambda b, g, q: (g, b)),
                  pl.BlockSpec((nb, LANE), lambda b, g, q: (b, g)),
                  pl.BlockSpec((HEAD_DIM, nb), lambda b, g, q: (g, b))],
        out_specs=pl.BlockSpec((NSA_GROUP * HEAD_DIM, Q_BLOCK), lambda b, g, q: (g, b * nq + q)),
        out_shape=jax.ShapeDtypeStruct((NSA_Q_W, n), jnp.float32),
        scratch_shapes=[pltpu.VMEM((nb, NSA_GROUP * Q_BLOCK), jnp.float32)]
        + [pltpu.VMEM((TILE_UNROLL * Q_BLOCK, NSA_GROUP * Q_BLOCK), jnp.float32)] * 2,
        compiler_params=pltpu.CompilerParams(dimension_semantics=("arbitrary", "arbitrary", "arbitrary"),
                                             vmem_limit_bytes=VMEM_LIMIT),
        name="nsa_prompt",
    )(qat, gat, ksel, vselt, kwin, vwint, kc, vct)


FOX_TQ = 512
FOX_TK = 128


def _fox_prompt_kernel(qt_ref, kb_ref, vbt_ref, o_ref, buf0_ref, buf1_ref):
    f32 = jnp.float32
    tq, tk = FOX_TQ, FOX_TK
    qi = pl.program_id(2)
    qt = qt_ref[...]

    logits = lambda kt: jnp.dot(kb_ref[pl.ds(pl.multiple_of(kt * tk, tk), tk), :], qt, preferred_element_type=f32)
    values = lambda kt: vbt_ref[:, pl.ds(pl.multiple_of(kt * tk, tk), tk)]

    ndiag = tq // tk
    last_group = kb_ref.shape[0] // (tk * TILE_UNROLL) - 1
    carry = _tile_loop(qi * ndiag, logits, values, _softmax_init(tq), (buf0_ref, buf1_ref), tk, last_group)
    krow = lax.broadcasted_iota(jnp.int32, (tk, tq), 0)
    qloc = lax.broadcasted_iota(jnp.int32, (tk, tq), 1)
    diag = []
    for j in range(ndiag):
        kt = qi * ndiag + j
        diag.append((jnp.where(krow + j * tk <= qloc, logits(kt), NEG_INF), values(kt)))
    m, l, acc = _softmax_step(diag, carry)
    o_ref[...] = acc / l


def _fox_prompt(qbt, kb, vbt, nseq, seq_len):
    n = qbt.shape[1]
    nq = seq_len // FOX_TQ
    return pl.pallas_call(
        _fox_prompt_kernel,
        grid=(nseq, FOX_HEADS, nq),
        in_specs=[pl.BlockSpec((LANE, FOX_TQ), lambda b, h, q: (h, b * nq + q)),
                  pl.BlockSpec((seq_len, LANE), lambda b, h, q: (b, h)),
                  pl.BlockSpec((HEAD_DIM, seq_len), lambda b, h, q: (h, b))],
        out_specs=pl.BlockSpec((HEAD_DIM, FOX_TQ), lambda b, h, q: (h, b * nq + q)),
        out_shape=jax.ShapeDtypeStruct((FOX_W, n), jnp.float32),
        scratch_shapes=[pltpu.VMEM((TILE_UNROLL * FOX_TK, FOX_TQ), jnp.float32)] * 2,
        compiler_params=pltpu.CompilerParams(dimension_semantics=("arbitrary", "arbitrary", "arbitrary"),
                                             vmem_limit_bytes=VMEM_LIMIT),
        name="fox_prompt",
    )(qbt, kb, vbt)


def _slope_rows(shape, rows_per_head):
    hd = lax.broadcasted_iota(jnp.int32, shape, 0) // rows_per_head
    return pltpu.bitcast((126 - hd) << 23, jnp.float32)


def _pad_rows(x, rows):
    return jnp.concatenate([x, jnp.zeros((rows - x.shape[0], x.shape[1]), x.dtype)], axis=0)


def _nt_dot(a, b):
    return lax.dot_general(a, b, (((1,), (1,)), ((), ())), preferred_element_type=jnp.float32)


def _joint_softmax(parts):
    m = functools.reduce(jnp.maximum, [jnp.max(p, axis=1, keepdims=True) for p in parts])
    es = [jnp.exp(p - m) for p in parts]
    return es, sum(jnp.sum(e, axis=1, keepdims=True) for e in es)


def _nsa_sample_kernel(npages, ns, pt_ref, *refs):
    f32, bf16 = jnp.float32, jnp.bfloat16
    pages = refs[:npages]
    (e_ref, qa_ref, ga_ref, rnew_ref, wnew_ref, state_ref, pe_ref, w1_ref, w2_ref, gk_ref, bd_ref,
     o_ref, bufk_ref, bufv_ref) = refs[npages:]
    page = pages[0].shape[-1]
    past = npages * page
    ncol = NSA_KV_HEADS * npages
    nrow = NSA_HEADS * ns

    for p in range(npages):
        bufk_ref[p * LANE:(p + 1) * LANE, :] = pages[p][0, 0].reshape(LANE, page)
        bufv_ref[p * LANE:(p + 1) * LANE, :] = pages[p][0, 1].reshape(LANE, page)
    kc = _summarize(bufk_ref, 0, pe_ref, w1_ref, w2_ref, ncol)
    kc = (kc * lax.rsqrt(_group_mean_sq(kc, bd_ref[...]) + EPS) * gk_ref[...]).astype(bf16)
    vc = _summarize(bufv_ref, 1, pe_ref, w1_ref, w2_ref, ncol).astype(bf16)

    lane8 = _lane(ns)
    ql = qa_ref[...]
    qrows = []
    for hd in range(NSA_HEADS):
        g = hd // NSA_GROUP
        t = ql[:, (hd // 2) * LANE:(hd // 2 + 1) * LANE]
        if hd % 2 != g:
            t = pltpu.roll(t, HEAD_DIM, axis=1)
        qrows.append(jnp.where(lane8 // HEAD_DIM == g, t, 0.0))
    qb = (jnp.concatenate(qrows, axis=0) * SCALE).astype(bf16)

    def geom(width):
        tok = lax.broadcasted_iota(jnp.int32, (nrow, width), 0) % ns
        col = lax.broadcasted_iota(jnp.int32, (nrow, width), 1)
        return tok, col, _slope_rows((nrow, width), ns)

    q_both = qb + pltpu.roll(qb.astype(f32), HEAD_DIM, axis=1).astype(bf16)
    lane64 = lax.broadcasted_iota(jnp.int32, (nrow, LANE), 1) // HEAD_DIM
    tok, col, slope = geom(ncol)
    own = (lax.broadcasted_iota(jnp.int32, (nrow, ncol), 0) // (NSA_GROUP * ns)) == col % NSA_KV_HEADS
    lcs, blks = [], []
    for half in range(2):
        blk = (col // NSA_KV_HEADS) * 2 + half
        lc = _nt_dot(jnp.where(lane64 == half, q_both, 0.0), kc)
        lc = lc - slope * (past + tok - (blk * CMP_BLOCK + CMP_BLOCK - 1)).astype(f32)
        lcs.append(jnp.where(own, lc, NEG_INF))
        blks.append(blk)
    ecs, lsum = _joint_softmax(lcs)
    pcs = [e / lsum for e in ecs]
    res = [jnp.dot(pc.astype(bf16), vc, preferred_element_type=f32) for pc in pcs]
    o_c = jnp.where(lane64 == 0, res[0], res[1])
    o_c = o_c + pltpu.roll(o_c, HEAD_DIM, axis=1)

    nsel = NSA_KV_HEADS * ns
    col_s = lax.broadcasted_iota(jnp.int32, (nsel, ncol), 1)
    own_s = (lax.broadcasted_iota(jnp.int32, (nsel, ncol), 0) // ns) == col_s % NSA_KV_HEADS
    blks_s = [(col_s // NSA_KV_HEADS) * 2 + half for half in range(2)]
    imps = []
    for half in range(2):
        imp = jnp.concatenate(
            [sum(pcs[half][(g * NSA_GROUP + r) * ns:(g * NSA_GROUP + r + 1) * ns] for r in range(NSA_GROUP))
             for g in range(NSA_KV_HEADS)], axis=0)
        imp = jnp.where(blks_s[half] == 0, FORCE_SCORE, imp)
        imps.append(jnp.where(own_s, imp, -1.0))
    ranks = [jnp.zeros((nsel, ncol), jnp.int32) for _ in range(2)]
    for h2 in range(2):
        for c in range(ncol):
            other = jnp.broadcast_to(imps[h2][:, c:c + 1], (nsel, ncol))
            blk_c = (c // NSA_KV_HEADS) * 2 + h2
            for half in range(2):
                ahead = (other > imps[half]) | ((other == imps[half]) & (blks_s[half] > blk_c))
                ranks[half] = ranks[half] + jnp.where(ahead, 1, 0)
    selexp = 0.0
    for half in range(2):
        sel = jnp.where((ranks[half] < SEL_TOPK - 1) & own_s, 1.0, 0.0)
        sel = jnp.concatenate([sel[g * ns:(g + 1) * ns] for g in range(NSA_KV_HEADS) for _ in range(NSA_GROUP)],
                              axis=0)
        selexp = selexp + jnp.dot(sel.astype(bf16), e_ref[half], preferred_element_type=f32)

    def new_tile(k_new):
        tok, col, slope = geom(LANE)
        s = _nt_dot(qb, _pad_rows(k_new, LANE).astype(bf16))
        return jnp.where(col <= tok, s - slope * (tok - col).astype(f32), NEG_INF)

    def weighted(es, vts, e_new, v_new, lsum):
        acc = sum(_nt_dot(e.astype(bf16), vt.astype(bf16)) for e, vt in zip(es, vts))
        acc = acc + jnp.dot(e_new.astype(bf16), _pad_rows(v_new, LANE).astype(bf16), preferred_element_type=f32)
        return acc / lsum

    tok, col, slope = geom(past)
    ls = jnp.concatenate([jnp.dot(qb, pages[p][0, 2].reshape(LANE, page).astype(bf16), preferred_element_type=f32)
                          for p in range(npages)], axis=1)
    ls = jnp.where(selexp > 0.5, ls - slope * (past + tok - col).astype(f32), NEG_INF)
    (es, en), lsum = _joint_softmax([ls, new_tile(rnew_ref[:, 2 * LANE:3 * LANE])])
    o_s = weighted([es[:, p * page:(p + 1) * page] for p in range(npages)],
                   [pages[p][0, 3].reshape(LANE, page) for p in range(npages)],
                   en, rnew_ref[:, 3 * LANE:4 * LANE], lsum)

    wbuf = state_ref.shape[-1]
    tok, col, slope = geom(wbuf)
    lw = jnp.dot(qb, state_ref[0, 0].reshape(LANE, wbuf).astype(bf16), preferred_element_type=f32)
    lw = jnp.where(col > tok + (wbuf - WINDOW), lw - slope * (wbuf + tok - col).astype(f32), NEG_INF)
    (ew, en), lsum = _joint_softmax([lw, new_tile(wnew_ref[:, 0:LANE])])
    o_w = weighted([ew], [state_ref[0, 1].reshape(LANE, wbuf)], en, wnew_ref[:, LANE:2 * LANE], lsum)

    ga = ga_ref[...]
    gate = lambda br: jnp.concatenate(
        [jnp.broadcast_to(ga[:, hd * 3 + br:hd * 3 + br + 1], (ns, LANE)) for hd in range(NSA_HEADS)], axis=0)
    o = gate(0) * o_c + gate(1) * o_s + gate(2) * o_w
    for j in range(NSA_HEADS // 2):
        g = (2 * j) // NSA_GROUP
        a = o[2 * j * ns:(2 * j + 1) * ns]
        b = o[(2 * j + 1) * ns:(2 * j + 2) * ns]
        if g == 0:
            b = pltpu.roll(b, HEAD_DIM, axis=1)
        else:
            a = pltpu.roll(a, HEAD_DIM, axis=1)
        o_ref[:, j * LANE:(j + 1) * LANE] = jnp.where(lane8 < HEAD_DIM, a, b)


def _compress_weights_t(cmp_pe, cmp_w1, cmp_w2):
    pe, w1, w2 = _compress_weights(cmp_pe.transpose(0, 2, 1), cmp_w1.transpose(0, 2, 1, 3), cmp_w2)
    return pe, w1, w2


def _nsa_sample(page_table, cache_t, qa, ga, rows_new, win_new, state_t, pe, w1, w2, gk, bd):
    db, npages = page_table.shape
    page = cache_t.shape[-1]
    assert page == 2 * CMP_BLOCK == LANE
    ns = qa.shape[0] // db
    past = npages * page
    ncol = NSA_KV_HEADS * npages
    half = lax.broadcasted_iota(jnp.int32, (2, ncol, past), 0)
    col = lax.broadcasted_iota(jnp.int32, (2, ncol, past), 1)
    key = lax.broadcasted_iota(jnp.int32, (2, ncol, past), 2)
    expand = jnp.where(key // SEL_BLOCK == (col // NSA_KV_HEADS) * 2 + half, 1.0, 0.0).astype(jnp.bfloat16)
    gk2 = jnp.concatenate([gk, gk]).reshape(1, LANE)
    full = lambda a: pl.BlockSpec(a.shape, lambda b, pt: (0,) * a.ndim)
    tok = lambda w: pl.BlockSpec((ns, w), lambda b, pt: (b, 0))
    page_specs = [pl.BlockSpec((1,) + cache_t.shape[1:], functools.partial(lambda p, b, pt: (pt[b, p], 0, 0, 0, 0), p))
                  for p in range(npages)]
    return pl.pallas_call(
        functools.partial(_nsa_sample_kernel, npages, ns),
        grid_spec=pltpu.PrefetchScalarGridSpec(
            num_scalar_prefetch=1, grid=(db,),
            in_specs=page_specs + [full(expand), tok(NSA_Q_W), tok(LANE), tok(512), tok(256),
                                   pl.BlockSpec((1,) + state_t.shape[1:], lambda b, pt: (b, 0, 0, 0, 0)),
                                   full(pe), full(w1), full(w2), full(gk2), full(bd)],
            out_specs=tok(NSA_Q_W),
            scratch_shapes=[pltpu.VMEM((npages * LANE, page), jnp.float32),
                            pltpu.VMEM((npages * LANE, page), jnp.float32)]),
        out_shape=jax.ShapeDtypeStruct((db * ns, NSA_Q_W), jnp.float32),
        compiler_params=pltpu.CompilerParams(dimension_semantics=("arbitrary",),
                                             vmem_limit_bytes=VMEM_LIMIT),
        name="nsa_sample",
    )(page_table, *([cache_t] * npages), expand, qa, ga, rows_new, win_new, state_t, pe, w1, w2, gk2, bd)


def _fox_decode_kernel(npages, ns, pt_ref, *refs):
    f32, bf16 = jnp.float32, jnp.bfloat16
    kvt = refs[:npages]
    lft = refs[npages:2 * npages]
    qb_ref, knew_ref, lfnew_ref, o_ref = refs[2 * npages:]
    page = kvt[0].shape[-1]
    nh = FOX_HEADS
    nrow = nh * ns

    r_io = lax.broadcasted_iota(jnp.int32, (page, page), 0)
    c_io = lax.broadcasted_iota(jnp.int32, (page, page), 1)
    triu = jnp.where(r_io <= c_io, 1.0, 0.0).astype(bf16)
    carry = jnp.zeros((nh, 1), f32)
    negc = []
    for t in range(npages + 1):
        lf = lft[t][0] if t < npages else lfnew_ref[0]
        ct = carry + sum(jnp.dot(pc_, triu, preferred_element_type=f32) for pc_ in _split3(lf))
        carry = ct[:, page - 1:page]
        negc.append(jnp.concatenate([jnp.broadcast_to(-ct[hd:hd + 1], (ns, page)) for hd in range(nh)], axis=0))

    head_of_lane = lax.broadcasted_iota(jnp.int32, (ns, FOX_W), 1) // HEAD_DIM
    q = qb_ref[...] * SCALE
    q_bd = jnp.concatenate([jnp.where(head_of_lane == hd, q, 0.0) for hd in range(nh)], axis=0).astype(bf16)

    parts = [jnp.dot(q_bd, kvt[p][0, 0].reshape(FOX_W, page).astype(bf16), preferred_element_type=f32) + negc[p]
             for p in range(npages)]
    tok = lax.broadcasted_iota(jnp.int32, (nrow, page), 0) % ns
    col = lax.broadcasted_iota(jnp.int32, (nrow, page), 1)
    s_new = _nt_dot(q_bd, _pad_rows(knew_ref[:, 0:FOX_W], page).astype(bf16)) + negc[npages]
    parts.append(jnp.where(col <= tok, s_new, NEG_INF))
    es, lsum = _joint_softmax(parts)
    acc = sum(_nt_dot(es[p].astype(bf16), kvt[p][0, 1].reshape(FOX_W, page).astype(bf16)) for p in range(npages))
    acc = acc + jnp.dot(es[npages].astype(bf16), _pad_rows(knew_ref[:, FOX_W:2 * FOX_W], page).astype(bf16),
                        preferred_element_type=f32)
    acc = acc / lsum
    o_ref[...] = sum(jnp.where(head_of_lane == hd, acc[hd * ns:(hd + 1) * ns], 0.0) for hd in range(nh))


def _fox_decode(page_table, cache_kvt, lft, qb, fox_new, lft_new):
    db, npages = page_table.shape
    ns = qb.shape[0] // db
    tok = lambda w: pl.BlockSpec((ns, w), lambda b, pt: (b, 0))
    pg = lambda a: [pl.BlockSpec((1,) + a.shape[1:],
                                 functools.partial(lambda p, nd, b, pt: (pt[b, p],) + (0,) * nd, p, a.ndim - 1))
                    for p in range(npages)]
    return pl.pallas_call(
        functools.partial(_fox_decode_kernel, npages, ns),
        grid_spec=pltpu.PrefetchScalarGridSpec(
            num_scalar_prefetch=1, grid=(db,),
            in_specs=pg(cache_kvt) + pg(lft) + [tok(FOX_W), tok(2 * FOX_W),
                                                pl.BlockSpec((1,) + lft_new.shape[1:], lambda b, pt: (b, 0, 0))],
            out_specs=tok(FOX_W)),
        out_shape=jax.ShapeDtypeStruct((db * ns, FOX_W), jnp.float32),
        compiler_params=pltpu.CompilerParams(dimension_semantics=("arbitrary",),
                                             vmem_limit_bytes=VMEM_LIMIT),
        name="fox_decode",
    )(page_table, *([cache_kvt] * npages), *([lft] * npages), qb, fox_new, lft_new)


BIG_ID = 1 << 20


def _topk_rows(s, k, ids):
    ids = ids.astype(jnp.float32)
    vals, idxs = [], []
    for _ in range(k):
        m = jnp.max(s, axis=0, keepdims=True)
        idx = jnp.min(jnp.where(s == m, ids, float(BIG_ID)), axis=0, keepdims=True)
        vals.append(m)
        idxs.append(idx)
        s = jnp.where(ids == idx, -jnp.inf, s)
    return jnp.concatenate(vals, axis=0), jnp.concatenate(idxs, axis=0).astype(jnp.int32)


def _pick_rows(sel, table):
    out = jnp.zeros(sel.shape, table.dtype)
    for r in range(table.shape[0]):
        out = jnp.where(sel == r, table[r:r + 1, :], out)
    return out


def _merge_route_kernel(x_ref, on_ref, of_ref, mg_ref, wun_ref, wuf_ref, wo_ref, nf_ref, wqt_ref, sk_ref,
                        x1_ref, h2_ref, i1_ref, i2_ref, g_ref):
    tm = x_ref.shape[0]
    f32, bf16 = jnp.float32, jnp.bfloat16
    tdot = lambda ot, wgt: lax.dot_general(ot.astype(bf16), wgt, (((0,), (0,)), ((), ())),
                                           preferred_element_type=f32)
    a = tdot(on_ref[...], wun_ref[...])
    b = tdot(of_ref[...], wuf_ref[...])
    mixed = mg_ref[:, 0:D_MODEL] * a + mg_ref[:, D_MODEL:2 * D_MODEL] * b
    x1 = x_ref[...] + jnp.dot(mixed.astype(bf16), wo_ref[...], preferred_element_type=f32)
    x1_ref[...] = x1
    h2 = x1 * lax.rsqrt(jnp.mean(x1 * x1, axis=-1, keepdims=True) + EPS) * nf_ref[...]
    h2b = h2.astype(bf16)
    h2_ref[...] = h2b

    nk = PEER_N_KEYS
    key_ids = lax.broadcasted_iota(jnp.int32, (nk, tm), 0)
    io16 = lax.broadcasted_iota(jnp.int32, (PEER_TOPK, tm), 0)
    io8 = lax.broadcasted_iota(jnp.int32, (8, tm), 0)
    cand_ids = jnp.concatenate([io16] + [a_ * PEER_TOPK + io8 for a_ in range(1, 8)]
                               + [(io8 + 8) * PEER_TOPK], axis=0)
    for h in range(PEER_HEADS):
        sv, si = [], []
        for p in range(2):
            hp = 2 * h + p
            qt = lax.dot_general(wqt_ref[hp * PEER_DK_HALF:(hp + 1) * PEER_DK_HALF, :], h2b,
                                 (((1,), (1,)), ((), ())), preferred_element_type=f32)
            st = jnp.dot(sk_ref[hp], qt.astype(bf16), preferred_element_type=f32)
            v, i = _topk_rows(st, PEER_TOPK, key_ids)
            sv.append(v)
            si.append(i)
        s1, s2 = sv
        cand = jnp.concatenate([s1[0:1] + s2] + [s1[a_:a_ + 1] + s2[0:8] for a_ in range(1, 8)]
                               + [s1[8:16] + s2[0:1]], axis=0)
        top, fid = _topk_rows(cand, PEER_TOPK, cand_ids)
        e = jnp.exp(top - jnp.max(top, axis=0, keepdims=True))
        g = e / jnp.sum(e, axis=0, keepdims=True)
        sl = slice(h * PEER_TOPK, (h + 1) * PEER_TOPK)
        i1_ref[:, sl] = _pick_rows(fid >> 4, si[0]).T
        i2_ref[:, sl] = _pick_rows(fid & (PEER_TOPK - 1), si[1]).T
        g_ref[:, sl] = g.T


def _merge_route(x2d, o_nsa, o_fox, mg, w_up_nsa, w_up_fox, w_out, norm_ffn, wq_t, sub_keys, tm=256):
    n = x2d.shape[0]
    row = lambda w: pl.BlockSpec((tm, w), lambda i: (i, 0))
    col = lambda h: pl.BlockSpec((h, tm), lambda i: (0, i))
    full = lambda a: pl.BlockSpec(a.shape, lambda i: (0,) * a.ndim)
    args = (x2d, o_nsa, o_fox, mg, w_up_nsa, w_up_fox, w_out, norm_ffn.reshape(1, D_MODEL), wq_t, sub_keys)
    hk = PEER_HEADS * PEER_TOPK
    return pl.pallas_call(
        _merge_route_kernel,
        grid=(n // tm,),
        in_specs=[row(D_MODEL), col(NSA_Q_W), col(FOX_W), row(MERGE_W)] + [full(a) for a in args[4:]],
        out_specs=[row(D_MODEL), row(D_MODEL), row(hk), row(hk), row(hk)],
        out_shape=[jax.ShapeDtypeStruct((n, D_MODEL), jnp.float32),
                   jax.ShapeDtypeStruct((n, D_MODEL), jnp.bfloat16),
                   jax.ShapeDtypeStruct((n, hk), jnp.int32),
                   jax.ShapeDtypeStruct((n, hk), jnp.int32),
                   jax.ShapeDtypeStruct((n, hk), jnp.float32)],
        compiler_params=pltpu.CompilerParams(dimension_semantics=("arbitrary",),
                                             vmem_limit_bytes=VMEM_LIMIT),
        name="merge_route",
    )(*args)


def _peer_act_kernel(h2_ref, u_ref, i1_ref, i2_ref, act_ref):
    c = pl.program_id(1)
    ec = u_ref.shape[0]

    @pl.when(c == 0)
    def _():
        act_ref[...] = jnp.zeros_like(act_ref)

    a = lax.dot_general(h2_ref[...], u_ref[...], (((1,), (1,)), ((), ())),
                        preferred_element_type=jnp.float32)
    i1 = i1_ref[...]
    i2 = i2_ref[...]
    act = act_ref[...]
    for ii in range(ec // PEER_N_KEYS):
        got = jnp.take_along_axis(a[:, ii * PEER_N_KEYS:(ii + 1) * PEER_N_KEYS], i2, axis=1)
        act = jnp.where(i1 == c * (ec // PEER_N_KEYS) + ii, got, act)
    act_ref[...] = act


def _peer_act(h2b, u_b, i1, i2, tm=512, ec=2048):
    n = h2b.shape[0]
    hk = i1.shape[1]
    return pl.pallas_call(
        _peer_act_kernel,
        grid=(n // tm, u_b.shape[0] // ec),
        in_specs=[pl.BlockSpec((tm, D_MODEL), lambda t, c: (t, 0)),
                  pl.BlockSpec((ec, D_MODEL), lambda t, c: (c, 0)),
                  pl.BlockSpec((tm, hk), lambda t, c: (t, 0)),
                  pl.BlockSpec((tm, hk), lambda t, c: (t, 0))],
        out_specs=pl.BlockSpec((tm, hk), lambda t, c: (t, 0)),
        out_shape=jax.ShapeDtypeStruct((n, hk), jnp.float32),
        compiler_params=pltpu.CompilerParams(dimension_semantics=("arbitrary", "arbitrary"),
                                             vmem_limit_bytes=VMEM_LIMIT),
        name="peer_act",
    )(h2b, u_b, i1, i2)


def _peer_coef_kernel(act_ref, g_ref, i1_ref, i2_ref, c_ref, coef_ref):
    tm = act_ref.shape[0]
    nk = PEER_N_KEYS
    coef_ref[...] = g_ref[...] * jax.nn.gelu(act_ref[...])
    sub = lax.broadcasted_iota(jnp.int32, (nk, i1_ref.shape[1]), 0)

    def token(t):
        r1 = i1_ref[pl.ds(t, 1), :]
        r2 = i2_ref[pl.ds(t, 1), :]
        cf = coef_ref[pl.ds(t, 1), :]
        m1 = jnp.where(r1 == sub, cf, 0.0).astype(jnp.bfloat16)
        m2t = jnp.where(r2 == sub, 1.0, 0.0).astype(jnp.bfloat16)
        return lax.dot_general(m1, m2t, (((1,), (1,)), ((), ())), preferred_element_type=jnp.float32)

    def body(tg, carry):
        t0 = pl.multiple_of(tg * COEF_GROUP, COEF_GROUP)
        ct = jnp.stack([token(t0 + u) for u in range(COEF_GROUP)], axis=0)
        c_ref[:, pl.ds(t0, COEF_GROUP), :] = pltpu.einshape("tij->itj", ct).astype(c_ref.dtype)
        return carry

    lax.fori_loop(0, tm // COEF_GROUP, body, 0)


COEF_GROUP = 16


def _peer_coef(act, g, i1, i2, tm=128):
    n, hk = act.shape
    nk = PEER_N_KEYS
    row = pl.BlockSpec((tm, hk), lambda t: (t, 0))
    return pl.pallas_call(
        _peer_coef_kernel,
        grid=(n // tm,),
        in_specs=[row, row, row, row],
        out_specs=pl.BlockSpec((nk, tm, nk), lambda t: (0, t, 0)),
        out_shape=jax.ShapeDtypeStruct((nk, n, nk), jnp.bfloat16),
        scratch_shapes=[pltpu.VMEM((tm, hk), jnp.float32)],
        compiler_params=pltpu.CompilerParams(dimension_semantics=("arbitrary",),
                                             vmem_limit_bytes=VMEM_LIMIT),
        name="peer_coef",
    )(act, g, i1, i2)


def _peer_out_kernel(c_ref, v_ref, x1_ref, y_ref, acc_ref):
    k = pl.program_id(1)

    @pl.when(k == 0)
    def _():
        acc_ref[...] = x1_ref[...]

    acc = acc_ref[...]
    nk = PEER_N_KEYS
    for p in range(c_ref.shape[0] // 2):
        lhs = jnp.concatenate([c_ref[2 * p], c_ref[2 * p + 1]], axis=1)
        acc = acc + jnp.dot(lhs, v_ref[2 * p * nk:(2 * p + 2) * nk, :], preferred_element_type=jnp.float32)
    acc_ref[...] = acc

    @pl.when(k == pl.num_programs(1) - 1)
    def _():
        y_ref[...] = acc_ref[...]


def _peer_out(c3, v_b, x1, tm=1024, tk=2048):
    nk, n, _ = c3.shape
    ne = nk * nk
    return pl.pallas_call(
        _peer_out_kernel,
        grid=(n // tm, ne // tk),
        in_specs=[pl.BlockSpec((tk // nk, tm, nk), lambda t, k: (k, t, 0)),
                  pl.BlockSpec((tk, D_MODEL), lambda t, k: (k, 0)),
                  pl.BlockSpec((tm, D_MODEL), lambda t, k: (t, 0))],
        out_specs=pl.BlockSpec((tm, D_MODEL), lambda t, k: (t, 0)),
        out_shape=jax.ShapeDtypeStruct((n, D_MODEL), jnp.float32),
        scratch_shapes=[pltpu.VMEM((tm, D_MODEL), jnp.float32)],
        compiler_params=pltpu.CompilerParams(dimension_semantics=("arbitrary", "arbitrary"),
                                             vmem_limit_bytes=VMEM_LIMIT),
        name="peer_out",
    )(c3, v_b, x1)


def _peer_weights(w_up_nsa, w_up_fox, w_out, norm_ffn, peer_w_query, peer_sub_keys, peer_u, peer_v):
    bf16 = jnp.bfloat16
    return dict(w_up_nsa=w_up_nsa.astype(bf16), w_up_fox=w_up_fox.astype(bf16), w_out=w_out.astype(bf16),
                norm_ffn=norm_ffn, wq_t=peer_w_query.T.astype(bf16),
                sub_keys=peer_sub_keys.reshape(2 * PEER_HEADS, PEER_N_KEYS, PEER_DK_HALF).astype(bf16),
                u=peer_u.astype(bf16), v=peer_v.astype(bf16))


def _merge_peer(x2d, o_nsa, o_fox, mg, wts):
    x1, h2b, i1, i2, g = _merge_route(x2d, o_nsa, o_fox, mg, wts['w_up_nsa'], wts['w_up_fox'], wts['w_out'],
                                      wts['norm_ffn'], wts['wq_t'], wts['sub_keys'])
    act = _peer_act(h2b, wts['u'], i1, i2)
    c3 = _peer_coef(act, g, i1, i2)
    return _peer_out(c3, wts['v'], x1)


def kernel(x_prompt, x_sample, cache_nsa, cache_fox_kv, cache_fox_logf, state_nsa_win, page_table,
           norm_attn, w_in, fox_f_bias, nsa_q_norm, nsa_k_norm, fox_q_norm, fox_k_norm,
           cmp_pe, cmp_w1, cmp_w2, w_up_nsa, w_up_fox, w_out, norm_ffn,
           peer_w_query, peer_sub_keys, peer_u, peer_v):
    w_front = _front_weights(w_in)
    bd = _block_diag_mean()
    wts = _peer_weights(w_up_nsa, w_up_fox, w_out, norm_ffn, peer_w_query, peer_sub_keys, peer_u, peer_v)
    cmp_wts = _compress_weights(cmp_pe, cmp_w1, cmp_w2)

    bp, seq, _ = x_prompt.shape
    n_p = bp * seq
    (rows_p, win_p2d, fox_p2d, logf_p2d, mg_p, qat, gat, ksel, kwin, vselt, vwint, qbt, kb, vbt) = _front_attn(
        x_prompt.reshape(n_p, D_MODEL), seq, norm_attn, w_front, bd, fox_f_bias,
        nsa_q_norm, nsa_k_norm, fox_q_norm, fox_k_norm)
    kc, vct = _compress(rows_p, seq // CMP_BLOCK, *cmp_wts, nsa_k_norm[0], bd)
    o_nsa_t = _nsa_prompt(qat, gat, ksel, vselt, kwin, vwint, kc, vct, bp, seq)
    o_fox_t = _fox_prompt(qbt, kb, vbt, bp, seq)
    y_p = _merge_peer(x_prompt.reshape(n_p, D_MODEL), o_nsa_t, o_fox_t, mg_p, wts).reshape(x_prompt.shape)
    nsa_p = rows_p.reshape(bp, seq, 4, NSA_KV_HEADS, HEAD_DIM)
    fox_p = fox_p2d.reshape(bp, seq, 2, FOX_HEADS, HEAD_DIM)
    logf_p = logf_p2d.reshape(bp, seq, FOX_HEADS)
    win_p = win_p2d.reshape(bp, seq, 2, NSA_KV_HEADS, HEAD_DIM)[:, seq - min(WINDOW, seq):]

    db, ns, _ = x_sample.shape
    n_s = db * ns
    n_pool, page = cache_nsa.shape[:2]
    wbuf = state_nsa_win.shape[1]
    qa_s, rows_s, win_s2d, ga_s, qb_s, fox_s2d, logf_s2d, mg_s = _front(
        x_sample.reshape(n_s, D_MODEL), norm_attn, w_front, bd, fox_f_bias,
        nsa_q_norm, nsa_k_norm, fox_q_norm, fox_k_norm)
    o_nsa_s = _nsa_sample(page_table, cache_nsa.transpose(0, 2, 3, 4, 1), qa_s, ga_s, rows_s, win_s2d,
                          state_nsa_win.transpose(0, 2, 3, 4, 1),
                          *_compress_weights_t(cmp_pe, cmp_w1, cmp_w2), nsa_k_norm[0], bd)
    lft_new = jnp.pad(logf_s2d.reshape(db, ns, FOX_HEADS).transpose(0, 2, 1), ((0, 0), (0, 0), (0, page - ns)))
    o_fox_s = _fox_decode(page_table, cache_fox_kv.transpose(0, 2, 3, 4, 1), cache_fox_logf.transpose(0, 2, 1),
                          qb_s, fox_s2d, lft_new)
    y_s = _merge_peer(x_sample.reshape(n_s, D_MODEL), o_nsa_s.T, o_fox_s.T, mg_s, wts).reshape(x_sample.shape)
    nsa_s = rows_s.reshape(db, ns, 4, NSA_KV_HEADS, HEAD_DIM)
    fox_s = fox_s2d.reshape(db, ns, 2, FOX_HEADS, HEAD_DIM)
    logf_s = logf_s2d.reshape(db, ns, FOX_HEADS)
    win_s = jnp.concatenate([state_nsa_win[:, ns:], win_s2d.reshape(db, ns, 2, NSA_KV_HEADS, HEAD_DIM)], axis=1)
    return (y_p, y_s, nsa_p, fox_p, logf_p, win_p, nsa_s, fox_s, logf_s, win_s)
```

```python
import functools

import jax
import jax.numpy as jnp
from jax import lax
from jax.experimental import pallas as pl
from jax.experimental.pallas import tpu as pltpu

D_MODEL = 1024
HEAD_DIM = 64
NSA_HEADS = 8
NSA_KV_HEADS = 2
NSA_GROUP = NSA_HEADS // NSA_KV_HEADS
CMP_BLOCK = 64
SEL_BLOCK = CMP_BLOCK
SEL_TOPK = 16
WINDOW = 512
FOX_HEADS = 8
Q_BLOCK = 128
PEER_HEADS = 8
PEER_N_KEYS = 128
PEER_DK = 256
PEER_DK_HALF = PEER_DK // 2
PEER_TOPK = 16
PEER_CHUNK = 256

NSA_Q_W = NSA_HEADS * HEAD_DIM
NSA_KV_W = 6 * NSA_KV_HEADS * HEAD_DIM
NSA_GATE_W = 3 * NSA_HEADS
FOX_W = FOX_HEADS * HEAD_DIM
FOX_QKV_W = 3 * FOX_W
FOX_F_W = FOX_HEADS
MERGE_W = 2 * D_MODEL
SPLIT_Q_A = NSA_Q_W
SPLIT_KV_A = SPLIT_Q_A + NSA_KV_W
SPLIT_G_A = SPLIT_KV_A + NSA_GATE_W
SPLIT_QKV_B = SPLIT_G_A + FOX_QKV_W
SPLIT_F_B = SPLIT_QKV_B + FOX_F_W
IN_WIDTH = SPLIT_F_B + MERGE_W

SCALE = HEAD_DIM ** -0.5
FORCE_SCORE = float(NSA_GROUP + 1)
NEG_INF = -1e30
EPS = 1e-6

LANE = 128
VMEM_LIMIT = 48 * 1024 * 1024


def _group_mean_sq(x, bd):
    sq = x * x
    hi = sq.astype(jnp.bfloat16)
    lo = (sq - hi.astype(jnp.float32)).astype(jnp.bfloat16)
    return (jnp.dot(hi, bd, preferred_element_type=jnp.float32)
            + jnp.dot(lo, bd, preferred_element_type=jnp.float32))


def _head_rms(x, g, bd):
    outs = []
    for c in range(x.shape[1] // LANE):
        xc = x[:, c * LANE:(c + 1) * LANE]
        outs.append(xc * lax.rsqrt(_group_mean_sq(xc, bd) + EPS) * g)
    return outs[0] if len(outs) == 1 else jnp.concatenate(outs, axis=1)


def _front_kernel(x_ref, na_ref, w_ref, bd_ref, fb_ref, gq_a_ref, gk_sel_ref, gk_win_ref,
                  gq_b_ref, gk_b_ref,
                  qa_ref, rows_ref, win_ref, ga_ref, qb_ref, fox_ref, logf_ref, mg_ref):
    x = x_ref[...]
    h = x * lax.rsqrt(jnp.mean(x * x, axis=-1, keepdims=True) + EPS) * na_ref[...]
    hb = h.astype(jnp.bfloat16)
    bd = bd_ref[...]

    def proj(c0, width):
        return jnp.dot(hb, w_ref[:, c0:c0 + width], preferred_element_type=jnp.float32)

    c = 0
    qa_ref[...] = _head_rms(proj(c, NSA_Q_W), gq_a_ref[...], bd)
    c += NSA_Q_W
    rows_ref[:, 0:256] = proj(c, 256)
    rows_ref[:, 256:384] = _head_rms(proj(c + 256, 128), gk_sel_ref[...], bd)
    rows_ref[:, 384:512] = proj(c + 384, 128)
    win_ref[:, 0:128] = _head_rms(proj(c + 512, 128), gk_win_ref[...], bd)
    win_ref[:, 128:256] = proj(c + 640, 128)
    c += NSA_KV_W
    qb_ref[...] = _head_rms(proj(c, FOX_W), gq_b_ref[...], bd)
    fox_ref[:, 0:FOX_W] = _head_rms(proj(c + FOX_W, FOX_W), gk_b_ref[...], bd)
    fox_ref[:, FOX_W:2 * FOX_W] = proj(c + 2 * FOX_W, FOX_W)
    c += FOX_QKV_W
    for j in range(MERGE_W // 512):
        mg_ref[:, j * 512:(j + 1) * 512] = jax.nn.sigmoid(proj(c + j * 512, 512))
    c += MERGE_W
    ga_ref[...] = jax.nn.sigmoid(proj(c, LANE))
    f = proj(c + LANE, LANE)[:, 0:FOX_F_W] + fb_ref[...]
    logf_ref[...] = jnp.minimum(f, 0.0) - jnp.log1p(jnp.exp(-jnp.abs(f)))


def _front(x2d, norm_attn, w_front, bd, fox_f_bias, nsa_q_norm, nsa_k_norm, fox_q_norm, fox_k_norm,
           tm=256):
    n = x2d.shape[0]
    wf = w_front.shape[1]
    two = lambda g: jnp.concatenate([g, g]).reshape(1, LANE)
    row = lambda w: pl.BlockSpec((tm, w), lambda i: (i, 0))
    full = lambda a: pl.BlockSpec(a.shape, lambda i: (0,) * a.ndim)
    args = (x2d, norm_attn.reshape(1, D_MODEL), w_front, bd, fox_f_bias.reshape(1, FOX_F_W),
            two(nsa_q_norm), two(nsa_k_norm[1]), two(nsa_k_norm[2]), two(fox_q_norm), two(fox_k_norm))
    widths = (NSA_Q_W, 512, 256, LANE, FOX_W, 2 * FOX_W, FOX_F_W, MERGE_W)
    return pl.pallas_call(
        _front_kernel,
        grid=(n // tm,),
        in_specs=[row(D_MODEL)] + [full(a) for a in args[1:]],
        out_specs=[row(w) for w in widths],
        out_shape=[jax.ShapeDtypeStruct((n, w), jnp.float32) for w in widths],
        compiler_params=pltpu.CompilerParams(dimension_semantics=("arbitrary",),
                                             vmem_limit_bytes=VMEM_LIMIT),
        name="front",
    )(*args)


FEAT = HEAD_DIM


def _lane(tm):
    return lax.broadcasted_iota(jnp.int32, (tm, LANE), 1)


def _expand_halves(x):
    lo = _lane(x.shape[0]) < HEAD_DIM
    return jnp.where(lo, x, 0.0), jnp.where(lo, pltpu.roll(x, HEAD_DIM, axis=1), 0.0)


def _split3(x):
    hi = x.astype(jnp.bfloat16)
    r = x - hi.astype(jnp.float32)
    mid = r.astype(jnp.bfloat16)
    lo = (r - mid.astype(jnp.float32)).astype(jnp.bfloat16)
    return hi, mid, lo


def _front_attn_kernel(seq_len, x_ref, na_ref, w_ref, bd_ref, fb_ref, gq_a_ref, gk_sel_ref, gk_win_ref,
                       gq_b_ref, gk_b_ref, place_ref,
                       rows_ref, win_ref, fox_ref, logf_ref, mg_ref,
                       qat_ref, gat_ref, ksel_ref, kwin_ref, vselt_ref, vwint_ref, qbt_ref, kb_ref, vbt_ref, rcmp_ref,
                       carry_ref):
    f32, bf16 = jnp.float32, jnp.bfloat16
    tm = x_ref.shape[0]
    i = pl.program_id(0)
    x = x_ref[...]
    h = x * lax.rsqrt(jnp.mean(x * x, axis=-1, keepdims=True) + EPS) * na_ref[...]
    hb = h.astype(bf16)
    bd = bd_ref[...]
    lane = _lane(tm)
    pos = (i * tm + lax.broadcasted_iota(jnp.int32, (tm, LANE), 0)) % seq_len
    kfeat = jnp.where(lane == FEAT, (pos // SEL_BLOCK).astype(f32),
                      jnp.where(lane == FEAT + 1, (pos % SEL_BLOCK).astype(f32), 0.0))

    def proj(c0, width):
        return jnp.dot(hb, w_ref[:, c0:c0 + width], preferred_element_type=f32)

    c = 0
    qa = _head_rms(proj(c, NSA_Q_W), gq_a_ref[...], bd)
    for j in range(NSA_HEADS // 2):
        for s, half in enumerate(_expand_halves(qa[:, j * LANE:(j + 1) * LANE])):
            hd = 2 * j + s
            slope = 2.0 ** -(hd + 1)
            qfeat = jnp.where(lane == FEAT, slope * SEL_BLOCK, jnp.where(lane == FEAT + 1, slope, 0.0))
            qat_ref[hd * LANE:(hd + 1) * LANE, :] = (half * SCALE + qfeat).T.astype(bf16)
    c += NSA_Q_W
    for j in range(2):
        raw = proj(c + j * LANE, LANE)
        rcmp_ref[:, j * LANE:(j + 1) * LANE] = raw
        rows_ref[0, j * LANE:(j + 1) * LANE, :] = raw.T
    ksel = _head_rms(proj(c + 256, 128), gk_sel_ref[...], bd)
    rows_ref[0, 2 * LANE:3 * LANE, :] = ksel.T
    vsel_t = proj(c + 384, 128).T
    rows_ref[0, 3 * LANE:4 * LANE, :] = vsel_t
    kwin = _head_rms(proj(c + 512, 128), gk_win_ref[...], bd)
    win_ref[0, 0:LANE, :] = kwin.T
    vwin_t = proj(c + 640, 128).T
    win_ref[0, LANE:2 * LANE, :] = vwin_t
    for g, (ks_g, kw_g) in enumerate(zip(_expand_halves(ksel), _expand_halves(kwin))):
        ksel_ref[:, g * LANE:(g + 1) * LANE] = (ks_g + kfeat).astype(bf16)
        kwin_ref[:, g * LANE:(g + 1) * LANE] = (kw_g + kfeat).astype(bf16)
    vselt_ref[...] = vsel_t.astype(bf16)
    vwint_ref[...] = vwin_t.astype(bf16)
    c += NSA_KV_W

    qb = _head_rms(proj(c, FOX_W), gq_b_ref[...], bd)
    ones3 = jnp.where((lane >= FEAT) & (lane < FEAT + 3), 1.0, 0.0)
    for j in range(FOX_HEADS // 2):
        for s, half in enumerate(_expand_halves(qb[:, j * LANE:(j + 1) * LANE])):
            hd = 2 * j + s
            qbt_ref[hd * LANE:(hd + 1) * LANE, :] = (half * SCALE + ones3).T.astype(bf16)
    kbn = _head_rms(proj(c + FOX_W, FOX_W), gk_b_ref[...], bd)
    for j in range(FOX_HEADS // 2):
        fox_ref[0, j * LANE:(j + 1) * LANE, :] = kbn[:, j * LANE:(j + 1) * LANE].T
        vb_t = proj(c + 2 * FOX_W + j * LANE, LANE).T
        fox_ref[0, FOX_W + j * LANE:FOX_W + (j + 1) * LANE, :] = vb_t
        vbt_ref[j * LANE:(j + 1) * LANE, :] = vb_t.astype(bf16)
    c += FOX_QKV_W
    for j in range(MERGE_W // 512):
        mg_ref[:, j * 512:(j + 1) * 512] = jax.nn.sigmoid(proj(c + j * 512, 512))
    c += MERGE_W
    gat_ref[...] = jax.nn.sigmoid(proj(c, LANE)).T
    f = proj(c + LANE, LANE) + fb_ref[...]
    lf = jnp.minimum(f, 0.0) - jnp.log1p(jnp.exp(-jnp.abs(f)))
    logf_ref[...] = lf[:, 0:FOX_F_W]

    @pl.when((i * tm) % seq_len == 0)
    def _():
        carry_ref[...] = jnp.zeros_like(carry_ref)

    r_io = lax.broadcasted_iota(jnp.int32, (tm, tm), 0)
    c_io = lax.broadcasted_iota(jnp.int32, (tm, tm), 1)
    tri = jnp.where(c_io <= r_io, 1.0, 0.0).astype(bf16)
    csum = carry_ref[...] + sum(jnp.dot(tri, p, preferred_element_type=f32) for p in _split3(lf))
    carry_ref[...] = csum[tm - 1:tm, :]
    pieces = jnp.concatenate(_split3(-csum), axis=1)
    cfeat = jnp.dot(pieces, place_ref[...], preferred_element_type=f32)
    for j in range(FOX_HEADS // 2):
        for s, half in enumerate(_expand_halves(kbn[:, j * LANE:(j + 1) * LANE])):
            hd = 2 * j + s
            kb_ref[:, hd * LANE:(hd + 1) * LANE] = (half + cfeat[:, hd * LANE:(hd + 1) * LANE]).astype(bf16)


def _fox_feature_placement():
    r = lax.broadcasted_iota(jnp.int32, (3 * LANE, FOX_HEADS * LANE), 0)
    c = lax.broadcasted_iota(jnp.int32, (3 * LANE, FOX_HEADS * LANE), 1)
    s, hd = r // LANE, r % LANE
    return jnp.where((hd < FOX_HEADS) & (c == hd * LANE + FEAT + s), 1.0, 0.0).astype(jnp.bfloat16)


def _front_attn(x2d, seq_len, norm_attn, w_front, bd, fox_f_bias, nsa_q_norm, nsa_k_norm, fox_q_norm,
                fox_k_norm, tm=256):
    n = x2d.shape[0]
    f32, bf16 = jnp.float32, jnp.bfloat16
    two = lambda g: jnp.concatenate([g, g]).reshape(1, LANE)
    row = lambda w: pl.BlockSpec((tm, w), lambda i: (i, 0))
    col = lambda h: pl.BlockSpec((h, tm), lambda i: (0, i))
    full = lambda a: pl.BlockSpec(a.shape, lambda i: (0,) * a.ndim)
    fb = jnp.pad(fox_f_bias, (0, LANE - FOX_F_W)).reshape(1, LANE)
    args = (x2d, norm_attn.reshape(1, D_MODEL), w_front, bd, fb,
            two(nsa_q_norm), two(nsa_k_norm[1]), two(nsa_k_norm[2]), two(fox_q_norm), two(fox_k_norm),
            _fox_feature_placement())
    per_seq = seq_len // tm
    nseq = n // seq_len
    leaf = lambda h: (pl.BlockSpec((1, h, tm), lambda i: (i // per_seq, 0, i % per_seq)), (nseq, h, seq_len), f32)
    outs = [leaf(512), leaf(256), leaf(2 * FOX_W),
            (row(FOX_F_W), (n, FOX_F_W), f32), (row(MERGE_W), (n, MERGE_W), f32),
            (col(NSA_HEADS * LANE), (NSA_HEADS * LANE, n), bf16), (col(LANE), (LANE, n), f32),
            (row(2 * LANE), (n, 2 * LANE), bf16), (row(2 * LANE), (n, 2 * LANE), bf16),
            (col(LANE), (LANE, n), bf16), (col(LANE), (LANE, n), bf16),
            (col(FOX_HEADS * LANE), (FOX_HEADS * LANE, n), bf16),
            (row(FOX_HEADS * LANE), (n, FOX_HEADS * LANE), bf16), (col(FOX_W), (FOX_W, n), bf16),
            (row(2 * LANE), (n, 2 * LANE), f32)]
    return pl.pallas_call(
        functools.partial(_front_attn_kernel, seq_len),
        grid=(n // tm,),
        in_specs=[row(D_MODEL)] + [full(a) for a in args[1:]],
        out_specs=[o[0] for o in outs],
        out_shape=[jax.ShapeDtypeStruct(o[1], o[2]) for o in outs],
        scratch_shapes=[pltpu.VMEM((1, LANE), f32)],
        compiler_params=pltpu.CompilerParams(dimension_semantics=("arbitrary",),
                                             vmem_limit_bytes=VMEM_LIMIT),
        name="front_attn",
    )(*args)


def _front_weights(w_in):
    pad = lambda w: jnp.pad(w, ((0, 0), (0, LANE - w.shape[1])))
    parts = [w_in[:, :SPLIT_KV_A], w_in[:, SPLIT_G_A:SPLIT_QKV_B], w_in[:, SPLIT_F_B:],
             pad(w_in[:, SPLIT_KV_A:SPLIT_G_A]), pad(w_in[:, SPLIT_QKV_B:SPLIT_F_B])]
    return jnp.concatenate(parts, axis=1).astype(jnp.bfloat16)


def _block_diag_mean():
    r = lax.broadcasted_iota(jnp.int32, (LANE, LANE), 0) // HEAD_DIM
    c = lax.broadcasted_iota(jnp.int32, (LANE, LANE), 1) // HEAD_DIM
    return jnp.where(r == c, 1.0 / HEAD_DIM, 0.0).astype(jnp.bfloat16)


def _softmax_step(tiles, carry):
    m, l, acc = carry
    m_new = functools.reduce(jnp.maximum, [jnp.max(s, axis=0, keepdims=True) for s, _ in tiles], m)
    alpha = jnp.exp(m - m_new)
    ps = [jnp.exp(s - m_new) for s, _ in tiles]
    l = alpha * l + sum(jnp.sum(p, axis=0, keepdims=True) for p in ps)
    acc = alpha * acc + sum(jnp.dot(vt, p.astype(jnp.bfloat16), preferred_element_type=jnp.float32)
                            for p, (_, vt) in zip(ps, tiles))
    return m_new, l, acc


TILE_UNROLL = 4


def _tile_loop(n, logits, values, carry, bufs, tk, last_group, unroll=TILE_UNROLL):
    tile = lambda kt: (logits(kt), values(kt))

    def fill(buf, grp):
        grp = jnp.minimum(grp, last_group)
        for j in range(unroll):
            buf[j * tk:(j + 1) * tk, :] = logits(grp * unroll + j)

    def drain(buf, grp, c):
        return _softmax_step([(buf[j * tk:(j + 1) * tk, :], values(grp * unroll + j)) for j in range(unroll)], c)

    def body(i, c):
        fill(bufs[1], 2 * i + 1)
        c = drain(bufs[0], 2 * i, c)
        fill(bufs[0], 2 * i + 2)
        return drain(bufs[1], 2 * i + 1, c)

    nu = n // unroll
    fill(bufs[0], 0)
    carry = lax.fori_loop(0, nu // 2, body, carry)
    carry = lax.cond(nu % 2 == 1, lambda c: drain(bufs[0], nu - 1, c), lambda c: c, carry)
    base = nu * unroll
    size = unroll // 2
    while size >= 1:
        has = (n & size) != 0
        carry = lax.cond(has, functools.partial(
            lambda b, sz, c: _softmax_step([tile(b + u) for u in range(sz)], c), base, size),
            lambda c: c, carry)
        base = base + jnp.where(has, size, 0)
        size //= 2
    return carry


def _softmax_init(w):
    return (jnp.full((1, w), NEG_INF, jnp.float32), jnp.zeros((1, w), jnp.float32),
            jnp.zeros((HEAD_DIM, w), jnp.float32))


def _summarize(xk_ref, xv_ref, pe_ref, w1_ref, w2_ref, nb):
    f32, bf16 = jnp.float32, jnp.bfloat16
    x_refs = (xk_ref, xv_ref)

    def body(l, accs):
        return tuple(
            acc + jnp.dot((x_refs[s][pl.ds(l, nb, stride=CMP_BLOCK), :] + pe_ref[s, pl.ds(l, 1), :]).astype(bf16),
                          w1_ref[s, l], preferred_element_type=f32)
            for s, acc in enumerate(accs))

    accs = lax.fori_loop(0, CMP_BLOCK, body, (jnp.zeros((nb, LANE), f32),) * 2, unroll=16)
    return tuple(jnp.dot(jax.nn.gelu(acc).astype(bf16), w2_ref[s], preferred_element_type=f32)
                 for s, acc in enumerate(accs))


def _compress_kernel(xk_ref, xv_ref, pe_ref, w1_ref, w2_ref, gk_ref, bd_ref, kc_ref, vct_ref):
    f32, bf16 = jnp.float32, jnp.bfloat16
    nb = kc_ref.shape[0]
    kc, vc = _summarize(xk_ref, xv_ref, pe_ref, w1_ref, w2_ref, nb)
    kc = kc * lax.rsqrt(_group_mean_sq(kc, bd_ref[...]) + EPS) * gk_ref[...]
    lane = _lane(nb)
    blk = lax.broadcasted_iota(jnp.int32, (nb, LANE), 0).astype(f32)
    feat = jnp.where(lane == FEAT, blk, jnp.where(lane == FEAT + 1, float(CMP_BLOCK - 1), 0.0))
    for g, half in enumerate(_expand_halves(kc)):
        kc_ref[:, g * LANE:(g + 1) * LANE] = (half + feat).astype(bf16)
    vct_ref[...] = vc.T.astype(bf16)


def _compress_weights(cmp_pe, cmp_w1, cmp_w2):
    def bdiag(w):
        z = jnp.zeros_like(w)
        return jnp.concatenate([jnp.concatenate([w, z], axis=-1), jnp.concatenate([z, w], axis=-1)], axis=-2)
    pe = jnp.concatenate([cmp_pe, cmp_pe], axis=-1)
    return pe, bdiag(cmp_w1).astype(jnp.bfloat16), bdiag(cmp_w2).astype(jnp.bfloat16)


def _compress(rows2d, nb, pe, w1, w2, gk, bd):
    nseq = rows2d.shape[0] // (nb * CMP_BLOCK)
    full = lambda a: pl.BlockSpec(a.shape, lambda b: (0,) * a.ndim)
    args = (rows2d, rows2d, pe, w1, w2, jnp.concatenate([gk, gk]).reshape(1, LANE), bd)
    return pl.pallas_call(
        _compress_kernel,
        grid=(nseq,),
        in_specs=[pl.BlockSpec((nb * CMP_BLOCK, LANE), lambda b: (b, 0)),
                  pl.BlockSpec((nb * CMP_BLOCK, LANE), lambda b: (b, 1))] + [full(a) for a in args[2:]],
        out_specs=[pl.BlockSpec((nb, 2 * LANE), lambda b: (b, 0)), pl.BlockSpec((LANE, nb), lambda b: (0, b))],
        out_shape=[jax.ShapeDtypeStruct((nseq * nb, 2 * LANE), jnp.bfloat16),
                   jax.ShapeDtypeStruct((LANE, nseq * nb), jnp.bfloat16)],
        compiler_params=pltpu.CompilerParams(dimension_semantics=("arbitrary",),
                                             vmem_limit_bytes=VMEM_LIMIT),
        name="nsa_compress",
    )(*args)


def _nsa_prompt_kernel(qt_ref, gat_ref, ksel_ref, vselt_ref, kwin_ref, vwint_ref, kc_ref, vct_ref,
                       o_ref, selb_ref, buf0_ref, buf1_ref):
    f32, bf16 = jnp.float32, jnp.bfloat16
    tq = Q_BLOCK
    w = NSA_GROUP * tq
    g = pl.program_id(1)
    qi = pl.program_id(2)
    nb = kc_ref.shape[0]
    qt = jnp.concatenate([qt_ref[r * LANE:(r + 1) * LANE, :] for r in range(NSA_GROUP)], axis=1)
    qloc = lax.broadcasted_iota(jnp.int32, (1, w), 1) % tq
    qpos = qi * tq + qloc

    sc = jnp.dot(kc_ref[...], qt, preferred_element_type=f32)
    blk = lax.broadcasted_iota(jnp.int32, (nb, w), 0)
    vis = blk * CMP_BLOCK + (CMP_BLOCK - 1) <= qpos
    sc = jnp.where(vis, sc, NEG_INF)
    pc = jnp.where(vis, jnp.exp(sc - jnp.max(sc, axis=0, keepdims=True)), 0.0)
    pc = pc / jnp.maximum(jnp.sum(pc, axis=0, keepdims=True), 1e-30)
    o_c = jnp.dot(vct_ref[...], pc.astype(bf16), preferred_element_type=f32)

    imp = sum(pc[:, r * tq:(r + 1) * tq] for r in range(NSA_GROUP))
    blk1 = blk[:, 0:tq]
    cur = qpos[:, 0:tq] // SEL_BLOCK
    imp = jnp.where((blk1 == cur) | (blk1 == 0), FORCE_SCORE, jnp.where(blk1 <= cur, imp, -1.0))
    for _ in range(min(SEL_TOPK, nb)):
        mx = jnp.max(imp, axis=0, keepdims=True)
        idx = jnp.min(jnp.where(imp == mx, blk1, BIG_ID), axis=0, keepdims=True)
        imp = jnp.where(blk1 == idx, -jnp.inf, imp)
    selb = jnp.where(imp == -jnp.inf, 0.0, NEG_INF)
    selb_ref[...] = jnp.concatenate([selb] * NSA_GROUP, axis=1)

    krow = lax.broadcasted_iota(jnp.int32, (tq, w), 0)
    causal = krow <= qloc

    def sel_bias(kt):
        per = tq // SEL_BLOCK
        return jnp.concatenate([jnp.broadcast_to(selb_ref[pl.ds(kt * per + j, 1), :], (SEL_BLOCK, w))
                                for j in range(per)], axis=0)

    def sel_logits(kt):
        k0 = pl.multiple_of(kt * tq, tq)
        return jnp.dot(ksel_ref[pl.ds(k0, tq), :], qt, preferred_element_type=f32) + sel_bias(kt)

    sel_values = lambda kt: vselt_ref[:, pl.ds(pl.multiple_of(kt * tq, tq), tq)]
    last_group = ksel_ref.shape[0] // (tq * TILE_UNROLL) - 1
    carry = _tile_loop(qi, sel_logits, sel_values, _softmax_init(w), (buf0_ref, buf1_ref), tq, last_group)
    m_s, l_s, acc_s = _softmax_step([(jnp.where(causal, sel_logits(qi), NEG_INF), sel_values(qi))], carry)
    o_s = acc_s / l_s

    nwin = WINDOW // tq
    tiles = []
    for j in range(nwin + 1):
        kt = qi - nwin + j
        k0 = pl.multiple_of(jnp.maximum(kt, 0) * tq, tq)
        s = jnp.dot(kwin_ref[pl.ds(k0, tq), :], qt, preferred_element_type=f32)
        ok = kt >= 0
        if j == 0:
            ok = ok & (krow > qloc)
        elif j == nwin:
            ok = ok & causal
        tiles.append((jnp.where(ok, s, NEG_INF), vwint_ref[:, pl.ds(k0, tq)]))
    m_w, l_w, acc_w = _softmax_step(tiles, _softmax_init(w))
    o_w = acc_w / l_w

    for r in range(NSA_GROUP):
        gate = lambda br: gat_ref[pl.ds((g * NSA_GROUP + r) * 3 + br, 1), :]
        sl = slice(r * tq, (r + 1) * tq)
        o_ref[r * HEAD_DIM:(r + 1) * HEAD_DIM, :] = (
            gate(0) * o_c[:, sl] + gate(1) * o_s[:, sl] + gate(2) * o_w[:, sl])


def _nsa_prompt(qat, gat, ksel, vselt, kwin, vwint, kc, vct, nseq, seq_len):
    n = qat.shape[1]
    nq = seq_len // Q_BLOCK
    nb = seq_len // CMP_BLOCK
    gw = NSA_GROUP * LANE
    return pl.pallas_call(
        _nsa_prompt_kernel,
        grid=(nseq, NSA_KV_HEADS, nq),
        in_specs=[pl.BlockSpec((gw, Q_BLOCK), lambda b, g, q: (g, b * nq + q)),
                  pl.BlockSpec((LANE, Q_BLOCK), lambda b, g, q: (0, b * nq + q)),
                  pl.BlockSpec((seq_len, LANE), lambda b, g, q: (b, g)),
                  pl.BlockSpec((HEAD_DIM, seq_len), lambda b, g, q: (g, b)),
                  pl.BlockSpec((seq_len, LANE), lambda b, g, q: (b, g)),
                  pl.BlockSpec((HEAD_DIM, seq_len), lambda b, g, q: (g, b)),
                  pl.BlockSpec((nb, LANE), lambda b, g, q: (b, g)),
                  pl.BlockSpec((HEAD_DIM, nb), lambda b, g, q: (g, b))],
        out_specs=pl.BlockSpec((NSA_GROUP * HEAD_DIM, Q_BLOCK), lambda b, g, q: (g, b * nq + q)),
        out_shape=jax.ShapeDtypeStruct((NSA_Q_W, n), jnp.float32),
        scratch_shapes=[pltpu.VMEM((nb, NSA_GROUP * Q_BLOCK), jnp.float32)]
        + [pltpu.VMEM((TILE_UNROLL * Q_BLOCK, NSA_GROUP * Q_BLOCK), jnp.float32)] * 2,
        compiler_params=pltpu.CompilerParams(dimension_semantics=("arbitrary", "arbitrary", "arbitrary"),
                                             vmem_limit_bytes=VMEM_LIMIT),
        name="nsa_prompt",
    )(qat, gat, ksel, vselt, kwin, vwint, kc, vct)


FOX_TQ = 512
FOX_TK = 128


def _fox_prompt_kernel(qt_ref, kb_ref, vbt_ref, o_ref, buf0_ref, buf1_ref):
    f32 = jnp.float32
    tq, tk = FOX_TQ, FOX_TK
    qi = pl.program_id(2)
    qt = qt_ref[...]

    logits = lambda kt: jnp.dot(kb_ref[pl.ds(pl.multiple_of(kt * tk, tk), tk), :], qt, preferred_element_type=f32)
    values = lambda kt: vbt_ref[:, pl.ds(pl.multiple_of(kt * tk, tk), tk)]

    ndiag = tq // tk
    last_group = kb_ref.shape[0] // (tk * TILE_UNROLL) - 1
    carry = _tile_loop(qi * ndiag, logits, values, _softmax_init(tq), (buf0_ref, buf1_ref), tk, last_group)
    krow = lax.broadcasted_iota(jnp.int32, (tk, tq), 0)
    qloc = lax.broadcasted_iota(jnp.int32, (tk, tq), 1)
    diag = []
    for j in range(ndiag):
        kt = qi * ndiag + j
        diag.append((jnp.where(krow + j * tk <= qloc, logits(kt), NEG_INF), values(kt)))
    m, l, acc = _softmax_step(diag, carry)
    o_ref[...] = acc / l


def _fox_prompt(qbt, kb, vbt, nseq, seq_len):
    n = qbt.shape[1]
    nq = seq_len // FOX_TQ
    return pl.pallas_call(
        _fox_prompt_kernel,
        grid=(nseq, FOX_HEADS, nq),
        in_specs=[pl.BlockSpec((LANE, FOX_TQ), lambda b, h, q: (h, b * nq + q)),
                  pl.BlockSpec((seq_len, LANE), lambda b, h, q: (b, h)),
                  pl.BlockSpec((HEAD_DIM, seq_len), lambda b, h, q: (h, b))],
        out_specs=pl.BlockSpec((HEAD_DIM, FOX_TQ), lambda b, h, q: (h, b * nq + q)),
        out_shape=jax.ShapeDtypeStruct((FOX_W, n), jnp.float32),
        scratch_shapes=[pltpu.VMEM((TILE_UNROLL * FOX_TK, FOX_TQ), jnp.float32)] * 2,
        compiler_params=pltpu.CompilerParams(dimension_semantics=("arbitrary", "arbitrary", "arbitrary"),
                                             vmem_limit_bytes=VMEM_LIMIT),
        name="fox_prompt",
    )(qbt, kb, vbt)


def _slope_rows(shape, rows_per_head):
    hd = lax.broadcasted_iota(jnp.int32, shape, 0) // rows_per_head
    return pltpu.bitcast((126 - hd) << 23, jnp.float32)


def _pad_rows(x, rows):
    return jnp.concatenate([x, jnp.zeros((rows - x.shape[0], x.shape[1]), x.dtype)], axis=0)


def _nt_dot(a, b):
    return lax.dot_general(a, b, (((1,), (1,)), ((), ())), preferred_element_type=jnp.float32)


def _joint_softmax(parts):
    m = functools.reduce(jnp.maximum, [jnp.max(p, axis=1, keepdims=True) for p in parts])
    es = [jnp.exp(p - m) for p in parts]
    return es, sum(jnp.sum(e, axis=1, keepdims=True) for e in es)


def _nsa_sample_kernel(npages, ns, pt_ref, *refs):
    f32, bf16 = jnp.float32, jnp.bfloat16
    pages = refs[:npages]
    (e_ref, qa_ref, ga_ref, rnew_ref, wnew_ref, state_ref, pe_ref, w1_ref, w2_ref, gk_ref, bd_ref,
     o_ref, bufk_ref, bufv_ref) = refs[npages:]
    page = pages[0].shape[-1]
    past = npages * page
    ncol = NSA_KV_HEADS * npages
    nrow = NSA_HEADS * ns

    for p in range(npages):
        bufk_ref[p * LANE:(p + 1) * LANE, :] = pages[p][0, 0].reshape(LANE, page)
        bufv_ref[p * LANE:(p + 1) * LANE, :] = pages[p][0, 1].reshape(LANE, page)
    kc, vc = _summarize(bufk_ref, bufv_ref, pe_ref, w1_ref, w2_ref, ncol)
    kc = (kc * lax.rsqrt(_group_mean_sq(kc, bd_ref[...]) + EPS) * gk_ref[...]).astype(bf16)
    vc = vc.astype(bf16)

    lane8 = _lane(ns)
    ql = qa_ref[...]
    qrows = []
    for hd in range(NSA_HEADS):
        g = hd // NSA_GROUP
        t = ql[:, (hd // 2) * LANE:(hd // 2 + 1) * LANE]
        if hd % 2 != g:
            t = pltpu.roll(t, HEAD_DIM, axis=1)
        qrows.append(jnp.where(lane8 // HEAD_DIM == g, t, 0.0))
    qb = (jnp.concatenate(qrows, axis=0) * SCALE).astype(bf16)

    def geom(width):
        tok = lax.broadcasted_iota(jnp.int32, (nrow, width), 0) % ns
        col = lax.broadcasted_iota(jnp.int32, (nrow, width), 1)
        return tok, col, _slope_rows((nrow, width), ns)

    q_both = qb + pltpu.roll(qb.astype(f32), HEAD_DIM, axis=1).astype(bf16)
    lane64 = lax.broadcasted_iota(jnp.int32, (nrow, LANE), 1) // HEAD_DIM
    tok, col, slope = geom(ncol)
    own = (lax.broadcasted_iota(jnp.int32, (nrow, ncol), 0) // (NSA_GROUP * ns)) == col % NSA_KV_HEADS
    lcs, blks = [], []
    for half in range(2):
        blk = (col // NSA_KV_HEADS) * 2 + half
        lc = _nt_dot(jnp.where(lane64 == half, q_both, 0.0), kc)
        lc = lc - slope * (past + tok - (blk * CMP_BLOCK + CMP_BLOCK - 1)).astype(f32)
        lcs.append(jnp.where(own, lc, NEG_INF))
        blks.append(blk)
    ecs, lsum = _joint_softmax(lcs)
    pcs = [e / lsum for e in ecs]
    res = [jnp.dot(pc.astype(bf16), vc, preferred_element_type=f32) for pc in pcs]
    o_c = jnp.where(lane64 == 0, res[0], res[1])
    o_c = o_c + pltpu.roll(o_c, HEAD_DIM, axis=1)

    nsel = NSA_KV_HEADS * ns
    col_s = lax.broadcasted_iota(jnp.int32, (nsel, ncol), 1)
    own_s = (lax.broadcasted_iota(jnp.int32, (nsel, ncol), 0) // ns) == col_s % NSA_KV_HEADS
    blks_s = [(col_s // NSA_KV_HEADS) * 2 + half for half in range(2)]
    imps = []
    for half in range(2):
        imp = jnp.concatenate(
            [sum(pcs[half][(g * NSA_GROUP + r) * ns:(g * NSA_GROUP + r + 1) * ns] for r in range(NSA_GROUP))
             for g in range(NSA_KV_HEADS)], axis=0)
        imp = jnp.where(blks_s[half] == 0, FORCE_SCORE, imp)
        imps.append(jnp.where(own_s, imp, -1.0))
    ranks = [jnp.zeros((nsel, ncol), jnp.int32) for _ in range(2)]
    for h2 in range(2):
        for c in range(ncol):
            other = jnp.broadcast_to(imps[h2][:, c:c + 1], (nsel, ncol))
            blk_c = (c // NSA_KV_HEADS) * 2 + h2
            for half in range(2):
                ahead = (other > imps[half]) | ((other == imps[half]) & (blks_s[half] > blk_c))
                ranks[half] = ranks[half] + jnp.where(ahead, 1, 0)
    selexp = 0.0
    for half in range(2):
        sel = jnp.where((ranks[half] < SEL_TOPK - 1) & own_s, 1.0, 0.0)
        sel = jnp.concatenate([sel[g * ns:(g + 1) * ns] for g in range(NSA_KV_HEADS) for _ in range(NSA_GROUP)],
                              axis=0)
        selexp = selexp + jnp.dot(sel.astype(bf16), e_ref[half], preferred_element_type=f32)

    def new_tile(k_new):
        tok, col, slope = geom(LANE)
        s = _nt_dot(qb, _pad_rows(k_new, LANE).astype(bf16))
        return jnp.where(col <= tok, s - slope * (tok - col).astype(f32), NEG_INF)

    def weighted(es, vts, e_new, v_new, lsum):
        acc = sum(_nt_dot(e.astype(bf16), vt.astype(bf16)) for e, vt in zip(es, vts))
        acc = acc + jnp.dot(e_new.astype(bf16), _pad_rows(v_new, LANE).astype(bf16), preferred_element_type=f32)
        return acc / lsum

    tok, col, slope = geom(past)
    ls = jnp.concatenate([jnp.dot(qb, pages[p][0, 2].reshape(LANE, page).astype(bf16), preferred_element_type=f32)
                          for p in range(npages)], axis=1)
    ls = jnp.where(selexp > 0.5, ls - slope * (past + tok - col).astype(f32), NEG_INF)
    (es, en), lsum = _joint_softmax([ls, new_tile(rnew_ref[:, 2 * LANE:3 * LANE])])
    o_s = weighted([es[:, p * page:(p + 1) * page] for p in range(npages)],
                   [pages[p][0, 3].reshape(LANE, page) for p in range(npages)],
                   en, rnew_ref[:, 3 * LANE:4 * LANE], lsum)

    wbuf = state_ref.shape[-1]
    tok, col, slope = geom(wbuf)
    lw = jnp.dot(qb, state_ref[0, 0].reshape(LANE, wbuf).astype(bf16), preferred_element_type=f32)
    lw = jnp.where(col > tok + (wbuf - WINDOW), lw - slope * (wbuf + tok - col).astype(f32), NEG_INF)
    (ew, en), lsum = _joint_softmax([lw, new_tile(wnew_ref[:, 0:LANE])])
    o_w = weighted([ew], [state_ref[0, 1].reshape(LANE, wbuf)], en, wnew_ref[:, LANE:2 * LANE], lsum)

    ga = ga_ref[...]
    gate = lambda br: jnp.concatenate(
        [jnp.broadcast_to(ga[:, hd * 3 + br:hd * 3 + br + 1], (ns, LANE)) for hd in range(NSA_HEADS)], axis=0)
    o = gate(0) * o_c + gate(1) * o_s + gate(2) * o_w
    for j in range(NSA_HEADS // 2):
        g = (2 * j) // NSA_GROUP
        a = o[2 * j * ns:(2 * j + 1) * ns]
        b = o[(2 * j + 1) * ns:(2 * j + 2) * ns]
        if g == 0:
            b = pltpu.roll(b, HEAD_DIM, axis=1)
        else:
            a = pltpu.roll(a, HEAD_DIM, axis=1)
        o_ref[:, j * LANE:(j + 1) * LANE] = jnp.where(lane8 < HEAD_DIM, a, b)


def _compress_weights_t(cmp_pe, cmp_w1, cmp_w2):
    pe, w1, w2 = _compress_weights(cmp_pe.transpose(0, 2, 1), cmp_w1.transpose(0, 2, 1, 3), cmp_w2)
    return pe, w1, w2


def _nsa_sample(page_table, cache_t, qa, ga, rows_new, win_new, state_t, pe, w1, w2, gk, bd):
    db, npages = page_table.shape
    page = cache_t.shape[-1]
    assert page == 2 * CMP_BLOCK == LANE
    ns = qa.shape[0] // db
    past = npages * page
    ncol = NSA_KV_HEADS * npages
    half = lax.broadcasted_iota(jnp.int32, (2, ncol, past), 0)
    col = lax.broadcasted_iota(jnp.int32, (2, ncol, past), 1)
    key = lax.broadcasted_iota(jnp.int32, (2, ncol, past), 2)
    expand = jnp.where(key // SEL_BLOCK == (col // NSA_KV_HEADS) * 2 + half, 1.0, 0.0).astype(jnp.bfloat16)
    gk2 = jnp.concatenate([gk, gk]).reshape(1, LANE)
    full = lambda a: pl.BlockSpec(a.shape, lambda b, pt: (0,) * a.ndim)
    tok = lambda w: pl.BlockSpec((ns, w), lambda b, pt: (b, 0))
    page_specs = [pl.BlockSpec((1,) + cache_t.shape[1:], functools.partial(lambda p, b, pt: (pt[b, p], 0, 0, 0, 0), p))
                  for p in range(npages)]
    return pl.pallas_call(
        functools.partial(_nsa_sample_kernel, npages, ns),
        grid_spec=pltpu.PrefetchScalarGridSpec(
            num_scalar_prefetch=1, grid=(db,),
            in_specs=page_specs + [full(expand), tok(NSA_Q_W), tok(LANE), tok(512), tok(256),
                                   pl.BlockSpec((1,) + state_t.shape[1:], lambda b, pt: (b, 0, 0, 0, 0)),
                                   full(pe), full(w1), full(w2), full(gk2), full(bd)],
            out_specs=tok(NSA_Q_W),
            scratch_shapes=[pltpu.VMEM((npages * LANE, page), jnp.float32),
                            pltpu.VMEM((npages * LANE, page), jnp.float32)]),
        out_shape=jax.ShapeDtypeStruct((db * ns, NSA_Q_W), jnp.float32),
        compiler_params=pltpu.CompilerParams(dimension_semantics=("arbitrary",),
                                             vmem_limit_bytes=VMEM_LIMIT),
        name="nsa_sample",
    )(page_table, *([cache_t] * npages), expand, qa, ga, rows_new, win_new, state_t, pe, w1, w2, gk2, bd)


def _fox_decode_kernel(npages, ns, pt_ref, *refs):
    f32, bf16 = jnp.float32, jnp.bfloat16
    kvt = refs[:npages]
    lft = refs[npages:2 * npages]
    qb_ref, knew_ref, lfnew_ref, o_ref = refs[2 * npages:]
    page = kvt[0].shape[-1]
    nh = FOX_HEADS
    nrow = nh * ns

    r_io = lax.broadcasted_iota(jnp.int32, (page, page), 0)
    c_io = lax.broadcasted_iota(jnp.int32, (page, page), 1)
    triu = jnp.where(r_io <= c_io, 1.0, 0.0).astype(bf16)
    carry = jnp.zeros((nh, 1), f32)
    negc = []
    for t in range(npages + 1):
        lf = lft[t][0] if t < npages else lfnew_ref[0]
        ct = carry + sum(jnp.dot(pc_, triu, preferred_element_type=f32) for pc_ in _split3(lf))
        carry = ct[:, page - 1:page]
        negc.append(jnp.concatenate([jnp.broadcast_to(-ct[hd:hd + 1], (ns, page)) for hd in range(nh)], axis=0))

    head_of_lane = lax.broadcasted_iota(jnp.int32, (ns, FOX_W), 1) // HEAD_DIM
    q = qb_ref[...] * SCALE
    q_bd = jnp.concatenate([jnp.where(head_of_lane == hd, q, 0.0) for hd in range(nh)], axis=0).astype(bf16)

    parts = [jnp.dot(q_bd, kvt[p][0, 0].reshape(FOX_W, page).astype(bf16), preferred_element_type=f32) + negc[p]
             for p in range(npages)]
    tok = lax.broadcasted_iota(jnp.int32, (nrow, page), 0) % ns
    col = lax.broadcasted_iota(jnp.int32, (nrow, page), 1)
    s_new = _nt_dot(q_bd, _pad_rows(knew_ref[:, 0:FOX_W], page).astype(bf16)) + negc[npages]
    parts.append(jnp.where(col <= tok, s_new, NEG_INF))
    es, lsum = _joint_softmax(parts)
    acc = sum(_nt_dot(es[p].astype(bf16), kvt[p][0, 1].reshape(FOX_W, page).astype(bf16)) for p in range(npages))
    acc = acc + jnp.dot(es[npages].astype(bf16), _pad_rows(knew_ref[:, FOX_W:2 * FOX_W], page).astype(bf16),
                        preferred_element_type=f32)
    acc = acc / lsum
    o_ref[...] = sum(jnp.where(head_of_lane == hd, acc[hd * ns:(hd + 1) * ns], 0.0) for hd in range(nh))


def _fox_decode(page_table, cache_kvt, lft, qb, fox_new, lft_new):
    db, npages = page_table.shape
    ns = qb.shape[0] // db
    tok = lambda w: pl.BlockSpec((ns, w), lambda b, pt: (b, 0))
    pg = lambda a: [pl.BlockSpec((1,) + a.shape[1:],
                                 functools.partial(lambda p, nd, b, pt: (pt[b, p],) + (0,) * nd, p, a.ndim - 1))
                    for p in range(npages)]
    return pl.pallas_call(
        functools.partial(_fox_decode_kernel, npages, ns),
        grid_spec=pltpu.PrefetchScalarGridSpec(
            num_scalar_prefetch=1, grid=(db,),
            in_specs=pg(cache_kvt) + pg(lft) + [tok(FOX_W), tok(2 * FOX_W),
                                                pl.BlockSpec((1,) + lft_new.shape[1:], lambda b, pt: (b, 0, 0))],
            out_specs=tok(FOX_W)),
        out_shape=jax.ShapeDtypeStruct((db * ns, FOX_W), jnp.float32),
        compiler_params=pltpu.CompilerParams(dimension_semantics=("arbitrary",),
                                             vmem_limit_bytes=VMEM_LIMIT),
        name="fox_decode",
    )(page_table, *([cache_kvt] * npages), *([lft] * npages), qb, fox_new, lft_new)


BIG_ID = 1 << 20


def _topk_rows(s, k, ids):
    w = ids.shape[1]
    if s.shape[1] > w:
        parts = [_topk_rows(s[:, c:c + w], k, ids) for c in range(0, s.shape[1], w)]
        return jnp.concatenate([p[0] for p in parts], axis=1), jnp.concatenate([p[1] for p in parts], axis=1)
    ids = ids.astype(jnp.float32)
    vals, idxs = [], []
    for _ in range(k):
        m = jnp.max(s, axis=0, keepdims=True)
        idx = jnp.min(jnp.where(s == m, ids, float(BIG_ID)), axis=0, keepdims=True)
        vals.append(m)
        idxs.append(idx)
        s = jnp.where(ids == idx, -jnp.inf, s)
    return jnp.concatenate(vals, axis=0), jnp.concatenate(idxs, axis=0).astype(jnp.int32)


def _pick_rows(sel, table):
    out = jnp.zeros(sel.shape, table.dtype)
    for r in range(table.shape[0]):
        out = jnp.where(sel == r, table[r:r + 1, :], out)
    return out


def _merge_route_kernel(x_ref, on_ref, of_ref, mg_ref, wun_ref, wuf_ref, wo_ref, nf_ref, wqt_ref, sk_ref,
                        x1_ref, h2_ref, i1_ref, i2_ref, g_ref):
    tm = x_ref.shape[0]
    f32, bf16 = jnp.float32, jnp.bfloat16
    tdot = lambda ot, wgt: lax.dot_general(ot.astype(bf16), wgt, (((0,), (0,)), ((), ())),
                                           preferred_element_type=f32)
    a = tdot(on_ref[...], wun_ref[...])
    b = tdot(of_ref[...], wuf_ref[...])
    mixed = mg_ref[:, 0:D_MODEL] * a + mg_ref[:, D_MODEL:2 * D_MODEL] * b
    x1 = x_ref[...] + jnp.dot(mixed.astype(bf16), wo_ref[...], preferred_element_type=f32)
    x1_ref[...] = x1
    h2 = x1 * lax.rsqrt(jnp.mean(x1 * x1, axis=-1, keepdims=True) + EPS) * nf_ref[...]
    h2b = h2.astype(bf16)
    h2_ref[...] = h2b

    nk = PEER_N_KEYS
    key_ids = lax.broadcasted_iota(jnp.int32, (nk, LANE), 0)
    io16 = lax.broadcasted_iota(jnp.int32, (PEER_TOPK, LANE), 0)
    io8 = lax.broadcasted_iota(jnp.int32, (8, LANE), 0)
    cand_ids = jnp.concatenate([io16] + [a_ * PEER_TOPK + io8 for a_ in range(1, 8)]
                               + [(io8 + 8) * PEER_TOPK], axis=0)
    for h in range(PEER_HEADS):
        sv, si = [], []
        for p in range(2):
            hp = 2 * h + p
            qt = lax.dot_general(wqt_ref[hp * PEER_DK_HALF:(hp + 1) * PEER_DK_HALF, :], h2b,
                                 (((1,), (1,)), ((), ())), preferred_element_type=f32)
            st = jnp.dot(sk_ref[hp], qt.astype(bf16), preferred_element_type=f32)
            v, i = _topk_rows(st, PEER_TOPK, key_ids)
            sv.append(v)
            si.append(i)
        s1, s2 = sv
        cand = jnp.concatenate([s1[0:1] + s2] + [s1[a_:a_ + 1] + s2[0:8] for a_ in range(1, 8)]
                               + [s1[8:16] + s2[0:1]], axis=0)
        top, fid = _topk_rows(cand, PEER_TOPK, cand_ids)
        e = jnp.exp(top - jnp.max(top, axis=0, keepdims=True))
        g = e / jnp.sum(e, axis=0, keepdims=True)
        sl = slice(h * PEER_TOPK, (h + 1) * PEER_TOPK)
        i1_ref[:, sl] = _pick_rows(fid >> 4, si[0]).T
        i2_ref[:, sl] = _pick_rows(fid & (PEER_TOPK - 1), si[1]).T
        g_ref[:, sl] = g.T


def _merge_route(x2d, o_nsa, o_fox, mg, w_up_nsa, w_up_fox, w_out, norm_ffn, wq_t, sub_keys, tm=256):
    n = x2d.shape[0]
    row = lambda w: pl.BlockSpec((tm, w), lambda i: (i, 0))
    col = lambda h: pl.BlockSpec((h, tm), lambda i: (0, i))
    full = lambda a: pl.BlockSpec(a.shape, lambda i: (0,) * a.ndim)
    args = (x2d, o_nsa, o_fox, mg, w_up_nsa, w_up_fox, w_out, norm_ffn.reshape(1, D_MODEL), wq_t, sub_keys)
    hk = PEER_HEADS * PEER_TOPK
    return pl.pallas_call(
        _merge_route_kernel,
        grid=(n // tm,),
        in_specs=[row(D_MODEL), col(NSA_Q_W), col(FOX_W), row(MERGE_W)] + [full(a) for a in args[4:]],
        out_specs=[row(D_MODEL), row(D_MODEL), row(hk), row(hk), row(hk)],
        out_shape=[jax.ShapeDtypeStruct((n, D_MODEL), jnp.float32),
                   jax.ShapeDtypeStruct((n, D_MODEL), jnp.bfloat16),
                   jax.ShapeDtypeStruct((n, hk), jnp.int32),
                   jax.ShapeDtypeStruct((n, hk), jnp.int32),
                   jax.ShapeDtypeStruct((n, hk), jnp.float32)],
        compiler_params=pltpu.CompilerParams(dimension_semantics=("arbitrary",),
                                             vmem_limit_bytes=VMEM_LIMIT),
        name="merge_route",
    )(*args)


def _peer_act_kernel(h2_ref, u_ref, i1_ref, i2_ref, act_ref):
    c = pl.program_id(1)
    ec = u_ref.shape[0]

    @pl.when(c == 0)
    def _():
        act_ref[...] = jnp.zeros_like(act_ref)

    a = lax.dot_general(h2_ref[...], u_ref[...], (((1,), (1,)), ((), ())),
                        preferred_element_type=jnp.float32)
    i1 = i1_ref[...]
    i2 = i2_ref[...]
    act = act_ref[...]
    for ii in range(ec // PEER_N_KEYS):
        got = jnp.take_along_axis(a[:, ii * PEER_N_KEYS:(ii + 1) * PEER_N_KEYS], i2, axis=1)
        act = jnp.where(i1 == c * (ec // PEER_N_KEYS) + ii, got, act)
    act_ref[...] = act


def _peer_act(h2b, u_b, i1, i2, tm=512, ec=2048):
    n = h2b.shape[0]
    hk = i1.shape[1]
    return pl.pallas_call(
        _peer_act_kernel,
        grid=(n // tm, u_b.shape[0] // ec),
        in_specs=[pl.BlockSpec((tm, D_MODEL), lambda t, c: (t, 0)),
                  pl.BlockSpec((ec, D_MODEL), lambda t, c: (c, 0)),
                  pl.BlockSpec((tm, hk), lambda t, c: (t, 0)),
                  pl.BlockSpec((tm, hk), lambda t, c: (t, 0))],
        out_specs=pl.BlockSpec((tm, hk), lambda t, c: (t, 0)),
        out_shape=jax.ShapeDtypeStruct((n, hk), jnp.float32),
        compiler_params=pltpu.CompilerParams(dimension_semantics=("arbitrary", "arbitrary"),
                                             vmem_limit_bytes=VMEM_LIMIT),
        name="peer_act",
    )(h2b, u_b, i1, i2)


def _peer_coef_kernel(act_ref, g_ref, i1_ref, i2_ref, c_ref, coef_ref):
    tm = act_ref.shape[0]
    nk = PEER_N_KEYS
    coef_ref[...] = g_ref[...] * jax.nn.gelu(act_ref[...])
    sub = lax.broadcasted_iota(jnp.int32, (nk, i1_ref.shape[1]), 0)

    def token(t):
        r1 = i1_ref[pl.ds(t, 1), :]
        r2 = i2_ref[pl.ds(t, 1), :]
        cf = coef_ref[pl.ds(t, 1), :]
        m1 = jnp.where(r1 == sub, cf, 0.0).astype(jnp.bfloat16)
        m2t = jnp.where(r2 == sub, 1.0, 0.0).astype(jnp.bfloat16)
        return lax.dot_general(m1, m2t, (((1,), (1,)), ((), ())), preferred_element_type=jnp.float32)

    def body(tg, carry):
        t0 = pl.multiple_of(tg * COEF_GROUP, COEF_GROUP)
        ct = jnp.stack([token(t0 + u) for u in range(COEF_GROUP)], axis=0)
        c_ref[:, pl.ds(t0, COEF_GROUP), :] = pltpu.einshape("tij->itj", ct).astype(c_ref.dtype)
        return carry

    lax.fori_loop(0, tm // COEF_GROUP, body, 0)


COEF_GROUP = 16


def _peer_coef(act, g, i1, i2, tm=128):
    n, hk = act.shape
    nk = PEER_N_KEYS
    row = pl.BlockSpec((tm, hk), lambda t: (t, 0))
    return pl.pallas_call(
        _peer_coef_kernel,
        grid=(n // tm,),
        in_specs=[row, row, row, row],
        out_specs=pl.BlockSpec((nk, tm, nk), lambda t: (0, t, 0)),
        out_shape=jax.ShapeDtypeStruct((nk, n, nk), jnp.bfloat16),
        scratch_shapes=[pltpu.VMEM((tm, hk), jnp.float32)],
        compiler_params=pltpu.CompilerParams(dimension_semantics=("arbitrary",),
                                             vmem_limit_bytes=VMEM_LIMIT),
        name="peer_coef",
    )(act, g, i1, i2)


def _peer_out_kernel(c_ref, v_ref, x1_ref, y_ref, acc_ref):
    k = pl.program_id(1)

    @pl.when(k == 0)
    def _():
        acc_ref[...] = x1_ref[...]

    acc = acc_ref[...]
    nk = PEER_N_KEYS
    for p in range(c_ref.shape[0] // 2):
        lhs = jnp.concatenate([c_ref[2 * p], c_ref[2 * p + 1]], axis=1)
        acc = acc + jnp.dot(lhs, v_ref[2 * p * nk:(2 * p + 2) * nk, :], preferred_element_type=jnp.float32)
    acc_ref[...] = acc

    @pl.when(k == pl.num_programs(1) - 1)
    def _():
        y_ref[...] = acc_ref[...]


def _peer_out(c3, v_b, x1, tm=1024, tk=2048):
    nk, n, _ = c3.shape
    ne = nk * nk
    return pl.pallas_call(
        _peer_out_kernel,
        grid=(n // tm, ne // tk),
        in_specs=[pl.BlockSpec((tk // nk, tm, nk), lambda t, k: (k, t, 0)),
                  pl.BlockSpec((tk, D_MODEL), lambda t, k: (k, 0)),
                  pl.BlockSpec((tm, D_MODEL), lambda t, k: (t, 0))],
        out_specs=pl.BlockSpec((tm, D_MODEL), lambda t, k: (t, 0)),
        out_shape=jax.ShapeDtypeStruct((n, D_MODEL), jnp.float32),
        scratch_shapes=[pltpu.VMEM((tm, D_MODEL), jnp.float32)],
        compiler_params=pltpu.CompilerParams(dimension_semantics=("arbitrary", "arbitrary"),
                                             vmem_limit_bytes=VMEM_LIMIT),
        name="peer_out",
    )(c3, v_b, x1)


def _peer_weights(w_up_nsa, w_up_fox, w_out, norm_ffn, peer_w_query, peer_sub_keys, peer_u, peer_v):
    bf16 = jnp.bfloat16
    return dict(w_up_nsa=w_up_nsa.astype(bf16), w_up_fox=w_up_fox.astype(bf16), w_out=w_out.astype(bf16),
                norm_ffn=norm_ffn, wq_t=peer_w_query.T.astype(bf16),
                sub_keys=peer_sub_keys.reshape(2 * PEER_HEADS, PEER_N_KEYS, PEER_DK_HALF).astype(bf16),
                u=peer_u.astype(bf16), v=peer_v.astype(bf16))


def _merge_peer(x2d, o_nsa, o_fox, mg, wts):
    x1, h2b, i1, i2, g = _merge_route(x2d, o_nsa, o_fox, mg, wts['w_up_nsa'], wts['w_up_fox'], wts['w_out'],
                                      wts['norm_ffn'], wts['wq_t'], wts['sub_keys'])
    act = _peer_act(h2b, wts['u'], i1, i2)
    c3 = _peer_coef(act, g, i1, i2)
    return _peer_out(c3, wts['v'], x1)


def kernel(x_prompt, x_sample, cache_nsa, cache_fox_kv, cache_fox_logf, state_nsa_win, page_table,
           norm_attn, w_in, fox_f_bias, nsa_q_norm, nsa_k_norm, fox_q_norm, fox_k_norm,
           cmp_pe, cmp_w1, cmp_w2, w_up_nsa, w_up_fox, w_out, norm_ffn,
           peer_w_query, peer_sub_keys, peer_u, peer_v):
    w_front = _front_weights(w_in)
    bd = _block_diag_mean()
    wts = _peer_weights(w_up_nsa, w_up_fox, w_out, norm_ffn, peer_w_query, peer_sub_keys, peer_u, peer_v)
    cmp_wts = _compress_weights(cmp_pe, cmp_w1, cmp_w2)

    bp, seq, _ = x_prompt.shape
    n_p = bp * seq
    (rows_t, win_t, fox_t, logf_p2d, mg_p, qat, gat, ksel, kwin, vselt, vwint, qbt, kb, vbt, rows_cmp) = _front_attn(
        x_prompt.reshape(n_p, D_MODEL), seq, norm_attn, w_front, bd, fox_f_bias,
        nsa_q_norm, nsa_k_norm, fox_q_norm, fox_k_norm)
    kc, vct = _compress(rows_cmp, seq // CMP_BLOCK, *cmp_wts, nsa_k_norm[0], bd)
    o_nsa_t = _nsa_prompt(qat, gat, ksel, vselt, kwin, vwint, kc, vct, bp, seq)
    o_fox_t = _fox_prompt(qbt, kb, vbt, bp, seq)
    y_p = _merge_peer(x_prompt.reshape(n_p, D_MODEL), o_nsa_t, o_fox_t, mg_p, wts).reshape(x_prompt.shape)
    to_rows = lambda a, *dims: a.reshape(bp, *dims, a.shape[-1]).transpose(0, len(dims) + 1, *range(1, len(dims) + 1))
    nsa_p = to_rows(rows_t, 4, NSA_KV_HEADS, HEAD_DIM)
    fox_p = to_rows(fox_t, 2, FOX_HEADS, HEAD_DIM)
    logf_p = logf_p2d.reshape(bp, seq, FOX_HEADS)
    win_p = to_rows(win_t[:, :, seq - min(WINDOW, seq):], 2, NSA_KV_HEADS, HEAD_DIM)

    db, ns, _ = x_sample.shape
    n_s = db * ns
    n_pool, page = cache_nsa.shape[:2]
    wbuf = state_nsa_win.shape[1]
    qa_s, rows_s, win_s2d, ga_s, qb_s, fox_s2d, logf_s2d, mg_s = _front(
        x_sample.reshape(n_s, D_MODEL), norm_attn, w_front, bd, fox_f_bias,
        nsa_q_norm, nsa_k_norm, fox_q_norm, fox_k_norm)
    o_nsa_s = _nsa_sample(page_table, cache_nsa.transpose(0, 2, 3, 4, 1), qa_s, ga_s, rows_s, win_s2d,
                          state_nsa_win.transpose(0, 2, 3, 4, 1),
                          *_compress_weights_t(cmp_pe, cmp_w1, cmp_w2), nsa_k_norm[0], bd)
    lft_new = jnp.pad(logf_s2d.reshape(db, ns, FOX_HEADS).transpose(0, 2, 1), ((0, 0), (0, 0), (0, page - ns)))
    o_fox_s = _fox_decode(page_table, cache_fox_kv.transpose(0, 2, 3, 4, 1), cache_fox_logf.transpose(0, 2, 1),
                          qb_s, fox_s2d, lft_new)
    y_s = _merge_peer(x_sample.reshape(n_s, D_MODEL), o_nsa_s.T, o_fox_s.T, mg_s, wts).reshape(x_sample.shape)
    nsa_s = rows_s.reshape(db, ns, 4, NSA_KV_HEADS, HEAD_DIM)
    fox_s = fox_s2d.reshape(db, ns, 2, FOX_HEADS, HEAD_DIM)
    logf_s = logf_s2d.reshape(db, ns, FOX_HEADS)
    win_s = jnp.concatenate([state_nsa_win[:, ns:], win_s2d.reshape(db, ns, 2, NSA_KV_HEADS, HEAD_DIM)], axis=1)
    return (y_p, y_s, nsa_p, fox_p, logf_p, win_p, nsa_s, fox_s, logf_s, win_s)
```

```python
import functools

import jax
import jax.numpy as jnp
from jax import lax
from jax.experimental import pallas as pl
from jax.experimental.pallas import tpu as pltpu

D_MODEL = 1024
HEAD_DIM = 64
NSA_HEADS = 8
NSA_KV_HEADS = 2
NSA_GROUP = NSA_HEADS // NSA_KV_HEADS
CMP_BLOCK = 64
SEL_BLOCK = CMP_BLOCK
SEL_TOPK = 16
WINDOW = 512
FOX_HEADS = 8
Q_BLOCK = 128
PEER_HEADS = 8
PEER_N_KEYS = 128
PEER_DK = 256
PEER_DK_HALF = PEER_DK // 2
PEER_TOPK = 16
PEER_CHUNK = 256

NSA_Q_W = NSA_HEADS * HEAD_DIM
NSA_KV_W = 6 * NSA_KV_HEADS * HEAD_DIM
NSA_GATE_W = 3 * NSA_HEADS
FOX_W = FOX_HEADS * HEAD_DIM
FOX_QKV_W = 3 * FOX_W
FOX_F_W = FOX_HEADS
MERGE_W = 2 * D_MODEL
SPLIT_Q_A = NSA_Q_W
SPLIT_KV_A = SPLIT_Q_A + NSA_KV_W
SPLIT_G_A = SPLIT_KV_A + NSA_GATE_W
SPLIT_QKV_B = SPLIT_G_A + FOX_QKV_W
SPLIT_F_B = SPLIT_QKV_B + FOX_F_W
IN_WIDTH = SPLIT_F_B + MERGE_W

SCALE = HEAD_DIM ** -0.5
FORCE_SCORE = float(NSA_GROUP + 1)
NEG_INF = -1e30
EPS = 1e-6

LANE = 128
VMEM_LIMIT = 48 * 1024 * 1024


def _group_mean_sq(x, bd):
    sq = x * x
    hi = sq.astype(jnp.bfloat16)
    lo = (sq - hi.astype(jnp.float32)).astype(jnp.bfloat16)
    return (jnp.dot(hi, bd, preferred_element_type=jnp.float32)
            + jnp.dot(lo, bd, preferred_element_type=jnp.float32))


def _head_rms(x, g, bd):
    outs = []
    for c in range(x.shape[1] // LANE):
        xc = x[:, c * LANE:(c + 1) * LANE]
        outs.append(xc * lax.rsqrt(_group_mean_sq(xc, bd) + EPS) * g)
    return outs[0] if len(outs) == 1 else jnp.concatenate(outs, axis=1)


def _front_kernel(x_ref, na_ref, w_ref, bd_ref, fb_ref, gq_a_ref, gk_sel_ref, gk_win_ref,
                  gq_b_ref, gk_b_ref,
                  qa_ref, rows_ref, win_ref, ga_ref, qb_ref, fox_ref, logf_ref, mg_ref):
    x = x_ref[...]
    h = x * lax.rsqrt(jnp.mean(x * x, axis=-1, keepdims=True) + EPS) * na_ref[...]
    hb = h.astype(jnp.bfloat16)
    bd = bd_ref[...]

    def proj(c0, width):
        return jnp.dot(hb, w_ref[:, c0:c0 + width], preferred_element_type=jnp.float32)

    c = 0
    qa_ref[...] = _head_rms(proj(c, NSA_Q_W), gq_a_ref[...], bd)
    c += NSA_Q_W
    rows_ref[:, 0:256] = proj(c, 256)
    rows_ref[:, 256:384] = _head_rms(proj(c + 256, 128), gk_sel_ref[...], bd)
    rows_ref[:, 384:512] = proj(c + 384, 128)
    win_ref[:, 0:128] = _head_rms(proj(c + 512, 128), gk_win_ref[...], bd)
    win_ref[:, 128:256] = proj(c + 640, 128)
    c += NSA_KV_W
    qb_ref[...] = _head_rms(proj(c, FOX_W), gq_b_ref[...], bd)
    fox_ref[:, 0:FOX_W] = _head_rms(proj(c + FOX_W, FOX_W), gk_b_ref[...], bd)
    fox_ref[:, FOX_W:2 * FOX_W] = proj(c + 2 * FOX_W, FOX_W)
    c += FOX_QKV_W
    for j in range(MERGE_W // 512):
        mg_ref[:, j * 512:(j + 1) * 512] = jax.nn.sigmoid(proj(c + j * 512, 512))
    c += MERGE_W
    ga_ref[...] = jax.nn.sigmoid(proj(c, LANE))
    f = proj(c + LANE, LANE)[:, 0:FOX_F_W] + fb_ref[...]
    logf_ref[...] = jnp.minimum(f, 0.0) - jnp.log1p(jnp.exp(-jnp.abs(f)))


def _front(x2d, norm_attn, w_front, bd, fox_f_bias, nsa_q_norm, nsa_k_norm, fox_q_norm, fox_k_norm,
           tm=256):
    n = x2d.shape[0]
    wf = w_front.shape[1]
    two = lambda g: jnp.concatenate([g, g]).reshape(1, LANE)
    row = lambda w: pl.BlockSpec((tm, w), lambda i: (i, 0))
    full = lambda a: pl.BlockSpec(a.shape, lambda i: (0,) * a.ndim)
    args = (x2d, norm_attn.reshape(1, D_MODEL), w_front, bd, fox_f_bias.reshape(1, FOX_F_W),
            two(nsa_q_norm), two(nsa_k_norm[1]), two(nsa_k_norm[2]), two(fox_q_norm), two(fox_k_norm))
    widths = (NSA_Q_W, 512, 256, LANE, FOX_W, 2 * FOX_W, FOX_F_W, MERGE_W)
    return pl.pallas_call(
        _front_kernel,
        grid=(n // tm,),
        in_specs=[row(D_MODEL)] + [full(a) for a in args[1:]],
        out_specs=[row(w) for w in widths],
        out_shape=[jax.ShapeDtypeStruct((n, w), jnp.float32) for w in widths],
        compiler_params=pltpu.CompilerParams(dimension_semantics=("arbitrary",),
                                             vmem_limit_bytes=VMEM_LIMIT),
        name="front",
    )(*args)


FEAT = HEAD_DIM


def _lane(tm):
    return lax.broadcasted_iota(jnp.int32, (tm, LANE), 1)


def _expand_halves(x):
    lo = _lane(x.shape[0]) < HEAD_DIM
    return jnp.where(lo, x, 0.0), jnp.where(lo, pltpu.roll(x, HEAD_DIM, axis=1), 0.0)


def _split3(x):
    hi = x.astype(jnp.bfloat16)
    r = x - hi.astype(jnp.float32)
    mid = r.astype(jnp.bfloat16)
    lo = (r - mid.astype(jnp.float32)).astype(jnp.bfloat16)
    return hi, mid, lo


def _front_attn_kernel(seq_len, x_ref, na_ref, w_ref, bd_ref, fb_ref, gq_a_ref, gk_sel_ref, gk_win_ref,
                       gq_b_ref, gk_b_ref, place_ref,
                       rows_ref, win_ref, fox_ref, logf_ref, mg_ref,
                       qat_ref, gat_ref, ksel_ref, kwin_ref, vselt_ref, vwint_ref, qbt_ref, kb_ref, vbt_ref, rcmp_ref,
                       carry_ref):
    f32, bf16 = jnp.float32, jnp.bfloat16
    tm = x_ref.shape[0]
    i = pl.program_id(0)
    x = x_ref[...]
    h = x * lax.rsqrt(jnp.mean(x * x, axis=-1, keepdims=True) + EPS) * na_ref[...]
    hb = h.astype(bf16)
    bd = bd_ref[...]
    lane = _lane(tm)
    pos = (i * tm + lax.broadcasted_iota(jnp.int32, (tm, LANE), 0)) % seq_len
    kfeat = jnp.where(lane == FEAT, (pos // SEL_BLOCK).astype(f32),
                      jnp.where(lane == FEAT + 1, (pos % SEL_BLOCK).astype(f32), 0.0))

    def proj(c0, width):
        return jnp.dot(hb, w_ref[:, c0:c0 + width], preferred_element_type=f32)

    c = 0
    qa = _head_rms(proj(c, NSA_Q_W), gq_a_ref[...], bd)
    for j in range(NSA_HEADS // 2):
        for s, half in enumerate(_expand_halves(qa[:, j * LANE:(j + 1) * LANE])):
            hd = 2 * j + s
            slope = 2.0 ** -(hd + 1)
            qfeat = jnp.where(lane == FEAT, slope * SEL_BLOCK, jnp.where(lane == FEAT + 1, slope, 0.0))
            qat_ref[hd * LANE:(hd + 1) * LANE, :] = (half * SCALE + qfeat).T.astype(bf16)
    c += NSA_Q_W
    for j in range(2):
        raw = proj(c + j * LANE, LANE)
        rcmp_ref[:, j * LANE:(j + 1) * LANE] = raw
        rows_ref[0, j * LANE:(j + 1) * LANE, :] = raw.T
    ksel = _head_rms(proj(c + 256, 128), gk_sel_ref[...], bd)
    rows_ref[0, 2 * LANE:3 * LANE, :] = ksel.T
    vsel_t = proj(c + 384, 128).T
    rows_ref[0, 3 * LANE:4 * LANE, :] = vsel_t
    kwin = _head_rms(proj(c + 512, 128), gk_win_ref[...], bd)
    win_ref[0, 0:LANE, :] = kwin.T
    vwin_t = proj(c + 640, 128).T
    win_ref[0, LANE:2 * LANE, :] = vwin_t
    for g, (ks_g, kw_g) in enumerate(zip(_expand_halves(ksel), _expand_halves(kwin))):
        ksel_ref[:, g * LANE:(g + 1) * LANE] = (ks_g + kfeat).astype(bf16)
        kwin_ref[:, g * LANE:(g + 1) * LANE] = (kw_g + kfeat).astype(bf16)
    vselt_ref[...] = vsel_t.astype(bf16)
    vwint_ref[...] = vwin_t.astype(bf16)
    c += NSA_KV_W

    qb = _head_rms(proj(c, FOX_W), gq_b_ref[...], bd)
    ones3 = jnp.where((lane >= FEAT) & (lane < FEAT + 3), 1.0, 0.0)
    for j in range(FOX_HEADS // 2):
        for s, half in enumerate(_expand_halves(qb[:, j * LANE:(j + 1) * LANE])):
            hd = 2 * j + s
            qbt_ref[hd * LANE:(hd + 1) * LANE, :] = (half * SCALE + ones3).T.astype(bf16)
    kbn = _head_rms(proj(c + FOX_W, FOX_W), gk_b_ref[...], bd)
    for j in range(FOX_HEADS // 2):
        fox_ref[0, j * LANE:(j + 1) * LANE, :] = kbn[:, j * LANE:(j + 1) * LANE].T
        vb_t = proj(c + 2 * FOX_W + j * LANE, LANE).T
        fox_ref[0, FOX_W + j * LANE:FOX_W + (j + 1) * LANE, :] = vb_t
        vbt_ref[j * LANE:(j + 1) * LANE, :] = vb_t.astype(bf16)
    c += FOX_QKV_W
    for j in range(MERGE_W // 512):
        mg_ref[:, j * 512:(j + 1) * 512] = jax.nn.sigmoid(proj(c + j * 512, 512))
    c += MERGE_W
    gat_ref[...] = jax.nn.sigmoid(proj(c, LANE)).T
    f = proj(c + LANE, LANE) + fb_ref[...]
    lf = jnp.minimum(f, 0.0) - jnp.log1p(jnp.exp(-jnp.abs(f)))
    logf_ref[...] = lf[:, 0:FOX_F_W]

    @pl.when((i * tm) % seq_len == 0)
    def _():
        carry_ref[...] = jnp.zeros_like(carry_ref)

    r_io = lax.broadcasted_iota(jnp.int32, (tm, tm), 0)
    c_io = lax.broadcasted_iota(jnp.int32, (tm, tm), 1)
    tri = jnp.where(c_io <= r_io, 1.0, 0.0).astype(bf16)
    csum = carry_ref[...] + sum(jnp.dot(tri, p, preferred_element_type=f32) for p in _split3(lf))
    carry_ref[...] = csum[tm - 1:tm, :]
    pieces = jnp.concatenate(_split3(-csum), axis=1)
    cfeat = jnp.dot(pieces, place_ref[...], preferred_element_type=f32)
    for j in range(FOX_HEADS // 2):
        for s, half in enumerate(_expand_halves(kbn[:, j * LANE:(j + 1) * LANE])):
            hd = 2 * j + s
            kb_ref[:, hd * LANE:(hd + 1) * LANE] = (half + cfeat[:, hd * LANE:(hd + 1) * LANE]).astype(bf16)


def _fox_feature_placement():
    r = lax.broadcasted_iota(jnp.int32, (3 * LANE, FOX_HEADS * LANE), 0)
    c = lax.broadcasted_iota(jnp.int32, (3 * LANE, FOX_HEADS * LANE), 1)
    s, hd = r // LANE, r % LANE
    return jnp.where((hd < FOX_HEADS) & (c == hd * LANE + FEAT + s), 1.0, 0.0).astype(jnp.bfloat16)


def _front_attn(x2d, seq_len, norm_attn, w_front, bd, fox_f_bias, nsa_q_norm, nsa_k_norm, fox_q_norm,
                fox_k_norm, tm=256):
    n = x2d.shape[0]
    f32, bf16 = jnp.float32, jnp.bfloat16
    two = lambda g: jnp.concatenate([g, g]).reshape(1, LANE)
    row = lambda w: pl.BlockSpec((tm, w), lambda i: (i, 0))
    col = lambda h: pl.BlockSpec((h, tm), lambda i: (0, i))
    full = lambda a: pl.BlockSpec(a.shape, lambda i: (0,) * a.ndim)
    fb = jnp.pad(fox_f_bias, (0, LANE - FOX_F_W)).reshape(1, LANE)
    args = (x2d, norm_attn.reshape(1, D_MODEL), w_front, bd, fb,
            two(nsa_q_norm), two(nsa_k_norm[1]), two(nsa_k_norm[2]), two(fox_q_norm), two(fox_k_norm),
            _fox_feature_placement())
    per_seq = seq_len // tm
    nseq = n // seq_len
    leaf = lambda h: (pl.BlockSpec((1, h, tm), lambda i: (i // per_seq, 0, i % per_seq)), (nseq, h, seq_len), f32)
    outs = [leaf(512), leaf(256), leaf(2 * FOX_W),
            (row(FOX_F_W), (n, FOX_F_W), f32), (row(MERGE_W), (n, MERGE_W), f32),
            (col(NSA_HEADS * LANE), (NSA_HEADS * LANE, n), bf16), (col(LANE), (LANE, n), f32),
            (row(2 * LANE), (n, 2 * LANE), bf16), (row(2 * LANE), (n, 2 * LANE), bf16),
            (col(LANE), (LANE, n), bf16), (col(LANE), (LANE, n), bf16),
            (col(FOX_HEADS * LANE), (FOX_HEADS * LANE, n), bf16),
            (row(FOX_HEADS * LANE), (n, FOX_HEADS * LANE), bf16), (col(FOX_W), (FOX_W, n), bf16),
            (row(2 * LANE), (n, 2 * LANE), f32)]
    return pl.pallas_call(
        functools.partial(_front_attn_kernel, seq_len),
        grid=(n // tm,),
        in_specs=[row(D_MODEL)] + [full(a) for a in args[1:]],
        out_specs=[o[0] for o in outs],
        out_shape=[jax.ShapeDtypeStruct(o[1], o[2]) for o in outs],
        scratch_shapes=[pltpu.VMEM((1, LANE), f32)],
        compiler_params=pltpu.CompilerParams(dimension_semantics=("arbitrary",),
                                             vmem_limit_bytes=VMEM_LIMIT),
        name="front_attn",
    )(*args)


def _front_weights(w_in):
    pad = lambda w: jnp.pad(w, ((0, 0), (0, LANE - w.shape[1])))
    parts = [w_in[:, :SPLIT_KV_A], w_in[:, SPLIT_G_A:SPLIT_QKV_B], w_in[:, SPLIT_F_B:],
             pad(w_in[:, SPLIT_KV_A:SPLIT_G_A]), pad(w_in[:, SPLIT_QKV_B:SPLIT_F_B])]
    return jnp.concatenate(parts, axis=1).astype(jnp.bfloat16)


def _block_diag_mean():
    r = lax.broadcasted_iota(jnp.int32, (LANE, LANE), 0) // HEAD_DIM
    c = lax.broadcasted_iota(jnp.int32, (LANE, LANE), 1) // HEAD_DIM
    return jnp.where(r == c, 1.0 / HEAD_DIM, 0.0).astype(jnp.bfloat16)


def _softmax_step(tiles, carry):
    m, l, acc = carry
    m_new = functools.reduce(jnp.maximum, [jnp.max(s, axis=0, keepdims=True) for s, _ in tiles], m)
    alpha = jnp.exp(m - m_new)
    ps = [jnp.exp(s - m_new) for s, _ in tiles]
    l = alpha * l + sum(jnp.sum(p, axis=0, keepdims=True) for p in ps)
    acc = alpha * acc + sum(jnp.dot(vt, p.astype(jnp.bfloat16), preferred_element_type=jnp.float32)
                            for p, (_, vt) in zip(ps, tiles))
    return m_new, l, acc


TILE_UNROLL = 4


def _tile_loop(n, logits, values, carry, bufs, tk, last_group, groups=None, unroll=TILE_UNROLL):
    tile = lambda kt: (logits(kt), values(kt))
    nu = n // unroll
    count, group_of = (nu, lambda i: i) if groups is None else groups

    def fill(buf, i):
        grp = jnp.minimum(group_of(jnp.minimum(i, jnp.maximum(count - 1, 0))), last_group)
        for j in range(unroll):
            buf[j * tk:(j + 1) * tk, :] = logits(grp * unroll + j)

    def drain(buf, i, c):
        grp = group_of(i)
        return _softmax_step([(buf[j * tk:(j + 1) * tk, :], values(grp * unroll + j)) for j in range(unroll)], c)

    def body(i, c):
        fill(bufs[1], 2 * i + 1)
        c = drain(bufs[0], 2 * i, c)
        fill(bufs[0], 2 * i + 2)
        return drain(bufs[1], 2 * i + 1, c)

    fill(bufs[0], 0)
    carry = lax.fori_loop(0, count // 2, body, carry)
    carry = lax.cond(count % 2 == 1, lambda c: drain(bufs[0], count - 1, c), lambda c: c, carry)
    base = nu * unroll
    size = unroll // 2
    while size >= 1:
        has = (n & size) != 0
        carry = lax.cond(has, functools.partial(
            lambda b, sz, c: _softmax_step([tile(b + u) for u in range(sz)], c), base, size),
            lambda c: c, carry)
        base = base + jnp.where(has, size, 0)
        size //= 2
    return carry


def _softmax_init(w):
    return (jnp.full((1, w), NEG_INF, jnp.float32), jnp.zeros((1, w), jnp.float32),
            jnp.zeros((HEAD_DIM, w), jnp.float32))


def _summarize(xk_ref, xv_ref, pe_ref, w1_ref, w2_ref, nb):
    f32, bf16 = jnp.float32, jnp.bfloat16
    x_refs = (xk_ref, xv_ref)

    def body(l, accs):
        return tuple(
            acc + jnp.dot((x_refs[s][pl.ds(l, nb, stride=CMP_BLOCK), :] + pe_ref[s, pl.ds(l, 1), :]).astype(bf16),
                          w1_ref[s, l], preferred_element_type=f32)
            for s, acc in enumerate(accs))

    accs = lax.fori_loop(0, CMP_BLOCK, body, (jnp.zeros((nb, LANE), f32),) * 2, unroll=16)
    return tuple(jnp.dot(jax.nn.gelu(acc).astype(bf16), w2_ref[s], preferred_element_type=f32)
                 for s, acc in enumerate(accs))


def _compress_kernel(xk_ref, xv_ref, pe_ref, w1_ref, w2_ref, gk_ref, bd_ref, kc_ref, vct_ref):
    f32, bf16 = jnp.float32, jnp.bfloat16
    nb = kc_ref.shape[0]
    kc, vc = _summarize(xk_ref, xv_ref, pe_ref, w1_ref, w2_ref, nb)
    kc = kc * lax.rsqrt(_group_mean_sq(kc, bd_ref[...]) + EPS) * gk_ref[...]
    lane = _lane(nb)
    blk = lax.broadcasted_iota(jnp.int32, (nb, LANE), 0).astype(f32)
    feat = jnp.where(lane == FEAT, blk, jnp.where(lane == FEAT + 1, float(CMP_BLOCK - 1), 0.0))
    for g, half in enumerate(_expand_halves(kc)):
        kc_ref[:, g * LANE:(g + 1) * LANE] = (half + feat).astype(bf16)
    vct_ref[...] = vc.T.astype(bf16)


def _compress_weights(cmp_pe, cmp_w1, cmp_w2):
    def bdiag(w):
        z = jnp.zeros_like(w)
        return jnp.concatenate([jnp.concatenate([w, z], axis=-1), jnp.concatenate([z, w], axis=-1)], axis=-2)
    pe = jnp.concatenate([cmp_pe, cmp_pe], axis=-1)
    return pe, bdiag(cmp_w1).astype(jnp.bfloat16), bdiag(cmp_w2).astype(jnp.bfloat16)


def _compress(rows2d, nb, pe, w1, w2, gk, bd):
    nseq = rows2d.shape[0] // (nb * CMP_BLOCK)
    full = lambda a: pl.BlockSpec(a.shape, lambda b: (0,) * a.ndim)
    args = (rows2d, rows2d, pe, w1, w2, jnp.concatenate([gk, gk]).reshape(1, LANE), bd)
    return pl.pallas_call(
        _compress_kernel,
        grid=(nseq,),
        in_specs=[pl.BlockSpec((nb * CMP_BLOCK, LANE), lambda b: (b, 0)),
                  pl.BlockSpec((nb * CMP_BLOCK, LANE), lambda b: (b, 1))] + [full(a) for a in args[2:]],
        out_specs=[pl.BlockSpec((nb, 2 * LANE), lambda b: (b, 0)), pl.BlockSpec((LANE, nb), lambda b: (0, b))],
        out_shape=[jax.ShapeDtypeStruct((nseq * nb, 2 * LANE), jnp.bfloat16),
                   jax.ShapeDtypeStruct((LANE, nseq * nb), jnp.bfloat16)],
        compiler_params=pltpu.CompilerParams(dimension_semantics=("arbitrary",),
                                             vmem_limit_bytes=VMEM_LIMIT),
        name="nsa_compress",
    )(*args)


def _nsa_prompt_kernel(qt_ref, gat_ref, ksel_ref, vselt_ref, kwin_ref, vwint_ref, kc_ref, vct_ref,
                       o_ref, selb_ref, buf0_ref, buf1_ref, glist_ref):
    f32, bf16 = jnp.float32, jnp.bfloat16
    tq = Q_BLOCK
    w = NSA_GROUP * tq
    g = pl.program_id(1)
    qi = pl.program_id(2)
    nb = kc_ref.shape[0]
    qt = jnp.concatenate([qt_ref[r * LANE:(r + 1) * LANE, :] for r in range(NSA_GROUP)], axis=1)
    qloc = lax.broadcasted_iota(jnp.int32, (1, w), 1) % tq
    qpos = qi * tq + qloc
    krow = lax.broadcasted_iota(jnp.int32, (tq, w), 0)
    causal = krow <= qloc

    nwin = WINDOW // tq
    tiles = []
    for j in range(nwin + 1):
        kt = qi - nwin + j
        k0 = pl.multiple_of(jnp.maximum(kt, 0) * tq, tq)
        s = jnp.dot(kwin_ref[pl.ds(k0, tq), :], qt, preferred_element_type=f32)
        ok = kt >= 0
        if j == 0:
            ok = ok & (krow > qloc)
        elif j == nwin:
            ok = ok & causal
        tiles.append((jnp.where(ok, s, NEG_INF), vwint_ref[:, pl.ds(k0, tq)]))
    m_w, l_w, acc_w = _softmax_step(tiles, _softmax_init(w))
    o_w = acc_w / l_w

    sc = jnp.dot(kc_ref[...], qt, preferred_element_type=f32)
    blk = lax.broadcasted_iota(jnp.int32, (nb, w), 0)
    vis = blk * CMP_BLOCK + (CMP_BLOCK - 1) <= qpos
    sc = jnp.where(vis, sc, NEG_INF)
    pc = jnp.where(vis, jnp.exp(sc - jnp.max(sc, axis=0, keepdims=True)), 0.0)
    pc = pc / jnp.maximum(jnp.sum(pc, axis=0, keepdims=True), 1e-30)
    o_c = jnp.dot(vct_ref[...], pc.astype(bf16), preferred_element_type=f32)

    imp = sum(pc[:, r * tq:(r + 1) * tq] for r in range(NSA_GROUP))
    blk1 = blk[:, 0:tq]
    cur = qpos[:, 0:tq] // SEL_BLOCK
    imp = jnp.where((blk1 == cur) | (blk1 == 0), FORCE_SCORE, jnp.where(blk1 <= cur, imp, -1.0))
    for _ in range(min(SEL_TOPK, nb)):
        mx = jnp.max(imp, axis=0, keepdims=True)
        idx = jnp.min(jnp.where(imp == mx, blk1, BIG_ID), axis=0, keepdims=True)
        imp = jnp.where(blk1 == idx, -jnp.inf, imp)
    selb = jnp.where(imp == -jnp.inf, 0.0, NEG_INF)
    selb_ref[...] = jnp.concatenate([selb] * NSA_GROUP, axis=1)

    blocks_per_group = TILE_UNROLL * (tq // SEL_BLOCK)
    nfull = qi // TILE_UNROLL
    ngroups = jnp.int32(0)
    for j in range(nb // blocks_per_group):
        picked = imp[j * blocks_per_group:(j + 1) * blocks_per_group, :] == -jnp.inf
        hit = jnp.max(jnp.where(picked, 1.0, 0.0)) > 0.5
        glist_ref[ngroups] = j
        ngroups = ngroups + jnp.where(hit & (j < nfull), 1, 0)

    def sel_bias(kt):
        per = tq // SEL_BLOCK
        return jnp.concatenate([jnp.broadcast_to(selb_ref[pl.ds(kt * per + j, 1), :], (SEL_BLOCK, w))
                                for j in range(per)], axis=0)

    def sel_logits(kt):
        k0 = pl.multiple_of(kt * tq, tq)
        return jnp.dot(ksel_ref[pl.ds(k0, tq), :], qt, preferred_element_type=f32) + sel_bias(kt)

    sel_values = lambda kt: vselt_ref[:, pl.ds(pl.multiple_of(kt * tq, tq), tq)]
    last_group = ksel_ref.shape[0] // (tq * TILE_UNROLL) - 1
    carry = _tile_loop(qi, sel_logits, sel_values, _softmax_init(w), (buf0_ref, buf1_ref), tq, last_group,
                       groups=(ngroups, lambda i: glist_ref[i]))
    m_s, l_s, acc_s = _softmax_step([(jnp.where(causal, sel_logits(qi), NEG_INF), sel_values(qi))], carry)
    o_s = acc_s / l_s

    for r in range(NSA_GROUP):
        gate = lambda br: gat_ref[pl.ds((g * NSA_GROUP + r) * 3 + br, 1), :]
        sl = slice(r * tq, (r + 1) * tq)
        o_ref[r * HEAD_DIM:(r + 1) * HEAD_DIM, :] = (
            gate(0) * o_c[:, sl] + gate(1) * o_s[:, sl] + gate(2) * o_w[:, sl])


def _nsa_prompt(qat, gat, ksel, vselt, kwin, vwint, kc, vct, nseq, seq_len):
    n = qat.shape[1]
    nq = seq_len // Q_BLOCK
    nb = seq_len // CMP_BLOCK
    gw = NSA_GROUP * LANE
    return pl.pallas_call(
        _nsa_prompt_kernel,
        grid=(nseq, NSA_KV_HEADS, nq),
        in_specs=[pl.BlockSpec((gw, Q_BLOCK), lambda b, g, q: (g, b * nq + q)),
                  pl.BlockSpec((LANE, Q_BLOCK), lambda b, g, q: (0, b * nq + q)),
                  pl.BlockSpec((seq_len, LANE), lambda b, g, q: (b, g)),
                  pl.BlockSpec((HEAD_DIM, seq_len), lambda b, g, q: (g, b)),
                  pl.BlockSpec((seq_len, LANE), lambda b, g, q: (b, g)),
                  pl.BlockSpec((HEAD_DIM, seq_len), lambda b, g, q: (g, b)),
                  pl.BlockSpec((nb, LANE), lambda b, g, q: (b, g)),
                  pl.BlockSpec((HEAD_DIM, nb), lambda b, g, q: (g, b))],
        out_specs=pl.BlockSpec((NSA_GROUP * HEAD_DIM, Q_BLOCK), lambda b, g, q: (g, b * nq + q)),
        out_shape=jax.ShapeDtypeStruct((NSA_Q_W, n), jnp.float32),
        scratch_shapes=[pltpu.VMEM((nb, NSA_GROUP * Q_BLOCK), jnp.float32)]
        + [pltpu.VMEM((TILE_UNROLL * Q_BLOCK, NSA_GROUP * Q_BLOCK), jnp.float32)] * 2
        + [pltpu.SMEM((nb // (TILE_UNROLL * (Q_BLOCK // SEL_BLOCK)),), jnp.int32)],
        compiler_params=pltpu.CompilerParams(dimension_semantics=("arbitrary", "arbitrary", "arbitrary"),
                                             vmem_limit_bytes=VMEM_LIMIT),
        name="nsa_prompt",
    )(qat, gat, ksel, vselt, kwin, vwint, kc, vct)


FOX_TQ = 512
FOX_TK = 128


def _fox_prompt_kernel(qt_ref, kb_ref, vbt_ref, o_ref, buf0_ref, buf1_ref):
    f32 = jnp.float32
    tq, tk = FOX_TQ, FOX_TK
    qi = pl.program_id(2)
    qt = qt_ref[...]

    logits = lambda kt: jnp.dot(kb_ref[pl.ds(pl.multiple_of(kt * tk, tk), tk), :], qt, preferred_element_type=f32)
    values = lambda kt: vbt_ref[:, pl.ds(pl.multiple_of(kt * tk, tk), tk)]

    ndiag = tq // tk
    last_group = kb_ref.shape[0] // (tk * TILE_UNROLL) - 1
    carry = _tile_loop(qi * ndiag, logits, values, _softmax_init(tq), (buf0_ref, buf1_ref), tk, last_group)
    krow = lax.broadcasted_iota(jnp.int32, (tk, tq), 0)
    qloc = lax.broadcasted_iota(jnp.int32, (tk, tq), 1)
    diag = []
    for j in range(ndiag):
        kt = qi * ndiag + j
        diag.append((jnp.where(krow + j * tk <= qloc, logits(kt), NEG_INF), values(kt)))
    m, l, acc = _softmax_step(diag, carry)
    o_ref[...] = acc / l


def _fox_prompt(qbt, kb, vbt, nseq, seq_len):
    n = qbt.shape[1]
    nq = seq_len // FOX_TQ
    return pl.pallas_call(
        _fox_prompt_kernel,
        grid=(nseq, FOX_HEADS, nq),
        in_specs=[pl.BlockSpec((LANE, FOX_TQ), lambda b, h, q: (h, b * nq + q)),
                  pl.BlockSpec((seq_len, LANE), lambda b, h, q: (b, h)),
                  pl.BlockSpec((HEAD_DIM, seq_len), lambda b, h, q: (h, b))],
        out_specs=pl.BlockSpec((HEAD_DIM, FOX_TQ), lambda b, h, q: (h, b * nq + q)),
        out_shape=jax.ShapeDtypeStruct((FOX_W, n), jnp.float32),
        scratch_shapes=[pltpu.VMEM((TILE_UNROLL * FOX_TK, FOX_TQ), jnp.float32)] * 2,
        compiler_params=pltpu.CompilerParams(dimension_semantics=("arbitrary", "arbitrary", "arbitrary"),
                                             vmem_limit_bytes=VMEM_LIMIT),
        name="fox_prompt",
    )(qbt, kb, vbt)


def _slope_rows(shape, rows_per_head):
    hd = lax.broadcasted_iota(jnp.int32, shape, 0) // rows_per_head
    return pltpu.bitcast((126 - hd) << 23, jnp.float32)


def _pad_rows(x, rows):
    return jnp.concatenate([x, jnp.zeros((rows - x.shape[0], x.shape[1]), x.dtype)], axis=0)


def _nt_dot(a, b):
    return lax.dot_general(a, b, (((1,), (1,)), ((), ())), preferred_element_type=jnp.float32)


def _joint_softmax(parts):
    m = functools.reduce(jnp.maximum, [jnp.max(p, axis=1, keepdims=True) for p in parts])
    es = [jnp.exp(p - m) for p in parts]
    return es, sum(jnp.sum(e, axis=1, keepdims=True) for e in es)


def _nsa_sample_kernel(npages, ns, pt_ref, *refs):
    f32, bf16 = jnp.float32, jnp.bfloat16
    pages = refs[:npages]
    (e_ref, qa_ref, ga_ref, rnew_ref, wnew_ref, state_ref, pe_ref, w1_ref, w2_ref, gk_ref, bd_ref,
     o_ref, bufk_ref, bufv_ref) = refs[npages:]
    page = pages[0].shape[-1]
    past = npages * page
    ncol = NSA_KV_HEADS * npages
    nrow = NSA_HEADS * ns

    for p in range(npages):
        bufk_ref[p * LANE:(p + 1) * LANE, :] = pages[p][0, 0].reshape(LANE, page)
        bufv_ref[p * LANE:(p + 1) * LANE, :] = pages[p][0, 1].reshape(LANE, page)
    kc, vc = _summarize(bufk_ref, bufv_ref, pe_ref, w1_ref, w2_ref, ncol)
    kc = (kc * lax.rsqrt(_group_mean_sq(kc, bd_ref[...]) + EPS) * gk_ref[...]).astype(bf16)
    vc = vc.astype(bf16)

    lane8 = _lane(ns)
    ql = qa_ref[...]
    qrows = []
    for hd in range(NSA_HEADS):
        g = hd // NSA_GROUP
        t = ql[:, (hd // 2) * LANE:(hd // 2 + 1) * LANE]
        if hd % 2 != g:
            t = pltpu.roll(t, HEAD_DIM, axis=1)
        qrows.append(jnp.where(lane8 // HEAD_DIM == g, t, 0.0))
    qb = (jnp.concatenate(qrows, axis=0) * SCALE).astype(bf16)

    def geom(width):
        tok = lax.broadcasted_iota(jnp.int32, (nrow, width), 0) % ns
        col = lax.broadcasted_iota(jnp.int32, (nrow, width), 1)
        return tok, col, _slope_rows((nrow, width), ns)

    q_both = qb + pltpu.roll(qb.astype(f32), HEAD_DIM, axis=1).astype(bf16)
    lane64 = lax.broadcasted_iota(jnp.int32, (nrow, LANE), 1) // HEAD_DIM
    tok, col, slope = geom(ncol)
    own = (lax.broadcasted_iota(jnp.int32, (nrow, ncol), 0) // (NSA_GROUP * ns)) == col % NSA_KV_HEADS
    lcs, blks = [], []
    for half in range(2):
        blk = (col // NSA_KV_HEADS) * 2 + half
        lc = _nt_dot(jnp.where(lane64 == half, q_both, 0.0), kc)
        lc = lc - slope * (past + tok - (blk * CMP_BLOCK + CMP_BLOCK - 1)).astype(f32)
        lcs.append(jnp.where(own, lc, NEG_INF))
        blks.append(blk)
    ecs, lsum = _joint_softmax(lcs)
    pcs = [e / lsum for e in ecs]
    res = [jnp.dot(pc.astype(bf16), vc, preferred_element_type=f32) for pc in pcs]
    o_c = jnp.where(lane64 == 0, res[0], res[1])
    o_c = o_c + pltpu.roll(o_c, HEAD_DIM, axis=1)

    nsel = NSA_KV_HEADS * ns
    col_s = lax.broadcasted_iota(jnp.int32, (nsel, ncol), 1)
    own_s = (lax.broadcasted_iota(jnp.int32, (nsel, ncol), 0) // ns) == col_s % NSA_KV_HEADS
    blks_s = [(col_s // NSA_KV_HEADS) * 2 + half for half in range(2)]
    imps = []
    for half in range(2):
        imp = jnp.concatenate(
            [sum(pcs[half][(g * NSA_GROUP + r) * ns:(g * NSA_GROUP + r + 1) * ns] for r in range(NSA_GROUP))
             for g in range(NSA_KV_HEADS)], axis=0)
        imp = jnp.where(blks_s[half] == 0, FORCE_SCORE, imp)
        imps.append(jnp.where(own_s, imp, -1.0))
    ranks = [jnp.zeros((nsel, ncol), jnp.int32) for _ in range(2)]
    for h2 in range(2):
        for c in range(ncol):
            other = jnp.broadcast_to(imps[h2][:, c:c + 1], (nsel, ncol))
            blk_c = (c // NSA_KV_HEADS) * 2 + h2
            for half in range(2):
                ahead = (other > imps[half]) | ((other == imps[half]) & (blks_s[half] > blk_c))
                ranks[half] = ranks[half] + jnp.where(ahead, 1, 0)
    selexp = 0.0
    for half in range(2):
        sel = jnp.where((ranks[half] < SEL_TOPK - 1) & own_s, 1.0, 0.0)
        sel = jnp.concatenate([sel[g * ns:(g + 1) * ns] for g in range(NSA_KV_HEADS) for _ in range(NSA_GROUP)],
                              axis=0)
        selexp = selexp + jnp.dot(sel.astype(bf16), e_ref[half], preferred_element_type=f32)

    def new_tile(k_new):
        tok, col, slope = geom(LANE)
        s = _nt_dot(qb, _pad_rows(k_new, LANE).astype(bf16))
        return jnp.where(col <= tok, s - slope * (tok - col).astype(f32), NEG_INF)

    def weighted(es, vts, e_new, v_new, lsum):
        acc = sum(_nt_dot(e.astype(bf16), vt.astype(bf16)) for e, vt in zip(es, vts))
        acc = acc + jnp.dot(e_new.astype(bf16), _pad_rows(v_new, LANE).astype(bf16), preferred_element_type=f32)
        return acc / lsum

    tok, col, slope = geom(past)
    ls = jnp.concatenate([jnp.dot(qb, pages[p][0, 2].reshape(LANE, page).astype(bf16), preferred_element_type=f32)
                          for p in range(npages)], axis=1)
    ls = jnp.where(selexp > 0.5, ls - slope * (past + tok - col).astype(f32), NEG_INF)
    (es, en), lsum = _joint_softmax([ls, new_tile(rnew_ref[:, 2 * LANE:3 * LANE])])
    o_s = weighted([es[:, p * page:(p + 1) * page] for p in range(npages)],
                   [pages[p][0, 3].reshape(LANE, page) for p in range(npages)],
                   en, rnew_ref[:, 3 * LANE:4 * LANE], lsum)

    wbuf = state_ref.shape[-1]
    tok, col, slope = geom(wbuf)
    lw = jnp.dot(qb, state_ref[0, 0].reshape(LANE, wbuf).astype(bf16), preferred_element_type=f32)
    lw = jnp.where(col > tok + (wbuf - WINDOW), lw - slope * (wbuf + tok - col).astype(f32), NEG_INF)
    (ew, en), lsum = _joint_softmax([lw, new_tile(wnew_ref[:, 0:LANE])])
    o_w = weighted([ew], [state_ref[0, 1].reshape(LANE, wbuf)], en, wnew_ref[:, LANE:2 * LANE], lsum)

    ga = ga_ref[...]
    gate = lambda br: jnp.concatenate(
        [jnp.broadcast_to(ga[:, hd * 3 + br:hd * 3 + br + 1], (ns, LANE)) for hd in range(NSA_HEADS)], axis=0)
    o = gate(0) * o_c + gate(1) * o_s + gate(2) * o_w
    for j in range(NSA_HEADS // 2):
        g = (2 * j) // NSA_GROUP
        a = o[2 * j * ns:(2 * j + 1) * ns]
        b = o[(2 * j + 1) * ns:(2 * j + 2) * ns]
        if g == 0:
            b = pltpu.roll(b, HEAD_DIM, axis=1)
        else:
            a = pltpu.roll(a, HEAD_DIM, axis=1)
        o_ref[:, j * LANE:(j + 1) * LANE] = jnp.where(lane8 < HEAD_DIM, a, b)


def _compress_weights_t(cmp_pe, cmp_w1, cmp_w2):
    pe, w1, w2 = _compress_weights(cmp_pe.transpose(0, 2, 1), cmp_w1.transpose(0, 2, 1, 3), cmp_w2)
    return pe, w1, w2


def _nsa_sample(page_table, cache_t, qa, ga, rows_new, win_new, state_t, pe, w1, w2, gk, bd):
    db, npages = page_table.shape
    page = cache_t.shape[-1]
    assert page == 2 * CMP_BLOCK == LANE
    ns = qa.shape[0] // db
    past = npages * page
    ncol = NSA_KV_HEADS * npages
    half = lax.broadcasted_iota(jnp.int32, (2, ncol, past), 0)
    col = lax.broadcasted_iota(jnp.int32, (2, ncol, past), 1)
    key = lax.broadcasted_iota(jnp.int32, (2, ncol, past), 2)
    expand = jnp.where(key // SEL_BLOCK == (col // NSA_KV_HEADS) * 2 + half, 1.0, 0.0).astype(jnp.bfloat16)
    gk2 = jnp.concatenate([gk, gk]).reshape(1, LANE)
    full = lambda a: pl.BlockSpec(a.shape, lambda b, pt: (0,) * a.ndim)
    tok = lambda w: pl.BlockSpec((ns, w), lambda b, pt: (b, 0))
    page_specs = [pl.BlockSpec((1,) + cache_t.shape[1:], functools.partial(lambda p, b, pt: (pt[b, p], 0, 0, 0, 0), p))
                  for p in range(npages)]
    return pl.pallas_call(
        functools.partial(_nsa_sample_kernel, npages, ns),
        grid_spec=pltpu.PrefetchScalarGridSpec(
            num_scalar_prefetch=1, grid=(db,),
            in_specs=page_specs + [full(expand), tok(NSA_Q_W), tok(LANE), tok(512), tok(256),
                                   pl.BlockSpec((1,) + state_t.shape[1:], lambda b, pt: (b, 0, 0, 0, 0)),
                                   full(pe), full(w1), full(w2), full(gk2), full(bd)],
            out_specs=tok(NSA_Q_W),
            scratch_shapes=[pltpu.VMEM((npages * LANE, page), jnp.float32),
                            pltpu.VMEM((npages * LANE, page), jnp.float32)]),
        out_shape=jax.ShapeDtypeStruct((db * ns, NSA_Q_W), jnp.float32),
        compiler_params=pltpu.CompilerParams(dimension_semantics=("arbitrary",),
                                             vmem_limit_bytes=VMEM_LIMIT),
        name="nsa_sample",
    )(page_table, *([cache_t] * npages), expand, qa, ga, rows_new, win_new, state_t, pe, w1, w2, gk2, bd)


def _fox_decode_kernel(npages, ns, pt_ref, *refs):
    f32, bf16 = jnp.float32, jnp.bfloat16
    kvt = refs[:npages]
    lft = refs[npages:2 * npages]
    qb_ref, knew_ref, lfnew_ref, o_ref = refs[2 * npages:]
    page = kvt[0].shape[-1]
    nh = FOX_HEADS
    nrow = nh * ns

    r_io = lax.broadcasted_iota(jnp.int32, (page, page), 0)
    c_io = lax.broadcasted_iota(jnp.int32, (page, page), 1)
    triu = jnp.where(r_io <= c_io, 1.0, 0.0).astype(bf16)
    carry = jnp.zeros((nh, 1), f32)
    negc = []
    for t in range(npages + 1):
        lf = lft[t][0] if t < npages else lfnew_ref[0]
        ct = carry + sum(jnp.dot(pc_, triu, preferred_element_type=f32) for pc_ in _split3(lf))
        carry = ct[:, page - 1:page]
        negc.append(jnp.concatenate([jnp.broadcast_to(-ct[hd:hd + 1], (ns, page)) for hd in range(nh)], axis=0))

    head_of_lane = lax.broadcasted_iota(jnp.int32, (ns, FOX_W), 1) // HEAD_DIM
    q = qb_ref[...] * SCALE
    q_bd = jnp.concatenate([jnp.where(head_of_lane == hd, q, 0.0) for hd in range(nh)], axis=0).astype(bf16)

    parts = [jnp.dot(q_bd, kvt[p][0, 0].reshape(FOX_W, page).astype(bf16), preferred_element_type=f32) + negc[p]
             for p in range(npages)]
    tok = lax.broadcasted_iota(jnp.int32, (nrow, page), 0) % ns
    col = lax.broadcasted_iota(jnp.int32, (nrow, page), 1)
    s_new = _nt_dot(q_bd, _pad_rows(knew_ref[:, 0:FOX_W], page).astype(bf16)) + negc[npages]
    parts.append(jnp.where(col <= tok, s_new, NEG_INF))
    es, lsum = _joint_softmax(parts)
    acc = sum(_nt_dot(es[p].astype(bf16), kvt[p][0, 1].reshape(FOX_W, page).astype(bf16)) for p in range(npages))
    acc = acc + jnp.dot(es[npages].astype(bf16), _pad_rows(knew_ref[:, FOX_W:2 * FOX_W], page).astype(bf16),
                        preferred_element_type=f32)
    acc = acc / lsum
    o_ref[...] = sum(jnp.where(head_of_lane == hd, acc[hd * ns:(hd + 1) * ns], 0.0) for hd in range(nh))


def _fox_decode(page_table, cache_kvt, lft, qb, fox_new, lft_new):
    db, npages = page_table.shape
    ns = qb.shape[0] // db
    tok = lambda w: pl.BlockSpec((ns, w), lambda b, pt: (b, 0))
    pg = lambda a: [pl.BlockSpec((1,) + a.shape[1:],
                                 functools.partial(lambda p, nd, b, pt: (pt[b, p],) + (0,) * nd, p, a.ndim - 1))
                    for p in range(npages)]
    return pl.pallas_call(
        functools.partial(_fox_decode_kernel, npages, ns),
        grid_spec=pltpu.PrefetchScalarGridSpec(
            num_scalar_prefetch=1, grid=(db,),
            in_specs=pg(cache_kvt) + pg(lft) + [tok(FOX_W), tok(2 * FOX_W),
                                                pl.BlockSpec((1,) + lft_new.shape[1:], lambda b, pt: (b, 0, 0))],
            out_specs=tok(FOX_W)),
        out_shape=jax.ShapeDtypeStruct((db * ns, FOX_W), jnp.float32),
        compiler_params=pltpu.CompilerParams(dimension_semantics=("arbitrary",),
                                             vmem_limit_bytes=VMEM_LIMIT),
        name="fox_decode",
    )(page_table, *([cache_kvt] * npages), *([lft] * npages), qb, fox_new, lft_new)


BIG_ID = 1 << 20


def _topk_rows(s, k, ids):
    w = ids.shape[1]
    if s.shape[1] > w:
        parts = [_topk_rows(s[:, c:c + w], k, ids) for c in range(0, s.shape[1], w)]
        return jnp.concatenate([p[0] for p in parts], axis=1), jnp.concatenate([p[1] for p in parts], axis=1)
    ids = ids.astype(jnp.float32)
    vals, idxs = [], []
    for _ in range(k):
        m = jnp.max(s, axis=0, keepdims=True)
        idx = jnp.min(jnp.where(s == m, ids, float(BIG_ID)), axis=0, keepdims=True)
        vals.append(m)
        idxs.append(idx)
        s = jnp.where(ids == idx, -jnp.inf, s)
    return jnp.concatenate(vals, axis=0), jnp.concatenate(idxs, axis=0).astype(jnp.int32)


def _pick_rows(sel, table):
    out = jnp.zeros(sel.shape, table.dtype)
    for r in range(table.shape[0]):
        out = jnp.where(sel == r, table[r:r + 1, :], out)
    return out


def _merge_route_kernel(x_ref, on_ref, of_ref, mg_ref, wun_ref, wuf_ref, wo_ref, nf_ref, wqt_ref, sk_ref,
                        x1_ref, h2_ref, i1_ref, i2_ref, g_ref):
    tm = x_ref.shape[0]
    f32, bf16 = jnp.float32, jnp.bfloat16
    tdot = lambda ot, wgt: lax.dot_general(ot.astype(bf16), wgt, (((0,), (0,)), ((), ())),
                                           preferred_element_type=f32)
    a = tdot(on_ref[...], wun_ref[...])
    b = tdot(of_ref[...], wuf_ref[...])
    mixed = mg_ref[:, 0:D_MODEL] * a + mg_ref[:, D_MODEL:2 * D_MODEL] * b
    x1 = x_ref[...] + jnp.dot(mixed.astype(bf16), wo_ref[...], preferred_element_type=f32)
    x1_ref[...] = x1
    h2 = x1 * lax.rsqrt(jnp.mean(x1 * x1, axis=-1, keepdims=True) + EPS) * nf_ref[...]
    h2b = h2.astype(bf16)
    h2_ref[...] = h2b

    nk = PEER_N_KEYS
    key_ids = lax.broadcasted_iota(jnp.int32, (nk, LANE), 0)
    io16 = lax.broadcasted_iota(jnp.int32, (PEER_TOPK, LANE), 0)
    io8 = lax.broadcasted_iota(jnp.int32, (8, LANE), 0)
    cand_ids = jnp.concatenate([io16] + [a_ * PEER_TOPK + io8 for a_ in range(1, 8)]
                               + [(io8 + 8) * PEER_TOPK], axis=0)
    for h in range(PEER_HEADS):
        sv, si = [], []
        for p in range(2):
            hp = 2 * h + p
            qt = lax.dot_general(wqt_ref[hp * PEER_DK_HALF:(hp + 1) * PEER_DK_HALF, :], h2b,
                                 (((1,), (1,)), ((), ())), preferred_element_type=f32)
            st = jnp.dot(sk_ref[hp], qt.astype(bf16), preferred_element_type=f32)
            v, i = _topk_rows(st, PEER_TOPK, key_ids)
            sv.append(v)
            si.append(i)
        s1, s2 = sv
        cand = jnp.concatenate([s1[0:1] + s2] + [s1[a_:a_ + 1] + s2[0:8] for a_ in range(1, 8)]
                               + [s1[8:16] + s2[0:1]], axis=0)
        top, fid = _topk_rows(cand, PEER_TOPK, cand_ids)
        e = jnp.exp(top - jnp.max(top, axis=0, keepdims=True))
        g = e / jnp.sum(e, axis=0, keepdims=True)
        sl = slice(h * PEER_TOPK, (h + 1) * PEER_TOPK)
        i1_ref[:, sl] = _pick_rows(fid >> 4, si[0]).T
        i2_ref[:, sl] = _pick_rows(fid & (PEER_TOPK - 1), si[1]).T
        g_ref[:, sl] = g.T


def _merge_route(x2d, o_nsa, o_fox, mg, w_up_nsa, w_up_fox, w_out, norm_ffn, wq_t, sub_keys, tm=256):
    n = x2d.shape[0]
    row = lambda w: pl.BlockSpec((tm, w), lambda i: (i, 0))
    col = lambda h: pl.BlockSpec((h, tm), lambda i: (0, i))
    full = lambda a: pl.BlockSpec(a.shape, lambda i: (0,) * a.ndim)
    args = (x2d, o_nsa, o_fox, mg, w_up_nsa, w_up_fox, w_out, norm_ffn.reshape(1, D_MODEL), wq_t, sub_keys)
    hk = PEER_HEADS * PEER_TOPK
    return pl.pallas_call(
        _merge_route_kernel,
        grid=(n // tm,),
        in_specs=[row(D_MODEL), col(NSA_Q_W), col(FOX_W), row(MERGE_W)] + [full(a) for a in args[4:]],
        out_specs=[row(D_MODEL), row(D_MODEL), row(hk), row(hk), row(hk)],
        out_shape=[jax.ShapeDtypeStruct((n, D_MODEL), jnp.float32),
                   jax.ShapeDtypeStruct((n, D_MODEL), jnp.bfloat16),
                   jax.ShapeDtypeStruct((n, hk), jnp.int32),
                   jax.ShapeDtypeStruct((n, hk), jnp.int32),
                   jax.ShapeDtypeStruct((n, hk), jnp.float32)],
        compiler_params=pltpu.CompilerParams(dimension_semantics=("arbitrary",),
                                             vmem_limit_bytes=VMEM_LIMIT),
        name="merge_route",
    )(*args)


def _peer_act_kernel(h2_ref, u_ref, i1_ref, i2_ref, act_ref):
    c = pl.program_id(1)
    ec = u_ref.shape[0]

    @pl.when(c == 0)
    def _():
        act_ref[...] = jnp.zeros_like(act_ref)

    a = lax.dot_general(h2_ref[...], u_ref[...], (((1,), (1,)), ((), ())),
                        preferred_element_type=jnp.float32)
    i1 = i1_ref[...]
    i2 = i2_ref[...]
    act = act_ref[...]
    for ii in range(ec // PEER_N_KEYS):
        got = jnp.take_along_axis(a[:, ii * PEER_N_KEYS:(ii + 1) * PEER_N_KEYS], i2, axis=1)
        act = jnp.where(i1 == c * (ec // PEER_N_KEYS) + ii, got, act)
    act_ref[...] = act


def _peer_act(h2b, u_b, i1, i2, tm=512, ec=2048):
    n = h2b.shape[0]
    hk = i1.shape[1]
    return pl.pallas_call(
        _peer_act_kernel,
        grid=(n // tm, u_b.shape[0] // ec),
        in_specs=[pl.BlockSpec((tm, D_MODEL), lambda t, c: (t, 0)),
                  pl.BlockSpec((ec, D_MODEL), lambda t, c: (c, 0)),
                  pl.BlockSpec((tm, hk), lambda t, c: (t, 0)),
                  pl.BlockSpec((tm, hk), lambda t, c: (t, 0))],
        out_specs=pl.BlockSpec((tm, hk), lambda t, c: (t, 0)),
        out_shape=jax.ShapeDtypeStruct((n, hk), jnp.float32),
        compiler_params=pltpu.CompilerParams(dimension_semantics=("arbitrary", "arbitrary"),
                                             vmem_limit_bytes=VMEM_LIMIT),
        name="peer_act",
    )(h2b, u_b, i1, i2)


def _peer_coef_kernel(act_ref, g_ref, i1_ref, i2_ref, c_ref, coef_ref):
    tm = act_ref.shape[0]
    nk = PEER_N_KEYS
    coef_ref[...] = g_ref[...] * jax.nn.gelu(act_ref[...])
    sub = lax.broadcasted_iota(jnp.int32, (nk, i1_ref.shape[1]), 0)

    def token(t):
        r1 = i1_ref[pl.ds(t, 1), :]
        r2 = i2_ref[pl.ds(t, 1), :]
        cf = coef_ref[pl.ds(t, 1), :]
        m1 = jnp.where(r1 == sub, cf, 0.0).astype(jnp.bfloat16)
        m2t = jnp.where(r2 == sub, 1.0, 0.0).astype(jnp.bfloat16)
        return lax.dot_general(m1, m2t, (((1,), (1,)), ((), ())), preferred_element_type=jnp.float32)

    def body(tg, carry):
        t0 = pl.multiple_of(tg * COEF_GROUP, COEF_GROUP)
        ct = jnp.stack([token(t0 + u) for u in range(COEF_GROUP)], axis=0)
        c_ref[:, pl.ds(t0, COEF_GROUP), :] = pltpu.einshape("tij->itj", ct).astype(c_ref.dtype)
        return carry

    lax.fori_loop(0, tm // COEF_GROUP, body, 0)


COEF_GROUP = 16


def _peer_coef(act, g, i1, i2, tm=128):
    n, hk = act.shape
    nk = PEER_N_KEYS
    row = pl.BlockSpec((tm, hk), lambda t: (t, 0))
    return pl.pallas_call(
        _peer_coef_kernel,
        grid=(n // tm,),
        in_specs=[row, row, row, row],
        out_specs=pl.BlockSpec((nk, tm, nk), lambda t: (0, t, 0)),
        out_shape=jax.ShapeDtypeStruct((nk, n, nk), jnp.bfloat16),
        scratch_shapes=[pltpu.VMEM((tm, hk), jnp.float32)],
        compiler_params=pltpu.CompilerParams(dimension_semantics=("arbitrary",),
                                             vmem_limit_bytes=VMEM_LIMIT),
        name="peer_coef",
    )(act, g, i1, i2)


def _peer_out_kernel(c_ref, v_ref, x1_ref, y_ref, acc_ref):
    k = pl.program_id(1)

    @pl.when(k == 0)
    def _():
        acc_ref[...] = x1_ref[...]

    acc = acc_ref[...]
    nk = PEER_N_KEYS
    for p in range(c_ref.shape[0] // 2):
        lhs = jnp.concatenate([c_ref[2 * p], c_ref[2 * p + 1]], axis=1)
        acc = acc + jnp.dot(lhs, v_ref[2 * p * nk:(2 * p + 2) * nk, :], preferred_element_type=jnp.float32)
    acc_ref[...] = acc

    @pl.when(k == pl.num_programs(1) - 1)
    def _():
        y_ref[...] = acc_ref[...]


def _peer_out(c3, v_b, x1, tm=1024, tk=2048):
    nk, n, _ = c3.shape
    ne = nk * nk
    return pl.pallas_call(
        _peer_out_kernel,
        grid=(n // tm, ne // tk),
        in_specs=[pl.BlockSpec((tk // nk, tm, nk), lambda t, k: (k, t, 0)),
                  pl.BlockSpec((tk, D_MODEL), lambda t, k: (k, 0)),
                  pl.BlockSpec((tm, D_MODEL), lambda t, k: (t, 0))],
        out_specs=pl.BlockSpec((tm, D_MODEL), lambda t, k: (t, 0)),
        out_shape=jax.ShapeDtypeStruct((n, D_MODEL), jnp.float32),
        scratch_shapes=[pltpu.VMEM((tm, D_MODEL), jnp.float32)],
        compiler_params=pltpu.CompilerParams(dimension_semantics=("arbitrary", "arbitrary"),
                                             vmem_limit_bytes=VMEM_LIMIT),
        name="peer_out",
    )(c3, v_b, x1)


def _peer_weights(w_up_nsa, w_up_fox, w_out, norm_ffn, peer_w_query, peer_sub_keys, peer_u, peer_v):
    bf16 = jnp.bfloat16
    return dict(w_up_nsa=w_up_nsa.astype(bf16), w_up_fox=w_up_fox.astype(bf16), w_out=w_out.astype(bf16),
                norm_ffn=norm_ffn, wq_t=peer_w_query.T.astype(bf16),
                sub_keys=peer_sub_keys.reshape(2 * PEER_HEADS, PEER_N_KEYS, PEER_DK_HALF).astype(bf16),
                u=peer_u.astype(bf16), v=peer_v.astype(bf16))


def _merge_peer(x2d, o_nsa, o_fox, mg, wts):
    x1, h2b, i1, i2, g = _merge_route(x2d, o_nsa, o_fox, mg, wts['w_up_nsa'], wts['w_up_fox'], wts['w_out'],
                                      wts['norm_ffn'], wts['wq_t'], wts['sub_keys'])
    act = _peer_act(h2b, wts['u'], i1, i2)
    c3 = _peer_coef(act, g, i1, i2)
    return _peer_out(c3, wts['v'], x1)


def kernel(x_prompt, x_sample, cache_nsa, cache_fox_kv, cache_fox_logf, state_nsa_win, page_table,
           norm_attn, w_in, fox_f_bias, nsa_q_norm, nsa_k_norm, fox_q_norm, fox_k_norm,
           cmp_pe, cmp_w1, cmp_w2, w_up_nsa, w_up_fox, w_out, norm_ffn,
           peer_w_query, peer_sub_keys, peer_u, peer_v):
    w_front = _front_weights(w_in)
    bd = _block_diag_mean()
    wts = _peer_weights(w_up_nsa, w_up_fox, w_out, norm_ffn, peer_w_query, peer_sub_keys, peer_u, peer_v)
    cmp_wts = _compress_weights(cmp_pe, cmp_w1, cmp_w2)

    bp, seq, _ = x_prompt.shape
    n_p = bp * seq
    (rows_t, win_t, fox_t, logf_p2d, mg_p, qat, gat, ksel, kwin, vselt, vwint, qbt, kb, vbt, rows_cmp) = _front_attn(
        x_prompt.reshape(n_p, D_MODEL), seq, norm_attn, w_front, bd, fox_f_bias,
        nsa_q_norm, nsa_k_norm, fox_q_norm, fox_k_norm)
    kc, vct = _compress(rows_cmp, seq // CMP_BLOCK, *cmp_wts, nsa_k_norm[0], bd)
    o_nsa_t = _nsa_prompt(qat, gat, ksel, vselt, kwin, vwint, kc, vct, bp, seq)
    o_fox_t = _fox_prompt(qbt, kb, vbt, bp, seq)
    y_p = _merge_peer(x_prompt.reshape(n_p, D_MODEL), o_nsa_t, o_fox_t, mg_p, wts).reshape(x_prompt.shape)
    to_rows = lambda a, *dims: a.reshape(bp, *dims, a.shape[-1]).transpose(0, len(dims) + 1, *range(1, len(dims) + 1))
    nsa_p = to_rows(rows_t, 4, NSA_KV_HEADS, HEAD_DIM)
    fox_p = to_rows(fox_t, 2, FOX_HEADS, HEAD_DIM)
    logf_p = logf_p2d.reshape(bp, seq, FOX_HEADS)
    win_p = to_rows(win_t[:, :, seq - min(WINDOW, seq):], 2, NSA_KV_HEADS, HEAD_DIM)

    db, ns, _ = x_sample.shape
    n_s = db * ns
    n_pool, page = cache_nsa.shape[:2]
    wbuf = state_nsa_win.shape[1]
    qa_s, rows_s, win_s2d, ga_s, qb_s, fox_s2d, logf_s2d, mg_s = _front(
        x_sample.reshape(n_s, D_MODEL), norm_attn, w_front, bd, fox_f_bias,
        nsa_q_norm, nsa_k_norm, fox_q_norm, fox_k_norm)
    o_nsa_s = _nsa_sample(page_table, cache_nsa.transpose(0, 2, 3, 4, 1), qa_s, ga_s, rows_s, win_s2d,
                          state_nsa_win.transpose(0, 2, 3, 4, 1),
                          *_compress_weights_t(cmp_pe, cmp_w1, cmp_w2), nsa_k_norm[0], bd)
    lft_new = jnp.pad(logf_s2d.reshape(db, ns, FOX_HEADS).transpose(0, 2, 1), ((0, 0), (0, 0), (0, page - ns)))
    o_fox_s = _fox_decode(page_table, cache_fox_kv.transpose(0, 2, 3, 4, 1), cache_fox_logf.transpose(0, 2, 1),
                          qb_s, fox_s2d, lft_new)
    y_s = _merge_peer(x_sample.reshape(n_s, D_MODEL), o_nsa_s.T, o_fox_s.T, mg_s, wts).reshape(x_sample.shape)
    nsa_s = rows_s.reshape(db, ns, 4, NSA_KV_HEADS, HEAD_DIM)
    fox_s = fox_s2d.reshape(db, ns, 2, FOX_HEADS, HEAD_DIM)
    logf_s = logf_s2d.reshape(db, ns, FOX_HEADS)
    win_s = jnp.concatenate([state_nsa_win[:, ns:], win_s2d.reshape(db, ns, 2, NSA_KV_HEADS, HEAD_DIM)], axis=1)
    return (y_p, y_s, nsa_p, fox_p, logf_p, win_p, nsa_s, fox_s, logf_s, win_s)
```

```python
import functools

import jax
import jax.numpy as jnp
from jax import lax
from jax.experimental import pallas as pl
from jax.experimental.pallas import tpu as pltpu

D_MODEL = 1024
HEAD_DIM = 64
NSA_HEADS = 8
NSA_KV_HEADS = 2
NSA_GROUP = NSA_HEADS // NSA_KV_HEADS
CMP_BLOCK = 64
SEL_BLOCK = CMP_BLOCK
SEL_TOPK = 16
WINDOW = 512
FOX_HEADS = 8
Q_BLOCK = 128
PEER_HEADS = 8
PEER_N_KEYS = 128
PEER_DK = 256
PEER_DK_HALF = PEER_DK // 2
PEER_TOPK = 16
PEER_CHUNK = 256

NSA_Q_W = NSA_HEADS * HEAD_DIM
NSA_KV_W = 6 * NSA_KV_HEADS * HEAD_DIM
NSA_GATE_W = 3 * NSA_HEADS
FOX_W = FOX_HEADS * HEAD_DIM
FOX_QKV_W = 3 * FOX_W
FOX_F_W = FOX_HEADS
MERGE_W = 2 * D_MODEL
SPLIT_Q_A = NSA_Q_W
SPLIT_KV_A = SPLIT_Q_A + NSA_KV_W
SPLIT_G_A = SPLIT_KV_A + NSA_GATE_W
SPLIT_QKV_B = SPLIT_G_A + FOX_QKV_W
SPLIT_F_B = SPLIT_QKV_B + FOX_F_W
IN_WIDTH = SPLIT_F_B + MERGE_W

SCALE = HEAD_DIM ** -0.5
FORCE_SCORE = float(NSA_GROUP + 1)
NEG_INF = -1e30
EPS = 1e-6

LANE = 128
VMEM_LIMIT = 48 * 1024 * 1024


def _group_mean_sq(x, bd):
    sq = x * x
    hi = sq.astype(jnp.bfloat16)
    lo = (sq - hi.astype(jnp.float32)).astype(jnp.bfloat16)
    return (jnp.dot(hi, bd, preferred_element_type=jnp.float32)
            + jnp.dot(lo, bd, preferred_element_type=jnp.float32))


def _head_rms(x, g, bd):
    outs = []
    for c in range(x.shape[1] // LANE):
        xc = x[:, c * LANE:(c + 1) * LANE]
        outs.append(xc * lax.rsqrt(_group_mean_sq(xc, bd) + EPS) * g)
    return outs[0] if len(outs) == 1 else jnp.concatenate(outs, axis=1)


def _front_kernel(x_ref, na_ref, w_ref, bd_ref, fb_ref, gq_a_ref, gk_sel_ref, gk_win_ref,
                  gq_b_ref, gk_b_ref,
                  qa_ref, rows_ref, win_ref, ga_ref, qb_ref, fox_ref, logf_ref, mg_ref):
    x = x_ref[...]
    h = x * lax.rsqrt(jnp.mean(x * x, axis=-1, keepdims=True) + EPS) * na_ref[...]
    hb = h.astype(jnp.bfloat16)
    bd = bd_ref[...]

    def proj(c0, width):
        return jnp.dot(hb, w_ref[:, c0:c0 + width], preferred_element_type=jnp.float32)

    c = 0
    qa_ref[...] = _head_rms(proj(c, NSA_Q_W), gq_a_ref[...], bd)
    c += NSA_Q_W
    rows_ref[:, 0:256] = proj(c, 256)
    rows_ref[:, 256:384] = _head_rms(proj(c + 256, 128), gk_sel_ref[...], bd)
    rows_ref[:, 384:512] = proj(c + 384, 128)
    win_ref[:, 0:128] = _head_rms(proj(c + 512, 128), gk_win_ref[...], bd)
    win_ref[:, 128:256] = proj(c + 640, 128)
    c += NSA_KV_W
    qb_ref[...] = _head_rms(proj(c, FOX_W), gq_b_ref[...], bd)
    fox_ref[:, 0:FOX_W] = _head_rms(proj(c + FOX_W, FOX_W), gk_b_ref[...], bd)
    fox_ref[:, FOX_W:2 * FOX_W] = proj(c + 2 * FOX_W, FOX_W)
    c += FOX_QKV_W
    for j in range(MERGE_W // 512):
        mg_ref[:, j * 512:(j + 1) * 512] = jax.nn.sigmoid(proj(c + j * 512, 512))
    c += MERGE_W
    ga_ref[...] = jax.nn.sigmoid(proj(c, LANE))
    f = proj(c + LANE, LANE)[:, 0:FOX_F_W] + fb_ref[...]
    logf_ref[...] = jnp.minimum(f, 0.0) - jnp.log1p(jnp.exp(-jnp.abs(f)))


def _front(x2d, norm_attn, w_front, bd, fox_f_bias, nsa_q_norm, nsa_k_norm, fox_q_norm, fox_k_norm,
           tm=256):
    n = x2d.shape[0]
    wf = w_front.shape[1]
    two = lambda g: jnp.concatenate([g, g]).reshape(1, LANE)
    row = lambda w: pl.BlockSpec((tm, w), lambda i: (i, 0))
    full = lambda a: pl.BlockSpec(a.shape, lambda i: (0,) * a.ndim)
    args = (x2d, norm_attn.reshape(1, D_MODEL), w_front, bd, fox_f_bias.reshape(1, FOX_F_W),
            two(nsa_q_norm), two(nsa_k_norm[1]), two(nsa_k_norm[2]), two(fox_q_norm), two(fox_k_norm))
    widths = (NSA_Q_W, 512, 256, LANE, FOX_W, 2 * FOX_W, FOX_F_W, MERGE_W)
    return pl.pallas_call(
        _front_kernel,
        grid=(n // tm,),
        in_specs=[row(D_MODEL)] + [full(a) for a in args[1:]],
        out_specs=[row(w) for w in widths],
        out_shape=[jax.ShapeDtypeStruct((n, w), jnp.float32) for w in widths],
        compiler_params=pltpu.CompilerParams(dimension_semantics=("arbitrary",),
                                             vmem_limit_bytes=VMEM_LIMIT),
        name="front",
    )(*args)


FEAT = HEAD_DIM


def _lane(tm):
    return lax.broadcasted_iota(jnp.int32, (tm, LANE), 1)


def _expand_halves(x):
    lo = _lane(x.shape[0]) < HEAD_DIM
    return jnp.where(lo, x, 0.0), jnp.where(lo, pltpu.roll(x, HEAD_DIM, axis=1), 0.0)


def _split3(x):
    hi = x.astype(jnp.bfloat16)
    r = x - hi.astype(jnp.float32)
    mid = r.astype(jnp.bfloat16)
    lo = (r - mid.astype(jnp.float32)).astype(jnp.bfloat16)
    return hi, mid, lo


def _front_attn_kernel(seq_len, x_ref, na_ref, w_ref, bd_ref, fb_ref, gq_a_ref, gk_sel_ref, gk_win_ref,
                       gq_b_ref, gk_b_ref, place_ref,
                       rows_ref, win_ref, fox_ref, logf_ref, mg_ref,
                       qat_ref, gat_ref, ksel_ref, kwin_ref, vselt_ref, vwint_ref, qbt_ref, kb_ref, vbt_ref, rcmp_ref,
                       carry_ref):
    f32, bf16 = jnp.float32, jnp.bfloat16
    tm = x_ref.shape[0]
    i = pl.program_id(0)
    x = x_ref[...]
    h = x * lax.rsqrt(jnp.mean(x * x, axis=-1, keepdims=True) + EPS) * na_ref[...]
    hb = h.astype(bf16)
    bd = bd_ref[...]
    lane = _lane(tm)
    pos = (i * tm + lax.broadcasted_iota(jnp.int32, (tm, LANE), 0)) % seq_len
    kfeat = jnp.where(lane == FEAT, (pos // SEL_BLOCK).astype(f32),
                      jnp.where(lane == FEAT + 1, (pos % SEL_BLOCK).astype(f32), 0.0))

    def proj(c0, width):
        return jnp.dot(hb, w_ref[:, c0:c0 + width], preferred_element_type=f32)

    c = 0
    qa = _head_rms(proj(c, NSA_Q_W), gq_a_ref[...], bd)
    for j in range(NSA_HEADS // 2):
        for s, half in enumerate(_expand_halves(qa[:, j * LANE:(j + 1) * LANE])):
            hd = 2 * j + s
            slope = 2.0 ** -(hd + 1)
            qfeat = jnp.where(lane == FEAT, slope * SEL_BLOCK, jnp.where(lane == FEAT + 1, slope, 0.0))
            qat_ref[hd * LANE:(hd + 1) * LANE, :] = (half * SCALE + qfeat).T.astype(bf16)
    c += NSA_Q_W
    for j in range(2):
        raw = proj(c + j * LANE, LANE)
        rcmp_ref[:, j * LANE:(j + 1) * LANE] = raw
        rows_ref[0, j * LANE:(j + 1) * LANE, :] = raw.T
    ksel = _head_rms(proj(c + 256, 128), gk_sel_ref[...], bd)
    rows_ref[0, 2 * LANE:3 * LANE, :] = ksel.T
    vsel_t = proj(c + 384, 128).T
    rows_ref[0, 3 * LANE:4 * LANE, :] = vsel_t
    kwin = _head_rms(proj(c + 512, 128), gk_win_ref[...], bd)
    win_ref[0, 0:LANE, :] = kwin.T
    vwin_t = proj(c + 640, 128).T
    win_ref[0, LANE:2 * LANE, :] = vwin_t
    for g, (ks_g, kw_g) in enumerate(zip(_expand_halves(ksel), _expand_halves(kwin))):
        ksel_ref[:, g * LANE:(g + 1) * LANE] = (ks_g + kfeat).astype(bf16)
        kwin_ref[:, g * LANE:(g + 1) * LANE] = (kw_g + kfeat).astype(bf16)
    vselt_ref[...] = vsel_t.astype(bf16)
    vwint_ref[...] = vwin_t.astype(bf16)
    c += NSA_KV_W

    qb = _head_rms(proj(c, FOX_W), gq_b_ref[...], bd)
    ones3 = jnp.where((lane >= FEAT) & (lane < FEAT + 3), 1.0, 0.0)
    for j in range(FOX_HEADS // 2):
        for s, half in enumerate(_expand_halves(qb[:, j * LANE:(j + 1) * LANE])):
            hd = 2 * j + s
            qbt_ref[hd * LANE:(hd + 1) * LANE, :] = (half * SCALE + ones3).T.astype(bf16)
    kbn = _head_rms(proj(c + FOX_W, FOX_W), gk_b_ref[...], bd)
    for j in range(FOX_HEADS // 2):
        fox_ref[0, j * LANE:(j + 1) * LANE, :] = kbn[:, j * LANE:(j + 1) * LANE].T
        vb_t = proj(c + 2 * FOX_W + j * LANE, LANE).T
        fox_ref[0, FOX_W + j * LANE:FOX_W + (j + 1) * LANE, :] = vb_t
        vbt_ref[j * LANE:(j + 1) * LANE, :] = vb_t.astype(bf16)
    c += FOX_QKV_W
    for j in range(MERGE_W // 512):
        mg_ref[:, j * 512:(j + 1) * 512] = jax.nn.sigmoid(proj(c + j * 512, 512))
    c += MERGE_W
    gat_ref[...] = jax.nn.sigmoid(proj(c, LANE)).T
    f = proj(c + LANE, LANE) + fb_ref[...]
    lf = jnp.minimum(f, 0.0) - jnp.log1p(jnp.exp(-jnp.abs(f)))
    logf_ref[...] = lf[:, 0:FOX_F_W]

    @pl.when((i * tm) % seq_len == 0)
    def _():
        carry_ref[...] = jnp.zeros_like(carry_ref)

    r_io = lax.broadcasted_iota(jnp.int32, (tm, tm), 0)
    c_io = lax.broadcasted_iota(jnp.int32, (tm, tm), 1)
    tri = jnp.where(c_io <= r_io, 1.0, 0.0).astype(bf16)
    csum = carry_ref[...] + sum(jnp.dot(tri, p, preferred_element_type=f32) for p in _split3(lf))
    carry_ref[...] = csum[tm - 1:tm, :]
    pieces = jnp.concatenate(_split3(-csum), axis=1)
    cfeat = jnp.dot(pieces, place_ref[...], preferred_element_type=f32)
    for j in range(FOX_HEADS // 2):
        for s, half in enumerate(_expand_halves(kbn[:, j * LANE:(j + 1) * LANE])):
            hd = 2 * j + s
            kb_ref[:, hd * LANE:(hd + 1) * LANE] = (half + cfeat[:, hd * LANE:(hd + 1) * LANE]).astype(bf16)


def _fox_feature_placement():
    r = lax.broadcasted_iota(jnp.int32, (3 * LANE, FOX_HEADS * LANE), 0)
    c = lax.broadcasted_iota(jnp.int32, (3 * LANE, FOX_HEADS * LANE), 1)
    s, hd = r // LANE, r % LANE
    return jnp.where((hd < FOX_HEADS) & (c == hd * LANE + FEAT + s), 1.0, 0.0).astype(jnp.bfloat16)


def _front_attn(x2d, seq_len, norm_attn, w_front, bd, fox_f_bias, nsa_q_norm, nsa_k_norm, fox_q_norm,
                fox_k_norm, tm=256):
    n = x2d.shape[0]
    f32, bf16 = jnp.float32, jnp.bfloat16
    two = lambda g: jnp.concatenate([g, g]).reshape(1, LANE)
    row = lambda w: pl.BlockSpec((tm, w), lambda i: (i, 0))
    col = lambda h: pl.BlockSpec((h, tm), lambda i: (0, i))
    full = lambda a: pl.BlockSpec(a.shape, lambda i: (0,) * a.ndim)
    fb = jnp.pad(fox_f_bias, (0, LANE - FOX_F_W)).reshape(1, LANE)
    args = (x2d, norm_attn.reshape(1, D_MODEL), w_front, bd, fb,
            two(nsa_q_norm), two(nsa_k_norm[1]), two(nsa_k_norm[2]), two(fox_q_norm), two(fox_k_norm),
            _fox_feature_placement())
    per_seq = seq_len // tm
    nseq = n // seq_len
    leaf = lambda h: (pl.BlockSpec((1, h, tm), lambda i: (i // per_seq, 0, i % per_seq)), (nseq, h, seq_len), f32)
    outs = [leaf(512), leaf(256), leaf(2 * FOX_W),
            (row(FOX_F_W), (n, FOX_F_W), f32), (row(MERGE_W), (n, MERGE_W), f32),
            (col(NSA_HEADS * LANE), (NSA_HEADS * LANE, n), bf16), (col(LANE), (LANE, n), f32),
            (row(2 * LANE), (n, 2 * LANE), bf16), (row(2 * LANE), (n, 2 * LANE), bf16),
            (col(LANE), (LANE, n), bf16), (col(LANE), (LANE, n), bf16),
            (col(FOX_HEADS * LANE), (FOX_HEADS * LANE, n), bf16),
            (row(FOX_HEADS * LANE), (n, FOX_HEADS * LANE), bf16), (col(FOX_W), (FOX_W, n), bf16),
            (row(2 * LANE), (n, 2 * LANE), f32)]
    return pl.pallas_call(
        functools.partial(_front_attn_kernel, seq_len),
        grid=(n // tm,),
        in_specs=[row(D_MODEL)] + [full(a) for a in args[1:]],
        out_specs=[o[0] for o in outs],
        out_shape=[jax.ShapeDtypeStruct(o[1], o[2]) for o in outs],
        scratch_shapes=[pltpu.VMEM((1, LANE), f32)],
        compiler_params=pltpu.CompilerParams(dimension_semantics=("arbitrary",),
                                             vmem_limit_bytes=VMEM_LIMIT),
        name="front_attn",
    )(*args)


def _front_weights(w_in):
    pad = lambda w: jnp.pad(w, ((0, 0), (0, LANE - w.shape[1])))
    parts = [w_in[:, :SPLIT_KV_A], w_in[:, SPLIT_G_A:SPLIT_QKV_B], w_in[:, SPLIT_F_B:],
             pad(w_in[:, SPLIT_KV_A:SPLIT_G_A]), pad(w_in[:, SPLIT_QKV_B:SPLIT_F_B])]
    return jnp.concatenate(parts, axis=1).astype(jnp.bfloat16)


def _block_diag_mean():
    r = lax.broadcasted_iota(jnp.int32, (LANE, LANE), 0) // HEAD_DIM
    c = lax.broadcasted_iota(jnp.int32, (LANE, LANE), 1) // HEAD_DIM
    return jnp.where(r == c, 1.0 / HEAD_DIM, 0.0).astype(jnp.bfloat16)


def _softmax_step(tiles, carry):
    m, l, acc = carry
    m_new = functools.reduce(jnp.maximum, [jnp.max(s, axis=0, keepdims=True) for s, _ in tiles], m)
    alpha = jnp.exp(m - m_new)
    ps = [jnp.exp(s - m_new) for s, _ in tiles]
    l = alpha * l + sum(jnp.sum(p, axis=0, keepdims=True) for p in ps)
    acc = alpha * acc + sum(jnp.dot(vt, p.astype(jnp.bfloat16), preferred_element_type=jnp.float32)
                            for p, (_, vt) in zip(ps, tiles))
    return m_new, l, acc


TILE_UNROLL = 4


def _tile_loop(n, logits, values, carry, bufs, tk, last_group, groups=None, unroll=TILE_UNROLL):
    tile = lambda kt: (logits(kt), values(kt))
    nu = n // unroll
    count, group_of = (nu, lambda i: i) if groups is None else groups

    def fill(buf, i):
        grp = jnp.minimum(group_of(jnp.minimum(i, jnp.maximum(count - 1, 0))), last_group)
        for j in range(unroll):
            buf[j * tk:(j + 1) * tk, :] = logits(grp * unroll + j)

    def drain(buf, i, c):
        grp = group_of(i)
        return _softmax_step([(buf[j * tk:(j + 1) * tk, :], values(grp * unroll + j)) for j in range(unroll)], c)

    def body(i, c):
        fill(bufs[1], 2 * i + 1)
        c = drain(bufs[0], 2 * i, c)
        fill(bufs[0], 2 * i + 2)
        return drain(bufs[1], 2 * i + 1, c)

    fill(bufs[0], 0)
    carry = lax.fori_loop(0, count // 2, body, carry)
    carry = lax.cond(count % 2 == 1, lambda c: drain(bufs[0], count - 1, c), lambda c: c, carry)
    base = nu * unroll
    size = unroll // 2
    while size >= 1:
        has = (n & size) != 0
        carry = lax.cond(has, functools.partial(
            lambda b, sz, c: _softmax_step([tile(b + u) for u in range(sz)], c), base, size),
            lambda c: c, carry)
        base = base + jnp.where(has, size, 0)
        size //= 2
    return carry


def _softmax_init(w):
    return (jnp.full((1, w), NEG_INF, jnp.float32), jnp.zeros((1, w), jnp.float32),
            jnp.zeros((HEAD_DIM, w), jnp.float32))


def _summarize(xk_ref, xv_ref, pe_ref, w1_ref, w2_ref, nb):
    f32, bf16 = jnp.float32, jnp.bfloat16
    x_refs = (xk_ref, xv_ref)

    def body(l, accs):
        return tuple(
            acc + jnp.dot((x_refs[s][pl.ds(l, nb, stride=CMP_BLOCK), :] + pe_ref[s, pl.ds(l, 1), :]).astype(bf16),
                          w1_ref[s, l], preferred_element_type=f32)
            for s, acc in enumerate(accs))

    accs = lax.fori_loop(0, CMP_BLOCK, body, (jnp.zeros((nb, LANE), f32),) * 2, unroll=16)
    return tuple(jnp.dot(jax.nn.gelu(acc).astype(bf16), w2_ref[s], preferred_element_type=f32)
                 for s, acc in enumerate(accs))


def _compress_kernel(xk_ref, xv_ref, pe_ref, w1_ref, w2_ref, gk_ref, bd_ref, kc_ref, vct_ref):
    f32, bf16 = jnp.float32, jnp.bfloat16
    nb = kc_ref.shape[0]
    kc, vc = _summarize(xk_ref, xv_ref, pe_ref, w1_ref, w2_ref, nb)
    kc = kc * lax.rsqrt(_group_mean_sq(kc, bd_ref[...]) + EPS) * gk_ref[...]
    lane = _lane(nb)
    blk = lax.broadcasted_iota(jnp.int32, (nb, LANE), 0).astype(f32)
    feat = jnp.where(lane == FEAT, blk, jnp.where(lane == FEAT + 1, float(CMP_BLOCK - 1), 0.0))
    for g, half in enumerate(_expand_halves(kc)):
        kc_ref[:, g * LANE:(g + 1) * LANE] = (half + feat).astype(bf16)
    vct_ref[...] = vc.T.astype(bf16)


def _compress_weights(cmp_pe, cmp_w1, cmp_w2):
    def bdiag(w):
        z = jnp.zeros_like(w)
        return jnp.concatenate([jnp.concatenate([w, z], axis=-1), jnp.concatenate([z, w], axis=-1)], axis=-2)
    pe = jnp.concatenate([cmp_pe, cmp_pe], axis=-1)
    return pe, bdiag(cmp_w1).astype(jnp.bfloat16), bdiag(cmp_w2).astype(jnp.bfloat16)


def _compress(rows2d, nb, pe, w1, w2, gk, bd):
    nseq = rows2d.shape[0] // (nb * CMP_BLOCK)
    full = lambda a: pl.BlockSpec(a.shape, lambda b: (0,) * a.ndim)
    args = (rows2d, rows2d, pe, w1, w2, jnp.concatenate([gk, gk]).reshape(1, LANE), bd)
    return pl.pallas_call(
        _compress_kernel,
        grid=(nseq,),
        in_specs=[pl.BlockSpec((nb * CMP_BLOCK, LANE), lambda b: (b, 0)),
                  pl.BlockSpec((nb * CMP_BLOCK, LANE), lambda b: (b, 1))] + [full(a) for a in args[2:]],
        out_specs=[pl.BlockSpec((nb, 2 * LANE), lambda b: (b, 0)), pl.BlockSpec((LANE, nb), lambda b: (0, b))],
        out_shape=[jax.ShapeDtypeStruct((nseq * nb, 2 * LANE), jnp.bfloat16),
                   jax.ShapeDtypeStruct((LANE, nseq * nb), jnp.bfloat16)],
        compiler_params=pltpu.CompilerParams(dimension_semantics=("arbitrary",),
                                             vmem_limit_bytes=VMEM_LIMIT),
        name="nsa_compress",
    )(*args)


def _nsa_prompt_kernel(qt_ref, gat_ref, ksel_ref, vselt_ref, kwin_ref, vwint_ref, kc_ref, vct_ref,
                       o_ref, selb_ref, buf0_ref, buf1_ref, glist_ref):
    f32, bf16 = jnp.float32, jnp.bfloat16
    tq = Q_BLOCK
    w = NSA_GROUP * tq
    g = pl.program_id(1)
    qi = pl.program_id(2)
    nb = kc_ref.shape[0]
    qt = jnp.concatenate([qt_ref[r * LANE:(r + 1) * LANE, :] for r in range(NSA_GROUP)], axis=1)
    qloc = lax.broadcasted_iota(jnp.int32, (1, w), 1) % tq
    qpos = qi * tq + qloc
    krow = lax.broadcasted_iota(jnp.int32, (tq, w), 0)
    causal = krow <= qloc

    nwin = WINDOW // tq
    tiles = []
    for j in range(nwin + 1):
        kt = qi - nwin + j
        k0 = pl.multiple_of(jnp.maximum(kt, 0) * tq, tq)
        s = jnp.dot(kwin_ref[pl.ds(k0, tq), :], qt, preferred_element_type=f32)
        ok = kt >= 0
        if j == 0:
            ok = ok & (krow > qloc)
        elif j == nwin:
            ok = ok & causal
        tiles.append((jnp.where(ok, s, NEG_INF), vwint_ref[:, pl.ds(k0, tq)]))
    m_w, l_w, acc_w = _softmax_step(tiles, _softmax_init(w))
    o_w = acc_w / l_w

    sc = jnp.dot(kc_ref[...], qt, preferred_element_type=f32)
    blk = lax.broadcasted_iota(jnp.int32, (nb, w), 0)
    vis = blk * CMP_BLOCK + (CMP_BLOCK - 1) <= qpos
    sc = jnp.where(vis, sc, NEG_INF)
    pc = jnp.where(vis, jnp.exp(sc - jnp.max(sc, axis=0, keepdims=True)), 0.0)
    pc = pc / jnp.maximum(jnp.sum(pc, axis=0, keepdims=True), 1e-30)
    o_c = jnp.dot(vct_ref[...], pc.astype(bf16), preferred_element_type=f32)

    imp = sum(pc[:, r * tq:(r + 1) * tq] for r in range(NSA_GROUP))
    blk1 = blk[:, 0:tq]
    cur = qpos[:, 0:tq] // SEL_BLOCK
    imp = jnp.where((blk1 == cur) | (blk1 == 0), FORCE_SCORE, jnp.where(blk1 <= cur, imp, -1.0))
    for _ in range(min(SEL_TOPK, nb)):
        mx = jnp.max(imp, axis=0, keepdims=True)
        idx = jnp.min(jnp.where(imp == mx, blk1, BIG_ID), axis=0, keepdims=True)
        imp = jnp.where(blk1 == idx, -jnp.inf, imp)
    selb = jnp.where(imp == -jnp.inf, 0.0, NEG_INF)
    selb_ref[...] = jnp.concatenate([selb] * NSA_GROUP, axis=1)

    blocks_per_group = TILE_UNROLL * (tq // SEL_BLOCK)
    nfull = qi // TILE_UNROLL
    ngroups = jnp.int32(0)
    for j in range(nb // blocks_per_group):
        picked = imp[j * blocks_per_group:(j + 1) * blocks_per_group, :] == -jnp.inf
        hit = jnp.max(jnp.where(picked, 1.0, 0.0)) > 0.5
        glist_ref[ngroups] = j
        ngroups = ngroups + jnp.where(hit & (j < nfull), 1, 0)

    def sel_bias(kt):
        per = tq // SEL_BLOCK
        return jnp.concatenate([jnp.broadcast_to(selb_ref[pl.ds(kt * per + j, 1), :], (SEL_BLOCK, w))
                                for j in range(per)], axis=0)

    def sel_logits(kt):
        k0 = pl.multiple_of(kt * tq, tq)
        return jnp.dot(ksel_ref[pl.ds(k0, tq), :], qt, preferred_element_type=f32) + sel_bias(kt)

    sel_values = lambda kt: vselt_ref[:, pl.ds(pl.multiple_of(kt * tq, tq), tq)]
    last_group = ksel_ref.shape[0] // (tq * TILE_UNROLL) - 1
    carry = _tile_loop(qi, sel_logits, sel_values, _softmax_init(w), (buf0_ref, buf1_ref), tq, last_group,
                       groups=(ngroups, lambda i: glist_ref[i]))
    m_s, l_s, acc_s = _softmax_step([(jnp.where(causal, sel_logits(qi), NEG_INF), sel_values(qi))], carry)
    o_s = acc_s / l_s

    for r in range(NSA_GROUP):
        gate = lambda br: gat_ref[pl.ds((g * NSA_GROUP + r) * 3 + br, 1), :]
        sl = slice(r * tq, (r + 1) * tq)
        o_ref[r * HEAD_DIM:(r + 1) * HEAD_DIM, :] = (
            gate(0) * o_c[:, sl] + gate(1) * o_s[:, sl] + gate(2) * o_w[:, sl])


def _nsa_prompt(qat, gat, ksel, vselt, kwin, vwint, kc, vct, nseq, seq_len):
    n = qat.shape[1]
    nq = seq_len // Q_BLOCK
    nb = seq_len // CMP_BLOCK
    gw = NSA_GROUP * LANE
    return pl.pallas_call(
        _nsa_prompt_kernel,
        grid=(nseq, NSA_KV_HEADS, nq),
        in_specs=[pl.BlockSpec((gw, Q_BLOCK), lambda b, g, q: (g, b * nq + q)),
                  pl.BlockSpec((LANE, Q_BLOCK), lambda b, g, q: (0, b * nq + q)),
                  pl.BlockSpec((seq_len, LANE), lambda b, g, q: (b, g)),
                  pl.BlockSpec((HEAD_DIM, seq_len), lambda b, g, q: (g, b)),
                  pl.BlockSpec((seq_len, LANE), lambda b, g, q: (b, g)),
                  pl.BlockSpec((HEAD_DIM, seq_len), lambda b, g, q: (g, b)),
                  pl.BlockSpec((nb, LANE), lambda b, g, q: (b, g)),
                  pl.BlockSpec((HEAD_DIM, nb), lambda b, g, q: (g, b))],
        out_specs=pl.BlockSpec((NSA_GROUP * HEAD_DIM, Q_BLOCK), lambda b, g, q: (g, b * nq + q)),
        out_shape=jax.ShapeDtypeStruct((NSA_Q_W, n), jnp.float32),
        scratch_shapes=[pltpu.VMEM((nb, NSA_GROUP * Q_BLOCK), jnp.float32)]
        + [pltpu.VMEM((TILE_UNROLL * Q_BLOCK, NSA_GROUP * Q_BLOCK), jnp.float32)] * 2
        + [pltpu.SMEM((nb // (TILE_UNROLL * (Q_BLOCK // SEL_BLOCK)),), jnp.int32)],
        compiler_params=pltpu.CompilerParams(dimension_semantics=("arbitrary", "arbitrary", "arbitrary"),
                                             vmem_limit_bytes=VMEM_LIMIT),
        name="nsa_prompt",
    )(qat, gat, ksel, vselt, kwin, vwint, kc, vct)


FOX_TQ = 512
FOX_TK = 128


def _fox_prompt_kernel(qt_ref, kb_ref, vbt_ref, o_ref, buf0_ref, buf1_ref):
    f32 = jnp.float32
    tq, tk = FOX_TQ, FOX_TK
    qi = pl.program_id(2)
    qt = qt_ref[...]

    logits = lambda kt: jnp.dot(kb_ref[pl.ds(pl.multiple_of(kt * tk, tk), tk), :], qt, preferred_element_type=f32)
    values = lambda kt: vbt_ref[:, pl.ds(pl.multiple_of(kt * tk, tk), tk)]

    ndiag = tq // tk
    last_group = kb_ref.shape[0] // (tk * TILE_UNROLL) - 1
    carry = _tile_loop(qi * ndiag, logits, values, _softmax_init(tq), (buf0_ref, buf1_ref), tk, last_group)
    krow = lax.broadcasted_iota(jnp.int32, (tk, tq), 0)
    qloc = lax.broadcasted_iota(jnp.int32, (tk, tq), 1)
    diag = []
    for j in range(ndiag):
        kt = qi * ndiag + j
        diag.append((jnp.where(krow + j * tk <= qloc, logits(kt), NEG_INF), values(kt)))
    m, l, acc = _softmax_step(diag, carry)
    o_ref[...] = acc / l


def _fox_prompt(qbt, kb, vbt, nseq, seq_len):
    n = qbt.shape[1]
    nq = seq_len // FOX_TQ
    return pl.pallas_call(
        _fox_prompt_kernel,
        grid=(nseq, FOX_HEADS, nq),
        in_specs=[pl.BlockSpec((LANE, FOX_TQ), lambda b, h, q: (h, b * nq + q)),
                  pl.BlockSpec((seq_len, LANE), lambda b, h, q: (b, h)),
                  pl.BlockSpec((HEAD_DIM, seq_len), lambda b, h, q: (h, b))],
        out_specs=pl.BlockSpec((HEAD_DIM, FOX_TQ), lambda b, h, q: (h, b * nq + q)),
        out_shape=jax.ShapeDtypeStruct((FOX_W, n), jnp.float32),
        scratch_shapes=[pltpu.VMEM((TILE_UNROLL * FOX_TK, FOX_TQ), jnp.float32)] * 2,
        compiler_params=pltpu.CompilerParams(dimension_semantics=("arbitrary", "arbitrary", "arbitrary"),
                                             vmem_limit_bytes=VMEM_LIMIT),
        name="fox_prompt",
    )(qbt, kb, vbt)


DECODE_PER_STEP = 2


def _slope_rows(shape, rows_per_head):
    hd = lax.broadcasted_iota(jnp.int32, shape, 0) // rows_per_head
    return pltpu.bitcast((126 - hd) << 23, jnp.float32)


def _pad_rows(x, rows):
    return jnp.concatenate([x, jnp.zeros((rows - x.shape[0], x.shape[1]), x.dtype)], axis=0)


def _nt_dot(a, b):
    return lax.dot_general(a, b, (((1,), (1,)), ((), ())), preferred_element_type=jnp.float32)


def _joint_softmax(parts):
    m = functools.reduce(jnp.maximum, [jnp.max(p, axis=1, keepdims=True) for p in parts])
    es = [jnp.exp(p - m) for p in parts]
    return es, sum(jnp.sum(e, axis=1, keepdims=True) for e in es)


def _nsa_sample_kernel(npages, ns, nel, pt_ref, *refs):
    f32, bf16 = jnp.float32, jnp.bfloat16
    pages = refs[:nel * npages]
    rest = refs[nel * npages:]
    pe_ref, w1_ref, w2_ref, gk_ref, bd_ref = rest[6:11]
    bufk_ref, bufv_ref = rest[12:14]
    page = pages[0].shape[-1]
    ncol = NSA_KV_HEADS * npages

    for p in range(nel * npages):
        bufk_ref[p * LANE:(p + 1) * LANE, :] = pages[p][0, 0].reshape(LANE, page)
        bufv_ref[p * LANE:(p + 1) * LANE, :] = pages[p][0, 1].reshape(LANE, page)
    kc, vc = _summarize(bufk_ref, bufv_ref, pe_ref, w1_ref, w2_ref, nel * ncol)
    kc = (kc * lax.rsqrt(_group_mean_sq(kc, bd_ref[...]) + EPS) * gk_ref[...]).astype(bf16)
    vc = vc.astype(bf16)

    running = [_nsa_sample_one(e, npages, ns, pages[e * npages:(e + 1) * npages], rest,
                               kc[e * ncol:(e + 1) * ncol], vc[e * ncol:(e + 1) * ncol]) for e in range(nel)]
    while running:
        running = [gen for gen in running if next(gen, "done") != "done"]


def _nsa_sample_one(e, npages, ns, pages, rest, kc, vc):
    f32, bf16 = jnp.float32, jnp.bfloat16
    e_ref, qa_ref, ga_ref, rnew_ref, wnew_ref, state_ref = rest[:6]
    o_ref = rest[11]
    rows = slice(e * ns, (e + 1) * ns)
    page = pages[0].shape[-1]
    past = npages * page
    ncol = NSA_KV_HEADS * npages
    nrow = NSA_HEADS * ns

    lane8 = _lane(ns)
    ql = qa_ref[rows, :]
    qrows = []
    for hd in range(NSA_HEADS):
        g = hd // NSA_GROUP
        t = ql[:, (hd // 2) * LANE:(hd // 2 + 1) * LANE]
        if hd % 2 != g:
            t = pltpu.roll(t, HEAD_DIM, axis=1)
        qrows.append(jnp.where(lane8 // HEAD_DIM == g, t, 0.0))
    qb = (jnp.concatenate(qrows, axis=0) * SCALE).astype(bf16)

    def geom(width):
        tok = lax.broadcasted_iota(jnp.int32, (nrow, width), 0) % ns
        col = lax.broadcasted_iota(jnp.int32, (nrow, width), 1)
        return tok, col, _slope_rows((nrow, width), ns)

    yield
    q_both =qb + pltpu.roll(qb.astype(f32), HEAD_DIM, axis=1).astype(bf16)
    lane64 = lax.broadcasted_iota(jnp.int32, (nrow, LANE), 1) // HEAD_DIM
    tok, col, slope = geom(ncol)
    own = (lax.broadcasted_iota(jnp.int32, (nrow, ncol), 0) // (NSA_GROUP * ns)) == col % NSA_KV_HEADS
    lcs, blks = [], []
    for half in range(2):
        blk = (col // NSA_KV_HEADS) * 2 + half
        lc = _nt_dot(jnp.where(lane64 == half, q_both, 0.0), kc)
        lc = lc - slope * (past + tok - (blk * CMP_BLOCK + CMP_BLOCK - 1)).astype(f32)
        lcs.append(jnp.where(own, lc, NEG_INF))
        blks.append(blk)
    ecs, lsum = _joint_softmax(lcs)
    pcs = [e / lsum for e in ecs]
    res = [jnp.dot(pc.astype(bf16), vc, preferred_element_type=f32) for pc in pcs]
    o_c = jnp.where(lane64 == 0, res[0], res[1])
    o_c = o_c + pltpu.roll(o_c, HEAD_DIM, axis=1)

    yield
    nsel = NSA_KV_HEADS * ns
    col_s = lax.broadcasted_iota(jnp.int32, (nsel, ncol), 1)
    own_s = (lax.broadcasted_iota(jnp.int32, (nsel, ncol), 0) // ns) == col_s % NSA_KV_HEADS
    blks_s = [(col_s // NSA_KV_HEADS) * 2 + half for half in range(2)]
    imps = []
    for half in range(2):
        imp = jnp.concatenate(
            [sum(pcs[half][(g * NSA_GROUP + r) * ns:(g * NSA_GROUP + r + 1) * ns] for r in range(NSA_GROUP))
             for g in range(NSA_KV_HEADS)], axis=0)
        imp = jnp.where(blks_s[half] == 0, FORCE_SCORE, imp)
        imps.append(jnp.where(own_s, imp, -1.0))
    ranks = [jnp.zeros((nsel, ncol), jnp.int32) for _ in range(2)]
    for h2 in range(2):
        for c in range(ncol):
            other = jnp.broadcast_to(imps[h2][:, c:c + 1], (nsel, ncol))
            blk_c = (c // NSA_KV_HEADS) * 2 + h2
            for half in range(2):
                ahead = (other > imps[half]) | ((other == imps[half]) & (blks_s[half] > blk_c))
                ranks[half] = ranks[half] + jnp.where(ahead, 1, 0)
    selexp = 0.0
    for half in range(2):
        sel = jnp.where((ranks[half] < SEL_TOPK - 1) & own_s, 1.0, 0.0)
        sel = jnp.concatenate([sel[g * ns:(g + 1) * ns] for g in range(NSA_KV_HEADS) for _ in range(NSA_GROUP)],
                              axis=0)
        selexp = selexp + jnp.dot(sel.astype(bf16), e_ref[half], preferred_element_type=f32)

    def new_tile(k_new):
        tok, col, slope = geom(LANE)
        s = _nt_dot(qb, _pad_rows(k_new, LANE).astype(bf16))
        return jnp.where(col <= tok, s - slope * (tok - col).astype(f32), NEG_INF)

    def weighted(es, vts, e_new, v_new, lsum):
        acc = sum(_nt_dot(e.astype(bf16), vt.astype(bf16)) for e, vt in zip(es, vts))
        acc = acc + jnp.dot(e_new.astype(bf16), _pad_rows(v_new, LANE).astype(bf16), preferred_element_type=f32)
        return acc / lsum

    yield
    tok, col, slope = geom(past)
    ls = jnp.concatenate([jnp.dot(qb, pages[p][0, 2].reshape(LANE, page).astype(bf16), preferred_element_type=f32)
                          for p in range(npages)], axis=1)
    ls = jnp.where(selexp > 0.5, ls - slope * (past + tok - col).astype(f32), NEG_INF)
    yield
    (es, en), lsum = _joint_softmax([ls, new_tile(rnew_ref[rows, 2 * LANE:3 * LANE])])
    yield
    o_s = weighted([es[:, p * page:(p + 1) * page] for p in range(npages)],
                   [pages[p][0, 3].reshape(LANE, page) for p in range(npages)],
                   en, rnew_ref[rows, 3 * LANE:4 * LANE], lsum)

    yield
    wbuf = state_ref.shape[-1]
    tok, col, slope = geom(wbuf)
    lw = jnp.dot(qb, state_ref[e, 0].reshape(LANE, wbuf).astype(bf16), preferred_element_type=f32)
    lw = jnp.where(col > tok + (wbuf - WINDOW), lw - slope * (wbuf + tok - col).astype(f32), NEG_INF)
    (ew, en), lsum = _joint_softmax([lw, new_tile(wnew_ref[rows, 0:LANE])])
    yield
    o_w = weighted([ew], [state_ref[e, 1].reshape(LANE, wbuf)], en, wnew_ref[rows, LANE:2 * LANE], lsum)

    yield
    ga = ga_ref[rows, :]
    gate = lambda br: jnp.concatenate(
        [jnp.broadcast_to(ga[:, hd * 3 + br:hd * 3 + br + 1], (ns, LANE)) for hd in range(NSA_HEADS)], axis=0)
    o = gate(0) * o_c + gate(1) * o_s + gate(2) * o_w
    for j in range(NSA_HEADS // 2):
        g = (2 * j) // NSA_GROUP
        a = o[2 * j * ns:(2 * j + 1) * ns]
        b = o[(2 * j + 1) * ns:(2 * j + 2) * ns]
        if g == 0:
            b = pltpu.roll(b, HEAD_DIM, axis=1)
        else:
            a = pltpu.roll(a, HEAD_DIM, axis=1)
        o_ref[rows, j * LANE:(j + 1) * LANE] = jnp.where(lane8 < HEAD_DIM, a, b)


def _compress_weights_t(cmp_pe, cmp_w1, cmp_w2):
    pe, w1, w2 = _compress_weights(cmp_pe.transpose(0, 2, 1), cmp_w1.transpose(0, 2, 1, 3), cmp_w2)
    return pe, w1, w2


def _nsa_sample(page_table, cache_t, qa, ga, rows_new, win_new, state_t, pe, w1, w2, gk, bd):
    db, npages = page_table.shape
    page = cache_t.shape[-1]
    assert page == 2 * CMP_BLOCK == LANE
    ns = qa.shape[0] // db
    past = npages * page
    ncol = NSA_KV_HEADS * npages
    half = lax.broadcasted_iota(jnp.int32, (2, ncol, past), 0)
    col = lax.broadcasted_iota(jnp.int32, (2, ncol, past), 1)
    key = lax.broadcasted_iota(jnp.int32, (2, ncol, past), 2)
    expand = jnp.where(key // SEL_BLOCK == (col // NSA_KV_HEADS) * 2 + half, 1.0, 0.0).astype(jnp.bfloat16)
    gk2 = jnp.concatenate([gk, gk]).reshape(1, LANE)
    full = lambda a: pl.BlockSpec(a.shape, lambda b, pt: (0,) * a.ndim)
    nel = DECODE_PER_STEP
    tok = lambda w: pl.BlockSpec((nel * ns, w), lambda b, pt: (b, 0))
    page_specs = [pl.BlockSpec((1,) + cache_t.shape[1:],
                               functools.partial(lambda e, p, b, pt: (pt[b * nel + e, p], 0, 0, 0, 0), e, p))
                  for e in range(nel) for p in range(npages)]
    return pl.pallas_call(
        functools.partial(_nsa_sample_kernel, npages, ns, nel),
        grid_spec=pltpu.PrefetchScalarGridSpec(
            num_scalar_prefetch=1, grid=(db // nel,),
            in_specs=page_specs + [full(expand), tok(NSA_Q_W), tok(LANE), tok(512), tok(256),
                                   pl.BlockSpec((nel,) + state_t.shape[1:], lambda b, pt: (b, 0, 0, 0, 0)),
                                   full(pe), full(w1), full(w2), full(gk2), full(bd)],
            out_specs=tok(NSA_Q_W),
            scratch_shapes=[pltpu.VMEM((nel * npages * LANE, page), jnp.float32),
                            pltpu.VMEM((nel * npages * LANE, page), jnp.float32)]),
        out_shape=jax.ShapeDtypeStruct((db * ns, NSA_Q_W), jnp.float32),
        compiler_params=pltpu.CompilerParams(dimension_semantics=("arbitrary",),
                                             vmem_limit_bytes=VMEM_LIMIT),
        name="nsa_sample",
    )(page_table, *([cache_t] * (nel * npages)), expand, qa, ga, rows_new, win_new, state_t, pe, w1, w2, gk2, bd)


def _fox_decode_kernel(npages, ns, nel, pt_ref, *refs):
    f32, bf16 = jnp.float32, jnp.bfloat16
    qb_ref, knew_ref, lfnew_ref, o_ref = refs[2 * nel * npages:]
    page = refs[0].shape[-1]
    nh = FOX_HEADS
    nrow = nh * ns
    els = range(nel)
    kvt = [refs[e * npages:(e + 1) * npages] for e in els]
    lft = [refs[(nel + e) * npages:(nel + e + 1) * npages] for e in els]
    rows = [slice(e * ns, (e + 1) * ns) for e in els]

    r_io = lax.broadcasted_iota(jnp.int32, (page, page), 0)
    c_io = lax.broadcasted_iota(jnp.int32, (page, page), 1)
    triu = jnp.where(r_io <= c_io, 1.0, 0.0).astype(bf16)
    carry = [jnp.zeros((nh, 1), f32) for _ in els]
    negc = [[] for _ in els]
    for t in range(npages + 1):
        for e in els:
            lf = lft[e][t][0] if t < npages else lfnew_ref[e]
            ct = carry[e] + sum(jnp.dot(pc_, triu, preferred_element_type=f32) for pc_ in _split3(lf))
            carry[e] = ct[:, page - 1:page]
            negc[e].append(jnp.concatenate([jnp.broadcast_to(-ct[hd:hd + 1], (ns, page)) for hd in range(nh)],
                                           axis=0))

    head_of_lane = lax.broadcasted_iota(jnp.int32, (ns, FOX_W), 1) // HEAD_DIM
    q_bd = []
    for e in els:
        q = qb_ref[rows[e], :] * SCALE
        q_bd.append(jnp.concatenate([jnp.where(head_of_lane == hd, q, 0.0) for hd in range(nh)],
                                    axis=0).astype(bf16))

    parts = [[] for _ in els]
    for p in range(npages):
        for e in els:
            parts[e].append(jnp.dot(q_bd[e], kvt[e][p][0, 0].reshape(FOX_W, page).astype(bf16),
                                    preferred_element_type=f32) + negc[e][p])
    tok = lax.broadcasted_iota(jnp.int32, (nrow, page), 0) % ns
    col = lax.broadcasted_iota(jnp.int32, (nrow, page), 1)
    for e in els:
        s_new = _nt_dot(q_bd[e], _pad_rows(knew_ref[rows[e], 0:FOX_W], page).astype(bf16)) + negc[e][npages]
        parts[e].append(jnp.where(col <= tok, s_new, NEG_INF))
    soft = [_joint_softmax(parts[e]) for e in els]
    accs = [jnp.dot(soft[e][0][npages].astype(bf16), _pad_rows(knew_ref[rows[e], FOX_W:2 * FOX_W], page).astype(bf16),
                    preferred_element_type=f32) for e in els]
    for p in range(npages):
        for e in els:
            accs[e] = accs[e] + _nt_dot(soft[e][0][p].astype(bf16), kvt[e][p][0, 1].reshape(FOX_W, page).astype(bf16))
    for e in els:
        acc = accs[e] / soft[e][1]
        o_ref[rows[e], :] = sum(jnp.where(head_of_lane == hd, acc[hd * ns:(hd + 1) * ns], 0.0) for hd in range(nh))


def _fox_decode(page_table, cache_kvt, lft, qb, fox_new, lft_new):
    db, npages = page_table.shape
    ns = qb.shape[0] // db
    nel = DECODE_PER_STEP
    tok = lambda w: pl.BlockSpec((nel * ns, w), lambda b, pt: (b, 0))
    pg = lambda a: [pl.BlockSpec((1,) + a.shape[1:], functools.partial(
        lambda e, p, nd, b, pt: (pt[b * nel + e, p],) + (0,) * nd, e, p, a.ndim - 1))
        for e in range(nel) for p in range(npages)]
    return pl.pallas_call(
        functools.partial(_fox_decode_kernel, npages, ns, nel),
        grid_spec=pltpu.PrefetchScalarGridSpec(
            num_scalar_prefetch=1, grid=(db // nel,),
            in_specs=pg(cache_kvt) + pg(lft) + [tok(FOX_W), tok(2 * FOX_W),
                                                pl.BlockSpec((nel,) + lft_new.shape[1:], lambda b, pt: (b, 0, 0))],
            out_specs=tok(FOX_W)),
        out_shape=jax.ShapeDtypeStruct((db * ns, FOX_W), jnp.float32),
        compiler_params=pltpu.CompilerParams(dimension_semantics=("arbitrary",),
                                             vmem_limit_bytes=VMEM_LIMIT),
        name="fox_decode",
    )(page_table, *([cache_kvt] * (nel * npages)), *([lft] * (nel * npages)), qb, fox_new, lft_new)


BIG_ID = 1 << 20


def _topk_rows(s, k, ids):
    w = ids.shape[1]
    if s.shape[1] > w:
        parts = [_topk_rows(s[:, c:c + w], k, ids) for c in range(0, s.shape[1], w)]
        return jnp.concatenate([p[0] for p in parts], axis=1), jnp.concatenate([p[1] for p in parts], axis=1)
    ids = ids.astype(jnp.float32)
    vals, idxs = [], []
    for _ in range(k):
        m = jnp.max(s, axis=0, keepdims=True)
        idx = jnp.min(jnp.where(s == m, ids, float(BIG_ID)), axis=0, keepdims=True)
        vals.append(m)
        idxs.append(idx)
        s = jnp.where(ids == idx, -jnp.inf, s)
    return jnp.concatenate(vals, axis=0), jnp.concatenate(idxs, axis=0).astype(jnp.int32)


def _pick_rows(sel, table):
    out = jnp.zeros(sel.shape, table.dtype)
    for r in range(table.shape[0]):
        out = jnp.where(sel == r, table[r:r + 1, :], out)
    return out


def _merge_route_kernel(x_ref, on_ref, of_ref, mg_ref, wun_ref, wuf_ref, wo_ref, nf_ref, wqt_ref, sk_ref,
                        x1_ref, h2_ref, i1_ref, i2_ref, g_ref):
    tm = x_ref.shape[0]
    f32, bf16 = jnp.float32, jnp.bfloat16
    tdot = lambda ot, wgt: lax.dot_general(ot.astype(bf16), wgt, (((0,), (0,)), ((), ())),
                                           preferred_element_type=f32)
    a = tdot(on_ref[...], wun_ref[...])
    b = tdot(of_ref[...], wuf_ref[...])
    mixed = mg_ref[:, 0:D_MODEL] * a + mg_ref[:, D_MODEL:2 * D_MODEL] * b
    x1 = x_ref[...] + jnp.dot(mixed.astype(bf16), wo_ref[...], preferred_element_type=f32)
    x1_ref[...] = x1
    h2 = x1 * lax.rsqrt(jnp.mean(x1 * x1, axis=-1, keepdims=True) + EPS) * nf_ref[...]
    h2b = h2.astype(bf16)
    h2_ref[...] = h2b

    nk = PEER_N_KEYS
    key_ids = lax.broadcasted_iota(jnp.int32, (nk, LANE), 0)
    io16 = lax.broadcasted_iota(jnp.int32, (PEER_TOPK, LANE), 0)
    io8 = lax.broadcasted_iota(jnp.int32, (8, LANE), 0)
    cand_ids = jnp.concatenate([io16] + [a_ * PEER_TOPK + io8 for a_ in range(1, 8)]
                               + [(io8 + 8) * PEER_TOPK], axis=0)
    for h in range(PEER_HEADS):
        sv, si = [], []
        for p in range(2):
            hp = 2 * h + p
            qt = lax.dot_general(wqt_ref[hp * PEER_DK_HALF:(hp + 1) * PEER_DK_HALF, :], h2b,
                                 (((1,), (1,)), ((), ())), preferred_element_type=f32)
            st = jnp.dot(sk_ref[hp], qt.astype(bf16), preferred_element_type=f32)
            v, i = _topk_rows(st, PEER_TOPK, key_ids)
            sv.append(v)
            si.append(i)
        s1, s2 = sv
        cand = jnp.concatenate([s1[0:1] + s2] + [s1[a_:a_ + 1] + s2[0:8] for a_ in range(1, 8)]
                               + [s1[8:16] + s2[0:1]], axis=0)
        top, fid = _topk_rows(cand, PEER_TOPK, cand_ids)
        e = jnp.exp(top - jnp.max(top, axis=0, keepdims=True))
        g = e / jnp.sum(e, axis=0, keepdims=True)
        sl = slice(h * PEER_TOPK, (h + 1) * PEER_TOPK)
        i1_ref[:, sl] = _pick_rows(fid >> 4, si[0]).T
        i2_ref[:, sl] = _pick_rows(fid & (PEER_TOPK - 1), si[1]).T
        g_ref[:, sl] = g.T


def _merge_route(x2d, o_nsa, o_fox, mg, w_up_nsa, w_up_fox, w_out, norm_ffn, wq_t, sub_keys, tm=256):
    n = x2d.shape[0]
    row = lambda w: pl.BlockSpec((tm, w), lambda i: (i, 0))
    col = lambda h: pl.BlockSpec((h, tm), lambda i: (0, i))
    full = lambda a: pl.BlockSpec(a.shape, lambda i: (0,) * a.ndim)
    args = (x2d, o_nsa, o_fox, mg, w_up_nsa, w_up_fox, w_out, norm_ffn.reshape(1, D_MODEL), wq_t, sub_keys)
    hk = PEER_HEADS * PEER_TOPK
    return pl.pallas_call(
        _merge_route_kernel,
        grid=(n // tm,),
        in_specs=[row(D_MODEL), col(NSA_Q_W), col(FOX_W), row(MERGE_W)] + [full(a) for a in args[4:]],
        out_specs=[row(D_MODEL), row(D_MODEL), row(hk), row(hk), row(hk)],
        out_shape=[jax.ShapeDtypeStruct((n, D_MODEL), jnp.float32),
                   jax.ShapeDtypeStruct((n, D_MODEL), jnp.bfloat16),
                   jax.ShapeDtypeStruct((n, hk), jnp.int32),
                   jax.ShapeDtypeStruct((n, hk), jnp.int32),
                   jax.ShapeDtypeStruct((n, hk), jnp.float32)],
        compiler_params=pltpu.CompilerParams(dimension_semantics=("arbitrary",),
                                             vmem_limit_bytes=VMEM_LIMIT),
        name="merge_route",
    )(*args)


def _peer_act_kernel(h2_ref, ut_ref, i1_ref, i2_ref, act_ref):
    c = pl.program_id(1)
    ec = ut_ref.shape[1]

    @pl.when(c == 0)
    def _():
        act_ref[...] = jnp.zeros_like(act_ref)

    h2 = h2_ref[...]
    i1 = i1_ref[...]
    i2 = i2_ref[...]
    act = act_ref[...]
    nk = PEER_N_KEYS
    for blk in range(ec // ACT_BLOCK):
        a = jnp.dot(h2, ut_ref[:, blk * ACT_BLOCK:(blk + 1) * ACT_BLOCK],
                    preferred_element_type=jnp.float32)
        for ii in range(ACT_BLOCK // nk):
            got = jnp.take_along_axis(a[:, ii * nk:(ii + 1) * nk], i2, axis=1)
            act = jnp.where(i1 == c * (ec // nk) + blk * (ACT_BLOCK // nk) + ii, got, act)
    act_ref[...] = act


ACT_BLOCK = 512


def _peer_act(h2b, u_t, i1, i2, tm=512, ec=2048):
    n = h2b.shape[0]
    hk = i1.shape[1]
    return pl.pallas_call(
        _peer_act_kernel,
        grid=(n // tm, u_t.shape[1] // ec),
        in_specs=[pl.BlockSpec((tm, D_MODEL), lambda t, c: (t, 0)),
                  pl.BlockSpec((D_MODEL, ec), lambda t, c: (0, c)),
                  pl.BlockSpec((tm, hk), lambda t, c: (t, 0)),
                  pl.BlockSpec((tm, hk), lambda t, c: (t, 0))],
        out_specs=pl.BlockSpec((tm, hk), lambda t, c: (t, 0)),
        out_shape=jax.ShapeDtypeStruct((n, hk), jnp.float32),
        compiler_params=pltpu.CompilerParams(dimension_semantics=("arbitrary", "arbitrary"),
                                             vmem_limit_bytes=VMEM_LIMIT),
        name="peer_act",
    )(h2b, u_t, i1, i2)


def _peer_coef_kernel(act_ref, g_ref, i1_ref, i2_ref, c_ref, coef_ref):
    tm = act_ref.shape[0]
    nk = PEER_N_KEYS
    coef_ref[...] = g_ref[...] * jax.nn.gelu(act_ref[...])
    sub = lax.broadcasted_iota(jnp.int32, (nk, i1_ref.shape[1]), 0)

    def token(t):
        r1 = i1_ref[pl.ds(t, 1), :]
        r2 = i2_ref[pl.ds(t, 1), :]
        cf = coef_ref[pl.ds(t, 1), :]
        m1 = jnp.where(r1 == sub, cf, 0.0).astype(jnp.bfloat16)
        m2t = jnp.where(r2 == sub, 1.0, 0.0).astype(jnp.bfloat16)
        return lax.dot_general(m1, m2t, (((1,), (1,)), ((), ())), preferred_element_type=jnp.float32)

    def body(tg, carry):
        t0 = pl.multiple_of(tg * COEF_GROUP, COEF_GROUP)
        ct = jnp.stack([token(t0 + u) for u in range(COEF_GROUP)], axis=0)
        c_ref[:, pl.ds(t0, COEF_GROUP), :] = pltpu.einshape("tij->itj", ct).astype(c_ref.dtype)
        return carry

    lax.fori_loop(0, tm // COEF_GROUP, body, 0)


COEF_GROUP = 16


def _peer_coef(act, g, i1, i2, tm=128):
    n, hk = act.shape
    nk = PEER_N_KEYS
    row = pl.BlockSpec((tm, hk), lambda t: (t, 0))
    return pl.pallas_call(
        _peer_coef_kernel,
        grid=(n // tm,),
        in_specs=[row, row, row, row],
        out_specs=pl.BlockSpec((nk, tm, nk), lambda t: (0, t, 0)),
        out_shape=jax.ShapeDtypeStruct((nk, n, nk), jnp.bfloat16),
        scratch_shapes=[pltpu.VMEM((tm, hk), jnp.float32)],
        compiler_params=pltpu.CompilerParams(dimension_semantics=("arbitrary",),
                                             vmem_limit_bytes=VMEM_LIMIT),
        name="peer_coef",
    )(act, g, i1, i2)


def _peer_out_kernel(c_ref, v_ref, x1_ref, y_ref, acc_ref):
    k = pl.program_id(1)

    @pl.when(k == 0)
    def _():
        acc_ref[...] = x1_ref[...]

    acc = acc_ref[...]
    nk = PEER_N_KEYS
    for p in range(c_ref.shape[0] // 2):
        lhs = jnp.concatenate([c_ref[2 * p], c_ref[2 * p + 1]], axis=1)
        acc = acc + jnp.dot(lhs, v_ref[2 * p * nk:(2 * p + 2) * nk, :], preferred_element_type=jnp.float32)
    acc_ref[...] = acc

    @pl.when(k == pl.num_programs(1) - 1)
    def _():
        y_ref[...] = acc_ref[...]


def _peer_out(c3, v_b, x1, tm=1024, tk=2048):
    nk, n, _ = c3.shape
    ne = nk * nk
    return pl.pallas_call(
        _peer_out_kernel,
        grid=(n // tm, ne // tk),
        in_specs=[pl.BlockSpec((tk // nk, tm, nk), lambda t, k: (k, t, 0)),
                  pl.BlockSpec((tk, D_MODEL), lambda t, k: (k, 0)),
                  pl.BlockSpec((tm, D_MODEL), lambda t, k: (t, 0))],
        out_specs=pl.BlockSpec((tm, D_MODEL), lambda t, k: (t, 0)),
        out_shape=jax.ShapeDtypeStruct((n, D_MODEL), jnp.float32),
        scratch_shapes=[pltpu.VMEM((tm, D_MODEL), jnp.float32)],
        compiler_params=pltpu.CompilerParams(dimension_semantics=("arbitrary", "arbitrary"),
                                             vmem_limit_bytes=VMEM_LIMIT),
        name="peer_out",
    )(c3, v_b, x1)


def _peer_weights(w_up_nsa, w_up_fox, w_out, norm_ffn, peer_w_query, peer_sub_keys, peer_u, peer_v):
    bf16 = jnp.bfloat16
    return dict(w_up_nsa=w_up_nsa.astype(bf16), w_up_fox=w_up_fox.astype(bf16), w_out=w_out.astype(bf16),
                norm_ffn=norm_ffn, wq_t=peer_w_query.T.astype(bf16),
                sub_keys=peer_sub_keys.reshape(2 * PEER_HEADS, PEER_N_KEYS, PEER_DK_HALF).astype(bf16),
                u=peer_u.T.astype(bf16), v=peer_v.astype(bf16))


def _merge_peer(x2d, o_nsa, o_fox, mg, wts):
    x1, h2b, i1, i2, g = _merge_route(x2d, o_nsa, o_fox, mg, wts['w_up_nsa'], wts['w_up_fox'], wts['w_out'],
                                      wts['norm_ffn'], wts['wq_t'], wts['sub_keys'])
    act = _peer_act(h2b, wts['u'], i1, i2)
    c3 = _peer_coef(act, g, i1, i2)
    return _peer_out(c3, wts['v'], x1)


def kernel(x_prompt, x_sample, cache_nsa, cache_fox_kv, cache_fox_logf, state_nsa_win, page_table,
           norm_attn, w_in, fox_f_bias, nsa_q_norm, nsa_k_norm, fox_q_norm, fox_k_norm,
           cmp_pe, cmp_w1, cmp_w2, w_up_nsa, w_up_fox, w_out, norm_ffn,
           peer_w_query, peer_sub_keys, peer_u, peer_v):
    w_front = _front_weights(w_in)
    bd = _block_diag_mean()
    wts = _peer_weights(w_up_nsa, w_up_fox, w_out, norm_ffn, peer_w_query, peer_sub_keys, peer_u, peer_v)
    cmp_wts = _compress_weights(cmp_pe, cmp_w1, cmp_w2)

    bp, seq, _ = x_prompt.shape
    n_p = bp * seq
    (rows_t, win_t, fox_t, logf_p2d, mg_p, qat, gat, ksel, kwin, vselt, vwint, qbt, kb, vbt, rows_cmp) = _front_attn(
        x_prompt.reshape(n_p, D_MODEL), seq, norm_attn, w_front, bd, fox_f_bias,
        nsa_q_norm, nsa_k_norm, fox_q_norm, fox_k_norm)
    kc, vct = _compress(rows_cmp, seq // CMP_BLOCK, *cmp_wts, nsa_k_norm[0], bd)
    o_nsa_t = _nsa_prompt(qat, gat, ksel, vselt, kwin, vwint, kc, vct, bp, seq)
    o_fox_t = _fox_prompt(qbt, kb, vbt, bp, seq)
    y_p = _merge_peer(x_prompt.reshape(n_p, D_MODEL), o_nsa_t, o_fox_t, mg_p, wts).reshape(x_prompt.shape)
    to_rows = lambda a, *dims: a.reshape(bp, *dims, a.shape[-1]).transpose(0, len(dims) + 1, *range(1, len(dims) + 1))
    nsa_p = to_rows(rows_t, 4, NSA_KV_HEADS, HEAD_DIM)
    fox_p = to_rows(fox_t, 2, FOX_HEADS, HEAD_DIM)
    logf_p = logf_p2d.reshape(bp, seq, FOX_HEADS)
    win_p = to_rows(win_t[:, :, seq - min(WINDOW, seq):], 2, NSA_KV_HEADS, HEAD_DIM)

    db, ns, _ = x_sample.shape
    n_s = db * ns
    n_pool, page = cache_nsa.shape[:2]
    wbuf = state_nsa_win.shape[1]
    qa_s, rows_s, win_s2d, ga_s, qb_s, fox_s2d, logf_s2d, mg_s = _front(
        x_sample.reshape(n_s, D_MODEL), norm_attn, w_front, bd, fox_f_bias,
        nsa_q_norm, nsa_k_norm, fox_q_norm, fox_k_norm)
    o_nsa_s = _nsa_sample(page_table, cache_nsa.transpose(0, 2, 3, 4, 1), qa_s, ga_s, rows_s, win_s2d,
                          state_nsa_win.transpose(0, 2, 3, 4, 1),
                          *_compress_weights_t(cmp_pe, cmp_w1, cmp_w2), nsa_k_norm[0], bd)
    lft_new = jnp.pad(logf_s2d.reshape(db, ns, FOX_HEADS).transpose(0, 2, 1), ((0, 0), (0, 0), (0, page - ns)))
    o_fox_s = _fox_decode(page_table, cache_fox_kv.transpose(0, 2, 3, 4, 1), cache_fox_logf.transpose(0, 2, 1),
                          qb_s, fox_s2d, lft_new)
    y_s = _merge_peer(x_sample.reshape(n_s, D_MODEL), o_nsa_s.T, o_fox_s.T, mg_s, wts).reshape(x_sample.shape)
    nsa_s = rows_s.reshape(db, ns, 4, NSA_KV_HEADS, HEAD_DIM)
    fox_s = fox_s2d.reshape(db, ns, 2, FOX_HEADS, HEAD_DIM)
    logf_s = logf_s2d.reshape(db, ns, FOX_HEADS)
    win_s = jnp.concatenate([state_nsa_win[:, ns:], win_s2d.reshape(db, ns, 2, NSA_KV_HEADS, HEAD_DIM)], axis=1)
    return (y_p, y_s, nsa_p, fox_p, logf_p, win_p, nsa_s, fox_s, logf_s, win_s)
```

```python
import functools

import jax
import jax.numpy as jnp
from jax import lax
from jax.experimental import pallas as pl
from jax.experimental.pallas import tpu as pltpu

D_MODEL = 1024
HEAD_DIM = 64
NSA_HEADS = 8
NSA_KV_HEADS = 2
NSA_GROUP = NSA_HEADS // NSA_KV_HEADS
CMP_BLOCK = 64
SEL_BLOCK = CMP_BLOCK
SEL_TOPK = 16
WINDOW = 512
FOX_HEADS = 8
Q_BLOCK = 128
PEER_HEADS = 8
PEER_N_KEYS = 128
PEER_DK = 256
PEER_DK_HALF = PEER_DK // 2
PEER_TOPK = 16
PEER_CHUNK = 256

NSA_Q_W = NSA_HEADS * HEAD_DIM
NSA_KV_W = 6 * NSA_KV_HEADS * HEAD_DIM
NSA_GATE_W = 3 * NSA_HEADS
FOX_W = FOX_HEADS * HEAD_DIM
FOX_QKV_W = 3 * FOX_W
FOX_F_W = FOX_HEADS
MERGE_W = 2 * D_MODEL
SPLIT_Q_A = NSA_Q_W
SPLIT_KV_A = SPLIT_Q_A + NSA_KV_W
SPLIT_G_A = SPLIT_KV_A + NSA_GATE_W
SPLIT_QKV_B = SPLIT_G_A + FOX_QKV_W
SPLIT_F_B = SPLIT_QKV_B + FOX_F_W
IN_WIDTH = SPLIT_F_B + MERGE_W

SCALE = HEAD_DIM ** -0.5
FORCE_SCORE = float(NSA_GROUP + 1)
NEG_INF = -1e30
EPS = 1e-6

LANE = 128
VMEM_LIMIT = 48 * 1024 * 1024


def _group_mean_sq(x, bd):
    sq = x * x
    hi = sq.astype(jnp.bfloat16)
    lo = (sq - hi.astype(jnp.float32)).astype(jnp.bfloat16)
    return (jnp.dot(hi, bd, preferred_element_type=jnp.float32)
            + jnp.dot(lo, bd, preferred_element_type=jnp.float32))


def _head_rms(x, g, bd):
    outs = []
    for c in range(x.shape[1] // LANE):
        xc = x[:, c * LANE:(c + 1) * LANE]
        outs.append(xc * lax.rsqrt(_group_mean_sq(xc, bd) + EPS) * g)
    return outs[0] if len(outs) == 1 else jnp.concatenate(outs, axis=1)


def _front_kernel(x_ref, na_ref, w_ref, bd_ref, fb_ref, gq_a_ref, gk_sel_ref, gk_win_ref,
                  gq_b_ref, gk_b_ref,
                  qa_ref, rows_ref, win_ref, ga_ref, qb_ref, fox_ref, logf_ref, mg_ref):
    x = x_ref[...]
    h = x * lax.rsqrt(jnp.mean(x * x, axis=-1, keepdims=True) + EPS) * na_ref[...]
    hb = h.astype(jnp.bfloat16)
    bd = bd_ref[...]

    def proj(c0, width):
        return jnp.dot(hb, w_ref[:, c0:c0 + width], preferred_element_type=jnp.float32)

    c = 0
    qa_ref[...] = _head_rms(proj(c, NSA_Q_W), gq_a_ref[...], bd)
    c += NSA_Q_W
    rows_ref[:, 0:256] = proj(c, 256)
    rows_ref[:, 256:384] = _head_rms(proj(c + 256, 128), gk_sel_ref[...], bd)
    rows_ref[:, 384:512] = proj(c + 384, 128)
    win_ref[:, 0:128] = _head_rms(proj(c + 512, 128), gk_win_ref[...], bd)
    win_ref[:, 128:256] = proj(c + 640, 128)
    c += NSA_KV_W
    qb_ref[...] = _head_rms(proj(c, FOX_W), gq_b_ref[...], bd)
    fox_ref[:, 0:FOX_W] = _head_rms(proj(c + FOX_W, FOX_W), gk_b_ref[...], bd)
    fox_ref[:, FOX_W:2 * FOX_W] = proj(c + 2 * FOX_W, FOX_W)
    c += FOX_QKV_W
    for j in range(MERGE_W // 512):
        mg_ref[:, j * 512:(j + 1) * 512] = jax.nn.sigmoid(proj(c + j * 512, 512))
    c += MERGE_W
    ga_ref[...] = jax.nn.sigmoid(proj(c, LANE))
    f = proj(c + LANE, LANE)[:, 0:FOX_F_W] + fb_ref[...]
    logf_ref[...] = jnp.minimum(f, 0.0) - jnp.log1p(jnp.exp(-jnp.abs(f)))


def _front(x2d, norm_attn, w_front, bd, fox_f_bias, nsa_q_norm, nsa_k_norm, fox_q_norm, fox_k_norm,
           tm=256):
    n = x2d.shape[0]
    wf = w_front.shape[1]
    two = lambda g: jnp.concatenate([g, g]).reshape(1, LANE)
    row = lambda w: pl.BlockSpec((tm, w), lambda i: (i, 0))
    full = lambda a: pl.BlockSpec(a.shape, lambda i: (0,) * a.ndim)
    args = (x2d, norm_attn.reshape(1, D_MODEL), w_front, bd, fox_f_bias.reshape(1, FOX_F_W),
            two(nsa_q_norm), two(nsa_k_norm[1]), two(nsa_k_norm[2]), two(fox_q_norm), two(fox_k_norm))
    widths = (NSA_Q_W, 512, 256, LANE, FOX_W, 2 * FOX_W, FOX_F_W, MERGE_W)
    return pl.pallas_call(
        _front_kernel,
        grid=(n // tm,),
        in_specs=[row(D_MODEL)] + [full(a) for a in args[1:]],
        out_specs=[row(w) for w in widths],
        out_shape=[jax.ShapeDtypeStruct((n, w), jnp.float32) for w in widths],
        compiler_params=pltpu.CompilerParams(dimension_semantics=("arbitrary",),
                                             vmem_limit_bytes=VMEM_LIMIT),
        name="front",
    )(*args)


FEAT = HEAD_DIM


def _lane(tm):
    return lax.broadcasted_iota(jnp.int32, (tm, LANE), 1)


def _expand_halves(x):
    lo = _lane(x.shape[0]) < HEAD_DIM
    return jnp.where(lo, x, 0.0), jnp.where(lo, pltpu.roll(x, HEAD_DIM, axis=1), 0.0)


def _split3(x):
    hi = x.astype(jnp.bfloat16)
    r = x - hi.astype(jnp.float32)
    mid = r.astype(jnp.bfloat16)
    lo = (r - mid.astype(jnp.float32)).astype(jnp.bfloat16)
    return hi, mid, lo


def _front_attn_kernel(seq_len, x_ref, na_ref, w_ref, bd_ref, fb_ref, gq_a_ref, gk_sel_ref, gk_win_ref,
                       gq_b_ref, gk_b_ref, place_ref,
                       rows_ref, win_ref, fox_ref, logf_ref, mg_ref,
                       qat_ref, gat_ref, ksel_ref, kwin_ref, vselt_ref, vwint_ref, qbt_ref, kb_ref, vbt_ref, rcmp_ref,
                       carry_ref):
    f32, bf16 = jnp.float32, jnp.bfloat16
    tm = x_ref.shape[0]
    i = pl.program_id(0)
    x = x_ref[...]
    h = x * lax.rsqrt(jnp.mean(x * x, axis=-1, keepdims=True) + EPS) * na_ref[...]
    hb = h.astype(bf16)
    bd = bd_ref[...]
    lane = _lane(tm)
    pos = (i * tm + lax.broadcasted_iota(jnp.int32, (tm, LANE), 0)) % seq_len
    kfeat = jnp.where(lane == FEAT, (pos // SEL_BLOCK).astype(f32),
                      jnp.where(lane == FEAT + 1, (pos % SEL_BLOCK).astype(f32), 0.0))

    def proj(c0, width):
        return jnp.dot(hb, w_ref[:, c0:c0 + width], preferred_element_type=f32)

    c = 0
    qa = _head_rms(proj(c, NSA_Q_W), gq_a_ref[...], bd)
    for j in range(NSA_HEADS // 2):
        for s, half in enumerate(_expand_halves(qa[:, j * LANE:(j + 1) * LANE])):
            hd = 2 * j + s
            slope = 2.0 ** -(hd + 1)
            qfeat = jnp.where(lane == FEAT, slope * SEL_BLOCK, jnp.where(lane == FEAT + 1, slope, 0.0))
            qat_ref[hd * LANE:(hd + 1) * LANE, :] = (half * SCALE + qfeat).T.astype(bf16)
    c += NSA_Q_W
    for j in range(2):
        raw = proj(c + j * LANE, LANE)
        rcmp_ref[:, j * LANE:(j + 1) * LANE] = raw
        rows_ref[0, j * LANE:(j + 1) * LANE, :] = raw.T
    ksel = _head_rms(proj(c + 256, 128), gk_sel_ref[...], bd)
    rows_ref[0, 2 * LANE:3 * LANE, :] = ksel.T
    vsel_t = proj(c + 384, 128).T
    rows_ref[0, 3 * LANE:4 * LANE, :] = vsel_t
    kwin = _head_rms(proj(c + 512, 128), gk_win_ref[...], bd)
    win_ref[0, 0:LANE, :] = kwin.T
    vwin_t = proj(c + 640, 128).T
    win_ref[0, LANE:2 * LANE, :] = vwin_t
    for g, (ks_g, kw_g) in enumerate(zip(_expand_halves(ksel), _expand_halves(kwin))):
        ksel_ref[:, g * LANE:(g + 1) * LANE] = (ks_g + kfeat).astype(bf16)
        kwin_ref[:, g * LANE:(g + 1) * LANE] = (kw_g + kfeat).astype(bf16)
    vselt_ref[...] = vsel_t.astype(bf16)
    vwint_ref[...] = vwin_t.astype(bf16)
    c += NSA_KV_W

    qb = _head_rms(proj(c, FOX_W), gq_b_ref[...], bd)
    ones3 = jnp.where((lane >= FEAT) & (lane < FEAT + 3), 1.0, 0.0)
    for j in range(FOX_HEADS // 2):
        for s, half in enumerate(_expand_halves(qb[:, j * LANE:(j + 1) * LANE])):
            hd = 2 * j + s
            qbt_ref[hd * LANE:(hd + 1) * LANE, :] = (half * SCALE + ones3).T.astype(bf16)
    kbn = _head_rms(proj(c + FOX_W, FOX_W), gk_b_ref[...], bd)
    for j in range(FOX_HEADS // 2):
        fox_ref[0, j * LANE:(j + 1) * LANE, :] = kbn[:, j * LANE:(j + 1) * LANE].T
        vb_t = proj(c + 2 * FOX_W + j * LANE, LANE).T
        fox_ref[0, FOX_W + j * LANE:FOX_W + (j + 1) * LANE, :] = vb_t
        vbt_ref[j * LANE:(j + 1) * LANE, :] = vb_t.astype(bf16)
    c += FOX_QKV_W
    for j in range(MERGE_W // 512):
        mg_ref[:, j * 512:(j + 1) * 512] = jax.nn.sigmoid(proj(c + j * 512, 512))
    c += MERGE_W
    gat_ref[...] = jax.nn.sigmoid(proj(c, LANE)).T
    f = proj(c + LANE, LANE) + fb_ref[...]
    lf = jnp.minimum(f, 0.0) - jnp.log1p(jnp.exp(-jnp.abs(f)))
    logf_ref[...] = lf[:, 0:FOX_F_W]

    @pl.when((i * tm) % seq_len == 0)
    def _():
        carry_ref[...] = jnp.zeros_like(carry_ref)

    r_io = lax.broadcasted_iota(jnp.int32, (tm, tm), 0)
    c_io = lax.broadcasted_iota(jnp.int32, (tm, tm), 1)
    tri = jnp.where(c_io <= r_io, 1.0, 0.0).astype(bf16)
    csum = carry_ref[...] + sum(jnp.dot(tri, p, preferred_element_type=f32) for p in _split3(lf))
    carry_ref[...] = csum[tm - 1:tm, :]
    pieces = jnp.concatenate(_split3(-csum), axis=1)
    cfeat = jnp.dot(pieces, place_ref[...], preferred_element_type=f32)
    for j in range(FOX_HEADS // 2):
        for s, half in enumerate(_expand_halves(kbn[:, j * LANE:(j + 1) * LANE])):
            hd = 2 * j + s
            kb_ref[:, hd * LANE:(hd + 1) * LANE] = (half + cfeat[:, hd * LANE:(hd + 1) * LANE]).astype(bf16)


def _fox_feature_placement():
    r = lax.broadcasted_iota(jnp.int32, (3 * LANE, FOX_HEADS * LANE), 0)
    c = lax.broadcasted_iota(jnp.int32, (3 * LANE, FOX_HEADS * LANE), 1)
    s, hd = r // LANE, r % LANE
    return jnp.where((hd < FOX_HEADS) & (c == hd * LANE + FEAT + s), 1.0, 0.0).astype(jnp.bfloat16)


def _front_attn(x2d, seq_len, norm_attn, w_front, bd, fox_f_bias, nsa_q_norm, nsa_k_norm, fox_q_norm,
                fox_k_norm, tm=256):
    n = x2d.shape[0]
    f32, bf16 = jnp.float32, jnp.bfloat16
    two = lambda g: jnp.concatenate([g, g]).reshape(1, LANE)
    row = lambda w: pl.BlockSpec((tm, w), lambda i: (i, 0))
    col = lambda h: pl.BlockSpec((h, tm), lambda i: (0, i))
    full = lambda a: pl.BlockSpec(a.shape, lambda i: (0,) * a.ndim)
    fb = jnp.pad(fox_f_bias, (0, LANE - FOX_F_W)).reshape(1, LANE)
    args = (x2d, norm_attn.reshape(1, D_MODEL), w_front, bd, fb,
            two(nsa_q_norm), two(nsa_k_norm[1]), two(nsa_k_norm[2]), two(fox_q_norm), two(fox_k_norm),
            _fox_feature_placement())
    per_seq = seq_len // tm
    nseq = n // seq_len
    leaf = lambda h: (pl.BlockSpec((1, h, tm), lambda i: (i // per_seq, 0, i % per_seq)), (nseq, h, seq_len), f32)
    outs = [leaf(512), leaf(256), leaf(2 * FOX_W),
            (row(FOX_F_W), (n, FOX_F_W), f32), (row(MERGE_W), (n, MERGE_W), f32),
            (col(NSA_HEADS * LANE), (NSA_HEADS * LANE, n), bf16), (col(LANE), (LANE, n), f32),
            (row(2 * LANE), (n, 2 * LANE), bf16), (row(2 * LANE), (n, 2 * LANE), bf16),
            (col(LANE), (LANE, n), bf16), (col(LANE), (LANE, n), bf16),
            (col(FOX_HEADS * LANE), (FOX_HEADS * LANE, n), bf16),
            (row(FOX_HEADS * LANE), (n, FOX_HEADS * LANE), bf16), (col(FOX_W), (FOX_W, n), bf16),
            (row(2 * LANE), (n, 2 * LANE), f32)]
    return pl.pallas_call(
        functools.partial(_front_attn_kernel, seq_len),
        grid=(n // tm,),
        in_specs=[row(D_MODEL)] + [full(a) for a in args[1:]],
        out_specs=[o[0] for o in outs],
        out_shape=[jax.ShapeDtypeStruct(o[1], o[2]) for o in outs],
        scratch_shapes=[pltpu.VMEM((1, LANE), f32)],
        compiler_params=pltpu.CompilerParams(dimension_semantics=("arbitrary",),
                                             vmem_limit_bytes=VMEM_LIMIT),
        name="front_attn",
    )(*args)


def _front_weights(w_in):
    pad = lambda w: jnp.pad(w, ((0, 0), (0, LANE - w.shape[1])))
    parts = [w_in[:, :SPLIT_KV_A], w_in[:, SPLIT_G_A:SPLIT_QKV_B], w_in[:, SPLIT_F_B:],
             pad(w_in[:, SPLIT_KV_A:SPLIT_G_A]), pad(w_in[:, SPLIT_QKV_B:SPLIT_F_B])]
    return jnp.concatenate(parts, axis=1).astype(jnp.bfloat16)


def _block_diag_mean():
    r = lax.broadcasted_iota(jnp.int32, (LANE, LANE), 0) // HEAD_DIM
    c = lax.broadcasted_iota(jnp.int32, (LANE, LANE), 1) // HEAD_DIM
    return jnp.where(r == c, 1.0 / HEAD_DIM, 0.0).astype(jnp.bfloat16)


def _softmax_step(tiles, carry):
    m, l, acc = carry
    m_new = functools.reduce(jnp.maximum, [jnp.max(s, axis=0, keepdims=True) for s, _ in tiles], m)
    alpha = jnp.exp(m - m_new)
    ps = [jnp.exp(s - m_new) for s, _ in tiles]
    l = alpha * l + sum(jnp.sum(p, axis=0, keepdims=True) for p in ps)
    acc = alpha * acc + sum(jnp.dot(vt, p.astype(jnp.bfloat16), preferred_element_type=jnp.float32)
                            for p, (_, vt) in zip(ps, tiles))
    return m_new, l, acc


TILE_UNROLL = 4


def _tile_loop(n, logits, values, carry, bufs, tk, last_group, groups=None, unroll=TILE_UNROLL):
    tile = lambda kt: (logits(kt), values(kt))
    nu = n // unroll
    count, group_of = (nu, lambda i: i) if groups is None else groups

    def fill(buf, i):
        grp = jnp.minimum(group_of(jnp.minimum(i, jnp.maximum(count - 1, 0))), last_group)
        for j in range(unroll):
            buf[j * tk:(j + 1) * tk, :] = logits(grp * unroll + j)

    def drain(buf, i, c):
        grp = group_of(i)
        return _softmax_step([(buf[j * tk:(j + 1) * tk, :], values(grp * unroll + j)) for j in range(unroll)], c)

    def body(i, c):
        fill(bufs[1], 2 * i + 1)
        c = drain(bufs[0], 2 * i, c)
        fill(bufs[0], 2 * i + 2)
        return drain(bufs[1], 2 * i + 1, c)

    fill(bufs[0], 0)
    carry = lax.fori_loop(0, count // 2, body, carry)
    carry = lax.cond(count % 2 == 1, lambda c: drain(bufs[0], count - 1, c), lambda c: c, carry)
    base = nu * unroll
    size = unroll // 2
    while size >= 1:
        has = (n & size) != 0
        carry = lax.cond(has, functools.partial(
            lambda b, sz, c: _softmax_step([tile(b + u) for u in range(sz)], c), base, size),
            lambda c: c, carry)
        base = base + jnp.where(has, size, 0)
        size //= 2
    return carry


def _softmax_init(w):
    return (jnp.full((1, w), NEG_INF, jnp.float32), jnp.zeros((1, w), jnp.float32),
            jnp.zeros((HEAD_DIM, w), jnp.float32))


def _summarize(xk_ref, xv_ref, pe_ref, w1_ref, w2_ref, nb):
    f32, bf16 = jnp.float32, jnp.bfloat16
    x_refs = (xk_ref, xv_ref)

    def body(l, accs):
        return tuple(
            acc + jnp.dot((x_refs[s][pl.ds(l, nb, stride=CMP_BLOCK), :] + pe_ref[s, pl.ds(l, 1), :]).astype(bf16),
                          w1_ref[s, l], preferred_element_type=f32)
            for s, acc in enumerate(accs))

    accs = lax.fori_loop(0, CMP_BLOCK, body, (jnp.zeros((nb, LANE), f32),) * 2, unroll=16)
    return tuple(jnp.dot(jax.nn.gelu(acc).astype(bf16), w2_ref[s], preferred_element_type=f32)
                 for s, acc in enumerate(accs))


def _compress_kernel(xk_ref, xv_ref, pe_ref, w1_ref, w2_ref, gk_ref, bd_ref, kc_ref, vct_ref):
    f32, bf16 = jnp.float32, jnp.bfloat16
    nb = kc_ref.shape[0]
    kc, vc = _summarize(xk_ref, xv_ref, pe_ref, w1_ref, w2_ref, nb)
    kc = kc * lax.rsqrt(_group_mean_sq(kc, bd_ref[...]) + EPS) * gk_ref[...]
    lane = _lane(nb)
    blk = lax.broadcasted_iota(jnp.int32, (nb, LANE), 0).astype(f32)
    feat = jnp.where(lane == FEAT, blk, jnp.where(lane == FEAT + 1, float(CMP_BLOCK - 1), 0.0))
    for g, half in enumerate(_expand_halves(kc)):
        kc_ref[:, g * LANE:(g + 1) * LANE] = (half + feat).astype(bf16)
    vct_ref[...] = vc.T.astype(bf16)


def _compress_weights(cmp_pe, cmp_w1, cmp_w2):
    def bdiag(w):
        z = jnp.zeros_like(w)
        return jnp.concatenate([jnp.concatenate([w, z], axis=-1), jnp.concatenate([z, w], axis=-1)], axis=-2)
    pe = jnp.concatenate([cmp_pe, cmp_pe], axis=-1)
    return pe, bdiag(cmp_w1).astype(jnp.bfloat16), bdiag(cmp_w2).astype(jnp.bfloat16)


def _compress(rows2d, nb, pe, w1, w2, gk, bd):
    nseq = rows2d.shape[0] // (nb * CMP_BLOCK)
    full = lambda a: pl.BlockSpec(a.shape, lambda b: (0,) * a.ndim)
    args = (rows2d, rows2d, pe, w1, w2, jnp.concatenate([gk, gk]).reshape(1, LANE), bd)
    return pl.pallas_call(
        _compress_kernel,
        grid=(nseq,),
        in_specs=[pl.BlockSpec((nb * CMP_BLOCK, LANE), lambda b: (b, 0)),
                  pl.BlockSpec((nb * CMP_BLOCK, LANE), lambda b: (b, 1))] + [full(a) for a in args[2:]],
        out_specs=[pl.BlockSpec((nb, 2 * LANE), lambda b: (b, 0)), pl.BlockSpec((LANE, nb), lambda b: (0, b))],
        out_shape=[jax.ShapeDtypeStruct((nseq * nb, 2 * LANE), jnp.bfloat16),
                   jax.ShapeDtypeStruct((LANE, nseq * nb), jnp.bfloat16)],
        compiler_params=pltpu.CompilerParams(dimension_semantics=("arbitrary",),
                                             vmem_limit_bytes=VMEM_LIMIT),
        name="nsa_compress",
    )(*args)


def _nsa_prompt_kernel(qt_ref, gat_ref, ksel_ref, vselt_ref, kwin_ref, vwint_ref, kc_ref, vct_ref,
                       o_ref, selb_ref, buf0_ref, buf1_ref, glist_ref):
    f32, bf16 = jnp.float32, jnp.bfloat16
    tq = Q_BLOCK
    w = NSA_GROUP * tq
    g = pl.program_id(1)
    qi = pl.program_id(2)
    nb = kc_ref.shape[0]
    qt = jnp.concatenate([qt_ref[r * LANE:(r + 1) * LANE, :] for r in range(NSA_GROUP)], axis=1)
    qloc = lax.broadcasted_iota(jnp.int32, (1, w), 1) % tq
    qpos = qi * tq + qloc
    krow = lax.broadcasted_iota(jnp.int32, (tq, w), 0)
    causal = krow <= qloc

    nwin = WINDOW // tq
    tiles = []
    for j in range(nwin + 1):
        kt = qi - nwin + j
        k0 = pl.multiple_of(jnp.maximum(kt, 0) * tq, tq)
        s = jnp.dot(kwin_ref[pl.ds(k0, tq), :], qt, preferred_element_type=f32)
        ok = kt >= 0
        if j == 0:
            ok = ok & (krow > qloc)
        elif j == nwin:
            ok = ok & causal
        tiles.append((jnp.where(ok, s, NEG_INF), vwint_ref[:, pl.ds(k0, tq)]))
    m_w, l_w, acc_w = _softmax_step(tiles, _softmax_init(w))
    o_w = acc_w / l_w

    sc = jnp.dot(kc_ref[...], qt, preferred_element_type=f32)
    blk = lax.broadcasted_iota(jnp.int32, (nb, w), 0)
    vis = blk * CMP_BLOCK + (CMP_BLOCK - 1) <= qpos
    sc = jnp.where(vis, sc, NEG_INF)
    pc = jnp.where(vis, jnp.exp(sc - jnp.max(sc, axis=0, keepdims=True)), 0.0)
    pc = pc / jnp.maximum(jnp.sum(pc, axis=0, keepdims=True), 1e-30)
    o_c = jnp.dot(vct_ref[...], pc.astype(bf16), preferred_element_type=f32)

    imp = sum(pc[:, r * tq:(r + 1) * tq] for r in range(NSA_GROUP))
    blk1 = blk[:, 0:tq]
    cur = qpos[:, 0:tq] // SEL_BLOCK
    imp = jnp.where((blk1 == cur) | (blk1 == 0), FORCE_SCORE, jnp.where(blk1 <= cur, imp, -1.0))
    for _ in range(min(SEL_TOPK, nb)):
        mx = jnp.max(imp, axis=0, keepdims=True)
        idx = jnp.min(jnp.where(imp == mx, blk1, BIG_ID), axis=0, keepdims=True)
        imp = jnp.where(blk1 == idx, -jnp.inf, imp)
    selb = jnp.where(imp == -jnp.inf, 0.0, NEG_INF)
    selb_ref[...] = jnp.concatenate([selb] * NSA_GROUP, axis=1)

    blocks_per_group = TILE_UNROLL * (tq // SEL_BLOCK)
    nfull = qi // TILE_UNROLL
    ngroups = jnp.int32(0)
    for j in range(nb // blocks_per_group):
        picked = imp[j * blocks_per_group:(j + 1) * blocks_per_group, :] == -jnp.inf
        hit = jnp.max(jnp.where(picked, 1.0, 0.0)) > 0.5
        glist_ref[ngroups] = j
        ngroups = ngroups + jnp.where(hit & (j < nfull), 1, 0)

    def sel_bias(kt):
        per = tq // SEL_BLOCK
        return jnp.concatenate([jnp.broadcast_to(selb_ref[pl.ds(kt * per + j, 1), :], (SEL_BLOCK, w))
                                for j in range(per)], axis=0)

    def sel_logits(kt):
        k0 = pl.multiple_of(kt * tq, tq)
        return jnp.dot(ksel_ref[pl.ds(k0, tq), :], qt, preferred_element_type=f32) + sel_bias(kt)

    sel_values = lambda kt: vselt_ref[:, pl.ds(pl.multiple_of(kt * tq, tq), tq)]
    last_group = ksel_ref.shape[0] // (tq * TILE_UNROLL) - 1
    carry = _tile_loop(qi, sel_logits, sel_values, _softmax_init(w), (buf0_ref, buf1_ref), tq, last_group,
                       groups=(ngroups, lambda i: glist_ref[i]))
    m_s, l_s, acc_s = _softmax_step([(jnp.where(causal, sel_logits(qi), NEG_INF), sel_values(qi))], carry)
    o_s = acc_s / l_s

    for r in range(NSA_GROUP):
        gate = lambda br: gat_ref[pl.ds((g * NSA_GROUP + r) * 3 + br, 1), :]
        sl = slice(r * tq, (r + 1) * tq)
        o_ref[r * HEAD_DIM:(r + 1) * HEAD_DIM, :] = (
            gate(0) * o_c[:, sl] + gate(1) * o_s[:, sl] + gate(2) * o_w[:, sl])


def _nsa_prompt(qat, gat, ksel, vselt, kwin, vwint, kc, vct, nseq, seq_len):
    n = qat.shape[1]
    nq = seq_len // Q_BLOCK
    nb = seq_len // CMP_BLOCK
    gw = NSA_GROUP * LANE
    return pl.pallas_call(
        _nsa_prompt_kernel,
        grid=(nseq, NSA_KV_HEADS, nq),
        in_specs=[pl.BlockSpec((gw, Q_BLOCK), lambda b, g, q: (g, b * nq + q)),
                  pl.BlockSpec((LANE, Q_BLOCK), lambda b, g, q: (0, b * nq + q)),
                  pl.BlockSpec((seq_len, LANE), lambda b, g, q: (b, g)),
                  pl.BlockSpec((HEAD_DIM, seq_len), lambda b, g, q: (g, b)),
                  pl.BlockSpec((seq_len, LANE), lambda b, g, q: (b, g)),
                  pl.BlockSpec((HEAD_DIM, seq_len), lambda b, g, q: (g, b)),
                  pl.BlockSpec((nb, LANE), lambda b, g, q: (b, g)),
                  pl.BlockSpec((HEAD_DIM, nb), lambda b, g, q: (g, b))],
        out_specs=pl.BlockSpec((NSA_GROUP * HEAD_DIM, Q_BLOCK), lambda b, g, q: (g, b * nq + q)),
        out_shape=jax.ShapeDtypeStruct((NSA_Q_W, n), jnp.float32),
        scratch_shapes=[pltpu.VMEM((nb, NSA_GROUP * Q_BLOCK), jnp.float32)]
        + [pltpu.VMEM((TILE_UNROLL * Q_BLOCK, NSA_GROUP * Q_BLOCK), jnp.float32)] * 2
        + [pltpu.SMEM((nb // (TILE_UNROLL * (Q_BLOCK // SEL_BLOCK)),), jnp.int32)],
        compiler_params=pltpu.CompilerParams(dimension_semantics=("arbitrary", "arbitrary", "arbitrary"),
                                             vmem_limit_bytes=VMEM_LIMIT),
        name="nsa_prompt",
    )(qat, gat, ksel, vselt, kwin, vwint, kc, vct)


FOX_TQ = 512
FOX_TK = 128


def _fox_prompt_kernel(qt_ref, kb_ref, vbt_ref, o_ref, buf0_ref, buf1_ref):
    f32 = jnp.float32
    tq, tk = FOX_TQ, FOX_TK
    qi = pl.program_id(2)
    qt = qt_ref[...]

    logits = lambda kt: jnp.dot(kb_ref[pl.ds(pl.multiple_of(kt * tk, tk), tk), :], qt, preferred_element_type=f32)
    values = lambda kt: vbt_ref[:, pl.ds(pl.multiple_of(kt * tk, tk), tk)]

    ndiag = tq // tk
    last_group = kb_ref.shape[0] // (tk * TILE_UNROLL) - 1
    carry = _tile_loop(qi * ndiag, logits, values, _softmax_init(tq), (buf0_ref, buf1_ref), tk, last_group)
    krow = lax.broadcasted_iota(jnp.int32, (tk, tq), 0)
    qloc = lax.broadcasted_iota(jnp.int32, (tk, tq), 1)
    diag = []
    for j in range(ndiag):
        kt = qi * ndiag + j
        diag.append((jnp.where(krow + j * tk <= qloc, logits(kt), NEG_INF), values(kt)))
    m, l, acc = _softmax_step(diag, carry)
    o_ref[...] = acc / l


def _fox_prompt(qbt, kb, vbt, nseq, seq_len):
    n = qbt.shape[1]
    nq = seq_len // FOX_TQ
    return pl.pallas_call(
        _fox_prompt_kernel,
        grid=(nseq, FOX_HEADS, nq),
        in_specs=[pl.BlockSpec((LANE, FOX_TQ), lambda b, h, q: (h, b * nq + q)),
                  pl.BlockSpec((seq_len, LANE), lambda b, h, q: (b, h)),
                  pl.BlockSpec((HEAD_DIM, seq_len), lambda b, h, q: (h, b))],
        out_specs=pl.BlockSpec((HEAD_DIM, FOX_TQ), lambda b, h, q: (h, b * nq + q)),
        out_shape=jax.ShapeDtypeStruct((FOX_W, n), jnp.float32),
        scratch_shapes=[pltpu.VMEM((TILE_UNROLL * FOX_TK, FOX_TQ), jnp.float32)] * 2,
        compiler_params=pltpu.CompilerParams(dimension_semantics=("arbitrary", "arbitrary", "arbitrary"),
                                             vmem_limit_bytes=VMEM_LIMIT),
        name="fox_prompt",
    )(qbt, kb, vbt)


DECODE_PER_STEP = 2


def _slope_rows(shape, rows_per_head):
    hd = lax.broadcasted_iota(jnp.int32, shape, 0) // rows_per_head
    return pltpu.bitcast((126 - hd) << 23, jnp.float32)


def _pad_rows(x, rows):
    return jnp.concatenate([x, jnp.zeros((rows - x.shape[0], x.shape[1]), x.dtype)], axis=0)


def _nt_dot(a, b):
    return lax.dot_general(a, b, (((1,), (1,)), ((), ())), preferred_element_type=jnp.float32)


def _joint_softmax(parts):
    m = functools.reduce(jnp.maximum, [jnp.max(p, axis=1, keepdims=True) for p in parts])
    es = [jnp.exp(p - m) for p in parts]
    return es, sum(jnp.sum(e, axis=1, keepdims=True) for e in es)


def _nsa_sample_kernel(npages, ns, nel, pt_ref, *refs):
    f32, bf16 = jnp.float32, jnp.bfloat16
    pages = refs[:nel * npages]
    rest = refs[nel * npages:]
    pe_ref, w1_ref, w2_ref, gk_ref, bd_ref = rest[6:11]
    bufk_ref, bufv_ref = rest[12:14]
    page = pages[0].shape[-1]
    ncol = NSA_KV_HEADS * npages

    for p in range(nel * npages):
        bufk_ref[p * LANE:(p + 1) * LANE, :] = pages[p][0, 0].reshape(LANE, page)
        bufv_ref[p * LANE:(p + 1) * LANE, :] = pages[p][0, 1].reshape(LANE, page)
    kc, vc = _summarize(bufk_ref, bufv_ref, pe_ref, w1_ref, w2_ref, nel * ncol)
    kc = (kc * lax.rsqrt(_group_mean_sq(kc, bd_ref[...]) + EPS) * gk_ref[...]).astype(bf16)
    vc = vc.astype(bf16)

    running = [_nsa_sample_one(e, npages, ns, pages[e * npages:(e + 1) * npages], rest,
                               kc[e * ncol:(e + 1) * ncol], vc[e * ncol:(e + 1) * ncol]) for e in range(nel)]
    while running:
        running = [gen for gen in running if next(gen, "done") != "done"]


def _nsa_sample_one(e, npages, ns, pages, rest, kc, vc):
    f32, bf16 = jnp.float32, jnp.bfloat16
    e_ref, qa_ref, ga_ref, rnew_ref, wnew_ref, state_ref = rest[:6]
    o_ref = rest[11]
    rows = slice(e * ns, (e + 1) * ns)
    page = pages[0].shape[-1]
    past = npages * page
    ncol = NSA_KV_HEADS * npages
    nrow = NSA_HEADS * ns

    lane8 = _lane(ns)
    ql = qa_ref[rows, :]
    qrows = []
    for hd in range(NSA_HEADS):
        g = hd // NSA_GROUP
        t = ql[:, (hd // 2) * LANE:(hd // 2 + 1) * LANE]
        if hd % 2 != g:
            t = pltpu.roll(t, HEAD_DIM, axis=1)
        qrows.append(jnp.where(lane8 // HEAD_DIM == g, t, 0.0))
    qb = (jnp.concatenate(qrows, axis=0) * SCALE).astype(bf16)

    def geom(width):
        tok = lax.broadcasted_iota(jnp.int32, (nrow, width), 0) % ns
        col = lax.broadcasted_iota(jnp.int32, (nrow, width), 1)
        return tok, col, _slope_rows((nrow, width), ns)

    yield
    q_both =qb + pltpu.roll(qb.astype(f32), HEAD_DIM, axis=1).astype(bf16)
    lane64 = lax.broadcasted_iota(jnp.int32, (nrow, LANE), 1) // HEAD_DIM
    tok, col, slope = geom(ncol)
    own = (lax.broadcasted_iota(jnp.int32, (nrow, ncol), 0) // (NSA_GROUP * ns)) == col % NSA_KV_HEADS
    lcs, blks = [], []
    for half in range(2):
        blk = (col // NSA_KV_HEADS) * 2 + half
        lc = _nt_dot(jnp.where(lane64 == half, q_both, 0.0), kc)
        lc = lc - slope * (past + tok - (blk * CMP_BLOCK + CMP_BLOCK - 1)).astype(f32)
        lcs.append(jnp.where(own, lc, NEG_INF))
        blks.append(blk)
    ecs, lsum = _joint_softmax(lcs)
    pcs = [e / lsum for e in ecs]
    res = [jnp.dot(pc.astype(bf16), vc, preferred_element_type=f32) for pc in pcs]
    o_c = jnp.where(lane64 == 0, res[0], res[1])
    o_c = o_c + pltpu.roll(o_c, HEAD_DIM, axis=1)

    yield
    nsel = NSA_KV_HEADS * ns
    col_s = lax.broadcasted_iota(jnp.int32, (nsel, ncol), 1)
    own_s = (lax.broadcasted_iota(jnp.int32, (nsel, ncol), 0) // ns) == col_s % NSA_KV_HEADS
    blks_s = [(col_s // NSA_KV_HEADS) * 2 + half for half in range(2)]
    imps = []
    for half in range(2):
        imp = jnp.concatenate(
            [sum(pcs[half][(g * NSA_GROUP + r) * ns:(g * NSA_GROUP + r + 1) * ns] for r in range(NSA_GROUP))
             for g in range(NSA_KV_HEADS)], axis=0)
        imp = jnp.where(blks_s[half] == 0, FORCE_SCORE, imp)
        imps.append(jnp.where(own_s, imp, -1.0))
    ranks = [jnp.zeros((nsel, ncol), jnp.int32) for _ in range(2)]
    for h2 in range(2):
        for c in range(ncol):
            other = jnp.broadcast_to(imps[h2][:, c:c + 1], (nsel, ncol))
            blk_c = (c // NSA_KV_HEADS) * 2 + h2
            for half in range(2):
                ahead = (other > imps[half]) | ((other == imps[half]) & (blks_s[half] > blk_c))
                ranks[half] = ranks[half] + jnp.where(ahead, 1, 0)
    selexp = 0.0
    for half in range(2):
        sel = jnp.where((ranks[half] < SEL_TOPK - 1) & own_s, 1.0, 0.0)
        sel = jnp.concatenate([sel[g * ns:(g + 1) * ns] for g in range(NSA_KV_HEADS) for _ in range(NSA_GROUP)],
                              axis=0)
        selexp = selexp + jnp.dot(sel.astype(bf16), e_ref[half], preferred_element_type=f32)

    def new_tile(k_new):
        tok, col, slope = geom(LANE)
        s = _nt_dot(qb, _pad_rows(k_new, LANE).astype(bf16))
        return jnp.where(col <= tok, s - slope * (tok - col).astype(f32), NEG_INF)

    def weighted(es, vts, e_new, v_new, lsum):
        acc = sum(_nt_dot(e.astype(bf16), vt.astype(bf16)) for e, vt in zip(es, vts))
        acc = acc + jnp.dot(e_new.astype(bf16), _pad_rows(v_new, LANE).astype(bf16), preferred_element_type=f32)
        return acc / lsum

    yield
    tok, col, slope = geom(past)
    ls = jnp.concatenate([jnp.dot(qb, pages[p][0, 2].reshape(LANE, page).astype(bf16), preferred_element_type=f32)
                          for p in range(npages)], axis=1)
    ls = jnp.where(selexp > 0.5, ls - slope * (past + tok - col).astype(f32), NEG_INF)
    yield
    (es, en), lsum = _joint_softmax([ls, new_tile(rnew_ref[rows, 2 * LANE:3 * LANE])])
    yield
    o_s = weighted([es[:, p * page:(p + 1) * page] for p in range(npages)],
                   [pages[p][0, 3].reshape(LANE, page) for p in range(npages)],
                   en, rnew_ref[rows, 3 * LANE:4 * LANE], lsum)

    yield
    wbuf = state_ref.shape[-1]
    tok, col, slope = geom(wbuf)
    lw = jnp.dot(qb, state_ref[e, 0].reshape(LANE, wbuf).astype(bf16), preferred_element_type=f32)
    lw = jnp.where(col > tok + (wbuf - WINDOW), lw - slope * (wbuf + tok - col).astype(f32), NEG_INF)
    (ew, en), lsum = _joint_softmax([lw, new_tile(wnew_ref[rows, 0:LANE])])
    yield
    o_w = weighted([ew], [state_ref[e, 1].reshape(LANE, wbuf)], en, wnew_ref[rows, LANE:2 * LANE], lsum)

    yield
    ga = ga_ref[rows, :]
    gate = lambda br: jnp.concatenate(
        [jnp.broadcast_to(ga[:, hd * 3 + br:hd * 3 + br + 1], (ns, LANE)) for hd in range(NSA_HEADS)], axis=0)
    o = gate(0) * o_c + gate(1) * o_s + gate(2) * o_w
    for j in range(NSA_HEADS // 2):
        g = (2 * j) // NSA_GROUP
        a = o[2 * j * ns:(2 * j + 1) * ns]
        b = o[(2 * j + 1) * ns:(2 * j + 2) * ns]
        if g == 0:
            b = pltpu.roll(b, HEAD_DIM, axis=1)
        else:
            a = pltpu.roll(a, HEAD_DIM, axis=1)
        o_ref[rows, j * LANE:(j + 1) * LANE] = jnp.where(lane8 < HEAD_DIM, a, b)


def _compress_weights_t(cmp_pe, cmp_w1, cmp_w2):
    pe, w1, w2 = _compress_weights(cmp_pe.transpose(0, 2, 1), cmp_w1.transpose(0, 2, 1, 3), cmp_w2)
    return pe, w1, w2


def _nsa_sample(page_table, cache_t, qa, ga, rows_new, win_new, state_t, pe, w1, w2, gk, bd):
    db, npages = page_table.shape
    page = cache_t.shape[-1]
    assert page == 2 * CMP_BLOCK == LANE
    ns = qa.shape[0] // db
    past = npages * page
    ncol = NSA_KV_HEADS * npages
    half = lax.broadcasted_iota(jnp.int32, (2, ncol, past), 0)
    col = lax.broadcasted_iota(jnp.int32, (2, ncol, past), 1)
    key = lax.broadcasted_iota(jnp.int32, (2, ncol, past), 2)
    expand = jnp.where(key // SEL_BLOCK == (col // NSA_KV_HEADS) * 2 + half, 1.0, 0.0).astype(jnp.bfloat16)
    gk2 = jnp.concatenate([gk, gk]).reshape(1, LANE)
    full = lambda a: pl.BlockSpec(a.shape, lambda b, pt: (0,) * a.ndim)
    nel = DECODE_PER_STEP
    tok = lambda w: pl.BlockSpec((nel * ns, w), lambda b, pt: (b, 0))
    page_specs = [pl.BlockSpec((1,) + cache_t.shape[1:],
                               functools.partial(lambda e, p, b, pt: (pt[b * nel + e, p], 0, 0, 0, 0), e, p))
                  for e in range(nel) for p in range(npages)]
    return pl.pallas_call(
        functools.partial(_nsa_sample_kernel, npages, ns, nel),
        grid_spec=pltpu.PrefetchScalarGridSpec(
            num_scalar_prefetch=1, grid=(db // nel,),
            in_specs=page_specs + [full(expand), tok(NSA_Q_W), tok(LANE), tok(512), tok(256),
                                   pl.BlockSpec((nel,) + state_t.shape[1:], lambda b, pt: (b, 0, 0, 0, 0)),
                                   full(pe), full(w1), full(w2), full(gk2), full(bd)],
            out_specs=tok(NSA_Q_W),
            scratch_shapes=[pltpu.VMEM((nel * npages * LANE, page), jnp.float32),
                            pltpu.VMEM((nel * npages * LANE, page), jnp.float32)]),
        out_shape=jax.ShapeDtypeStruct((db * ns, NSA_Q_W), jnp.float32),
        compiler_params=pltpu.CompilerParams(dimension_semantics=("arbitrary",),
                                             vmem_limit_bytes=VMEM_LIMIT),
        name="nsa_sample",
    )(page_table, *([cache_t] * (nel * npages)), expand, qa, ga, rows_new, win_new, state_t, pe, w1, w2, gk2, bd)


def _fox_decode_kernel(npages, ns, nel, pt_ref, *refs):
    f32, bf16 = jnp.float32, jnp.bfloat16
    qb_ref, knew_ref, lfnew_ref, o_ref = refs[2 * nel * npages:]
    page = refs[0].shape[-1]
    nh = FOX_HEADS
    nrow = nh * ns
    els = range(nel)
    kvt = [refs[e * npages:(e + 1) * npages] for e in els]
    lft = [refs[(nel + e) * npages:(nel + e + 1) * npages] for e in els]
    rows = [slice(e * ns, (e + 1) * ns) for e in els]

    r_io = lax.broadcasted_iota(jnp.int32, (page, page), 0)
    c_io = lax.broadcasted_iota(jnp.int32, (page, page), 1)
    triu = jnp.where(r_io <= c_io, 1.0, 0.0).astype(bf16)
    carry = [jnp.zeros((nh, 1), f32) for _ in els]
    negc = [[] for _ in els]
    for t in range(npages + 1):
        for e in els:
            lf = lft[e][t][0] if t < npages else lfnew_ref[e]
            ct = carry[e] + sum(jnp.dot(pc_, triu, preferred_element_type=f32) for pc_ in _split3(lf))
            carry[e] = ct[:, page - 1:page]
            negc[e].append(jnp.concatenate([jnp.broadcast_to(-ct[hd:hd + 1], (ns, page)) for hd in range(nh)],
                                           axis=0))

    head_of_lane = lax.broadcasted_iota(jnp.int32, (ns, FOX_W), 1) // HEAD_DIM
    q_bd = []
    for e in els:
        q = qb_ref[rows[e], :] * SCALE
        q_bd.append(jnp.concatenate([jnp.where(head_of_lane == hd, q, 0.0) for hd in range(nh)],
                                    axis=0).astype(bf16))

    parts = [[] for _ in els]
    for p in range(npages):
        for e in els:
            parts[e].append(jnp.dot(q_bd[e], kvt[e][p][0, 0].reshape(FOX_W, page).astype(bf16),
                                    preferred_element_type=f32) + negc[e][p])
    tok = lax.broadcasted_iota(jnp.int32, (nrow, page), 0) % ns
    col = lax.broadcasted_iota(jnp.int32, (nrow, page), 1)
    for e in els:
        s_new = _nt_dot(q_bd[e], _pad_rows(knew_ref[rows[e], 0:FOX_W], page).astype(bf16)) + negc[e][npages]
        parts[e].append(jnp.where(col <= tok, s_new, NEG_INF))
    soft = [_joint_softmax(parts[e]) for e in els]
    accs = [jnp.dot(soft[e][0][npages].astype(bf16), _pad_rows(knew_ref[rows[e], FOX_W:2 * FOX_W], page).astype(bf16),
                    preferred_element_type=f32) for e in els]
    for p in range(npages):
        for e in els:
            accs[e] = accs[e] + _nt_dot(soft[e][0][p].astype(bf16), kvt[e][p][0, 1].reshape(FOX_W, page).astype(bf16))
    for e in els:
        acc = accs[e] / soft[e][1]
        o_ref[rows[e], :] = sum(jnp.where(head_of_lane == hd, acc[hd * ns:(hd + 1) * ns], 0.0) for hd in range(nh))


def _fox_decode(page_table, cache_kvt, lft, qb, fox_new, lft_new):
    db, npages = page_table.shape
    ns = qb.shape[0] // db
    nel = DECODE_PER_STEP
    tok = lambda w: pl.BlockSpec((nel * ns, w), lambda b, pt: (b, 0))
    pg = lambda a: [pl.BlockSpec((1,) + a.shape[1:], functools.partial(
        lambda e, p, nd, b, pt: (pt[b * nel + e, p],) + (0,) * nd, e, p, a.ndim - 1))
        for e in range(nel) for p in range(npages)]
    return pl.pallas_call(
        functools.partial(_fox_decode_kernel, npages, ns, nel),
        grid_spec=pltpu.PrefetchScalarGridSpec(
            num_scalar_prefetch=1, grid=(db // nel,),
            in_specs=pg(cache_kvt) + pg(lft) + [tok(FOX_W), tok(2 * FOX_W),
                                                pl.BlockSpec((nel,) + lft_new.shape[1:], lambda b, pt: (b, 0, 0))],
            out_specs=tok(FOX_W)),
        out_shape=jax.ShapeDtypeStruct((db * ns, FOX_W), jnp.float32),
        compiler_params=pltpu.CompilerParams(dimension_semantics=("arbitrary",),
                                             vmem_limit_bytes=VMEM_LIMIT),
        name="fox_decode",
    )(page_table, *([cache_kvt] * (nel * npages)), *([lft] * (nel * npages)), qb, fox_new, lft_new)


BIG_ID = 1 << 20


def _topk_rows(s, k, ids):
    w = ids.shape[1]
    if s.shape[1] > w:
        parts = [_topk_rows(s[:, c:c + w], k, ids) for c in range(0, s.shape[1], w)]
        return jnp.concatenate([p[0] for p in parts], axis=1), jnp.concatenate([p[1] for p in parts], axis=1)
    ids = ids.astype(jnp.float32)
    vals, idxs = [], []
    for _ in range(k):
        m = jnp.max(s, axis=0, keepdims=True)
        idx = jnp.min(jnp.where(s == m, ids, float(BIG_ID)), axis=0, keepdims=True)
        vals.append(m)
        idxs.append(idx)
        s = jnp.where(ids == idx, -jnp.inf, s)
    return jnp.concatenate(vals, axis=0), jnp.concatenate(idxs, axis=0).astype(jnp.int32)


def _pick_rows(sel, table):
    out = jnp.zeros(sel.shape, table.dtype)
    for r in range(table.shape[0]):
        out = jnp.where(sel == r, table[r:r + 1, :], out)
    return out


def _merge_route_kernel(x_ref, on_ref, of_ref, mg_ref, wun_ref, wuf_ref, wo_ref, nf_ref, wqt_ref, sk_ref,
                        x1_ref, h2_ref, i1_ref, i2_ref, g_ref):
    tm = x_ref.shape[0]
    f32, bf16 = jnp.float32, jnp.bfloat16
    tdot = lambda ot, wgt: lax.dot_general(ot.astype(bf16), wgt, (((0,), (0,)), ((), ())),
                                           preferred_element_type=f32)
    a = tdot(on_ref[...], wun_ref[...])
    b = tdot(of_ref[...], wuf_ref[...])
    mixed = mg_ref[:, 0:D_MODEL] * a + mg_ref[:, D_MODEL:2 * D_MODEL] * b
    x1 = x_ref[...] + jnp.dot(mixed.astype(bf16), wo_ref[...], preferred_element_type=f32)
    x1_ref[...] = x1
    h2 = x1 * lax.rsqrt(jnp.mean(x1 * x1, axis=-1, keepdims=True) + EPS) * nf_ref[...]
    h2b = h2.astype(bf16)
    h2_ref[...] = h2b

    nk = PEER_N_KEYS
    key_ids = lax.broadcasted_iota(jnp.int32, (nk, LANE), 0)
    io16 = lax.broadcasted_iota(jnp.int32, (PEER_TOPK, LANE), 0)
    io8 = lax.broadcasted_iota(jnp.int32, (8, LANE), 0)
    cand_ids = jnp.concatenate([io16] + [a_ * PEER_TOPK + io8 for a_ in range(1, 8)]
                               + [(io8 + 8) * PEER_TOPK], axis=0)
    for h in range(PEER_HEADS):
        sv, si = [], []
        for p in range(2):
            hp = 2 * h + p
            qt = lax.dot_general(wqt_ref[hp * PEER_DK_HALF:(hp + 1) * PEER_DK_HALF, :], h2b,
                                 (((1,), (1,)), ((), ())), preferred_element_type=f32)
            st = jnp.dot(sk_ref[hp], qt.astype(bf16), preferred_element_type=f32)
            v, i = _topk_rows(st, PEER_TOPK, key_ids)
            sv.append(v)
            si.append(i)
        s1, s2 = sv
        cand = jnp.concatenate([s1[0:1] + s2] + [s1[a_:a_ + 1] + s2[0:8] for a_ in range(1, 8)]
                               + [s1[8:16] + s2[0:1]], axis=0)
        top, fid = _topk_rows(cand, PEER_TOPK, cand_ids)
        e = jnp.exp(top - jnp.max(top, axis=0, keepdims=True))
        g = e / jnp.sum(e, axis=0, keepdims=True)
        sl = slice(h * PEER_TOPK, (h + 1) * PEER_TOPK)
        i1_ref[:, sl] = _pick_rows(fid >> 4, si[0]).T
        i2_ref[:, sl] = _pick_rows(fid & (PEER_TOPK - 1), si[1]).T
        g_ref[:, sl] = g.T


def _merge_route(x2d, o_nsa, o_fox, mg, w_up_nsa, w_up_fox, w_out, norm_ffn, wq_t, sub_keys, tm=256):
    n = x2d.shape[0]
    row = lambda w: pl.BlockSpec((tm, w), lambda i: (i, 0))
    col = lambda h: pl.BlockSpec((h, tm), lambda i: (0, i))
    full = lambda a: pl.BlockSpec(a.shape, lambda i: (0,) * a.ndim)
    args = (x2d, o_nsa, o_fox, mg, w_up_nsa, w_up_fox, w_out, norm_ffn.reshape(1, D_MODEL), wq_t, sub_keys)
    hk = PEER_HEADS * PEER_TOPK
    return pl.pallas_call(
        _merge_route_kernel,
        grid=(n // tm,),
        in_specs=[row(D_MODEL), col(NSA_Q_W), col(FOX_W), row(MERGE_W)] + [full(a) for a in args[4:]],
        out_specs=[row(D_MODEL), row(D_MODEL), row(hk), row(hk), row(hk)],
        out_shape=[jax.ShapeDtypeStruct((n, D_MODEL), jnp.float32),
                   jax.ShapeDtypeStruct((n, D_MODEL), jnp.bfloat16),
                   jax.ShapeDtypeStruct((n, hk), jnp.int32),
                   jax.ShapeDtypeStruct((n, hk), jnp.int32),
                   jax.ShapeDtypeStruct((n, hk), jnp.float32)],
        compiler_params=pltpu.CompilerParams(dimension_semantics=("arbitrary",),
                                             vmem_limit_bytes=VMEM_LIMIT),
        name="merge_route",
    )(*args)


def _peer_act_kernel(h2_ref, ut_ref, i1_ref, i2_ref, act_ref):
    c = pl.program_id(1)
    ec = ut_ref.shape[1]

    @pl.when(c == 0)
    def _():
        act_ref[...] = jnp.zeros_like(act_ref)

    h2 = h2_ref[...]
    i1 = i1_ref[...]
    i2 = i2_ref[...]
    act = act_ref[...]
    nk = PEER_N_KEYS
    for blk in range(ec // ACT_BLOCK):
        a = jnp.dot(h2, ut_ref[:, blk * ACT_BLOCK:(blk + 1) * ACT_BLOCK],
                    preferred_element_type=jnp.float32)
        for ii in range(ACT_BLOCK // nk):
            got = jnp.take_along_axis(a[:, ii * nk:(ii + 1) * nk], i2, axis=1)
            act = jnp.where(i1 == c * (ec // nk) + blk * (ACT_BLOCK // nk) + ii, got, act)
    act_ref[...] = act


ACT_BLOCK = 512


def _peer_act(h2b, u_t, i1, i2, tm=512, ec=2048):
    n = h2b.shape[0]
    hk = i1.shape[1]
    return pl.pallas_call(
        _peer_act_kernel,
        grid=(n // tm, u_t.shape[1] // ec),
        in_specs=[pl.BlockSpec((tm, D_MODEL), lambda t, c: (t, 0)),
                  pl.BlockSpec((D_MODEL, ec), lambda t, c: (0, c)),
                  pl.BlockSpec((tm, hk), lambda t, c: (t, 0)),
                  pl.BlockSpec((tm, hk), lambda t, c: (t, 0))],
        out_specs=pl.BlockSpec((tm, hk), lambda t, c: (t, 0)),
        out_shape=jax.ShapeDtypeStruct((n, hk), jnp.float32),
        compiler_params=pltpu.CompilerParams(dimension_semantics=("arbitrary", "arbitrary"),
                                             vmem_limit_bytes=VMEM_LIMIT),
        name="peer_act",
    )(h2b, u_t, i1, i2)


def _peer_coef_kernel(act_ref, g_ref, i1_ref, i2_ref, c_ref, coef_ref):
    tm = act_ref.shape[0]
    nk = PEER_N_KEYS
    coef_ref[...] = g_ref[...] * jax.nn.gelu(act_ref[...])
    sub = lax.broadcasted_iota(jnp.int32, (nk, i1_ref.shape[1]), 0)

    def token(t):
        r1 = i1_ref[pl.ds(t, 1), :]
        r2 = i2_ref[pl.ds(t, 1), :]
        cf = coef_ref[pl.ds(t, 1), :]
        m1 = jnp.where(r1 == sub, cf, 0.0).astype(jnp.bfloat16)
        m2t = jnp.where(r2 == sub, 1.0, 0.0).astype(jnp.bfloat16)
        return lax.dot_general(m1, m2t, (((1,), (1,)), ((), ())), preferred_element_type=jnp.float32)

    def body(tg, carry):
        t0 = pl.multiple_of(tg * COEF_GROUP, COEF_GROUP)
        ct = jnp.stack([token(t0 + u) for u in range(COEF_GROUP)], axis=0)
        c_ref[:, pl.ds(t0, COEF_GROUP), :] = pltpu.einshape("tij->itj", ct).astype(c_ref.dtype)
        return carry

    lax.fori_loop(0, tm // COEF_GROUP, body, 0)


COEF_GROUP = 64


def _peer_coef(act, g, i1, i2, tm=128):
    n, hk = act.shape
    nk = PEER_N_KEYS
    row = pl.BlockSpec((tm, hk), lambda t: (t, 0))
    return pl.pallas_call(
        _peer_coef_kernel,
        grid=(n // tm,),
        in_specs=[row, row, row, row],
        out_specs=pl.BlockSpec((nk, tm, nk), lambda t: (0, t, 0)),
        out_shape=jax.ShapeDtypeStruct((nk, n, nk), jnp.bfloat16),
        scratch_shapes=[pltpu.VMEM((tm, hk), jnp.float32)],
        compiler_params=pltpu.CompilerParams(dimension_semantics=("arbitrary",),
                                             vmem_limit_bytes=VMEM_LIMIT),
        name="peer_coef",
    )(act, g, i1, i2)


def _peer_out_kernel(c_ref, v_ref, x1_ref, y_ref, acc_ref):
    k = pl.program_id(1)

    @pl.when(k == 0)
    def _():
        acc_ref[...] = x1_ref[...]

    acc = acc_ref[...]
    nk = PEER_N_KEYS
    for p in range(c_ref.shape[0] // 2):
        lhs = jnp.concatenate([c_ref[2 * p], c_ref[2 * p + 1]], axis=1)
        acc = acc + jnp.dot(lhs, v_ref[2 * p * nk:(2 * p + 2) * nk, :], preferred_element_type=jnp.float32)
    acc_ref[...] = acc

    @pl.when(k == pl.num_programs(1) - 1)
    def _():
        y_ref[...] = acc_ref[...]


def _peer_out(c3, v_b, x1, tm=1024, tk=2048):
    nk, n, _ = c3.shape
    ne = nk * nk
    return pl.pallas_call(
        _peer_out_kernel,
        grid=(n // tm, ne // tk),
        in_specs=[pl.BlockSpec((tk // nk, tm, nk), lambda t, k: (k, t, 0)),
                  pl.BlockSpec((tk, D_MODEL), lambda t, k: (k, 0)),
                  pl.BlockSpec((tm, D_MODEL), lambda t, k: (t, 0))],
        out_specs=pl.BlockSpec((tm, D_MODEL), lambda t, k: (t, 0)),
        out_shape=jax.ShapeDtypeStruct((n, D_MODEL), jnp.float32),
        scratch_shapes=[pltpu.VMEM((tm, D_MODEL), jnp.float32)],
        compiler_params=pltpu.CompilerParams(dimension_semantics=("arbitrary", "arbitrary"),
                                             vmem_limit_bytes=VMEM_LIMIT),
        name="peer_out",
    )(c3, v_b, x1)


def _peer_weights(w_up_nsa, w_up_fox, w_out, norm_ffn, peer_w_query, peer_sub_keys, peer_u, peer_v):
    bf16 = jnp.bfloat16
    return dict(w_up_nsa=w_up_nsa.astype(bf16), w_up_fox=w_up_fox.astype(bf16), w_out=w_out.astype(bf16),
                norm_ffn=norm_ffn, wq_t=peer_w_query.T.astype(bf16),
                sub_keys=peer_sub_keys.reshape(2 * PEER_HEADS, PEER_N_KEYS, PEER_DK_HALF).astype(bf16),
                u=peer_u.T.astype(bf16), v=peer_v.astype(bf16))


def _merge_peer(x2d, o_nsa, o_fox, mg, wts):
    x1, h2b, i1, i2, g = _merge_route(x2d, o_nsa, o_fox, mg, wts['w_up_nsa'], wts['w_up_fox'], wts['w_out'],
                                      wts['norm_ffn'], wts['wq_t'], wts['sub_keys'])
    act = _peer_act(h2b, wts['u'], i1, i2)
    c3 = _peer_coef(act, g, i1, i2)
    return _peer_out(c3, wts['v'], x1)


def kernel(x_prompt, x_sample, cache_nsa, cache_fox_kv, cache_fox_logf, state_nsa_win, page_table,
           norm_attn, w_in, fox_f_bias, nsa_q_norm, nsa_k_norm, fox_q_norm, fox_k_norm,
           cmp_pe, cmp_w1, cmp_w2, w_up_nsa, w_up_fox, w_out, norm_ffn,
           peer_w_query, peer_sub_keys, peer_u, peer_v):
    w_front = _front_weights(w_in)
    bd = _block_diag_mean()
    wts = _peer_weights(w_up_nsa, w_up_fox, w_out, norm_ffn, peer_w_query, peer_sub_keys, peer_u, peer_v)
    cmp_wts = _compress_weights(cmp_pe, cmp_w1, cmp_w2)

    bp, seq, _ = x_prompt.shape
    n_p = bp * seq
    (rows_t, win_t, fox_t, logf_p2d, mg_p, qat, gat, ksel, kwin, vselt, vwint, qbt, kb, vbt, rows_cmp) = _front_attn(
        x_prompt.reshape(n_p, D_MODEL), seq, norm_attn, w_front, bd, fox_f_bias,
        nsa_q_norm, nsa_k_norm, fox_q_norm, fox_k_norm)
    kc, vct = _compress(rows_cmp, seq // CMP_BLOCK, *cmp_wts, nsa_k_norm[0], bd)
    o_nsa_t = _nsa_prompt(qat, gat, ksel, vselt, kwin, vwint, kc, vct, bp, seq)
    o_fox_t = _fox_prompt(qbt, kb, vbt, bp, seq)
    y_p = _merge_peer(x_prompt.reshape(n_p, D_MODEL), o_nsa_t, o_fox_t, mg_p, wts).reshape(x_prompt.shape)
    to_rows = lambda a, *dims: a.reshape(bp, *dims, a.shape[-1]).transpose(0, len(dims) + 1, *range(1, len(dims) + 1))
    nsa_p = to_rows(rows_t, 4, NSA_KV_HEADS, HEAD_DIM)
    fox_p = to_rows(fox_t, 2, FOX_HEADS, HEAD_DIM)
    logf_p = logf_p2d.reshape(bp, seq, FOX_HEADS)
    win_p = to_rows(win_t[:, :, seq - min(WINDOW, seq):], 2, NSA_KV_HEADS, HEAD_DIM)

    db, ns, _ = x_sample.shape
    n_s = db * ns
    n_pool, page = cache_nsa.shape[:2]
    wbuf = state_nsa_win.shape[1]
    qa_s, rows_s, win_s2d, ga_s, qb_s, fox_s2d, logf_s2d, mg_s = _front(
        x_sample.reshape(n_s, D_MODEL), norm_attn, w_front, bd, fox_f_bias,
        nsa_q_norm, nsa_k_norm, fox_q_norm, fox_k_norm)
    o_nsa_s = _nsa_sample(page_table, cache_nsa.transpose(0, 2, 3, 4, 1), qa_s, ga_s, rows_s, win_s2d,
                          state_nsa_win.transpose(0, 2, 3, 4, 1),
                          *_compress_weights_t(cmp_pe, cmp_w1, cmp_w2), nsa_k_norm[0], bd)
    lft_new = jnp.pad(logf_s2d.reshape(db, ns, FOX_HEADS).transpose(0, 2, 1), ((0, 0), (0, 0), (0, page - ns)))
    o_fox_s = _fox_decode(page_table, cache_fox_kv.transpose(0, 2, 3, 4, 1), cache_fox_logf.transpose(0, 2, 1),
                          qb_s, fox_s2d, lft_new)
    y_s = _merge_peer(x_sample.reshape(n_s, D_MODEL), o_nsa_s.T, o_fox_s.T, mg_s, wts).reshape(x_sample.shape)
    nsa_s = rows_s.reshape(db, ns, 4, NSA_KV_HEADS, HEAD_DIM)
    fox_s = fox_s2d.reshape(db, ns, 2, FOX_HEADS, HEAD_DIM)
    logf_s = logf_s2d.reshape(db, ns, FOX_HEADS)
    win_s = jnp.concatenate([state_nsa_win[:, ns:], win_s2d.reshape(db, ns, 2, NSA_KV_HEADS, HEAD_DIM)], axis=1)
    return (y_p, y_s, nsa_p, fox_p, logf_p, win_p, nsa_s, fox_s, logf_s, win_s)
```

```python
import functools

import jax
import jax.numpy as jnp
from jax import lax
from jax.experimental import pallas as pl
from jax.experimental.pallas import tpu as pltpu

D_MODEL = 1024
HEAD_DIM = 64
NSA_HEADS = 8
NSA_KV_HEADS = 2
NSA_GROUP = NSA_HEADS // NSA_KV_HEADS
CMP_BLOCK = 64
SEL_BLOCK = CMP_BLOCK
SEL_TOPK = 16
WINDOW = 512
FOX_HEADS = 8
Q_BLOCK = 128
PEER_HEADS = 8
PEER_N_KEYS = 128
PEER_DK = 256
PEER_DK_HALF = PEER_DK // 2
PEER_TOPK = 16
PEER_CHUNK = 256

NSA_Q_W = NSA_HEADS * HEAD_DIM
NSA_KV_W = 6 * NSA_KV_HEADS * HEAD_DIM
NSA_GATE_W = 3 * NSA_HEADS
FOX_W = FOX_HEADS * HEAD_DIM
FOX_QKV_W = 3 * FOX_W
FOX_F_W = FOX_HEADS
MERGE_W = 2 * D_MODEL
SPLIT_Q_A = NSA_Q_W
SPLIT_KV_A = SPLIT_Q_A + NSA_KV_W
SPLIT_G_A = SPLIT_KV_A + NSA_GATE_W
SPLIT_QKV_B = SPLIT_G_A + FOX_QKV_W
SPLIT_F_B = SPLIT_QKV_B + FOX_F_W
IN_WIDTH = SPLIT_F_B + MERGE_W

SCALE = HEAD_DIM ** -0.5
FORCE_SCORE = float(NSA_GROUP + 1)
NEG_INF = -1e30
EPS = 1e-6

LANE = 128
VMEM_LIMIT = 48 * 1024 * 1024


def _group_mean_sq(x, bd):
    sq = x * x
    hi = sq.astype(jnp.bfloat16)
    lo = (sq - hi.astype(jnp.float32)).astype(jnp.bfloat16)
    return (jnp.dot(hi, bd, preferred_element_type=jnp.float32)
            + jnp.dot(lo, bd, preferred_element_type=jnp.float32))


def _head_rms(x, g, bd):
    outs = []
    for c in range(x.shape[1] // LANE):
        xc = x[:, c * LANE:(c + 1) * LANE]
        outs.append(xc * lax.rsqrt(_group_mean_sq(xc, bd) + EPS) * g)
    return outs[0] if len(outs) == 1 else jnp.concatenate(outs, axis=1)


def _front_kernel(x_ref, na_ref, w_ref, bd_ref, fb_ref, gq_a_ref, gk_sel_ref, gk_win_ref,
                  gq_b_ref, gk_b_ref,
                  qa_ref, rows_ref, win_ref, ga_ref, qb_ref, fox_ref, logf_ref, mg_ref):
    x = x_ref[...]
    h = x * lax.rsqrt(jnp.mean(x * x, axis=-1, keepdims=True) + EPS) * na_ref[...]
    hb = h.astype(jnp.bfloat16)
    bd = bd_ref[...]

    def proj(c0, width):
        return jnp.dot(hb, w_ref[:, c0:c0 + width], preferred_element_type=jnp.float32)

    c = 0
    qa_ref[...] = _head_rms(proj(c, NSA_Q_W), gq_a_ref[...], bd)
    c += NSA_Q_W
    rows_ref[:, 0:256] = proj(c, 256)
    rows_ref[:, 256:384] = _head_rms(proj(c + 256, 128), gk_sel_ref[...], bd)
    rows_ref[:, 384:512] = proj(c + 384, 128)
    win_ref[:, 0:128] = _head_rms(proj(c + 512, 128), gk_win_ref[...], bd)
    win_ref[:, 128:256] = proj(c + 640, 128)
    c += NSA_KV_W
    qb_ref[...] = _head_rms(proj(c, FOX_W), gq_b_ref[...], bd)
    fox_ref[:, 0:FOX_W] = _head_rms(proj(c + FOX_W, FOX_W), gk_b_ref[...], bd)
    fox_ref[:, FOX_W:2 * FOX_W] = proj(c + 2 * FOX_W, FOX_W)
    c += FOX_QKV_W
    for j in range(MERGE_W // 512):
        mg_ref[:, j * 512:(j + 1) * 512] = jax.nn.sigmoid(proj(c + j * 512, 512))
    c += MERGE_W
    ga_ref[...] = jax.nn.sigmoid(proj(c, LANE))
    f = proj(c + LANE, LANE)[:, 0:FOX_F_W] + fb_ref[...]
    logf_ref[...] = jnp.minimum(f, 0.0) - jnp.log1p(jnp.exp(-jnp.abs(f)))


def _front(x2d, norm_attn, w_front, bd, fox_f_bias, nsa_q_norm, nsa_k_norm, fox_q_norm, fox_k_norm,
           tm=256):
    n = x2d.shape[0]
    wf = w_front.shape[1]
    two = lambda g: jnp.concatenate([g, g]).reshape(1, LANE)
    row = lambda w: pl.BlockSpec((tm, w), lambda i: (i, 0))
    full = lambda a: pl.BlockSpec(a.shape, lambda i: (0,) * a.ndim)
    args = (x2d, norm_attn.reshape(1, D_MODEL), w_front, bd, fox_f_bias.reshape(1, FOX_F_W),
            two(nsa_q_norm), two(nsa_k_norm[1]), two(nsa_k_norm[2]), two(fox_q_norm), two(fox_k_norm))
    widths = (NSA_Q_W, 512, 256, LANE, FOX_W, 2 * FOX_W, FOX_F_W, MERGE_W)
    return pl.pallas_call(
        _front_kernel,
        grid=(n // tm,),
        in_specs=[row(D_MODEL)] + [full(a) for a in args[1:]],
        out_specs=[row(w) for w in widths],
        out_shape=[jax.ShapeDtypeStruct((n, w), jnp.float32) for w in widths],
        compiler_params=pltpu.CompilerParams(dimension_semantics=("arbitrary",),
                                             vmem_limit_bytes=VMEM_LIMIT),
        name="front",
    )(*args)


FEAT = HEAD_DIM


def _lane(tm):
    return lax.broadcasted_iota(jnp.int32, (tm, LANE), 1)


def _expand_halves(x):
    lo = _lane(x.shape[0]) < HEAD_DIM
    return jnp.where(lo, x, 0.0), jnp.where(lo, pltpu.roll(x, HEAD_DIM, axis=1), 0.0)


def _split3(x):
    hi = x.astype(jnp.bfloat16)
    r = x - hi.astype(jnp.float32)
    mid = r.astype(jnp.bfloat16)
    lo = (r - mid.astype(jnp.float32)).astype(jnp.bfloat16)
    return hi, mid, lo


def _front_attn_kernel(seq_len, x_ref, na_ref, w_ref, bd_ref, fb_ref, gq_a_ref, gk_sel_ref, gk_win_ref,
                       gq_b_ref, gk_b_ref, place_ref,
                       rows_ref, win_ref, fox_ref, logf_ref, mg_ref,
                       qat_ref, gat_ref, ksel_ref, kwin_ref, vselt_ref, vwint_ref, qbt_ref, kb_ref, vbt_ref, rcmp_ref,
                       carry_ref):
    f32, bf16 = jnp.float32, jnp.bfloat16
    tm = x_ref.shape[0]
    i = pl.program_id(0)
    x = x_ref[...]
    h = x * lax.rsqrt(jnp.mean(x * x, axis=-1, keepdims=True) + EPS) * na_ref[...]
    hb = h.astype(bf16)
    bd = bd_ref[...]
    lane = _lane(tm)
    pos = (i * tm + lax.broadcasted_iota(jnp.int32, (tm, LANE), 0)) % seq_len
    kfeat = jnp.where(lane == FEAT, (pos // SEL_BLOCK).astype(f32),
                      jnp.where(lane == FEAT + 1, (pos % SEL_BLOCK).astype(f32), 0.0))

    def proj(c0, width):
        return jnp.dot(hb, w_ref[:, c0:c0 + width], preferred_element_type=f32)

    c = 0
    qa = _head_rms(proj(c, NSA_Q_W), gq_a_ref[...], bd)
    for j in range(NSA_HEADS // 2):
        for s, half in enumerate(_expand_halves(qa[:, j * LANE:(j + 1) * LANE])):
            hd = 2 * j + s
            slope = 2.0 ** -(hd + 1)
            qfeat = jnp.where(lane == FEAT, slope * SEL_BLOCK, jnp.where(lane == FEAT + 1, slope, 0.0))
            qat_ref[hd * LANE:(hd + 1) * LANE, :] = (half * SCALE + qfeat).T.astype(bf16)
    c += NSA_Q_W
    for j in range(2):
        raw = proj(c + j * LANE, LANE)
        rcmp_ref[:, j * LANE:(j + 1) * LANE] = raw
        rows_ref[0, j * LANE:(j + 1) * LANE, :] = raw.T
    ksel = _head_rms(proj(c + 256, 128), gk_sel_ref[...], bd)
    rows_ref[0, 2 * LANE:3 * LANE, :] = ksel.T
    vsel_t = proj(c + 384, 128).T
    rows_ref[0, 3 * LANE:4 * LANE, :] = vsel_t
    kwin = _head_rms(proj(c + 512, 128), gk_win_ref[...], bd)
    win_ref[0, 0:LANE, :] = kwin.T
    vwin_t = proj(c + 640, 128).T
    win_ref[0, LANE:2 * LANE, :] = vwin_t
    for g, (ks_g, kw_g) in enumerate(zip(_expand_halves(ksel), _expand_halves(kwin))):
        ksel_ref[:, g * LANE:(g + 1) * LANE] = (ks_g + kfeat).astype(bf16)
        kwin_ref[:, g * LANE:(g + 1) * LANE] = (kw_g + kfeat).astype(bf16)
    vselt_ref[...] = vsel_t.astype(bf16)
    vwint_ref[...] = vwin_t.astype(bf16)
    c += NSA_KV_W

    qb = _head_rms(proj(c, FOX_W), gq_b_ref[...], bd)
    ones3 = jnp.where((lane >= FEAT) & (lane < FEAT + 3), 1.0, 0.0)
    for j in range(FOX_HEADS // 2):
        for s, half in enumerate(_expand_halves(qb[:, j * LANE:(j + 1) * LANE])):
            hd = 2 * j + s
            qbt_ref[hd * LANE:(hd + 1) * LANE, :] = (half * SCALE + ones3).T.astype(bf16)
    kbn = _head_rms(proj(c + FOX_W, FOX_W), gk_b_ref[...], bd)
    for j in range(FOX_HEADS // 2):
        fox_ref[0, j * LANE:(j + 1) * LANE, :] = kbn[:, j * LANE:(j + 1) * LANE].T
        vb_t = proj(c + 2 * FOX_W + j * LANE, LANE).T
        fox_ref[0, FOX_W + j * LANE:FOX_W + (j + 1) * LANE, :] = vb_t
        vbt_ref[j * LANE:(j + 1) * LANE, :] = vb_t.astype(bf16)
    c += FOX_QKV_W
    for j in range(MERGE_W // 512):
        mg_ref[:, j * 512:(j + 1) * 512] = jax.nn.sigmoid(proj(c + j * 512, 512))
    c += MERGE_W
    gat_ref[...] = jax.nn.sigmoid(proj(c, LANE)).T
    f = proj(c + LANE, LANE) + fb_ref[...]
    lf = jnp.minimum(f, 0.0) - jnp.log1p(jnp.exp(-jnp.abs(f)))
    logf_ref[...] = lf[:, 0:FOX_F_W]

    @pl.when((i * tm) % seq_len == 0)
    def _():
        carry_ref[...] = jnp.zeros_like(carry_ref)

    r_io = lax.broadcasted_iota(jnp.int32, (tm, tm), 0)
    c_io = lax.broadcasted_iota(jnp.int32, (tm, tm), 1)
    tri = jnp.where(c_io <= r_io, 1.0, 0.0).astype(bf16)
    csum = carry_ref[...] + sum(jnp.dot(tri, p, preferred_element_type=f32) for p in _split3(lf))
    carry_ref[...] = csum[tm - 1:tm, :]
    pieces = jnp.concatenate(_split3(-csum), axis=1)
    cfeat = jnp.dot(pieces, place_ref[...], preferred_element_type=f32)
    for j in range(FOX_HEADS // 2):
        for s, half in enumerate(_expand_halves(kbn[:, j * LANE:(j + 1) * LANE])):
            hd = 2 * j + s
            kb_ref[:, hd * LANE:(hd + 1) * LANE] = (half + cfeat[:, hd * LANE:(hd + 1) * LANE]).astype(bf16)


def _fox_feature_placement():
    r = lax.broadcasted_iota(jnp.int32, (3 * LANE, FOX_HEADS * LANE), 0)
    c = lax.broadcasted_iota(jnp.int32, (3 * LANE, FOX_HEADS * LANE), 1)
    s, hd = r // LANE, r % LANE
    return jnp.where((hd < FOX_HEADS) & (c == hd * LANE + FEAT + s), 1.0, 0.0).astype(jnp.bfloat16)


def _front_attn(x2d, seq_len, norm_attn, w_front, bd, fox_f_bias, nsa_q_norm, nsa_k_norm, fox_q_norm,
                fox_k_norm, tm=256):
    n = x2d.shape[0]
    f32, bf16 = jnp.float32, jnp.bfloat16
    two = lambda g: jnp.concatenate([g, g]).reshape(1, LANE)
    row = lambda w: pl.BlockSpec((tm, w), lambda i: (i, 0))
    col = lambda h: pl.BlockSpec((h, tm), lambda i: (0, i))
    full = lambda a: pl.BlockSpec(a.shape, lambda i: (0,) * a.ndim)
    fb = jnp.pad(fox_f_bias, (0, LANE - FOX_F_W)).reshape(1, LANE)
    args = (x2d, norm_attn.reshape(1, D_MODEL), w_front, bd, fb,
            two(nsa_q_norm), two(nsa_k_norm[1]), two(nsa_k_norm[2]), two(fox_q_norm), two(fox_k_norm),
            _fox_feature_placement())
    per_seq = seq_len // tm
    nseq = n // seq_len
    leaf = lambda h: (pl.BlockSpec((1, h, tm), lambda i: (i // per_seq, 0, i % per_seq)), (nseq, h, seq_len), f32)
    outs = [leaf(512), leaf(256), leaf(2 * FOX_W),
            (row(FOX_F_W), (n, FOX_F_W), f32), (row(MERGE_W), (n, MERGE_W), f32),
            (col(NSA_HEADS * LANE), (NSA_HEADS * LANE, n), bf16), (col(LANE), (LANE, n), f32),
            (row(2 * LANE), (n, 2 * LANE), bf16), (row(2 * LANE), (n, 2 * LANE), bf16),
            (col(LANE), (LANE, n), bf16), (col(LANE), (LANE, n), bf16),
            (col(FOX_HEADS * LANE), (FOX_HEADS * LANE, n), bf16),
            (row(FOX_HEADS * LANE), (n, FOX_HEADS * LANE), bf16), (col(FOX_W), (FOX_W, n), bf16),
            (row(2 * LANE), (n, 2 * LANE), f32)]
    return pl.pallas_call(
        functools.partial(_front_attn_kernel, seq_len),
        grid=(n // tm,),
        in_specs=[row(D_MODEL)] + [full(a) for a in args[1:]],
        out_specs=[o[0] for o in outs],
        out_shape=[jax.ShapeDtypeStruct(o[1], o[2]) for o in outs],
        scratch_shapes=[pltpu.VMEM((1, LANE), f32)],
        compiler_params=pltpu.CompilerParams(dimension_semantics=("arbitrary",),
                                             vmem_limit_bytes=VMEM_LIMIT),
        name="front_attn",
    )(*args)


def _front_weights(w_in):
    pad = lambda w: jnp.pad(w, ((0, 0), (0, LANE - w.shape[1])))
    parts = [w_in[:, :SPLIT_KV_A], w_in[:, SPLIT_G_A:SPLIT_QKV_B], w_in[:, SPLIT_F_B:],
             pad(w_in[:, SPLIT_KV_A:SPLIT_G_A]), pad(w_in[:, SPLIT_QKV_B:SPLIT_F_B])]
    return jnp.concatenate(parts, axis=1).astype(jnp.bfloat16)


def _block_diag_mean():
    r = lax.broadcasted_iota(jnp.int32, (LANE, LANE), 0) // HEAD_DIM
    c = lax.broadcasted_iota(jnp.int32, (LANE, LANE), 1) // HEAD_DIM
    return jnp.where(r == c, 1.0 / HEAD_DIM, 0.0).astype(jnp.bfloat16)


def _softmax_step(tiles, carry):
    m, l, acc = carry
    m_new = functools.reduce(jnp.maximum, [jnp.max(s, axis=0, keepdims=True) for s, _ in tiles], m)
    alpha = jnp.exp(m - m_new)
    ps = [jnp.exp(s - m_new) for s, _ in tiles]
    l = alpha * l + sum(jnp.sum(p, axis=0, keepdims=True) for p in ps)
    acc = alpha * acc + sum(jnp.dot(vt, p.astype(jnp.bfloat16), preferred_element_type=jnp.float32)
                            for p, (_, vt) in zip(ps, tiles))
    return m_new, l, acc


TILE_UNROLL = 4


def _tile_loop(n, logits, values, carry, bufs, tk, last_group, groups=None, unroll=TILE_UNROLL):
    tile = lambda kt: (logits(kt), values(kt))
    nu = n // unroll
    count, group_of = (nu, lambda i: i) if groups is None else groups

    def fill(buf, i):
        grp = jnp.minimum(group_of(jnp.minimum(i, jnp.maximum(count - 1, 0))), last_group)
        for j in range(unroll):
            buf[j * tk:(j + 1) * tk, :] = logits(grp * unroll + j)

    def drain(buf, i, c):
        grp = group_of(i)
        return _softmax_step([(buf[j * tk:(j + 1) * tk, :], values(grp * unroll + j)) for j in range(unroll)], c)

    def body(i, c):
        fill(bufs[1], 2 * i + 1)
        c = drain(bufs[0], 2 * i, c)
        fill(bufs[0], 2 * i + 2)
        return drain(bufs[1], 2 * i + 1, c)

    fill(bufs[0], 0)
    carry = lax.fori_loop(0, count // 2, body, carry)
    carry = lax.cond(count % 2 == 1, lambda c: drain(bufs[0], count - 1, c), lambda c: c, carry)
    base = nu * unroll
    size = unroll // 2
    while size >= 1:
        has = (n & size) != 0
        carry = lax.cond(has, functools.partial(
            lambda b, sz, c: _softmax_step([tile(b + u) for u in range(sz)], c), base, size),
            lambda c: c, carry)
        base = base + jnp.where(has, size, 0)
        size //= 2
    return carry


def _softmax_init(w):
    return (jnp.full((1, w), NEG_INF, jnp.float32), jnp.zeros((1, w), jnp.float32),
            jnp.zeros((HEAD_DIM, w), jnp.float32))


def _summarize(xk_ref, xv_ref, pe_ref, w1_ref, w2_ref, nb):
    f32, bf16 = jnp.float32, jnp.bfloat16
    x_refs = (xk_ref, xv_ref)

    def body(l, accs):
        return tuple(
            acc + jnp.dot((x_refs[s][pl.ds(l, nb, stride=CMP_BLOCK), :] + pe_ref[s, pl.ds(l, 1), :]).astype(bf16),
                          w1_ref[s, l], preferred_element_type=f32)
            for s, acc in enumerate(accs))

    accs = lax.fori_loop(0, CMP_BLOCK, body, (jnp.zeros((nb, LANE), f32),) * 2, unroll=16)
    return tuple(jnp.dot(jax.nn.gelu(acc).astype(bf16), w2_ref[s], preferred_element_type=f32)
                 for s, acc in enumerate(accs))


def _compress_kernel(xk_ref, xv_ref, pe_ref, w1_ref, w2_ref, gk_ref, bd_ref, kc_ref, vct_ref):
    f32, bf16 = jnp.float32, jnp.bfloat16
    nb = kc_ref.shape[0]
    kc, vc = _summarize(xk_ref, xv_ref, pe_ref, w1_ref, w2_ref, nb)
    kc = kc * lax.rsqrt(_group_mean_sq(kc, bd_ref[...]) + EPS) * gk_ref[...]
    lane = _lane(nb)
    blk = lax.broadcasted_iota(jnp.int32, (nb, LANE), 0).astype(f32)
    feat = jnp.where(lane == FEAT, blk, jnp.where(lane == FEAT + 1, float(CMP_BLOCK - 1), 0.0))
    for g, half in enumerate(_expand_halves(kc)):
        kc_ref[:, g * LANE:(g + 1) * LANE] = (half + feat).astype(bf16)
    vct_ref[...] = vc.T.astype(bf16)


def _compress_weights(cmp_pe, cmp_w1, cmp_w2):
    def bdiag(w):
        z = jnp.zeros_like(w)
        return jnp.concatenate([jnp.concatenate([w, z], axis=-1), jnp.concatenate([z, w], axis=-1)], axis=-2)
    pe = jnp.concatenate([cmp_pe, cmp_pe], axis=-1)
    return pe, bdiag(cmp_w1).astype(jnp.bfloat16), bdiag(cmp_w2).astype(jnp.bfloat16)


def _compress(rows2d, nb, pe, w1, w2, gk, bd):
    nseq = rows2d.shape[0] // (nb * CMP_BLOCK)
    full = lambda a: pl.BlockSpec(a.shape, lambda b: (0,) * a.ndim)
    args = (rows2d, rows2d, pe, w1, w2, jnp.concatenate([gk, gk]).reshape(1, LANE), bd)
    return pl.pallas_call(
        _compress_kernel,
        grid=(nseq,),
        in_specs=[pl.BlockSpec((nb * CMP_BLOCK, LANE), lambda b: (b, 0)),
                  pl.BlockSpec((nb * CMP_BLOCK, LANE), lambda b: (b, 1))] + [full(a) for a in args[2:]],
        out_specs=[pl.BlockSpec((nb, 2 * LANE), lambda b: (b, 0)), pl.BlockSpec((LANE, nb), lambda b: (0, b))],
        out_shape=[jax.ShapeDtypeStruct((nseq * nb, 2 * LANE), jnp.bfloat16),
                   jax.ShapeDtypeStruct((LANE, nseq * nb), jnp.bfloat16)],
        compiler_params=pltpu.CompilerParams(dimension_semantics=("arbitrary",),
                                             vmem_limit_bytes=VMEM_LIMIT),
        name="nsa_compress",
    )(*args)


NSA_TQ = 256
NSA_TK = 128


def _nsa_prompt_kernel(qt_ref, gat_ref, ksel_ref, vselt_ref, kwin_ref, vwint_ref, kc_ref, vct_ref,
                       o_ref, selb_ref, buf0_ref, buf1_ref, glist_ref):
    f32, bf16 = jnp.float32, jnp.bfloat16
    tq, tk = NSA_TQ, NSA_TK
    per_q = tq // tk
    w = NSA_GROUP * tq
    g = pl.program_id(1)
    qi = pl.program_id(2)
    nb = kc_ref.shape[0]
    qt = jnp.concatenate([qt_ref[r * LANE:(r + 1) * LANE, :] for r in range(NSA_GROUP)], axis=1)
    qloc = lax.broadcasted_iota(jnp.int32, (1, w), 1) % tq
    qpos = qi * tq + qloc
    krow = lax.broadcasted_iota(jnp.int32, (tk, w), 0)
    key_tile = lambda ref, kt: ref[pl.ds(pl.multiple_of(kt * tk, tk), tk), :]
    val_tile = lambda ref, kt: ref[:, pl.ds(pl.multiple_of(kt * tk, tk), tk)]

    nwin = WINDOW // tk
    tiles = []
    for j in range(nwin + per_q):
        kt = qi * per_q - nwin + j
        ktc = jnp.maximum(kt, 0)
        s = jnp.dot(key_tile(kwin_ref, ktc), qt, preferred_element_type=f32)
        ok = kt >= 0
        if j < per_q:
            ok = ok & (krow > qloc - j * tk)
        if j >= nwin:
            ok = ok & (krow <= qloc + (WINDOW - j * tk))
        tiles.append((jnp.where(ok, s, NEG_INF), val_tile(vwint_ref, ktc)))
    m_w, l_w, acc_w = _softmax_step(tiles, _softmax_init(w))
    o_w = acc_w / l_w

    sc = jnp.dot(kc_ref[...], qt, preferred_element_type=f32)
    blk = lax.broadcasted_iota(jnp.int32, (nb, w), 0)
    vis = blk * CMP_BLOCK + (CMP_BLOCK - 1) <= qpos
    sc = jnp.where(vis, sc, NEG_INF)
    pc = jnp.where(vis, jnp.exp(sc - jnp.max(sc, axis=0, keepdims=True)), 0.0)
    pc = pc / jnp.maximum(jnp.sum(pc, axis=0, keepdims=True), 1e-30)
    o_c = jnp.dot(vct_ref[...], pc.astype(bf16), preferred_element_type=f32)

    imp = sum(pc[:, r * tq:(r + 1) * tq] for r in range(NSA_GROUP))
    blk1 = lax.broadcasted_iota(jnp.int32, (nb, tq), 0)
    cur = (qi * tq + lax.broadcasted_iota(jnp.int32, (1, tq), 1)) // SEL_BLOCK
    imp = jnp.where((blk1 == cur) | (blk1 == 0), FORCE_SCORE, jnp.where(blk1 <= cur, imp, -1.0))
    for _ in range(min(SEL_TOPK, nb)):
        mx = jnp.max(imp, axis=0, keepdims=True)
        idx = jnp.min(jnp.where(imp == mx, blk1, BIG_ID), axis=0, keepdims=True)
        imp = jnp.where(blk1 == idx, -jnp.inf, imp)
    selb = jnp.where(imp == -jnp.inf, 0.0, NEG_INF)
    selb_ref[...] = jnp.concatenate([selb] * NSA_GROUP, axis=1)

    blocks_per_group = TILE_UNROLL * (tk // SEL_BLOCK)
    ntiles = qi * per_q
    nfull = ntiles // TILE_UNROLL
    ngroups = jnp.int32(0)
    for j in range(nb // blocks_per_group):
        picked = imp[j * blocks_per_group:(j + 1) * blocks_per_group, :] == -jnp.inf
        hit = jnp.max(jnp.where(picked, 1.0, 0.0)) > 0.5
        glist_ref[ngroups] = j
        ngroups = ngroups + jnp.where(hit & (j < nfull), 1, 0)

    def sel_bias(kt):
        per = tk // SEL_BLOCK
        return jnp.concatenate([jnp.broadcast_to(selb_ref[pl.ds(kt * per + j, 1), :], (SEL_BLOCK, w))
                                for j in range(per)], axis=0)

    sel_logits = lambda kt: jnp.dot(key_tile(ksel_ref, kt), qt, preferred_element_type=f32) + sel_bias(kt)
    sel_values = lambda kt: val_tile(vselt_ref, kt)
    last_group = ksel_ref.shape[0] // (tk * TILE_UNROLL) - 1
    carry = _tile_loop(ntiles, sel_logits, sel_values, _softmax_init(w), (buf0_ref, buf1_ref), tk, last_group,
                       groups=(ngroups, lambda i: glist_ref[i]))
    diag = [(jnp.where(krow + d * tk <= qloc, sel_logits(ntiles + d), NEG_INF), sel_values(ntiles + d))
            for d in range(per_q)]
    m_s, l_s, acc_s = _softmax_step(diag, carry)
    o_s = acc_s / l_s

    for r in range(NSA_GROUP):
        gate = lambda br: gat_ref[pl.ds((g * NSA_GROUP + r) * 3 + br, 1), :]
        sl = slice(r * tq, (r + 1) * tq)
        o_ref[r * HEAD_DIM:(r + 1) * HEAD_DIM, :] = (
            gate(0) * o_c[:, sl] + gate(1) * o_s[:, sl] + gate(2) * o_w[:, sl])


def _nsa_prompt(qat, gat, ksel, vselt, kwin, vwint, kc, vct, nseq, seq_len):
    n = qat.shape[1]
    tq, tk = NSA_TQ, NSA_TK
    nq = seq_len // tq
    nb = seq_len // CMP_BLOCK
    gw = NSA_GROUP * LANE
    return pl.pallas_call(
        _nsa_prompt_kernel,
        grid=(nseq, NSA_KV_HEADS, nq),
        in_specs=[pl.BlockSpec((gw, tq), lambda b, g, q: (g, b * nq + q)),
                  pl.BlockSpec((LANE, tq), lambda b, g, q: (0, b * nq + q)),
                  pl.BlockSpec((seq_len, LANE), lambda b, g, q: (b, g)),
                  pl.BlockSpec((HEAD_DIM, seq_len), lambda b, g, q: (g, b)),
                  pl.BlockSpec((seq_len, LANE), lambda b, g, q: (b, g)),
                  pl.BlockSpec((HEAD_DIM, seq_len), lambda b, g, q: (g, b)),
                  pl.BlockSpec((nb, LANE), lambda b, g, q: (b, g)),
                  pl.BlockSpec((HEAD_DIM, nb), lambda b, g, q: (g, b))],
        out_specs=pl.BlockSpec((NSA_GROUP * HEAD_DIM, tq), lambda b, g, q: (g, b * nq + q)),
        out_shape=jax.ShapeDtypeStruct((NSA_Q_W, n), jnp.float32),
        scratch_shapes=[pltpu.VMEM((nb, NSA_GROUP * tq), jnp.float32)]
        + [pltpu.VMEM((TILE_UNROLL * tk, NSA_GROUP * tq), jnp.float32)] * 2
        + [pltpu.SMEM((nb // (TILE_UNROLL * (tk // SEL_BLOCK)),), jnp.int32)],
        compiler_params=pltpu.CompilerParams(dimension_semantics=("arbitrary", "arbitrary", "arbitrary"),
                                             vmem_limit_bytes=VMEM_LIMIT),
        name="nsa_prompt",
    )(qat, gat, ksel, vselt, kwin, vwint, kc, vct)


FOX_TQ = 512
FOX_TK = 128


def _fox_prompt_kernel(qt_ref, kb_ref, vbt_ref, o_ref, buf0_ref, buf1_ref):
    f32 = jnp.float32
    tq, tk = FOX_TQ, FOX_TK
    qi = pl.program_id(2)
    qt = qt_ref[...]

    logits = lambda kt: jnp.dot(kb_ref[pl.ds(pl.multiple_of(kt * tk, tk), tk), :], qt, preferred_element_type=f32)
    values = lambda kt: vbt_ref[:, pl.ds(pl.multiple_of(kt * tk, tk), tk)]

    ndiag = tq // tk
    last_group = kb_ref.shape[0] // (tk * TILE_UNROLL) - 1
    carry = _tile_loop(qi * ndiag, logits, values, _softmax_init(tq), (buf0_ref, buf1_ref), tk, last_group)
    krow = lax.broadcasted_iota(jnp.int32, (tk, tq), 0)
    qloc = lax.broadcasted_iota(jnp.int32, (tk, tq), 1)
    diag = []
    for j in range(ndiag):
        kt = qi * ndiag + j
        diag.append((jnp.where(krow + j * tk <= qloc, logits(kt), NEG_INF), values(kt)))
    m, l, acc = _softmax_step(diag, carry)
    o_ref[...] = acc / l


def _fox_prompt(qbt, kb, vbt, nseq, seq_len):
    n = qbt.shape[1]
    nq = seq_len // FOX_TQ
    return pl.pallas_call(
        _fox_prompt_kernel,
        grid=(nseq, FOX_HEADS, nq),
        in_specs=[pl.BlockSpec((LANE, FOX_TQ), lambda b, h, q: (h, b * nq + q)),
                  pl.BlockSpec((seq_len, LANE), lambda b, h, q: (b, h)),
                  pl.BlockSpec((HEAD_DIM, seq_len), lambda b, h, q: (h, b))],
        out_specs=pl.BlockSpec((HEAD_DIM, FOX_TQ), lambda b, h, q: (h, b * nq + q)),
        out_shape=jax.ShapeDtypeStruct((FOX_W, n), jnp.float32),
        scratch_shapes=[pltpu.VMEM((TILE_UNROLL * FOX_TK, FOX_TQ), jnp.float32)] * 2,
        compiler_params=pltpu.CompilerParams(dimension_semantics=("arbitrary", "arbitrary", "arbitrary"),
                                             vmem_limit_bytes=VMEM_LIMIT),
        name="fox_prompt",
    )(qbt, kb, vbt)


DECODE_PER_STEP = 2


def _slope_rows(shape, rows_per_head):
    hd = lax.broadcasted_iota(jnp.int32, shape, 0) // rows_per_head
    return pltpu.bitcast((126 - hd) << 23, jnp.float32)


def _pad_rows(x, rows):
    return jnp.concatenate([x, jnp.zeros((rows - x.shape[0], x.shape[1]), x.dtype)], axis=0)


def _nt_dot(a, b):
    return lax.dot_general(a, b, (((1,), (1,)), ((), ())), preferred_element_type=jnp.float32)


def _joint_softmax(parts):
    m = functools.reduce(jnp.maximum, [jnp.max(p, axis=1, keepdims=True) for p in parts])
    es = [jnp.exp(p - m) for p in parts]
    return es, sum(jnp.sum(e, axis=1, keepdims=True) for e in es)


def _nsa_sample_kernel(npages, ns, nel, pt_ref, *refs):
    f32, bf16 = jnp.float32, jnp.bfloat16
    pages = refs[:nel * npages]
    rest = refs[nel * npages:]
    pe_ref, w1_ref, w2_ref, gk_ref, bd_ref = rest[6:11]
    bufk_ref, bufv_ref = rest[12:14]
    page = pages[0].shape[-1]
    ncol = NSA_KV_HEADS * npages

    for p in range(nel * npages):
        bufk_ref[p * LANE:(p + 1) * LANE, :] = pages[p][0, 0].reshape(LANE, page)
        bufv_ref[p * LANE:(p + 1) * LANE, :] = pages[p][0, 1].reshape(LANE, page)
    kc, vc = _summarize(bufk_ref, bufv_ref, pe_ref, w1_ref, w2_ref, nel * ncol)
    kc = (kc * lax.rsqrt(_group_mean_sq(kc, bd_ref[...]) + EPS) * gk_ref[...]).astype(bf16)
    vc = vc.astype(bf16)

    running = [_nsa_sample_one(e, npages, ns, pages[e * npages:(e + 1) * npages], rest,
                               kc[e * ncol:(e + 1) * ncol], vc[e * ncol:(e + 1) * ncol]) for e in range(nel)]
    while running:
        running = [gen for gen in running if next(gen, "done") != "done"]


def _nsa_sample_one(e, npages, ns, pages, rest, kc, vc):
    f32, bf16 = jnp.float32, jnp.bfloat16
    e_ref, qa_ref, ga_ref, rnew_ref, wnew_ref, state_ref = rest[:6]
    o_ref = rest[11]
    rows = slice(e * ns, (e + 1) * ns)
    page = pages[0].shape[-1]
    past = npages * page
    ncol = NSA_KV_HEADS * npages
    nrow = NSA_HEADS * ns

    lane8 = _lane(ns)
    ql = qa_ref[rows, :]
    qrows = []
    for hd in range(NSA_HEADS):
        g = hd // NSA_GROUP
        t = ql[:, (hd // 2) * LANE:(hd // 2 + 1) * LANE]
        if hd % 2 != g:
            t = pltpu.roll(t, HEAD_DIM, axis=1)
        qrows.append(jnp.where(lane8 // HEAD_DIM == g, t, 0.0))
    qb = (jnp.concatenate(qrows, axis=0) * SCALE).astype(bf16)

    def geom(width):
        tok = lax.broadcasted_iota(jnp.int32, (nrow, width), 0) % ns
        col = lax.broadcasted_iota(jnp.int32, (nrow, width), 1)
        return tok, col, _slope_rows((nrow, width), ns)

    yield
    q_both =qb + pltpu.roll(qb.astype(f32), HEAD_DIM, axis=1).astype(bf16)
    lane64 = lax.broadcasted_iota(jnp.int32, (nrow, LANE), 1) // HEAD_DIM
    tok, col, slope = geom(ncol)
    own = (lax.broadcasted_iota(jnp.int32, (nrow, ncol), 0) // (NSA_GROUP * ns)) == col % NSA_KV_HEADS
    lcs, blks = [], []
    for half in range(2):
        blk = (col // NSA_KV_HEADS) * 2 + half
        lc = _nt_dot(jnp.where(lane64 == half, q_both, 0.0), kc)
        lc = lc - slope * (past + tok - (blk * CMP_BLOCK + CMP_BLOCK - 1)).astype(f32)
        lcs.append(jnp.where(own, lc, NEG_INF))
        blks.append(blk)
    ecs, lsum = _joint_softmax(lcs)
    pcs = [e / lsum for e in ecs]
    res = [jnp.dot(pc.astype(bf16), vc, preferred_element_type=f32) for pc in pcs]
    o_c = jnp.where(lane64 == 0, res[0], res[1])
    o_c = o_c + pltpu.roll(o_c, HEAD_DIM, axis=1)

    yield
    nsel = NSA_KV_HEADS * ns
    col_s = lax.broadcasted_iota(jnp.int32, (nsel, ncol), 1)
    own_s = (lax.broadcasted_iota(jnp.int32, (nsel, ncol), 0) // ns) == col_s % NSA_KV_HEADS
    blks_s = [(col_s // NSA_KV_HEADS) * 2 + half for half in range(2)]
    imps = []
    for half in range(2):
        imp = jnp.concatenate(
            [sum(pcs[half][(g * NSA_GROUP + r) * ns:(g * NSA_GROUP + r + 1) * ns] for r in range(NSA_GROUP))
             for g in range(NSA_KV_HEADS)], axis=0)
        imp = jnp.where(blks_s[half] == 0, FORCE_SCORE, imp)
        imps.append(jnp.where(own_s, imp, -1.0))
    ranks = [jnp.zeros((nsel, ncol), jnp.int32) for _ in range(2)]
    for h2 in range(2):
        for c in range(ncol):
            other = jnp.broadcast_to(imps[h2][:, c:c + 1], (nsel, ncol))
            blk_c = (c // NSA_KV_HEADS) * 2 + h2
            for half in range(2):
                ahead = (other > imps[half]) | ((other == imps[half]) & (blks_s[half] > blk_c))
                ranks[half] = ranks[half] + jnp.where(ahead, 1, 0)
    selexp = 0.0
    for half in range(2):
        sel = jnp.where((ranks[half] < SEL_TOPK - 1) & own_s, 1.0, 0.0)
        sel = jnp.concatenate([sel[g * ns:(g + 1) * ns] for g in range(NSA_KV_HEADS) for _ in range(NSA_GROUP)],
                              axis=0)
        selexp = selexp + jnp.dot(sel.astype(bf16), e_ref[half], preferred_element_type=f32)

    def new_tile(k_new):
        tok, col, slope = geom(LANE)
        s = _nt_dot(qb, _pad_rows(k_new, LANE).astype(bf16))
        return jnp.where(col <= tok, s - slope * (tok - col).astype(f32), NEG_INF)

    def weighted(es, vts, e_new, v_new, lsum):
        acc = sum(_nt_dot(e.astype(bf16), vt.astype(bf16)) for e, vt in zip(es, vts))
        acc = acc + jnp.dot(e_new.astype(bf16), _pad_rows(v_new, LANE).astype(bf16), preferred_element_type=f32)
        return acc / lsum

    yield
    tok, col, slope = geom(past)
    ls = jnp.concatenate([jnp.dot(qb, pages[p][0, 2].reshape(LANE, page).astype(bf16), preferred_element_type=f32)
                          for p in range(npages)], axis=1)
    ls = jnp.where(selexp > 0.5, ls - slope * (past + tok - col).astype(f32), NEG_INF)
    yield
    (es, en), lsum = _joint_softmax([ls, new_tile(rnew_ref[rows, 2 * LANE:3 * LANE])])
    yield
    o_s = weighted([es[:, p * page:(p + 1) * page] for p in range(npages)],
                   [pages[p][0, 3].reshape(LANE, page) for p in range(npages)],
                   en, rnew_ref[rows, 3 * LANE:4 * LANE], lsum)

    yield
    wbuf = state_ref.shape[-1]
    tok, col, slope = geom(wbuf)
    lw = jnp.dot(qb, state_ref[e, 0].reshape(LANE, wbuf).astype(bf16), preferred_element_type=f32)
    lw = jnp.where(col > tok + (wbuf - WINDOW), lw - slope * (wbuf + tok - col).astype(f32), NEG_INF)
    (ew, en), lsum = _joint_softmax([lw, new_tile(wnew_ref[rows, 0:LANE])])
    yield
    o_w = weighted([ew], [state_ref[e, 1].reshape(LANE, wbuf)], en, wnew_ref[rows, LANE:2 * LANE], lsum)

    yield
    ga = ga_ref[rows, :]
    gate = lambda br: jnp.concatenate(
        [jnp.broadcast_to(ga[:, hd * 3 + br:hd * 3 + br + 1], (ns, LANE)) for hd in range(NSA_HEADS)], axis=0)
    o = gate(0) * o_c + gate(1) * o_s + gate(2) * o_w
    for j in range(NSA_HEADS // 2):
        g = (2 * j) // NSA_GROUP
        a = o[2 * j * ns:(2 * j + 1) * ns]
        b = o[(2 * j + 1) * ns:(2 * j + 2) * ns]
        if g == 0:
            b = pltpu.roll(b, HEAD_DIM, axis=1)
        else:
            a = pltpu.roll(a, HEAD_DIM, axis=1)
        o_ref[rows, j * LANE:(j + 1) * LANE] = jnp.where(lane8 < HEAD_DIM, a, b)


def _compress_weights_t(cmp_pe, cmp_w1, cmp_w2):
    pe, w1, w2 = _compress_weights(cmp_pe.transpose(0, 2, 1), cmp_w1.transpose(0, 2, 1, 3), cmp_w2)
    return pe, w1, w2


def _nsa_sample(page_table, cache_t, qa, ga, rows_new, win_new, state_t, pe, w1, w2, gk, bd):
    db, npages = page_table.shape
    page = cache_t.shape[-1]
    assert page == 2 * CMP_BLOCK == LANE
    ns = qa.shape[0] // db
    past = npages * page
    ncol = NSA_KV_HEADS * npages
    half = lax.broadcasted_iota(jnp.int32, (2, ncol, past), 0)
    col = lax.broadcasted_iota(jnp.int32, (2, ncol, past), 1)
    key = lax.broadcasted_iota(jnp.int32, (2, ncol, past), 2)
    expand = jnp.where(key // SEL_BLOCK == (col // NSA_KV_HEADS) * 2 + half, 1.0, 0.0).astype(jnp.bfloat16)
    gk2 = jnp.concatenate([gk, gk]).reshape(1, LANE)
    full = lambda a: pl.BlockSpec(a.shape, lambda b, pt: (0,) * a.ndim)
    nel = DECODE_PER_STEP
    tok = lambda w: pl.BlockSpec((nel * ns, w), lambda b, pt: (b, 0))
    page_specs = [pl.BlockSpec((1,) + cache_t.shape[1:],
                               functools.partial(lambda e, p, b, pt: (pt[b * nel + e, p], 0, 0, 0, 0), e, p))
                  for e in range(nel) for p in range(npages)]
    return pl.pallas_call(
        functools.partial(_nsa_sample_kernel, npages, ns, nel),
        grid_spec=pltpu.PrefetchScalarGridSpec(
            num_scalar_prefetch=1, grid=(db // nel,),
            in_specs=page_specs + [full(expand), tok(NSA_Q_W), tok(LANE), tok(512), tok(256),
                                   pl.BlockSpec((nel,) + state_t.shape[1:], lambda b, pt: (b, 0, 0, 0, 0)),
                                   full(pe), full(w1), full(w2), full(gk2), full(bd)],
            out_specs=tok(NSA_Q_W),
            scratch_shapes=[pltpu.VMEM((nel * npages * LANE, page), jnp.float32),
                            pltpu.VMEM((nel * npages * LANE, page), jnp.float32)]),
        out_shape=jax.ShapeDtypeStruct((db * ns, NSA_Q_W), jnp.float32),
        compiler_params=pltpu.CompilerParams(dimension_semantics=("arbitrary",),
                                             vmem_limit_bytes=VMEM_LIMIT),
        name="nsa_sample",
    )(page_table, *([cache_t] * (nel * npages)), expand, qa, ga, rows_new, win_new, state_t, pe, w1, w2, gk2, bd)


def _fox_decode_kernel(npages, ns, nel, pt_ref, *refs):
    f32, bf16 = jnp.float32, jnp.bfloat16
    qb_ref, knew_ref, lfnew_ref, o_ref = refs[2 * nel * npages:]
    page = refs[0].shape[-1]
    nh = FOX_HEADS
    nrow = nh * ns
    els = range(nel)
    kvt = [refs[e * npages:(e + 1) * npages] for e in els]
    lft = [refs[(nel + e) * npages:(nel + e + 1) * npages] for e in els]
    rows = [slice(e * ns, (e + 1) * ns) for e in els]

    r_io = lax.broadcasted_iota(jnp.int32, (page, page), 0)
    c_io = lax.broadcasted_iota(jnp.int32, (page, page), 1)
    triu = jnp.where(r_io <= c_io, 1.0, 0.0).astype(bf16)
    carry = [jnp.zeros((nh, 1), f32) for _ in els]
    negc = [[] for _ in els]
    for t in range(npages + 1):
        for e in els:
            lf = lft[e][t][0] if t < npages else lfnew_ref[e]
            ct = carry[e] + sum(jnp.dot(pc_, triu, preferred_element_type=f32) for pc_ in _split3(lf))
            carry[e] = ct[:, page - 1:page]
            negc[e].append(jnp.concatenate([jnp.broadcast_to(-ct[hd:hd + 1], (ns, page)) for hd in range(nh)],
                                           axis=0))

    head_of_lane = lax.broadcasted_iota(jnp.int32, (ns, FOX_W), 1) // HEAD_DIM
    q_bd = []
    for e in els:
        q = qb_ref[rows[e], :] * SCALE
        q_bd.append(jnp.concatenate([jnp.where(head_of_lane == hd, q, 0.0) for hd in range(nh)],
                                    axis=0).astype(bf16))

    parts = [[] for _ in els]
    for p in range(npages):
        for e in els:
            parts[e].append(jnp.dot(q_bd[e], kvt[e][p][0, 0].reshape(FOX_W, page).astype(bf16),
                                    preferred_element_type=f32) + negc[e][p])
    tok = lax.broadcasted_iota(jnp.int32, (nrow, page), 0) % ns
    col = lax.broadcasted_iota(jnp.int32, (nrow, page), 1)
    for e in els:
        s_new = _nt_dot(q_bd[e], _pad_rows(knew_ref[rows[e], 0:FOX_W], page).astype(bf16)) + negc[e][npages]
        parts[e].append(jnp.where(col <= tok, s_new, NEG_INF))
    soft = [_joint_softmax(parts[e]) for e in els]
    accs = [jnp.dot(soft[e][0][npages].astype(bf16), _pad_rows(knew_ref[rows[e], FOX_W:2 * FOX_W], page).astype(bf16),
                    preferred_element_type=f32) for e in els]
    for p in range(npages):
        for e in els:
            accs[e] = accs[e] + _nt_dot(soft[e][0][p].astype(bf16), kvt[e][p][0, 1].reshape(FOX_W, page).astype(bf16))
    for e in els:
        acc = accs[e] / soft[e][1]
        o_ref[rows[e], :] = sum(jnp.where(head_of_lane == hd, acc[hd * ns:(hd + 1) * ns], 0.0) for hd in range(nh))


def _fox_decode(page_table, cache_kvt, lft, qb, fox_new, lft_new):
    db, npages = page_table.shape
    ns = qb.shape[0] // db
    nel = DECODE_PER_STEP
    tok = lambda w: pl.BlockSpec((nel * ns, w), lambda b, pt: (b, 0))
    pg = lambda a: [pl.BlockSpec((1,) + a.shape[1:], functools.partial(
        lambda e, p, nd, b, pt: (pt[b * nel + e, p],) + (0,) * nd, e, p, a.ndim - 1))
        for e in range(nel) for p in range(npages)]
    return pl.pallas_call(
        functools.partial(_fox_decode_kernel, npages, ns, nel),
        grid_spec=pltpu.PrefetchScalarGridSpec(
            num_scalar_prefetch=1, grid=(db // nel,),
            in_specs=pg(cache_kvt) + pg(lft) + [tok(FOX_W), tok(2 * FOX_W),
                                                pl.BlockSpec((nel,) + lft_new.shape[1:], lambda b, pt: (b, 0, 0))],
            out_specs=tok(FOX_W)),
        out_shape=jax.ShapeDtypeStruct((db * ns, FOX_W), jnp.float32),
        compiler_params=pltpu.CompilerParams(dimension_semantics=("arbitrary",),
                                             vmem_limit_bytes=VMEM_LIMIT),
        name="fox_decode",
    )(page_table, *([cache_kvt] * (nel * npages)), *([lft] * (nel * npages)), qb, fox_new, lft_new)


BIG_ID = 1 << 20


def _topk_rows(s, k, ids):
    w = ids.shape[1]
    if s.shape[1] > w:
        parts = [_topk_rows(s[:, c:c + w], k, ids) for c in range(0, s.shape[1], w)]
        return jnp.concatenate([p[0] for p in parts], axis=1), jnp.concatenate([p[1] for p in parts], axis=1)
    ids = ids.astype(jnp.float32)
    vals, idxs = [], []
    for _ in range(k):
        m = jnp.max(s, axis=0, keepdims=True)
        idx = jnp.min(jnp.where(s == m, ids, float(BIG_ID)), axis=0, keepdims=True)
        vals.append(m)
        idxs.append(idx)
        s = jnp.where(ids == idx, -jnp.inf, s)
    return jnp.concatenate(vals, axis=0), jnp.concatenate(idxs, axis=0).astype(jnp.int32)


def _pick_rows(sel, table):
    out = jnp.zeros(sel.shape, table.dtype)
    for r in range(table.shape[0]):
        out = jnp.where(sel == r, table[r:r + 1, :], out)
    return out


def _merge_route_kernel(x_ref, on_ref, of_ref, mg_ref, wun_ref, wuf_ref, wo_ref, nf_ref, wqt_ref, sk_ref,
                        x1_ref, h2_ref, i1_ref, i2_ref, g_ref):
    tm = x_ref.shape[0]
    f32, bf16 = jnp.float32, jnp.bfloat16
    tdot = lambda ot, wgt: lax.dot_general(ot.astype(bf16), wgt, (((0,), (0,)), ((), ())),
                                           preferred_element_type=f32)
    a = tdot(on_ref[...], wun_ref[...])
    b = tdot(of_ref[...], wuf_ref[...])
    mixed = mg_ref[:, 0:D_MODEL] * a + mg_ref[:, D_MODEL:2 * D_MODEL] * b
    x1 = x_ref[...] + jnp.dot(mixed.astype(bf16), wo_ref[...], preferred_element_type=f32)
    x1_ref[...] = x1
    h2 = x1 * lax.rsqrt(jnp.mean(x1 * x1, axis=-1, keepdims=True) + EPS) * nf_ref[...]
    h2b = h2.astype(bf16)
    h2_ref[...] = h2b

    nk = PEER_N_KEYS
    key_ids = lax.broadcasted_iota(jnp.int32, (nk, LANE), 0)
    io16 = lax.broadcasted_iota(jnp.int32, (PEER_TOPK, LANE), 0)
    io8 = lax.broadcasted_iota(jnp.int32, (8, LANE), 0)
    cand_ids = jnp.concatenate([io16] + [a_ * PEER_TOPK + io8 for a_ in range(1, 8)]
                               + [(io8 + 8) * PEER_TOPK], axis=0)
    for h in range(PEER_HEADS):
        sv, si = [], []
        for p in range(2):
            hp = 2 * h + p
            qt = lax.dot_general(wqt_ref[hp * PEER_DK_HALF:(hp + 1) * PEER_DK_HALF, :], h2b,
                                 (((1,), (1,)), ((), ())), preferred_element_type=f32)
            st = jnp.dot(sk_ref[hp], qt.astype(bf16), preferred_element_type=f32)
            v, i = _topk_rows(st, PEER_TOPK, key_ids)
            sv.append(v)
            si.append(i)
        s1, s2 = sv
        cand = jnp.concatenate([s1[0:1] + s2] + [s1[a_:a_ + 1] + s2[0:8] for a_ in range(1, 8)]
                               + [s1[8:16] + s2[0:1]], axis=0)
        top, fid = _topk_rows(cand, PEER_TOPK, cand_ids)
        e = jnp.exp(top - jnp.max(top, axis=0, keepdims=True))
        g = e / jnp.sum(e, axis=0, keepdims=True)
        sl = slice(h * PEER_TOPK, (h + 1) * PEER_TOPK)
        i1_ref[:, sl] = _pick_rows(fid >> 4, si[0]).T
        i2_ref[:, sl] = _pick_rows(fid & (PEER_TOPK - 1), si[1]).T
        g_ref[:, sl] = g.T


def _merge_route(x2d, o_nsa, o_fox, mg, w_up_nsa, w_up_fox, w_out, norm_ffn, wq_t, sub_keys, tm=256):
    n = x2d.shape[0]
    row = lambda w: pl.BlockSpec((tm, w), lambda i: (i, 0))
    col = lambda h: pl.BlockSpec((h, tm), lambda i: (0, i))
    full = lambda a: pl.BlockSpec(a.shape, lambda i: (0,) * a.ndim)
    args = (x2d, o_nsa, o_fox, mg, w_up_nsa, w_up_fox, w_out, norm_ffn.reshape(1, D_MODEL), wq_t, sub_keys)
    hk = PEER_HEADS * PEER_TOPK
    return pl.pallas_call(
        _merge_route_kernel,
        grid=(n // tm,),
        in_specs=[row(D_MODEL), col(NSA_Q_W), col(FOX_W), row(MERGE_W)] + [full(a) for a in args[4:]],
        out_specs=[row(D_MODEL), row(D_MODEL), row(hk), row(hk), row(hk)],
        out_shape=[jax.ShapeDtypeStruct((n, D_MODEL), jnp.float32),
                   jax.ShapeDtypeStruct((n, D_MODEL), jnp.bfloat16),
                   jax.ShapeDtypeStruct((n, hk), jnp.int32),
                   jax.ShapeDtypeStruct((n, hk), jnp.int32),
                   jax.ShapeDtypeStruct((n, hk), jnp.float32)],
        compiler_params=pltpu.CompilerParams(dimension_semantics=("arbitrary",),
                                             vmem_limit_bytes=VMEM_LIMIT),
        name="merge_route",
    )(*args)


def _peer_act_kernel(h2_ref, ut_ref, i1_ref, i2_ref, act_ref):
    c = pl.program_id(1)
    ec = ut_ref.shape[1]

    @pl.when(c == 0)
    def _():
        act_ref[...] = jnp.zeros_like(act_ref)

    h2 = h2_ref[...]
    i1 = i1_ref[...]
    i2 = i2_ref[...]
    act = act_ref[...]
    nk = PEER_N_KEYS
    for blk in range(ec // ACT_BLOCK):
        a = jnp.dot(h2, ut_ref[:, blk * ACT_BLOCK:(blk + 1) * ACT_BLOCK],
                    preferred_element_type=jnp.float32)
        for ii in range(ACT_BLOCK // nk):
            got = jnp.take_along_axis(a[:, ii * nk:(ii + 1) * nk], i2, axis=1)
            act = jnp.where(i1 == c * (ec // nk) + blk * (ACT_BLOCK // nk) + ii, got, act)
    act_ref[...] = act


ACT_BLOCK = 512


def _peer_act(h2b, u_t, i1, i2, tm=1024, ec=2048):
    n = h2b.shape[0]
    hk = i1.shape[1]
    return pl.pallas_call(
        _peer_act_kernel,
        grid=(n // tm, u_t.shape[1] // ec),
        in_specs=[pl.BlockSpec((tm, D_MODEL), lambda t, c: (t, 0)),
                  pl.BlockSpec((D_MODEL, ec), lambda t, c: (0, c)),
                  pl.BlockSpec((tm, hk), lambda t, c: (t, 0)),
                  pl.BlockSpec((tm, hk), lambda t, c: (t, 0))],
        out_specs=pl.BlockSpec((tm, hk), lambda t, c: (t, 0)),
        out_shape=jax.ShapeDtypeStruct((n, hk), jnp.float32),
        compiler_params=pltpu.CompilerParams(dimension_semantics=("arbitrary", "arbitrary"),
                                             vmem_limit_bytes=VMEM_LIMIT),
        name="peer_act",
    )(h2b, u_t, i1, i2)


def _peer_coef_kernel(act_ref, g_ref, i1_ref, i2_ref, c_ref, coef_ref):
    tm = act_ref.shape[0]
    nk = PEER_N_KEYS
    coef_ref[...] = g_ref[...] * jax.nn.gelu(act_ref[...])
    sub = lax.broadcasted_iota(jnp.int32, (nk, i1_ref.shape[1]), 0)

    def token(t):
        r1 = i1_ref[pl.ds(t, 1), :]
        r2 = i2_ref[pl.ds(t, 1), :]
        cf = coef_ref[pl.ds(t, 1), :]
        m1 = jnp.where(r1 == sub, cf, 0.0).astype(jnp.bfloat16)
        m2t = jnp.where(r2 == sub, 1.0, 0.0).astype(jnp.bfloat16)
        return lax.dot_general(m1, m2t, (((1,), (1,)), ((), ())), preferred_element_type=jnp.float32)

    def body(tg, carry):
        t0 = pl.multiple_of(tg * COEF_GROUP, COEF_GROUP)
        ct = jnp.stack([token(t0 + u) for u in range(COEF_GROUP)], axis=0)
        c_ref[:, pl.ds(t0, COEF_GROUP), :] = pltpu.einshape("tij->itj", ct).astype(c_ref.dtype)
        return carry

    lax.fori_loop(0, tm // COEF_GROUP, body, 0)


COEF_GROUP = 64


def _peer_coef(act, g, i1, i2, tm=128):
    n, hk = act.shape
    nk = PEER_N_KEYS
    row = pl.BlockSpec((tm, hk), lambda t: (t, 0))
    return pl.pallas_call(
        _peer_coef_kernel,
        grid=(n // tm,),
        in_specs=[row, row, row, row],
        out_specs=pl.BlockSpec((nk, tm, nk), lambda t: (0, t, 0)),
        out_shape=jax.ShapeDtypeStruct((nk, n, nk), jnp.bfloat16),
        scratch_shapes=[pltpu.VMEM((tm, hk), jnp.float32)],
        compiler_params=pltpu.CompilerParams(dimension_semantics=("arbitrary",),
                                             vmem_limit_bytes=VMEM_LIMIT),
        name="peer_coef",
    )(act, g, i1, i2)


def _peer_out_kernel(c_ref, v_ref, x1_ref, y_ref, acc_ref):
    k = pl.program_id(1)

    @pl.when(k == 0)
    def _():
        acc_ref[...] = x1_ref[...]

    acc = acc_ref[...]
    nk = PEER_N_KEYS
    for p in range(c_ref.shape[0] // 2):
        lhs = jnp.concatenate([c_ref[2 * p], c_ref[2 * p + 1]], axis=1)
        acc = acc + jnp.dot(lhs, v_ref[2 * p * nk:(2 * p + 2) * nk, :], preferred_element_type=jnp.float32)
    acc_ref[...] = acc

    @pl.when(k == pl.num_programs(1) - 1)
    def _():
        y_ref[...] = acc_ref[...]


def _peer_out(c3, v_b, x1, tm=1024, tk=2048):
    nk, n, _ = c3.shape
    ne = nk * nk
    return pl.pallas_call(
        _peer_out_kernel,
        grid=(n // tm, ne // tk),
        in_specs=[pl.BlockSpec((tk // nk, tm, nk), lambda t, k: (k, t, 0)),
                  pl.BlockSpec((tk, D_MODEL), lambda t, k: (k, 0)),
                  pl.BlockSpec((tm, D_MODEL), lambda t, k: (t, 0))],
        out_specs=pl.BlockSpec((tm, D_MODEL), lambda t, k: (t, 0)),
        out_shape=jax.ShapeDtypeStruct((n, D_MODEL), jnp.float32),
        scratch_shapes=[pltpu.VMEM((tm, D_MODEL), jnp.float32)],
        compiler_params=pltpu.CompilerParams(dimension_semantics=("arbitrary", "arbitrary"),
                                             vmem_limit_bytes=VMEM_LIMIT),
        name="peer_out",
    )(c3, v_b, x1)


def _peer_weights(w_up_nsa, w_up_fox, w_out, norm_ffn, peer_w_query, peer_sub_keys, peer_u, peer_v):
    bf16 = jnp.bfloat16
    return dict(w_up_nsa=w_up_nsa.astype(bf16), w_up_fox=w_up_fox.astype(bf16), w_out=w_out.astype(bf16),
                norm_ffn=norm_ffn, wq_t=peer_w_query.T.astype(bf16),
                sub_keys=peer_sub_keys.reshape(2 * PEER_HEADS, PEER_N_KEYS, PEER_DK_HALF).astype(bf16),
                u=peer_u.T.astype(bf16), v=peer_v.astype(bf16))


def _merge_peer(x2d, o_nsa, o_fox, mg, wts):
    x1, h2b, i1, i2, g = _merge_route(x2d, o_nsa, o_fox, mg, wts['w_up_nsa'], wts['w_up_fox'], wts['w_out'],
                                      wts['norm_ffn'], wts['wq_t'], wts['sub_keys'])
    act = _peer_act(h2b, wts['u'], i1, i2)
    c3 = _peer_coef(act, g, i1, i2)
    return _peer_out(c3, wts['v'], x1)


def kernel(x_prompt, x_sample, cache_nsa, cache_fox_kv, cache_fox_logf, state_nsa_win, page_table,
           norm_attn, w_in, fox_f_bias, nsa_q_norm, nsa_k_norm, fox_q_norm, fox_k_norm,
           cmp_pe, cmp_w1, cmp_w2, w_up_nsa, w_up_fox, w_out, norm_ffn,
           peer_w_query, peer_sub_keys, peer_u, peer_v):
    w_front = _front_weights(w_in)
    bd = _block_diag_mean()
    wts = _peer_weights(w_up_nsa, w_up_fox, w_out, norm_ffn, peer_w_query, peer_sub_keys, peer_u, peer_v)
    cmp_wts = _compress_weights(cmp_pe, cmp_w1, cmp_w2)

    bp, seq, _ = x_prompt.shape
    n_p = bp * seq
    (rows_t, win_t, fox_t, logf_p2d, mg_p, qat, gat, ksel, kwin, vselt, vwint, qbt, kb, vbt, rows_cmp) = _front_attn(
        x_prompt.reshape(n_p, D_MODEL), seq, norm_attn, w_front, bd, fox_f_bias,
        nsa_q_norm, nsa_k_norm, fox_q_norm, fox_k_norm)
    kc, vct = _compress(rows_cmp, seq // CMP_BLOCK, *cmp_wts, nsa_k_norm[0], bd)
    o_nsa_t = _nsa_prompt(qat, gat, ksel, vselt, kwin, vwint, kc, vct, bp, seq)
    o_fox_t = _fox_prompt(qbt, kb, vbt, bp, seq)
    y_p = _merge_peer(x_prompt.reshape(n_p, D_MODEL), o_nsa_t, o_fox_t, mg_p, wts).reshape(x_prompt.shape)
    to_rows = lambda a, *dims: a.reshape(bp, *dims, a.shape[-1]).transpose(0, len(dims) + 1, *range(1, len(dims) + 1))
    nsa_p = to_rows(rows_t, 4, NSA_KV_HEADS, HEAD_DIM)
    fox_p = to_rows(fox_t, 2, FOX_HEADS, HEAD_DIM)
    logf_p = logf_p2d.reshape(bp, seq, FOX_HEADS)
    win_p = to_rows(win_t[:, :, seq - min(WINDOW, seq):], 2, NSA_KV_HEADS, HEAD_DIM)

    db, ns, _ = x_sample.shape
    n_s = db * ns
    n_pool, page = cache_nsa.shape[:2]
    wbuf = state_nsa_win.shape[1]
    qa_s, rows_s, win_s2d, ga_s, qb_s, fox_s2d, logf_s2d, mg_s = _front(
        x_sample.reshape(n_s, D_MODEL), norm_attn, w_front, bd, fox_f_bias,
        nsa_q_norm, nsa_k_norm, fox_q_norm, fox_k_norm)
    o_nsa_s = _nsa_sample(page_table, cache_nsa.transpose(0, 2, 3, 4, 1), qa_s, ga_s, rows_s, win_s2d,
                          state_nsa_win.transpose(0, 2, 3, 4, 1),
                          *_compress_weights_t(cmp_pe, cmp_w1, cmp_w2), nsa_k_norm[0], bd)
    lft_new = jnp.pad(logf_s2d.reshape(db, ns, FOX_HEADS).transpose(0, 2, 1), ((0, 0), (0, 0), (0, page - ns)))
    o_fox_s = _fox_decode(page_table, cache_fox_kv.transpose(0, 2, 3, 4, 1), cache_fox_logf.transpose(0, 2, 1),
                          qb_s, fox_s2d, lft_new)
    y_s = _merge_peer(x_sample.reshape(n_s, D_MODEL), o_nsa_s.T, o_fox_s.T, mg_s, wts).reshape(x_sample.shape)
    nsa_s = rows_s.reshape(db, ns, 4, NSA_KV_HEADS, HEAD_DIM)
    fox_s = fox_s2d.reshape(db, ns, 2, FOX_HEADS, HEAD_DIM)
    logf_s = logf_s2d.reshape(db, ns, FOX_HEADS)
    win_s = jnp.concatenate([state_nsa_win[:, ns:], win_s2d.reshape(db, ns, 2, NSA_KV_HEADS, HEAD_DIM)], axis=1)
    return (y_p, y_s, nsa_p, fox_p, logf_p, win_p, nsa_s, fox_s, logf_s, win_s)
```

```python
import functools

import jax
import jax.numpy as jnp
from jax import lax
from jax.experimental import pallas as pl
from jax.experimental.pallas import tpu as pltpu

D_MODEL = 1024
HEAD_DIM = 64
NSA_HEADS = 8
NSA_KV_HEADS = 2
NSA_GROUP = NSA_HEADS // NSA_KV_HEADS
CMP_BLOCK = 64
SEL_BLOCK = CMP_BLOCK
SEL_TOPK = 16
WINDOW = 512
FOX_HEADS = 8
Q_BLOCK = 128
PEER_HEADS = 8
PEER_N_KEYS = 128
PEER_DK = 256
PEER_DK_HALF = PEER_DK // 2
PEER_TOPK = 16
PEER_CHUNK = 256

NSA_Q_W = NSA_HEADS * HEAD_DIM
NSA_KV_W = 6 * NSA_KV_HEADS * HEAD_DIM
NSA_GATE_W = 3 * NSA_HEADS
FOX_W = FOX_HEADS * HEAD_DIM
FOX_QKV_W = 3 * FOX_W
FOX_F_W = FOX_HEADS
MERGE_W = 2 * D_MODEL
SPLIT_Q_A = NSA_Q_W
SPLIT_KV_A = SPLIT_Q_A + NSA_KV_W
SPLIT_G_A = SPLIT_KV_A + NSA_GATE_W
SPLIT_QKV_B = SPLIT_G_A + FOX_QKV_W
SPLIT_F_B = SPLIT_QKV_B + FOX_F_W
IN_WIDTH = SPLIT_F_B + MERGE_W

SCALE = HEAD_DIM ** -0.5
FORCE_SCORE = float(NSA_GROUP + 1)
NEG_INF = -1e30
EPS = 1e-6

LANE = 128
VMEM_LIMIT = 48 * 1024 * 1024


def _group_mean_sq(x, bd):
    sq = x * x
    hi = sq.astype(jnp.bfloat16)
    lo = (sq - hi.astype(jnp.float32)).astype(jnp.bfloat16)
    return (jnp.dot(hi, bd, preferred_element_type=jnp.float32)
            + jnp.dot(lo, bd, preferred_element_type=jnp.float32))


def _head_rms(x, g, bd):
    outs = []
    for c in range(x.shape[1] // LANE):
        xc = x[:, c * LANE:(c + 1) * LANE]
        outs.append(xc * lax.rsqrt(_group_mean_sq(xc, bd) + EPS) * g)
    return outs[0] if len(outs) == 1 else jnp.concatenate(outs, axis=1)


def _front_kernel(x_ref, na_ref, w_ref, bd_ref, fb_ref, gq_a_ref, gk_sel_ref, gk_win_ref,
                  gq_b_ref, gk_b_ref,
                  qa_ref, rows_ref, win_ref, ga_ref, qb_ref, fox_ref, logf_ref, mg_ref):
    x = x_ref[...]
    h = x * lax.rsqrt(jnp.mean(x * x, axis=-1, keepdims=True) + EPS) * na_ref[...]
    hb = h.astype(jnp.bfloat16)
    bd = bd_ref[...]

    def proj(c0, width):
        return jnp.dot(hb, w_ref[:, c0:c0 + width], preferred_element_type=jnp.float32)

    c = 0
    qa_ref[...] = _head_rms(proj(c, NSA_Q_W), gq_a_ref[...], bd)
    c += NSA_Q_W
    rows_ref[:, 0:256] = proj(c, 256)
    rows_ref[:, 256:384] = _head_rms(proj(c + 256, 128), gk_sel_ref[...], bd)
    rows_ref[:, 384:512] = proj(c + 384, 128)
    win_ref[:, 0:128] = _head_rms(proj(c + 512, 128), gk_win_ref[...], bd)
    win_ref[:, 128:256] = proj(c + 640, 128)
    c += NSA_KV_W
    qb_ref[...] = _head_rms(proj(c, FOX_W), gq_b_ref[...], bd)
    fox_ref[:, 0:FOX_W] = _head_rms(proj(c + FOX_W, FOX_W), gk_b_ref[...], bd)
    fox_ref[:, FOX_W:2 * FOX_W] = proj(c + 2 * FOX_W, FOX_W)
    c += FOX_QKV_W
    for j in range(MERGE_W // 512):
        mg_ref[:, j * 512:(j + 1) * 512] = jax.nn.sigmoid(proj(c + j * 512, 512))
    c += MERGE_W
    ga_ref[...] = jax.nn.sigmoid(proj(c, LANE))
    f = proj(c + LANE, LANE)[:, 0:FOX_F_W] + fb_ref[...]
    logf_ref[...] = jnp.minimum(f, 0.0) - jnp.log1p(jnp.exp(-jnp.abs(f)))


def _front(x2d, norm_attn, w_front, bd, fox_f_bias, nsa_q_norm, nsa_k_norm, fox_q_norm, fox_k_norm,
           tm=256):
    n = x2d.shape[0]
    wf = w_front.shape[1]
    two = lambda g: jnp.concatenate([g, g]).reshape(1, LANE)
    row = lambda w: pl.BlockSpec((tm, w), lambda i: (i, 0))
    full = lambda a: pl.BlockSpec(a.shape, lambda i: (0,) * a.ndim)
    args = (x2d, norm_attn.reshape(1, D_MODEL), w_front, bd, fox_f_bias.reshape(1, FOX_F_W),
            two(nsa_q_norm), two(nsa_k_norm[1]), two(nsa_k_norm[2]), two(fox_q_norm), two(fox_k_norm))
    widths = (NSA_Q_W, 512, 256, LANE, FOX_W, 2 * FOX_W, FOX_F_W, MERGE_W)
    return pl.pallas_call(
        _front_kernel,
        grid=(n // tm,),
        in_specs=[row(D_MODEL)] + [full(a) for a in args[1:]],
        out_specs=[row(w) for w in widths],
        out_shape=[jax.ShapeDtypeStruct((n, w), jnp.float32) for w in widths],
        compiler_params=pltpu.CompilerParams(dimension_semantics=("arbitrary",),
                                             vmem_limit_bytes=VMEM_LIMIT),
        name="front",
    )(*args)


FEAT = HEAD_DIM


def _lane(tm):
    return lax.broadcasted_iota(jnp.int32, (tm, LANE), 1)


def _expand_halves(x):
    lo = _lane(x.shape[0]) < HEAD_DIM
    return jnp.where(lo, x, 0.0), jnp.where(lo, pltpu.roll(x, HEAD_DIM, axis=1), 0.0)


def _split3(x):
    hi = x.astype(jnp.bfloat16)
    r = x - hi.astype(jnp.float32)
    mid = r.astype(jnp.bfloat16)
    lo = (r - mid.astype(jnp.float32)).astype(jnp.bfloat16)
    return hi, mid, lo


def _front_attn_kernel(seq_len, x_ref, na_ref, w_ref, bd_ref, fb_ref, gq_a_ref, gk_sel_ref, gk_win_ref,
                       gq_b_ref, gk_b_ref, place_ref,
                       rows_ref, win_ref, fox_ref, logf_ref, mg_ref,
                       qat_ref, gat_ref, ksel_ref, kwin_ref, vselt_ref, vwint_ref, qbt_ref, kb_ref, vbt_ref, rcmp_ref,
                       carry_ref):
    f32, bf16 = jnp.float32, jnp.bfloat16
    tm = x_ref.shape[0]
    i = pl.program_id(0)
    x = x_ref[...]
    h = x * lax.rsqrt(jnp.mean(x * x, axis=-1, keepdims=True) + EPS) * na_ref[...]
    hb = h.astype(bf16)
    bd = bd_ref[...]
    lane = _lane(tm)
    pos = (i * tm + lax.broadcasted_iota(jnp.int32, (tm, LANE), 0)) % seq_len
    kfeat = jnp.where(lane == FEAT, (pos // SEL_BLOCK).astype(f32),
                      jnp.where(lane == FEAT + 1, (pos % SEL_BLOCK).astype(f32), 0.0))

    def proj(c0, width):
        return jnp.dot(hb, w_ref[:, c0:c0 + width], preferred_element_type=f32)

    c = 0
    qa = _head_rms(proj(c, NSA_Q_W), gq_a_ref[...], bd)
    for j in range(NSA_HEADS // 2):
        for s, half in enumerate(_expand_halves(qa[:, j * LANE:(j + 1) * LANE])):
            hd = 2 * j + s
            slope = 2.0 ** -(hd + 1)
            qfeat = jnp.where(lane == FEAT, slope * SEL_BLOCK, jnp.where(lane == FEAT + 1, slope, 0.0))
            qat_ref[hd * LANE:(hd + 1) * LANE, :] = (half * SCALE + qfeat).T.astype(bf16)
    c += NSA_Q_W
    for j in range(2):
        raw = proj(c + j * LANE, LANE)
        rcmp_ref[:, j * LANE:(j + 1) * LANE] = raw
        rows_ref[0, j * LANE:(j + 1) * LANE, :] = raw.T
    ksel = _head_rms(proj(c + 256, 128), gk_sel_ref[...], bd)
    rows_ref[0, 2 * LANE:3 * LANE, :] = ksel.T
    vsel_t = proj(c + 384, 128).T
    rows_ref[0, 3 * LANE:4 * LANE, :] = vsel_t
    kwin = _head_rms(proj(c + 512, 128), gk_win_ref[...], bd)
    win_ref[0, 0:LANE, :] = kwin.T
    vwin_t = proj(c + 640, 128).T
    win_ref[0, LANE:2 * LANE, :] = vwin_t
    for g, (ks_g, kw_g) in enumerate(zip(_expand_halves(ksel), _expand_halves(kwin))):
        ksel_ref[:, g * LANE:(g + 1) * LANE] = (ks_g + kfeat).astype(bf16)
        kwin_ref[:, g * LANE:(g + 1) * LANE] = (kw_g + kfeat).astype(bf16)
    vselt_ref[...] = vsel_t.astype(bf16)
    vwint_ref[...] = vwin_t.astype(bf16)
    c += NSA_KV_W

    qb = _head_rms(proj(c, FOX_W), gq_b_ref[...], bd)
    ones3 = jnp.where((lane >= FEAT) & (lane < FEAT + 3), 1.0, 0.0)
    for j in range(FOX_HEADS // 2):
        for s, half in enumerate(_expand_halves(qb[:, j * LANE:(j + 1) * LANE])):
            hd = 2 * j + s
            qbt_ref[hd * LANE:(hd + 1) * LANE, :] = (half * SCALE + ones3).T.astype(bf16)
    kbn = _head_rms(proj(c + FOX_W, FOX_W), gk_b_ref[...], bd)
    for j in range(FOX_HEADS // 2):
        fox_ref[0, j * LANE:(j + 1) * LANE, :] = kbn[:, j * LANE:(j + 1) * LANE].T
        vb_t = proj(c + 2 * FOX_W + j * LANE, LANE).T
        fox_ref[0, FOX_W + j * LANE:FOX_W + (j + 1) * LANE, :] = vb_t
        vbt_ref[j * LANE:(j + 1) * LANE, :] = vb_t.astype(bf16)
    c += FOX_QKV_W
    for j in range(MERGE_W // 512):
        mg_ref[:, j * 512:(j + 1) * 512] = jax.nn.sigmoid(proj(c + j * 512, 512))
    c += MERGE_W
    gat_ref[...] = jax.nn.sigmoid(proj(c, LANE)).T
    f = proj(c + LANE, LANE) + fb_ref[...]
    lf = jnp.minimum(f, 0.0) - jnp.log1p(jnp.exp(-jnp.abs(f)))
    logf_ref[...] = lf[:, 0:FOX_F_W]

    @pl.when((i * tm) % seq_len == 0)
    def _():
        carry_ref[...] = jnp.zeros_like(carry_ref)

    r_io = lax.broadcasted_iota(jnp.int32, (tm, tm), 0)
    c_io = lax.broadcasted_iota(jnp.int32, (tm, tm), 1)
    tri = jnp.where(c_io <= r_io, 1.0, 0.0).astype(bf16)
    csum = carry_ref[...] + sum(jnp.dot(tri, p, preferred_element_type=f32) for p in _split3(lf))
    carry_ref[...] = csum[tm - 1:tm, :]
    pieces = jnp.concatenate(_split3(-csum), axis=1)
    cfeat = jnp.dot(pieces, place_ref[...], preferred_element_type=f32)
    for j in range(FOX_HEADS // 2):
        for s, half in enumerate(_expand_halves(kbn[:, j * LANE:(j + 1) * LANE])):
            hd = 2 * j + s
            kb_ref[:, hd * LANE:(hd + 1) * LANE] = (half + cfeat[:, hd * LANE:(hd + 1) * LANE]).astype(bf16)


def _fox_feature_placement():
    r = lax.broadcasted_iota(jnp.int32, (3 * LANE, FOX_HEADS * LANE), 0)
    c = lax.broadcasted_iota(jnp.int32, (3 * LANE, FOX_HEADS * LANE), 1)
    s, hd = r // LANE, r % LANE
    return jnp.where((hd < FOX_HEADS) & (c == hd * LANE + FEAT + s), 1.0, 0.0).astype(jnp.bfloat16)


def _front_attn(x2d, seq_len, norm_attn, w_front, bd, fox_f_bias, nsa_q_norm, nsa_k_norm, fox_q_norm,
                fox_k_norm, tm=256):
    n = x2d.shape[0]
    f32, bf16 = jnp.float32, jnp.bfloat16
    two = lambda g: jnp.concatenate([g, g]).reshape(1, LANE)
    row = lambda w: pl.BlockSpec((tm, w), lambda i: (i, 0))
    col = lambda h: pl.BlockSpec((h, tm), lambda i: (0, i))
    full = lambda a: pl.BlockSpec(a.shape, lambda i: (0,) * a.ndim)
    fb = jnp.pad(fox_f_bias, (0, LANE - FOX_F_W)).reshape(1, LANE)
    args = (x2d, norm_attn.reshape(1, D_MODEL), w_front, bd, fb,
            two(nsa_q_norm), two(nsa_k_norm[1]), two(nsa_k_norm[2]), two(fox_q_norm), two(fox_k_norm),
            _fox_feature_placement())
    per_seq = seq_len // tm
    nseq = n // seq_len
    leaf = lambda h: (pl.BlockSpec((1, h, tm), lambda i: (i // per_seq, 0, i % per_seq)), (nseq, h, seq_len), f32)
    outs = [leaf(512), leaf(256), leaf(2 * FOX_W),
            (row(FOX_F_W), (n, FOX_F_W), f32), (row(MERGE_W), (n, MERGE_W), f32),
            (col(NSA_HEADS * LANE), (NSA_HEADS * LANE, n), bf16), (col(LANE), (LANE, n), f32),
            (row(2 * LANE), (n, 2 * LANE), bf16), (row(2 * LANE), (n, 2 * LANE), bf16),
            (col(LANE), (LANE, n), bf16), (col(LANE), (LANE, n), bf16),
            (col(FOX_HEADS * LANE), (FOX_HEADS * LANE, n), bf16),
            (row(FOX_HEADS * LANE), (n, FOX_HEADS * LANE), bf16), (col(FOX_W), (FOX_W, n), bf16),
            (row(2 * LANE), (n, 2 * LANE), f32)]
    return pl.pallas_call(
        functools.partial(_front_attn_kernel, seq_len),
        grid=(n // tm,),
        in_specs=[row(D_MODEL)] + [full(a) for a in args[1:]],
        out_specs=[o[0] for o in outs],
        out_shape=[jax.ShapeDtypeStruct(o[1], o[2]) for o in outs],
        scratch_shapes=[pltpu.VMEM((1, LANE), f32)],
        compiler_params=pltpu.CompilerParams(dimension_semantics=("arbitrary",),
                                             vmem_limit_bytes=VMEM_LIMIT),
        name="front_attn",
    )(*args)


def _front_weights(w_in):
    pad = lambda w: jnp.pad(w, ((0, 0), (0, LANE - w.shape[1])))
    parts = [w_in[:, :SPLIT_KV_A], w_in[:, SPLIT_G_A:SPLIT_QKV_B], w_in[:, SPLIT_F_B:],
             pad(w_in[:, SPLIT_KV_A:SPLIT_G_A]), pad(w_in[:, SPLIT_QKV_B:SPLIT_F_B])]
    return jnp.concatenate(parts, axis=1).astype(jnp.bfloat16)


def _block_diag_mean():
    r = lax.broadcasted_iota(jnp.int32, (LANE, LANE), 0) // HEAD_DIM
    c = lax.broadcasted_iota(jnp.int32, (LANE, LANE), 1) // HEAD_DIM
    return jnp.where(r == c, 1.0 / HEAD_DIM, 0.0).astype(jnp.bfloat16)


def _softmax_step(tiles, carry):
    m, l, acc = carry
    m_new = functools.reduce(jnp.maximum, [jnp.max(s, axis=0, keepdims=True) for s, _ in tiles], m)
    alpha = jnp.exp(m - m_new)
    ps = [jnp.exp(s - m_new) for s, _ in tiles]
    l = alpha * l + sum(jnp.sum(p, axis=0, keepdims=True) for p in ps)
    acc = alpha * acc + sum(jnp.dot(vt, p.astype(jnp.bfloat16), preferred_element_type=jnp.float32)
                            for p, (_, vt) in zip(ps, tiles))
    return m_new, l, acc


TILE_UNROLL = 4


def _tile_loop(n, logits, values, carry, bufs, tk, last_group, groups=None, unroll=TILE_UNROLL):
    tile = lambda kt: (logits(kt), values(kt))
    nu = n // unroll
    count, group_of = (nu, lambda i: i) if groups is None else groups

    def fill(buf, i):
        grp = jnp.minimum(group_of(jnp.minimum(i, jnp.maximum(count - 1, 0))), last_group)
        for j in range(unroll):
            buf[j * tk:(j + 1) * tk, :] = logits(grp * unroll + j)

    def drain(buf, i, c):
        grp = group_of(i)
        return _softmax_step([(buf[j * tk:(j + 1) * tk, :], values(grp * unroll + j)) for j in range(unroll)], c)

    def body(i, c):
        fill(bufs[1], 2 * i + 1)
        c = drain(bufs[0], 2 * i, c)
        fill(bufs[0], 2 * i + 2)
        return drain(bufs[1], 2 * i + 1, c)

    fill(bufs[0], 0)
    carry = lax.fori_loop(0, count // 2, body, carry)
    carry = lax.cond(count % 2 == 1, lambda c: drain(bufs[0], count - 1, c), lambda c: c, carry)
    base = nu * unroll
    size = unroll // 2
    while size >= 1:
        has = (n & size) != 0
        carry = lax.cond(has, functools.partial(
            lambda b, sz, c: _softmax_step([tile(b + u) for u in range(sz)], c), base, size),
            lambda c: c, carry)
        base = base + jnp.where(has, size, 0)
        size //= 2
    return carry


def _softmax_init(w):
    return (jnp.full((1, w), NEG_INF, jnp.float32), jnp.zeros((1, w), jnp.float32),
            jnp.zeros((HEAD_DIM, w), jnp.float32))


def _summarize(xk_ref, xv_ref, pe_ref, w1_ref, w2_ref, nb):
    f32, bf16 = jnp.float32, jnp.bfloat16
    x_refs = (xk_ref, xv_ref)

    def body(l, accs):
        return tuple(
            acc + jnp.dot((x_refs[s][pl.ds(l, nb, stride=CMP_BLOCK), :] + pe_ref[s, pl.ds(l, 1), :]).astype(bf16),
                          w1_ref[s, l], preferred_element_type=f32)
            for s, acc in enumerate(accs))

    accs = lax.fori_loop(0, CMP_BLOCK, body, (jnp.zeros((nb, LANE), f32),) * 2, unroll=16)
    return tuple(jnp.dot(jax.nn.gelu(acc).astype(bf16), w2_ref[s], preferred_element_type=f32)
                 for s, acc in enumerate(accs))


def _compress_kernel(xk_ref, xv_ref, pe_ref, w1_ref, w2_ref, gk_ref, bd_ref, kc_ref, vct_ref):
    f32, bf16 = jnp.float32, jnp.bfloat16
    nb = kc_ref.shape[0]
    kc, vc = _summarize(xk_ref, xv_ref, pe_ref, w1_ref, w2_ref, nb)
    kc = kc * lax.rsqrt(_group_mean_sq(kc, bd_ref[...]) + EPS) * gk_ref[...]
    lane = _lane(nb)
    blk = lax.broadcasted_iota(jnp.int32, (nb, LANE), 0).astype(f32)
    feat = jnp.where(lane == FEAT, blk, jnp.where(lane == FEAT + 1, float(CMP_BLOCK - 1), 0.0))
    for g, half in enumerate(_expand_halves(kc)):
        kc_ref[:, g * LANE:(g + 1) * LANE] = (half + feat).astype(bf16)
    vct_ref[...] = vc.T.astype(bf16)


def _compress_weights(cmp_pe, cmp_w1, cmp_w2):
    def bdiag(w):
        z = jnp.zeros_like(w)
        return jnp.concatenate([jnp.concatenate([w, z], axis=-1), jnp.concatenate([z, w], axis=-1)], axis=-2)
    pe = jnp.concatenate([cmp_pe, cmp_pe], axis=-1)
    return pe, bdiag(cmp_w1).astype(jnp.bfloat16), bdiag(cmp_w2).astype(jnp.bfloat16)


def _compress(rows2d, nb, pe, w1, w2, gk, bd):
    nseq = rows2d.shape[0] // (nb * CMP_BLOCK)
    full = lambda a: pl.BlockSpec(a.shape, lambda b: (0,) * a.ndim)
    args = (rows2d, rows2d, pe, w1, w2, jnp.concatenate([gk, gk]).reshape(1, LANE), bd)
    return pl.pallas_call(
        _compress_kernel,
        grid=(nseq,),
        in_specs=[pl.BlockSpec((nb * CMP_BLOCK, LANE), lambda b: (b, 0)),
                  pl.BlockSpec((nb * CMP_BLOCK, LANE), lambda b: (b, 1))] + [full(a) for a in args[2:]],
        out_specs=[pl.BlockSpec((nb, 2 * LANE), lambda b: (b, 0)), pl.BlockSpec((LANE, nb), lambda b: (0, b))],
        out_shape=[jax.ShapeDtypeStruct((nseq * nb, 2 * LANE), jnp.bfloat16),
                   jax.ShapeDtypeStruct((LANE, nseq * nb), jnp.bfloat16)],
        compiler_params=pltpu.CompilerParams(dimension_semantics=("arbitrary",),
                                             vmem_limit_bytes=VMEM_LIMIT),
        name="nsa_compress",
    )(*args)


NSA_TQ = 256
NSA_TK = 128


def _nsa_prompt_kernel(qt_ref, gat_ref, ksel_ref, vselt_ref, kwin_ref, vwint_ref, kc_ref, vct_ref,
                       o_ref, selb_ref, buf0_ref, buf1_ref, glist_ref):
    f32, bf16 = jnp.float32, jnp.bfloat16
    tq, tk = NSA_TQ, NSA_TK
    per_q = tq // tk
    w = NSA_GROUP * tq
    g = pl.program_id(1)
    qi = pl.program_id(2)
    nb = kc_ref.shape[0]
    qt = jnp.concatenate([qt_ref[r * LANE:(r + 1) * LANE, :] for r in range(NSA_GROUP)], axis=1)
    qloc = lax.broadcasted_iota(jnp.int32, (1, w), 1) % tq
    qpos = qi * tq + qloc
    krow = lax.broadcasted_iota(jnp.int32, (tk, w), 0)
    key_tile = lambda ref, kt: ref[pl.ds(pl.multiple_of(kt * tk, tk), tk), :]
    val_tile = lambda ref, kt: ref[:, pl.ds(pl.multiple_of(kt * tk, tk), tk)]

    nwin = WINDOW // tk
    tiles = []
    for j in range(nwin + per_q):
        kt = qi * per_q - nwin + j
        ktc = jnp.maximum(kt, 0)
        s = jnp.dot(key_tile(kwin_ref, ktc), qt, preferred_element_type=f32)
        ok = kt >= 0
        if j < per_q:
            ok = ok & (krow > qloc - j * tk)
        if j >= nwin:
            ok = ok & (krow <= qloc + (WINDOW - j * tk))
        tiles.append((jnp.where(ok, s, NEG_INF), val_tile(vwint_ref, ktc)))
    m_w, l_w, acc_w = _softmax_step(tiles, _softmax_init(w))
    o_w = acc_w / l_w

    sc = jnp.dot(kc_ref[...], qt, preferred_element_type=f32)
    blk = lax.broadcasted_iota(jnp.int32, (nb, w), 0)
    vis = blk * CMP_BLOCK + (CMP_BLOCK - 1) <= qpos
    sc = jnp.where(vis, sc, NEG_INF)
    pc = jnp.where(vis, jnp.exp(sc - jnp.max(sc, axis=0, keepdims=True)), 0.0)
    pc = pc / jnp.maximum(jnp.sum(pc, axis=0, keepdims=True), 1e-30)
    o_c = jnp.dot(vct_ref[...], pc.astype(bf16), preferred_element_type=f32)

    imp = sum(pc[:, r * tq:(r + 1) * tq] for r in range(NSA_GROUP))
    blk1 = lax.broadcasted_iota(jnp.int32, (nb, tq), 0)
    cur = (qi * tq + lax.broadcasted_iota(jnp.int32, (1, tq), 1)) // SEL_BLOCK
    imp = jnp.where((blk1 == cur) | (blk1 == 0), FORCE_SCORE, jnp.where(blk1 <= cur, imp, -1.0))
    for _ in range(min(SEL_TOPK, nb)):
        mx = jnp.max(imp, axis=0, keepdims=True)
        idx = jnp.min(jnp.where(imp == mx, blk1, BIG_ID), axis=0, keepdims=True)
        imp = jnp.where(blk1 == idx, -jnp.inf, imp)
    selb = jnp.where(imp == -jnp.inf, 0.0, NEG_INF)
    selb_ref[...] = jnp.concatenate([selb] * NSA_GROUP, axis=1)

    blocks_per_group = TILE_UNROLL * (tk // SEL_BLOCK)
    ntiles = qi * per_q
    nfull = ntiles // TILE_UNROLL
    ngroups = jnp.int32(0)
    for j in range(nb // blocks_per_group):
        picked = imp[j * blocks_per_group:(j + 1) * blocks_per_group, :] == -jnp.inf
        hit = jnp.max(jnp.where(picked, 1.0, 0.0)) > 0.5
        glist_ref[ngroups] = j
        ngroups = ngroups + jnp.where(hit & (j < nfull), 1, 0)

    def sel_bias(kt):
        per = tk // SEL_BLOCK
        return jnp.concatenate([jnp.broadcast_to(selb_ref[pl.ds(kt * per + j, 1), :], (SEL_BLOCK, w))
                                for j in range(per)], axis=0)

    sel_logits = lambda kt: jnp.dot(key_tile(ksel_ref, kt), qt, preferred_element_type=f32) + sel_bias(kt)
    sel_values = lambda kt: val_tile(vselt_ref, kt)
    last_group = ksel_ref.shape[0] // (tk * TILE_UNROLL) - 1
    carry = _tile_loop(ntiles, sel_logits, sel_values, _softmax_init(w), (buf0_ref, buf1_ref), tk, last_group,
                       groups=(ngroups, lambda i: glist_ref[i]))
    diag = [(jnp.where(krow + d * tk <= qloc, sel_logits(ntiles + d), NEG_INF), sel_values(ntiles + d))
            for d in range(per_q)]
    m_s, l_s, acc_s = _softmax_step(diag, carry)
    o_s = acc_s / l_s

    for r in range(NSA_GROUP):
        gate = lambda br: gat_ref[pl.ds((g * NSA_GROUP + r) * 3 + br, 1), :]
        sl = slice(r * tq, (r + 1) * tq)
        o_ref[r * HEAD_DIM:(r + 1) * HEAD_DIM, :] = (
            gate(0) * o_c[:, sl] + gate(1) * o_s[:, sl] + gate(2) * o_w[:, sl])


def _nsa_prompt(qat, gat, ksel, vselt, kwin, vwint, kc, vct, nseq, seq_len):
    n = qat.shape[1]
    tq, tk = NSA_TQ, NSA_TK
    nq = seq_len // tq
    nb = seq_len // CMP_BLOCK
    gw = NSA_GROUP * LANE
    return pl.pallas_call(
        _nsa_prompt_kernel,
        grid=(nseq, NSA_KV_HEADS, nq),
        in_specs=[pl.BlockSpec((gw, tq), lambda b, g, q: (g, b * nq + q)),
                  pl.BlockSpec((LANE, tq), lambda b, g, q: (0, b * nq + q)),
                  pl.BlockSpec((seq_len, LANE), lambda b, g, q: (b, g)),
                  pl.BlockSpec((HEAD_DIM, seq_len), lambda b, g, q: (g, b)),
                  pl.BlockSpec((seq_len, LANE), lambda b, g, q: (b, g)),
                  pl.BlockSpec((HEAD_DIM, seq_len), lambda b, g, q: (g, b)),
                  pl.BlockSpec((nb, LANE), lambda b, g, q: (b, g)),
                  pl.BlockSpec((HEAD_DIM, nb), lambda b, g, q: (g, b))],
        out_specs=pl.BlockSpec((NSA_GROUP * HEAD_DIM, tq), lambda b, g, q: (g, b * nq + q)),
        out_shape=jax.ShapeDtypeStruct((NSA_Q_W, n), jnp.float32),
        scratch_shapes=[pltpu.VMEM((nb, NSA_GROUP * tq), jnp.float32)]
        + [pltpu.VMEM((TILE_UNROLL * tk, NSA_GROUP * tq), jnp.float32)] * 2
        + [pltpu.SMEM((nb // (TILE_UNROLL * (tk // SEL_BLOCK)),), jnp.int32)],
        compiler_params=pltpu.CompilerParams(dimension_semantics=("arbitrary", "arbitrary", "arbitrary"),
                                             vmem_limit_bytes=VMEM_LIMIT),
        name="nsa_prompt",
    )(qat, gat, ksel, vselt, kwin, vwint, kc, vct)


FOX_TQ = 1024
FOX_TK = 128


def _fox_prompt_kernel(qt_ref, kb_ref, vbt_ref, o_ref, buf0_ref, buf1_ref):
    f32 = jnp.float32
    tq, tk = FOX_TQ, FOX_TK
    qi = pl.program_id(2)
    qt = qt_ref[...]

    logits = lambda kt: jnp.dot(kb_ref[pl.ds(pl.multiple_of(kt * tk, tk), tk), :], qt, preferred_element_type=f32)
    values = lambda kt: vbt_ref[:, pl.ds(pl.multiple_of(kt * tk, tk), tk)]

    ndiag = tq // tk
    last_group = kb_ref.shape[0] // (tk * TILE_UNROLL) - 1
    carry = _tile_loop(qi * ndiag, logits, values, _softmax_init(tq), (buf0_ref, buf1_ref), tk, last_group)
    krow = lax.broadcasted_iota(jnp.int32, (tk, tq), 0)
    qloc = lax.broadcasted_iota(jnp.int32, (tk, tq), 1)
    diag = []
    for j in range(ndiag):
        kt = qi * ndiag + j
        diag.append((jnp.where(krow + j * tk <= qloc, logits(kt), NEG_INF), values(kt)))
    m, l, acc = _softmax_step(diag, carry)
    o_ref[...] = acc / l


def _fox_prompt(qbt, kb, vbt, nseq, seq_len):
    n = qbt.shape[1]
    nq = seq_len // FOX_TQ
    return pl.pallas_call(
        _fox_prompt_kernel,
        grid=(nseq, FOX_HEADS, nq),
        in_specs=[pl.BlockSpec((LANE, FOX_TQ), lambda b, h, q: (h, b * nq + q)),
                  pl.BlockSpec((seq_len, LANE), lambda b, h, q: (b, h)),
                  pl.BlockSpec((HEAD_DIM, seq_len), lambda b, h, q: (h, b))],
        out_specs=pl.BlockSpec((HEAD_DIM, FOX_TQ), lambda b, h, q: (h, b * nq + q)),
        out_shape=jax.ShapeDtypeStruct((FOX_W, n), jnp.float32),
        scratch_shapes=[pltpu.VMEM((TILE_UNROLL * FOX_TK, FOX_TQ), jnp.float32)] * 2,
        compiler_params=pltpu.CompilerParams(dimension_semantics=("arbitrary", "arbitrary", "arbitrary"),
                                             vmem_limit_bytes=VMEM_LIMIT),
        name="fox_prompt",
    )(qbt, kb, vbt)


DECODE_PER_STEP = 2


def _slope_rows(shape, rows_per_head):
    hd = lax.broadcasted_iota(jnp.int32, shape, 0) // rows_per_head
    return pltpu.bitcast((126 - hd) << 23, jnp.float32)


def _pad_rows(x, rows):
    return jnp.concatenate([x, jnp.zeros((rows - x.shape[0], x.shape[1]), x.dtype)], axis=0)


def _nt_dot(a, b):
    return lax.dot_general(a, b, (((1,), (1,)), ((), ())), preferred_element_type=jnp.float32)


def _joint_softmax(parts):
    m = functools.reduce(jnp.maximum, [jnp.max(p, axis=1, keepdims=True) for p in parts])
    es = [jnp.exp(p - m) for p in parts]
    return es, sum(jnp.sum(e, axis=1, keepdims=True) for e in es)


def _nsa_sample_kernel(npages, ns, nel, pt_ref, *refs):
    f32, bf16 = jnp.float32, jnp.bfloat16
    pages = refs[:nel * npages]
    rest = refs[nel * npages:]
    pe_ref, w1_ref, w2_ref, gk_ref, bd_ref = rest[6:11]
    bufk_ref, bufv_ref = rest[12:14]
    page = pages[0].shape[-1]
    ncol = NSA_KV_HEADS * npages

    for p in range(nel * npages):
        bufk_ref[p * LANE:(p + 1) * LANE, :] = pages[p][0, 0].reshape(LANE, page)
        bufv_ref[p * LANE:(p + 1) * LANE, :] = pages[p][0, 1].reshape(LANE, page)
    kc, vc = _summarize(bufk_ref, bufv_ref, pe_ref, w1_ref, w2_ref, nel * ncol)
    kc = (kc * lax.rsqrt(_group_mean_sq(kc, bd_ref[...]) + EPS) * gk_ref[...]).astype(bf16)
    vc = vc.astype(bf16)

    running = [_nsa_sample_one(e, npages, ns, pages[e * npages:(e + 1) * npages], rest,
                               kc[e * ncol:(e + 1) * ncol], vc[e * ncol:(e + 1) * ncol]) for e in range(nel)]
    while running:
        running = [gen for gen in running if next(gen, "done") != "done"]


def _nsa_sample_one(e, npages, ns, pages, rest, kc, vc):
    f32, bf16 = jnp.float32, jnp.bfloat16
    e_ref, qa_ref, ga_ref, rnew_ref, wnew_ref, state_ref = rest[:6]
    o_ref = rest[11]
    rows = slice(e * ns, (e + 1) * ns)
    page = pages[0].shape[-1]
    past = npages * page
    ncol = NSA_KV_HEADS * npages
    nrow = NSA_HEADS * ns

    lane8 = _lane(ns)
    ql = qa_ref[rows, :]
    qrows = []
    for hd in range(NSA_HEADS):
        g = hd // NSA_GROUP
        t = ql[:, (hd // 2) * LANE:(hd // 2 + 1) * LANE]
        if hd % 2 != g:
            t = pltpu.roll(t, HEAD_DIM, axis=1)
        qrows.append(jnp.where(lane8 // HEAD_DIM == g, t, 0.0))
    qb = (jnp.concatenate(qrows, axis=0) * SCALE).astype(bf16)

    def geom(width):
        tok = lax.broadcasted_iota(jnp.int32, (nrow, width), 0) % ns
        col = lax.broadcasted_iota(jnp.int32, (nrow, width), 1)
        return tok, col, _slope_rows((nrow, width), ns)

    yield
    q_both =qb + pltpu.roll(qb.astype(f32), HEAD_DIM, axis=1).astype(bf16)
    lane64 = lax.broadcasted_iota(jnp.int32, (nrow, LANE), 1) // HEAD_DIM
    tok, col, slope = geom(ncol)
    own = (lax.broadcasted_iota(jnp.int32, (nrow, ncol), 0) // (NSA_GROUP * ns)) == col % NSA_KV_HEADS
    lcs, blks = [], []
    for half in range(2):
        blk = (col // NSA_KV_HEADS) * 2 + half
        lc = _nt_dot(jnp.where(lane64 == half, q_both, 0.0), kc)
        lc = lc - slope * (past + tok - (blk * CMP_BLOCK + CMP_BLOCK - 1)).astype(f32)
        lcs.append(jnp.where(own, lc, NEG_INF))
        blks.append(blk)
    ecs, lsum = _joint_softmax(lcs)
    pcs = [e / lsum for e in ecs]
    res = [jnp.dot(pc.astype(bf16), vc, preferred_element_type=f32) for pc in pcs]
    o_c = jnp.where(lane64 == 0, res[0], res[1])
    o_c = o_c + pltpu.roll(o_c, HEAD_DIM, axis=1)

    yield
    nsel = NSA_KV_HEADS * ns
    col_s = lax.broadcasted_iota(jnp.int32, (nsel, ncol), 1)
    own_s = (lax.broadcasted_iota(jnp.int32, (nsel, ncol), 0) // ns) == col_s % NSA_KV_HEADS
    blks_s = [(col_s // NSA_KV_HEADS) * 2 + half for half in range(2)]
    imps = []
    for half in range(2):
        imp = jnp.concatenate(
            [sum(pcs[half][(g * NSA_GROUP + r) * ns:(g * NSA_GROUP + r + 1) * ns] for r in range(NSA_GROUP))
             for g in range(NSA_KV_HEADS)], axis=0)
        imp = jnp.where(blks_s[half] == 0, FORCE_SCORE, imp)
        imps.append(jnp.where(own_s, imp, -1.0))
    ranks = [jnp.zeros((nsel, ncol), jnp.int32) for _ in range(2)]
    for h2 in range(2):
        for c in range(ncol):
            other = jnp.broadcast_to(imps[h2][:, c:c + 1], (nsel, ncol))
            blk_c = (c // NSA_KV_HEADS) * 2 + h2
            for half in range(2):
                ahead = (other > imps[half]) | ((other == imps[half]) & (blks_s[half] > blk_c))
                ranks[half] = ranks[half] + jnp.where(ahead, 1, 0)
    selexp = 0.0
    for half in range(2):
        sel = jnp.where((ranks[half] < SEL_TOPK - 1) & own_s, 1.0, 0.0)
        sel = jnp.concatenate([sel[g * ns:(g + 1) * ns] for g in range(NSA_KV_HEADS) for _ in range(NSA_GROUP)],
                              axis=0)
        selexp = selexp + jnp.dot(sel.astype(bf16), e_ref[half], preferred_element_type=f32)

    def new_tile(k_new):
        tok, col, slope = geom(LANE)
        s = _nt_dot(qb, _pad_rows(k_new, LANE).astype(bf16))
        return jnp.where(col <= tok, s - slope * (tok - col).astype(f32), NEG_INF)

    def weighted(es, vts, e_new, v_new, lsum):
        acc = sum(_nt_dot(e.astype(bf16), vt.astype(bf16)) for e, vt in zip(es, vts))
        acc = acc + jnp.dot(e_new.astype(bf16), _pad_rows(v_new, LANE).astype(bf16), preferred_element_type=f32)
        return acc / lsum

    yield
    tok, col, slope = geom(past)
    ls = jnp.concatenate([jnp.dot(qb, pages[p][0, 2].reshape(LANE, page).astype(bf16), preferred_element_type=f32)
                          for p in range(npages)], axis=1)
    ls = jnp.where(selexp > 0.5, ls - slope * (past + tok - col).astype(f32), NEG_INF)
    yield
    (es, en), lsum = _joint_softmax([ls, new_tile(rnew_ref[rows, 2 * LANE:3 * LANE])])
    yield
    o_s = weighted([es[:, p * page:(p + 1) * page] for p in range(npages)],
                   [pages[p][0, 3].reshape(LANE, page) for p in range(npages)],
                   en, rnew_ref[rows, 3 * LANE:4 * LANE], lsum)

    yield
    wbuf = state_ref.shape[-1]
    tok, col, slope = geom(wbuf)
    lw = jnp.dot(qb, state_ref[e, 0].reshape(LANE, wbuf).astype(bf16), preferred_element_type=f32)
    lw = jnp.where(col > tok + (wbuf - WINDOW), lw - slope * (wbuf + tok - col).astype(f32), NEG_INF)
    (ew, en), lsum = _joint_softmax([lw, new_tile(wnew_ref[rows, 0:LANE])])
    yield
    o_w = weighted([ew], [state_ref[e, 1].reshape(LANE, wbuf)], en, wnew_ref[rows, LANE:2 * LANE], lsum)

    yield
    ga = ga_ref[rows, :]
    gate = lambda br: jnp.concatenate(
        [jnp.broadcast_to(ga[:, hd * 3 + br:hd * 3 + br + 1], (ns, LANE)) for hd in range(NSA_HEADS)], axis=0)
    o = gate(0) * o_c + gate(1) * o_s + gate(2) * o_w
    for j in range(NSA_HEADS // 2):
        g = (2 * j) // NSA_GROUP
        a = o[2 * j * ns:(2 * j + 1) * ns]
        b = o[(2 * j + 1) * ns:(2 * j + 2) * ns]
        if g == 0:
            b = pltpu.roll(b, HEAD_DIM, axis=1)
        else:
            a = pltpu.roll(a, HEAD_DIM, axis=1)
        o_ref[rows, j * LANE:(j + 1) * LANE] = jnp.where(lane8 < HEAD_DIM, a, b)


def _compress_weights_t(cmp_pe, cmp_w1, cmp_w2):
    pe, w1, w2 = _compress_weights(cmp_pe.transpose(0, 2, 1), cmp_w1.transpose(0, 2, 1, 3), cmp_w2)
    return pe, w1, w2


def _nsa_sample(page_table, cache_t, qa, ga, rows_new, win_new, state_t, pe, w1, w2, gk, bd):
    db, npages = page_table.shape
    page = cache_t.shape[-1]
    assert page == 2 * CMP_BLOCK == LANE
    ns = qa.shape[0] // db
    past = npages * page
    ncol = NSA_KV_HEADS * npages
    half = lax.broadcasted_iota(jnp.int32, (2, ncol, past), 0)
    col = lax.broadcasted_iota(jnp.int32, (2, ncol, past), 1)
    key = lax.broadcasted_iota(jnp.int32, (2, ncol, past), 2)
    expand = jnp.where(key // SEL_BLOCK == (col // NSA_KV_HEADS) * 2 + half, 1.0, 0.0).astype(jnp.bfloat16)
    gk2 = jnp.concatenate([gk, gk]).reshape(1, LANE)
    full = lambda a: pl.BlockSpec(a.shape, lambda b, pt: (0,) * a.ndim)
    nel = DECODE_PER_STEP
    tok = lambda w: pl.BlockSpec((nel * ns, w), lambda b, pt: (b, 0))
    page_specs = [pl.BlockSpec((1,) + cache_t.shape[1:],
                               functools.partial(lambda e, p, b, pt: (pt[b * nel + e, p], 0, 0, 0, 0), e, p))
                  for e in range(nel) for p in range(npages)]
    return pl.pallas_call(
        functools.partial(_nsa_sample_kernel, npages, ns, nel),
        grid_spec=pltpu.PrefetchScalarGridSpec(
            num_scalar_prefetch=1, grid=(db // nel,),
            in_specs=page_specs + [full(expand), tok(NSA_Q_W), tok(LANE), tok(512), tok(256),
                                   pl.BlockSpec((nel,) + state_t.shape[1:], lambda b, pt: (b, 0, 0, 0, 0)),
                                   full(pe), full(w1), full(w2), full(gk2), full(bd)],
            out_specs=tok(NSA_Q_W),
            scratch_shapes=[pltpu.VMEM((nel * npages * LANE, page), jnp.float32),
                            pltpu.VMEM((nel * npages * LANE, page), jnp.float32)]),
        out_shape=jax.ShapeDtypeStruct((db * ns, NSA_Q_W), jnp.float32),
        compiler_params=pltpu.CompilerParams(dimension_semantics=("arbitrary",),
                                             vmem_limit_bytes=VMEM_LIMIT),
        name="nsa_sample",
    )(page_table, *([cache_t] * (nel * npages)), expand, qa, ga, rows_new, win_new, state_t, pe, w1, w2, gk2, bd)


def _fox_decode_kernel(npages, ns, nel, pt_ref, *refs):
    f32, bf16 = jnp.float32, jnp.bfloat16
    qb_ref, knew_ref, lfnew_ref, o_ref = refs[2 * nel * npages:]
    page = refs[0].shape[-1]
    nh = FOX_HEADS
    nrow = nh * ns
    els = range(nel)
    kvt = [refs[e * npages:(e + 1) * npages] for e in els]
    lft = [refs[(nel + e) * npages:(nel + e + 1) * npages] for e in els]
    rows = [slice(e * ns, (e + 1) * ns) for e in els]

    r_io = lax.broadcasted_iota(jnp.int32, (page, page), 0)
    c_io = lax.broadcasted_iota(jnp.int32, (page, page), 1)
    triu = jnp.where(r_io <= c_io, 1.0, 0.0).astype(bf16)
    carry = [jnp.zeros((nh, 1), f32) for _ in els]
    negc = [[] for _ in els]
    for t in range(npages + 1):
        for e in els:
            lf = lft[e][t][0] if t < npages else lfnew_ref[e]
            ct = carry[e] + sum(jnp.dot(pc_, triu, preferred_element_type=f32) for pc_ in _split3(lf))
            carry[e] = ct[:, page - 1:page]
            negc[e].append(jnp.concatenate([jnp.broadcast_to(-ct[hd:hd + 1], (ns, page)) for hd in range(nh)],
                                           axis=0))

    head_of_lane = lax.broadcasted_iota(jnp.int32, (ns, FOX_W), 1) // HEAD_DIM
    q_bd = []
    for e in els:
        q = qb_ref[rows[e], :] * SCALE
        q_bd.append(jnp.concatenate([jnp.where(head_of_lane == hd, q, 0.0) for hd in range(nh)],
                                    axis=0).astype(bf16))

    parts = [[] for _ in els]
    for p in range(npages):
        for e in els:
            parts[e].append(jnp.dot(q_bd[e], kvt[e][p][0, 0].reshape(FOX_W, page).astype(bf16),
                                    preferred_element_type=f32) + negc[e][p])
    tok = lax.broadcasted_iota(jnp.int32, (nrow, page), 0) % ns
    col = lax.broadcasted_iota(jnp.int32, (nrow, page), 1)
    for e in els:
        s_new = _nt_dot(q_bd[e], _pad_rows(knew_ref[rows[e], 0:FOX_W], page).astype(bf16)) + negc[e][npages]
        parts[e].append(jnp.where(col <= tok, s_new, NEG_INF))
    soft = [_joint_softmax(parts[e]) for e in els]
    accs = [jnp.dot(soft[e][0][npages].astype(bf16), _pad_rows(knew_ref[rows[e], FOX_W:2 * FOX_W], page).astype(bf16),
                    preferred_element_type=f32) for e in els]
    for p in range(npages):
        for e in els:
            accs[e] = accs[e] + _nt_dot(soft[e][0][p].astype(bf16), kvt[e][p][0, 1].reshape(FOX_W, page).astype(bf16))
    for e in els:
        acc = accs[e] / soft[e][1]
        o_ref[rows[e], :] = sum(jnp.where(head_of_lane == hd, acc[hd * ns:(hd + 1) * ns], 0.0) for hd in range(nh))


def _fox_decode(page_table, cache_kvt, lft, qb, fox_new, lft_new):
    db, npages = page_table.shape
    ns = qb.shape[0] // db
    nel = DECODE_PER_STEP
    tok = lambda w: pl.BlockSpec((nel * ns, w), lambda b, pt: (b, 0))
    pg = lambda a: [pl.BlockSpec((1,) + a.shape[1:], functools.partial(
        lambda e, p, nd, b, pt: (pt[b * nel + e, p],) + (0,) * nd, e, p, a.ndim - 1))
        for e in range(nel) for p in range(npages)]
    return pl.pallas_call(
        functools.partial(_fox_decode_kernel, npages, ns, nel),
        grid_spec=pltpu.PrefetchScalarGridSpec(
            num_scalar_prefetch=1, grid=(db // nel,),
            in_specs=pg(cache_kvt) + pg(lft) + [tok(FOX_W), tok(2 * FOX_W),
                                                pl.BlockSpec((nel,) + lft_new.shape[1:], lambda b, pt: (b, 0, 0))],
            out_specs=tok(FOX_W)),
        out_shape=jax.ShapeDtypeStruct((db * ns, FOX_W), jnp.float32),
        compiler_params=pltpu.CompilerParams(dimension_semantics=("arbitrary",),
                                             vmem_limit_bytes=VMEM_LIMIT),
        name="fox_decode",
    )(page_table, *([cache_kvt] * (nel * npages)), *([lft] * (nel * npages)), qb, fox_new, lft_new)


BIG_ID = 1 << 20


def _topk_rows(s, k, ids):
    w = ids.shape[1]
    if s.shape[1] > w:
        parts = [_topk_rows(s[:, c:c + w], k, ids) for c in range(0, s.shape[1], w)]
        return jnp.concatenate([p[0] for p in parts], axis=1), jnp.concatenate([p[1] for p in parts], axis=1)
    ids = ids.astype(jnp.float32)
    vals, idxs = [], []
    for _ in range(k):
        m = jnp.max(s, axis=0, keepdims=True)
        idx = jnp.min(jnp.where(s == m, ids, float(BIG_ID)), axis=0, keepdims=True)
        vals.append(m)
        idxs.append(idx)
        s = jnp.where(ids == idx, -jnp.inf, s)
    return jnp.concatenate(vals, axis=0), jnp.concatenate(idxs, axis=0).astype(jnp.int32)


def _pick_rows(sel, table):
    out = jnp.zeros(sel.shape, table.dtype)
    for r in range(table.shape[0]):
        out = jnp.where(sel == r, table[r:r + 1, :], out)
    return out


def _merge_route_kernel(x_ref, on_ref, of_ref, mg_ref, wun_ref, wuf_ref, wo_ref, nf_ref, wqt_ref, sk_ref,
                        x1_ref, h2_ref, i1_ref, i2_ref, g_ref):
    tm = x_ref.shape[0]
    f32, bf16 = jnp.float32, jnp.bfloat16
    tdot = lambda ot, wgt: lax.dot_general(ot.astype(bf16), wgt, (((0,), (0,)), ((), ())),
                                           preferred_element_type=f32)
    a = tdot(on_ref[...], wun_ref[...])
    b = tdot(of_ref[...], wuf_ref[...])
    mixed = mg_ref[:, 0:D_MODEL] * a + mg_ref[:, D_MODEL:2 * D_MODEL] * b
    x1 = x_ref[...] + jnp.dot(mixed.astype(bf16), wo_ref[...], preferred_element_type=f32)
    x1_ref[...] = x1
    h2 = x1 * lax.rsqrt(jnp.mean(x1 * x1, axis=-1, keepdims=True) + EPS) * nf_ref[...]
    h2b = h2.astype(bf16)
    h2_ref[...] = h2b

    nk = PEER_N_KEYS
    key_ids = lax.broadcasted_iota(jnp.int32, (nk, LANE), 0)
    io16 = lax.broadcasted_iota(jnp.int32, (PEER_TOPK, LANE), 0)
    io8 = lax.broadcasted_iota(jnp.int32, (8, LANE), 0)
    cand_ids = jnp.concatenate([io16] + [a_ * PEER_TOPK + io8 for a_ in range(1, 8)]
                               + [(io8 + 8) * PEER_TOPK], axis=0)
    for h in range(PEER_HEADS):
        sv, si = [], []
        for p in range(2):
            hp = 2 * h + p
            qt = lax.dot_general(wqt_ref[hp * PEER_DK_HALF:(hp + 1) * PEER_DK_HALF, :], h2b,
                                 (((1,), (1,)), ((), ())), preferred_element_type=f32)
            st = jnp.dot(sk_ref[hp], qt.astype(bf16), preferred_element_type=f32)
            v, i = _topk_rows(st, PEER_TOPK, key_ids)
            sv.append(v)
            si.append(i)
        s1, s2 = sv
        cand = jnp.concatenate([s1[0:1] + s2] + [s1[a_:a_ + 1] + s2[0:8] for a_ in range(1, 8)]
                               + [s1[8:16] + s2[0:1]], axis=0)
        top, fid = _topk_rows(cand, PEER_TOPK, cand_ids)
        e = jnp.exp(top - jnp.max(top, axis=0, keepdims=True))
        g = e / jnp.sum(e, axis=0, keepdims=True)
        sl = slice(h * PEER_TOPK, (h + 1) * PEER_TOPK)
        i1_ref[:, sl] = _pick_rows(fid >> 4, si[0]).T
        i2_ref[:, sl] = _pick_rows(fid & (PEER_TOPK - 1), si[1]).T
        g_ref[:, sl] = g.T


def _merge_route(x2d, o_nsa, o_fox, mg, w_up_nsa, w_up_fox, w_out, norm_ffn, wq_t, sub_keys, tm=256):
    n = x2d.shape[0]
    row = lambda w: pl.BlockSpec((tm, w), lambda i: (i, 0))
    col = lambda h: pl.BlockSpec((h, tm), lambda i: (0, i))
    full = lambda a: pl.BlockSpec(a.shape, lambda i: (0,) * a.ndim)
    args = (x2d, o_nsa, o_fox, mg, w_up_nsa, w_up_fox, w_out, norm_ffn.reshape(1, D_MODEL), wq_t, sub_keys)
    hk = PEER_HEADS * PEER_TOPK
    return pl.pallas_call(
        _merge_route_kernel,
        grid=(n // tm,),
        in_specs=[row(D_MODEL), col(NSA_Q_W), col(FOX_W), row(MERGE_W)] + [full(a) for a in args[4:]],
        out_specs=[row(D_MODEL), row(D_MODEL), row(hk), row(hk), row(hk)],
        out_shape=[jax.ShapeDtypeStruct((n, D_MODEL), jnp.float32),
                   jax.ShapeDtypeStruct((n, D_MODEL), jnp.bfloat16),
                   jax.ShapeDtypeStruct((n, hk), jnp.int32),
                   jax.ShapeDtypeStruct((n, hk), jnp.int32),
                   jax.ShapeDtypeStruct((n, hk), jnp.float32)],
        compiler_params=pltpu.CompilerParams(dimension_semantics=("arbitrary",),
                                             vmem_limit_bytes=VMEM_LIMIT),
        name="merge_route",
    )(*args)


def _peer_act_kernel(h2_ref, ut_ref, i1_ref, i2_ref, act_ref):
    c = pl.program_id(1)
    ec = ut_ref.shape[1]

    @pl.when(c == 0)
    def _():
        act_ref[...] = jnp.zeros_like(act_ref)

    h2 = h2_ref[...]
    i1 = i1_ref[...]
    i2 = i2_ref[...]
    act = act_ref[...]
    nk = PEER_N_KEYS
    for blk in range(ec // ACT_BLOCK):
        a = jnp.dot(h2, ut_ref[:, blk * ACT_BLOCK:(blk + 1) * ACT_BLOCK],
                    preferred_element_type=jnp.float32)
        for ii in range(ACT_BLOCK // nk):
            got = jnp.take_along_axis(a[:, ii * nk:(ii + 1) * nk], i2, axis=1)
            act = jnp.where(i1 == c * (ec // nk) + blk * (ACT_BLOCK // nk) + ii, got, act)
    act_ref[...] = act


ACT_BLOCK = 512


def _peer_act(h2b, u_t, i1, i2, tm=1024, ec=2048):
    n = h2b.shape[0]
    hk = i1.shape[1]
    return pl.pallas_call(
        _peer_act_kernel,
        grid=(n // tm, u_t.shape[1] // ec),
        in_specs=[pl.BlockSpec((tm, D_MODEL), lambda t, c: (t, 0)),
                  pl.BlockSpec((D_MODEL, ec), lambda t, c: (0, c)),
                  pl.BlockSpec((tm, hk), lambda t, c: (t, 0)),
                  pl.BlockSpec((tm, hk), lambda t, c: (t, 0))],
        out_specs=pl.BlockSpec((tm, hk), lambda t, c: (t, 0)),
        out_shape=jax.ShapeDtypeStruct((n, hk), jnp.float32),
        compiler_params=pltpu.CompilerParams(dimension_semantics=("arbitrary", "arbitrary"),
                                             vmem_limit_bytes=VMEM_LIMIT),
        name="peer_act",
    )(h2b, u_t, i1, i2)


def _peer_coef_kernel(act_ref, g_ref, i1_ref, i2_ref, c_ref, coef_ref):
    tm = act_ref.shape[0]
    nk = PEER_N_KEYS
    coef_ref[...] = g_ref[...] * jax.nn.gelu(act_ref[...])
    sub = lax.broadcasted_iota(jnp.int32, (nk, i1_ref.shape[1]), 0)

    def token(t):
        r1 = i1_ref[pl.ds(t, 1), :]
        r2 = i2_ref[pl.ds(t, 1), :]
        cf = coef_ref[pl.ds(t, 1), :]
        m1 = jnp.where(r1 == sub, cf, 0.0).astype(jnp.bfloat16)
        m2t = jnp.where(r2 == sub, 1.0, 0.0).astype(jnp.bfloat16)
        return lax.dot_general(m1, m2t, (((1,), (1,)), ((), ())), preferred_element_type=jnp.float32)

    def body(tg, carry):
        t0 = pl.multiple_of(tg * COEF_GROUP, COEF_GROUP)
        ct = jnp.stack([token(t0 + u) for u in range(COEF_GROUP)], axis=0)
        c_ref[:, pl.ds(t0, COEF_GROUP), :] = pltpu.einshape("tij->itj", ct).astype(c_ref.dtype)
        return carry

    lax.fori_loop(0, tm // COEF_GROUP, body, 0)


COEF_GROUP = 64


def _peer_coef(act, g, i1, i2, tm=128):
    n, hk = act.shape
    nk = PEER_N_KEYS
    row = pl.BlockSpec((tm, hk), lambda t: (t, 0))
    return pl.pallas_call(
        _peer_coef_kernel,
        grid=(n // tm,),
        in_specs=[row, row, row, row],
        out_specs=pl.BlockSpec((nk, tm, nk), lambda t: (0, t, 0)),
        out_shape=jax.ShapeDtypeStruct((nk, n, nk), jnp.bfloat16),
        scratch_shapes=[pltpu.VMEM((tm, hk), jnp.float32)],
        compiler_params=pltpu.CompilerParams(dimension_semantics=("arbitrary",),
                                             vmem_limit_bytes=VMEM_LIMIT),
        name="peer_coef",
    )(act, g, i1, i2)


def _peer_out_kernel(c_ref, v_ref, x1_ref, y_ref, acc_ref):
    k = pl.program_id(1)

    @pl.when(k == 0)
    def _():
        acc_ref[...] = x1_ref[...]

    acc = acc_ref[...]
    nk = PEER_N_KEYS
    for p in range(c_ref.shape[0] // 2):
        lhs = jnp.concatenate([c_ref[2 * p], c_ref[2 * p + 1]], axis=1)
        acc = acc + jnp.dot(lhs, v_ref[2 * p * nk:(2 * p + 2) * nk, :], preferred_element_type=jnp.float32)
    acc_ref[...] = acc

    @pl.when(k == pl.num_programs(1) - 1)
    def _():
        y_ref[...] = acc_ref[...]


def _peer_out(c3, v_b, x1, tm=1024, tk=2048):
    nk, n, _ = c3.shape
    ne = nk * nk
    return pl.pallas_call(
        _peer_out_kernel,
        grid=(n // tm, ne // tk),
        in_specs=[pl.BlockSpec((tk // nk, tm, nk), lambda t, k: (k, t, 0)),
                  pl.BlockSpec((tk, D_MODEL), lambda t, k: (k, 0)),
                  pl.BlockSpec((tm, D_MODEL), lambda t, k: (t, 0))],
        out_specs=pl.BlockSpec((tm, D_MODEL), lambda t, k: (t, 0)),
        out_shape=jax.ShapeDtypeStruct((n, D_MODEL), jnp.float32),
        scratch_shapes=[pltpu.VMEM((tm, D_MODEL), jnp.float32)],
        compiler_params=pltpu.CompilerParams(dimension_semantics=("arbitrary", "arbitrary"),
                                             vmem_limit_bytes=VMEM_LIMIT),
        name="peer_out",
    )(c3, v_b, x1)


def _peer_weights(w_up_nsa, w_up_fox, w_out, norm_ffn, peer_w_query, peer_sub_keys, peer_u, peer_v):
    bf16 = jnp.bfloat16
    return dict(w_up_nsa=w_up_nsa.astype(bf16), w_up_fox=w_up_fox.astype(bf16), w_out=w_out.astype(bf16),
                norm_ffn=norm_ffn, wq_t=peer_w_query.T.astype(bf16),
                sub_keys=peer_sub_keys.reshape(2 * PEER_HEADS, PEER_N_KEYS, PEER_DK_HALF).astype(bf16),
                u=peer_u.T.astype(bf16), v=peer_v.astype(bf16))


def _merge_peer(x2d, o_nsa, o_fox, mg, wts):
    x1, h2b, i1, i2, g = _merge_route(x2d, o_nsa, o_fox, mg, wts['w_up_nsa'], wts['w_up_fox'], wts['w_out'],
                                      wts['norm_ffn'], wts['wq_t'], wts['sub_keys'])
    act = _peer_act(h2b, wts['u'], i1, i2)
    c3 = _peer_coef(act, g, i1, i2)
    return _peer_out(c3, wts['v'], x1)


def kernel(x_prompt, x_sample, cache_nsa, cache_fox_kv, cache_fox_logf, state_nsa_win, page_table,
           norm_attn, w_in, fox_f_bias, nsa_q_norm, nsa_k_norm, fox_q_norm, fox_k_norm,
           cmp_pe, cmp_w1, cmp_w2, w_up_nsa, w_up_fox, w_out, norm_ffn,
           peer_w_query, peer_sub_keys, peer_u, peer_v):
    w_front = _front_weights(w_in)
    bd = _block_diag_mean()
    wts = _peer_weights(w_up_nsa, w_up_fox, w_out, norm_ffn, peer_w_query, peer_sub_keys, peer_u, peer_v)
    cmp_wts = _compress_weights(cmp_pe, cmp_w1, cmp_w2)

    bp, seq, _ = x_prompt.shape
    n_p = bp * seq
    (rows_t, win_t, fox_t, logf_p2d, mg_p, qat, gat, ksel, kwin, vselt, vwint, qbt, kb, vbt, rows_cmp) = _front_attn(
        x_prompt.reshape(n_p, D_MODEL), seq, norm_attn, w_front, bd, fox_f_bias,
        nsa_q_norm, nsa_k_norm, fox_q_norm, fox_k_norm)
    kc, vct = _compress(rows_cmp, seq // CMP_BLOCK, *cmp_wts, nsa_k_norm[0], bd)
    o_nsa_t = _nsa_prompt(qat, gat, ksel, vselt, kwin, vwint, kc, vct, bp, seq)
    o_fox_t = _fox_prompt(qbt, kb, vbt, bp, seq)
    y_p = _merge_peer(x_prompt.reshape(n_p, D_MODEL), o_nsa_t, o_fox_t, mg_p, wts).reshape(x_prompt.shape)
    to_rows = lambda a, *dims: a.reshape(bp, *dims, a.shape[-1]).transpose(0, len(dims) + 1, *range(1, len(dims) + 1))
    nsa_p = to_rows(rows_t, 4, NSA_KV_HEADS, HEAD_DIM)
    fox_p = to_rows(fox_t, 2, FOX_HEADS, HEAD_DIM)
    logf_p = logf_p2d.reshape(bp, seq, FOX_HEADS)
    win_p = to_rows(win_t[:, :, seq - min(WINDOW, seq):], 2, NSA_KV_HEADS, HEAD_DIM)

    db, ns, _ = x_sample.shape
    n_s = db * ns
    n_pool, page = cache_nsa.shape[:2]
    wbuf = state_nsa_win.shape[1]
    qa_s, rows_s, win_s2d, ga_s, qb_s, fox_s2d, logf_s2d, mg_s = _front(
        x_sample.reshape(n_s, D_MODEL), norm_attn, w_front, bd, fox_f_bias,
        nsa_q_norm, nsa_k_norm, fox_q_norm, fox_k_norm)
    o_nsa_s = _nsa_sample(page_table, cache_nsa.transpose(0, 2, 3, 4, 1), qa_s, ga_s, rows_s, win_s2d,
                          state_nsa_win.transpose(0, 2, 3, 4, 1),
                          *_compress_weights_t(cmp_pe, cmp_w1, cmp_w2), nsa_k_norm[0], bd)
    lft_new = jnp.pad(logf_s2d.reshape(db, ns, FOX_HEADS).transpose(0, 2, 1), ((0, 0), (0, 0), (0, page - ns)))
    o_fox_s = _fox_decode(page_table, cache_fox_kv.transpose(0, 2, 3, 4, 1), cache_fox_logf.transpose(0, 2, 1),
                          qb_s, fox_s2d, lft_new)
    y_s = _merge_peer(x_sample.reshape(n_s, D_MODEL), o_nsa_s.T, o_fox_s.T, mg_s, wts).reshape(x_sample.shape)
    nsa_s = rows_s.reshape(db, ns, 4, NSA_KV_HEADS, HEAD_DIM)
    fox_s = fox_s2d.reshape(db, ns, 2, FOX_HEADS, HEAD_DIM)
    logf_s = logf_s2d.reshape(db, ns, FOX_HEADS)
    win_s = jnp.concatenate([state_nsa_win[:, ns:], win_s2d.reshape(db, ns, 2, NSA_KV_HEADS, HEAD_DIM)], axis=1)
    return (y_p, y_s, nsa_p, fox_p, logf_p, win_p, nsa_s, fox_s, logf_s, win_s)
```

```python
import functools

import jax
import jax.numpy as jnp
from jax import lax
from jax.experimental import pallas as pl
from jax.experimental.pallas import tpu as pltpu

D_MODEL = 1024
HEAD_DIM = 64
NSA_HEADS = 8
NSA_KV_HEADS = 2
NSA_GROUP = NSA_HEADS // NSA_KV_HEADS
CMP_BLOCK = 64
SEL_BLOCK = CMP_BLOCK
SEL_TOPK = 16
WINDOW = 512
FOX_HEADS = 8
Q_BLOCK = 128
PEER_HEADS = 8
PEER_N_KEYS = 128
PEER_DK = 256
PEER_DK_HALF = PEER_DK // 2
PEER_TOPK = 16
PEER_CHUNK = 256

NSA_Q_W = NSA_HEADS * HEAD_DIM
NSA_KV_W = 6 * NSA_KV_HEADS * HEAD_DIM
NSA_GATE_W = 3 * NSA_HEADS
FOX_W = FOX_HEADS * HEAD_DIM
FOX_QKV_W = 3 * FOX_W
FOX_F_W = FOX_HEADS
MERGE_W = 2 * D_MODEL
SPLIT_Q_A = NSA_Q_W
SPLIT_KV_A = SPLIT_Q_A + NSA_KV_W
SPLIT_G_A = SPLIT_KV_A + NSA_GATE_W
SPLIT_QKV_B = SPLIT_G_A + FOX_QKV_W
SPLIT_F_B = SPLIT_QKV_B + FOX_F_W
IN_WIDTH = SPLIT_F_B + MERGE_W

SCALE = HEAD_DIM ** -0.5
FORCE_SCORE = float(NSA_GROUP + 1)
NEG_INF = -1e30
EPS = 1e-6

LANE = 128
VMEM_LIMIT = 48 * 1024 * 1024


def _group_mean_sq(x, bd):
    sq = x * x
    hi = sq.astype(jnp.bfloat16)
    lo = (sq - hi.astype(jnp.float32)).astype(jnp.bfloat16)
    return (jnp.dot(hi, bd, preferred_element_type=jnp.float32)
            + jnp.dot(lo, bd, preferred_element_type=jnp.float32))


def _head_rms(x, g, bd):
    outs = []
    for c in range(x.shape[1] // LANE):
        xc = x[:, c * LANE:(c + 1) * LANE]
        outs.append(xc * lax.rsqrt(_group_mean_sq(xc, bd) + EPS) * g)
    return outs[0] if len(outs) == 1 else jnp.concatenate(outs, axis=1)


def _front_kernel(x_ref, na_ref, w_ref, bd_ref, fb_ref, gq_a_ref, gk_sel_ref, gk_win_ref,
                  gq_b_ref, gk_b_ref,
                  qa_ref, rows_ref, win_ref, ga_ref, qb_ref, fox_ref, logf_ref, mg_ref):
    x = x_ref[...]
    h = x * lax.rsqrt(jnp.mean(x * x, axis=-1, keepdims=True) + EPS) * na_ref[...]
    hb = h.astype(jnp.bfloat16)
    bd = bd_ref[...]

    def proj(c0, width):
        return jnp.dot(hb, w_ref[:, c0:c0 + width], preferred_element_type=jnp.float32)

    c = 0
    qa_ref[...] = _head_rms(proj(c, NSA_Q_W), gq_a_ref[...], bd)
    c += NSA_Q_W
    rows_ref[:, 0:256] = proj(c, 256)
    rows_ref[:, 256:384] = _head_rms(proj(c + 256, 128), gk_sel_ref[...], bd)
    rows_ref[:, 384:512] = proj(c + 384, 128)
    win_ref[:, 0:128] = _head_rms(proj(c + 512, 128), gk_win_ref[...], bd)
    win_ref[:, 128:256] = proj(c + 640, 128)
    c += NSA_KV_W
    qb_ref[...] = _head_rms(proj(c, FOX_W), gq_b_ref[...], bd)
    fox_ref[:, 0:FOX_W] = _head_rms(proj(c + FOX_W, FOX_W), gk_b_ref[...], bd)
    fox_ref[:, FOX_W:2 * FOX_W] = proj(c + 2 * FOX_W, FOX_W)
    c += FOX_QKV_W
    for j in range(MERGE_W // 512):
        mg_ref[:, j * 512:(j + 1) * 512] = jax.nn.sigmoid(proj(c + j * 512, 512))
    c += MERGE_W
    ga_ref[...] = jax.nn.sigmoid(proj(c, LANE))
    f = proj(c + LANE, LANE)[:, 0:FOX_F_W] + fb_ref[...]
    logf_ref[...] = jnp.minimum(f, 0.0) - jnp.log1p(jnp.exp(-jnp.abs(f)))


def _front(x2d, norm_attn, w_front, bd, fox_f_bias, nsa_q_norm, nsa_k_norm, fox_q_norm, fox_k_norm,
           tm=256):
    n = x2d.shape[0]
    wf = w_front.shape[1]
    two = lambda g: jnp.concatenate([g, g]).reshape(1, LANE)
    row = lambda w: pl.BlockSpec((tm, w), lambda i: (i, 0))
    full = lambda a: pl.BlockSpec(a.shape, lambda i: (0,) * a.ndim)
    args = (x2d, norm_attn.reshape(1, D_MODEL), w_front, bd, fox_f_bias.reshape(1, FOX_F_W),
            two(nsa_q_norm), two(nsa_k_norm[1]), two(nsa_k_norm[2]), two(fox_q_norm), two(fox_k_norm))
    widths = (NSA_Q_W, 512, 256, LANE, FOX_W, 2 * FOX_W, FOX_F_W, MERGE_W)
    return pl.pallas_call(
        _front_kernel,
        grid=(n // tm,),
        in_specs=[row(D_MODEL)] + [full(a) for a in args[1:]],
        out_specs=[row(w) for w in widths],
        out_shape=[jax.ShapeDtypeStruct((n, w), jnp.float32) for w in widths],
        compiler_params=pltpu.CompilerParams(dimension_semantics=("arbitrary",),
                                             vmem_limit_bytes=VMEM_LIMIT),
        name="front",
    )(*args)


FEAT = HEAD_DIM


def _lane(tm):
    return lax.broadcasted_iota(jnp.int32, (tm, LANE), 1)


def _expand_halves(x):
    lo = _lane(x.shape[0]) < HEAD_DIM
    return jnp.where(lo, x, 0.0), jnp.where(lo, pltpu.roll(x, HEAD_DIM, axis=1), 0.0)


def _split3(x):
    hi = x.astype(jnp.bfloat16)
    r = x - hi.astype(jnp.float32)
    mid = r.astype(jnp.bfloat16)
    lo = (r - mid.astype(jnp.float32)).astype(jnp.bfloat16)
    return hi, mid, lo


def _front_attn_kernel(seq_len, x_ref, na_ref, w_ref, bd_ref, fb_ref, gq_a_ref, gk_sel_ref, gk_win_ref,
                       gq_b_ref, gk_b_ref, place_ref,
                       rows_ref, win_ref, fox_ref, logf_ref, mg_ref,
                       qat_ref, gat_ref, ksel_ref, kwin_ref, vselt_ref, vwint_ref, qbt_ref, kb_ref, vbt_ref, rcmp_ref,
                       carry_ref):
    f32, bf16 = jnp.float32, jnp.bfloat16
    tm = x_ref.shape[0]
    i = pl.program_id(0)
    x = x_ref[...]
    h = x * lax.rsqrt(jnp.mean(x * x, axis=-1, keepdims=True) + EPS) * na_ref[...]
    hb = h.astype(bf16)
    bd = bd_ref[...]
    lane = _lane(tm)
    pos = (i * tm + lax.broadcasted_iota(jnp.int32, (tm, LANE), 0)) % seq_len
    kfeat = jnp.where(lane == FEAT, (pos // SEL_BLOCK).astype(f32),
                      jnp.where(lane == FEAT + 1, (pos % SEL_BLOCK).astype(f32), 0.0))

    def proj(c0, width):
        return jnp.dot(hb, w_ref[:, c0:c0 + width], preferred_element_type=f32)

    c = 0
    qa = _head_rms(proj(c, NSA_Q_W), gq_a_ref[...], bd)
    for j in range(NSA_HEADS // 2):
        for s, half in enumerate(_expand_halves(qa[:, j * LANE:(j + 1) * LANE])):
            hd = 2 * j + s
            slope = 2.0 ** -(hd + 1)
            qfeat = jnp.where(lane == FEAT, slope * SEL_BLOCK, jnp.where(lane == FEAT + 1, slope, 0.0))
            qat_ref[hd * LANE:(hd + 1) * LANE, :] = (half * SCALE + qfeat).T.astype(bf16)
    c += NSA_Q_W
    for j in range(2):
        raw = proj(c + j * LANE, LANE)
        rcmp_ref[:, j * LANE:(j + 1) * LANE] = raw
        rows_ref[0, j * LANE:(j + 1) * LANE, :] = raw.T
    ksel = _head_rms(proj(c + 256, 128), gk_sel_ref[...], bd)
    rows_ref[0, 2 * LANE:3 * LANE, :] = ksel.T
    vsel_t = proj(c + 384, 128).T
    rows_ref[0, 3 * LANE:4 * LANE, :] = vsel_t
    kwin = _head_rms(proj(c + 512, 128), gk_win_ref[...], bd)
    win_ref[0, 0:LANE, :] = kwin.T
    vwin_t = proj(c + 640, 128).T
    win_ref[0, LANE:2 * LANE, :] = vwin_t
    for g, (ks_g, kw_g) in enumerate(zip(_expand_halves(ksel), _expand_halves(kwin))):
        ksel_ref[:, g * LANE:(g + 1) * LANE] = (ks_g + kfeat).astype(bf16)
        kwin_ref[:, g * LANE:(g + 1) * LANE] = (kw_g + kfeat).astype(bf16)
    vselt_ref[...] = vsel_t.astype(bf16)
    vwint_ref[...] = vwin_t.astype(bf16)
    c += NSA_KV_W

    qb = _head_rms(proj(c, FOX_W), gq_b_ref[...], bd)
    ones3 = jnp.where((lane >= FEAT) & (lane < FEAT + 3), 1.0, 0.0)
    for j in range(FOX_HEADS // 2):
        for s, half in enumerate(_expand_halves(qb[:, j * LANE:(j + 1) * LANE])):
            hd = 2 * j + s
            qbt_ref[hd * LANE:(hd + 1) * LANE, :] = (half * SCALE + ones3).T.astype(bf16)
    kbn = _head_rms(proj(c + FOX_W, FOX_W), gk_b_ref[...], bd)
    for j in range(FOX_HEADS // 2):
        fox_ref[0, j * LANE:(j + 1) * LANE, :] = kbn[:, j * LANE:(j + 1) * LANE].T
        vb_t = proj(c + 2 * FOX_W + j * LANE, LANE).T
        fox_ref[0, FOX_W + j * LANE:FOX_W + (j + 1) * LANE, :] = vb_t
        vbt_ref[j * LANE:(j + 1) * LANE, :] = vb_t.astype(bf16)
    c += FOX_QKV_W
    for j in range(MERGE_W // 512):
        mg_ref[:, j * 512:(j + 1) * 512] = jax.nn.sigmoid(proj(c + j * 512, 512))
    c += MERGE_W
    gat_ref[...] = jax.nn.sigmoid(proj(c, LANE)).T
    f = proj(c + LANE, LANE) + fb_ref[...]
    lf = jnp.minimum(f, 0.0) - jnp.log1p(jnp.exp(-jnp.abs(f)))
    logf_ref[...] = lf[:, 0:FOX_F_W]

    @pl.when((i * tm) % seq_len == 0)
    def _():
        carry_ref[...] = jnp.zeros_like(carry_ref)

    r_io = lax.broadcasted_iota(jnp.int32, (tm, tm), 0)
    c_io = lax.broadcasted_iota(jnp.int32, (tm, tm), 1)
    tri = jnp.where(c_io <= r_io, 1.0, 0.0).astype(bf16)
    csum = carry_ref[...] + sum(jnp.dot(tri, p, preferred_element_type=f32) for p in _split3(lf))
    carry_ref[...] = csum[tm - 1:tm, :]
    pieces = jnp.concatenate(_split3(-csum), axis=1)
    cfeat = jnp.dot(pieces, place_ref[...], preferred_element_type=f32)
    for j in range(FOX_HEADS // 2):
        for s, half in enumerate(_expand_halves(kbn[:, j * LANE:(j + 1) * LANE])):
            hd = 2 * j + s
            kb_ref[:, hd * LANE:(hd + 1) * LANE] = (half + cfeat[:, hd * LANE:(hd + 1) * LANE]).astype(bf16)


def _fox_feature_placement():
    r = lax.broadcasted_iota(jnp.int32, (3 * LANE, FOX_HEADS * LANE), 0)
    c = lax.broadcasted_iota(jnp.int32, (3 * LANE, FOX_HEADS * LANE), 1)
    s, hd = r // LANE, r % LANE
    return jnp.where((hd < FOX_HEADS) & (c == hd * LANE + FEAT + s), 1.0, 0.0).astype(jnp.bfloat16)


def _front_attn(x2d, seq_len, norm_attn, w_front, bd, fox_f_bias, nsa_q_norm, nsa_k_norm, fox_q_norm,
                fox_k_norm, tm=256):
    n = x2d.shape[0]
    f32, bf16 = jnp.float32, jnp.bfloat16
    two = lambda g: jnp.concatenate([g, g]).reshape(1, LANE)
    row = lambda w: pl.BlockSpec((tm, w), lambda i: (i, 0))
    col = lambda h: pl.BlockSpec((h, tm), lambda i: (0, i))
    full = lambda a: pl.BlockSpec(a.shape, lambda i: (0,) * a.ndim)
    fb = jnp.pad(fox_f_bias, (0, LANE - FOX_F_W)).reshape(1, LANE)
    args = (x2d, norm_attn.reshape(1, D_MODEL), w_front, bd, fb,
            two(nsa_q_norm), two(nsa_k_norm[1]), two(nsa_k_norm[2]), two(fox_q_norm), two(fox_k_norm),
            _fox_feature_placement())
    per_seq = seq_len // tm
    nseq = n // seq_len
    leaf = lambda h: (pl.BlockSpec((1, h, tm), lambda i: (i // per_seq, 0, i % per_seq)), (nseq, h, seq_len), f32)
    outs = [leaf(512), leaf(256), leaf(2 * FOX_W),
            (row(FOX_F_W), (n, FOX_F_W), f32), (row(MERGE_W), (n, MERGE_W), f32),
            (col(NSA_HEADS * LANE), (NSA_HEADS * LANE, n), bf16), (col(LANE), (LANE, n), f32),
            (row(2 * LANE), (n, 2 * LANE), bf16), (row(2 * LANE), (n, 2 * LANE), bf16),
            (col(LANE), (LANE, n), bf16), (col(LANE), (LANE, n), bf16),
            (col(FOX_HEADS * LANE), (FOX_HEADS * LANE, n), bf16),
            (row(FOX_HEADS * LANE), (n, FOX_HEADS * LANE), bf16), (col(FOX_W), (FOX_W, n), bf16),
            (row(2 * LANE), (n, 2 * LANE), f32)]
    return pl.pallas_call(
        functools.partial(_front_attn_kernel, seq_len),
        grid=(n // tm,),
        in_specs=[row(D_MODEL)] + [full(a) for a in args[1:]],
        out_specs=[o[0] for o in outs],
        out_shape=[jax.ShapeDtypeStruct(o[1], o[2]) for o in outs],
        scratch_shapes=[pltpu.VMEM((1, LANE), f32)],
        compiler_params=pltpu.CompilerParams(dimension_semantics=("arbitrary",),
                                             vmem_limit_bytes=VMEM_LIMIT),
        name="front_attn",
    )(*args)


def _front_weights(w_in):
    pad = lambda w: jnp.pad(w, ((0, 0), (0, LANE - w.shape[1])))
    parts = [w_in[:, :SPLIT_KV_A], w_in[:, SPLIT_G_A:SPLIT_QKV_B], w_in[:, SPLIT_F_B:],
             pad(w_in[:, SPLIT_KV_A:SPLIT_G_A]), pad(w_in[:, SPLIT_QKV_B:SPLIT_F_B])]
    return jnp.concatenate(parts, axis=1).astype(jnp.bfloat16)


def _block_diag_mean():
    r = lax.broadcasted_iota(jnp.int32, (LANE, LANE), 0) // HEAD_DIM
    c = lax.broadcasted_iota(jnp.int32, (LANE, LANE), 1) // HEAD_DIM
    return jnp.where(r == c, 1.0 / HEAD_DIM, 0.0).astype(jnp.bfloat16)


def _softmax_step(tiles, carry):
    m, l, acc = carry
    m_new = functools.reduce(jnp.maximum, [jnp.max(s, axis=0, keepdims=True) for s, _ in tiles], m)
    alpha = jnp.exp(m - m_new)
    ps = [jnp.exp(s - m_new) for s, _ in tiles]
    l = alpha * l + sum(jnp.sum(p, axis=0, keepdims=True) for p in ps)
    acc = alpha * acc + sum(jnp.dot(vt, p.astype(jnp.bfloat16), preferred_element_type=jnp.float32)
                            for p, (_, vt) in zip(ps, tiles))
    return m_new, l, acc


TILE_UNROLL = 4


def _tile_loop(n, logits, values, carry, bufs, tk, last_group, groups=None, unroll=TILE_UNROLL):
    tile = lambda kt: (logits(kt), values(kt))
    nu = n // unroll
    count, group_of = (nu, lambda i: i) if groups is None else groups

    def fill(buf, i):
        grp = jnp.minimum(group_of(jnp.minimum(i, jnp.maximum(count - 1, 0))), last_group)
        for j in range(unroll):
            buf[j * tk:(j + 1) * tk, :] = logits(grp * unroll + j)

    def drain(buf, i, c):
        grp = group_of(i)
        return _softmax_step([(buf[j * tk:(j + 1) * tk, :], values(grp * unroll + j)) for j in range(unroll)], c)

    def body(i, c):
        fill(bufs[1], 2 * i + 1)
        c = drain(bufs[0], 2 * i, c)
        fill(bufs[0], 2 * i + 2)
        return drain(bufs[1], 2 * i + 1, c)

    fill(bufs[0], 0)
    carry = lax.fori_loop(0, count // 2, body, carry)
    carry = lax.cond(count % 2 == 1, lambda c: drain(bufs[0], count - 1, c), lambda c: c, carry)
    base = nu * unroll
    size = unroll // 2
    while size >= 1:
        has = (n & size) != 0
        carry = lax.cond(has, functools.partial(
            lambda b, sz, c: _softmax_step([tile(b + u) for u in range(sz)], c), base, size),
            lambda c: c, carry)
        base = base + jnp.where(has, size, 0)
        size //= 2
    return carry


def _softmax_init(w):
    return (jnp.full((1, w), NEG_INF, jnp.float32), jnp.zeros((1, w), jnp.float32),
            jnp.zeros((HEAD_DIM, w), jnp.float32))


def _summarize(xk_ref, xv_ref, pe_ref, w1_ref, w2_ref, nb):
    f32, bf16 = jnp.float32, jnp.bfloat16
    x_refs = (xk_ref, xv_ref)

    def body(l, accs):
        return tuple(
            acc + jnp.dot((x_refs[s][pl.ds(l, nb, stride=CMP_BLOCK), :] + pe_ref[s, pl.ds(l, 1), :]).astype(bf16),
                          w1_ref[s, l], preferred_element_type=f32)
            for s, acc in enumerate(accs))

    accs = lax.fori_loop(0, CMP_BLOCK, body, (jnp.zeros((nb, LANE), f32),) * 2, unroll=16)
    return tuple(jnp.dot(jax.nn.gelu(acc).astype(bf16), w2_ref[s], preferred_element_type=f32)
                 for s, acc in enumerate(accs))


def _compress_kernel(xk_ref, xv_ref, pe_ref, w1_ref, w2_ref, gk_ref, bd_ref, kc_ref, vct_ref):
    f32, bf16 = jnp.float32, jnp.bfloat16
    nb = kc_ref.shape[0]
    kc, vc = _summarize(xk_ref, xv_ref, pe_ref, w1_ref, w2_ref, nb)
    kc = kc * lax.rsqrt(_group_mean_sq(kc, bd_ref[...]) + EPS) * gk_ref[...]
    lane = _lane(nb)
    blk = lax.broadcasted_iota(jnp.int32, (nb, LANE), 0).astype(f32)
    feat = jnp.where(lane == FEAT, blk, jnp.where(lane == FEAT + 1, float(CMP_BLOCK - 1), 0.0))
    for g, half in enumerate(_expand_halves(kc)):
        kc_ref[:, g * LANE:(g + 1) * LANE] = (half + feat).astype(bf16)
    vct_ref[...] = vc.T.astype(bf16)


def _compress_weights(cmp_pe, cmp_w1, cmp_w2):
    def bdiag(w):
        z = jnp.zeros_like(w)
        return jnp.concatenate([jnp.concatenate([w, z], axis=-1), jnp.concatenate([z, w], axis=-1)], axis=-2)
    pe = jnp.concatenate([cmp_pe, cmp_pe], axis=-1)
    return pe, bdiag(cmp_w1).astype(jnp.bfloat16), bdiag(cmp_w2).astype(jnp.bfloat16)


def _compress(rows2d, nb, pe, w1, w2, gk, bd):
    nseq = rows2d.shape[0] // (nb * CMP_BLOCK)
    full = lambda a: pl.BlockSpec(a.shape, lambda b: (0,) * a.ndim)
    args = (rows2d, rows2d, pe, w1, w2, jnp.concatenate([gk, gk]).reshape(1, LANE), bd)
    return pl.pallas_call(
        _compress_kernel,
        grid=(nseq,),
        in_specs=[pl.BlockSpec((nb * CMP_BLOCK, LANE), lambda b: (b, 0)),
                  pl.BlockSpec((nb * CMP_BLOCK, LANE), lambda b: (b, 1))] + [full(a) for a in args[2:]],
        out_specs=[pl.BlockSpec((nb, 2 * LANE), lambda b: (b, 0)), pl.BlockSpec((LANE, nb), lambda b: (0, b))],
        out_shape=[jax.ShapeDtypeStruct((nseq * nb, 2 * LANE), jnp.bfloat16),
                   jax.ShapeDtypeStruct((LANE, nseq * nb), jnp.bfloat16)],
        compiler_params=pltpu.CompilerParams(dimension_semantics=("arbitrary",),
                                             vmem_limit_bytes=VMEM_LIMIT),
        name="nsa_compress",
    )(*args)


NSA_TQ = 256
NSA_TK = 128
NSA_UNROLL = 2


def _nsa_prompt_kernel(qt_ref, gat_ref, ksel_ref, vselt_ref, kwin_ref, vwint_ref, kc_ref, vct_ref,
                       o_ref, selb_ref, buf0_ref, buf1_ref, glist_ref):
    f32, bf16 = jnp.float32, jnp.bfloat16
    tq, tk = NSA_TQ, NSA_TK
    per_q = tq // tk
    w = NSA_GROUP * tq
    g = pl.program_id(1)
    qi = pl.program_id(2)
    nb = kc_ref.shape[0]
    qt = jnp.concatenate([qt_ref[r * LANE:(r + 1) * LANE, :] for r in range(NSA_GROUP)], axis=1)
    qloc = lax.broadcasted_iota(jnp.int32, (1, w), 1) % tq
    qpos = qi * tq + qloc
    krow = lax.broadcasted_iota(jnp.int32, (tk, w), 0)
    key_tile = lambda ref, kt: ref[pl.ds(pl.multiple_of(kt * tk, tk), tk), :]
    val_tile = lambda ref, kt: ref[:, pl.ds(pl.multiple_of(kt * tk, tk), tk)]

    nwin = WINDOW // tk
    tiles = []
    for j in range(nwin + per_q):
        kt = qi * per_q - nwin + j
        ktc = jnp.maximum(kt, 0)
        s = jnp.dot(key_tile(kwin_ref, ktc), qt, preferred_element_type=f32)
        ok = kt >= 0
        if j < per_q:
            ok = ok & (krow > qloc - j * tk)
        if j >= nwin:
            ok = ok & (krow <= qloc + (WINDOW - j * tk))
        tiles.append((jnp.where(ok, s, NEG_INF), val_tile(vwint_ref, ktc)))
    m_w, l_w, acc_w = _softmax_step(tiles, _softmax_init(w))
    o_w = acc_w / l_w

    sc = jnp.dot(kc_ref[...], qt, preferred_element_type=f32)
    blk = lax.broadcasted_iota(jnp.int32, (nb, w), 0)
    vis = blk * CMP_BLOCK + (CMP_BLOCK - 1) <= qpos
    sc = jnp.where(vis, sc, NEG_INF)
    pc = jnp.where(vis, jnp.exp(sc - jnp.max(sc, axis=0, keepdims=True)), 0.0)
    pc = pc / jnp.maximum(jnp.sum(pc, axis=0, keepdims=True), 1e-30)
    o_c = jnp.dot(vct_ref[...], pc.astype(bf16), preferred_element_type=f32)

    imp = sum(pc[:, r * tq:(r + 1) * tq] for r in range(NSA_GROUP))
    blk1 = lax.broadcasted_iota(jnp.int32, (nb, tq), 0)
    cur = (qi * tq + lax.broadcasted_iota(jnp.int32, (1, tq), 1)) // SEL_BLOCK
    imp = jnp.where((blk1 == cur) | (blk1 == 0), FORCE_SCORE, jnp.where(blk1 <= cur, imp, -1.0))
    for _ in range(min(SEL_TOPK, nb)):
        mx = jnp.max(imp, axis=0, keepdims=True)
        idx = jnp.min(jnp.where(imp == mx, blk1, BIG_ID), axis=0, keepdims=True)
        imp = jnp.where(blk1 == idx, -jnp.inf, imp)
    selb = jnp.where(imp == -jnp.inf, 0.0, NEG_INF)
    selb_ref[...] = jnp.concatenate([selb] * NSA_GROUP, axis=1)

    blocks_per_group = NSA_UNROLL * (tk // SEL_BLOCK)
    ntiles = qi * per_q
    nfull = ntiles // NSA_UNROLL
    ngroups = jnp.int32(0)
    for j in range(nb // blocks_per_group):
        picked = imp[j * blocks_per_group:(j + 1) * blocks_per_group, :] == -jnp.inf
        hit = jnp.max(jnp.where(picked, 1.0, 0.0)) > 0.5
        glist_ref[ngroups] = j
        ngroups = ngroups + jnp.where(hit & (j < nfull), 1, 0)

    def sel_bias(kt):
        per = tk // SEL_BLOCK
        return jnp.concatenate([jnp.broadcast_to(selb_ref[pl.ds(kt * per + j, 1), :], (SEL_BLOCK, w))
                                for j in range(per)], axis=0)

    sel_logits = lambda kt: jnp.dot(key_tile(ksel_ref, kt), qt, preferred_element_type=f32) + sel_bias(kt)
    sel_values = lambda kt: val_tile(vselt_ref, kt)
    last_group = ksel_ref.shape[0] // (tk * NSA_UNROLL) - 1
    carry = _tile_loop(ntiles, sel_logits, sel_values, _softmax_init(w), (buf0_ref, buf1_ref), tk, last_group,
                       groups=(ngroups, lambda i: glist_ref[i]), unroll=NSA_UNROLL)
    diag = [(jnp.where(krow + d * tk <= qloc, sel_logits(ntiles + d), NEG_INF), sel_values(ntiles + d))
            for d in range(per_q)]
    m_s, l_s, acc_s = _softmax_step(diag, carry)
    o_s = acc_s / l_s

    for r in range(NSA_GROUP):
        gate = lambda br: gat_ref[pl.ds((g * NSA_GROUP + r) * 3 + br, 1), :]
        sl = slice(r * tq, (r + 1) * tq)
        o_ref[r * HEAD_DIM:(r + 1) * HEAD_DIM, :] = (
            gate(0) * o_c[:, sl] + gate(1) * o_s[:, sl] + gate(2) * o_w[:, sl])


def _nsa_prompt(qat, gat, ksel, vselt, kwin, vwint, kc, vct, nseq, seq_len):
    n = qat.shape[1]
    tq, tk = NSA_TQ, NSA_TK
    nq = seq_len // tq
    nb = seq_len // CMP_BLOCK
    gw = NSA_GROUP * LANE
    return pl.pallas_call(
        _nsa_prompt_kernel,
        grid=(nseq, NSA_KV_HEADS, nq),
        in_specs=[pl.BlockSpec((gw, tq), lambda b, g, q: (g, b * nq + q)),
                  pl.BlockSpec((LANE, tq), lambda b, g, q: (0, b * nq + q)),
                  pl.BlockSpec((seq_len, LANE), lambda b, g, q: (b, g)),
                  pl.BlockSpec((HEAD_DIM, seq_len), lambda b, g, q: (g, b)),
                  pl.BlockSpec((seq_len, LANE), lambda b, g, q: (b, g)),
                  pl.BlockSpec((HEAD_DIM, seq_len), lambda b, g, q: (g, b)),
                  pl.BlockSpec((nb, LANE), lambda b, g, q: (b, g)),
                  pl.BlockSpec((HEAD_DIM, nb), lambda b, g, q: (g, b))],
        out_specs=pl.BlockSpec((NSA_GROUP * HEAD_DIM, tq), lambda b, g, q: (g, b * nq + q)),
        out_shape=jax.ShapeDtypeStruct((NSA_Q_W, n), jnp.float32),
        scratch_shapes=[pltpu.VMEM((nb, NSA_GROUP * tq), jnp.float32)]
        + [pltpu.VMEM((NSA_UNROLL * tk, NSA_GROUP * tq), jnp.float32)] * 2
        + [pltpu.SMEM((nb // (NSA_UNROLL * (tk // SEL_BLOCK)),), jnp.int32)],
        compiler_params=pltpu.CompilerParams(dimension_semantics=("arbitrary", "arbitrary", "arbitrary"),
                                             vmem_limit_bytes=VMEM_LIMIT),
        name="nsa_prompt",
    )(qat, gat, ksel, vselt, kwin, vwint, kc, vct)


FOX_TQ = 1024
FOX_TK = 128


def _fox_prompt_kernel(qt_ref, kb_ref, vbt_ref, o_ref, buf0_ref, buf1_ref):
    f32 = jnp.float32
    tq, tk = FOX_TQ, FOX_TK
    qi = pl.program_id(2)
    qt = qt_ref[...]

    logits = lambda kt: jnp.dot(kb_ref[pl.ds(pl.multiple_of(kt * tk, tk), tk), :], qt, preferred_element_type=f32)
    values = lambda kt: vbt_ref[:, pl.ds(pl.multiple_of(kt * tk, tk), tk)]

    ndiag = tq // tk
    last_group = kb_ref.shape[0] // (tk * TILE_UNROLL) - 1
    carry = _tile_loop(qi * ndiag, logits, values, _softmax_init(tq), (buf0_ref, buf1_ref), tk, last_group)
    krow = lax.broadcasted_iota(jnp.int32, (tk, tq), 0)
    qloc = lax.broadcasted_iota(jnp.int32, (tk, tq), 1)
    diag = []
    for j in range(ndiag):
        kt = qi * ndiag + j
        diag.append((jnp.where(krow + j * tk <= qloc, logits(kt), NEG_INF), values(kt)))
    m, l, acc = _softmax_step(diag, carry)
    o_ref[...] = acc / l


def _fox_prompt(qbt, kb, vbt, nseq, seq_len):
    n = qbt.shape[1]
    nq = seq_len // FOX_TQ
    return pl.pallas_call(
        _fox_prompt_kernel,
        grid=(nseq, FOX_HEADS, nq),
        in_specs=[pl.BlockSpec((LANE, FOX_TQ), lambda b, h, q: (h, b * nq + q)),
                  pl.BlockSpec((seq_len, LANE), lambda b, h, q: (b, h)),
                  pl.BlockSpec((HEAD_DIM, seq_len), lambda b, h, q: (h, b))],
        out_specs=pl.BlockSpec((HEAD_DIM, FOX_TQ), lambda b, h, q: (h, b * nq + q)),
        out_shape=jax.ShapeDtypeStruct((FOX_W, n), jnp.float32),
        scratch_shapes=[pltpu.VMEM((TILE_UNROLL * FOX_TK, FOX_TQ), jnp.float32)] * 2,
        compiler_params=pltpu.CompilerParams(dimension_semantics=("arbitrary", "arbitrary", "arbitrary"),
                                             vmem_limit_bytes=VMEM_LIMIT),
        name="fox_prompt",
    )(qbt, kb, vbt)


DECODE_PER_STEP = 2


def _slope_rows(shape, rows_per_head):
    hd = lax.broadcasted_iota(jnp.int32, shape, 0) // rows_per_head
    return pltpu.bitcast((126 - hd) << 23, jnp.float32)


def _pad_rows(x, rows):
    return jnp.concatenate([x, jnp.zeros((rows - x.shape[0], x.shape[1]), x.dtype)], axis=0)


def _nt_dot(a, b):
    return lax.dot_general(a, b, (((1,), (1,)), ((), ())), preferred_element_type=jnp.float32)


def _joint_softmax(parts):
    m = functools.reduce(jnp.maximum, [jnp.max(p, axis=1, keepdims=True) for p in parts])
    es = [jnp.exp(p - m) for p in parts]
    return es, sum(jnp.sum(e, axis=1, keepdims=True) for e in es)


def _nsa_sample_kernel(npages, ns, nel, pt_ref, *refs):
    f32, bf16 = jnp.float32, jnp.bfloat16
    pages = refs[:nel * npages]
    rest = refs[nel * npages:]
    pe_ref, w1_ref, w2_ref, gk_ref, bd_ref = rest[6:11]
    bufk_ref, bufv_ref = rest[12:14]
    page = pages[0].shape[-1]
    ncol = NSA_KV_HEADS * npages

    for p in range(nel * npages):
        bufk_ref[p * LANE:(p + 1) * LANE, :] = pages[p][0, 0].reshape(LANE, page)
        bufv_ref[p * LANE:(p + 1) * LANE, :] = pages[p][0, 1].reshape(LANE, page)
    kc, vc = _summarize(bufk_ref, bufv_ref, pe_ref, w1_ref, w2_ref, nel * ncol)
    kc = (kc * lax.rsqrt(_group_mean_sq(kc, bd_ref[...]) + EPS) * gk_ref[...]).astype(bf16)
    vc = vc.astype(bf16)

    running = [_nsa_sample_one(e, npages, ns, pages[e * npages:(e + 1) * npages], rest,
                               kc[e * ncol:(e + 1) * ncol], vc[e * ncol:(e + 1) * ncol]) for e in range(nel)]
    while running:
        running = [gen for gen in running if next(gen, "done") != "done"]


def _nsa_sample_one(e, npages, ns, pages, rest, kc, vc):
    f32, bf16 = jnp.float32, jnp.bfloat16
    e_ref, qa_ref, ga_ref, rnew_ref, wnew_ref, state_ref = rest[:6]
    o_ref = rest[11]
    rows = slice(e * ns, (e + 1) * ns)
    page = pages[0].shape[-1]
    past = npages * page
    ncol = NSA_KV_HEADS * npages
    nrow = NSA_HEADS * ns

    lane8 = _lane(ns)
    ql = qa_ref[rows, :]
    qrows = []
    for hd in range(NSA_HEADS):
        g = hd // NSA_GROUP
        t = ql[:, (hd // 2) * LANE:(hd // 2 + 1) * LANE]
        if hd % 2 != g:
            t = pltpu.roll(t, HEAD_DIM, axis=1)
        qrows.append(jnp.where(lane8 // HEAD_DIM == g, t, 0.0))
    qb = (jnp.concatenate(qrows, axis=0) * SCALE).astype(bf16)

    def geom(width):
        tok = lax.broadcasted_iota(jnp.int32, (nrow, width), 0) % ns
        col = lax.broadcasted_iota(jnp.int32, (nrow, width), 1)
        return tok, col, _slope_rows((nrow, width), ns)

    yield
    q_both =qb + pltpu.roll(qb.astype(f32), HEAD_DIM, axis=1).astype(bf16)
    lane64 = lax.broadcasted_iota(jnp.int32, (nrow, LANE), 1) // HEAD_DIM
    tok, col, slope = geom(ncol)
    own = (lax.broadcasted_iota(jnp.int32, (nrow, ncol), 0) // (NSA_GROUP * ns)) == col % NSA_KV_HEADS
    lcs, blks = [], []
    for half in range(2):
        blk = (col // NSA_KV_HEADS) * 2 + half
        lc = _nt_dot(jnp.where(lane64 == half, q_both, 0.0), kc)
        lc = lc - slope * (past + tok - (blk * CMP_BLOCK + CMP_BLOCK - 1)).astype(f32)
        lcs.append(jnp.where(own, lc, NEG_INF))
        blks.append(blk)
    ecs, lsum = _joint_softmax(lcs)
    pcs = [e / lsum for e in ecs]
    res = [jnp.dot(pc.astype(bf16), vc, preferred_element_type=f32) for pc in pcs]
    o_c = jnp.where(lane64 == 0, res[0], res[1])
    o_c = o_c + pltpu.roll(o_c, HEAD_DIM, axis=1)

    yield
    nsel = NSA_KV_HEADS * ns
    col_s = lax.broadcasted_iota(jnp.int32, (nsel, ncol), 1)
    own_s = (lax.broadcasted_iota(jnp.int32, (nsel, ncol), 0) // ns) == col_s % NSA_KV_HEADS
    blks_s = [(col_s // NSA_KV_HEADS) * 2 + half for half in range(2)]
    imps = []
    for half in range(2):
        imp = jnp.concatenate(
            [sum(pcs[half][(g * NSA_GROUP + r) * ns:(g * NSA_GROUP + r + 1) * ns] for r in range(NSA_GROUP))
             for g in range(NSA_KV_HEADS)], axis=0)
        imp = jnp.where(blks_s[half] == 0, FORCE_SCORE, imp)
        imps.append(jnp.where(own_s, imp, -1.0))
    ranks = [jnp.zeros((nsel, ncol), jnp.int32) for _ in range(2)]
    for h2 in range(2):
        for c in range(ncol):
            other = jnp.broadcast_to(imps[h2][:, c:c + 1], (nsel, ncol))
            blk_c = (c // NSA_KV_HEADS) * 2 + h2
            for half in range(2):
                ahead = (other > imps[half]) | ((other == imps[half]) & (blks_s[half] > blk_c))
                ranks[half] = ranks[half] + jnp.where(ahead, 1, 0)
    selexp = 0.0
    for half in range(2):
        sel = jnp.where((ranks[half] < SEL_TOPK - 1) & own_s, 1.0, 0.0)
        sel = jnp.concatenate([sel[g * ns:(g + 1) * ns] for g in range(NSA_KV_HEADS) for _ in range(NSA_GROUP)],
                              axis=0)
        selexp = selexp + jnp.dot(sel.astype(bf16), e_ref[half], preferred_element_type=f32)

    def new_tile(k_new):
        tok, col, slope = geom(LANE)
        s = _nt_dot(qb, _pad_rows(k_new, LANE).astype(bf16))
        return jnp.where(col <= tok, s - slope * (tok - col).astype(f32), NEG_INF)

    def weighted(es, vts, e_new, v_new, lsum):
        acc = sum(_nt_dot(e.astype(bf16), vt.astype(bf16)) for e, vt in zip(es, vts))
        acc = acc + jnp.dot(e_new.astype(bf16), _pad_rows(v_new, LANE).astype(bf16), preferred_element_type=f32)
        return acc / lsum

    yield
    tok, col, slope = geom(past)
    ls = jnp.concatenate([jnp.dot(qb, pages[p][0, 2].reshape(LANE, page).astype(bf16), preferred_element_type=f32)
                          for p in range(npages)], axis=1)
    ls = jnp.where(selexp > 0.5, ls - slope * (past + tok - col).astype(f32), NEG_INF)
    yield
    (es, en), lsum = _joint_softmax([ls, new_tile(rnew_ref[rows, 2 * LANE:3 * LANE])])
    yield
    o_s = weighted([es[:, p * page:(p + 1) * page] for p in range(npages)],
                   [pages[p][0, 3].reshape(LANE, page) for p in range(npages)],
                   en, rnew_ref[rows, 3 * LANE:4 * LANE], lsum)

    yield
    wbuf = state_ref.shape[-1]
    tok, col, slope = geom(wbuf)
    lw = jnp.dot(qb, state_ref[e, 0].reshape(LANE, wbuf).astype(bf16), preferred_element_type=f32)
    lw = jnp.where(col > tok + (wbuf - WINDOW), lw - slope * (wbuf + tok - col).astype(f32), NEG_INF)
    (ew, en), lsum = _joint_softmax([lw, new_tile(wnew_ref[rows, 0:LANE])])
    yield
    o_w = weighted([ew], [state_ref[e, 1].reshape(LANE, wbuf)], en, wnew_ref[rows, LANE:2 * LANE], lsum)

    yield
    ga = ga_ref[rows, :]
    gate = lambda br: jnp.concatenate(
        [jnp.broadcast_to(ga[:, hd * 3 + br:hd * 3 + br + 1], (ns, LANE)) for hd in range(NSA_HEADS)], axis=0)
    o = gate(0) * o_c + gate(1) * o_s + gate(2) * o_w
    for j in range(NSA_HEADS // 2):
        g = (2 * j) // NSA_GROUP
        a = o[2 * j * ns:(2 * j + 1) * ns]
        b = o[(2 * j + 1) * ns:(2 * j + 2) * ns]
        if g == 0:
            b = pltpu.roll(b, HEAD_DIM, axis=1)
        else:
            a = pltpu.roll(a, HEAD_DIM, axis=1)
        o_ref[rows, j * LANE:(j + 1) * LANE] = jnp.where(lane8 < HEAD_DIM, a, b)


def _compress_weights_t(cmp_pe, cmp_w1, cmp_w2):
    pe, w1, w2 = _compress_weights(cmp_pe.transpose(0, 2, 1), cmp_w1.transpose(0, 2, 1, 3), cmp_w2)
    return pe, w1, w2


def _nsa_sample(page_table, cache_t, qa, ga, rows_new, win_new, state_t, pe, w1, w2, gk, bd):
    db, npages = page_table.shape
    page = cache_t.shape[-1]
    assert page == 2 * CMP_BLOCK == LANE
    ns = qa.shape[0] // db
    past = npages * page
    ncol = NSA_KV_HEADS * npages
    half = lax.broadcasted_iota(jnp.int32, (2, ncol, past), 0)
    col = lax.broadcasted_iota(jnp.int32, (2, ncol, past), 1)
    key = lax.broadcasted_iota(jnp.int32, (2, ncol, past), 2)
    expand = jnp.where(key // SEL_BLOCK == (col // NSA_KV_HEADS) * 2 + half, 1.0, 0.0).astype(jnp.bfloat16)
    gk2 = jnp.concatenate([gk, gk]).reshape(1, LANE)
    full = lambda a: pl.BlockSpec(a.shape, lambda b, pt: (0,) * a.ndim)
    nel = DECODE_PER_STEP
    tok = lambda w: pl.BlockSpec((nel * ns, w), lambda b, pt: (b, 0))
    page_specs = [pl.BlockSpec((1,) + cache_t.shape[1:],
                               functools.partial(lambda e, p, b, pt: (pt[b * nel + e, p], 0, 0, 0, 0), e, p))
                  for e in range(nel) for p in range(npages)]
    return pl.pallas_call(
        functools.partial(_nsa_sample_kernel, npages, ns, nel),
        grid_spec=pltpu.PrefetchScalarGridSpec(
            num_scalar_prefetch=1, grid=(db // nel,),
            in_specs=page_specs + [full(expand), tok(NSA_Q_W), tok(LANE), tok(512), tok(256),
                                   pl.BlockSpec((nel,) + state_t.shape[1:], lambda b, pt: (b, 0, 0, 0, 0)),
                                   full(pe), full(w1), full(w2), full(gk2), full(bd)],
            out_specs=tok(NSA_Q_W),
            scratch_shapes=[pltpu.VMEM((nel * npages * LANE, page), jnp.float32),
                            pltpu.VMEM((nel * npages * LANE, page), jnp.float32)]),
        out_shape=jax.ShapeDtypeStruct((db * ns, NSA_Q_W), jnp.float32),
        compiler_params=pltpu.CompilerParams(dimension_semantics=("arbitrary",),
                                             vmem_limit_bytes=VMEM_LIMIT),
        name="nsa_sample",
    )(page_table, *([cache_t] * (nel * npages)), expand, qa, ga, rows_new, win_new, state_t, pe, w1, w2, gk2, bd)


def _fox_decode_kernel(npages, ns, nel, pt_ref, *refs):
    f32, bf16 = jnp.float32, jnp.bfloat16
    qb_ref, knew_ref, lfnew_ref, o_ref = refs[2 * nel * npages:]
    page = refs[0].shape[-1]
    nh = FOX_HEADS
    nrow = nh * ns
    els = range(nel)
    kvt = [refs[e * npages:(e + 1) * npages] for e in els]
    lft = [refs[(nel + e) * npages:(nel + e + 1) * npages] for e in els]
    rows = [slice(e * ns, (e + 1) * ns) for e in els]

    r_io = lax.broadcasted_iota(jnp.int32, (page, page), 0)
    c_io = lax.broadcasted_iota(jnp.int32, (page, page), 1)
    triu = jnp.where(r_io <= c_io, 1.0, 0.0).astype(bf16)
    carry = [jnp.zeros((nh, 1), f32) for _ in els]
    negc = [[] for _ in els]
    for t in range(npages + 1):
        for e in els:
            lf = lft[e][t][0] if t < npages else lfnew_ref[e]
            ct = carry[e] + sum(jnp.dot(pc_, triu, preferred_element_type=f32) for pc_ in _split3(lf))
            carry[e] = ct[:, page - 1:page]
            negc[e].append(jnp.concatenate([jnp.broadcast_to(-ct[hd:hd + 1], (ns, page)) for hd in range(nh)],
                                           axis=0))

    head_of_lane = lax.broadcasted_iota(jnp.int32, (ns, FOX_W), 1) // HEAD_DIM
    q_bd = []
    for e in els:
        q = qb_ref[rows[e], :] * SCALE
        q_bd.append(jnp.concatenate([jnp.where(head_of_lane == hd, q, 0.0) for hd in range(nh)],
                                    axis=0).astype(bf16))

    parts = [[] for _ in els]
    for p in range(npages):
        for e in els:
            parts[e].append(jnp.dot(q_bd[e], kvt[e][p][0, 0].reshape(FOX_W, page).astype(bf16),
                                    preferred_element_type=f32) + negc[e][p])
    tok = lax.broadcasted_iota(jnp.int32, (nrow, page), 0) % ns
    col = lax.broadcasted_iota(jnp.int32, (nrow, page), 1)
    for e in els:
        s_new = _nt_dot(q_bd[e], _pad_rows(knew_ref[rows[e], 0:FOX_W], page).astype(bf16)) + negc[e][npages]
        parts[e].append(jnp.where(col <= tok, s_new, NEG_INF))
    soft = [_joint_softmax(parts[e]) for e in els]
    accs = [jnp.dot(soft[e][0][npages].astype(bf16), _pad_rows(knew_ref[rows[e], FOX_W:2 * FOX_W], page).astype(bf16),
                    preferred_element_type=f32) for e in els]
    for p in range(npages):
        for e in els:
            accs[e] = accs[e] + _nt_dot(soft[e][0][p].astype(bf16), kvt[e][p][0, 1].reshape(FOX_W, page).astype(bf16))
    for e in els:
        acc = accs[e] / soft[e][1]
        o_ref[rows[e], :] = sum(jnp.where(head_of_lane == hd, acc[hd * ns:(hd + 1) * ns], 0.0) for hd in range(nh))


def _fox_decode(page_table, cache_kvt, lft, qb, fox_new, lft_new):
    db, npages = page_table.shape
    ns = qb.shape[0] // db
    nel = DECODE_PER_STEP
    tok = lambda w: pl.BlockSpec((nel * ns, w), lambda b, pt: (b, 0))
    pg = lambda a: [pl.BlockSpec((1,) + a.shape[1:], functools.partial(
        lambda e, p, nd, b, pt: (pt[b * nel + e, p],) + (0,) * nd, e, p, a.ndim - 1))
        for e in range(nel) for p in range(npages)]
    return pl.pallas_call(
        functools.partial(_fox_decode_kernel, npages, ns, nel),
        grid_spec=pltpu.PrefetchScalarGridSpec(
            num_scalar_prefetch=1, grid=(db // nel,),
            in_specs=pg(cache_kvt) + pg(lft) + [tok(FOX_W), tok(2 * FOX_W),
                                                pl.BlockSpec((nel,) + lft_new.shape[1:], lambda b, pt: (b, 0, 0))],
            out_specs=tok(FOX_W)),
        out_shape=jax.ShapeDtypeStruct((db * ns, FOX_W), jnp.float32),
        compiler_params=pltpu.CompilerParams(dimension_semantics=("arbitrary",),
                                             vmem_limit_bytes=VMEM_LIMIT),
        name="fox_decode",
    )(page_table, *([cache_kvt] * (nel * npages)), *([lft] * (nel * npages)), qb, fox_new, lft_new)


BIG_ID = 1 << 20


def _topk_rows(s, k, ids):
    w = ids.shape[1]
    if s.shape[1] > w:
        parts = [_topk_rows(s[:, c:c + w], k, ids) for c in range(0, s.shape[1], w)]
        return jnp.concatenate([p[0] for p in parts], axis=1), jnp.concatenate([p[1] for p in parts], axis=1)
    ids = ids.astype(jnp.float32)
    vals, idxs = [], []
    for _ in range(k):
        m = jnp.max(s, axis=0, keepdims=True)
        idx = jnp.min(jnp.where(s == m, ids, float(BIG_ID)), axis=0, keepdims=True)
        vals.append(m)
        idxs.append(idx)
        s = jnp.where(ids == idx, -jnp.inf, s)
    return jnp.concatenate(vals, axis=0), jnp.concatenate(idxs, axis=0).astype(jnp.int32)


def _pick_rows(sel, table):
    out = jnp.zeros(sel.shape, table.dtype)
    for r in range(table.shape[0]):
        out = jnp.where(sel == r, table[r:r + 1, :], out)
    return out


def _merge_route_kernel(x_ref, on_ref, of_ref, mg_ref, wun_ref, wuf_ref, wo_ref, nf_ref, wqt_ref, sk_ref,
                        x1_ref, h2_ref, i1_ref, i2_ref, g_ref):
    tm = x_ref.shape[0]
    f32, bf16 = jnp.float32, jnp.bfloat16
    tdot = lambda ot, wgt: lax.dot_general(ot.astype(bf16), wgt, (((0,), (0,)), ((), ())),
                                           preferred_element_type=f32)
    a = tdot(on_ref[...], wun_ref[...])
    b = tdot(of_ref[...], wuf_ref[...])
    mixed = mg_ref[:, 0:D_MODEL] * a + mg_ref[:, D_MODEL:2 * D_MODEL] * b
    x1 = x_ref[...] + jnp.dot(mixed.astype(bf16), wo_ref[...], preferred_element_type=f32)
    x1_ref[...] = x1
    h2 = x1 * lax.rsqrt(jnp.mean(x1 * x1, axis=-1, keepdims=True) + EPS) * nf_ref[...]
    h2b = h2.astype(bf16)
    h2_ref[...] = h2b

    nk = PEER_N_KEYS
    key_ids = lax.broadcasted_iota(jnp.int32, (nk, LANE), 0)
    io16 = lax.broadcasted_iota(jnp.int32, (PEER_TOPK, LANE), 0)
    io8 = lax.broadcasted_iota(jnp.int32, (8, LANE), 0)
    cand_ids = jnp.concatenate([io16] + [a_ * PEER_TOPK + io8 for a_ in range(1, 8)]
                               + [(io8 + 8) * PEER_TOPK], axis=0)
    for h in range(PEER_HEADS):
        sv, si = [], []
        for p in range(2):
            hp = 2 * h + p
            qt = lax.dot_general(wqt_ref[hp * PEER_DK_HALF:(hp + 1) * PEER_DK_HALF, :], h2b,
                                 (((1,), (1,)), ((), ())), preferred_element_type=f32)
            st = jnp.dot(sk_ref[hp], qt.astype(bf16), preferred_element_type=f32)
            v, i = _topk_rows(st, PEER_TOPK, key_ids)
            sv.append(v)
            si.append(i)
        s1, s2 = sv
        cand = jnp.concatenate([s1[0:1] + s2] + [s1[a_:a_ + 1] + s2[0:8] for a_ in range(1, 8)]
                               + [s1[8:16] + s2[0:1]], axis=0)
        top, fid = _topk_rows(cand, PEER_TOPK, cand_ids)
        e = jnp.exp(top - jnp.max(top, axis=0, keepdims=True))
        g = e / jnp.sum(e, axis=0, keepdims=True)
        sl = slice(h * PEER_TOPK, (h + 1) * PEER_TOPK)
        i1_ref[:, sl] = _pick_rows(fid >> 4, si[0]).T
        i2_ref[:, sl] = _pick_rows(fid & (PEER_TOPK - 1), si[1]).T
        g_ref[:, sl] = g.T


def _merge_route(x2d, o_nsa, o_fox, mg, w_up_nsa, w_up_fox, w_out, norm_ffn, wq_t, sub_keys, tm=256):
    n = x2d.shape[0]
    row = lambda w: pl.BlockSpec((tm, w), lambda i: (i, 0))
    col = lambda h: pl.BlockSpec((h, tm), lambda i: (0, i))
    full = lambda a: pl.BlockSpec(a.shape, lambda i: (0,) * a.ndim)
    args = (x2d, o_nsa, o_fox, mg, w_up_nsa, w_up_fox, w_out, norm_ffn.reshape(1, D_MODEL), wq_t, sub_keys)
    hk = PEER_HEADS * PEER_TOPK
    return pl.pallas_call(
        _merge_route_kernel,
        grid=(n // tm,),
        in_specs=[row(D_MODEL), col(NSA_Q_W), col(FOX_W), row(MERGE_W)] + [full(a) for a in args[4:]],
        out_specs=[row(D_MODEL), row(D_MODEL), row(hk), row(hk), row(hk)],
        out_shape=[jax.ShapeDtypeStruct((n, D_MODEL), jnp.float32),
                   jax.ShapeDtypeStruct((n, D_MODEL), jnp.bfloat16),
                   jax.ShapeDtypeStruct((n, hk), jnp.int32),
                   jax.ShapeDtypeStruct((n, hk), jnp.int32),
                   jax.ShapeDtypeStruct((n, hk), jnp.float32)],
        compiler_params=pltpu.CompilerParams(dimension_semantics=("arbitrary",),
                                             vmem_limit_bytes=VMEM_LIMIT),
        name="merge_route",
    )(*args)


def _peer_act_kernel(h2_ref, ut_ref, i1_ref, i2_ref, act_ref):
    c = pl.program_id(1)
    ec = ut_ref.shape[1]

    @pl.when(c == 0)
    def _():
        act_ref[...] = jnp.zeros_like(act_ref)

    h2 = h2_ref[...]
    i1 = i1_ref[...]
    i2 = i2_ref[...]
    act = act_ref[...]
    nk = PEER_N_KEYS
    for blk in range(ec // ACT_BLOCK):
        a = jnp.dot(h2, ut_ref[:, blk * ACT_BLOCK:(blk + 1) * ACT_BLOCK],
                    preferred_element_type=jnp.float32)
        for ii in range(ACT_BLOCK // nk):
            got = jnp.take_along_axis(a[:, ii * nk:(ii + 1) * nk], i2, axis=1)
            act = jnp.where(i1 == c * (ec // nk) + blk * (ACT_BLOCK // nk) + ii, got, act)
    act_ref[...] = act


ACT_BLOCK = 512


def _peer_act(h2b, u_t, i1, i2, tm=1024, ec=2048):
    n = h2b.shape[0]
    hk = i1.shape[1]
    return pl.pallas_call(
        _peer_act_kernel,
        grid=(n // tm, u_t.shape[1] // ec),
        in_specs=[pl.BlockSpec((tm, D_MODEL), lambda t, c: (t, 0)),
                  pl.BlockSpec((D_MODEL, ec), lambda t, c: (0, c)),
                  pl.BlockSpec((tm, hk), lambda t, c: (t, 0)),
                  pl.BlockSpec((tm, hk), lambda t, c: (t, 0))],
        out_specs=pl.BlockSpec((tm, hk), lambda t, c: (t, 0)),
        out_shape=jax.ShapeDtypeStruct((n, hk), jnp.float32),
        compiler_params=pltpu.CompilerParams(dimension_semantics=("arbitrary", "arbitrary"),
                                             vmem_limit_bytes=VMEM_LIMIT),
        name="peer_act",
    )(h2b, u_t, i1, i2)


def _peer_coef_kernel(act_ref, g_ref, i1_ref, i2_ref, c_ref, coef_ref):
    tm = act_ref.shape[0]
    nk = PEER_N_KEYS
    coef_ref[...] = g_ref[...] * jax.nn.gelu(act_ref[...])
    sub = lax.broadcasted_iota(jnp.int32, (nk, i1_ref.shape[1]), 0)

    def token(t):
        r1 = i1_ref[pl.ds(t, 1), :]
        r2 = i2_ref[pl.ds(t, 1), :]
        cf = coef_ref[pl.ds(t, 1), :]
        m1 = jnp.where(r1 == sub, cf, 0.0).astype(jnp.bfloat16)
        m2t = jnp.where(r2 == sub, 1.0, 0.0).astype(jnp.bfloat16)
        return lax.dot_general(m1, m2t, (((1,), (1,)), ((), ())), preferred_element_type=jnp.float32)

    def body(tg, carry):
        t0 = pl.multiple_of(tg * COEF_GROUP, COEF_GROUP)
        ct = jnp.stack([token(t0 + u) for u in range(COEF_GROUP)], axis=0)
        c_ref[:, pl.ds(t0, COEF_GROUP), :] = pltpu.einshape("tij->itj", ct).astype(c_ref.dtype)
        return carry

    lax.fori_loop(0, tm // COEF_GROUP, body, 0)


COEF_GROUP = 64


def _peer_coef(act, g, i1, i2, tm=128):
    n, hk = act.shape
    nk = PEER_N_KEYS
    row = pl.BlockSpec((tm, hk), lambda t: (t, 0))
    return pl.pallas_call(
        _peer_coef_kernel,
        grid=(n // tm,),
        in_specs=[row, row, row, row],
        out_specs=pl.BlockSpec((nk, tm, nk), lambda t: (0, t, 0)),
        out_shape=jax.ShapeDtypeStruct((nk, n, nk), jnp.bfloat16),
        scratch_shapes=[pltpu.VMEM((tm, hk), jnp.float32)],
        compiler_params=pltpu.CompilerParams(dimension_semantics=("arbitrary",),
                                             vmem_limit_bytes=VMEM_LIMIT),
        name="peer_coef",
    )(act, g, i1, i2)


def _peer_out_kernel(c_ref, v_ref, x1_ref, y_ref, acc_ref):
    k = pl.program_id(1)

    @pl.when(k == 0)
    def _():
        acc_ref[...] = x1_ref[...]

    acc = acc_ref[...]
    nk = PEER_N_KEYS
    for p in range(c_ref.shape[0] // 2):
        lhs = jnp.concatenate([c_ref[2 * p], c_ref[2 * p + 1]], axis=1)
        acc = acc + jnp.dot(lhs, v_ref[2 * p * nk:(2 * p + 2) * nk, :], preferred_element_type=jnp.float32)
    acc_ref[...] = acc

    @pl.when(k == pl.num_programs(1) - 1)
    def _():
        y_ref[...] = acc_ref[...]


def _peer_out(c3, v_b, x1, tm=1024, tk=2048):
    nk, n, _ = c3.shape
    ne = nk * nk
    return pl.pallas_call(
        _peer_out_kernel,
        grid=(n // tm, ne // tk),
        in_specs=[pl.BlockSpec((tk // nk, tm, nk), lambda t, k: (k, t, 0)),
                  pl.BlockSpec((tk, D_MODEL), lambda t, k: (k, 0)),
                  pl.BlockSpec((tm, D_MODEL), lambda t, k: (t, 0))],
        out_specs=pl.BlockSpec((tm, D_MODEL), lambda t, k: (t, 0)),
        out_shape=jax.ShapeDtypeStruct((n, D_MODEL), jnp.float32),
        scratch_shapes=[pltpu.VMEM((tm, D_MODEL), jnp.float32)],
        compiler_params=pltpu.CompilerParams(dimension_semantics=("arbitrary", "arbitrary"),
                                             vmem_limit_bytes=VMEM_LIMIT),
        name="peer_out",
    )(c3, v_b, x1)


def _peer_weights(w_up_nsa, w_up_fox, w_out, norm_ffn, peer_w_query, peer_sub_keys, peer_u, peer_v):
    bf16 = jnp.bfloat16
    return dict(w_up_nsa=w_up_nsa.astype(bf16), w_up_fox=w_up_fox.astype(bf16), w_out=w_out.astype(bf16),
                norm_ffn=norm_ffn, wq_t=peer_w_query.T.astype(bf16),
                sub_keys=peer_sub_keys.reshape(2 * PEER_HEADS, PEER_N_KEYS, PEER_DK_HALF).astype(bf16),
                u=peer_u.T.astype(bf16), v=peer_v.astype(bf16))


def _merge_peer(x2d, o_nsa, o_fox, mg, wts):
    x1, h2b, i1, i2, g = _merge_route(x2d, o_nsa, o_fox, mg, wts['w_up_nsa'], wts['w_up_fox'], wts['w_out'],
                                      wts['norm_ffn'], wts['wq_t'], wts['sub_keys'])
    act = _peer_act(h2b, wts['u'], i1, i2)
    c3 = _peer_coef(act, g, i1, i2)
    return _peer_out(c3, wts['v'], x1)


def kernel(x_prompt, x_sample, cache_nsa, cache_fox_kv, cache_fox_logf, state_nsa_win, page_table,
           norm_attn, w_in, fox_f_bias, nsa_q_norm, nsa_k_norm, fox_q_norm, fox_k_norm,
           cmp_pe, cmp_w1, cmp_w2, w_up_nsa, w_up_fox, w_out, norm_ffn,
           peer_w_query, peer_sub_keys, peer_u, peer_v):
    w_front = _front_weights(w_in)
    bd = _block_diag_mean()
    wts = _peer_weights(w_up_nsa, w_up_fox, w_out, norm_ffn, peer_w_query, peer_sub_keys, peer_u, peer_v)
    cmp_wts = _compress_weights(cmp_pe, cmp_w1, cmp_w2)

    bp, seq, _ = x_prompt.shape
    n_p = bp * seq
    (rows_t, win_t, fox_t, logf_p2d, mg_p, qat, gat, ksel, kwin, vselt, vwint, qbt, kb, vbt, rows_cmp) = _front_attn(
        x_prompt.reshape(n_p, D_MODEL), seq, norm_attn, w_front, bd, fox_f_bias,
        nsa_q_norm, nsa_k_norm, fox_q_norm, fox_k_norm)
    kc, vct = _compress(rows_cmp, seq // CMP_BLOCK, *cmp_wts, nsa_k_norm[0], bd)
    o_nsa_t = _nsa_prompt(qat, gat, ksel, vselt, kwin, vwint, kc, vct, bp, seq)
    o_fox_t = _fox_prompt(qbt, kb, vbt, bp, seq)
    y_p = _merge_peer(x_prompt.reshape(n_p, D_MODEL), o_nsa_t, o_fox_t, mg_p, wts).reshape(x_prompt.shape)
    to_rows = lambda a, *dims: a.reshape(bp, *dims, a.shape[-1]).transpose(0, len(dims) + 1, *range(1, len(dims) + 1))
    nsa_p = to_rows(rows_t, 4, NSA_KV_HEADS, HEAD_DIM)
    fox_p = to_rows(fox_t, 2, FOX_HEADS, HEAD_DIM)
    logf_p = logf_p2d.reshape(bp, seq, FOX_HEADS)
    win_p = to_rows(win_t[:, :, seq - min(WINDOW, seq):], 2, NSA_KV_HEADS, HEAD_DIM)

    db, ns, _ = x_sample.shape
    n_s = db * ns
    n_pool, page = cache_nsa.shape[:2]
    wbuf = state_nsa_win.shape[1]
    qa_s, rows_s, win_s2d, ga_s, qb_s, fox_s2d, logf_s2d, mg_s = _front(
        x_sample.reshape(n_s, D_MODEL), norm_attn, w_front, bd, fox_f_bias,
        nsa_q_norm, nsa_k_norm, fox_q_norm, fox_k_norm)
    o_nsa_s = _nsa_sample(page_table, cache_nsa.transpose(0, 2, 3, 4, 1), qa_s, ga_s, rows_s, win_s2d,
                          state_nsa_win.transpose(0, 2, 3, 4, 1),
                          *_compress_weights_t(cmp_pe, cmp_w1, cmp_w2), nsa_k_norm[0], bd)
    lft_new = jnp.pad(logf_s2d.reshape(db, ns, FOX_HEADS).transpose(0, 2, 1), ((0, 0), (0, 0), (0, page - ns)))
    o_fox_s = _fox_decode(page_table, cache_fox_kv.transpose(0, 2, 3, 4, 1), cache_fox_logf.transpose(0, 2, 1),
                          qb_s, fox_s2d, lft_new)
    y_s = _merge_peer(x_sample.reshape(n_s, D_MODEL), o_nsa_s.T, o_fox_s.T, mg_s, wts).reshape(x_sample.shape)
    nsa_s = rows_s.reshape(db, ns, 4, NSA_KV_HEADS, HEAD_DIM)
    fox_s = fox_s2d.reshape(db, ns, 2, FOX_HEADS, HEAD_DIM)
    logf_s = logf_s2d.reshape(db, ns, FOX_HEADS)
    win_s = jnp.concatenate([state_nsa_win[:, ns:], win_s2d.reshape(db, ns, 2, NSA_KV_HEADS, HEAD_DIM)], axis=1)
    return (y_p, y_s, nsa_p, fox_p, logf_p, win_p, nsa_s, fox_s, logf_s, win_s)
```

```python
import functools

import jax
import jax.numpy as jnp
from jax import lax
from jax.experimental import pallas as pl
from jax.experimental.pallas import tpu as pltpu

D_MODEL = 1024
HEAD_DIM = 64
NSA_HEADS = 8
NSA_KV_HEADS = 2
NSA_GROUP = NSA_HEADS // NSA_KV_HEADS
CMP_BLOCK = 64
SEL_BLOCK = CMP_BLOCK
SEL_TOPK = 16
WINDOW = 512
FOX_HEADS = 8
Q_BLOCK = 128
PEER_HEADS = 8
PEER_N_KEYS = 128
PEER_DK = 256
PEER_DK_HALF = PEER_DK // 2
PEER_TOPK = 16
PEER_CHUNK = 256

NSA_Q_W = NSA_HEADS * HEAD_DIM
NSA_KV_W = 6 * NSA_KV_HEADS * HEAD_DIM
NSA_GATE_W = 3 * NSA_HEADS
FOX_W = FOX_HEADS * HEAD_DIM
FOX_QKV_W = 3 * FOX_W
FOX_F_W = FOX_HEADS
MERGE_W = 2 * D_MODEL
SPLIT_Q_A = NSA_Q_W
SPLIT_KV_A = SPLIT_Q_A + NSA_KV_W
SPLIT_G_A = SPLIT_KV_A + NSA_GATE_W
SPLIT_QKV_B = SPLIT_G_A + FOX_QKV_W
SPLIT_F_B = SPLIT_QKV_B + FOX_F_W
IN_WIDTH = SPLIT_F_B + MERGE_W

SCALE = HEAD_DIM ** -0.5
FORCE_SCORE = float(NSA_GROUP + 1)
NEG_INF = -1e30
EPS = 1e-6

LANE = 128
VMEM_LIMIT = 48 * 1024 * 1024


def _group_mean_sq(x, bd):
    sq = x * x
    hi = sq.astype(jnp.bfloat16)
    lo = (sq - hi.astype(jnp.float32)).astype(jnp.bfloat16)
    return (jnp.dot(hi, bd, preferred_element_type=jnp.float32)
            + jnp.dot(lo, bd, preferred_element_type=jnp.float32))


def _head_rms(x, g, bd):
    outs = []
    for c in range(x.shape[1] // LANE):
        xc = x[:, c * LANE:(c + 1) * LANE]
        outs.append(xc * lax.rsqrt(_group_mean_sq(xc, bd) + EPS) * g)
    return outs[0] if len(outs) == 1 else jnp.concatenate(outs, axis=1)


def _front_kernel(x_ref, na_ref, w_ref, bd_ref, fb_ref, gq_a_ref, gk_sel_ref, gk_win_ref,
                  gq_b_ref, gk_b_ref,
                  qa_ref, rows_ref, win_ref, ga_ref, qb_ref, fox_ref, logf_ref, mg_ref):
    x = x_ref[...]
    h = x * lax.rsqrt(jnp.mean(x * x, axis=-1, keepdims=True) + EPS) * na_ref[...]
    hb = h.astype(jnp.bfloat16)
    bd = bd_ref[...]

    def proj(c0, width):
        return jnp.dot(hb, w_ref[:, c0:c0 + width], preferred_element_type=jnp.float32)

    c = 0
    qa_ref[...] = _head_rms(proj(c, NSA_Q_W), gq_a_ref[...], bd)
    c += NSA_Q_W
    rows_ref[:, 0:256] = proj(c, 256)
    rows_ref[:, 256:384] = _head_rms(proj(c + 256, 128), gk_sel_ref[...], bd)
    rows_ref[:, 384:512] = proj(c + 384, 128)
    win_ref[:, 0:128] = _head_rms(proj(c + 512, 128), gk_win_ref[...], bd)
    win_ref[:, 128:256] = proj(c + 640, 128)
    c += NSA_KV_W
    qb_ref[...] = _head_rms(proj(c, FOX_W), gq_b_ref[...], bd)
    fox_ref[:, 0:FOX_W] = _head_rms(proj(c + FOX_W, FOX_W), gk_b_ref[...], bd)
    fox_ref[:, FOX_W:2 * FOX_W] = proj(c + 2 * FOX_W, FOX_W)
    c += FOX_QKV_W
    for j in range(MERGE_W // 512):
        mg_ref[:, j * 512:(j + 1) * 512] = jax.nn.sigmoid(proj(c + j * 512, 512))
    c += MERGE_W
    ga_ref[...] = jax.nn.sigmoid(proj(c, LANE))
    f = proj(c + LANE, LANE)[:, 0:FOX_F_W] + fb_ref[...]
    logf_ref[...] = jnp.minimum(f, 0.0) - jnp.log1p(jnp.exp(-jnp.abs(f)))


def _front(x2d, norm_attn, w_front, bd, fox_f_bias, nsa_q_norm, nsa_k_norm, fox_q_norm, fox_k_norm,
           tm=256):
    n = x2d.shape[0]
    wf = w_front.shape[1]
    two = lambda g: jnp.concatenate([g, g]).reshape(1, LANE)
    row = lambda w: pl.BlockSpec((tm, w), lambda i: (i, 0))
    full = lambda a: pl.BlockSpec(a.shape, lambda i: (0,) * a.ndim)
    args = (x2d, norm_attn.reshape(1, D_MODEL), w_front, bd, fox_f_bias.reshape(1, FOX_F_W),
            two(nsa_q_norm), two(nsa_k_norm[1]), two(nsa_k_norm[2]), two(fox_q_norm), two(fox_k_norm))
    widths = (NSA_Q_W, 512, 256, LANE, FOX_W, 2 * FOX_W, FOX_F_W, MERGE_W)
    return pl.pallas_call(
        _front_kernel,
        grid=(n // tm,),
        in_specs=[row(D_MODEL)] + [full(a) for a in args[1:]],
        out_specs=[row(w) for w in widths],
        out_shape=[jax.ShapeDtypeStruct((n, w), jnp.float32) for w in widths],
        compiler_params=pltpu.CompilerParams(dimension_semantics=("arbitrary",),
                                             vmem_limit_bytes=VMEM_LIMIT),
        name="front",
    )(*args)


FEAT = HEAD_DIM


def _lane(tm):
    return lax.broadcasted_iota(jnp.int32, (tm, LANE), 1)


def _expand_halves(x):
    lo = _lane(x.shape[0]) < HEAD_DIM
    return jnp.where(lo, x, 0.0), jnp.where(lo, pltpu.roll(x, HEAD_DIM, axis=1), 0.0)


def _split3(x):
    hi = x.astype(jnp.bfloat16)
    r = x - hi.astype(jnp.float32)
    mid = r.astype(jnp.bfloat16)
    lo = (r - mid.astype(jnp.float32)).astype(jnp.bfloat16)
    return hi, mid, lo


def _front_attn_kernel(seq_len, x_ref, na_ref, w_ref, bd_ref, fb_ref, gq_a_ref, gk_sel_ref, gk_win_ref,
                       gq_b_ref, gk_b_ref, place_ref,
                       rows_ref, win_ref, fox_ref, logf_ref, mg_ref,
                       qat_ref, gat_ref, ksel_ref, kwin_ref, vselt_ref, vwint_ref, qbt_ref, kb_ref, vbt_ref, rcmp_ref,
                       carry_ref):
    f32, bf16 = jnp.float32, jnp.bfloat16
    tm = x_ref.shape[0]
    i = pl.program_id(0)
    x = x_ref[...]
    h = x * lax.rsqrt(jnp.mean(x * x, axis=-1, keepdims=True) + EPS) * na_ref[...]
    hb = h.astype(bf16)
    bd = bd_ref[...]
    lane = _lane(tm)
    pos = (i * tm + lax.broadcasted_iota(jnp.int32, (tm, LANE), 0)) % seq_len
    kfeat = jnp.where(lane == FEAT, (pos // SEL_BLOCK).astype(f32),
                      jnp.where(lane == FEAT + 1, (pos % SEL_BLOCK).astype(f32), 0.0))

    def proj(c0, width):
        return jnp.dot(hb, w_ref[:, c0:c0 + width], preferred_element_type=f32)

    c = 0
    qa = _head_rms(proj(c, NSA_Q_W), gq_a_ref[...], bd)
    for j in range(NSA_HEADS // 2):
        for s, half in enumerate(_expand_halves(qa[:, j * LANE:(j + 1) * LANE])):
            hd = 2 * j + s
            slope = 2.0 ** -(hd + 1)
            qfeat = jnp.where(lane == FEAT, slope * SEL_BLOCK, jnp.where(lane == FEAT + 1, slope, 0.0))
            qat_ref[hd * LANE:(hd + 1) * LANE, :] = (half * SCALE + qfeat).T.astype(bf16)
    c += NSA_Q_W
    for j in range(2):
        raw = proj(c + j * LANE, LANE)
        rcmp_ref[:, j * LANE:(j + 1) * LANE] = raw
        rows_ref[0, j * LANE:(j + 1) * LANE, :] = raw.T
    ksel = _head_rms(proj(c + 256, 128), gk_sel_ref[...], bd)
    rows_ref[0, 2 * LANE:3 * LANE, :] = ksel.T
    vsel_t = proj(c + 384, 128).T
    rows_ref[0, 3 * LANE:4 * LANE, :] = vsel_t
    kwin = _head_rms(proj(c + 512, 128), gk_win_ref[...], bd)
    win_ref[0, 0:LANE, :] = kwin.T
    vwin_t = proj(c + 640, 128).T
    win_ref[0, LANE:2 * LANE, :] = vwin_t
    for g, (ks_g, kw_g) in enumerate(zip(_expand_halves(ksel), _expand_halves(kwin))):
        ksel_ref[:, g * LANE:(g + 1) * LANE] = (ks_g + kfeat).astype(bf16)
        kwin_ref[:, g * LANE:(g + 1) * LANE] = (kw_g + kfeat).astype(bf16)
    vselt_ref[...] = vsel_t.astype(bf16)
    vwint_ref[...] = vwin_t.astype(bf16)
    c += NSA_KV_W

    qb = _head_rms(proj(c, FOX_W), gq_b_ref[...], bd)
    ones3 = jnp.where((lane >= FEAT) & (lane < FEAT + 3), 1.0, 0.0)
    for j in range(FOX_HEADS // 2):
        for s, half in enumerate(_expand_halves(qb[:, j * LANE:(j + 1) * LANE])):
            hd = 2 * j + s
            qbt_ref[hd * LANE:(hd + 1) * LANE, :] = (half * SCALE + ones3).T.astype(bf16)
    kbn = _head_rms(proj(c + FOX_W, FOX_W), gk_b_ref[...], bd)
    for j in range(FOX_HEADS // 2):
        fox_ref[0, j * LANE:(j + 1) * LANE, :] = kbn[:, j * LANE:(j + 1) * LANE].T
        vb_t = proj(c + 2 * FOX_W + j * LANE, LANE).T
        fox_ref[0, FOX_W + j * LANE:FOX_W + (j + 1) * LANE, :] = vb_t
        vbt_ref[j * LANE:(j + 1) * LANE, :] = vb_t.astype(bf16)
    c += FOX_QKV_W
    for j in range(MERGE_W // 512):
        mg_ref[:, j * 512:(j + 1) * 512] = jax.nn.sigmoid(proj(c + j * 512, 512))
    c += MERGE_W
    gat_ref[...] = jax.nn.sigmoid(proj(c, LANE)).T
    f = proj(c + LANE, LANE) + fb_ref[...]
    lf = jnp.minimum(f, 0.0) - jnp.log1p(jnp.exp(-jnp.abs(f)))
    logf_ref[...] = lf[:, 0:FOX_F_W]

    @pl.when((i * tm) % seq_len == 0)
    def _():
        carry_ref[...] = jnp.zeros_like(carry_ref)

    r_io = lax.broadcasted_iota(jnp.int32, (tm, tm), 0)
    c_io = lax.broadcasted_iota(jnp.int32, (tm, tm), 1)
    tri = jnp.where(c_io <= r_io, 1.0, 0.0).astype(bf16)
    csum = carry_ref[...] + sum(jnp.dot(tri, p, preferred_element_type=f32) for p in _split3(lf))
    carry_ref[...] = csum[tm - 1:tm, :]
    pieces = jnp.concatenate(_split3(-csum), axis=1)
    cfeat = jnp.dot(pieces, place_ref[...], preferred_element_type=f32)
    for j in range(FOX_HEADS // 2):
        for s, half in enumerate(_expand_halves(kbn[:, j * LANE:(j + 1) * LANE])):
            hd = 2 * j + s
            kb_ref[:, hd * LANE:(hd + 1) * LANE] = (half + cfeat[:, hd * LANE:(hd + 1) * LANE]).astype(bf16)


def _fox_feature_placement():
    r = lax.broadcasted_iota(jnp.int32, (3 * LANE, FOX_HEADS * LANE), 0)
    c = lax.broadcasted_iota(jnp.int32, (3 * LANE, FOX_HEADS * LANE), 1)
    s, hd = r // LANE, r % LANE
    return jnp.where((hd < FOX_HEADS) & (c == hd * LANE + FEAT + s), 1.0, 0.0).astype(jnp.bfloat16)


def _front_attn(x2d, seq_len, norm_attn, w_front, bd, fox_f_bias, nsa_q_norm, nsa_k_norm, fox_q_norm,
                fox_k_norm, tm=256):
    n = x2d.shape[0]
    f32, bf16 = jnp.float32, jnp.bfloat16
    two = lambda g: jnp.concatenate([g, g]).reshape(1, LANE)
    row = lambda w: pl.BlockSpec((tm, w), lambda i: (i, 0))
    col = lambda h: pl.BlockSpec((h, tm), lambda i: (0, i))
    full = lambda a: pl.BlockSpec(a.shape, lambda i: (0,) * a.ndim)
    fb = jnp.pad(fox_f_bias, (0, LANE - FOX_F_W)).reshape(1, LANE)
    args = (x2d, norm_attn.reshape(1, D_MODEL), w_front, bd, fb,
            two(nsa_q_norm), two(nsa_k_norm[1]), two(nsa_k_norm[2]), two(fox_q_norm), two(fox_k_norm),
            _fox_feature_placement())
    per_seq = seq_len // tm
    nseq = n // seq_len
    leaf = lambda h: (pl.BlockSpec((1, h, tm), lambda i: (i // per_seq, 0, i % per_seq)), (nseq, h, seq_len), f32)
    outs = [leaf(512), leaf(256), leaf(2 * FOX_W),
            (row(FOX_F_W), (n, FOX_F_W), f32), (row(MERGE_W), (n, MERGE_W), f32),
            (col(NSA_HEADS * LANE), (NSA_HEADS * LANE, n), bf16), (col(LANE), (LANE, n), f32),
            (row(2 * LANE), (n, 2 * LANE), bf16), (row(2 * LANE), (n, 2 * LANE), bf16),
            (col(LANE), (LANE, n), bf16), (col(LANE), (LANE, n), bf16),
            (col(FOX_HEADS * LANE), (FOX_HEADS * LANE, n), bf16),
            (row(FOX_HEADS * LANE), (n, FOX_HEADS * LANE), bf16), (col(FOX_W), (FOX_W, n), bf16),
            (row(2 * LANE), (n, 2 * LANE), f32)]
    return pl.pallas_call(
        functools.partial(_front_attn_kernel, seq_len),
        grid=(n // tm,),
        in_specs=[row(D_MODEL)] + [full(a) for a in args[1:]],
        out_specs=[o[0] for o in outs],
        out_shape=[jax.ShapeDtypeStruct(o[1], o[2]) for o in outs],
        scratch_shapes=[pltpu.VMEM((1, LANE), f32)],
        compiler_params=pltpu.CompilerParams(dimension_semantics=("arbitrary",),
                                             vmem_limit_bytes=VMEM_LIMIT),
        name="front_attn",
    )(*args)


def _front_weights(w_in):
    pad = lambda w: jnp.pad(w, ((0, 0), (0, LANE - w.shape[1])))
    parts = [w_in[:, :SPLIT_KV_A], w_in[:, SPLIT_G_A:SPLIT_QKV_B], w_in[:, SPLIT_F_B:],
             pad(w_in[:, SPLIT_KV_A:SPLIT_G_A]), pad(w_in[:, SPLIT_QKV_B:SPLIT_F_B])]
    return jnp.concatenate(parts, axis=1).astype(jnp.bfloat16)


def _block_diag_mean():
    r = lax.broadcasted_iota(jnp.int32, (LANE, LANE), 0) // HEAD_DIM
    c = lax.broadcasted_iota(jnp.int32, (LANE, LANE), 1) // HEAD_DIM
    return jnp.where(r == c, 1.0 / HEAD_DIM, 0.0).astype(jnp.bfloat16)


def _softmax_step(tiles, carry):
    m, l, acc = carry
    m_new = functools.reduce(jnp.maximum, [jnp.max(s, axis=0, keepdims=True) for s, _ in tiles], m)
    alpha = jnp.exp(m - m_new)
    ps = [jnp.exp(s - m_new) for s, _ in tiles]
    ones = jnp.ones((SUM_ROWS, tiles[0][1].shape[1]), jnp.bfloat16)
    res = sum(jnp.dot(jnp.concatenate([vt, ones], axis=0), p.astype(jnp.bfloat16),
                      preferred_element_type=jnp.float32) for p, (_, vt) in zip(ps, tiles))
    l = alpha * l + res[HEAD_DIM:HEAD_DIM + 1]
    acc = alpha * acc + res[0:HEAD_DIM]
    return m_new, l, acc


SUM_ROWS = 16


TILE_UNROLL = 4


def _tile_loop(n, logits, values, carry, bufs, tk, last_group, groups=None, unroll=TILE_UNROLL):
    tile = lambda kt: (logits(kt), values(kt))
    nu = n // unroll
    count, group_of = (nu, lambda i: i) if groups is None else groups

    def fill(buf, i):
        grp = jnp.minimum(group_of(jnp.minimum(i, jnp.maximum(count - 1, 0))), last_group)
        for j in range(unroll):
            buf[j * tk:(j + 1) * tk, :] = logits(grp * unroll + j)

    def drain(buf, i, c):
        grp = group_of(i)
        return _softmax_step([(buf[j * tk:(j + 1) * tk, :], values(grp * unroll + j)) for j in range(unroll)], c)

    def body(i, c):
        fill(bufs[1], 2 * i + 1)
        c = drain(bufs[0], 2 * i, c)
        fill(bufs[0], 2 * i + 2)
        return drain(bufs[1], 2 * i + 1, c)

    fill(bufs[0], 0)
    carry = lax.fori_loop(0, count // 2, body, carry)
    carry = lax.cond(count % 2 == 1, lambda c: drain(bufs[0], count - 1, c), lambda c: c, carry)
    base = nu * unroll
    size = unroll // 2
    while size >= 1:
        has = (n & size) != 0
        carry = lax.cond(has, functools.partial(
            lambda b, sz, c: _softmax_step([tile(b + u) for u in range(sz)], c), base, size),
            lambda c: c, carry)
        base = base + jnp.where(has, size, 0)
        size //= 2
    return carry


def _softmax_init(w):
    return (jnp.full((1, w), NEG_INF, jnp.float32), jnp.zeros((1, w), jnp.float32),
            jnp.zeros((HEAD_DIM, w), jnp.float32))


def _summarize(xk_ref, xv_ref, pe_ref, w1_ref, w2_ref, nb):
    f32, bf16 = jnp.float32, jnp.bfloat16
    x_refs = (xk_ref, xv_ref)

    def body(l, accs):
        return tuple(
            acc + jnp.dot((x_refs[s][pl.ds(l, nb, stride=CMP_BLOCK), :] + pe_ref[s, pl.ds(l, 1), :]).astype(bf16),
                          w1_ref[s, l], preferred_element_type=f32)
            for s, acc in enumerate(accs))

    accs = lax.fori_loop(0, CMP_BLOCK, body, (jnp.zeros((nb, LANE), f32),) * 2, unroll=16)
    return tuple(jnp.dot(jax.nn.gelu(acc).astype(bf16), w2_ref[s], preferred_element_type=f32)
                 for s, acc in enumerate(accs))


def _compress_kernel(xk_ref, xv_ref, pe_ref, w1_ref, w2_ref, gk_ref, bd_ref, kc_ref, vct_ref):
    f32, bf16 = jnp.float32, jnp.bfloat16
    nb = kc_ref.shape[0]
    kc, vc = _summarize(xk_ref, xv_ref, pe_ref, w1_ref, w2_ref, nb)
    kc = kc * lax.rsqrt(_group_mean_sq(kc, bd_ref[...]) + EPS) * gk_ref[...]
    lane = _lane(nb)
    blk = lax.broadcasted_iota(jnp.int32, (nb, LANE), 0).astype(f32)
    feat = jnp.where(lane == FEAT, blk, jnp.where(lane == FEAT + 1, float(CMP_BLOCK - 1), 0.0))
    for g, half in enumerate(_expand_halves(kc)):
        kc_ref[:, g * LANE:(g + 1) * LANE] = (half + feat).astype(bf16)
    vct_ref[...] = vc.T.astype(bf16)


def _compress_weights(cmp_pe, cmp_w1, cmp_w2):
    def bdiag(w):
        z = jnp.zeros_like(w)
        return jnp.concatenate([jnp.concatenate([w, z], axis=-1), jnp.concatenate([z, w], axis=-1)], axis=-2)
    pe = jnp.concatenate([cmp_pe, cmp_pe], axis=-1)
    return pe, bdiag(cmp_w1).astype(jnp.bfloat16), bdiag(cmp_w2).astype(jnp.bfloat16)


def _compress(rows2d, nb, pe, w1, w2, gk, bd):
    nseq = rows2d.shape[0] // (nb * CMP_BLOCK)
    full = lambda a: pl.BlockSpec(a.shape, lambda b: (0,) * a.ndim)
    args = (rows2d, rows2d, pe, w1, w2, jnp.concatenate([gk, gk]).reshape(1, LANE), bd)
    return pl.pallas_call(
        _compress_kernel,
        grid=(nseq,),
        in_specs=[pl.BlockSpec((nb * CMP_BLOCK, LANE), lambda b: (b, 0)),
                  pl.BlockSpec((nb * CMP_BLOCK, LANE), lambda b: (b, 1))] + [full(a) for a in args[2:]],
        out_specs=[pl.BlockSpec((nb, 2 * LANE), lambda b: (b, 0)), pl.BlockSpec((LANE, nb), lambda b: (0, b))],
        out_shape=[jax.ShapeDtypeStruct((nseq * nb, 2 * LANE), jnp.bfloat16),
                   jax.ShapeDtypeStruct((LANE, nseq * nb), jnp.bfloat16)],
        compiler_params=pltpu.CompilerParams(dimension_semantics=("arbitrary",),
                                             vmem_limit_bytes=VMEM_LIMIT),
        name="nsa_compress",
    )(*args)


NSA_TQ = 256
NSA_TK = 128
NSA_UNROLL = 2


def _nsa_prompt_kernel(qt_ref, gat_ref, ksel_ref, vselt_ref, kwin_ref, vwint_ref, kc_ref, vct_ref,
                       o_ref, selb_ref, buf0_ref, buf1_ref, glist_ref):
    f32, bf16 = jnp.float32, jnp.bfloat16
    tq, tk = NSA_TQ, NSA_TK
    per_q = tq // tk
    w = NSA_GROUP * tq
    g = pl.program_id(1)
    qi = pl.program_id(2)
    nb = kc_ref.shape[0]
    qt = jnp.concatenate([qt_ref[r * LANE:(r + 1) * LANE, :] for r in range(NSA_GROUP)], axis=1)
    qloc = lax.broadcasted_iota(jnp.int32, (1, w), 1) % tq
    qpos = qi * tq + qloc
    krow = lax.broadcasted_iota(jnp.int32, (tk, w), 0)
    key_tile = lambda ref, kt: ref[pl.ds(pl.multiple_of(kt * tk, tk), tk), :]
    val_tile = lambda ref, kt: ref[:, pl.ds(pl.multiple_of(kt * tk, tk), tk)]

    nwin = WINDOW // tk
    tiles = []
    for j in range(nwin + per_q):
        kt = qi * per_q - nwin + j
        ktc = jnp.maximum(kt, 0)
        s = jnp.dot(key_tile(kwin_ref, ktc), qt, preferred_element_type=f32)
        ok = kt >= 0
        if j < per_q:
            ok = ok & (krow > qloc - j * tk)
        if j >= nwin:
            ok = ok & (krow <= qloc + (WINDOW - j * tk))
        tiles.append((jnp.where(ok, s, NEG_INF), val_tile(vwint_ref, ktc)))
    m_w, l_w, acc_w = _softmax_step(tiles, _softmax_init(w))
    o_w = acc_w / l_w

    sc = jnp.dot(kc_ref[...], qt, preferred_element_type=f32)
    blk = lax.broadcasted_iota(jnp.int32, (nb, w), 0)
    vis = blk * CMP_BLOCK + (CMP_BLOCK - 1) <= qpos
    sc = jnp.where(vis, sc, NEG_INF)
    pc = jnp.where(vis, jnp.exp(sc - jnp.max(sc, axis=0, keepdims=True)), 0.0)
    pc = pc / jnp.maximum(jnp.sum(pc, axis=0, keepdims=True), 1e-30)
    o_c = jnp.dot(vct_ref[...], pc.astype(bf16), preferred_element_type=f32)

    imp = sum(pc[:, r * tq:(r + 1) * tq] for r in range(NSA_GROUP))
    blk1 = lax.broadcasted_iota(jnp.int32, (nb, tq), 0)
    cur = (qi * tq + lax.broadcasted_iota(jnp.int32, (1, tq), 1)) // SEL_BLOCK
    imp = jnp.where((blk1 == cur) | (blk1 == 0), FORCE_SCORE, jnp.where(blk1 <= cur, imp, -1.0))
    for _ in range(min(SEL_TOPK, nb)):
        mx = jnp.max(imp, axis=0, keepdims=True)
        idx = jnp.min(jnp.where(imp == mx, blk1, BIG_ID), axis=0, keepdims=True)
        imp = jnp.where(blk1 == idx, -jnp.inf, imp)
    selb = jnp.where(imp == -jnp.inf, 0.0, NEG_INF)
    selb_ref[...] = jnp.concatenate([selb] * NSA_GROUP, axis=1)

    blocks_per_group = NSA_UNROLL * (tk // SEL_BLOCK)
    ntiles = qi * per_q
    nfull = ntiles // NSA_UNROLL
    ngroups = jnp.int32(0)
    for j in range(nb // blocks_per_group):
        picked = imp[j * blocks_per_group:(j + 1) * blocks_per_group, :] == -jnp.inf
        hit = jnp.max(jnp.where(picked, 1.0, 0.0)) > 0.5
        glist_ref[ngroups] = j
        ngroups = ngroups + jnp.where(hit & (j < nfull), 1, 0)

    def sel_bias(kt):
        per = tk // SEL_BLOCK
        return jnp.concatenate([jnp.broadcast_to(selb_ref[pl.ds(kt * per + j, 1), :], (SEL_BLOCK, w))
                                for j in range(per)], axis=0)

    sel_logits = lambda kt: jnp.dot(key_tile(ksel_ref, kt), qt, preferred_element_type=f32) + sel_bias(kt)
    sel_values = lambda kt: val_tile(vselt_ref, kt)
    last_group = ksel_ref.shape[0] // (tk * NSA_UNROLL) - 1
    carry = _tile_loop(ntiles, sel_logits, sel_values, _softmax_init(w), (buf0_ref, buf1_ref), tk, last_group,
                       groups=(ngroups, lambda i: glist_ref[i]), unroll=NSA_UNROLL)
    diag = [(jnp.where(krow + d * tk <= qloc, sel_logits(ntiles + d), NEG_INF), sel_values(ntiles + d))
            for d in range(per_q)]
    m_s, l_s, acc_s = _softmax_step(diag, carry)
    o_s = acc_s / l_s

    for r in range(NSA_GROUP):
        gate = lambda br: gat_ref[pl.ds((g * NSA_GROUP + r) * 3 + br, 1), :]
        sl = slice(r * tq, (r + 1) * tq)
        o_ref[r * HEAD_DIM:(r + 1) * HEAD_DIM, :] = (
            gate(0) * o_c[:, sl] + gate(1) * o_s[:, sl] + gate(2) * o_w[:, sl])


def _nsa_prompt(qat, gat, ksel, vselt, kwin, vwint, kc, vct, nseq, seq_len):
    n = qat.shape[1]
    tq, tk = NSA_TQ, NSA_TK
    nq = seq_len // tq
    nb = seq_len // CMP_BLOCK
    gw = NSA_GROUP * LANE
    return pl.pallas_call(
        _nsa_prompt_kernel,
        grid=(nseq, NSA_KV_HEADS, nq),
        in_specs=[pl.BlockSpec((gw, tq), lambda b, g, q: (g, b * nq + q)),
                  pl.BlockSpec((LANE, tq), lambda b, g, q: (0, b * nq + q)),
                  pl.BlockSpec((seq_len, LANE), lambda b, g, q: (b, g)),
                  pl.BlockSpec((HEAD_DIM, seq_len), lambda b, g, q: (g, b)),
                  pl.BlockSpec((seq_len, LANE), lambda b, g, q: (b, g)),
                  pl.BlockSpec((HEAD_DIM, seq_len), lambda b, g, q: (g, b)),
                  pl.BlockSpec((nb, LANE), lambda b, g, q: (b, g)),
                  pl.BlockSpec((HEAD_DIM, nb), lambda b, g, q: (g, b))],
        out_specs=pl.BlockSpec((NSA_GROUP * HEAD_DIM, tq), lambda b, g, q: (g, b * nq + q)),
        out_shape=jax.ShapeDtypeStruct((NSA_Q_W, n), jnp.float32),
        scratch_shapes=[pltpu.VMEM((nb, NSA_GROUP * tq), jnp.float32)]
        + [pltpu.VMEM((NSA_UNROLL * tk, NSA_GROUP * tq), jnp.float32)] * 2
        + [pltpu.SMEM((nb // (NSA_UNROLL * (tk // SEL_BLOCK)),), jnp.int32)],
        compiler_params=pltpu.CompilerParams(dimension_semantics=("arbitrary", "arbitrary", "arbitrary"),
                                             vmem_limit_bytes=VMEM_LIMIT),
        name="nsa_prompt",
    )(qat, gat, ksel, vselt, kwin, vwint, kc, vct)


FOX_TQ = 1024
FOX_TK = 128


def _fox_prompt_kernel(qt_ref, kb_ref, vbt_ref, o_ref, buf0_ref, buf1_ref):
    f32 = jnp.float32
    tq, tk = FOX_TQ, FOX_TK
    qi = pl.program_id(2)
    qt = qt_ref[...]

    logits = lambda kt: jnp.dot(kb_ref[pl.ds(pl.multiple_of(kt * tk, tk), tk), :], qt, preferred_element_type=f32)
    values = lambda kt: vbt_ref[:, pl.ds(pl.multiple_of(kt * tk, tk), tk)]

    ndiag = tq // tk
    last_group = kb_ref.shape[0] // (tk * TILE_UNROLL) - 1
    carry = _tile_loop(qi * ndiag, logits, values, _softmax_init(tq), (buf0_ref, buf1_ref), tk, last_group)
    krow = lax.broadcasted_iota(jnp.int32, (tk, tq), 0)
    qloc = lax.broadcasted_iota(jnp.int32, (tk, tq), 1)
    diag = []
    for j in range(ndiag):
        kt = qi * ndiag + j
        diag.append((jnp.where(krow + j * tk <= qloc, logits(kt), NEG_INF), values(kt)))
    m, l, acc = _softmax_step(diag, carry)
    o_ref[...] = acc / l


def _fox_prompt(qbt, kb, vbt, nseq, seq_len):
    n = qbt.shape[1]
    nq = seq_len // FOX_TQ
    return pl.pallas_call(
        _fox_prompt_kernel,
        grid=(nseq, FOX_HEADS, nq),
        in_specs=[pl.BlockSpec((LANE, FOX_TQ), lambda b, h, q: (h, b * nq + q)),
                  pl.BlockSpec((seq_len, LANE), lambda b, h, q: (b, h)),
                  pl.BlockSpec((HEAD_DIM, seq_len), lambda b, h, q: (h, b))],
        out_specs=pl.BlockSpec((HEAD_DIM, FOX_TQ), lambda b, h, q: (h, b * nq + q)),
        out_shape=jax.ShapeDtypeStruct((FOX_W, n), jnp.float32),
        scratch_shapes=[pltpu.VMEM((TILE_UNROLL * FOX_TK, FOX_TQ), jnp.float32)] * 2,
        compiler_params=pltpu.CompilerParams(dimension_semantics=("arbitrary", "arbitrary", "arbitrary"),
                                             vmem_limit_bytes=VMEM_LIMIT),
        name="fox_prompt",
    )(qbt, kb, vbt)


DECODE_PER_STEP = 2


def _slope_rows(shape, rows_per_head):
    hd = lax.broadcasted_iota(jnp.int32, shape, 0) // rows_per_head
    return pltpu.bitcast((126 - hd) << 23, jnp.float32)


def _pad_rows(x, rows):
    return jnp.concatenate([x, jnp.zeros((rows - x.shape[0], x.shape[1]), x.dtype)], axis=0)


def _nt_dot(a, b):
    return lax.dot_general(a, b, (((1,), (1,)), ((), ())), preferred_element_type=jnp.float32)


def _joint_softmax(parts):
    m = functools.reduce(jnp.maximum, [jnp.max(p, axis=1, keepdims=True) for p in parts])
    es = [jnp.exp(p - m) for p in parts]
    return es, sum(jnp.sum(e, axis=1, keepdims=True) for e in es)


def _nsa_sample_kernel(npages, ns, nel, pt_ref, *refs):
    f32, bf16 = jnp.float32, jnp.bfloat16
    pages = refs[:nel * npages]
    rest = refs[nel * npages:]
    pe_ref, w1_ref, w2_ref, gk_ref, bd_ref = rest[6:11]
    bufk_ref, bufv_ref = rest[12:14]
    page = pages[0].shape[-1]
    ncol = NSA_KV_HEADS * npages

    for p in range(nel * npages):
        bufk_ref[p * LANE:(p + 1) * LANE, :] = pages[p][0, 0].reshape(LANE, page)
        bufv_ref[p * LANE:(p + 1) * LANE, :] = pages[p][0, 1].reshape(LANE, page)
    kc, vc = _summarize(bufk_ref, bufv_ref, pe_ref, w1_ref, w2_ref, nel * ncol)
    kc = (kc * lax.rsqrt(_group_mean_sq(kc, bd_ref[...]) + EPS) * gk_ref[...]).astype(bf16)
    vc = vc.astype(bf16)

    running = [_nsa_sample_one(e, npages, ns, pages[e * npages:(e + 1) * npages], rest,
                               kc[e * ncol:(e + 1) * ncol], vc[e * ncol:(e + 1) * ncol]) for e in range(nel)]
    while running:
        running = [gen for gen in running if next(gen, "done") != "done"]


def _nsa_sample_one(e, npages, ns, pages, rest, kc, vc):
    f32, bf16 = jnp.float32, jnp.bfloat16
    e_ref, qa_ref, ga_ref, rnew_ref, wnew_ref, state_ref = rest[:6]
    o_ref = rest[11]
    rows = slice(e * ns, (e + 1) * ns)
    page = pages[0].shape[-1]
    past = npages * page
    ncol = NSA_KV_HEADS * npages
    nrow = NSA_HEADS * ns

    lane8 = _lane(ns)
    ql = qa_ref[rows, :]
    qrows = []
    for hd in range(NSA_HEADS):
        g = hd // NSA_GROUP
        t = ql[:, (hd // 2) * LANE:(hd // 2 + 1) * LANE]
        if hd % 2 != g:
            t = pltpu.roll(t, HEAD_DIM, axis=1)
        qrows.append(jnp.where(lane8 // HEAD_DIM == g, t, 0.0))
    qb = (jnp.concatenate(qrows, axis=0) * SCALE).astype(bf16)

    def geom(width):
        tok = lax.broadcasted_iota(jnp.int32, (nrow, width), 0) % ns
        col = lax.broadcasted_iota(jnp.int32, (nrow, width), 1)
        return tok, col, _slope_rows((nrow, width), ns)

    yield
    q_both =qb + pltpu.roll(qb.astype(f32), HEAD_DIM, axis=1).astype(bf16)
    lane64 = lax.broadcasted_iota(jnp.int32, (nrow, LANE), 1) // HEAD_DIM
    tok, col, slope = geom(ncol)
    own = (lax.broadcasted_iota(jnp.int32, (nrow, ncol), 0) // (NSA_GROUP * ns)) == col % NSA_KV_HEADS
    lcs, blks = [], []
    for half in range(2):
        blk = (col // NSA_KV_HEADS) * 2 + half
        lc = _nt_dot(jnp.where(lane64 == half, q_both, 0.0), kc)
        lc = lc - slope * (past + tok - (blk * CMP_BLOCK + CMP_BLOCK - 1)).astype(f32)
        lcs.append(jnp.where(own, lc, NEG_INF))
        blks.append(blk)
    ecs, lsum = _joint_softmax(lcs)
    pcs = [e / lsum for e in ecs]
    res = [jnp.dot(pc.astype(bf16), vc, preferred_element_type=f32) for pc in pcs]
    o_c = jnp.where(lane64 == 0, res[0], res[1])
    o_c = o_c + pltpu.roll(o_c, HEAD_DIM, axis=1)

    yield
    nsel = NSA_KV_HEADS * ns
    col_s = lax.broadcasted_iota(jnp.int32, (nsel, ncol), 1)
    own_s = (lax.broadcasted_iota(jnp.int32, (nsel, ncol), 0) // ns) == col_s % NSA_KV_HEADS
    blks_s = [(col_s // NSA_KV_HEADS) * 2 + half for half in range(2)]
    imps = []
    for half in range(2):
        imp = jnp.concatenate(
            [sum(pcs[half][(g * NSA_GROUP + r) * ns:(g * NSA_GROUP + r + 1) * ns] for r in range(NSA_GROUP))
             for g in range(NSA_KV_HEADS)], axis=0)
        imp = jnp.where(blks_s[half] == 0, FORCE_SCORE, imp)
        imps.append(jnp.where(own_s, imp, -1.0))
    ranks = [jnp.zeros((nsel, ncol), jnp.int32) for _ in range(2)]
    for h2 in range(2):
        for c in range(ncol):
            other = jnp.broadcast_to(imps[h2][:, c:c + 1], (nsel, ncol))
            blk_c = (c // NSA_KV_HEADS) * 2 + h2
            for half in range(2):
                ahead = (other > imps[half]) | ((other == imps[half]) & (blks_s[half] > blk_c))
                ranks[half] = ranks[half] + jnp.where(ahead, 1, 0)
    selexp = 0.0
    for half in range(2):
        sel = jnp.where((ranks[half] < SEL_TOPK - 1) & own_s, 1.0, 0.0)
        sel = jnp.concatenate([sel[g * ns:(g + 1) * ns] for g in range(NSA_KV_HEADS) for _ in range(NSA_GROUP)],
                              axis=0)
        selexp = selexp + jnp.dot(sel.astype(bf16), e_ref[half], preferred_element_type=f32)

    def new_tile(k_new):
        tok, col, slope = geom(LANE)
        s = _nt_dot(qb, _pad_rows(k_new, LANE).astype(bf16))
        return jnp.where(col <= tok, s - slope * (tok - col).astype(f32), NEG_INF)

    def weighted(es, vts, e_new, v_new, lsum):
        acc = sum(_nt_dot(e.astype(bf16), vt.astype(bf16)) for e, vt in zip(es, vts))
        acc = acc + jnp.dot(e_new.astype(bf16), _pad_rows(v_new, LANE).astype(bf16), preferred_element_type=f32)
        return acc / lsum

    yield
    tok, col, slope = geom(past)
    ls = jnp.concatenate([jnp.dot(qb, pages[p][0, 2].reshape(LANE, page).astype(bf16), preferred_element_type=f32)
                          for p in range(npages)], axis=1)
    ls = jnp.where(selexp > 0.5, ls - slope * (past + tok - col).astype(f32), NEG_INF)
    yield
    (es, en), lsum = _joint_softmax([ls, new_tile(rnew_ref[rows, 2 * LANE:3 * LANE])])
    yield
    o_s = weighted([es[:, p * page:(p + 1) * page] for p in range(npages)],
                   [pages[p][0, 3].reshape(LANE, page) for p in range(npages)],
                   en, rnew_ref[rows, 3 * LANE:4 * LANE], lsum)

    yield
    wbuf = state_ref.shape[-1]
    tok, col, slope = geom(wbuf)
    lw = jnp.dot(qb, state_ref[e, 0].reshape(LANE, wbuf).astype(bf16), preferred_element_type=f32)
    lw = jnp.where(col > tok + (wbuf - WINDOW), lw - slope * (wbuf + tok - col).astype(f32), NEG_INF)
    (ew, en), lsum = _joint_softmax([lw, new_tile(wnew_ref[rows, 0:LANE])])
    yield
    o_w = weighted([ew], [state_ref[e, 1].reshape(LANE, wbuf)], en, wnew_ref[rows, LANE:2 * LANE], lsum)

    yield
    ga = ga_ref[rows, :]
    gate = lambda br: jnp.concatenate(
        [jnp.broadcast_to(ga[:, hd * 3 + br:hd * 3 + br + 1], (ns, LANE)) for hd in range(NSA_HEADS)], axis=0)
    o = gate(0) * o_c + gate(1) * o_s + gate(2) * o_w
    for j in range(NSA_HEADS // 2):
        g = (2 * j) // NSA_GROUP
        a = o[2 * j * ns:(2 * j + 1) * ns]
        b = o[(2 * j + 1) * ns:(2 * j + 2) * ns]
        if g == 0:
            b = pltpu.roll(b, HEAD_DIM, axis=1)
        else:
            a = pltpu.roll(a, HEAD_DIM, axis=1)
        o_ref[rows, j * LANE:(j + 1) * LANE] = jnp.where(lane8 < HEAD_DIM, a, b)


def _compress_weights_t(cmp_pe, cmp_w1, cmp_w2):
    pe, w1, w2 = _compress_weights(cmp_pe.transpose(0, 2, 1), cmp_w1.transpose(0, 2, 1, 3), cmp_w2)
    return pe, w1, w2


def _nsa_sample(page_table, cache_t, qa, ga, rows_new, win_new, state_t, pe, w1, w2, gk, bd):
    db, npages = page_table.shape
    page = cache_t.shape[-1]
    assert page == 2 * CMP_BLOCK == LANE
    ns = qa.shape[0] // db
    past = npages * page
    ncol = NSA_KV_HEADS * npages
    half = lax.broadcasted_iota(jnp.int32, (2, ncol, past), 0)
    col = lax.broadcasted_iota(jnp.int32, (2, ncol, past), 1)
    key = lax.broadcasted_iota(jnp.int32, (2, ncol, past), 2)
    expand = jnp.where(key // SEL_BLOCK == (col // NSA_KV_HEADS) * 2 + half, 1.0, 0.0).astype(jnp.bfloat16)
    gk2 = jnp.concatenate([gk, gk]).reshape(1, LANE)
    full = lambda a: pl.BlockSpec(a.shape, lambda b, pt: (0,) * a.ndim)
    nel = DECODE_PER_STEP
    tok = lambda w: pl.BlockSpec((nel * ns, w), lambda b, pt: (b, 0))
    page_specs = [pl.BlockSpec((1,) + cache_t.shape[1:],
                               functools.partial(lambda e, p, b, pt: (pt[b * nel + e, p], 0, 0, 0, 0), e, p))
                  for e in range(nel) for p in range(npages)]
    return pl.pallas_call(
        functools.partial(_nsa_sample_kernel, npages, ns, nel),
        grid_spec=pltpu.PrefetchScalarGridSpec(
            num_scalar_prefetch=1, grid=(db // nel,),
            in_specs=page_specs + [full(expand), tok(NSA_Q_W), tok(LANE), tok(512), tok(256),
                                   pl.BlockSpec((nel,) + state_t.shape[1:], lambda b, pt: (b, 0, 0, 0, 0)),
                                   full(pe), full(w1), full(w2), full(gk2), full(bd)],
            out_specs=tok(NSA_Q_W),
            scratch_shapes=[pltpu.VMEM((nel * npages * LANE, page), jnp.float32),
                            pltpu.VMEM((nel * npages * LANE, page), jnp.float32)]),
        out_shape=jax.ShapeDtypeStruct((db * ns, NSA_Q_W), jnp.float32),
        compiler_params=pltpu.CompilerParams(dimension_semantics=("arbitrary",),
                                             vmem_limit_bytes=VMEM_LIMIT),
        name="nsa_sample",
    )(page_table, *([cache_t] * (nel * npages)), expand, qa, ga, rows_new, win_new, state_t, pe, w1, w2, gk2, bd)


def _fox_decode_kernel(npages, ns, nel, pt_ref, *refs):
    f32, bf16 = jnp.float32, jnp.bfloat16
    qb_ref, knew_ref, lfnew_ref, o_ref = refs[2 * nel * npages:]
    page = refs[0].shape[-1]
    nh = FOX_HEADS
    nrow = nh * ns
    els = range(nel)
    kvt = [refs[e * npages:(e + 1) * npages] for e in els]
    lft = [refs[(nel + e) * npages:(nel + e + 1) * npages] for e in els]
    rows = [slice(e * ns, (e + 1) * ns) for e in els]

    r_io = lax.broadcasted_iota(jnp.int32, (page, page), 0)
    c_io = lax.broadcasted_iota(jnp.int32, (page, page), 1)
    triu = jnp.where(r_io <= c_io, 1.0, 0.0).astype(bf16)
    carry = [jnp.zeros((nh, 1), f32) for _ in els]
    negc = [[] for _ in els]
    for t in range(npages + 1):
        for e in els:
            lf = lft[e][t][0] if t < npages else lfnew_ref[e]
            ct = carry[e] + sum(jnp.dot(pc_, triu, preferred_element_type=f32) for pc_ in _split3(lf))
            carry[e] = ct[:, page - 1:page]
            negc[e].append(jnp.concatenate([jnp.broadcast_to(-ct[hd:hd + 1], (ns, page)) for hd in range(nh)],
                                           axis=0))

    head_of_lane = lax.broadcasted_iota(jnp.int32, (ns, FOX_W), 1) // HEAD_DIM
    q_bd = []
    for e in els:
        q = qb_ref[rows[e], :] * SCALE
        q_bd.append(jnp.concatenate([jnp.where(head_of_lane == hd, q, 0.0) for hd in range(nh)],
                                    axis=0).astype(bf16))

    parts = [[] for _ in els]
    for p in range(npages):
        for e in els:
            parts[e].append(jnp.dot(q_bd[e], kvt[e][p][0, 0].reshape(FOX_W, page).astype(bf16),
                                    preferred_element_type=f32) + negc[e][p])
    tok = lax.broadcasted_iota(jnp.int32, (nrow, page), 0) % ns
    col = lax.broadcasted_iota(jnp.int32, (nrow, page), 1)
    for e in els:
        s_new = _nt_dot(q_bd[e], _pad_rows(knew_ref[rows[e], 0:FOX_W], page).astype(bf16)) + negc[e][npages]
        parts[e].append(jnp.where(col <= tok, s_new, NEG_INF))
    soft = [_joint_softmax(parts[e]) for e in els]
    accs = [jnp.dot(soft[e][0][npages].astype(bf16), _pad_rows(knew_ref[rows[e], FOX_W:2 * FOX_W], page).astype(bf16),
                    preferred_element_type=f32) for e in els]
    for p in range(npages):
        for e in els:
            accs[e] = accs[e] + _nt_dot(soft[e][0][p].astype(bf16), kvt[e][p][0, 1].reshape(FOX_W, page).astype(bf16))
    for e in els:
        acc = accs[e] / soft[e][1]
        o_ref[rows[e], :] = sum(jnp.where(head_of_lane == hd, acc[hd * ns:(hd + 1) * ns], 0.0) for hd in range(nh))


def _fox_decode(page_table, cache_kvt, lft, qb, fox_new, lft_new):
    db, npages = page_table.shape
    ns = qb.shape[0] // db
    nel = DECODE_PER_STEP
    tok = lambda w: pl.BlockSpec((nel * ns, w), lambda b, pt: (b, 0))
    pg = lambda a: [pl.BlockSpec((1,) + a.shape[1:], functools.partial(
        lambda e, p, nd, b, pt: (pt[b * nel + e, p],) + (0,) * nd, e, p, a.ndim - 1))
        for e in range(nel) for p in range(npages)]
    return pl.pallas_call(
        functools.partial(_fox_decode_kernel, npages, ns, nel),
        grid_spec=pltpu.PrefetchScalarGridSpec(
            num_scalar_prefetch=1, grid=(db // nel,),
            in_specs=pg(cache_kvt) + pg(lft) + [tok(FOX_W), tok(2 * FOX_W),
                                                pl.BlockSpec((nel,) + lft_new.shape[1:], lambda b, pt: (b, 0, 0))],
            out_specs=tok(FOX_W)),
        out_shape=jax.ShapeDtypeStruct((db * ns, FOX_W), jnp.float32),
        compiler_params=pltpu.CompilerParams(dimension_semantics=("arbitrary",),
                                             vmem_limit_bytes=VMEM_LIMIT),
        name="fox_decode",
    )(page_table, *([cache_kvt] * (nel * npages)), *([lft] * (nel * npages)), qb, fox_new, lft_new)


BIG_ID = 1 << 20


def _topk_rows(s, k, ids):
    w = ids.shape[1]
    if s.shape[1] > w:
        parts = [_topk_rows(s[:, c:c + w], k, ids) for c in range(0, s.shape[1], w)]
        return jnp.concatenate([p[0] for p in parts], axis=1), jnp.concatenate([p[1] for p in parts], axis=1)
    ids = ids.astype(jnp.float32)
    vals, idxs = [], []
    for _ in range(k):
        m = jnp.max(s, axis=0, keepdims=True)
        idx = jnp.min(jnp.where(s == m, ids, float(BIG_ID)), axis=0, keepdims=True)
        vals.append(m)
        idxs.append(idx)
        s = jnp.where(ids == idx, -jnp.inf, s)
    return jnp.concatenate(vals, axis=0), jnp.concatenate(idxs, axis=0).astype(jnp.int32)


def _pick_rows(sel, table):
    out = jnp.zeros(sel.shape, table.dtype)
    for r in range(table.shape[0]):
        out = jnp.where(sel == r, table[r:r + 1, :], out)
    return out


def _merge_route_kernel(x_ref, on_ref, of_ref, mg_ref, wun_ref, wuf_ref, wo_ref, nf_ref, wqt_ref, sk_ref,
                        x1_ref, h2_ref, i1_ref, i2_ref, g_ref):
    tm = x_ref.shape[0]
    f32, bf16 = jnp.float32, jnp.bfloat16
    tdot = lambda ot, wgt: lax.dot_general(ot.astype(bf16), wgt, (((0,), (0,)), ((), ())),
                                           preferred_element_type=f32)
    a = tdot(on_ref[...], wun_ref[...])
    b = tdot(of_ref[...], wuf_ref[...])
    mixed = mg_ref[:, 0:D_MODEL] * a + mg_ref[:, D_MODEL:2 * D_MODEL] * b
    x1 = x_ref[...] + jnp.dot(mixed.astype(bf16), wo_ref[...], preferred_element_type=f32)
    x1_ref[...] = x1
    h2 = x1 * lax.rsqrt(jnp.mean(x1 * x1, axis=-1, keepdims=True) + EPS) * nf_ref[...]
    h2b = h2.astype(bf16)
    h2_ref[...] = h2b

    nk = PEER_N_KEYS
    key_ids = lax.broadcasted_iota(jnp.int32, (nk, LANE), 0)
    io16 = lax.broadcasted_iota(jnp.int32, (PEER_TOPK, LANE), 0)
    io8 = lax.broadcasted_iota(jnp.int32, (8, LANE), 0)
    cand_ids = jnp.concatenate([io16] + [a_ * PEER_TOPK + io8 for a_ in range(1, 8)]
                               + [(io8 + 8) * PEER_TOPK], axis=0)
    for h in range(PEER_HEADS):
        sv, si = [], []
        for p in range(2):
            hp = 2 * h + p
            qt = lax.dot_general(wqt_ref[hp * PEER_DK_HALF:(hp + 1) * PEER_DK_HALF, :], h2b,
                                 (((1,), (1,)), ((), ())), preferred_element_type=f32)
            st = jnp.dot(sk_ref[hp], qt.astype(bf16), preferred_element_type=f32)
            v, i = _topk_rows(st, PEER_TOPK, key_ids)
            sv.append(v)
            si.append(i)
        s1, s2 = sv
        cand = jnp.concatenate([s1[0:1] + s2] + [s1[a_:a_ + 1] + s2[0:8] for a_ in range(1, 8)]
                               + [s1[8:16] + s2[0:1]], axis=0)
        top, fid = _topk_rows(cand, PEER_TOPK, cand_ids)
        e = jnp.exp(top - jnp.max(top, axis=0, keepdims=True))
        g = e / jnp.sum(e, axis=0, keepdims=True)
        sl = slice(h * PEER_TOPK, (h + 1) * PEER_TOPK)
        i1_ref[:, sl] = _pick_rows(fid >> 4, si[0]).T
        i2_ref[:, sl] = _pick_rows(fid & (PEER_TOPK - 1), si[1]).T
        g_ref[:, sl] = g.T


def _merge_route(x2d, o_nsa, o_fox, mg, w_up_nsa, w_up_fox, w_out, norm_ffn, wq_t, sub_keys, tm=256):
    n = x2d.shape[0]
    row = lambda w: pl.BlockSpec((tm, w), lambda i: (i, 0))
    col = lambda h: pl.BlockSpec((h, tm), lambda i: (0, i))
    full = lambda a: pl.BlockSpec(a.shape, lambda i: (0,) * a.ndim)
    args = (x2d, o_nsa, o_fox, mg, w_up_nsa, w_up_fox, w_out, norm_ffn.reshape(1, D_MODEL), wq_t, sub_keys)
    hk = PEER_HEADS * PEER_TOPK
    return pl.pallas_call(
        _merge_route_kernel,
        grid=(n // tm,),
        in_specs=[row(D_MODEL), col(NSA_Q_W), col(FOX_W), row(MERGE_W)] + [full(a) for a in args[4:]],
        out_specs=[row(D_MODEL), row(D_MODEL), row(hk), row(hk), row(hk)],
        out_shape=[jax.ShapeDtypeStruct((n, D_MODEL), jnp.float32),
                   jax.ShapeDtypeStruct((n, D_MODEL), jnp.bfloat16),
                   jax.ShapeDtypeStruct((n, hk), jnp.int32),
                   jax.ShapeDtypeStruct((n, hk), jnp.int32),
                   jax.ShapeDtypeStruct((n, hk), jnp.float32)],
        compiler_params=pltpu.CompilerParams(dimension_semantics=("arbitrary",),
                                             vmem_limit_bytes=VMEM_LIMIT),
        name="merge_route",
    )(*args)


def _peer_act_kernel(h2_ref, ut_ref, i1_ref, i2_ref, act_ref):
    c = pl.program_id(1)
    ec = ut_ref.shape[1]

    @pl.when(c == 0)
    def _():
        act_ref[...] = jnp.zeros_like(act_ref)

    h2 = h2_ref[...]
    i1 = i1_ref[...]
    i2 = i2_ref[...]
    act = act_ref[...]
    nk = PEER_N_KEYS
    for blk in range(ec // ACT_BLOCK):
        a = jnp.dot(h2, ut_ref[:, blk * ACT_BLOCK:(blk + 1) * ACT_BLOCK],
                    preferred_element_type=jnp.float32)
        for ii in range(ACT_BLOCK // nk):
            got = jnp.take_along_axis(a[:, ii * nk:(ii + 1) * nk], i2, axis=1)
            act = jnp.where(i1 == c * (ec // nk) + blk * (ACT_BLOCK // nk) + ii, got, act)
    act_ref[...] = act


ACT_BLOCK = 512


def _peer_act(h2b, u_t, i1, i2, tm=1024, ec=2048):
    n = h2b.shape[0]
    hk = i1.shape[1]
    return pl.pallas_call(
        _peer_act_kernel,
        grid=(n // tm, u_t.shape[1] // ec),
        in_specs=[pl.BlockSpec((tm, D_MODEL), lambda t, c: (t, 0)),
                  pl.BlockSpec((D_MODEL, ec), lambda t, c: (0, c)),
                  pl.BlockSpec((tm, hk), lambda t, c: (t, 0)),
                  pl.BlockSpec((tm, hk), lambda t, c: (t, 0))],
        out_specs=pl.BlockSpec((tm, hk), lambda t, c: (t, 0)),
        out_shape=jax.ShapeDtypeStruct((n, hk), jnp.float32),
        compiler_params=pltpu.CompilerParams(dimension_semantics=("arbitrary", "arbitrary"),
                                             vmem_limit_bytes=VMEM_LIMIT),
        name="peer_act",
    )(h2b, u_t, i1, i2)


def _peer_coef_kernel(act_ref, g_ref, i1_ref, i2_ref, c_ref, coef_ref):
    tm = act_ref.shape[0]
    nk = PEER_N_KEYS
    coef_ref[...] = g_ref[...] * jax.nn.gelu(act_ref[...])
    sub = lax.broadcasted_iota(jnp.int32, (nk, i1_ref.shape[1]), 0)

    def token(t):
        r1 = i1_ref[pl.ds(t, 1), :]
        r2 = i2_ref[pl.ds(t, 1), :]
        cf = coef_ref[pl.ds(t, 1), :]
        m1 = jnp.where(r1 == sub, cf, 0.0).astype(jnp.bfloat16)
        m2t = jnp.where(r2 == sub, 1.0, 0.0).astype(jnp.bfloat16)
        return lax.dot_general(m1, m2t, (((1,), (1,)), ((), ())), preferred_element_type=jnp.float32)

    def body(tg, carry):
        t0 = pl.multiple_of(tg * COEF_GROUP, COEF_GROUP)
        ct = jnp.stack([token(t0 + u) for u in range(COEF_GROUP)], axis=0)
        c_ref[:, pl.ds(t0, COEF_GROUP), :] = pltpu.einshape("tij->itj", ct).astype(c_ref.dtype)
        return carry

    lax.fori_loop(0, tm // COEF_GROUP, body, 0)


COEF_GROUP = 64


def _peer_coef(act, g, i1, i2, tm=128):
    n, hk = act.shape
    nk = PEER_N_KEYS
    row = pl.BlockSpec((tm, hk), lambda t: (t, 0))
    return pl.pallas_call(
        _peer_coef_kernel,
        grid=(n // tm,),
        in_specs=[row, row, row, row],
        out_specs=pl.BlockSpec((nk, tm, nk), lambda t: (0, t, 0)),
        out_shape=jax.ShapeDtypeStruct((nk, n, nk), jnp.bfloat16),
        scratch_shapes=[pltpu.VMEM((tm, hk), jnp.float32)],
        compiler_params=pltpu.CompilerParams(dimension_semantics=("arbitrary",),
                                             vmem_limit_bytes=VMEM_LIMIT),
        name="peer_coef",
    )(act, g, i1, i2)


def _peer_out_kernel(c_ref, v_ref, x1_ref, y_ref, acc_ref):
    k = pl.program_id(1)

    @pl.when(k == 0)
    def _():
        acc_ref[...] = x1_ref[...]

    acc = acc_ref[...]
    nk = PEER_N_KEYS
    for p in range(c_ref.shape[0] // 2):
        lhs = jnp.concatenate([c_ref[2 * p], c_ref[2 * p + 1]], axis=1)
        acc = acc + jnp.dot(lhs, v_ref[2 * p * nk:(2 * p + 2) * nk, :], preferred_element_type=jnp.float32)
    acc_ref[...] = acc

    @pl.when(k == pl.num_programs(1) - 1)
    def _():
        y_ref[...] = acc_ref[...]


def _peer_out(c3, v_b, x1, tm=1024, tk=2048):
    nk, n, _ = c3.shape
    ne = nk * nk
    return pl.pallas_call(
        _peer_out_kernel,
        grid=(n // tm, ne // tk),
        in_specs=[pl.BlockSpec((tk // nk, tm, nk), lambda t, k: (k, t, 0)),
                  pl.BlockSpec((tk, D_MODEL), lambda t, k: (k, 0)),
                  pl.BlockSpec((tm, D_MODEL), lambda t, k: (t, 0))],
        out_specs=pl.BlockSpec((tm, D_MODEL), lambda t, k: (t, 0)),
        out_shape=jax.ShapeDtypeStruct((n, D_MODEL), jnp.float32),
        scratch_shapes=[pltpu.VMEM((tm, D_MODEL), jnp.float32)],
        compiler_params=pltpu.CompilerParams(dimension_semantics=("arbitrary", "arbitrary"),
                                             vmem_limit_bytes=VMEM_LIMIT),
        name="peer_out",
    )(c3, v_b, x1)


def _peer_weights(w_up_nsa, w_up_fox, w_out, norm_ffn, peer_w_query, peer_sub_keys, peer_u, peer_v):
    bf16 = jnp.bfloat16
    return dict(w_up_nsa=w_up_nsa.astype(bf16), w_up_fox=w_up_fox.astype(bf16), w_out=w_out.astype(bf16),
                norm_ffn=norm_ffn, wq_t=peer_w_query.T.astype(bf16),
                sub_keys=peer_sub_keys.reshape(2 * PEER_HEADS, PEER_N_KEYS, PEER_DK_HALF).astype(bf16),
                u=peer_u.T.astype(bf16), v=peer_v.astype(bf16))


def _merge_peer(x2d, o_nsa, o_fox, mg, wts):
    x1, h2b, i1, i2, g = _merge_route(x2d, o_nsa, o_fox, mg, wts['w_up_nsa'], wts['w_up_fox'], wts['w_out'],
                                      wts['norm_ffn'], wts['wq_t'], wts['sub_keys'])
    act = _peer_act(h2b, wts['u'], i1, i2)
    c3 = _peer_coef(act, g, i1, i2)
    return _peer_out(c3, wts['v'], x1)


def kernel(x_prompt, x_sample, cache_nsa, cache_fox_kv, cache_fox_logf, state_nsa_win, page_table,
           norm_attn, w_in, fox_f_bias, nsa_q_norm, nsa_k_norm, fox_q_norm, fox_k_norm,
           cmp_pe, cmp_w1, cmp_w2, w_up_nsa, w_up_fox, w_out, norm_ffn,
           peer_w_query, peer_sub_keys, peer_u, peer_v):
    w_front = _front_weights(w_in)
    bd = _block_diag_mean()
    wts = _peer_weights(w_up_nsa, w_up_fox, w_out, norm_ffn, peer_w_query, peer_sub_keys, peer_u, peer_v)
    cmp_wts = _compress_weights(cmp_pe, cmp_w1, cmp_w2)

    bp, seq, _ = x_prompt.shape
    n_p = bp * seq
    (rows_t, win_t, fox_t, logf_p2d, mg_p, qat, gat, ksel, kwin, vselt, vwint, qbt, kb, vbt, rows_cmp) = _front_attn(
        x_prompt.reshape(n_p, D_MODEL), seq, norm_attn, w_front, bd, fox_f_bias,
        nsa_q_norm, nsa_k_norm, fox_q_norm, fox_k_norm)
    kc, vct = _compress(rows_cmp, seq // CMP_BLOCK, *cmp_wts, nsa_k_norm[0], bd)
    o_nsa_t = _nsa_prompt(qat, gat, ksel, vselt, kwin, vwint, kc, vct, bp, seq)
    o_fox_t = _fox_prompt(qbt, kb, vbt, bp, seq)
    y_p = _merge_peer(x_prompt.reshape(n_p, D_MODEL), o_nsa_t, o_fox_t, mg_p, wts).reshape(x_prompt.shape)
    to_rows = lambda a, *dims: a.reshape(bp, *dims, a.shape[-1]).transpose(0, len(dims) + 1, *range(1, len(dims) + 1))
    nsa_p = to_rows(rows_t, 4, NSA_KV_HEADS, HEAD_DIM)
    fox_p = to_rows(fox_t, 2, FOX_HEADS, HEAD_DIM)
    logf_p = logf_p2d.reshape(bp, seq, FOX_HEADS)
    win_p = to_rows(win_t[:, :, seq - min(WINDOW, seq):], 2, NSA_KV_HEADS, HEAD_DIM)

    db, ns, _ = x_sample.shape
    n_s = db * ns
    n_pool, page = cache_nsa.shape[:2]
    wbuf = state_nsa_win.shape[1]
    qa_s, rows_s, win_s2d, ga_s, qb_s, fox_s2d, logf_s2d, mg_s = _front(
        x_sample.reshape(n_s, D_MODEL), norm_attn, w_front, bd, fox_f_bias,
        nsa_q_norm, nsa_k_norm, fox_q_norm, fox_k_norm)
    o_nsa_s = _nsa_sample(page_table, cache_nsa.transpose(0, 2, 3, 4, 1), qa_s, ga_s, rows_s, win_s2d,
                          state_nsa_win.transpose(0, 2, 3, 4, 1),
                          *_compress_weights_t(cmp_pe, cmp_w1, cmp_w2), nsa_k_norm[0], bd)
    lft_new = jnp.pad(logf_s2d.reshape(db, ns, FOX_HEADS).transpose(0, 2, 1), ((0, 0), (0, 0), (0, page - ns)))
    o_fox_s = _fox_decode(page_table, cache_fox_kv.transpose(0, 2, 3, 4, 1), cache_fox_logf.transpose(0, 2, 1),
                          qb_s, fox_s2d, lft_new)
    y_s = _merge_peer(x_sample.reshape(n_s, D_MODEL), o_nsa_s.T, o_fox_s.T, mg_s, wts).reshape(x_sample.shape)
    nsa_s = rows_s.reshape(db, ns, 4, NSA_KV_HEADS, HEAD_DIM)
    fox_s = fox_s2d.reshape(db, ns, 2, FOX_HEADS, HEAD_DIM)
    logf_s = logf_s2d.reshape(db, ns, FOX_HEADS)
    win_s = jnp.concatenate([state_nsa_win[:, ns:], win_s2d.reshape(db, ns, 2, NSA_KV_HEADS, HEAD_DIM)], axis=1)
    return (y_p, y_s, nsa_p, fox_p, logf_p, win_p, nsa_s, fox_s, logf_s, win_s)
```
